```python
import math
import jax, jax.numpy as jnp
from jax import lax
import numpy as np

D_MODEL = 1024
BATCH = 8
SEQ = 2048
DEPTH = 2

HEAD_DIM = 64
POOL_DIM = D_MODEL // 4
POOL_WINDOWS = (2, 4, 8, 16)
POOL_GROUP = POOL_DIM // len(POOL_WINDOWS)
N_ATT_HEADS = (3 * D_MODEL // 8) // HEAD_DIM
N_KV_HEADS = 2
GQA_GROUP = N_ATT_HEADS // N_KV_HEADS
ATT_DIM = N_ATT_HEADS * HEAD_DIM
KV_DIM = N_KV_HEADS * HEAD_DIM
WINDOW = 128
N_BUCKETS = 32
MAX_DISTANCE = 128
N_GDN_HEADS = (3 * D_MODEL // 8) // HEAD_DIM
GDN_DIM = N_GDN_HEADS * HEAD_DIM
CONV_WIDTH = 4
GDN_CHUNK = 64
MIX_DIM = POOL_DIM + ATT_DIM + GDN_DIM
IN_DIM = POOL_DIM + ATT_DIM + 2 * KV_DIM + 4 * GDN_DIM + 2 * N_GDN_HEADS
N_EXPERTS = 32
TOP_K = 4
EXPERT_DIM = D_MODEL
SWIGLU_ALPHA = 1.702
SWIGLU_LIMIT = 7.0
MOE_BLOCK = 256
DEEPNORM_ALPHA = (2 * DEPTH) ** 0.25
DEEPNORM_BETA = (8 * DEPTH) ** -0.25
LN_EPS = 1e-5
NORM_EPS = 1e-6
NEG_INF = -1e30

kernel_name = "hybrid_pool_swa_gdn_moe_deepnorm"


def layer_norm(x, g, b):
    xf = x.astype(jnp.float32)
    mu = xf.mean(-1, keepdims=True)
    var = jnp.square(xf - mu).mean(-1, keepdims=True)
    y = (xf - mu) * lax.rsqrt(var + LN_EPS) * g.astype(jnp.float32) + b.astype(jnp.float32)
    return y.astype(x.dtype)


def t5_bucket(n):
    max_exact = N_BUCKETS // 2
    nf = jnp.maximum(n, 1).astype(jnp.float32)
    large = max_exact + (jnp.log(nf / max_exact) / math.log(MAX_DISTANCE / max_exact)
                         * (N_BUCKETS - max_exact)).astype(jnp.int32)
    large = jnp.minimum(large, N_BUCKETS - 1)
    return jnp.where(n < max_exact, n, large)


def pool_mixer(u, pool_w, pool_scale):
    Bb, S_, _ = u.shape
    uf = u.astype(jnp.float32)
    cnt_base = jnp.arange(1, S_ + 1, dtype=jnp.float32)
    outs = []
    for gi, w in enumerate(POOL_WINDOWS):
        ug = uf[..., gi * POOL_GROUP:(gi + 1) * POOL_GROUP]
        cs = jnp.cumsum(ug, axis=1)
        shifted = jnp.pad(cs, ((0, 0), (w, 0), (0, 0)))[:, :S_]
        cnt = jnp.minimum(cnt_base, float(w))
        outs.append((cs - shifted) / cnt[None, :, None] - ug)
    p = jnp.stack(outs, axis=2)
    y = jnp.einsum('bsgc,gcd->bsgd', p, pool_w.astype(jnp.float32)).reshape(Bb, S_, POOL_DIM)
    return (y * pool_scale.astype(jnp.float32)).astype(u.dtype)


def swa_attention(q, k, v, sinks, rel_bias):
    Bb, S_ = q.shape[:2]
    NB = S_ // WINDOW
    qf, kf, vf = (t.astype(jnp.float32) for t in (q, k, v))
    qb = qf.reshape(Bb, NB, WINDOW, N_KV_HEADS, GQA_GROUP, HEAD_DIM)

    def band(t):
        tp = jnp.pad(t, ((0, 0), (WINDOW, 0), (0, 0), (0, 0)))
        prev = tp[:, :S_].reshape(Bb, NB, WINDOW, N_KV_HEADS, HEAD_DIM)
        cur = t.reshape(Bb, NB, WINDOW, N_KV_HEADS, HEAD_DIM)
        return jnp.concatenate([prev, cur], axis=2)

    kb, vb = band(kf), band(vf)
    s = jnp.einsum('bnqhgd,bnkhd->bnhgqk', qb, kb) * (HEAD_DIM ** -0.5)
    qi = jnp.arange(WINDOW)[:, None]
    kj = jnp.arange(2 * WINDOW)[None, :]
    dist = qi + WINDOW - kj
    in_band = (dist >= 0) & (dist < WINDOW)
    key_abs = jnp.arange(NB)[:, None] * WINDOW - WINDOW + kj
    valid = in_band[None] & (key_abs >= 0)[:, None, :]
    bias = rel_bias.astype(jnp.float32)[t5_bucket(jnp.maximum(dist, 0))]
    bias = bias.transpose(2, 0, 1).reshape(N_KV_HEADS, GQA_GROUP, WINDOW, 2 * WINDOW)
    s = jnp.where(valid[None, :, None, None], s + bias[None, None], NEG_INF)
    sink = sinks.astype(jnp.float32).reshape(N_KV_HEADS, GQA_GROUP)[None, None, :, :, None, None]
    m = jnp.maximum(s.max(-1, keepdims=True), sink)
    p = jnp.exp(s - m)
    den = p.sum(-1, keepdims=True) + jnp.exp(sink - m)
    o = jnp.einsum('bnhgqk,bnkhd->bnqhgd', p / den, vb)
    return o.reshape(Bb, S_, ATT_DIM).astype(q.dtype)


def gated_delta_chunked(q, k, v, g, beta):
    Bb, S_, H, Dk = q.shape
    Dv = v.shape[-1]
    N = S_ // GDN_CHUNK
    C = GDN_CHUNK
    q = q * (Dk ** -0.5)

    def chunks(t):
        return t.reshape(Bb, N, C, H, -1).transpose(0, 3, 1, 2, 4)

    qc, kc, vc = chunks(q), chunks(k), chunks(v)
    gc = g.reshape(Bb, N, C, H).transpose(0, 3, 1, 2)
    bc = beta.reshape(Bb, N, C, H).transpose(0, 3, 1, 2)
    gcum = jnp.cumsum(gc, axis=-1)
    causal = jnp.tril(jnp.ones((C, C), dtype=bool))
    strict = jnp.tril(jnp.ones((C, C), dtype=jnp.float32), -1)
    decay = jnp.exp(jnp.where(causal, gcum[..., :, None] - gcum[..., None, :], -jnp.inf))
    kbeta = kc * bc[..., None]
    vbeta = vc * bc[..., None]
    L = jnp.einsum('bhnid,bhnjd->bhnij', kbeta, kc) * decay * strict
    a_mat = jnp.eye(C, dtype=jnp.float32) + L
    rhs = jnp.concatenate([vbeta, kbeta * jnp.exp(gcum)[..., None]], axis=-1)
    sol = lax.linalg.triangular_solve(a_mat, rhs, left_side=True, lower=True, unit_diagonal=True)
    u, w = sol[..., :Dv], sol[..., Dv:]
    attn = jnp.einsum('bhnid,bhnjd->bhnij', qc, kc) * decay
    q_dec = qc * jnp.exp(gcum)[..., None]
    k_dec = kc * jnp.exp(gcum[..., -1:] - gcum)[..., None]
    chunk_decay = jnp.exp(gcum[..., -1])

    def step(state, inp):
        u_n, w_n, qd_n, kd_n, a_n, cd_n = inp
        v_new = u_n - jnp.einsum('bhck,bhkv->bhcv', w_n, state)
        o_n = jnp.einsum('bhck,bhkv->bhcv', qd_n, state) + jnp.einsum('bhij,bhjv->bhiv', a_n, v_new)
        state = state * cd_n[..., None, None] + jnp.einsum('bhck,bhcv->bhkv', kd_n, v_new)
        return state, o_n

    xs = tuple(jnp.moveaxis(t, 2, 0) for t in (u, w, q_dec, k_dec, attn, chunk_decay))
    state0 = jnp.zeros((Bb, H, Dk, Dv), jnp.float32)
    _, o = lax.scan(step, state0, xs)
    return o.transpose(1, 0, 3, 2, 4).reshape(Bb, S_, H, Dv)


def gdn_mixer(qkv, z, b_raw, a_raw, conv_w, a_log, dt_bias, norm_w):
    Bb, S_, Cc = qkv.shape
    conv = lax.conv_general_dilated(
        qkv, conv_w.astype(qkv.dtype)[:, None, :], window_strides=(1,),
        padding=[(CONV_WIDTH - 1, 0)], dimension_numbers=('NWC', 'WIO', 'NWC'),
        feature_group_count=Cc)
    conv = jax.nn.silu(conv.astype(jnp.float32))
    q, k, v = (t.reshape(Bb, S_, N_GDN_HEADS, HEAD_DIM) for t in jnp.split(conv, 3, axis=-1))
    q = q * lax.rsqrt(jnp.sum(q * q, -1, keepdims=True) + NORM_EPS)
    k = k * lax.rsqrt(jnp.sum(k * k, -1, keepdims=True) + NORM_EPS)
    beta = jax.nn.sigmoid(b_raw.astype(jnp.float32))
    g = -jnp.exp(a_log.astype(jnp.float32)) * jax.nn.softplus(a_raw.astype(jnp.float32) + dt_bias.astype(jnp.float32))
    o = gated_delta_chunked(q, k, v, g, beta)
    o = o * lax.rsqrt(jnp.mean(o * o, -1, keepdims=True) + NORM_EPS) * norm_w.astype(jnp.float32)
    o = o * jax.nn.silu(z.astype(jnp.float32).reshape(Bb, S_, N_GDN_HEADS, HEAD_DIM))
    return o.reshape(Bb, S_, GDN_DIM).astype(qkv.dtype)


def hybrid_mixer(h, w_in, w_out, pool_w, pool_scale, sinks, rel_bias, conv_w, a_log, dt_bias, norm_w):
    Bb, S_, _ = h.shape
    proj = h @ w_in.astype(h.dtype)
    sizes = (POOL_DIM, ATT_DIM, KV_DIM, KV_DIM, 3 * GDN_DIM, GDN_DIM, N_GDN_HEADS, N_GDN_HEADS)
    cuts = tuple(int(i) for i in np.cumsum(sizes)[:-1])
    u_pool, aq, ak, av, gqkv, gz, gb, ga = jnp.split(proj, cuts, axis=-1)
    y_pool = pool_mixer(u_pool, pool_w, pool_scale)
    y_att = swa_attention(aq.reshape(Bb, S_, N_ATT_HEADS, HEAD_DIM),
                          ak.reshape(Bb, S_, N_KV_HEADS, HEAD_DIM),
                          av.reshape(Bb, S_, N_KV_HEADS, HEAD_DIM), sinks, rel_bias)
    y_gdn = gdn_mixer(gqkv, gz, gb, ga, conv_w, a_log, dt_bias, norm_w)
    y = jnp.concatenate([y_pool, y_att, y_gdn], axis=-1)
    return y @ w_out.astype(h.dtype)


def moe_ffn(h, router_w, router_b, w_up, b_up, w_down, b_down):
    Bb, S_, D = h.shape
    T = Bb * S_
    A = T * TOP_K
    xt = h.reshape(T, D)
    logits = xt.astype(jnp.float32) @ router_w.astype(jnp.float32) + router_b.astype(jnp.float32)
    top_vals, top_idx = lax.top_k(logits, TOP_K)
    gates = jax.nn.softmax(top_vals, axis=-1)
    flat_e = top_idx.reshape(A)
    order = jnp.argsort(flat_e, stable=True)
    sorted_e = flat_e[order]
    counts = jnp.bincount(flat_e, length=N_EXPERTS)
    padded = ((counts + MOE_BLOCK - 1) // MOE_BLOCK) * MOE_BLOCK
    start = jnp.cumsum(counts) - counts
    pcum = jnp.cumsum(padded)
    pstart = pcum - padded
    dest_sorted = pstart[sorted_e] + (jnp.arange(A) - start[sorted_e])
    dest = jnp.zeros((A,), jnp.int32).at[order].set(dest_sorted.astype(jnp.int32))
    n_blocks = -(-A // MOE_BLOCK) + N_EXPERTS
    P = n_blocks * MOE_BLOCK
    tok = jnp.arange(A) // TOP_K
    xbuf = jnp.zeros((P, D), h.dtype).at[dest].set(xt[tok])
    block_e = jnp.minimum(jnp.searchsorted(pcum, jnp.arange(n_blocks) * MOE_BLOCK, side='right'),
                          N_EXPERTS - 1)

    def expert_block(args):
        xb, e = args
        hb = xb @ w_up[e].astype(xb.dtype) + b_up[e].astype(xb.dtype)
        x_glu = jnp.minimum(hb[:, :EXPERT_DIM], SWIGLU_LIMIT)
        x_lin = jnp.clip(hb[:, EXPERT_DIM:], -SWIGLU_LIMIT, SWIGLU_LIMIT)
        act = x_glu * jax.nn.sigmoid(SWIGLU_ALPHA * x_glu) * (x_lin + 1)
        return act @ w_down[e].astype(xb.dtype) + b_down[e].astype(xb.dtype)

    ybuf = lax.map(expert_block, (xbuf.reshape(n_blocks, MOE_BLOCK, D), block_e)).reshape(P, D)
    y = ybuf[dest].reshape(T, TOP_K, D)
    out = jnp.einsum('tk,tkd->td', gates.astype(y.dtype), y)
    return out.reshape(Bb, S_, D)


def setup_inputs(seed: int = 0) -> dict:
    key = jax.random.key(seed)
    ks = jax.random.split(key, 26)
    f32 = jnp.float32

    def nrm(k, shape, s):
        return jax.random.normal(k, shape, f32) * s

    dt = jnp.exp(jax.random.uniform(ks[15], (DEPTH, N_GDN_HEADS), f32, math.log(1e-3), math.log(1e-1)))
    return {
        "x": nrm(ks[0], (BATCH, SEQ, D_MODEL), 1.0),
        "c": nrm(ks[1], (BATCH, D_MODEL), 1.0),
        "rel_bias": nrm(ks[2], (N_BUCKETS, N_ATT_HEADS), 0.3),
        "w_in": nrm(ks[3], (DEPTH, D_MODEL, IN_DIM), D_MODEL ** -0.5),
        "w_out": nrm(ks[4], (DEPTH, MIX_DIM, D_MODEL), MIX_DIM ** -0.5 * DEEPNORM_BETA),
        "w_ada": nrm(ks[5], (DEPTH, D_MODEL, 6 * D_MODEL), D_MODEL ** -0.5),
        "b_ada": nrm(ks[6], (DEPTH, 6 * D_MODEL), 0.02),
        "ln1_g": 1.0 + nrm(ks[7], (DEPTH, D_MODEL), 0.02),
        "ln1_b": nrm(ks[8], (DEPTH, D_MODEL), 0.02),
        "ln2_g": 1.0 + nrm(ks[9], (DEPTH, D_MODEL), 0.02),
        "ln2_b": nrm(ks[10], (DEPTH, D_MODEL), 0.02),
        "pool_w": nrm(ks[11], (DEPTH, len(POOL_WINDOWS), POOL_GROUP, POOL_GROUP), POOL_GROUP ** -0.5),
        "pool_scale": 1.0 + nrm(ks[12], (DEPTH, POOL_DIM), 0.02),
        "attn_sinks": nrm(ks[13], (DEPTH, N_ATT_HEADS), 1.0),
        "conv_w": nrm(ks[14], (DEPTH, CONV_WIDTH, 3 * GDN_DIM), CONV_WIDTH ** -0.5),
        "gdn_a_log": jnp.log(jax.random.uniform(ks[16], (DEPTH, N_GDN_HEADS), f32, 1.0, 16.0)),
        "gdn_dt_bias": dt + jnp.log(-jnp.expm1(-dt)),
        "gdn_norm_w": 1.0 + nrm(ks[17], (DEPTH, HEAD_DIM), 0.02),
        "router_w": nrm(ks[18], (DEPTH, D_MODEL, N_EXPERTS), D_MODEL ** -0.5),
        "router_b": nrm(ks[19], (DEPTH, N_EXPERTS), 0.01),
        "exp_w_up": nrm(ks[20], (DEPTH, N_EXPERTS, D_MODEL, 2 * EXPERT_DIM), D_MODEL ** -0.5),
        "exp_b_up": nrm(ks[21], (DEPTH, N_EXPERTS, 2 * EXPERT_DIM), 0.01),
        "exp_w_down": nrm(ks[22], (DEPTH, N_EXPERTS, EXPERT_DIM, D_MODEL), EXPERT_DIM ** -0.5 * DEEPNORM_BETA),
        "exp_b_down": nrm(ks[23], (DEPTH, N_EXPERTS, D_MODEL), 0.01),
    }


def reference(x, c, rel_bias, w_in, w_out, w_ada, b_ada, ln1_g, ln1_b, ln2_g, ln2_b,
              pool_w, pool_scale, attn_sinks, conv_w, gdn_a_log, gdn_dt_bias, gdn_norm_w,
              router_w, router_b, exp_w_up, exp_b_up, exp_w_down, exp_b_down):
    c_act = jax.nn.silu(c)
    for l in range(DEPTH):
        mod = (c_act @ w_ada[l] + b_ada[l]).astype(x.dtype)
        sh1, sc1, g1, sh2, sc2, g2 = (m[:, None, :] for m in jnp.split(mod, 6, axis=-1))
        h = x * (1 + sc1) + sh1
        y = hybrid_mixer(h, w_in[l], w_out[l], pool_w[l], pool_scale[l], attn_sinks[l], rel_bias,
                         conv_w[l], gdn_a_log[l], gdn_dt_bias[l], gdn_norm_w[l])
        x = layer_norm(DEEPNORM_ALPHA * x + g1 * y, ln1_g[l], ln1_b[l])
        h = x * (1 + sc2) + sh2
        y = moe_ffn(h, router_w[l], router_b[l], exp_w_up[l], exp_b_up[l], exp_w_down[l], exp_b_down[l])
        x = layer_norm(DEEPNORM_ALPHA * x + g2 * y, ln2_g[l], ln2_b[l])
    return x
```

```python
import functools

import numpy as np
import jax
import jax.numpy as jnp
from jax import lax
from jax.experimental import pallas as pl
from jax.experimental.pallas import tpu as pltpu

F32 = jnp.float32
BF16 = jnp.bfloat16

D_MODEL = 1024
HEAD_DIM = 64
POOL_DIM = 256
POOL_WINDOWS = (2, 4, 8, 16)
POOL_GROUP = 64
N_ATT_HEADS = 6
N_KV_HEADS = 2
ATT_DIM = 384
KV_DIM = 128
WINDOW = 128
N_BUCKETS = 32
MAX_DISTANCE = 128
N_GDN_HEADS = 6
GDN_DIM = 384
CONV_WIDTH = 4
GDN_CHUNK = 64
N_EXPERTS = 32
TOP_K = 4
EXPERT_DIM = 1024
SWIGLU_ALPHA = 1.702
SWIGLU_LIMIT = 7.0
DEPTH = 2
DEEPNORM_ALPHA = (2 * DEPTH) ** 0.25
LN_EPS = 1e-5
NORM_EPS = 1e-6
NEG_INF = -1e30

LANES = 128
SUBLANES = 8
VMEM_LIMIT = 56 * 1024 * 1024

ROW_TILE = 512
GDN_SUPER = 256
ROUTE_TILE = 256
EXPERT_BLOCK = 256

_OFF_AQ = POOL_DIM
_OFF_AK = _OFF_AQ + ATT_DIM
_OFF_AV = _OFF_AK + KV_DIM
_OFF_GQ = _OFF_AV + KV_DIM
_OFF_GK = _OFF_GQ + GDN_DIM
_OFF_GV = _OFF_GK + GDN_DIM
_OFF_GZ = _OFF_GV + GDN_DIM
_OFF_GB = _OFF_GZ + GDN_DIM
_OFF_GA = _OFF_GB + N_GDN_HEADS
IN_DIM = _OFF_GA + N_GDN_HEADS

P_POOL = (0, POOL_DIM)
P_Q = (P_POOL[1], P_POOL[1] + ATT_DIM)
P_KV = (P_Q[1], P_Q[1] + 2 * KV_DIM)
P_GDN = (P_KV[1], P_KV[1] + 4 * GDN_DIM)
P_BA = (P_GDN[1], P_GDN[1] + LANES)
P_TOTAL = P_BA[1]


def _head_cols(off, h):
    return list(range(off + HEAD_DIM * h, off + HEAD_DIM * (h + 1)))


def _build_in_perm():
    cols = list(range(POOL_DIM))
    for p in range(N_ATT_HEADS // 2):
        cols += _head_cols(_OFF_AQ, p) + _head_cols(_OFF_AQ, p + 3)
    cols += list(range(_OFF_AK, _OFF_AK + 2 * KV_DIM))
    gdn_src = []
    for p in range(N_GDN_HEADS // 2):
        e, o = 2 * p, 2 * p + 1
        grp = (_head_cols(_OFF_GK, e) + _head_cols(_OFF_GQ, e)
               + _head_cols(_OFF_GQ, o) + _head_cols(_OFF_GK, o)
               + _head_cols(_OFF_GV, o) + _head_cols(_OFF_GV, e)
               + _head_cols(_OFF_GZ, o) + _head_cols(_OFF_GZ, e))
        cols += grp
        gdn_src += [c - _OFF_GQ if c < _OFF_GZ else -1 for c in grp]
    cols += list(range(_OFF_GB, _OFF_GB + 2 * N_GDN_HEADS))
    cols += [-1] * (LANES - 2 * N_GDN_HEADS)
    assert len(cols) == P_TOTAL
    return np.asarray(cols, np.int32), np.asarray(gdn_src, np.int32)


_IN_PERM, _GDN_CONV_SRC = _build_in_perm()


def _build_out_perm():
    rows = list(range(POOL_DIM))
    for p in range(N_ATT_HEADS // 2):
        rows += _head_cols(POOL_DIM, p) + _head_cols(POOL_DIM, p + 3)
    for p in range(N_GDN_HEADS // 2):
        rows += _head_cols(POOL_DIM + ATT_DIM, 2 * p + 1) + _head_cols(POOL_DIM + ATT_DIM, 2 * p)
    return np.asarray(rows, np.int32)


_OUT_PERM = _build_out_perm()


def _t5_bucket_table():
    qi = np.arange(WINDOW)[:, None]
    kj = np.arange(2 * WINDOW)[None, :]
    n = np.maximum(qi + WINDOW - kj, 0)
    max_exact = N_BUCKETS // 2
    nf = np.maximum(n, 1).astype(np.float32)
    large = max_exact + (np.log(nf / max_exact) / np.float32(np.log(MAX_DISTANCE / max_exact))
                         * (N_BUCKETS - max_exact)).astype(np.int32)
    large = np.minimum(large, N_BUCKETS - 1)
    return np.where(n < max_exact, n, large).astype(np.int32)


_BUCKETS = _t5_bucket_table()


def _take_cols(w, perm):
    safe = np.where(perm < 0, 0, perm)
    out = jnp.take(w, jnp.asarray(safe), axis=-1)
    if (perm < 0).any():
        out = jnp.where(jnp.asarray(perm >= 0), out, jnp.zeros((), w.dtype))
    return out


def _split_bf16(x):
    hi = x.astype(BF16)
    lo = (x - hi.astype(F32)).astype(BF16)
    return hi, lo


def _dot(a, b):
    return jnp.dot(a, b, preferred_element_type=F32)


def _dot_nt(a, b):
    return lax.dot_general(a, b, (((1,), (1,)), ((), ())), preferred_element_type=F32)


def _dot3(a, b):
    ah, al = _split_bf16(a)
    bh, bl = _split_bf16(b)
    return _dot(ah, bh) + _dot(al, bh) + _dot(ah, bl)


def _dot_hi_exact_rhs(x, m_bf16):
    hi, lo = _split_bf16(x)
    return _dot(hi, m_bf16) + _dot(lo, m_bf16)


def _sigmoid(x):
    return 1.0 / (1.0 + jnp.exp(-x))


def _layer_norm(r, g, b):
    mu = jnp.mean(r, axis=-1, keepdims=True)
    d = r - mu
    var = jnp.mean(d * d, axis=-1, keepdims=True)
    return d * lax.rsqrt(var + LN_EPS) * g + b


def _cparams(sem):
    return pltpu.CompilerParams(dimension_semantics=sem, vmem_limit_bytes=VMEM_LIMIT)


def _mod_kernel(c_ref, w_ref, b_ref, o_ref):
    c = c_ref[...]
    ca = c * _sigmoid(c)
    ch, cl = _split_bf16(ca)
    wh, wl = _split_bf16(w_ref[0])
    o_ref[0] = _dot(ch, wh) + _dot(cl, wh) + _dot(ch, wl) + b_ref[0]


def _modulation(c, w_ada, b_ada):
    depth, d, n = w_ada.shape
    bsz = c.shape[0]
    tn = 512
    return pl.pallas_call(
        _mod_kernel,
        grid=(depth, n // tn),
        in_specs=[
            pl.BlockSpec((bsz, d), lambda l, j: (0, 0)),
            pl.BlockSpec((1, d, tn), lambda l, j: (l, 0, j)),
            pl.BlockSpec((1, 1, tn), lambda l, j: (l, 0, j)),
        ],
        out_specs=pl.BlockSpec((1, bsz, tn), lambda l, j: (l, 0, j)),
        out_shape=jax.ShapeDtypeStruct((depth, bsz, n), F32),
        compiler_params=_cparams(("arbitrary", "arbitrary")),
        name="adaln_mod",
    )(c, w_ada, b_ada.reshape(depth, 1, n))


def _inproj_kernel(x_ref, mod_ref, w_ref, pool_ref, q_ref, kv_ref, gdn_ref, ba_ref):
    sh = mod_ref[0, 0:1, :]
    sc = mod_ref[0, 1:2, :]
    h = (x_ref[...] * (1.0 + sc) + sh).astype(BF16)

    def mm(rng):
        return _dot(h, w_ref[:, rng[0]:rng[1]])

    pool_ref[...] = mm(P_POOL)
    q_ref[...] = mm(P_Q).astype(BF16)
    kv_ref[...] = mm(P_KV).astype(BF16)
    gdn_ref[...] = mm(P_GDN)
    ba_ref[...] = mm(P_BA)


def _in_projection(x2d, mod, w_in_p, seq):
    t, d = x2d.shape
    tm = ROW_TILE
    widths = [r[1] - r[0] for r in (P_POOL, P_Q, P_KV, P_GDN, P_BA)]
    dtypes = [F32, BF16, BF16, F32, F32]
    return pl.pallas_call(
        _inproj_kernel,
        grid=(t // tm,),
        in_specs=[
            pl.BlockSpec((tm, d), lambda i: (i, 0)),
            pl.BlockSpec((1, 6, d), lambda i: ((i * tm) // seq, 0, 0)),
            pl.BlockSpec((d, P_TOTAL), lambda i: (0, 0)),
        ],
        out_specs=[pl.BlockSpec((tm, w), lambda i: (i, 0)) for w in widths],
        out_shape=[jax.ShapeDtypeStruct((t, w), dt) for w, dt in zip(widths, dtypes)],
        compiler_params=_cparams(("arbitrary",)),
        name="in_proj",
    )(x2d, mod, w_in_p)


def _pool_kernel(u_ref, w_ref, scale_ref, o_ref):
    u = u_ref[...]
    row = lax.broadcasted_iota(jnp.int32, u.shape, 0)
    lane = lax.broadcasted_iota(jnp.int32, u.shape, 1)

    def shifted(a, s):
        return jnp.where(row >= s, pltpu.roll(a, s, axis=0), 0.0)

    sums = []
    acc = u
    for wdt in POOL_WINDOWS:
        acc = acc + shifted(acc, wdt // 2)
        sums.append(acc)
    grp = lane // POOL_GROUP
    wsum = sums[-1]
    win = jnp.full(u.shape, POOL_WINDOWS[-1], jnp.int32)
    for gi in range(len(POOL_WINDOWS) - 2, -1, -1):
        wsum = jnp.where(grp == gi, sums[gi], wsum)
        win = jnp.where(grp == gi, POOL_WINDOWS[gi], win)
    cnt = jnp.minimum(row + 1, win).astype(F32)
    p = wsum / cnt - u
    y = _dot(p.astype(BF16), w_ref[...]) * scale_ref[...]
    o_ref[...] = y.astype(BF16)


def _pool_mixer(u, pool_w_bd, pool_scale, seq):
    t, c = u.shape
    return pl.pallas_call(
        _pool_kernel,
        grid=(t // seq,),
        in_specs=[
            pl.BlockSpec((seq, c), lambda b: (b, 0)),
            pl.BlockSpec((c, c), lambda b: (0, 0)),
            pl.BlockSpec((1, c), lambda b: (0, 0)),
        ],
        out_specs=pl.BlockSpec((seq, c), lambda b: (b, 0)),
        out_shape=jax.ShapeDtypeStruct((t, c), BF16),
        compiler_params=_cparams(("arbitrary",)),
        name="pool_mixer",
    )(u, pool_w_bd, pool_scale.reshape(1, c))


def _attn_kernel(sink_ref, q_ref, kvc_ref, kvp_ref, bias_ref, o_ref):
    nb = pl.program_id(1)
    kv = jnp.concatenate([kvp_ref[...], kvc_ref[...]], axis=0)
    k = kv[:, :KV_DIM]
    v = kv[:, KV_DIM:]
    qi = lax.broadcasted_iota(jnp.int32, (WINDOW, 2 * WINDOW), 0)
    kj = lax.broadcasted_iota(jnp.int32, (WINDOW, 2 * WINDOW), 1)
    dist = qi + WINDOW - kj
    valid = (dist >= 0) & (dist < WINDOW) & ((kj >= WINDOW) | (nb > 0))
    lo = lax.broadcasted_iota(jnp.int32, (WINDOW, LANES), 1) < HEAD_DIM
    for p in range(N_ATT_HEADS // 2):
        qp = q_ref[:, p * LANES:(p + 1) * LANES]
        halves = []
        for half in range(2):
            h = p + 3 * half
            qm = jnp.where(lo if half == 0 else jnp.logical_not(lo), qp, jnp.zeros_like(qp))
            s = _dot_nt(qm, k) * (HEAD_DIM ** -0.5)
            s = jnp.where(valid, s + bias_ref[h], NEG_INF)
            sink = sink_ref[h]
            m = jnp.maximum(jnp.max(s, axis=-1, keepdims=True), sink)
            pr = jnp.exp(s - m)
            den = jnp.sum(pr, axis=-1, keepdims=True) + jnp.exp(sink - m)
            halves.append(_dot(pr.astype(BF16), v) / den)
        o_ref[:, p * LANES:(p + 1) * LANES] = jnp.where(lo, halves[0], halves[1]).astype(BF16)


def _swa_attention(q, kv, bias, sinks, seq):
    t = q.shape[0]
    nblk = seq // WINDOW
    return pl.pallas_call(
        _attn_kernel,
        grid=(t // seq, nblk),
        in_specs=[
            pl.BlockSpec(memory_space=pltpu.SMEM),
            pl.BlockSpec((WINDOW, ATT_DIM), lambda b, n: (b * nblk + n, 0)),
            pl.BlockSpec((WINDOW, 2 * KV_DIM), lambda b, n: (b * nblk + n, 0)),
            pl.BlockSpec((WINDOW, 2 * KV_DIM), lambda b, n: (b * nblk + jnp.maximum(n - 1, 0), 0)),
            pl.BlockSpec((N_ATT_HEADS, WINDOW, 2 * WINDOW), lambda b, n: (0, 0, 0)),
        ],
        out_specs=pl.BlockSpec((WINDOW, ATT_DIM), lambda b, n: (b * nblk + n, 0)),
        out_shape=jax.ShapeDtypeStruct((t, ATT_DIM), BF16),
        compiler_params=_cparams(("arbitrary", "arbitrary")),
        name="swa_attention",
    )(sinks, q, kv, kv, bias)


def _gdn_kernel(x_ref, halo_ref, ba_ref, cw_ref, alog_ref, dtb_ref, nw_ref, y_ref, state_ref):
    sc_id = pl.program_id(1)
    rows = GDN_SUPER
    nchunk = rows // GDN_CHUNK
    c_sz = GDN_CHUNK

    @pl.when(sc_id == 0)
    def _():
        state_ref[...] = jnp.zeros_like(state_ref)

    x = x_ref[...]
    halo = jnp.where(sc_id == 0, 0.0, halo_ref[...])
    row8 = lax.broadcasted_iota(jnp.int32, halo.shape, 0)
    acc = x * cw_ref[CONV_WIDTH - 1:CONV_WIDTH, :]
    for s in range(1, CONV_WIDTH):
        xr = pltpu.roll(x, s, axis=0)
        hr = pltpu.roll(halo, s, axis=0)
        top = jnp.where(row8 < s, hr, xr[:SUBLANES])
        xs = jnp.concatenate([top, xr[SUBLANES:]], axis=0)
        acc = acc + xs * cw_ref[CONV_WIDTH - 1 - s:CONV_WIDTH - s, :]
    act = acc * _sigmoid(acc)

    ri = lax.broadcasted_iota(jnp.int32, (rows, rows), 0)
    ci = lax.broadcasted_iota(jnp.int32, (rows, rows), 1)
    same_chunk = (ri // c_sz) == (ci // c_sz)
    incl = same_chunk & (ri >= ci)
    strict = same_chunk & (ri > ci)
    tri_incl = jnp.where(incl, 1.0, 0.0).astype(BF16)
    eye = jnp.where(ri == ci, 1.0, 0.0)
    blk = []
    bsz = SUBLANES
    while bsz <= c_sz:
        blk.append((ri // bsz) == (ci // bsz))
        bsz *= 2
    li = lax.broadcasted_iota(jnp.int32, (LANES, LANES), 0)
    lj = lax.broadcasted_iota(jnp.int32, (LANES, LANES), 1)
    half_ones = jnp.where((li // HEAD_DIM) == (lj // HEAD_DIM), 1.0, 0.0).astype(BF16)
    lane_lo = lax.broadcasted_iota(jnp.int32, (rows, LANES), 1) < HEAD_DIM
    lane_lo_c = lax.broadcasted_iota(jnp.int32, (c_sz, LANES), 1) < HEAD_DIM

    ba = ba_ref[...]
    beta_all = _sigmoid(ba)
    sp_in = ba + dtb_ref[...]
    softplus = jnp.maximum(sp_in, 0.0) + jnp.log(1.0 + jnp.exp(-jnp.abs(sp_in)))
    g_all = -jnp.exp(alog_ref[...]) * softplus
    gcum = _dot_hi_exact_rhs_lhs(tri_incl, g_all)
    gcum_t = gcum.T

    for p in range(N_GDN_HEADS // 2):
        base = 4 * LANES * p
        g1 = [act[:, base:base + LANES], act[:, base + LANES:base + 2 * LANES]]
        vv = act[:, base + 2 * LANES:base + 3 * LANES]
        zz = act[:, base + 3 * LANES:base + 4 * LANES]
        o_pair = []
        for par in range(2):
            h = 2 * p + par
            mk = lane_lo if par == 0 else jnp.logical_not(lane_lo)
            mk_c = lane_lo_c if par == 0 else jnp.logical_not(lane_lo_c)
            g = g1[par]
            nrm = _dot_hi_exact_rhs(g * g, half_ones)
            g = g * lax.rsqrt(nrm + NORM_EPS)
            g_rot = pltpu.roll(g, HEAD_DIM, axis=1)
            xk = jnp.where(mk, g, 0.0)
            xq = jnp.where(mk, g_rot, 0.0)
            beta = beta_all[:, h:h + 1]
            gc_col = gcum[:, N_GDN_HEADS + h:N_GDN_HEADS + h + 1]
            gc_row = gcum_t[N_GDN_HEADS + h:N_GDN_HEADS + h + 1, :]
            decay = jnp.exp(jnp.where(incl, gc_col - gc_row, -jnp.inf))
            eg = jnp.exp(gc_col)
            xk_b = xk.astype(BF16)
            kk = _dot_nt((xk * beta).astype(BF16), xk_b)
            l_mat = jnp.where(strict, kk * decay, 0.0)
            attn = _dot_nt((xq * (HEAD_DIM ** -0.5)).astype(BF16), xk_b) * decay
            sol = jnp.where(mk, g * eg, vv) * beta
            a1 = jnp.where(blk[0], -l_mat, 0.0)
            a2 = _dot3(a1, a1)
            a4 = _dot3(a2, a2)
            inv = eye + a1
            inv = inv + _dot3(a2, inv)
            inv = inv + _dot3(a4, inv)
            for lvl in range(1, len(blk)):
                off = jnp.where(blk[lvl] & jnp.logical_not(blk[lvl - 1]), l_mat, 0.0)
                inv = inv - _dot3(inv, _dot3(off, inv))
            sol = _dot3(inv, sol)
            qd = (xq * (HEAD_DIM ** -0.5)) * eg
            st = state_ref[h]
            vn_parts = []
            qs_parts = []
            for c in range(nchunk):
                r0 = c * c_sz
                sol_c = sol[r0:r0 + c_sz]
                glast = gcum[r0 + c_sz - 1:r0 + c_sz, N_GDN_HEADS + h:N_GDN_HEADS + h + 1]
                kd_t = (xk[r0:r0 + c_sz] * jnp.exp(glast - gc_col[r0:r0 + c_sz])).T
                lhs = jnp.concatenate([jnp.where(mk_c, sol_c, 0.0), qd[r0:r0 + c_sz]], axis=0)
                m1 = _dot(lhs.astype(BF16), st.astype(BF16))
                vn = sol_c - m1[:c_sz]
                vn_parts.append(vn)
                qs_parts.append(m1[c_sz:])
                vn_m = jnp.where(mk_c, 0.0, vn)
                st = st * jnp.exp(glast) + _dot(kd_t.astype(BF16), vn_m.astype(BF16))
            state_ref[h] = st
            vn_all = jnp.concatenate(vn_parts, axis=0)
            o_pair.append(jnp.concatenate(qs_parts, axis=0)
                          + _dot(attn.astype(BF16), vn_all.astype(BF16)))
        o = jnp.where(lane_lo, o_pair[1], o_pair[0])
        ms = _dot_hi_exact_rhs(o * o, half_ones) * (1.0 / HEAD_DIM)
        y = o * lax.rsqrt(ms + NORM_EPS) * nw_ref[...] * zz
        y_ref[:, p * LANES:(p + 1) * LANES] = y.astype(BF16)


def _dot_hi_exact_rhs_lhs(m_bf16, x):
    hi, lo = _split_bf16(x)
    return _dot(m_bf16, hi) + _dot(m_bf16, lo)


def _gdn_mixer(gdn, ba, conv_p, alog_v, dtb_v, nw_v, seq):
    t, c = gdn.shape
    rows = GDN_SUPER
    nsc = seq // rows
    hb = rows // SUBLANES
    return pl.pallas_call(
        _gdn_kernel,
        grid=(t // seq, nsc),
        in_specs=[
            pl.BlockSpec((rows, c), lambda b, s: (b * nsc + s, 0)),
            pl.BlockSpec((SUBLANES, c), lambda b, s: (jnp.maximum((b * nsc + s) * hb - 1, 0), 0)),
            pl.BlockSpec((rows, LANES), lambda b, s: (b * nsc + s, 0)),
            pl.BlockSpec((CONV_WIDTH, c), lambda b, s: (0, 0)),
            pl.BlockSpec((1, LANES), lambda b, s: (0, 0)),
            pl.BlockSpec((1, LANES), lambda b, s: (0, 0)),
            pl.BlockSpec((1, LANES), lambda b, s: (0, 0)),
        ],
        out_specs=pl.BlockSpec((rows, GDN_DIM), lambda b, s: (b * nsc + s, 0)),
        out_shape=jax.ShapeDtypeStruct((t, GDN_DIM), BF16),
        scratch_shapes=[pltpu.VMEM((N_GDN_HEADS, LANES, LANES), F32)],
        compiler_params=_cparams(("arbitrary", "arbitrary")),
        name="gdn_mixer",
    )(gdn, gdn, ba, conv_p, alog_v, dtb_v, nw_v)


def _outproj_kernel(x_ref, mod_ref, yp_ref, ya_ref, yg_ref, wp_ref, wa_ref, wg_ref, lng_ref, lnb_ref,
                    rwh_ref, rwl_ref, rb_ref, x1_ref, h2_ref, logit_ref):
    y = _dot(yp_ref[...], wp_ref[...]) + _dot(ya_ref[...], wa_ref[...]) + _dot(yg_ref[...], wg_ref[...])
    g1 = mod_ref[0, 2:3, :]
    sh2 = mod_ref[0, 3:4, :]
    sc2 = mod_ref[0, 4:5, :]
    x1 = _layer_norm(DEEPNORM_ALPHA * x_ref[...] + g1 * y, lng_ref[...], lnb_ref[...])
    x1_ref[...] = x1
    h2 = x1 * (1.0 + sc2) + sh2
    hh, hl = _split_bf16(h2)
    h2_ref[...] = hh
    logit_ref[...] = (_dot(hh, rwh_ref[...]) + _dot(hl, rwh_ref[...]) + _dot(hh, rwl_ref[...])
                      + rb_ref[...])


def _out_projection(x2d, mod, yp, ya, yg, w_out_p, ln_g, ln_b, rw_hi, rw_lo, rb, seq):
    t, d = x2d.shape
    tm = ROW_TILE
    wp = w_out_p[:POOL_DIM]
    wa = w_out_p[POOL_DIM:POOL_DIM + ATT_DIM]
    wg = w_out_p[POOL_DIM + ATT_DIM:]
    row = lambda i: (i, 0)
    fixed = lambda i: (0, 0)
    return pl.pallas_call(
        _outproj_kernel,
        grid=(t // tm,),
        in_specs=[
            pl.BlockSpec((tm, d), row),
            pl.BlockSpec((1, 6, d), lambda i: ((i * tm) // seq, 0, 0)),
            pl.BlockSpec((tm, POOL_DIM), row),
            pl.BlockSpec((tm, ATT_DIM), row),
            pl.BlockSpec((tm, GDN_DIM), row),
            pl.BlockSpec((POOL_DIM, d), fixed),
            pl.BlockSpec((ATT_DIM, d), fixed),
            pl.BlockSpec((GDN_DIM, d), fixed),
            pl.BlockSpec((1, d), fixed),
            pl.BlockSpec((1, d), fixed),
            pl.BlockSpec((d, LANES), fixed),
            pl.BlockSpec((d, LANES), fixed),
            pl.BlockSpec((1, LANES), fixed),
        ],
        out_specs=[pl.BlockSpec((tm, d), row), pl.BlockSpec((tm, d), row), pl.BlockSpec((tm, LANES), row)],
        out_shape=[jax.ShapeDtypeStruct((t, d), F32), jax.ShapeDtypeStruct((t, d), BF16),
                   jax.ShapeDtypeStruct((t, LANES), F32)],
        compiler_params=_cparams(("arbitrary",)),
        name="out_proj_ln",
    )(x2d, mod, yp, ya, yg, wp, wa, wg, ln_g.reshape(1, d), ln_b.reshape(1, d), rw_hi, rw_lo, rb)


def _route_kernel(logit_ref, info_ref, cnt_ref, carry_ref):
    i = pl.program_id(0)

    @pl.when(i == 0)
    def _():
        carry_ref[...] = jnp.zeros_like(carry_ref)

    lg = logit_ref[...]
    tm = lg.shape[0]
    lane = lax.broadcasted_iota(jnp.int32, lg.shape, 1).astype(F32)
    work = lg
    vals, idxs = [], []
    for _k in range(TOP_K):
        m = jnp.max(work, axis=-1, keepdims=True)
        idx = jnp.min(jnp.where(work == m, lane, float(LANES)), axis=-1, keepdims=True)
        vals.append(m)
        idxs.append(idx)
        work = jnp.where(lane == idx, -jnp.inf, work)
    exps = [jnp.exp(v - vals[0]) for v in vals]
    den = exps[0] + exps[1] + exps[2] + exps[3]
    onehots = [lane == idx for idx in idxs]
    member = jnp.zeros(lg.shape, F32)
    for oh in onehots:
        member = member + jnp.where(oh, 1.0, 0.0)
    ri = lax.broadcasted_iota(jnp.int32, (tm, tm), 0)
    ci = lax.broadcasted_iota(jnp.int32, (tm, tm), 1)
    before = jnp.where(ri > ci, 1.0, 0.0).astype(BF16)
    rank = _dot(before, member.astype(BF16)) + carry_ref[...]
    carry_ref[...] = carry_ref[...] + jnp.sum(member, axis=0, keepdims=True)
    info = jnp.zeros(lg.shape, F32)
    for k in range(TOP_K):
        rank_k = jnp.sum(jnp.where(onehots[k], rank, 0.0), axis=-1, keepdims=True)
        info = jnp.where(lane == float(k), idxs[k], info)
        info = jnp.where(lane == float(TOP_K + k), rank_k, info)
        info = jnp.where(lane == float(2 * TOP_K + k), exps[k] / den, info)
    info_ref[...] = info
    cnt_ref[...] = carry_ref[...]


def _routing(logits):
    t = logits.shape[0]
    tm = ROUTE_TILE
    return pl.pallas_call(
        _route_kernel,
        grid=(t // tm,),
        in_specs=[pl.BlockSpec((tm, LANES), lambda i: (i, 0))],
        out_specs=[pl.BlockSpec((tm, LANES), lambda i: (i, 0)), pl.BlockSpec((1, LANES), lambda i: (0, 0))],
        out_shape=[jax.ShapeDtypeStruct((t, LANES), F32), jax.ShapeDtypeStruct((1, LANES), F32)],
        scratch_shapes=[pltpu.VMEM((1, LANES), F32)],
        compiler_params=_cparams(("arbitrary",)),
        name="moe_route",
    )(logits)


def _expert_kernel(be_ref, nu_ref, x_ref, wup_ref, bup_ref, wdn_ref, bdn_ref, y_ref, wup_bf, wdn_bf):
    i = pl.program_id(0)
    e = be_ref[i]
    prev = be_ref[jnp.maximum(i - 1, 0)]

    @pl.when((i == 0) | (e != prev))
    def _():
        wup_bf[...] = wup_ref[0].astype(BF16)
        wdn_bf[...] = wdn_ref[0].astype(BF16)

    @pl.when(i < nu_ref[0])
    def _():
        hb = _dot(x_ref[...], wup_bf[...]) + bup_ref[0]
        x_glu = jnp.minimum(hb[:, :EXPERT_DIM], SWIGLU_LIMIT)
        x_lin = jnp.clip(hb[:, EXPERT_DIM:], -SWIGLU_LIMIT, SWIGLU_LIMIT)
        act = x_glu * _sigmoid(SWIGLU_ALPHA * x_glu) * (x_lin + 1.0)
        y = _dot(act.astype(BF16), wdn_bf[...]) + bdn_ref[0]
        y_ref[...] = y.astype(BF16)

    @pl.when(i >= nu_ref[0])
    def _():
        y_ref[...] = jnp.zeros_like(y_ref)


def _expert_ffn(xbuf, block_e, n_used, w_up, b_up, w_down, b_down):
    p, d = xbuf.shape
    bm = EXPERT_BLOCK
    ne, _, n_up = w_up.shape
    grid_spec = pltpu.PrefetchScalarGridSpec(
        num_scalar_prefetch=2,
        grid=(p // bm,),
        in_specs=[
            pl.BlockSpec((bm, d), lambda i, be, nu: (i, 0)),
            pl.BlockSpec((1, d, n_up), lambda i, be, nu: (be[i], 0, 0)),
            pl.BlockSpec((1, 1, n_up), lambda i, be, nu: (be[i], 0, 0)),
            pl.BlockSpec((1, EXPERT_DIM, d), lambda i, be, nu: (be[i], 0, 0)),
            pl.BlockSpec((1, 1, d), lambda i, be, nu: (be[i], 0, 0)),
        ],
        out_specs=pl.BlockSpec((bm, d), lambda i, be, nu: (i, 0)),
        scratch_shapes=[pltpu.VMEM((d, n_up), BF16), pltpu.VMEM((EXPERT_DIM, d), BF16)],
    )
    return pl.pallas_call(
        _expert_kernel,
        grid_spec=grid_spec,
        out_shape=jax.ShapeDtypeStruct((p, d), BF16),
        compiler_params=_cparams(("arbitrary",)),
        name="expert_ffn",
    )(block_e, n_used, xbuf, w_up, b_up.reshape(ne, 1, n_up), w_down, b_down.reshape(ne, 1, d))


def _combine_kernel(x1_ref, mod_ref, yg_ref, info_ref, lng_ref, lnb_ref, o_ref):
    d = x1_ref.shape[1]
    info = info_ref[...]
    y = jnp.zeros(x1_ref.shape, F32)
    for k in range(TOP_K):
        gate = info[:, 2 * TOP_K + k:2 * TOP_K + k + 1]
        y = y + gate * yg_ref[:, k * d:(k + 1) * d].astype(F32)
    g2 = mod_ref[0, 5:6, :]
    o_ref[...] = _layer_norm(DEEPNORM_ALPHA * x1_ref[...] + g2 * y, lng_ref[...], lnb_ref[...])


def _combine(x1, mod, yg, info, ln_g, ln_b, seq):
    t, d = x1.shape
    tm = ROW_TILE
    row = lambda i: (i, 0)
    fixed = lambda i: (0, 0)
    return pl.pallas_call(
        _combine_kernel,
        grid=(t // tm,),
        in_specs=[
            pl.BlockSpec((tm, d), row),
            pl.BlockSpec((1, 6, d), lambda i: ((i * tm) // seq, 0, 0)),
            pl.BlockSpec((tm, TOP_K * d), row),
            pl.BlockSpec((tm, LANES), row),
            pl.BlockSpec((1, d), fixed),
            pl.BlockSpec((1, d), fixed),
        ],
        out_specs=pl.BlockSpec((tm, d), row),
        out_shape=jax.ShapeDtypeStruct((t, d), F32),
        compiler_params=_cparams(("arbitrary",)),
        name="moe_combine_ln",
    )(x1, mod, yg, info, ln_g.reshape(1, d), ln_b.reshape(1, d))


def _lane_vector(vals, offset):
    return jnp.zeros((1, LANES), F32).at[0, offset:offset + vals.shape[0]].set(vals.astype(F32))


def _moe(h2, logits, x1, mod, ln_g, ln_b, w_up, b_up, w_down, b_down, seq):
    t, d = h2.shape
    a = t * TOP_K
    bm = EXPERT_BLOCK
    info, cnt = _routing(logits)
    e_idx = info[:, 0:TOP_K].astype(jnp.int32)
    rank = info[:, TOP_K:2 * TOP_K].astype(jnp.int32)
    counts = cnt[0, :N_EXPERTS].astype(jnp.int32)
    padded = ((counts + bm - 1) // bm) * bm
    pcum = jnp.cumsum(padded)
    pstart = pcum - padded
    dest = (pstart[e_idx] + rank).reshape(a)
    n_blocks = -(-a // bm) + N_EXPERTS
    block_e = jnp.minimum(jnp.searchsorted(pcum, jnp.arange(n_blocks) * bm, side='right'),
                          N_EXPERTS - 1).astype(jnp.int32)
    n_used = (pcum[-1] // bm).astype(jnp.int32).reshape(1)
    src_tok = jnp.zeros((n_blocks * bm,), jnp.int32).at[dest].set(jnp.arange(a, dtype=jnp.int32) // TOP_K)
    xbuf = h2[src_tok]
    ybuf = _expert_ffn(xbuf, block_e, n_used, w_up, b_up, w_down, b_down)
    yg = ybuf[dest].reshape(t, TOP_K * d)
    return _combine(x1, mod, yg, info, ln_g, ln_b, seq)


def kernel(x, c, rel_bias, w_in, w_out, w_ada, b_ada, ln1_g, ln1_b, ln2_g, ln2_b, pool_w, pool_scale,
           attn_sinks, conv_w, gdn_a_log, gdn_dt_bias, gdn_norm_w, router_w, router_b,
           exp_w_up, exp_b_up, exp_w_down, exp_b_down):
    bsz, seq, d = x.shape
    depth = w_in.shape[0]
    t = bsz * seq
    assert d == D_MODEL and w_in.shape[2] == IN_DIM
    assert seq % GDN_SUPER == 0 and seq % WINDOW == 0 and t % ROW_TILE == 0 and seq % ROW_TILE == 0

    mod_all = _modulation(c, w_ada, b_ada).reshape(depth, bsz, 6, d)
    bias = jnp.transpose(jnp.take(rel_bias.astype(F32), jnp.asarray(_BUCKETS), axis=0), (2, 0, 1))

    x2d = x.reshape(t, d)
    for l in range(depth):
        mod = mod_all[l]
        w_in_p = _take_cols(w_in[l], _IN_PERM).astype(BF16)
        w_out_p = jnp.take(w_out[l], jnp.asarray(_OUT_PERM), axis=0).astype(BF16)
        ident = jnp.zeros((CONV_WIDTH, 1), F32).at[CONV_WIDTH - 1, 0].set(1.0)
        conv_p = jnp.where(jnp.asarray(_GDN_CONV_SRC >= 0),
                           jnp.take(conv_w[l].astype(F32), jnp.asarray(np.maximum(_GDN_CONV_SRC, 0)), axis=1),
                           ident)
        pool_bd = jnp.zeros((POOL_DIM, POOL_DIM), F32)
        for gi in range(len(POOL_WINDOWS)):
            sl = slice(gi * POOL_GROUP, (gi + 1) * POOL_GROUP)
            pool_bd = pool_bd.at[sl, sl].set(pool_w[l, gi].astype(F32))
        alog_v = _lane_vector(gdn_a_log[l], N_GDN_HEADS)
        dtb_v = _lane_vector(gdn_dt_bias[l], N_GDN_HEADS)
        nw_v = jnp.tile(gdn_norm_w[l].astype(F32), 2).reshape(1, LANES)
        rw = jnp.zeros((d, LANES), F32).at[:, :N_EXPERTS].set(router_w[l].astype(F32))
        rw_hi, rw_lo = _split_bf16(rw)
        rb = jnp.full((1, LANES), NEG_INF, F32).at[0, :N_EXPERTS].set(router_b[l].astype(F32))

        u_pool, aq, akv, gdn, ba = _in_projection(x2d, mod, w_in_p, seq)
        y_pool = _pool_mixer(u_pool, pool_bd.astype(BF16), pool_scale[l].astype(F32), seq)
        y_att = _swa_attention(aq, akv, bias, attn_sinks[l].astype(F32), seq)
        y_gdn = _gdn_mixer(gdn, ba, conv_p, alog_v, dtb_v, nw_v, seq)
        x1, h2, logits = _out_projection(x2d, mod, y_pool, y_att, y_gdn, w_out_p, ln1_g[l], ln1_b[l],
                                         rw_hi, rw_lo, rb, seq)
        x2d = _moe(h2, logits, x1, mod, ln2_g[l], ln2_b[l], exp_w_up[l], exp_b_up[l],
                   exp_w_down[l], exp_b_down[l], seq)
    return x2d.reshape(bsz, seq, d)
```

```python
import functools

import numpy as np
import jax
import jax.numpy as jnp
from jax import lax
from jax.experimental import pallas as pl
from jax.experimental.pallas import tpu as pltpu

F32 = jnp.float32
BF16 = jnp.bfloat16

D_MODEL = 1024
HEAD_DIM = 64
POOL_DIM = 256
POOL_WINDOWS = (2, 4, 8, 16)
POOL_GROUP = 64
N_ATT_HEADS = 6
N_KV_HEADS = 2
ATT_DIM = 384
KV_DIM = 128
WINDOW = 128
N_BUCKETS = 32
MAX_DISTANCE = 128
N_GDN_HEADS = 6
GDN_DIM = 384
CONV_WIDTH = 4
GDN_CHUNK = 64
N_EXPERTS = 32
TOP_K = 4
EXPERT_DIM = 1024
SWIGLU_ALPHA = 1.702
SWIGLU_LIMIT = 7.0
DEPTH = 2
DEEPNORM_ALPHA = (2 * DEPTH) ** 0.25
LN_EPS = 1e-5
NORM_EPS = 1e-6
NEG_INF = -1e30

LANES = 128
SUBLANES = 8
VMEM_LIMIT = 56 * 1024 * 1024

ROW_TILE = 512
GDN_SUPER = 256
ROUTE_TILE = 256
EXPERT_BLOCK = 256

_OFF_AQ = POOL_DIM
_OFF_AK = _OFF_AQ + ATT_DIM
_OFF_AV = _OFF_AK + KV_DIM
_OFF_GQ = _OFF_AV + KV_DIM
_OFF_GK = _OFF_GQ + GDN_DIM
_OFF_GV = _OFF_GK + GDN_DIM
_OFF_GZ = _OFF_GV + GDN_DIM
_OFF_GB = _OFF_GZ + GDN_DIM
_OFF_GA = _OFF_GB + N_GDN_HEADS
IN_DIM = _OFF_GA + N_GDN_HEADS

P_POOL = (0, POOL_DIM)
P_Q = (P_POOL[1], P_POOL[1] + ATT_DIM)
P_KV = (P_Q[1], P_Q[1] + 2 * KV_DIM)
P_GDN = (P_KV[1], P_KV[1] + 4 * GDN_DIM)
P_BA = (P_GDN[1], P_GDN[1] + LANES)
P_TOTAL = P_BA[1]


def _head_cols(off, h):
    return list(range(off + HEAD_DIM * h, off + HEAD_DIM * (h + 1)))


def _build_in_perm():
    cols = list(range(POOL_DIM))
    for p in range(N_ATT_HEADS // 2):
        cols += _head_cols(_OFF_AQ, p) + _head_cols(_OFF_AQ, p + 3)
    cols += list(range(_OFF_AK, _OFF_AK + 2 * KV_DIM))
    gdn_src = []
    for p in range(N_GDN_HEADS // 2):
        e, o = 2 * p, 2 * p + 1
        grp = (_head_cols(_OFF_GK, e) + _head_cols(_OFF_GQ, e)
               + _head_cols(_OFF_GQ, o) + _head_cols(_OFF_GK, o)
               + _head_cols(_OFF_GV, o) + _head_cols(_OFF_GV, e)
               + _head_cols(_OFF_GZ, o) + _head_cols(_OFF_GZ, e))
        cols += grp
        gdn_src += [c - _OFF_GQ if c < _OFF_GZ else -1 for c in grp]
    cols += list(range(_OFF_GB, _OFF_GB + 2 * N_GDN_HEADS))
    cols += [-1] * (LANES - 2 * N_GDN_HEADS)
    assert len(cols) == P_TOTAL
    return np.asarray(cols, np.int32), np.asarray(gdn_src, np.int32)


_IN_PERM, _GDN_CONV_SRC = _build_in_perm()


def _build_out_perm():
    rows = list(range(POOL_DIM))
    for p in range(N_ATT_HEADS // 2):
        rows += _head_cols(POOL_DIM, p) + _head_cols(POOL_DIM, p + 3)
    for p in range(N_GDN_HEADS // 2):
        rows += _head_cols(POOL_DIM + ATT_DIM, 2 * p + 1) + _head_cols(POOL_DIM + ATT_DIM, 2 * p)
    return np.asarray(rows, np.int32)


_OUT_PERM = _build_out_perm()


def _t5_bucket_table():
    qi = np.arange(WINDOW)[:, None]
    kj = np.arange(2 * WINDOW)[None, :]
    n = np.maximum(qi + WINDOW - kj, 0)
    max_exact = N_BUCKETS // 2
    nf = np.maximum(n, 1).astype(np.float32)
    large = max_exact + (np.log(nf / max_exact) / np.float32(np.log(MAX_DISTANCE / max_exact))
                         * (N_BUCKETS - max_exact)).astype(np.int32)
    large = np.minimum(large, N_BUCKETS - 1)
    return np.where(n < max_exact, n, large).astype(np.int32)


_BUCKETS = _t5_bucket_table()


def _take_cols(w, perm):
    safe = np.where(perm < 0, 0, perm)
    out = jnp.take(w, jnp.asarray(safe), axis=-1)
    if (perm < 0).any():
        out = jnp.where(jnp.asarray(perm >= 0), out, jnp.zeros((), w.dtype))
    return out


def _split_bf16(x):
    hi = x.astype(BF16)
    lo = (x - hi.astype(F32)).astype(BF16)
    return hi, lo


def _dot(a, b):
    return jnp.dot(a, b, preferred_element_type=F32)


def _dot_nt(a, b):
    return lax.dot_general(a, b, (((1,), (1,)), ((), ())), preferred_element_type=F32)


def _dot_hi_exact_rhs(x, m_bf16):
    hi, lo = _split_bf16(x)
    return _dot(hi, m_bf16) + _dot(lo, m_bf16)


def _sigmoid(x):
    return 1.0 / (1.0 + jnp.exp(-x))


def _layer_norm(r, g, b):
    mu = jnp.mean(r, axis=-1, keepdims=True)
    d = r - mu
    var = jnp.mean(d * d, axis=-1, keepdims=True)
    return d * lax.rsqrt(var + LN_EPS) * g + b


def _cparams(sem):
    return pltpu.CompilerParams(dimension_semantics=sem, vmem_limit_bytes=VMEM_LIMIT)


def _mod_kernel(c_ref, w_ref, b_ref, o_ref):
    c = c_ref[...]
    ca = c * _sigmoid(c)
    ch, cl = _split_bf16(ca)
    wh, wl = _split_bf16(w_ref[0])
    o_ref[0] = _dot(ch, wh) + _dot(cl, wh) + _dot(ch, wl) + b_ref[0]


def _modulation(c, w_ada, b_ada):
    depth, d, n = w_ada.shape
    bsz = c.shape[0]
    tn = 512
    return pl.pallas_call(
        _mod_kernel,
        grid=(depth, n // tn),
        in_specs=[
            pl.BlockSpec((bsz, d), lambda l, j: (0, 0)),
            pl.BlockSpec((1, d, tn), lambda l, j: (l, 0, j)),
            pl.BlockSpec((1, 1, tn), lambda l, j: (l, 0, j)),
        ],
        out_specs=pl.BlockSpec((1, bsz, tn), lambda l, j: (l, 0, j)),
        out_shape=jax.ShapeDtypeStruct((depth, bsz, n), F32),
        compiler_params=_cparams(("arbitrary", "arbitrary")),
        name="adaln_mod",
    )(c, w_ada, b_ada.reshape(depth, 1, n))


def _inproj_kernel(x_ref, mod_ref, w_ref, pool_ref, q_ref, kv_ref, gdn_ref, ba_ref):
    sh = mod_ref[0, 0:1, :]
    sc = mod_ref[0, 1:2, :]
    h = (x_ref[...] * (1.0 + sc) + sh).astype(BF16)

    def mm(rng):
        return _dot(h, w_ref[:, rng[0]:rng[1]])

    pool_ref[...] = mm(P_POOL)
    q_ref[...] = mm(P_Q).astype(BF16)
    kv_ref[...] = mm(P_KV).astype(BF16)
    gdn_ref[...] = mm(P_GDN)
    ba_ref[...] = mm(P_BA)


def _in_projection(x2d, mod, w_in_p, seq):
    t, d = x2d.shape
    tm = ROW_TILE
    widths = [r[1] - r[0] for r in (P_POOL, P_Q, P_KV, P_GDN, P_BA)]
    dtypes = [F32, BF16, BF16, F32, F32]
    return pl.pallas_call(
        _inproj_kernel,
        grid=(t // tm,),
        in_specs=[
            pl.BlockSpec((tm, d), lambda i: (i, 0)),
            pl.BlockSpec((1, 6, d), lambda i: ((i * tm) // seq, 0, 0)),
            pl.BlockSpec((d, P_TOTAL), lambda i: (0, 0)),
        ],
        out_specs=[pl.BlockSpec((tm, w), lambda i: (i, 0)) for w in widths],
        out_shape=[jax.ShapeDtypeStruct((t, w), dt) for w, dt in zip(widths, dtypes)],
        compiler_params=_cparams(("arbitrary",)),
        name="in_proj",
    )(x2d, mod, w_in_p)


def _pool_kernel(u_ref, w_ref, scale_ref, o_ref):
    u = u_ref[...]
    row = lax.broadcasted_iota(jnp.int32, u.shape, 0)
    lane = lax.broadcasted_iota(jnp.int32, u.shape, 1)

    def shifted(a, s):
        return jnp.where(row >= s, pltpu.roll(a, s, axis=0), 0.0)

    sums = []
    acc = u
    for wdt in POOL_WINDOWS:
        acc = acc + shifted(acc, wdt // 2)
        sums.append(acc)
    grp = lane // POOL_GROUP
    wsum = sums[-1]
    win = jnp.full(u.shape, POOL_WINDOWS[-1], jnp.int32)
    for gi in range(len(POOL_WINDOWS) - 2, -1, -1):
        wsum = jnp.where(grp == gi, sums[gi], wsum)
        win = jnp.where(grp == gi, POOL_WINDOWS[gi], win)
    cnt = jnp.minimum(row + 1, win).astype(F32)
    p = wsum / cnt - u
    y = _dot(p.astype(BF16), w_ref[...]) * scale_ref[...]
    o_ref[...] = y.astype(BF16)


def _pool_mixer(u, pool_w_bd, pool_scale, seq):
    t, c = u.shape
    return pl.pallas_call(
        _pool_kernel,
        grid=(t // seq,),
        in_specs=[
            pl.BlockSpec((seq, c), lambda b: (b, 0)),
            pl.BlockSpec((c, c), lambda b: (0, 0)),
            pl.BlockSpec((1, c), lambda b: (0, 0)),
        ],
        out_specs=pl.BlockSpec((seq, c), lambda b: (b, 0)),
        out_shape=jax.ShapeDtypeStruct((t, c), BF16),
        compiler_params=_cparams(("arbitrary",)),
        name="pool_mixer",
    )(u, pool_w_bd, pool_scale.reshape(1, c))


def _attn_kernel(sink_ref, q_ref, kvc_ref, kvp_ref, bias_ref, o_ref):
    nb = pl.program_id(1)
    kv = jnp.concatenate([kvp_ref[...], kvc_ref[...]], axis=0)
    k = kv[:, :KV_DIM]
    v = kv[:, KV_DIM:]
    qi = lax.broadcasted_iota(jnp.int32, (WINDOW, 2 * WINDOW), 0)
    kj = lax.broadcasted_iota(jnp.int32, (WINDOW, 2 * WINDOW), 1)
    dist = qi + WINDOW - kj
    valid = (dist >= 0) & (dist < WINDOW) & ((kj >= WINDOW) | (nb > 0))
    lo = lax.broadcasted_iota(jnp.int32, (WINDOW, LANES), 1) < HEAD_DIM
    for p in range(N_ATT_HEADS // 2):
        qp = q_ref[:, p * LANES:(p + 1) * LANES]
        halves = []
        for half in range(2):
            h = p + 3 * half
            qm = jnp.where(lo if half == 0 else jnp.logical_not(lo), qp, jnp.zeros_like(qp))
            s = _dot_nt(qm, k) * (HEAD_DIM ** -0.5)
            s = jnp.where(valid, s + bias_ref[h], NEG_INF)
            sink = sink_ref[h]
            m = jnp.maximum(jnp.max(s, axis=-1, keepdims=True), sink)
            pr = jnp.exp(s - m)
            den = jnp.sum(pr, axis=-1, keepdims=True) + jnp.exp(sink - m)
            halves.append(_dot(pr.astype(BF16), v) / den)
        o_ref[:, p * LANES:(p + 1) * LANES] = jnp.where(lo, halves[0], halves[1]).astype(BF16)


def _swa_attention(q, kv, bias, sinks, seq):
    t = q.shape[0]
    nblk = seq // WINDOW
    return pl.pallas_call(
        _attn_kernel,
        grid=(t // seq, nblk),
        in_specs=[
            pl.BlockSpec(memory_space=pltpu.SMEM),
            pl.BlockSpec((WINDOW, ATT_DIM), lambda b, n: (b * nblk + n, 0)),
            pl.BlockSpec((WINDOW, 2 * KV_DIM), lambda b, n: (b * nblk + n, 0)),
            pl.BlockSpec((WINDOW, 2 * KV_DIM), lambda b, n: (b * nblk + jnp.maximum(n - 1, 0), 0)),
            pl.BlockSpec((N_ATT_HEADS, WINDOW, 2 * WINDOW), lambda b, n: (0, 0, 0)),
        ],
        out_specs=pl.BlockSpec((WINDOW, ATT_DIM), lambda b, n: (b * nblk + n, 0)),
        out_shape=jax.ShapeDtypeStruct((t, ATT_DIM), BF16),
        compiler_params=_cparams(("arbitrary", "arbitrary")),
        name="swa_attention",
    )(sinks, q, kv, kv, bias)


def _gdn_kernel(x_ref, halo_ref, ba_ref, cw_ref, alog_ref, dtb_ref, nw_ref, y_ref, state_ref):
    sc_id = pl.program_id(1)
    rows = GDN_SUPER
    nchunk = rows // GDN_CHUNK
    c_sz = GDN_CHUNK

    @pl.when(sc_id == 0)
    def _():
        state_ref[...] = jnp.zeros_like(state_ref)

    x = x_ref[...]
    halo = jnp.where(sc_id == 0, 0.0, halo_ref[...])
    row8 = lax.broadcasted_iota(jnp.int32, halo.shape, 0)
    acc = x * cw_ref[CONV_WIDTH - 1:CONV_WIDTH, :]
    for s in range(1, CONV_WIDTH):
        xr = pltpu.roll(x, s, axis=0)
        hr = pltpu.roll(halo, s, axis=0)
        top = jnp.where(row8 < s, hr, xr[:SUBLANES])
        xs = jnp.concatenate([top, xr[SUBLANES:]], axis=0)
        acc = acc + xs * cw_ref[CONV_WIDTH - 1 - s:CONV_WIDTH - s, :]
    act = acc * _sigmoid(acc)

    ri = lax.broadcasted_iota(jnp.int32, (rows, rows), 0)
    ci = lax.broadcasted_iota(jnp.int32, (rows, rows), 1)
    same_chunk = (ri // c_sz) == (ci // c_sz)
    incl = same_chunk & (ri >= ci)
    strict = same_chunk & (ri > ci)
    tri_incl = jnp.where(incl, 1.0, 0.0).astype(BF16)
    eye = jnp.where(ri == ci, 1.0, 0.0)
    blk = []
    bsz = SUBLANES
    while bsz <= c_sz:
        blk.append((ri // bsz) == (ci // bsz))
        bsz *= 2
    li = lax.broadcasted_iota(jnp.int32, (LANES, LANES), 0)
    lj = lax.broadcasted_iota(jnp.int32, (LANES, LANES), 1)
    half_ones = jnp.where((li // HEAD_DIM) == (lj // HEAD_DIM), 1.0, 0.0).astype(BF16)
    lane_lo = lax.broadcasted_iota(jnp.int32, (rows, LANES), 1) < HEAD_DIM
    lane_lo_c = lax.broadcasted_iota(jnp.int32, (c_sz, LANES), 1) < HEAD_DIM

    ba = ba_ref[...]
    beta_all = _sigmoid(ba)
    sp_in = ba + dtb_ref[...]
    softplus = jnp.maximum(sp_in, 0.0) + jnp.log(1.0 + jnp.exp(-jnp.abs(sp_in)))
    g_all = -jnp.exp(alog_ref[...]) * softplus
    gcum = _dot_hi_exact_rhs_lhs(tri_incl, g_all)
    gcum_t = gcum.T

    heads = range(N_GDN_HEADS)
    lane_hi = jnp.logical_not(lane_lo)
    lane_hi_c = jnp.logical_not(lane_lo_c)
    mk = [lane_lo if h % 2 == 0 else lane_hi for h in heads]
    mk_c = [lane_lo_c if h % 2 == 0 else lane_hi_c for h in heads]
    scale = HEAD_DIM ** -0.5

    def bdot(a, b):
        return _dot(a.astype(BF16), b.astype(BF16))

    xk, xq, gn, gc_col, beta, eg = [], [], [], [], [], []
    for h in heads:
        base = 4 * LANES * (h // 2) + LANES * (h % 2)
        g = act[:, base:base + LANES]
        g = g * lax.rsqrt(_dot_hi_exact_rhs(g * g, half_ones) + NORM_EPS)
        gn.append(g)
        xk.append(jnp.where(mk[h], g, 0.0))
        xq.append(jnp.where(mk[h], pltpu.roll(g, HEAD_DIM, axis=1), 0.0) * scale)
        beta.append(beta_all[:, h:h + 1])
        gc_col.append(gcum[:, N_GDN_HEADS + h:N_GDN_HEADS + h + 1])
        eg.append(jnp.exp(gc_col[h]))

    l_mat, attn, rhs = [], [], []
    for h in heads:
        gc_row = gcum_t[N_GDN_HEADS + h:N_GDN_HEADS + h + 1, :]
        decay = jnp.exp(jnp.where(incl, gc_col[h] - gc_row, -jnp.inf))
        xk_b = xk[h].astype(BF16)
        kk = _dot_nt((xk[h] * beta[h]).astype(BF16), xk_b)
        l_mat.append(jnp.where(strict, kk * decay, 0.0))
        attn.append((_dot_nt(xq[h].astype(BF16), xk_b) * decay).astype(BF16))
        vv = act[:, 4 * LANES * (h // 2) + 2 * LANES:4 * LANES * (h // 2) + 3 * LANES]
        rhs.append(jnp.where(mk[h], gn[h] * eg[h], vv) * beta[h])

    a1 = [jnp.where(blk[0], -l_mat[h], 0.0).astype(BF16) for h in heads]
    a2 = [_dot(a1[h], a1[h]).astype(BF16) for h in heads]
    a4 = [_dot(a2[h], a2[h]).astype(BF16) for h in heads]
    inv = [eye + a1[h].astype(F32) for h in heads]
    inv = [inv[h] + _dot(a2[h], inv[h].astype(BF16)) for h in heads]
    inv = [inv[h] + _dot(a4[h], inv[h].astype(BF16)) for h in heads]
    for lvl in range(1, len(blk) - 1):
        band = blk[lvl] & jnp.logical_not(blk[lvl - 1])
        inv_b = [inv[h].astype(BF16) for h in heads]
        mid = [_dot(jnp.where(band, l_mat[h], 0.0).astype(BF16), inv_b[h]) for h in heads]
        inv = [inv[h] - _dot(inv_b[h], mid[h].astype(BF16)) for h in heads]
    band = blk[-1] & jnp.logical_not(blk[-2])
    inv_b = [inv[h].astype(BF16) for h in heads]
    half = [_dot(inv_b[h], rhs[h].astype(BF16)) for h in heads]
    mid = [_dot(jnp.where(band, l_mat[h], 0.0).astype(BF16), half[h].astype(BF16)) for h in heads]
    sol = [half[h] - _dot(inv_b[h], mid[h].astype(BF16)) for h in heads]

    st = [state_ref[h] for h in heads]
    vn_parts = [[] for _ in heads]
    qs_parts = [[] for _ in heads]
    for c in range(nchunk):
        r0 = c * c_sz
        for h in heads:
            sol_c = sol[h][r0:r0 + c_sz]
            glast = gcum[r0 + c_sz - 1:r0 + c_sz, N_GDN_HEADS + h:N_GDN_HEADS + h + 1]
            kd_t = (xk[h][r0:r0 + c_sz] * jnp.exp(glast - gc_col[h][r0:r0 + c_sz])).T
            qd = xq[h][r0:r0 + c_sz] * eg[h][r0:r0 + c_sz]
            lhs = jnp.concatenate([jnp.where(mk_c[h], sol_c, 0.0), qd], axis=0)
            m1 = bdot(lhs, st[h])
            vn = sol_c - m1[:c_sz]
            vn_parts[h].append(vn)
            qs_parts[h].append(m1[c_sz:])
            st[h] = st[h] * jnp.exp(glast) + bdot(kd_t, jnp.where(mk_c[h], 0.0, vn))
    for h in heads:
        state_ref[h] = st[h]

    for p in range(N_GDN_HEADS // 2):
        o_pair = []
        for h in (2 * p, 2 * p + 1):
            vn_all = jnp.concatenate(vn_parts[h], axis=0)
            o_pair.append(jnp.concatenate(qs_parts[h], axis=0) + _dot(attn[h], vn_all.astype(BF16)))
        o = jnp.where(lane_lo, o_pair[1], o_pair[0])
        ms = _dot_hi_exact_rhs(o * o, half_ones) * (1.0 / HEAD_DIM)
        zz = act[:, 4 * LANES * p + 3 * LANES:4 * LANES * (p + 1)]
        y = o * lax.rsqrt(ms + NORM_EPS) * nw_ref[...] * zz
        y_ref[:, p * LANES:(p + 1) * LANES] = y.astype(BF16)


def _dot_hi_exact_rhs_lhs(m_bf16, x):
    hi, lo = _split_bf16(x)
    return _dot(m_bf16, hi) + _dot(m_bf16, lo)


def _gdn_mixer(gdn, ba, conv_p, alog_v, dtb_v, nw_v, seq):
    t, c = gdn.shape
    rows = GDN_SUPER
    nsc = seq // rows
    hb = rows // SUBLANES
    return pl.pallas_call(
        _gdn_kernel,
        grid=(t // seq, nsc),
        in_specs=[
            pl.BlockSpec((rows, c), lambda b, s: (b * nsc + s, 0)),
            pl.BlockSpec((SUBLANES, c), lambda b, s: (jnp.maximum((b * nsc + s) * hb - 1, 0), 0)),
            pl.BlockSpec((rows, LANES), lambda b, s: (b * nsc + s, 0)),
            pl.BlockSpec((CONV_WIDTH, c), lambda b, s: (0, 0)),
            pl.BlockSpec((1, LANES), lambda b, s: (0, 0)),
            pl.BlockSpec((1, LANES), lambda b, s: (0, 0)),
            pl.BlockSpec((1, LANES), lambda b, s: (0, 0)),
        ],
        out_specs=pl.BlockSpec((rows, GDN_DIM), lambda b, s: (b * nsc + s, 0)),
        out_shape=jax.ShapeDtypeStruct((t, GDN_DIM), BF16),
        scratch_shapes=[pltpu.VMEM((N_GDN_HEADS, LANES, LANES), F32)],
        compiler_params=_cparams(("arbitrary", "arbitrary")),
        name="gdn_mixer",
    )(gdn, gdn, ba, conv_p, alog_v, dtb_v, nw_v)


def _outproj_kernel(x_ref, mod_ref, yp_ref, ya_ref, yg_ref, wp_ref, wa_ref, wg_ref, lng_ref, lnb_ref,
                    rwh_ref, rwl_ref, rb_ref, x1_ref, h2_ref, logit_ref):
    y = _dot(yp_ref[...], wp_ref[...]) + _dot(ya_ref[...], wa_ref[...]) + _dot(yg_ref[...], wg_ref[...])
    g1 = mod_ref[0, 2:3, :]
    sh2 = mod_ref[0, 3:4, :]
    sc2 = mod_ref[0, 4:5, :]
    x1 = _layer_norm(DEEPNORM_ALPHA * x_ref[...] + g1 * y, lng_ref[...], lnb_ref[...])
    x1_ref[...] = x1
    h2 = x1 * (1.0 + sc2) + sh2
    hh, hl = _split_bf16(h2)
    h2_ref[...] = hh
    logit_ref[...] = (_dot(hh, rwh_ref[...]) + _dot(hl, rwh_ref[...]) + _dot(hh, rwl_ref[...])
                      + rb_ref[...])


def _out_projection(x2d, mod, yp, ya, yg, w_out_p, ln_g, ln_b, rw_hi, rw_lo, rb, seq):
    t, d = x2d.shape
    tm = ROW_TILE
    wp = w_out_p[:POOL_DIM]
    wa = w_out_p[POOL_DIM:POOL_DIM + ATT_DIM]
    wg = w_out_p[POOL_DIM + ATT_DIM:]
    row = lambda i: (i, 0)
    fixed = lambda i: (0, 0)
    return pl.pallas_call(
        _outproj_kernel,
        grid=(t // tm,),
        in_specs=[
            pl.BlockSpec((tm, d), row),
            pl.BlockSpec((1, 6, d), lambda i: ((i * tm) // seq, 0, 0)),
            pl.BlockSpec((tm, POOL_DIM), row),
            pl.BlockSpec((tm, ATT_DIM), row),
            pl.BlockSpec((tm, GDN_DIM), row),
            pl.BlockSpec((POOL_DIM, d), fixed),
            pl.BlockSpec((ATT_DIM, d), fixed),
            pl.BlockSpec((GDN_DIM, d), fixed),
            pl.BlockSpec((1, d), fixed),
            pl.BlockSpec((1, d), fixed),
            pl.BlockSpec((d, LANES), fixed),
            pl.BlockSpec((d, LANES), fixed),
            pl.BlockSpec((1, LANES), fixed),
        ],
        out_specs=[pl.BlockSpec((tm, d), row), pl.BlockSpec((tm, d), row), pl.BlockSpec((tm, LANES), row)],
        out_shape=[jax.ShapeDtypeStruct((t, d), F32), jax.ShapeDtypeStruct((t, d), BF16),
                   jax.ShapeDtypeStruct((t, LANES), F32)],
        compiler_params=_cparams(("arbitrary",)),
        name="out_proj_ln",
    )(x2d, mod, yp, ya, yg, wp, wa, wg, ln_g.reshape(1, d), ln_b.reshape(1, d), rw_hi, rw_lo, rb)


def _route_kernel(logit_ref, info_ref, cnt_ref, carry_ref):
    i = pl.program_id(0)

    @pl.when(i == 0)
    def _():
        carry_ref[...] = jnp.zeros_like(carry_ref)

    lg = logit_ref[...]
    tm = lg.shape[0]
    lane = lax.broadcasted_iota(jnp.int32, lg.shape, 1).astype(F32)
    work = lg
    vals, idxs = [], []
    for _k in range(TOP_K):
        m = jnp.max(work, axis=-1, keepdims=True)
        idx = jnp.min(jnp.where(work == m, lane, float(LANES)), axis=-1, keepdims=True)
        vals.append(m)
        idxs.append(idx)
        work = jnp.where(lane == idx, -jnp.inf, work)
    exps = [jnp.exp(v - vals[0]) for v in vals]
    den = exps[0] + exps[1] + exps[2] + exps[3]
    onehots = [lane == idx for idx in idxs]
    member = jnp.zeros(lg.shape, F32)
    for oh in onehots:
        member = member + jnp.where(oh, 1.0, 0.0)
    ri = lax.broadcasted_iota(jnp.int32, (tm, tm), 0)
    ci = lax.broadcasted_iota(jnp.int32, (tm, tm), 1)
    before = jnp.where(ri > ci, 1.0, 0.0).astype(BF16)
    rank = _dot(before, member.astype(BF16)) + carry_ref[...]
    carry_ref[...] = carry_ref[...] + jnp.sum(member, axis=0, keepdims=True)
    info = jnp.zeros(lg.shape, F32)
    for k in range(TOP_K):
        rank_k = jnp.sum(jnp.where(onehots[k], rank, 0.0), axis=-1, keepdims=True)
        info = jnp.where(lane == float(k), idxs[k], info)
        info = jnp.where(lane == float(TOP_K + k), rank_k, info)
        info = jnp.where(lane == float(2 * TOP_K + k), exps[k] / den, info)
    info_ref[...] = info
    cnt_ref[...] = carry_ref[...]


def _routing(logits):
    t = logits.shape[0]
    tm = ROUTE_TILE
    return pl.pallas_call(
        _route_kernel,
        grid=(t // tm,),
        in_specs=[pl.BlockSpec((tm, LANES), lambda i: (i, 0))],
        out_specs=[pl.BlockSpec((tm, LANES), lambda i: (i, 0)), pl.BlockSpec((1, LANES), lambda i: (0, 0))],
        out_shape=[jax.ShapeDtypeStruct((t, LANES), F32), jax.ShapeDtypeStruct((1, LANES), F32)],
        scratch_shapes=[pltpu.VMEM((1, LANES), F32)],
        compiler_params=_cparams(("arbitrary",)),
        name="moe_route",
    )(logits)


def _expert_kernel(be_ref, nu_ref, x_ref, wup_ref, bup_ref, wdn_ref, bdn_ref, y_ref, wup_bf, wdn_bf):
    i = pl.program_id(0)
    e = be_ref[i]
    prev = be_ref[jnp.maximum(i - 1, 0)]

    @pl.when((i == 0) | (e != prev))
    def _():
        wup_bf[...] = wup_ref[0].astype(BF16)
        wdn_bf[...] = wdn_ref[0].astype(BF16)

    @pl.when(i < nu_ref[0])
    def _():
        hb = _dot(x_ref[...], wup_bf[...]) + bup_ref[0]
        x_glu = jnp.minimum(hb[:, :EXPERT_DIM], SWIGLU_LIMIT)
        x_lin = jnp.clip(hb[:, EXPERT_DIM:], -SWIGLU_LIMIT, SWIGLU_LIMIT)
        act = x_glu * _sigmoid(SWIGLU_ALPHA * x_glu) * (x_lin + 1.0)
        y = _dot(act.astype(BF16), wdn_bf[...]) + bdn_ref[0]
        y_ref[...] = y.astype(BF16)

    @pl.when(i >= nu_ref[0])
    def _():
        y_ref[...] = jnp.zeros_like(y_ref)


def _expert_ffn(xbuf, block_e, n_used, w_up, b_up, w_down, b_down, layer):
    p, d = xbuf.shape
    bm = EXPERT_BLOCK
    ne, _, n_up = w_up.shape
    e0 = layer * N_EXPERTS
    grid_spec = pltpu.PrefetchScalarGridSpec(
        num_scalar_prefetch=2,
        grid=(p // bm,),
        in_specs=[
            pl.BlockSpec((bm, d), lambda i, be, nu: (i, 0)),
            pl.BlockSpec((1, d, n_up), lambda i, be, nu: (e0 + be[i], 0, 0)),
            pl.BlockSpec((1, 1, n_up), lambda i, be, nu: (e0 + be[i], 0, 0)),
            pl.BlockSpec((1, EXPERT_DIM, d), lambda i, be, nu: (e0 + be[i], 0, 0)),
            pl.BlockSpec((1, 1, d), lambda i, be, nu: (e0 + be[i], 0, 0)),
        ],
        out_specs=pl.BlockSpec((bm, d), lambda i, be, nu: (i, 0)),
        scratch_shapes=[pltpu.VMEM((d, n_up), BF16), pltpu.VMEM((EXPERT_DIM, d), BF16)],
    )
    return pl.pallas_call(
        _expert_kernel,
        grid_spec=grid_spec,
        out_shape=jax.ShapeDtypeStruct((p, d), BF16),
        compiler_params=_cparams(("arbitrary",)),
        name="expert_ffn",
    )(block_e, n_used, xbuf, w_up, b_up, w_down, b_down)


def _combine_kernel(x1_ref, mod_ref, yg_ref, info_ref, lng_ref, lnb_ref, o_ref):
    d = x1_ref.shape[1]
    info = info_ref[...]
    y = jnp.zeros(x1_ref.shape, F32)
    for k in range(TOP_K):
        gate = info[:, 2 * TOP_K + k:2 * TOP_K + k + 1]
        y = y + gate * yg_ref[:, k * d:(k + 1) * d].astype(F32)
    g2 = mod_ref[0, 5:6, :]
    o_ref[...] = _layer_norm(DEEPNORM_ALPHA * x1_ref[...] + g2 * y, lng_ref[...], lnb_ref[...])


def _combine(x1, mod, yg, info, ln_g, ln_b, seq):
    t, d = x1.shape
    tm = ROW_TILE
    row = lambda i: (i, 0)
    fixed = lambda i: (0, 0)
    return pl.pallas_call(
        _combine_kernel,
        grid=(t // tm,),
        in_specs=[
            pl.BlockSpec((tm, d), row),
            pl.BlockSpec((1, 6, d), lambda i: ((i * tm) // seq, 0, 0)),
            pl.BlockSpec((tm, TOP_K * d), row),
            pl.BlockSpec((tm, LANES), row),
            pl.BlockSpec((1, d), fixed),
            pl.BlockSpec((1, d), fixed),
        ],
        out_specs=pl.BlockSpec((tm, d), row),
        out_shape=jax.ShapeDtypeStruct((t, d), F32),
        compiler_params=_cparams(("arbitrary",)),
        name="moe_combine_ln",
    )(x1, mod, yg, info, ln_g.reshape(1, d), ln_b.reshape(1, d))


def _lane_vector(vals, offset):
    return jnp.zeros((1, LANES), F32).at[0, offset:offset + vals.shape[0]].set(vals.astype(F32))


def _moe(h2, logits, x1, mod, ln_g, ln_b, w_up, b_up, w_down, b_down, layer, seq):
    t, d = h2.shape
    a = t * TOP_K
    bm = EXPERT_BLOCK
    info, cnt = _routing(logits)
    e_idx = info[:, 0:TOP_K].astype(jnp.int32)
    rank = info[:, TOP_K:2 * TOP_K].astype(jnp.int32)
    counts = cnt[0, :N_EXPERTS].astype(jnp.int32)
    padded = ((counts + bm - 1) // bm) * bm
    pcum = jnp.cumsum(padded)
    pstart = pcum - padded
    dest = (pstart[e_idx] + rank).reshape(a)
    n_blocks = -(-a // bm) + N_EXPERTS
    block_e = jnp.minimum(jnp.searchsorted(pcum, jnp.arange(n_blocks) * bm, side='right'),
                          N_EXPERTS - 1).astype(jnp.int32)
    n_used = (pcum[-1] // bm).astype(jnp.int32).reshape(1)
    src_tok = jnp.zeros((n_blocks * bm,), jnp.int32).at[dest].set(jnp.arange(a, dtype=jnp.int32) // TOP_K)
    xbuf = h2[src_tok]
    ybuf = _expert_ffn(xbuf, block_e, n_used, w_up, b_up, w_down, b_down, layer)
    yg = ybuf[dest].reshape(t, TOP_K * d)
    return _combine(x1, mod, yg, info, ln_g, ln_b, seq)


def kernel(x, c, rel_bias, w_in, w_out, w_ada, b_ada, ln1_g, ln1_b, ln2_g, ln2_b, pool_w, pool_scale,
           attn_sinks, conv_w, gdn_a_log, gdn_dt_bias, gdn_norm_w, router_w, router_b,
           exp_w_up, exp_b_up, exp_w_down, exp_b_down):
    bsz, seq, d = x.shape
    depth = w_in.shape[0]
    t = bsz * seq
    assert d == D_MODEL and w_in.shape[2] == IN_DIM
    assert seq % GDN_SUPER == 0 and seq % WINDOW == 0 and t % ROW_TILE == 0 and seq % ROW_TILE == 0

    mod_all = _modulation(c, w_ada, b_ada).reshape(depth, bsz, 6, d)
    bias = jnp.transpose(jnp.take(rel_bias.astype(F32), jnp.asarray(_BUCKETS), axis=0), (2, 0, 1))

    w_up_all = exp_w_up.reshape((depth * N_EXPERTS,) + exp_w_up.shape[2:])
    b_up_all = exp_b_up.reshape(depth * N_EXPERTS, 1, exp_b_up.shape[2])
    w_down_all = exp_w_down.reshape((depth * N_EXPERTS,) + exp_w_down.shape[2:])
    b_down_all = exp_b_down.reshape(depth * N_EXPERTS, 1, exp_b_down.shape[2])

    x2d = x.reshape(t, d)
    for l in range(depth):
        mod = mod_all[l]
        w_in_p = _take_cols(w_in[l], _IN_PERM).astype(BF16)
        w_out_p = jnp.take(w_out[l], jnp.asarray(_OUT_PERM), axis=0).astype(BF16)
        ident = jnp.zeros((CONV_WIDTH, 1), F32).at[CONV_WIDTH - 1, 0].set(1.0)
        conv_p = jnp.where(jnp.asarray(_GDN_CONV_SRC >= 0),
                           jnp.take(conv_w[l].astype(F32), jnp.asarray(np.maximum(_GDN_CONV_SRC, 0)), axis=1),
                           ident)
        pool_bd = jnp.zeros((POOL_DIM, POOL_DIM), F32)
        for gi in range(len(POOL_WINDOWS)):
            sl = slice(gi * POOL_GROUP, (gi + 1) * POOL_GROUP)
            pool_bd = pool_bd.at[sl, sl].set(pool_w[l, gi].astype(F32))
        alog_v = _lane_vector(gdn_a_log[l], N_GDN_HEADS)
        dtb_v = _lane_vector(gdn_dt_bias[l], N_GDN_HEADS)
        nw_v = jnp.tile(gdn_norm_w[l].astype(F32), 2).reshape(1, LANES)
        rw = jnp.zeros((d, LANES), F32).at[:, :N_EXPERTS].set(router_w[l].astype(F32))
        rw_hi, rw_lo = _split_bf16(rw)
        rb = jnp.full((1, LANES), NEG_INF, F32).at[0, :N_EXPERTS].set(router_b[l].astype(F32))

        u_pool, aq, akv, gdn, ba = _in_projection(x2d, mod, w_in_p, seq)
        y_pool = _pool_mixer(u_pool, pool_bd.astype(BF16), pool_scale[l].astype(F32), seq)
        y_att = _swa_attention(aq, akv, bias, attn_sinks[l].astype(F32), seq)
        y_gdn = _gdn_mixer(gdn, ba, conv_p, alog_v, dtb_v, nw_v, seq)
        x1, h2, logits = _out_projection(x2d, mod, y_pool, y_att, y_gdn, w_out_p, ln1_g[l], ln1_b[l],
                                         rw_hi, rw_lo, rb, seq)
        x2d = _moe(h2, logits, x1, mod, ln2_g[l], ln2_b[l], w_up_all, b_up_all, w_down_all, b_down_all,
                   l, seq)
    return x2d.reshape(bsz, seq, d)
```

```python
import functools

import numpy as np
import jax
import jax.numpy as jnp
from jax import lax
from jax.experimental import pallas as pl
from jax.experimental.pallas import tpu as pltpu
from jax.experimental.pallas import tpu_sc as plsc

F32 = jnp.float32
BF16 = jnp.bfloat16

D_MODEL = 1024
HEAD_DIM = 64
POOL_DIM = 256
POOL_WINDOWS = (2, 4, 8, 16)
POOL_GROUP = 64
N_ATT_HEADS = 6
N_KV_HEADS = 2
ATT_DIM = 384
KV_DIM = 128
WINDOW = 128
N_BUCKETS = 32
MAX_DISTANCE = 128
N_GDN_HEADS = 6
GDN_DIM = 384
CONV_WIDTH = 4
GDN_CHUNK = 64
N_EXPERTS = 32
TOP_K = 4
EXPERT_DIM = 1024
SWIGLU_ALPHA = 1.702
SWIGLU_LIMIT = 7.0
DEPTH = 2
DEEPNORM_ALPHA = (2 * DEPTH) ** 0.25
LN_EPS = 1e-5
NORM_EPS = 1e-6
NEG_INF = -1e30

LANES = 128
SUBLANES = 8
VMEM_LIMIT = 56 * 1024 * 1024

ROW_TILE = 512
GDN_SUPER = 256
ROUTE_TILE = 256
EXPERT_BLOCK = 256
SC_CHUNK = 64

_OFF_AQ = POOL_DIM
_OFF_AK = _OFF_AQ + ATT_DIM
_OFF_AV = _OFF_AK + KV_DIM
_OFF_GQ = _OFF_AV + KV_DIM
_OFF_GK = _OFF_GQ + GDN_DIM
_OFF_GV = _OFF_GK + GDN_DIM
_OFF_GZ = _OFF_GV + GDN_DIM
_OFF_GB = _OFF_GZ + GDN_DIM
_OFF_GA = _OFF_GB + N_GDN_HEADS
IN_DIM = _OFF_GA + N_GDN_HEADS

P_POOL = (0, POOL_DIM)
P_Q = (P_POOL[1], P_POOL[1] + ATT_DIM)
P_KV = (P_Q[1], P_Q[1] + 2 * KV_DIM)
P_GDN = (P_KV[1], P_KV[1] + 4 * GDN_DIM)
P_BA = (P_GDN[1], P_GDN[1] + LANES)
P_TOTAL = P_BA[1]


def _head_cols(off, h):
    return list(range(off + HEAD_DIM * h, off + HEAD_DIM * (h + 1)))


def _build_in_perm():
    cols = list(range(POOL_DIM))
    for p in range(N_ATT_HEADS // 2):
        cols += _head_cols(_OFF_AQ, p) + _head_cols(_OFF_AQ, p + 3)
    cols += list(range(_OFF_AK, _OFF_AK + 2 * KV_DIM))
    gdn_src = []
    for p in range(N_GDN_HEADS // 2):
        e, o = 2 * p, 2 * p + 1
        grp = (_head_cols(_OFF_GK, e) + _head_cols(_OFF_GQ, e)
               + _head_cols(_OFF_GQ, o) + _head_cols(_OFF_GK, o)
               + _head_cols(_OFF_GV, o) + _head_cols(_OFF_GV, e)
               + _head_cols(_OFF_GZ, o) + _head_cols(_OFF_GZ, e))
        cols += grp
        gdn_src += [c - _OFF_GQ if c < _OFF_GZ else -1 for c in grp]
    cols += list(range(_OFF_GB, _OFF_GB + 2 * N_GDN_HEADS))
    cols += [-1] * (LANES - 2 * N_GDN_HEADS)
    assert len(cols) == P_TOTAL
    return np.asarray(cols, np.int32), np.asarray(gdn_src, np.int32)


_IN_PERM, _GDN_CONV_SRC = _build_in_perm()


def _build_out_perm():
    rows = list(range(POOL_DIM))
    for p in range(N_ATT_HEADS // 2):
        rows += _head_cols(POOL_DIM, p) + _head_cols(POOL_DIM, p + 3)
    for p in range(N_GDN_HEADS // 2):
        rows += _head_cols(POOL_DIM + ATT_DIM, 2 * p + 1) + _head_cols(POOL_DIM + ATT_DIM, 2 * p)
    return np.asarray(rows, np.int32)


_OUT_PERM = _build_out_perm()


def _t5_bucket_table():
    qi = np.arange(WINDOW)[:, None]
    kj = np.arange(2 * WINDOW)[None, :]
    n = np.maximum(qi + WINDOW - kj, 0)
    max_exact = N_BUCKETS // 2
    nf = np.maximum(n, 1).astype(np.float32)
    large = max_exact + (np.log(nf / max_exact) / np.float32(np.log(MAX_DISTANCE / max_exact))
                         * (N_BUCKETS - max_exact)).astype(np.int32)
    large = np.minimum(large, N_BUCKETS - 1)
    return np.where(n < max_exact, n, large).astype(np.int32)


_BUCKETS = _t5_bucket_table()


def _take_cols(w, perm):
    safe = np.where(perm < 0, 0, perm)
    out = jnp.take(w, jnp.asarray(safe), axis=-1)
    if (perm < 0).any():
        out = jnp.where(jnp.asarray(perm >= 0), out, jnp.zeros((), w.dtype))
    return out


def _split_bf16(x):
    hi = x.astype(BF16)
    lo = (x - hi.astype(F32)).astype(BF16)
    return hi, lo


def _pack_bf16_pairs(x):
    n = x.shape[1] // 2
    bits = pltpu.bitcast(x.astype(BF16).astype(F32), jnp.int32)
    return lax.shift_right_logical(bits[:, :n], 16) | bits[:, n:]


def _unpack_bf16_pairs(u):
    lo = pltpu.bitcast(lax.shift_left(u, 16), F32)
    hi = pltpu.bitcast(u & jnp.int32(-65536), F32)
    return jnp.concatenate([lo, hi], axis=1)


def _dot(a, b):
    return jnp.dot(a, b, preferred_element_type=F32)


def _dot_nt(a, b):
    return lax.dot_general(a, b, (((1,), (1,)), ((), ())), preferred_element_type=F32)


def _dot_hi_exact_rhs(x, m_bf16):
    hi, lo = _split_bf16(x)
    return _dot(hi, m_bf16) + _dot(lo, m_bf16)


def _sigmoid(x):
    return 1.0 / (1.0 + jnp.exp(-x))


def _layer_norm(r, g, b):
    mu = jnp.mean(r, axis=-1, keepdims=True)
    d = r - mu
    var = jnp.mean(d * d, axis=-1, keepdims=True)
    return d * lax.rsqrt(var + LN_EPS) * g + b


def _cparams(sem):
    return pltpu.CompilerParams(dimension_semantics=sem, vmem_limit_bytes=VMEM_LIMIT)


def _mod_kernel(c_ref, w_ref, b_ref, o_ref):
    c = c_ref[...]
    ca = c * _sigmoid(c)
    ch, cl = _split_bf16(ca)
    wh, wl = _split_bf16(w_ref[0])
    o_ref[0] = _dot(ch, wh) + _dot(cl, wh) + _dot(ch, wl) + b_ref[0]


def _modulation(c, w_ada, b_ada):
    depth, d, n = w_ada.shape
    bsz = c.shape[0]
    tn = 512
    return pl.pallas_call(
        _mod_kernel,
        grid=(depth, n // tn),
        in_specs=[
            pl.BlockSpec((bsz, d), lambda l, j: (0, 0)),
            pl.BlockSpec((1, d, tn), lambda l, j: (l, 0, j)),
            pl.BlockSpec((1, 1, tn), lambda l, j: (l, 0, j)),
        ],
        out_specs=pl.BlockSpec((1, bsz, tn), lambda l, j: (l, 0, j)),
        out_shape=jax.ShapeDtypeStruct((depth, bsz, n), F32),
        compiler_params=_cparams(("arbitrary", "arbitrary")),
        name="adaln_mod",
    )(c, w_ada, b_ada.reshape(depth, 1, n))


def _inproj_kernel(x_ref, mod_ref, w_ref, pool_ref, q_ref, kv_ref, gdn_ref, ba_ref):
    sh = mod_ref[0, 0:1, :]
    sc = mod_ref[0, 1:2, :]
    h = (x_ref[...] * (1.0 + sc) + sh).astype(BF16)

    def mm(rng):
        return _dot(h, w_ref[:, rng[0]:rng[1]])

    pool_ref[...] = mm(P_POOL)
    q_ref[...] = mm(P_Q).astype(BF16)
    kv_ref[...] = mm(P_KV).astype(BF16)
    gdn_ref[...] = mm(P_GDN)
    ba_ref[...] = mm(P_BA)


def _in_projection(x2d, mod, w_in_p, seq):
    t, d = x2d.shape
    tm = ROW_TILE
    widths = [r[1] - r[0] for r in (P_POOL, P_Q, P_KV, P_GDN, P_BA)]
    dtypes = [F32, BF16, BF16, F32, F32]
    return pl.pallas_call(
        _inproj_kernel,
        grid=(t // tm,),
        in_specs=[
            pl.BlockSpec((tm, d), lambda i: (i, 0)),
            pl.BlockSpec((1, 6, d), lambda i: ((i * tm) // seq, 0, 0)),
            pl.BlockSpec((d, P_TOTAL), lambda i: (0, 0)),
        ],
        out_specs=[pl.BlockSpec((tm, w), lambda i: (i, 0)) for w in widths],
        out_shape=[jax.ShapeDtypeStruct((t, w), dt) for w, dt in zip(widths, dtypes)],
        compiler_params=_cparams(("arbitrary",)),
        name="in_proj",
    )(x2d, mod, w_in_p)


def _pool_kernel(u_ref, w_ref, scale_ref, o_ref):
    u = u_ref[...]
    row = lax.broadcasted_iota(jnp.int32, u.shape, 0)
    lane = lax.broadcasted_iota(jnp.int32, u.shape, 1)

    def shifted(a, s):
        return jnp.where(row >= s, pltpu.roll(a, s, axis=0), 0.0)

    sums = []
    acc = u
    for wdt in POOL_WINDOWS:
        acc = acc + shifted(acc, wdt // 2)
        sums.append(acc)
    grp = lane // POOL_GROUP
    wsum = sums[-1]
    win = jnp.full(u.shape, POOL_WINDOWS[-1], jnp.int32)
    for gi in range(len(POOL_WINDOWS) - 2, -1, -1):
        wsum = jnp.where(grp == gi, sums[gi], wsum)
        win = jnp.where(grp == gi, POOL_WINDOWS[gi], win)
    cnt = jnp.minimum(row + 1, win).astype(F32)
    p = wsum / cnt - u
    y = _dot(p.astype(BF16), w_ref[...]) * scale_ref[...]
    o_ref[...] = y.astype(BF16)


def _pool_mixer(u, pool_w_bd, pool_scale, seq):
    t, c = u.shape
    return pl.pallas_call(
        _pool_kernel,
        grid=(t // seq,),
        in_specs=[
            pl.BlockSpec((seq, c), lambda b: (b, 0)),
            pl.BlockSpec((c, c), lambda b: (0, 0)),
            pl.BlockSpec((1, c), lambda b: (0, 0)),
        ],
        out_specs=pl.BlockSpec((seq, c), lambda b: (b, 0)),
        out_shape=jax.ShapeDtypeStruct((t, c), BF16),
        compiler_params=_cparams(("arbitrary",)),
        name="pool_mixer",
    )(u, pool_w_bd, pool_scale.reshape(1, c))


def _attn_kernel(sink_ref, q_ref, kvc_ref, kvp_ref, bias_ref, o_ref):
    nb = pl.program_id(1)
    kv = jnp.concatenate([kvp_ref[...], kvc_ref[...]], axis=0)
    k = kv[:, :KV_DIM]
    v = kv[:, KV_DIM:]
    qi = lax.broadcasted_iota(jnp.int32, (WINDOW, 2 * WINDOW), 0)
    kj = lax.broadcasted_iota(jnp.int32, (WINDOW, 2 * WINDOW), 1)
    dist = qi + WINDOW - kj
    valid = (dist >= 0) & (dist < WINDOW) & ((kj >= WINDOW) | (nb > 0))
    lo = lax.broadcasted_iota(jnp.int32, (WINDOW, LANES), 1) < HEAD_DIM
    for p in range(N_ATT_HEADS // 2):
        qp = q_ref[:, p * LANES:(p + 1) * LANES]
        halves = []
        for half in range(2):
            h = p + 3 * half
            qm = jnp.where(lo if half == 0 else jnp.logical_not(lo), qp, jnp.zeros_like(qp))
            s = _dot_nt(qm, k) * (HEAD_DIM ** -0.5)
            s = jnp.where(valid, s + bias_ref[h], NEG_INF)
            sink = sink_ref[h]
            m = jnp.maximum(jnp.max(s, axis=-1, keepdims=True), sink)
            pr = jnp.exp(s - m)
            den = jnp.sum(pr, axis=-1, keepdims=True) + jnp.exp(sink - m)
            halves.append(_dot(pr.astype(BF16), v) / den)
        o_ref[:, p * LANES:(p + 1) * LANES] = jnp.where(lo, halves[0], halves[1]).astype(BF16)


def _swa_attention(q, kv, bias, sinks, seq):
    t = q.shape[0]
    nblk = seq // WINDOW
    return pl.pallas_call(
        _attn_kernel,
        grid=(t // seq, nblk),
        in_specs=[
            pl.BlockSpec(memory_space=pltpu.SMEM),
            pl.BlockSpec((WINDOW, ATT_DIM), lambda b, n: (b * nblk + n, 0)),
            pl.BlockSpec((WINDOW, 2 * KV_DIM), lambda b, n: (b * nblk + n, 0)),
            pl.BlockSpec((WINDOW, 2 * KV_DIM), lambda b, n: (b * nblk + jnp.maximum(n - 1, 0), 0)),
            pl.BlockSpec((N_ATT_HEADS, WINDOW, 2 * WINDOW), lambda b, n: (0, 0, 0)),
        ],
        out_specs=pl.BlockSpec((WINDOW, ATT_DIM), lambda b, n: (b * nblk + n, 0)),
        out_shape=jax.ShapeDtypeStruct((t, ATT_DIM), BF16),
        compiler_params=_cparams(("arbitrary", "arbitrary")),
        name="swa_attention",
    )(sinks, q, kv, kv, bias)


def _gdn_kernel(x_ref, halo_ref, ba_ref, cw_ref, alog_ref, dtb_ref, nw_ref, y_ref, state_ref):
    sc_id = pl.program_id(1)
    rows = GDN_SUPER
    nchunk = rows // GDN_CHUNK
    c_sz = GDN_CHUNK

    @pl.when(sc_id == 0)
    def _():
        state_ref[...] = jnp.zeros_like(state_ref)

    x = x_ref[...]
    halo = jnp.where(sc_id == 0, 0.0, halo_ref[...])
    row8 = lax.broadcasted_iota(jnp.int32, halo.shape, 0)
    acc = x * cw_ref[CONV_WIDTH - 1:CONV_WIDTH, :]
    for s in range(1, CONV_WIDTH):
        xr = pltpu.roll(x, s, axis=0)
        hr = pltpu.roll(halo, s, axis=0)
        top = jnp.where(row8 < s, hr, xr[:SUBLANES])
        xs = jnp.concatenate([top, xr[SUBLANES:]], axis=0)
        acc = acc + xs * cw_ref[CONV_WIDTH - 1 - s:CONV_WIDTH - s, :]
    act = acc * _sigmoid(acc)

    ri = lax.broadcasted_iota(jnp.int32, (rows, rows), 0)
    ci = lax.broadcasted_iota(jnp.int32, (rows, rows), 1)
    same_chunk = (ri // c_sz) == (ci // c_sz)
    incl = same_chunk & (ri >= ci)
    strict = same_chunk & (ri > ci)
    tri_incl = jnp.where(incl, 1.0, 0.0).astype(BF16)
    eye = jnp.where(ri == ci, 1.0, 0.0)
    blk = []
    bsz = SUBLANES
    while bsz <= c_sz:
        blk.append((ri // bsz) == (ci // bsz))
        bsz *= 2
    li = lax.broadcasted_iota(jnp.int32, (LANES, LANES), 0)
    lj = lax.broadcasted_iota(jnp.int32, (LANES, LANES), 1)
    half_ones = jnp.where((li // HEAD_DIM) == (lj // HEAD_DIM), 1.0, 0.0).astype(BF16)
    lane_lo = lax.broadcasted_iota(jnp.int32, (rows, LANES), 1) < HEAD_DIM
    lane_lo_c = lax.broadcasted_iota(jnp.int32, (c_sz, LANES), 1) < HEAD_DIM

    ba = ba_ref[...]
    beta_all = _sigmoid(ba)
    sp_in = ba + dtb_ref[...]
    softplus = jnp.maximum(sp_in, 0.0) + jnp.log(1.0 + jnp.exp(-jnp.abs(sp_in)))
    g_all = -jnp.exp(alog_ref[...]) * softplus
    gcum = _dot_hi_exact_rhs_lhs(tri_incl, g_all)
    gcum_t = gcum.T

    heads = range(N_GDN_HEADS)
    lane_hi = jnp.logical_not(lane_lo)
    lane_hi_c = jnp.logical_not(lane_lo_c)
    mk = [lane_lo if h % 2 == 0 else lane_hi for h in heads]
    mk_c = [lane_lo_c if h % 2 == 0 else lane_hi_c for h in heads]
    scale = HEAD_DIM ** -0.5

    def bdot(a, b):
        return _dot(a.astype(BF16), b.astype(BF16))

    xk, xq, gn, gc_col, beta, eg = [], [], [], [], [], []
    for h in heads:
        base = 4 * LANES * (h // 2) + LANES * (h % 2)
        g = act[:, base:base + LANES]
        g = g * lax.rsqrt(_dot_hi_exact_rhs(g * g, half_ones) + NORM_EPS)
        gn.append(g)
        xk.append(jnp.where(mk[h], g, 0.0))
        xq.append(jnp.where(mk[h], pltpu.roll(g, HEAD_DIM, axis=1), 0.0) * scale)
        beta.append(beta_all[:, h:h + 1])
        gc_col.append(gcum[:, N_GDN_HEADS + h:N_GDN_HEADS + h + 1])
        eg.append(jnp.exp(gc_col[h]))

    l_mat, attn, rhs = [], [], []
    for h in heads:
        gc_row = gcum_t[N_GDN_HEADS + h:N_GDN_HEADS + h + 1, :]
        decay = jnp.exp(jnp.where(incl, gc_col[h] - gc_row, -jnp.inf))
        xk_b = xk[h].astype(BF16)
        kk = _dot_nt((xk[h] * beta[h]).astype(BF16), xk_b)
        l_mat.append(jnp.where(strict, kk * decay, 0.0))
        attn.append((_dot_nt(xq[h].astype(BF16), xk_b) * decay).astype(BF16))
        vv = act[:, 4 * LANES * (h // 2) + 2 * LANES:4 * LANES * (h // 2) + 3 * LANES]
        rhs.append(jnp.where(mk[h], gn[h] * eg[h], vv) * beta[h])

    a1 = [jnp.where(blk[0], -l_mat[h], 0.0).astype(BF16) for h in heads]
    a2 = [_dot(a1[h], a1[h]).astype(BF16) for h in heads]
    a4 = [_dot(a2[h], a2[h]).astype(BF16) for h in heads]
    inv = [eye + a1[h].astype(F32) for h in heads]
    inv = [inv[h] + _dot(a2[h], inv[h].astype(BF16)) for h in heads]
    inv = [inv[h] + _dot(a4[h], inv[h].astype(BF16)) for h in heads]
    for lvl in range(1, len(blk) - 1):
        band = blk[lvl] & jnp.logical_not(blk[lvl - 1])
        inv_b = [inv[h].astype(BF16) for h in heads]
        mid = [_dot(jnp.where(band, l_mat[h], 0.0).astype(BF16), inv_b[h]) for h in heads]
        inv = [inv[h] - _dot(inv_b[h], mid[h].astype(BF16)) for h in heads]
    band = blk[-1] & jnp.logical_not(blk[-2])
    inv_b = [inv[h].astype(BF16) for h in heads]
    half = [_dot(inv_b[h], rhs[h].astype(BF16)) for h in heads]
    mid = [_dot(jnp.where(band, l_mat[h], 0.0).astype(BF16), half[h].astype(BF16)) for h in heads]
    sol = [half[h] - _dot(inv_b[h], mid[h].astype(BF16)) for h in heads]

    st = [state_ref[h] for h in heads]
    vn_parts = [[] for _ in heads]
    qs_parts = [[] for _ in heads]
    for c in range(nchunk):
        r0 = c * c_sz
        for h in heads:
            sol_c = sol[h][r0:r0 + c_sz]
            glast = gcum[r0 + c_sz - 1:r0 + c_sz, N_GDN_HEADS + h:N_GDN_HEADS + h + 1]
            kd_t = (xk[h][r0:r0 + c_sz] * jnp.exp(glast - gc_col[h][r0:r0 + c_sz])).T
            qd = xq[h][r0:r0 + c_sz] * eg[h][r0:r0 + c_sz]
            lhs = jnp.concatenate([jnp.where(mk_c[h], sol_c, 0.0), qd], axis=0)
            m1 = bdot(lhs, st[h])
            vn = sol_c - m1[:c_sz]
            vn_parts[h].append(vn)
            qs_parts[h].append(m1[c_sz:])
            st[h] = st[h] * jnp.exp(glast) + bdot(kd_t, jnp.where(mk_c[h], 0.0, vn))
    for h in heads:
        state_ref[h] = st[h]

    for p in range(N_GDN_HEADS // 2):
        o_pair = []
        for h in (2 * p, 2 * p + 1):
            vn_all = jnp.concatenate(vn_parts[h], axis=0)
            o_pair.append(jnp.concatenate(qs_parts[h], axis=0) + _dot(attn[h], vn_all.astype(BF16)))
        o = jnp.where(lane_lo, o_pair[1], o_pair[0])
        ms = _dot_hi_exact_rhs(o * o, half_ones) * (1.0 / HEAD_DIM)
        zz = act[:, 4 * LANES * p + 3 * LANES:4 * LANES * (p + 1)]
        y = o * lax.rsqrt(ms + NORM_EPS) * nw_ref[...] * zz
        y_ref[:, p * LANES:(p + 1) * LANES] = y.astype(BF16)


def _dot_hi_exact_rhs_lhs(m_bf16, x):
    hi, lo = _split_bf16(x)
    return _dot(m_bf16, hi) + _dot(m_bf16, lo)


def _gdn_mixer(gdn, ba, conv_p, alog_v, dtb_v, nw_v, seq):
    t, c = gdn.shape
    rows = GDN_SUPER
    nsc = seq // rows
    hb = rows // SUBLANES
    return pl.pallas_call(
        _gdn_kernel,
        grid=(t // seq, nsc),
        in_specs=[
            pl.BlockSpec((rows, c), lambda b, s: (b * nsc + s, 0)),
            pl.BlockSpec((SUBLANES, c), lambda b, s: (jnp.maximum((b * nsc + s) * hb - 1, 0), 0)),
            pl.BlockSpec((rows, LANES), lambda b, s: (b * nsc + s, 0)),
            pl.BlockSpec((CONV_WIDTH, c), lambda b, s: (0, 0)),
            pl.BlockSpec((1, LANES), lambda b, s: (0, 0)),
            pl.BlockSpec((1, LANES), lambda b, s: (0, 0)),
            pl.BlockSpec((1, LANES), lambda b, s: (0, 0)),
        ],
        out_specs=pl.BlockSpec((rows, GDN_DIM), lambda b, s: (b * nsc + s, 0)),
        out_shape=jax.ShapeDtypeStruct((t, GDN_DIM), BF16),
        scratch_shapes=[pltpu.VMEM((N_GDN_HEADS, LANES, LANES), F32)],
        compiler_params=_cparams(("arbitrary", "arbitrary")),
        name="gdn_mixer",
    )(gdn, gdn, ba, conv_p, alog_v, dtb_v, nw_v)


def _outproj_kernel(x_ref, mod_ref, yp_ref, ya_ref, yg_ref, wp_ref, wa_ref, wg_ref, lng_ref, lnb_ref,
                    rwh_ref, rwl_ref, rb_ref, x1_ref, h2_ref, logit_ref):
    y = _dot(yp_ref[...], wp_ref[...]) + _dot(ya_ref[...], wa_ref[...]) + _dot(yg_ref[...], wg_ref[...])
    g1 = mod_ref[0, 2:3, :]
    sh2 = mod_ref[0, 3:4, :]
    sc2 = mod_ref[0, 4:5, :]
    x1 = _layer_norm(DEEPNORM_ALPHA * x_ref[...] + g1 * y, lng_ref[...], lnb_ref[...])
    x1_ref[...] = x1
    h2 = x1 * (1.0 + sc2) + sh2
    hh, hl = _split_bf16(h2)
    h2_ref[...] = _pack_bf16_pairs(h2)
    logit_ref[...] = (_dot(hh, rwh_ref[...]) + _dot(hl, rwh_ref[...]) + _dot(hh, rwl_ref[...])
                      + rb_ref[...])


def _out_projection(x2d, mod, yp, ya, yg, w_out_p, ln_g, ln_b, rw_hi, rw_lo, rb, seq):
    t, d = x2d.shape
    tm = ROW_TILE
    wp = w_out_p[:POOL_DIM]
    wa = w_out_p[POOL_DIM:POOL_DIM + ATT_DIM]
    wg = w_out_p[POOL_DIM + ATT_DIM:]
    row = lambda i: (i, 0)
    fixed = lambda i: (0, 0)
    return pl.pallas_call(
        _outproj_kernel,
        grid=(t // tm,),
        in_specs=[
            pl.BlockSpec((tm, d), row),
            pl.BlockSpec((1, 6, d), lambda i: ((i * tm) // seq, 0, 0)),
            pl.BlockSpec((tm, POOL_DIM), row),
            pl.BlockSpec((tm, ATT_DIM), row),
            pl.BlockSpec((tm, GDN_DIM), row),
            pl.BlockSpec((POOL_DIM, d), fixed),
            pl.BlockSpec((ATT_DIM, d), fixed),
            pl.BlockSpec((GDN_DIM, d), fixed),
            pl.BlockSpec((1, d), fixed),
            pl.BlockSpec((1, d), fixed),
            pl.BlockSpec((d, LANES), fixed),
            pl.BlockSpec((d, LANES), fixed),
            pl.BlockSpec((1, LANES), fixed),
        ],
        out_specs=[pl.BlockSpec((tm, d), row), pl.BlockSpec((tm, d // 2), row), pl.BlockSpec((tm, LANES), row)],
        out_shape=[jax.ShapeDtypeStruct((t, d), F32), jax.ShapeDtypeStruct((t, d // 2), jnp.int32),
                   jax.ShapeDtypeStruct((t, LANES), F32)],
        compiler_params=_cparams(("arbitrary",)),
        name="out_proj_ln",
    )(x2d, mod, yp, ya, yg, wp, wa, wg, ln_g.reshape(1, d), ln_b.reshape(1, d), rw_hi, rw_lo, rb)


def _route_kernel(logit_ref, info_ref, cnt_ref, carry_ref):
    i = pl.program_id(0)

    @pl.when(i == 0)
    def _():
        carry_ref[...] = jnp.zeros_like(carry_ref)

    lg = logit_ref[...]
    tm = lg.shape[0]
    lane = lax.broadcasted_iota(jnp.int32, lg.shape, 1).astype(F32)
    work = lg
    vals, idxs = [], []
    for _k in range(TOP_K):
        m = jnp.max(work, axis=-1, keepdims=True)
        idx = jnp.min(jnp.where(work == m, lane, float(LANES)), axis=-1, keepdims=True)
        vals.append(m)
        idxs.append(idx)
        work = jnp.where(lane == idx, -jnp.inf, work)
    exps = [jnp.exp(v - vals[0]) for v in vals]
    den = exps[0] + exps[1] + exps[2] + exps[3]
    onehots = [lane == idx for idx in idxs]
    member = jnp.zeros(lg.shape, F32)
    for oh in onehots:
        member = member + jnp.where(oh, 1.0, 0.0)
    ri = lax.broadcasted_iota(jnp.int32, (tm, tm), 0)
    ci = lax.broadcasted_iota(jnp.int32, (tm, tm), 1)
    before = jnp.where(ri > ci, 1.0, 0.0).astype(BF16)
    rank = _dot(before, member.astype(BF16)) + carry_ref[...]
    carry_ref[...] = carry_ref[...] + jnp.sum(member, axis=0, keepdims=True)
    info = jnp.zeros(lg.shape, F32)
    for k in range(TOP_K):
        rank_k = jnp.sum(jnp.where(onehots[k], rank, 0.0), axis=-1, keepdims=True)
        info = jnp.where(lane == float(k), idxs[k], info)
        info = jnp.where(lane == float(TOP_K + k), rank_k, info)
        info = jnp.where(lane == float(2 * TOP_K + k), exps[k] / den, info)
    info_ref[...] = info
    cnt_ref[...] = carry_ref[...]


def _routing(logits):
    t = logits.shape[0]
    tm = ROUTE_TILE
    return pl.pallas_call(
        _route_kernel,
        grid=(t // tm,),
        in_specs=[pl.BlockSpec((tm, LANES), lambda i: (i, 0))],
        out_specs=[pl.BlockSpec((tm, LANES), lambda i: (i, 0)), pl.BlockSpec((1, LANES), lambda i: (0, 0))],
        out_shape=[jax.ShapeDtypeStruct((t, LANES), F32), jax.ShapeDtypeStruct((1, LANES), F32)],
        scratch_shapes=[pltpu.VMEM((1, LANES), F32)],
        compiler_params=_cparams(("arbitrary",)),
        name="moe_route",
    )(logits)


def _expert_kernel(be_ref, nu_ref, x_ref, wup_ref, bup_ref, wdn_ref, bdn_ref, y_ref, wup_bf, wdn_bf):
    i = pl.program_id(0)
    e = be_ref[i]
    prev = be_ref[jnp.maximum(i - 1, 0)]

    @pl.when((i == 0) | (e != prev))
    def _():
        wup_bf[...] = wup_ref[0].astype(BF16)
        wdn_bf[...] = wdn_ref[0].astype(BF16)

    @pl.when(i < nu_ref[0])
    def _():
        xb = _unpack_bf16_pairs(x_ref[...]).astype(BF16)
        hb = _dot(xb, wup_bf[...]) + bup_ref[0]
        x_glu = jnp.minimum(hb[:, :EXPERT_DIM], SWIGLU_LIMIT)
        x_lin = jnp.clip(hb[:, EXPERT_DIM:], -SWIGLU_LIMIT, SWIGLU_LIMIT)
        act = x_glu * _sigmoid(SWIGLU_ALPHA * x_glu) * (x_lin + 1.0)
        y = _dot(act.astype(BF16), wdn_bf[...]) + bdn_ref[0]
        y_ref[...] = _pack_bf16_pairs(y)

    @pl.when(i >= nu_ref[0])
    def _():
        y_ref[...] = jnp.zeros_like(y_ref)


def _expert_ffn(xbuf, block_e, n_used, w_up, b_up, w_down, b_down, layer):
    p, dh = xbuf.shape
    d = 2 * dh
    bm = EXPERT_BLOCK
    ne, _, n_up = w_up.shape
    e0 = layer * N_EXPERTS
    grid_spec = pltpu.PrefetchScalarGridSpec(
        num_scalar_prefetch=2,
        grid=(p // bm,),
        in_specs=[
            pl.BlockSpec((bm, dh), lambda i, be, nu: (i, 0)),
            pl.BlockSpec((1, d, n_up), lambda i, be, nu: (e0 + be[i], 0, 0)),
            pl.BlockSpec((1, 1, n_up), lambda i, be, nu: (e0 + be[i], 0, 0)),
            pl.BlockSpec((1, EXPERT_DIM, d), lambda i, be, nu: (e0 + be[i], 0, 0)),
            pl.BlockSpec((1, 1, d), lambda i, be, nu: (e0 + be[i], 0, 0)),
        ],
        out_specs=pl.BlockSpec((bm, dh), lambda i, be, nu: (i, 0)),
        scratch_shapes=[pltpu.VMEM((d, n_up), BF16), pltpu.VMEM((EXPERT_DIM, d), BF16)],
    )
    return pl.pallas_call(
        _expert_kernel,
        grid_spec=grid_spec,
        out_shape=jax.ShapeDtypeStruct((p, dh), jnp.int32),
        compiler_params=_cparams(("arbitrary",)),
        name="expert_ffn",
    )(block_e, n_used, xbuf, w_up, b_up, w_down, b_down)


def _combine_kernel(x1_ref, mod_ref, yg_ref, info_ref, lng_ref, lnb_ref, o_ref):
    info = info_ref[...]
    y = jnp.zeros(x1_ref.shape, F32)
    for k in range(TOP_K):
        gate = info[:, 2 * TOP_K + k:2 * TOP_K + k + 1]
        y = y + gate * _unpack_bf16_pairs(yg_ref[k])
    g2 = mod_ref[0, 5:6, :]
    o_ref[...] = _layer_norm(DEEPNORM_ALPHA * x1_ref[...] + g2 * y, lng_ref[...], lnb_ref[...])


def _combine(x1, mod, yg, info, ln_g, ln_b, seq):
    t, d = x1.shape
    tm = ROW_TILE
    row = lambda i: (i, 0)
    fixed = lambda i: (0, 0)
    return pl.pallas_call(
        _combine_kernel,
        grid=(t // tm,),
        in_specs=[
            pl.BlockSpec((tm, d), row),
            pl.BlockSpec((1, 6, d), lambda i: ((i * tm) // seq, 0, 0)),
            pl.BlockSpec((TOP_K, tm, d // 2), lambda i: (0, i, 0)),
            pl.BlockSpec((tm, LANES), row),
            pl.BlockSpec((1, d), fixed),
            pl.BlockSpec((1, d), fixed),
        ],
        out_specs=pl.BlockSpec((tm, d), row),
        out_shape=jax.ShapeDtypeStruct((t, d), F32),
        compiler_params=_cparams(("arbitrary",)),
        name="moe_combine_ln",
    )(x1, mod, yg, info, ln_g.reshape(1, d), ln_b.reshape(1, d))


def _sc_workers():
    info = plsc.get_sparse_core_info()
    return info.num_cores, info.num_cores * info.num_subcores


def _sc_scatter_rows(rows, idx, n_out):
    t, w = rows.shape
    kk = idx.shape[0]
    n_cores, n_workers = _sc_workers()
    ch = SC_CHUNK
    assert t % (n_workers * ch) == 0
    n_chunk = t // (n_workers * ch)
    idx_c = jnp.transpose(idx.reshape(kk, t // ch, ch), (1, 0, 2))

    @functools.partial(
        pl.kernel,
        mesh=plsc.VectorSubcoreMesh(core_axis_name="c", subcore_axis_name="s"),
        out_type=jax.ShapeDtypeStruct((n_out, w), rows.dtype),
        scratch_types=[pltpu.VMEM((kk, ch), jnp.int32), pltpu.VMEM((ch, w), rows.dtype)],
        name="sc_dispatch_scatter",
    )
    def scatter_kernel(rows_hbm, idx_hbm, out_hbm, idx_v, rows_v):
        wid = lax.axis_index("s") * n_cores + lax.axis_index("c")

        @pl.loop(0, n_chunk)
        def _(j):
            cidx = wid * n_chunk + j
            pltpu.sync_copy(idx_hbm.at[cidx], idx_v)
            pltpu.sync_copy(rows_hbm.at[pl.ds(cidx * ch, ch)], rows_v)
            for q in range(kk):
                pltpu.sync_copy(rows_v, out_hbm.at[idx_v.at[q]])

    return scatter_kernel(rows, idx_c)


def _sc_gather_rows(table, idx):
    m = idx.shape[0]
    w = table.shape[1]
    n_cores, n_workers = _sc_workers()
    ch = SC_CHUNK
    assert m % (n_workers * ch) == 0
    n_chunk = m // (n_workers * ch)
    idx_c = idx.reshape(m // ch, 1, ch)

    @functools.partial(
        pl.kernel,
        mesh=plsc.VectorSubcoreMesh(core_axis_name="c", subcore_axis_name="s"),
        out_type=jax.ShapeDtypeStruct((m, w), table.dtype),
        scratch_types=[pltpu.VMEM((1, ch), jnp.int32), pltpu.VMEM((ch, w), table.dtype)],
        name="sc_combine_gather",
    )
    def gather_kernel(table_hbm, idx_hbm, out_hbm, idx_v, rows_v):
        wid = lax.axis_index("s") * n_cores + lax.axis_index("c")

        @pl.loop(0, n_chunk)
        def _(j):
            cidx = wid * n_chunk + j
            pltpu.sync_copy(idx_hbm.at[cidx], idx_v)
            pltpu.sync_copy(table_hbm.at[idx_v.at[0]], rows_v)
            pltpu.sync_copy(rows_v, out_hbm.at[pl.ds(cidx * ch, ch)])

    return gather_kernel(table, idx_c)


def _lane_vector(vals, offset):
    return jnp.zeros((1, LANES), F32).at[0, offset:offset + vals.shape[0]].set(vals.astype(F32))


def _moe(h2, logits, x1, mod, ln_g, ln_b, w_up, b_up, w_down, b_down, layer, seq):
    t, dh = h2.shape
    a = t * TOP_K
    bm = EXPERT_BLOCK
    info, cnt = _routing(logits)
    e_idx = info[:, 0:TOP_K].astype(jnp.int32)
    rank = info[:, TOP_K:2 * TOP_K].astype(jnp.int32)
    counts = cnt[0, :N_EXPERTS].astype(jnp.int32)
    padded = ((counts + bm - 1) // bm) * bm
    pcum = jnp.cumsum(padded)
    pstart = pcum - padded
    dest = jnp.transpose(pstart[e_idx] + rank)
    n_blocks = -(-a // bm) + N_EXPERTS
    block_e = jnp.minimum(jnp.sum(pcum[None, :] <= (jnp.arange(n_blocks) * bm)[:, None], axis=1),
                          N_EXPERTS - 1).astype(jnp.int32)
    n_used = (pcum[-1] // bm).astype(jnp.int32).reshape(1)
    xbuf = _sc_scatter_rows(h2, dest, n_blocks * bm)
    ybuf = _expert_ffn(xbuf, block_e, n_used, w_up, b_up, w_down, b_down, layer)
    yg = _sc_gather_rows(ybuf, dest.reshape(a)).reshape(TOP_K, t, dh)
    return _combine(x1, mod, yg, info, ln_g, ln_b, seq)


def kernel(x, c, rel_bias, w_in, w_out, w_ada, b_ada, ln1_g, ln1_b, ln2_g, ln2_b, pool_w, pool_scale,
           attn_sinks, conv_w, gdn_a_log, gdn_dt_bias, gdn_norm_w, router_w, router_b,
           exp_w_up, exp_b_up, exp_w_down, exp_b_down):
    bsz, seq, d = x.shape
    depth = w_in.shape[0]
    t = bsz * seq
    assert d == D_MODEL and w_in.shape[2] == IN_DIM
    assert seq % GDN_SUPER == 0 and seq % WINDOW == 0 and t % ROW_TILE == 0 and seq % ROW_TILE == 0

    mod_all = _modulation(c, w_ada, b_ada).reshape(depth, bsz, 6, d)
    bias = jnp.transpose(jnp.take(rel_bias.astype(F32), jnp.asarray(_BUCKETS), axis=0), (2, 0, 1))

    w_up_all = exp_w_up.reshape((depth * N_EXPERTS,) + exp_w_up.shape[2:])
    b_up_all = exp_b_up.reshape(depth * N_EXPERTS, 1, exp_b_up.shape[2])
    w_down_all = exp_w_down.reshape((depth * N_EXPERTS,) + exp_w_down.shape[2:])
    b_down_all = exp_b_down.reshape(depth * N_EXPERTS, 1, exp_b_down.shape[2])

    x2d = x.reshape(t, d)
    for l in range(depth):
        mod = mod_all[l]
        w_in_p = _take_cols(w_in[l], _IN_PERM).astype(BF16)
        w_out_p = jnp.take(w_out[l], jnp.asarray(_OUT_PERM), axis=0).astype(BF16)
        ident = jnp.zeros((CONV_WIDTH, 1), F32).at[CONV_WIDTH - 1, 0].set(1.0)
        conv_p = jnp.where(jnp.asarray(_GDN_CONV_SRC >= 0),
                           jnp.take(conv_w[l].astype(F32), jnp.asarray(np.maximum(_GDN_CONV_SRC, 0)), axis=1),
                           ident)
        pool_bd = jnp.zeros((POOL_DIM, POOL_DIM), F32)
        for gi in range(len(POOL_WINDOWS)):
            sl = slice(gi * POOL_GROUP, (gi + 1) * POOL_GROUP)
            pool_bd = pool_bd.at[sl, sl].set(pool_w[l, gi].astype(F32))
        alog_v = _lane_vector(gdn_a_log[l], N_GDN_HEADS)
        dtb_v = _lane_vector(gdn_dt_bias[l], N_GDN_HEADS)
        nw_v = jnp.tile(gdn_norm_w[l].astype(F32), 2).reshape(1, LANES)
        rw = jnp.zeros((d, LANES), F32).at[:, :N_EXPERTS].set(router_w[l].astype(F32))
        rw_hi, rw_lo = _split_bf16(rw)
        rb = jnp.full((1, LANES), NEG_INF, F32).at[0, :N_EXPERTS].set(router_b[l].astype(F32))

        u_pool, aq, akv, gdn, ba = _in_projection(x2d, mod, w_in_p, seq)
        y_pool = _pool_mixer(u_pool, pool_bd.astype(BF16), pool_scale[l].astype(F32), seq)
        y_att = _swa_attention(aq, akv, bias, attn_sinks[l].astype(F32), seq)
        y_gdn = _gdn_mixer(gdn, ba, conv_p, alog_v, dtb_v, nw_v, seq)
        x1, h2, logits = _out_projection(x2d, mod, y_pool, y_att, y_gdn, w_out_p, ln1_g[l], ln1_b[l],
                                         rw_hi, rw_lo, rb, seq)
        x2d = _moe(h2, logits, x1, mod, ln2_g[l], ln2_b[l], w_up_all, b_up_all, w_down_all, b_down_all,
                   l, seq)
    return x2d.reshape(bsz, seq, d)
```

```python
import functools

import numpy as np
import jax
import jax.numpy as jnp
from jax import lax
from jax.experimental import pallas as pl
from jax.experimental.pallas import tpu as pltpu
from jax.experimental.pallas import tpu_sc as plsc

F32 = jnp.float32
BF16 = jnp.bfloat16

D_MODEL = 1024
HEAD_DIM = 64
POOL_DIM = 256
POOL_WINDOWS = (2, 4, 8, 16)
POOL_GROUP = 64
N_ATT_HEADS = 6
N_KV_HEADS = 2
ATT_DIM = 384
KV_DIM = 128
WINDOW = 128
N_BUCKETS = 32
MAX_DISTANCE = 128
N_GDN_HEADS = 6
GDN_DIM = 384
CONV_WIDTH = 4
GDN_CHUNK = 64
N_EXPERTS = 32
TOP_K = 4
EXPERT_DIM = 1024
SWIGLU_ALPHA = 1.702
SWIGLU_LIMIT = 7.0
DEPTH = 2
DEEPNORM_ALPHA = (2 * DEPTH) ** 0.25
LN_EPS = 1e-5
NORM_EPS = 1e-6
NEG_INF = -1e30

LANES = 128
SUBLANES = 8
VMEM_LIMIT = 56 * 1024 * 1024

ROW_TILE = 512
GDN_SUPER = 256
ROUTE_TILE = 256
EXPERT_BLOCK = 256
SC_CHUNK = 64

_OFF_AQ = POOL_DIM
_OFF_AK = _OFF_AQ + ATT_DIM
_OFF_AV = _OFF_AK + KV_DIM
_OFF_GQ = _OFF_AV + KV_DIM
_OFF_GK = _OFF_GQ + GDN_DIM
_OFF_GV = _OFF_GK + GDN_DIM
_OFF_GZ = _OFF_GV + GDN_DIM
_OFF_GB = _OFF_GZ + GDN_DIM
_OFF_GA = _OFF_GB + N_GDN_HEADS
IN_DIM = _OFF_GA + N_GDN_HEADS

P_POOL = (0, POOL_DIM)
P_Q = (P_POOL[1], P_POOL[1] + ATT_DIM)
P_KV = (P_Q[1], P_Q[1] + 2 * KV_DIM)
P_GDN = (P_KV[1], P_KV[1] + 4 * GDN_DIM)
P_BA = (P_GDN[1], P_GDN[1] + LANES)
P_TOTAL = P_BA[1]


def _head_cols(off, h):
    return list(range(off + HEAD_DIM * h, off + HEAD_DIM * (h + 1)))


def _build_in_perm():
    cols = list(range(POOL_DIM))
    for p in range(N_ATT_HEADS // 2):
        cols += _head_cols(_OFF_AQ, p) + _head_cols(_OFF_AQ, p + 3)
    cols += list(range(_OFF_AK, _OFF_AK + 2 * KV_DIM))
    gdn_src = []
    for p in range(N_GDN_HEADS // 2):
        e, o = 2 * p, 2 * p + 1
        grp = (_head_cols(_OFF_GK, e) + _head_cols(_OFF_GQ, e)
               + _head_cols(_OFF_GQ, o) + _head_cols(_OFF_GK, o)
               + _head_cols(_OFF_GV, o) + _head_cols(_OFF_GV, e)
               + _head_cols(_OFF_GZ, o) + _head_cols(_OFF_GZ, e))
        cols += grp
        gdn_src += [c - _OFF_GQ if c < _OFF_GZ else -1 for c in grp]
    cols += list(range(_OFF_GB, _OFF_GB + 2 * N_GDN_HEADS))
    cols += [-1] * (LANES - 2 * N_GDN_HEADS)
    assert len(cols) == P_TOTAL
    return np.asarray(cols, np.int32), np.asarray(gdn_src, np.int32)


_IN_PERM, _GDN_CONV_SRC = _build_in_perm()


def _build_out_perm():
    rows = list(range(POOL_DIM))
    for p in range(N_ATT_HEADS // 2):
        rows += _head_cols(POOL_DIM, p) + _head_cols(POOL_DIM, p + 3)
    for p in range(N_GDN_HEADS // 2):
        rows += _head_cols(POOL_DIM + ATT_DIM, 2 * p + 1) + _head_cols(POOL_DIM + ATT_DIM, 2 * p)
    return np.asarray(rows, np.int32)


_OUT_PERM = _build_out_perm()


def _t5_bucket_table():
    qi = np.arange(WINDOW)[:, None]
    kj = np.arange(2 * WINDOW)[None, :]
    n = np.maximum(qi + WINDOW - kj, 0)
    max_exact = N_BUCKETS // 2
    nf = np.maximum(n, 1).astype(np.float32)
    large = max_exact + (np.log(nf / max_exact) / np.float32(np.log(MAX_DISTANCE / max_exact))
                         * (N_BUCKETS - max_exact)).astype(np.int32)
    large = np.minimum(large, N_BUCKETS - 1)
    return np.where(n < max_exact, n, large).astype(np.int32)


_BUCKETS = _t5_bucket_table()


def _take_static(w, perm, axis):
    parts = []
    start = 0
    for i in range(1, len(perm) + 1):
        run_ends = (i == len(perm) or ((perm[i] < 0) != (perm[i - 1] < 0))
                    or (perm[i] >= 0 and perm[i] != perm[i - 1] + 1))
        if run_ends:
            if perm[start] < 0:
                shape = list(w.shape)
                shape[axis] = i - start
                parts.append(jnp.zeros(shape, w.dtype))
            else:
                parts.append(lax.slice_in_dim(w, int(perm[start]), int(perm[start]) + (i - start), axis=axis))
            start = i
    return jnp.concatenate(parts, axis=axis)


def _take_cols(w, perm):
    return _take_static(w, perm, w.ndim - 1)


def _split_bf16(x):
    hi = x.astype(BF16)
    lo = (x - hi.astype(F32)).astype(BF16)
    return hi, lo


def _pack_bf16_pairs(x):
    n = x.shape[1] // 2
    bits = pltpu.bitcast(x.astype(BF16).astype(F32), jnp.int32)
    return lax.shift_right_logical(bits[:, :n], 16) | bits[:, n:]


def _unpack_bf16_pairs(u):
    lo = pltpu.bitcast(lax.shift_left(u, 16), F32)
    hi = pltpu.bitcast(u & jnp.int32(-65536), F32)
    return jnp.concatenate([lo, hi], axis=1)


def _dot(a, b):
    return jnp.dot(a, b, preferred_element_type=F32)


def _dot_nt(a, b):
    return lax.dot_general(a, b, (((1,), (1,)), ((), ())), preferred_element_type=F32)


def _dot_hi_exact_rhs(x, m_bf16):
    hi, lo = _split_bf16(x)
    return _dot(hi, m_bf16) + _dot(lo, m_bf16)


def _sigmoid(x):
    return 1.0 / (1.0 + jnp.exp(-x))


def _layer_norm(r, g, b):
    mu = jnp.mean(r, axis=-1, keepdims=True)
    d = r - mu
    var = jnp.mean(d * d, axis=-1, keepdims=True)
    return d * lax.rsqrt(var + LN_EPS) * g + b


def _cparams(sem):
    return pltpu.CompilerParams(dimension_semantics=sem, vmem_limit_bytes=VMEM_LIMIT)


def _mod_kernel(c_ref, w_ref, b_ref, o_ref):
    c = c_ref[...]
    ca = c * _sigmoid(c)
    ch, cl = _split_bf16(ca)
    wh, wl = _split_bf16(w_ref[0])
    o_ref[0] = _dot(ch, wh) + _dot(cl, wh) + _dot(ch, wl) + b_ref[0]


def _modulation(c, w_ada, b_ada):
    depth, d, n = w_ada.shape
    bsz = c.shape[0]
    tn = 512
    return pl.pallas_call(
        _mod_kernel,
        grid=(depth, n // tn),
        in_specs=[
            pl.BlockSpec((bsz, d), lambda l, j: (0, 0)),
            pl.BlockSpec((1, d, tn), lambda l, j: (l, 0, j)),
            pl.BlockSpec((1, 1, tn), lambda l, j: (l, 0, j)),
        ],
        out_specs=pl.BlockSpec((1, bsz, tn), lambda l, j: (l, 0, j)),
        out_shape=jax.ShapeDtypeStruct((depth, bsz, n), F32),
        compiler_params=_cparams(("arbitrary", "arbitrary")),
        name="adaln_mod",
    )(c, w_ada, b_ada.reshape(depth, 1, n))


def _inproj_kernel(x_ref, mod_ref, w_ref, pool_ref, q_ref, kv_ref, gdn_ref, ba_ref):
    sh = mod_ref[0, 0:1, :]
    sc = mod_ref[0, 1:2, :]
    h = (x_ref[...] * (1.0 + sc) + sh).astype(BF16)

    def mm(rng):
        return _dot(h, w_ref[:, rng[0]:rng[1]])

    pool_ref[...] = mm(P_POOL)
    q_ref[...] = mm(P_Q).astype(BF16)
    kv_ref[...] = mm(P_KV).astype(BF16)
    gdn_ref[...] = mm(P_GDN)
    ba_ref[...] = mm(P_BA)


def _in_projection(x2d, mod, w_in_p, seq):
    t, d = x2d.shape
    tm = ROW_TILE
    widths = [r[1] - r[0] for r in (P_POOL, P_Q, P_KV, P_GDN, P_BA)]
    dtypes = [F32, BF16, BF16, F32, F32]
    return pl.pallas_call(
        _inproj_kernel,
        grid=(t // tm,),
        in_specs=[
            pl.BlockSpec((tm, d), lambda i: (i, 0)),
            pl.BlockSpec((1, 6, d), lambda i: ((i * tm) // seq, 0, 0)),
            pl.BlockSpec((d, P_TOTAL), lambda i: (0, 0)),
        ],
        out_specs=[pl.BlockSpec((tm, w), lambda i: (i, 0)) for w in widths],
        out_shape=[jax.ShapeDtypeStruct((t, w), dt) for w, dt in zip(widths, dtypes)],
        compiler_params=_cparams(("arbitrary",)),
        name="in_proj",
    )(x2d, mod, w_in_p)


def _pool_kernel(u_ref, w_ref, scale_ref, o_ref):
    u = u_ref[...]
    row = lax.broadcasted_iota(jnp.int32, u.shape, 0)
    lane = lax.broadcasted_iota(jnp.int32, u.shape, 1)

    def shifted(a, s):
        return jnp.where(row >= s, pltpu.roll(a, s, axis=0), 0.0)

    sums = []
    acc = u
    for wdt in POOL_WINDOWS:
        acc = acc + shifted(acc, wdt // 2)
        sums.append(acc)
    grp = lane // POOL_GROUP
    wsum = sums[-1]
    win = jnp.full(u.shape, POOL_WINDOWS[-1], jnp.int32)
    for gi in range(len(POOL_WINDOWS) - 2, -1, -1):
        wsum = jnp.where(grp == gi, sums[gi], wsum)
        win = jnp.where(grp == gi, POOL_WINDOWS[gi], win)
    cnt = jnp.minimum(row + 1, win).astype(F32)
    p = wsum / cnt - u
    y = _dot(p.astype(BF16), w_ref[...]) * scale_ref[...]
    o_ref[...] = y.astype(BF16)


def _pool_mixer(u, pool_w_bd, pool_scale, seq):
    t, c = u.shape
    return pl.pallas_call(
        _pool_kernel,
        grid=(t // seq,),
        in_specs=[
            pl.BlockSpec((seq, c), lambda b: (b, 0)),
            pl.BlockSpec((c, c), lambda b: (0, 0)),
            pl.BlockSpec((1, c), lambda b: (0, 0)),
        ],
        out_specs=pl.BlockSpec((seq, c), lambda b: (b, 0)),
        out_shape=jax.ShapeDtypeStruct((t, c), BF16),
        compiler_params=_cparams(("arbitrary",)),
        name="pool_mixer",
    )(u, pool_w_bd, pool_scale.reshape(1, c))


def _attn_kernel(sink_ref, q_ref, kvc_ref, kvp_ref, bias_ref, o_ref):
    nb = pl.program_id(1)
    kv = jnp.concatenate([kvp_ref[...], kvc_ref[...]], axis=0)
    k = kv[:, :KV_DIM]
    v = kv[:, KV_DIM:]
    qi = lax.broadcasted_iota(jnp.int32, (WINDOW, 2 * WINDOW), 0)
    kj = lax.broadcasted_iota(jnp.int32, (WINDOW, 2 * WINDOW), 1)
    dist = qi + WINDOW - kj
    valid = (dist >= 0) & (dist < WINDOW) & ((kj >= WINDOW) | (nb > 0))
    lo = lax.broadcasted_iota(jnp.int32, (WINDOW, LANES), 1) < HEAD_DIM
    for p in range(N_ATT_HEADS // 2):
        qp = q_ref[:, p * LANES:(p + 1) * LANES]
        halves = []
        for half in range(2):
            h = p + 3 * half
            qm = jnp.where(lo if half == 0 else jnp.logical_not(lo), qp, jnp.zeros_like(qp))
            s = _dot_nt(qm, k) * (HEAD_DIM ** -0.5)
            s = jnp.where(valid, s + bias_ref[h], NEG_INF)
            sink = sink_ref[h]
            m = jnp.maximum(jnp.max(s, axis=-1, keepdims=True), sink)
            pr = jnp.exp(s - m)
            den = jnp.sum(pr, axis=-1, keepdims=True) + jnp.exp(sink - m)
            halves.append(_dot(pr.astype(BF16), v) / den)
        o_ref[:, p * LANES:(p + 1) * LANES] = jnp.where(lo, halves[0], halves[1]).astype(BF16)


def _swa_attention(q, kv, bias, sinks, seq):
    t = q.shape[0]
    nblk = seq // WINDOW
    return pl.pallas_call(
        _attn_kernel,
        grid=(t // seq, nblk),
        in_specs=[
            pl.BlockSpec(memory_space=pltpu.SMEM),
            pl.BlockSpec((WINDOW, ATT_DIM), lambda b, n: (b * nblk + n, 0)),
            pl.BlockSpec((WINDOW, 2 * KV_DIM), lambda b, n: (b * nblk + n, 0)),
            pl.BlockSpec((WINDOW, 2 * KV_DIM), lambda b, n: (b * nblk + jnp.maximum(n - 1, 0), 0)),
            pl.BlockSpec((N_ATT_HEADS, WINDOW, 2 * WINDOW), lambda b, n: (0, 0, 0)),
        ],
        out_specs=pl.BlockSpec((WINDOW, ATT_DIM), lambda b, n: (b * nblk + n, 0)),
        out_shape=jax.ShapeDtypeStruct((t, ATT_DIM), BF16),
        compiler_params=_cparams(("arbitrary", "arbitrary")),
        name="swa_attention",
    )(sinks, q, kv, kv, bias)


def _gdn_kernel(x_ref, halo_ref, ba_ref, cw_ref, alog_ref, dtb_ref, nw_ref, y_ref, state_ref):
    sc_id = pl.program_id(1)
    rows = GDN_SUPER
    nchunk = rows // GDN_CHUNK
    c_sz = GDN_CHUNK

    @pl.when(sc_id == 0)
    def _():
        state_ref[...] = jnp.zeros_like(state_ref)

    x = x_ref[...]
    halo = jnp.where(sc_id == 0, 0.0, halo_ref[...])
    row8 = lax.broadcasted_iota(jnp.int32, halo.shape, 0)
    acc = x * cw_ref[CONV_WIDTH - 1:CONV_WIDTH, :]
    for s in range(1, CONV_WIDTH):
        xr = pltpu.roll(x, s, axis=0)
        hr = pltpu.roll(halo, s, axis=0)
        top = jnp.where(row8 < s, hr, xr[:SUBLANES])
        xs = jnp.concatenate([top, xr[SUBLANES:]], axis=0)
        acc = acc + xs * cw_ref[CONV_WIDTH - 1 - s:CONV_WIDTH - s, :]
    act = acc * _sigmoid(acc)

    ri = lax.broadcasted_iota(jnp.int32, (rows, rows), 0)
    ci = lax.broadcasted_iota(jnp.int32, (rows, rows), 1)
    same_chunk = (ri // c_sz) == (ci // c_sz)
    incl = same_chunk & (ri >= ci)
    strict = same_chunk & (ri > ci)
    tri_incl = jnp.where(incl, 1.0, 0.0).astype(BF16)
    eye = jnp.where(ri == ci, 1.0, 0.0)
    blk = []
    bsz = SUBLANES
    while bsz <= c_sz:
        blk.append((ri // bsz) == (ci // bsz))
        bsz *= 2
    li = lax.broadcasted_iota(jnp.int32, (LANES, LANES), 0)
    lj = lax.broadcasted_iota(jnp.int32, (LANES, LANES), 1)
    half_ones = jnp.where((li // HEAD_DIM) == (lj // HEAD_DIM), 1.0, 0.0).astype(BF16)
    lane_lo = lax.broadcasted_iota(jnp.int32, (rows, LANES), 1) < HEAD_DIM
    lane_lo_c = lax.broadcasted_iota(jnp.int32, (c_sz, LANES), 1) < HEAD_DIM

    ba = ba_ref[...]
    beta_all = _sigmoid(ba)
    sp_in = ba + dtb_ref[...]
    softplus = jnp.maximum(sp_in, 0.0) + jnp.log(1.0 + jnp.exp(-jnp.abs(sp_in)))
    g_all = -jnp.exp(alog_ref[...]) * softplus
    gcum = _dot_hi_exact_rhs_lhs(tri_incl, g_all)
    gcum_t = gcum.T

    heads = range(N_GDN_HEADS)
    lane_hi = jnp.logical_not(lane_lo)
    lane_hi_c = jnp.logical_not(lane_lo_c)
    mk = [lane_lo if h % 2 == 0 else lane_hi for h in heads]
    mk_c = [lane_lo_c if h % 2 == 0 else lane_hi_c for h in heads]
    scale = HEAD_DIM ** -0.5

    def bdot(a, b):
        return _dot(a.astype(BF16), b.astype(BF16))

    xk, xq, gn, gc_col, beta, eg = [], [], [], [], [], []
    for h in heads:
        base = 4 * LANES * (h // 2) + LANES * (h % 2)
        g = act[:, base:base + LANES]
        g = g * lax.rsqrt(_dot_hi_exact_rhs(g * g, half_ones) + NORM_EPS)
        gn.append(g)
        xk.append(jnp.where(mk[h], g, 0.0))
        xq.append(jnp.where(mk[h], pltpu.roll(g, HEAD_DIM, axis=1), 0.0) * scale)
        beta.append(beta_all[:, h:h + 1])
        gc_col.append(gcum[:, N_GDN_HEADS + h:N_GDN_HEADS + h + 1])
        eg.append(jnp.exp(gc_col[h]))

    l_mat, attn, rhs = [], [], []
    for h in heads:
        gc_row = gcum_t[N_GDN_HEADS + h:N_GDN_HEADS + h + 1, :]
        decay = jnp.exp(jnp.where(incl, gc_col[h] - gc_row, -jnp.inf))
        xk_b = xk[h].astype(BF16)
        kk = _dot_nt((xk[h] * beta[h]).astype(BF16), xk_b)
        l_mat.append(jnp.where(strict, kk * decay, 0.0))
        attn.append((_dot_nt(xq[h].astype(BF16), xk_b) * decay).astype(BF16))
        vv = act[:, 4 * LANES * (h // 2) + 2 * LANES:4 * LANES * (h // 2) + 3 * LANES]
        rhs.append(jnp.where(mk[h], gn[h] * eg[h], vv) * beta[h])

    a1 = [jnp.where(blk[0], -l_mat[h], 0.0).astype(BF16) for h in heads]
    a2 = [_dot(a1[h], a1[h]).astype(BF16) for h in heads]
    a4 = [_dot(a2[h], a2[h]).astype(BF16) for h in heads]
    inv = [eye + a1[h].astype(F32) for h in heads]
    inv = [inv[h] + _dot(a2[h], inv[h].astype(BF16)) for h in heads]
    inv = [inv[h] + _dot(a4[h], inv[h].astype(BF16)) for h in heads]
    for lvl in range(1, len(blk) - 1):
        band = blk[lvl] & jnp.logical_not(blk[lvl - 1])
        inv_b = [inv[h].astype(BF16) for h in heads]
        mid = [_dot(jnp.where(band, l_mat[h], 0.0).astype(BF16), inv_b[h]) for h in heads]
        inv = [inv[h] - _dot(inv_b[h], mid[h].astype(BF16)) for h in heads]
    band = blk[-1] & jnp.logical_not(blk[-2])
    inv_b = [inv[h].astype(BF16) for h in heads]
    half = [_dot(inv_b[h], rhs[h].astype(BF16)) for h in heads]
    mid = [_dot(jnp.where(band, l_mat[h], 0.0).astype(BF16), half[h].astype(BF16)) for h in heads]
    sol = [half[h] - _dot(inv_b[h], mid[h].astype(BF16)) for h in heads]

    st = [state_ref[h] for h in heads]
    vn_parts = [[] for _ in heads]
    qs_parts = [[] for _ in heads]
    for c in range(nchunk):
        r0 = c * c_sz
        for h in heads:
            sol_c = sol[h][r0:r0 + c_sz]
            glast = gcum[r0 + c_sz - 1:r0 + c_sz, N_GDN_HEADS + h:N_GDN_HEADS + h + 1]
            kd_t = (xk[h][r0:r0 + c_sz] * jnp.exp(glast - gc_col[h][r0:r0 + c_sz])).T
            qd = xq[h][r0:r0 + c_sz] * eg[h][r0:r0 + c_sz]
            lhs = jnp.concatenate([jnp.where(mk_c[h], sol_c, 0.0), qd], axis=0)
            m1 = bdot(lhs, st[h])
            vn = sol_c - m1[:c_sz]
            vn_parts[h].append(vn)
            qs_parts[h].append(m1[c_sz:])
            st[h] = st[h] * jnp.exp(glast) + bdot(kd_t, jnp.where(mk_c[h], 0.0, vn))
    for h in heads:
        state_ref[h] = st[h]

    for p in range(N_GDN_HEADS // 2):
        o_pair = []
        for h in (2 * p, 2 * p + 1):
            vn_all = jnp.concatenate(vn_parts[h], axis=0)
            o_pair.append(jnp.concatenate(qs_parts[h], axis=0) + _dot(attn[h], vn_all.astype(BF16)))
        o = jnp.where(lane_lo, o_pair[1], o_pair[0])
        ms = _dot_hi_exact_rhs(o * o, half_ones) * (1.0 / HEAD_DIM)
        zz = act[:, 4 * LANES * p + 3 * LANES:4 * LANES * (p + 1)]
        y = o * lax.rsqrt(ms + NORM_EPS) * nw_ref[...] * zz
        y_ref[:, p * LANES:(p + 1) * LANES] = y.astype(BF16)


def _dot_hi_exact_rhs_lhs(m_bf16, x):
    hi, lo = _split_bf16(x)
    return _dot(m_bf16, hi) + _dot(m_bf16, lo)


def _gdn_mixer(gdn, ba, conv_p, alog_v, dtb_v, nw_v, seq):
    t, c = gdn.shape
    rows = GDN_SUPER
    nsc = seq // rows
    hb = rows // SUBLANES
    return pl.pallas_call(
        _gdn_kernel,
        grid=(t // seq, nsc),
        in_specs=[
            pl.BlockSpec((rows, c), lambda b, s: (b * nsc + s, 0)),
            pl.BlockSpec((SUBLANES, c), lambda b, s: (jnp.maximum((b * nsc + s) * hb - 1, 0), 0)),
            pl.BlockSpec((rows, LANES), lambda b, s: (b * nsc + s, 0)),
            pl.BlockSpec((CONV_WIDTH, c), lambda b, s: (0, 0)),
            pl.BlockSpec((1, LANES), lambda b, s: (0, 0)),
            pl.BlockSpec((1, LANES), lambda b, s: (0, 0)),
            pl.BlockSpec((1, LANES), lambda b, s: (0, 0)),
        ],
        out_specs=pl.BlockSpec((rows, GDN_DIM), lambda b, s: (b * nsc + s, 0)),
        out_shape=jax.ShapeDtypeStruct((t, GDN_DIM), BF16),
        scratch_shapes=[pltpu.VMEM((N_GDN_HEADS, LANES, LANES), F32)],
        compiler_params=_cparams(("arbitrary", "arbitrary")),
        name="gdn_mixer",
    )(gdn, gdn, ba, conv_p, alog_v, dtb_v, nw_v)


def _outproj_kernel(x_ref, mod_ref, yp_ref, ya_ref, yg_ref, wp_ref, wa_ref, wg_ref, lng_ref, lnb_ref,
                    rwh_ref, rwl_ref, rb_ref, x1_ref, h2_ref, logit_ref):
    y = _dot(yp_ref[...], wp_ref[...]) + _dot(ya_ref[...], wa_ref[...]) + _dot(yg_ref[...], wg_ref[...])
    g1 = mod_ref[0, 2:3, :]
    sh2 = mod_ref[0, 3:4, :]
    sc2 = mod_ref[0, 4:5, :]
    x1 = _layer_norm(DEEPNORM_ALPHA * x_ref[...] + g1 * y, lng_ref[...], lnb_ref[...])
    x1_ref[...] = x1
    h2 = x1 * (1.0 + sc2) + sh2
    hh, hl = _split_bf16(h2)
    h2_ref[...] = _pack_bf16_pairs(h2)
    logit_ref[...] = (_dot(hh, rwh_ref[...]) + _dot(hl, rwh_ref[...]) + _dot(hh, rwl_ref[...])
                      + rb_ref[...])


def _out_projection(x2d, mod, yp, ya, yg, w_out_p, ln_g, ln_b, rw_hi, rw_lo, rb, seq):
    t, d = x2d.shape
    tm = ROW_TILE
    wp = w_out_p[:POOL_DIM]
    wa = w_out_p[POOL_DIM:POOL_DIM + ATT_DIM]
    wg = w_out_p[POOL_DIM + ATT_DIM:]
    row = lambda i: (i, 0)
    fixed = lambda i: (0, 0)
    return pl.pallas_call(
        _outproj_kernel,
        grid=(t // tm,),
        in_specs=[
            pl.BlockSpec((tm, d), row),
            pl.BlockSpec((1, 6, d), lambda i: ((i * tm) // seq, 0, 0)),
            pl.BlockSpec((tm, POOL_DIM), row),
            pl.BlockSpec((tm, ATT_DIM), row),
            pl.BlockSpec((tm, GDN_DIM), row),
            pl.BlockSpec((POOL_DIM, d), fixed),
            pl.BlockSpec((ATT_DIM, d), fixed),
            pl.BlockSpec((GDN_DIM, d), fixed),
            pl.BlockSpec((1, d), fixed),
            pl.BlockSpec((1, d), fixed),
            pl.BlockSpec((d, LANES), fixed),
            pl.BlockSpec((d, LANES), fixed),
            pl.BlockSpec((1, LANES), fixed),
        ],
        out_specs=[pl.BlockSpec((tm, d), row), pl.BlockSpec((tm, d // 2), row), pl.BlockSpec((tm, LANES), row)],
        out_shape=[jax.ShapeDtypeStruct((t, d), F32), jax.ShapeDtypeStruct((t, d // 2), jnp.int32),
                   jax.ShapeDtypeStruct((t, LANES), F32)],
        compiler_params=_cparams(("arbitrary",)),
        name="out_proj_ln",
    )(x2d, mod, yp, ya, yg, wp, wa, wg, ln_g.reshape(1, d), ln_b.reshape(1, d), rw_hi, rw_lo, rb)


def _route_kernel(logit_ref, info_ref, cnt_ref, carry_ref):
    i = pl.program_id(0)

    @pl.when(i == 0)
    def _():
        carry_ref[...] = jnp.zeros_like(carry_ref)

    lg = logit_ref[...]
    tm = lg.shape[0]
    lane = lax.broadcasted_iota(jnp.int32, lg.shape, 1).astype(F32)
    work = lg
    vals, idxs = [], []
    for _k in range(TOP_K):
        m = jnp.max(work, axis=-1, keepdims=True)
        idx = jnp.min(jnp.where(work == m, lane, float(LANES)), axis=-1, keepdims=True)
        vals.append(m)
        idxs.append(idx)
        work = jnp.where(lane == idx, -jnp.inf, work)
    exps = [jnp.exp(v - vals[0]) for v in vals]
    den = exps[0] + exps[1] + exps[2] + exps[3]
    onehots = [lane == idx for idx in idxs]
    member = jnp.zeros(lg.shape, F32)
    for oh in onehots:
        member = member + jnp.where(oh, 1.0, 0.0)
    ri = lax.broadcasted_iota(jnp.int32, (tm, tm), 0)
    ci = lax.broadcasted_iota(jnp.int32, (tm, tm), 1)
    before = jnp.where(ri > ci, 1.0, 0.0).astype(BF16)
    rank = _dot(before, member.astype(BF16)) + carry_ref[...]
    carry_ref[...] = carry_ref[...] + jnp.sum(member, axis=0, keepdims=True)
    info = jnp.zeros(lg.shape, F32)
    for k in range(TOP_K):
        rank_k = jnp.sum(jnp.where(onehots[k], rank, 0.0), axis=-1, keepdims=True)
        info = jnp.where(lane == float(k), idxs[k], info)
        info = jnp.where(lane == float(TOP_K + k), rank_k, info)
        info = jnp.where(lane == float(2 * TOP_K + k), exps[k] / den, info)
    info_ref[...] = info
    cnt_ref[...] = carry_ref[...]


def _routing(logits):
    t = logits.shape[0]
    tm = ROUTE_TILE
    return pl.pallas_call(
        _route_kernel,
        grid=(t // tm,),
        in_specs=[pl.BlockSpec((tm, LANES), lambda i: (i, 0))],
        out_specs=[pl.BlockSpec((tm, LANES), lambda i: (i, 0)), pl.BlockSpec((1, LANES), lambda i: (0, 0))],
        out_shape=[jax.ShapeDtypeStruct((t, LANES), F32), jax.ShapeDtypeStruct((1, LANES), F32)],
        scratch_shapes=[pltpu.VMEM((1, LANES), F32)],
        compiler_params=_cparams(("arbitrary",)),
        name="moe_route",
    )(logits)


def _expert_kernel(e0, be_ref, nxt_ref, nu_ref, x_ref, wup_hbm, bup_ref, wdn_hbm, bdn_ref, y_ref,
                   wup_st, wdn_st, wup_bf, wdn_bf, sems):
    i = pl.program_id(0)
    e = be_ref[i]
    prev = be_ref[jnp.maximum(i - 1, 0)]
    used = i < nu_ref[0]

    def weight_copies(expert):
        return (pltpu.make_async_copy(wup_hbm.at[e0 + expert], wup_st, sems.at[0]),
                pltpu.make_async_copy(wdn_hbm.at[e0 + expert], wdn_st, sems.at[1]))

    @pl.when(i == 0)
    def _():
        for cp in weight_copies(e):
            cp.start()

    @pl.when(used & ((i == 0) | (e != prev)))
    def _():
        for cp in weight_copies(e):
            cp.wait()
        wup_bf[...] = wup_st[...].astype(BF16)
        wdn_bf[...] = wdn_st[...].astype(BF16)

        @pl.when(nxt_ref[i] >= 0)
        def _():
            for cp in weight_copies(nxt_ref[i]):
                cp.start()

    @pl.when(used)
    def _():
        xb = _unpack_bf16_pairs(x_ref[...]).astype(BF16)
        hb = _dot(xb, wup_bf[...]) + bup_ref[0]
        x_glu = jnp.minimum(hb[:, :EXPERT_DIM], SWIGLU_LIMIT)
        x_lin = jnp.clip(hb[:, EXPERT_DIM:], -SWIGLU_LIMIT, SWIGLU_LIMIT)
        act = x_glu * _sigmoid(SWIGLU_ALPHA * x_glu) * (x_lin + 1.0)
        y = _dot(act.astype(BF16), wdn_bf[...]) + bdn_ref[0]
        y_ref[...] = _pack_bf16_pairs(y)

    @pl.when(i >= nu_ref[0])
    def _():
        y_ref[...] = jnp.zeros_like(y_ref)


def _expert_ffn(xbuf, block_e, next_e, n_used, w_up, b_up, w_down, b_down, layer):
    p, dh = xbuf.shape
    d = 2 * dh
    bm = EXPERT_BLOCK
    ne, _, n_up = w_up.shape
    e0 = layer * N_EXPERTS
    grid_spec = pltpu.PrefetchScalarGridSpec(
        num_scalar_prefetch=3,
        grid=(p // bm,),
        in_specs=[
            pl.BlockSpec((bm, dh), lambda i, be, nx, nu: (i, 0)),
            pl.BlockSpec(memory_space=pl.ANY),
            pl.BlockSpec((1, 1, n_up), lambda i, be, nx, nu: (e0 + be[i], 0, 0)),
            pl.BlockSpec(memory_space=pl.ANY),
            pl.BlockSpec((1, 1, d), lambda i, be, nx, nu: (e0 + be[i], 0, 0)),
        ],
        out_specs=pl.BlockSpec((bm, dh), lambda i, be, nx, nu: (i, 0)),
        scratch_shapes=[pltpu.VMEM((d, n_up), F32), pltpu.VMEM((EXPERT_DIM, d), F32),
                        pltpu.VMEM((d, n_up), BF16), pltpu.VMEM((EXPERT_DIM, d), BF16),
                        pltpu.SemaphoreType.DMA((2,))],
    )
    return pl.pallas_call(
        functools.partial(_expert_kernel, e0),
        grid_spec=grid_spec,
        out_shape=jax.ShapeDtypeStruct((p, dh), jnp.int32),
        compiler_params=_cparams(("arbitrary",)),
        name="expert_ffn",
    )(block_e, next_e, n_used, xbuf, w_up, b_up, w_down, b_down)


def _combine_kernel(x1_ref, mod_ref, yg_ref, info_ref, lng_ref, lnb_ref, o_ref):
    info = info_ref[...]
    y = jnp.zeros(x1_ref.shape, F32)
    for k in range(TOP_K):
        gate = info[:, 2 * TOP_K + k:2 * TOP_K + k + 1]
        y = y + gate * _unpack_bf16_pairs(yg_ref[k])
    g2 = mod_ref[0, 5:6, :]
    o_ref[...] = _layer_norm(DEEPNORM_ALPHA * x1_ref[...] + g2 * y, lng_ref[...], lnb_ref[...])


def _combine(x1, mod, yg, info, ln_g, ln_b, seq):
    t, d = x1.shape
    tm = ROW_TILE
    row = lambda i: (i, 0)
    fixed = lambda i: (0, 0)
    return pl.pallas_call(
        _combine_kernel,
        grid=(t // tm,),
        in_specs=[
            pl.BlockSpec((tm, d), row),
            pl.BlockSpec((1, 6, d), lambda i: ((i * tm) // seq, 0, 0)),
            pl.BlockSpec((TOP_K, tm, d // 2), lambda i: (0, i, 0)),
            pl.BlockSpec((tm, LANES), row),
            pl.BlockSpec((1, d), fixed),
            pl.BlockSpec((1, d), fixed),
        ],
        out_specs=pl.BlockSpec((tm, d), row),
        out_shape=jax.ShapeDtypeStruct((t, d), F32),
        compiler_params=_cparams(("arbitrary",)),
        name="moe_combine_ln",
    )(x1, mod, yg, info, ln_g.reshape(1, d), ln_b.reshape(1, d))


def _sc_workers():
    info = plsc.get_sparse_core_info()
    return info.num_cores, info.num_cores * info.num_subcores


def _sc_scatter_rows(rows, idx, n_out):
    t, w = rows.shape
    kk = idx.shape[0]
    n_cores, n_workers = _sc_workers()
    ch = SC_CHUNK
    assert t % (n_workers * ch) == 0
    n_chunk = t // (n_workers * ch)
    idx_c = jnp.transpose(idx.reshape(kk, t // ch, ch), (1, 0, 2))

    @functools.partial(
        pl.kernel,
        mesh=plsc.VectorSubcoreMesh(core_axis_name="c", subcore_axis_name="s"),
        out_type=jax.ShapeDtypeStruct((n_out, w), rows.dtype),
        scratch_types=[pltpu.VMEM((kk, ch), jnp.int32), pltpu.VMEM((ch, w), rows.dtype)],
        name="sc_dispatch_scatter",
    )
    def scatter_kernel(rows_hbm, idx_hbm, out_hbm, idx_v, rows_v):
        wid = lax.axis_index("s") * n_cores + lax.axis_index("c")

        @pl.loop(0, n_chunk)
        def _(j):
            cidx = wid * n_chunk + j
            pltpu.sync_copy(idx_hbm.at[cidx], idx_v)
            pltpu.sync_copy(rows_hbm.at[pl.ds(cidx * ch, ch)], rows_v)
            for q in range(kk):
                pltpu.sync_copy(rows_v, out_hbm.at[idx_v.at[q]])

    return scatter_kernel(rows, idx_c)


def _sc_gather_rows(table, idx):
    m = idx.shape[0]
    w = table.shape[1]
    n_cores, n_workers = _sc_workers()
    ch = SC_CHUNK
    assert m % (n_workers * ch) == 0
    n_chunk = m // (n_workers * ch)
    idx_c = idx.reshape(m // ch, 1, ch)

    @functools.partial(
        pl.kernel,
        mesh=plsc.VectorSubcoreMesh(core_axis_name="c", subcore_axis_name="s"),
        out_type=jax.ShapeDtypeStruct((m, w), table.dtype),
        scratch_types=[pltpu.VMEM((1, ch), jnp.int32), pltpu.VMEM((ch, w), table.dtype)],
        name="sc_combine_gather",
    )
    def gather_kernel(table_hbm, idx_hbm, out_hbm, idx_v, rows_v):
        wid = lax.axis_index("s") * n_cores + lax.axis_index("c")

        @pl.loop(0, n_chunk)
        def _(j):
            cidx = wid * n_chunk + j
            pltpu.sync_copy(idx_hbm.at[cidx], idx_v)
            pltpu.sync_copy(table_hbm.at[idx_v.at[0]], rows_v)
            pltpu.sync_copy(rows_v, out_hbm.at[pl.ds(cidx * ch, ch)])

    return gather_kernel(table, idx_c)


def _lane_vector(vals, offset):
    return jnp.zeros((1, LANES), F32).at[0, offset:offset + vals.shape[0]].set(vals.astype(F32))


def _moe(h2, logits, x1, mod, ln_g, ln_b, w_up, b_up, w_down, b_down, layer, seq):
    t, dh = h2.shape
    a = t * TOP_K
    bm = EXPERT_BLOCK
    info, cnt = _routing(logits)
    e_idx = info[:, 0:TOP_K].astype(jnp.int32)
    rank = info[:, TOP_K:2 * TOP_K].astype(jnp.int32)
    counts = cnt[0, :N_EXPERTS].astype(jnp.int32)
    padded = ((counts + bm - 1) // bm) * bm
    pcum = jnp.cumsum(padded)
    pstart = pcum - padded
    dest = jnp.transpose(pstart[e_idx] + rank)
    n_blocks = -(-a // bm) + N_EXPERTS
    block_e = jnp.minimum(jnp.sum(pcum[None, :] <= (jnp.arange(n_blocks) * bm)[:, None], axis=1),
                          N_EXPERTS - 1).astype(jnp.int32)
    n_used = (pcum[-1] // bm).astype(jnp.int32).reshape(1)
    group_end = jnp.sum(block_e[None, :] <= block_e[:, None], axis=1)
    next_e = jnp.where(group_end < n_used[0], block_e[jnp.minimum(group_end, n_blocks - 1)], -1).astype(jnp.int32)
    xbuf = _sc_scatter_rows(h2, dest, n_blocks * bm)
    ybuf = _expert_ffn(xbuf, block_e, next_e, n_used, w_up, b_up, w_down, b_down, layer)
    yg = _sc_gather_rows(ybuf, dest.reshape(a)).reshape(TOP_K, t, dh)
    return _combine(x1, mod, yg, info, ln_g, ln_b, seq)


def kernel(x, c, rel_bias, w_in, w_out, w_ada, b_ada, ln1_g, ln1_b, ln2_g, ln2_b, pool_w, pool_scale,
           attn_sinks, conv_w, gdn_a_log, gdn_dt_bias, gdn_norm_w, router_w, router_b,
           exp_w_up, exp_b_up, exp_w_down, exp_b_down):
    bsz, seq, d = x.shape
    depth = w_in.shape[0]
    t = bsz * seq
    assert d == D_MODEL and w_in.shape[2] == IN_DIM
    assert seq % GDN_SUPER == 0 and seq % WINDOW == 0 and t % ROW_TILE == 0 and seq % ROW_TILE == 0

    mod_all = _modulation(c, w_ada, b_ada).reshape(depth, bsz, 6, d)
    bias = jnp.transpose(jnp.take(rel_bias.astype(F32), jnp.asarray(_BUCKETS), axis=0), (2, 0, 1))

    w_up_all = exp_w_up.reshape((depth * N_EXPERTS,) + exp_w_up.shape[2:])
    b_up_all = exp_b_up.reshape(depth * N_EXPERTS, 1, exp_b_up.shape[2])
    w_down_all = exp_w_down.reshape((depth * N_EXPERTS,) + exp_w_down.shape[2:])
    b_down_all = exp_b_down.reshape(depth * N_EXPERTS, 1, exp_b_down.shape[2])

    x2d = x.reshape(t, d)
    for l in range(depth):
        mod = mod_all[l]
        w_in_p = _take_cols(w_in[l], _IN_PERM).astype(BF16)
        w_out_p = _take_static(w_out[l], _OUT_PERM, 0).astype(BF16)
        ident = jnp.zeros((CONV_WIDTH, 1), F32).at[CONV_WIDTH - 1, 0].set(1.0)
        conv_p = jnp.where(jnp.asarray(_GDN_CONV_SRC >= 0), _take_cols(conv_w[l].astype(F32), _GDN_CONV_SRC),
                           ident)
        pool_bd = jnp.zeros((POOL_DIM, POOL_DIM), F32)
        for gi in range(len(POOL_WINDOWS)):
            sl = slice(gi * POOL_GROUP, (gi + 1) * POOL_GROUP)
            pool_bd = pool_bd.at[sl, sl].set(pool_w[l, gi].astype(F32))
        alog_v = _lane_vector(gdn_a_log[l], N_GDN_HEADS)
        dtb_v = _lane_vector(gdn_dt_bias[l], N_GDN_HEADS)
        nw_v = jnp.tile(gdn_norm_w[l].astype(F32), 2).reshape(1, LANES)
        rw = jnp.zeros((d, LANES), F32).at[:, :N_EXPERTS].set(router_w[l].astype(F32))
        rw_hi, rw_lo = _split_bf16(rw)
        rb = jnp.full((1, LANES), NEG_INF, F32).at[0, :N_EXPERTS].set(router_b[l].astype(F32))

        u_pool, aq, akv, gdn, ba = _in_projection(x2d, mod, w_in_p, seq)
        y_pool = _pool_mixer(u_pool, pool_bd.astype(BF16), pool_scale[l].astype(F32), seq)
        y_att = _swa_attention(aq, akv, bias, attn_sinks[l].astype(F32), seq)
        y_gdn = _gdn_mixer(gdn, ba, conv_p, alog_v, dtb_v, nw_v, seq)
        x1, h2, logits = _out_projection(x2d, mod, y_pool, y_att, y_gdn, w_out_p, ln1_g[l], ln1_b[l],
                                         rw_hi, rw_lo, rb, seq)
        x2d = _moe(h2, logits, x1, mod, ln2_g[l], ln2_b[l], w_up_all, b_up_all, w_down_all, b_down_all,
                   l, seq)
    return x2d.reshape(bsz, seq, d)
```

```python
import functools

import numpy as np
import jax
import jax.numpy as jnp
from jax import lax
from jax.experimental import pallas as pl
from jax.experimental.pallas import tpu as pltpu
from jax.experimental.pallas import tpu_sc as plsc

F32 = jnp.float32
BF16 = jnp.bfloat16

D_MODEL = 1024
HEAD_DIM = 64
POOL_DIM = 256
POOL_WINDOWS = (2, 4, 8, 16)
POOL_GROUP = 64
N_ATT_HEADS = 6
N_KV_HEADS = 2
ATT_DIM = 384
KV_DIM = 128
WINDOW = 128
N_BUCKETS = 32
MAX_DISTANCE = 128
N_GDN_HEADS = 6
GDN_DIM = 384
CONV_WIDTH = 4
GDN_CHUNK = 64
N_EXPERTS = 32
TOP_K = 4
EXPERT_DIM = 1024
SWIGLU_ALPHA = 1.702
SWIGLU_LIMIT = 7.0
DEPTH = 2
DEEPNORM_ALPHA = (2 * DEPTH) ** 0.25
LN_EPS = 1e-5
NORM_EPS = 1e-6
NEG_INF = -1e30

LANES = 128
SUBLANES = 8
VMEM_LIMIT = 56 * 1024 * 1024

ROW_TILE = 512
ATT_BLOCKS = 2
GDN_SUPER = 256
ROUTE_TILE = 256
EXPERT_BLOCK = 512
SC_CHUNK = 64

_OFF_AQ = POOL_DIM
_OFF_AK = _OFF_AQ + ATT_DIM
_OFF_AV = _OFF_AK + KV_DIM
_OFF_GQ = _OFF_AV + KV_DIM
_OFF_GK = _OFF_GQ + GDN_DIM
_OFF_GV = _OFF_GK + GDN_DIM
_OFF_GZ = _OFF_GV + GDN_DIM
_OFF_GB = _OFF_GZ + GDN_DIM
_OFF_GA = _OFF_GB + N_GDN_HEADS
IN_DIM = _OFF_GA + N_GDN_HEADS

P_POOL = (0, POOL_DIM)
P_Q = (P_POOL[1], P_POOL[1] + ATT_DIM)
P_KV = (P_Q[1], P_Q[1] + 2 * KV_DIM)
P_GDN = (P_KV[1], P_KV[1] + 4 * GDN_DIM)
P_BA = (P_GDN[1], P_GDN[1] + LANES)
P_TOTAL = P_BA[1]


def _head_cols(off, h):
    return list(range(off + HEAD_DIM * h, off + HEAD_DIM * (h + 1)))


def _build_in_perm():
    cols = list(range(POOL_DIM))
    for p in range(N_ATT_HEADS // 2):
        cols += _head_cols(_OFF_AQ, p) + _head_cols(_OFF_AQ, p + 3)
    cols += list(range(_OFF_AK, _OFF_AK + 2 * KV_DIM))
    gdn_src = []
    for p in range(N_GDN_HEADS // 2):
        e, o = 2 * p, 2 * p + 1
        grp = (_head_cols(_OFF_GK, e) + _head_cols(_OFF_GQ, e)
               + _head_cols(_OFF_GQ, o) + _head_cols(_OFF_GK, o)
               + _head_cols(_OFF_GV, o) + _head_cols(_OFF_GV, e)
               + _head_cols(_OFF_GZ, o) + _head_cols(_OFF_GZ, e))
        cols += grp
        gdn_src += [c - _OFF_GQ if c < _OFF_GZ else -1 for c in grp]
    cols += list(range(_OFF_GB, _OFF_GB + 2 * N_GDN_HEADS))
    cols += [-1] * (LANES - 2 * N_GDN_HEADS)
    assert len(cols) == P_TOTAL
    return np.asarray(cols, np.int32), np.asarray(gdn_src, np.int32)


_IN_PERM, _GDN_CONV_SRC = _build_in_perm()


def _build_out_perm():
    rows = list(range(POOL_DIM))
    for p in range(N_ATT_HEADS // 2):
        rows += _head_cols(POOL_DIM, p) + _head_cols(POOL_DIM, p + 3)
    for p in range(N_GDN_HEADS // 2):
        rows += _head_cols(POOL_DIM + ATT_DIM, 2 * p + 1) + _head_cols(POOL_DIM + ATT_DIM, 2 * p)
    return np.asarray(rows, np.int32)


_OUT_PERM = _build_out_perm()


def _t5_bucket_line():
    n = np.maximum(2 * WINDOW - 1 - np.arange(3 * WINDOW - 1), 0)
    max_exact = N_BUCKETS // 2
    nf = np.maximum(n, 1).astype(np.float32)
    large = max_exact + (np.log(nf / max_exact) / np.float32(np.log(MAX_DISTANCE / max_exact))
                         * (N_BUCKETS - max_exact)).astype(np.int32)
    large = np.minimum(large, N_BUCKETS - 1)
    return np.where(n < max_exact, n, large).astype(np.int32)


_BUCKET_LINE = _t5_bucket_line()


def _band_bias(rel_bias):
    n_line = 3 * WINDOW - 1
    line = jnp.take(rel_bias.astype(F32), jnp.asarray(_BUCKET_LINE), axis=0).T
    heads = line.shape[0]
    padded = jnp.concatenate([line, jnp.zeros((heads, 1), F32)], axis=1)
    skew = jnp.tile(padded, (1, WINDOW))[:, :WINDOW * n_line].reshape(heads, WINDOW, n_line)
    return skew[:, :, WINDOW - 1:3 * WINDOW - 1]


def _take_static(w, perm, axis):
    parts = []
    start = 0
    for i in range(1, len(perm) + 1):
        run_ends = (i == len(perm) or ((perm[i] < 0) != (perm[i - 1] < 0))
                    or (perm[i] >= 0 and perm[i] != perm[i - 1] + 1))
        if run_ends:
            if perm[start] < 0:
                shape = list(w.shape)
                shape[axis] = i - start
                parts.append(jnp.zeros(shape, w.dtype))
            else:
                parts.append(lax.slice_in_dim(w, int(perm[start]), int(perm[start]) + (i - start), axis=axis))
            start = i
    return jnp.concatenate(parts, axis=axis)


def _take_cols(w, perm):
    return _take_static(w, perm, w.ndim - 1)


def _split_bf16(x):
    hi = x.astype(BF16)
    lo = (x - hi.astype(F32)).astype(BF16)
    return hi, lo


def _pack_bf16_pairs(x):
    n = x.shape[1] // 2
    bits = pltpu.bitcast(x.astype(BF16).astype(F32), jnp.int32)
    return lax.shift_right_logical(bits[:, :n], 16) | bits[:, n:]


def _unpack_bf16_pairs(u):
    lo = pltpu.bitcast(lax.shift_left(u, 16), F32)
    hi = pltpu.bitcast(u & jnp.int32(-65536), F32)
    return jnp.concatenate([lo, hi], axis=1)


def _dot(a, b):
    return jnp.dot(a, b, preferred_element_type=F32)


def _dot_nt(a, b):
    return lax.dot_general(a, b, (((1,), (1,)), ((), ())), preferred_element_type=F32)


def _dot_hi_exact_rhs(x, m_bf16):
    hi, lo = _split_bf16(x)
    return _dot(hi, m_bf16) + _dot(lo, m_bf16)


def _sigmoid(x):
    return 1.0 / (1.0 + jnp.exp(-x))


def _layer_norm(r, g, b):
    mu = jnp.mean(r, axis=-1, keepdims=True)
    d = r - mu
    var = jnp.mean(d * d, axis=-1, keepdims=True)
    return d * lax.rsqrt(var + LN_EPS) * g + b


def _cparams(sem):
    return pltpu.CompilerParams(dimension_semantics=sem, vmem_limit_bytes=VMEM_LIMIT)


def _mod_kernel(c_ref, w_ref, b_ref, o_ref):
    c = c_ref[...]
    ca = c * _sigmoid(c)
    ch, cl = _split_bf16(ca)
    wh, wl = _split_bf16(w_ref[0])
    o_ref[0] = _dot(ch, wh) + _dot(cl, wh) + _dot(ch, wl) + b_ref[0]


def _modulation(c, w_ada, b_ada):
    depth, d, n = w_ada.shape
    bsz = c.shape[0]
    tn = 512
    return pl.pallas_call(
        _mod_kernel,
        grid=(depth, n // tn),
        in_specs=[
            pl.BlockSpec((bsz, d), lambda l, j: (0, 0)),
            pl.BlockSpec((1, d, tn), lambda l, j: (l, 0, j)),
            pl.BlockSpec((1, 1, tn), lambda l, j: (l, 0, j)),
        ],
        out_specs=pl.BlockSpec((1, bsz, tn), lambda l, j: (l, 0, j)),
        out_shape=jax.ShapeDtypeStruct((depth, bsz, n), F32),
        compiler_params=_cparams(("arbitrary", "arbitrary")),
        name="adaln_mod",
    )(c, w_ada, b_ada.reshape(depth, 1, n))


def _inproj_kernel(x_ref, mod_ref, w_ref, pool_ref, q_ref, kv_ref, gdn_ref, ba_ref):
    sh = mod_ref[0, 0:1, :]
    sc = mod_ref[0, 1:2, :]
    h = (x_ref[...] * (1.0 + sc) + sh).astype(BF16)

    def mm(rng):
        return _dot(h, w_ref[:, rng[0]:rng[1]])

    pool_ref[...] = mm(P_POOL)
    q_ref[...] = mm(P_Q).astype(BF16)
    kv_ref[...] = mm(P_KV).astype(BF16)
    gdn_ref[...] = mm(P_GDN)
    ba_ref[...] = mm(P_BA)


def _in_projection(x2d, mod, w_in_p, seq):
    t, d = x2d.shape
    tm = ROW_TILE
    widths = [r[1] - r[0] for r in (P_POOL, P_Q, P_KV, P_GDN, P_BA)]
    dtypes = [F32, BF16, BF16, F32, F32]
    return pl.pallas_call(
        _inproj_kernel,
        grid=(t // tm,),
        in_specs=[
            pl.BlockSpec((tm, d), lambda i: (i, 0)),
            pl.BlockSpec((1, 6, d), lambda i: ((i * tm) // seq, 0, 0)),
            pl.BlockSpec((d, P_TOTAL), lambda i: (0, 0)),
        ],
        out_specs=[pl.BlockSpec((tm, w), lambda i: (i, 0)) for w in widths],
        out_shape=[jax.ShapeDtypeStruct((t, w), dt) for w, dt in zip(widths, dtypes)],
        compiler_params=_cparams(("arbitrary",)),
        name="in_proj",
    )(x2d, mod, w_in_p)


def _pool_kernel(u_ref, w_ref, scale_ref, o_ref):
    u = u_ref[...]
    row = lax.broadcasted_iota(jnp.int32, u.shape, 0)
    lane = lax.broadcasted_iota(jnp.int32, u.shape, 1)

    def shifted(a, s):
        return jnp.where(row >= s, pltpu.roll(a, s, axis=0), 0.0)

    sums = []
    acc = u
    for wdt in POOL_WINDOWS:
        acc = acc + shifted(acc, wdt // 2)
        sums.append(acc)
    grp = lane // POOL_GROUP
    wsum = sums[-1]
    win = jnp.full(u.shape, POOL_WINDOWS[-1], jnp.int32)
    for gi in range(len(POOL_WINDOWS) - 2, -1, -1):
        wsum = jnp.where(grp == gi, sums[gi], wsum)
        win = jnp.where(grp == gi, POOL_WINDOWS[gi], win)
    cnt = jnp.minimum(row + 1, win).astype(F32)
    p = wsum / cnt - u
    y = _dot(p.astype(BF16), w_ref[...]) * scale_ref[...]
    o_ref[...] = y.astype(BF16)


def _pool_mixer(u, pool_w_bd, pool_scale, seq):
    t, c = u.shape
    return pl.pallas_call(
        _pool_kernel,
        grid=(t // seq,),
        in_specs=[
            pl.BlockSpec((seq, c), lambda b: (b, 0)),
            pl.BlockSpec((c, c), lambda b: (0, 0)),
            pl.BlockSpec((1, c), lambda b: (0, 0)),
        ],
        out_specs=pl.BlockSpec((seq, c), lambda b: (b, 0)),
        out_shape=jax.ShapeDtypeStruct((t, c), BF16),
        compiler_params=_cparams(("arbitrary",)),
        name="pool_mixer",
    )(u, pool_w_bd, pool_scale.reshape(1, c))


def _attn_kernel(sink_ref, q_ref, kvc_ref, kvp_ref, bias_ref, o_ref):
    step = pl.program_id(1)
    qi = lax.broadcasted_iota(jnp.int32, (WINDOW, 2 * WINDOW), 0)
    kj = lax.broadcasted_iota(jnp.int32, (WINDOW, 2 * WINDOW), 1)
    dist = qi + WINDOW - kj
    in_band = (dist >= 0) & (dist < WINDOW)
    lo = lax.broadcasted_iota(jnp.int32, (WINDOW, LANES), 1) < HEAD_DIM
    for sub in range(ATT_BLOCKS):
        r0 = sub * WINDOW
        prev = kvp_ref[...] if sub == 0 else kvc_ref[r0 - WINDOW:r0, :]
        kv = jnp.concatenate([prev, kvc_ref[r0:r0 + WINDOW, :]], axis=0)
        k = kv[:, :KV_DIM]
        v = kv[:, KV_DIM:]
        valid = in_band & ((kj >= WINDOW) | (step > 0)) if sub == 0 else in_band
        for p in range(N_ATT_HEADS // 2):
            qp = q_ref[r0:r0 + WINDOW, p * LANES:(p + 1) * LANES]
            halves = []
            for half in range(2):
                h = p + 3 * half
                qm = jnp.where(lo if half == 0 else jnp.logical_not(lo), qp, jnp.zeros_like(qp))
                s = _dot_nt(qm, k) * (HEAD_DIM ** -0.5)
                s = jnp.where(valid, s + bias_ref[h], NEG_INF)
                sink = sink_ref[h]
                m = jnp.maximum(jnp.max(s, axis=-1, keepdims=True), sink)
                pr = jnp.exp(s - m)
                den = jnp.sum(pr, axis=-1, keepdims=True) + jnp.exp(sink - m)
                halves.append(_dot(pr.astype(BF16), v) / den)
            o_ref[r0:r0 + WINDOW, p * LANES:(p + 1) * LANES] = (
                jnp.where(lo, halves[0], halves[1]).astype(BF16))


def _swa_attention(q, kv, bias, sinks, seq):
    t = q.shape[0]
    rows = ATT_BLOCKS * WINDOW
    nblk = seq // rows
    return pl.pallas_call(
        _attn_kernel,
        grid=(t // seq, nblk),
        in_specs=[
            pl.BlockSpec(memory_space=pltpu.SMEM),
            pl.BlockSpec((rows, ATT_DIM), lambda b, n: (b * nblk + n, 0)),
            pl.BlockSpec((rows, 2 * KV_DIM), lambda b, n: (b * nblk + n, 0)),
            pl.BlockSpec((WINDOW, 2 * KV_DIM),
                         lambda b, n: (jnp.maximum((b * nblk + n) * ATT_BLOCKS - 1, 0), 0)),
            pl.BlockSpec((N_ATT_HEADS, WINDOW, 2 * WINDOW), lambda b, n: (0, 0, 0)),
        ],
        out_specs=pl.BlockSpec((rows, ATT_DIM), lambda b, n: (b * nblk + n, 0)),
        out_shape=jax.ShapeDtypeStruct((t, ATT_DIM), BF16),
        compiler_params=_cparams(("arbitrary", "arbitrary")),
        name="swa_attention",
    )(sinks, q, kv, kv, bias)


def _gdn_kernel(x_ref, halo_ref, ba_ref, cw_ref, alog_ref, dtb_ref, nw_ref, y_ref, state_ref):
    sc_id = pl.program_id(1)
    rows = GDN_SUPER
    nchunk = rows // GDN_CHUNK
    c_sz = GDN_CHUNK

    @pl.when(sc_id == 0)
    def _():
        state_ref[...] = jnp.zeros_like(state_ref)

    x = x_ref[...]
    halo = jnp.where(sc_id == 0, 0.0, halo_ref[...])
    row8 = lax.broadcasted_iota(jnp.int32, halo.shape, 0)
    acc = x * cw_ref[CONV_WIDTH - 1:CONV_WIDTH, :]
    for s in range(1, CONV_WIDTH):
        xr = pltpu.roll(x, s, axis=0)
        hr = pltpu.roll(halo, s, axis=0)
        top = jnp.where(row8 < s, hr, xr[:SUBLANES])
        xs = jnp.concatenate([top, xr[SUBLANES:]], axis=0)
        acc = acc + xs * cw_ref[CONV_WIDTH - 1 - s:CONV_WIDTH - s, :]
    act = acc * _sigmoid(acc)

    ri = lax.broadcasted_iota(jnp.int32, (rows, rows), 0)
    ci = lax.broadcasted_iota(jnp.int32, (rows, rows), 1)
    same_chunk = (ri // c_sz) == (ci // c_sz)
    incl = same_chunk & (ri >= ci)
    strict = same_chunk & (ri > ci)
    tri_incl = jnp.where(incl, 1.0, 0.0).astype(BF16)
    eye = jnp.where(ri == ci, 1.0, 0.0)
    blk = []
    bsz = SUBLANES
    while bsz <= c_sz:
        blk.append((ri // bsz) == (ci // bsz))
        bsz *= 2
    li = lax.broadcasted_iota(jnp.int32, (LANES, LANES), 0)
    lj = lax.broadcasted_iota(jnp.int32, (LANES, LANES), 1)
    half_ones = jnp.where((li // HEAD_DIM) == (lj // HEAD_DIM), 1.0, 0.0).astype(BF16)
    lane_lo = lax.broadcasted_iota(jnp.int32, (rows, LANES), 1) < HEAD_DIM
    lane_lo_c = lax.broadcasted_iota(jnp.int32, (c_sz, LANES), 1) < HEAD_DIM

    ba = ba_ref[...]
    beta_all = _sigmoid(ba)
    sp_in = ba + dtb_ref[...]
    softplus = jnp.maximum(sp_in, 0.0) + jnp.log(1.0 + jnp.exp(-jnp.abs(sp_in)))
    g_all = -jnp.exp(alog_ref[...]) * softplus
    gcum = _dot_hi_exact_rhs_lhs(tri_incl, g_all)
    gcum_t = gcum.T

    heads = range(N_GDN_HEADS)
    lane_hi = jnp.logical_not(lane_lo)
    lane_hi_c = jnp.logical_not(lane_lo_c)
    mk = [lane_lo if h % 2 == 0 else lane_hi for h in heads]
    mk_c = [lane_lo_c if h % 2 == 0 else lane_hi_c for h in heads]
    scale = HEAD_DIM ** -0.5

    def bdot(a, b):
        return _dot(a.astype(BF16), b.astype(BF16))

    xk, xq, gn, gc_col, beta, eg = [], [], [], [], [], []
    for h in heads:
        base = 4 * LANES * (h // 2) + LANES * (h % 2)
        g = act[:, base:base + LANES]
        g = g * lax.rsqrt(_dot_hi_exact_rhs(g * g, half_ones) + NORM_EPS)
        gn.append(g)
        xk.append(jnp.where(mk[h], g, 0.0))
        xq.append(jnp.where(mk[h], pltpu.roll(g, HEAD_DIM, axis=1), 0.0) * scale)
        beta.append(beta_all[:, h:h + 1])
        gc_col.append(gcum[:, N_GDN_HEADS + h:N_GDN_HEADS + h + 1])
        eg.append(jnp.exp(gc_col[h]))

    l_mat, attn, rhs = [], [], []
    for h in heads:
        gc_row = gcum_t[N_GDN_HEADS + h:N_GDN_HEADS + h + 1, :]
        decay = jnp.exp(jnp.where(incl, gc_col[h] - gc_row, -jnp.inf))
        xk_b = xk[h].astype(BF16)
        kk = _dot_nt((xk[h] * beta[h]).astype(BF16), xk_b)
        l_mat.append(jnp.where(strict, kk * decay, 0.0))
        attn.append((_dot_nt(xq[h].astype(BF16), xk_b) * decay).astype(BF16))
        vv = act[:, 4 * LANES * (h // 2) + 2 * LANES:4 * LANES * (h // 2) + 3 * LANES]
        rhs.append(jnp.where(mk[h], gn[h] * eg[h], vv) * beta[h])

    a1 = [jnp.where(blk[0], -l_mat[h], 0.0).astype(BF16) for h in heads]
    a2 = [_dot(a1[h], a1[h]).astype(BF16) for h in heads]
    a4 = [_dot(a2[h], a2[h]).astype(BF16) for h in heads]
    inv = [eye + a1[h].astype(F32) for h in heads]
    inv = [inv[h] + _dot(a2[h], inv[h].astype(BF16)) for h in heads]
    inv = [inv[h] + _dot(a4[h], inv[h].astype(BF16)) for h in heads]
    for lvl in range(1, len(blk) - 1):
        band = blk[lvl] & jnp.logical_not(blk[lvl - 1])
        inv_b = [inv[h].astype(BF16) for h in heads]
        mid = [_dot(jnp.where(band, l_mat[h], 0.0).astype(BF16), inv_b[h]) for h in heads]
        inv = [inv[h] - _dot(inv_b[h], mid[h].astype(BF16)) for h in heads]
    band = blk[-1] & jnp.logical_not(blk[-2])
    inv_b = [inv[h].astype(BF16) for h in heads]
    half = [_dot(inv_b[h], rhs[h].astype(BF16)) for h in heads]
    mid = [_dot(jnp.where(band, l_mat[h], 0.0).astype(BF16), half[h].astype(BF16)) for h in heads]
    sol = [half[h] - _dot(inv_b[h], mid[h].astype(BF16)) for h in heads]

    st = [state_ref[h] for h in heads]
    vn_parts = [[] for _ in heads]
    qs_parts = [[] for _ in heads]
    for c in range(nchunk):
        r0 = c * c_sz
        for h in heads:
            sol_c = sol[h][r0:r0 + c_sz]
            glast = gcum[r0 + c_sz - 1:r0 + c_sz, N_GDN_HEADS + h:N_GDN_HEADS + h + 1]
            kd_t = (xk[h][r0:r0 + c_sz] * jnp.exp(glast - gc_col[h][r0:r0 + c_sz])).T
            qd = xq[h][r0:r0 + c_sz] * eg[h][r0:r0 + c_sz]
            lhs = jnp.concatenate([jnp.where(mk_c[h], sol_c, 0.0), qd], axis=0)
            m1 = bdot(lhs, st[h])
            vn = sol_c - m1[:c_sz]
            vn_parts[h].append(vn)
            qs_parts[h].append(m1[c_sz:])
            st[h] = st[h] * jnp.exp(glast) + bdot(kd_t, jnp.where(mk_c[h], 0.0, vn))
    for h in heads:
        state_ref[h] = st[h]

    for p in range(N_GDN_HEADS // 2):
        o_pair = []
        for h in (2 * p, 2 * p + 1):
            vn_all = jnp.concatenate(vn_parts[h], axis=0)
            o_pair.append(jnp.concatenate(qs_parts[h], axis=0) + _dot(attn[h], vn_all.astype(BF16)))
        o = jnp.where(lane_lo, o_pair[1], o_pair[0])
        ms = _dot_hi_exact_rhs(o * o, half_ones) * (1.0 / HEAD_DIM)
        zz = act[:, 4 * LANES * p + 3 * LANES:4 * LANES * (p + 1)]
        y = o * lax.rsqrt(ms + NORM_EPS) * nw_ref[...] * zz
        y_ref[:, p * LANES:(p + 1) * LANES] = y.astype(BF16)


def _dot_hi_exact_rhs_lhs(m_bf16, x):
    hi, lo = _split_bf16(x)
    return _dot(m_bf16, hi) + _dot(m_bf16, lo)


def _gdn_mixer(gdn, ba, conv_p, alog_v, dtb_v, nw_v, seq):
    t, c = gdn.shape
    rows = GDN_SUPER
    nsc = seq // rows
    hb = rows // SUBLANES
    return pl.pallas_call(
        _gdn_kernel,
        grid=(t // seq, nsc),
        in_specs=[
            pl.BlockSpec((rows, c), lambda b, s: (b * nsc + s, 0)),
            pl.BlockSpec((SUBLANES, c), lambda b, s: (jnp.maximum((b * nsc + s) * hb - 1, 0), 0)),
            pl.BlockSpec((rows, LANES), lambda b, s: (b * nsc + s, 0)),
            pl.BlockSpec((CONV_WIDTH, c), lambda b, s: (0, 0)),
            pl.BlockSpec((1, LANES), lambda b, s: (0, 0)),
            pl.BlockSpec((1, LANES), lambda b, s: (0, 0)),
            pl.BlockSpec((1, LANES), lambda b, s: (0, 0)),
        ],
        out_specs=pl.BlockSpec((rows, GDN_DIM), lambda b, s: (b * nsc + s, 0)),
        out_shape=jax.ShapeDtypeStruct((t, GDN_DIM), BF16),
        scratch_shapes=[pltpu.VMEM((N_GDN_HEADS, LANES, LANES), F32)],
        compiler_params=_cparams(("arbitrary", "arbitrary")),
        name="gdn_mixer",
    )(gdn, gdn, ba, conv_p, alog_v, dtb_v, nw_v)


def _outproj_kernel(x_ref, mod_ref, yp_ref, ya_ref, yg_ref, wp_ref, wa_ref, wg_ref, lng_ref, lnb_ref,
                    rwh_ref, rwl_ref, rb_ref, x1_ref, h2_ref, logit_ref):
    y = _dot(yp_ref[...], wp_ref[...]) + _dot(ya_ref[...], wa_ref[...]) + _dot(yg_ref[...], wg_ref[...])
    g1 = mod_ref[0, 2:3, :]
    sh2 = mod_ref[0, 3:4, :]
    sc2 = mod_ref[0, 4:5, :]
    x1 = _layer_norm(DEEPNORM_ALPHA * x_ref[...] + g1 * y, lng_ref[...], lnb_ref[...])
    x1_ref[...] = x1
    h2 = x1 * (1.0 + sc2) + sh2
    hh, hl = _split_bf16(h2)
    h2_ref[...] = _pack_bf16_pairs(h2)
    logit_ref[...] = (_dot(hh, rwh_ref[...]) + _dot(hl, rwh_ref[...]) + _dot(hh, rwl_ref[...])
                      + rb_ref[...])


def _out_projection(x2d, mod, yp, ya, yg, w_out_p, ln_g, ln_b, rw_hi, rw_lo, rb, seq):
    t, d = x2d.shape
    tm = ROW_TILE
    wp = w_out_p[:POOL_DIM]
    wa = w_out_p[POOL_DIM:POOL_DIM + ATT_DIM]
    wg = w_out_p[POOL_DIM + ATT_DIM:]
    row = lambda i: (i, 0)
    fixed = lambda i: (0, 0)
    return pl.pallas_call(
        _outproj_kernel,
        grid=(t // tm,),
        in_specs=[
            pl.BlockSpec((tm, d), row),
            pl.BlockSpec((1, 6, d), lambda i: ((i * tm) // seq, 0, 0)),
            pl.BlockSpec((tm, POOL_DIM), row),
            pl.BlockSpec((tm, ATT_DIM), row),
            pl.BlockSpec((tm, GDN_DIM), row),
            pl.BlockSpec((POOL_DIM, d), fixed),
            pl.BlockSpec((ATT_DIM, d), fixed),
            pl.BlockSpec((GDN_DIM, d), fixed),
            pl.BlockSpec((1, d), fixed),
            pl.BlockSpec((1, d), fixed),
            pl.BlockSpec((d, LANES), fixed),
            pl.BlockSpec((d, LANES), fixed),
            pl.BlockSpec((1, LANES), fixed),
        ],
        out_specs=[pl.BlockSpec((tm, d), row), pl.BlockSpec((tm, d // 2), row), pl.BlockSpec((tm, LANES), row)],
        out_shape=[jax.ShapeDtypeStruct((t, d), F32), jax.ShapeDtypeStruct((t, d // 2), jnp.int32),
                   jax.ShapeDtypeStruct((t, LANES), F32)],
        compiler_params=_cparams(("arbitrary",)),
        name="out_proj_ln",
    )(x2d, mod, yp, ya, yg, wp, wa, wg, ln_g.reshape(1, d), ln_b.reshape(1, d), rw_hi, rw_lo, rb)


def _route_kernel(logit_ref, info_ref, cnt_ref, carry_ref):
    i = pl.program_id(0)

    @pl.when(i == 0)
    def _():
        carry_ref[...] = jnp.zeros_like(carry_ref)

    lg = logit_ref[...]
    tm = lg.shape[0]
    lane = lax.broadcasted_iota(jnp.int32, lg.shape, 1).astype(F32)
    work = lg
    vals, idxs = [], []
    for _k in range(TOP_K):
        m = jnp.max(work, axis=-1, keepdims=True)
        idx = jnp.min(jnp.where(work == m, lane, float(LANES)), axis=-1, keepdims=True)
        vals.append(m)
        idxs.append(idx)
        work = jnp.where(lane == idx, -jnp.inf, work)
    exps = [jnp.exp(v - vals[0]) for v in vals]
    den = exps[0] + exps[1] + exps[2] + exps[3]
    onehots = [lane == idx for idx in idxs]
    member = jnp.zeros(lg.shape, F32)
    for oh in onehots:
        member = member + jnp.where(oh, 1.0, 0.0)
    ri = lax.broadcasted_iota(jnp.int32, (tm, tm), 0)
    ci = lax.broadcasted_iota(jnp.int32, (tm, tm), 1)
    before = jnp.where(ri > ci, 1.0, 0.0).astype(BF16)
    rank = _dot(before, member.astype(BF16)) + carry_ref[...]
    carry_ref[...] = carry_ref[...] + jnp.sum(member, axis=0, keepdims=True)
    info = jnp.zeros(lg.shape, F32)
    for k in range(TOP_K):
        rank_k = jnp.sum(jnp.where(onehots[k], rank, 0.0), axis=-1, keepdims=True)
        info = jnp.where(lane == float(k), idxs[k], info)
        info = jnp.where(lane == float(TOP_K + k), rank_k, info)
        info = jnp.where(lane == float(2 * TOP_K + k), exps[k] / den, info)
    info_ref[...] = info
    cnt_ref[...] = carry_ref[...]


def _routing(logits):
    t = logits.shape[0]
    tm = ROUTE_TILE
    return pl.pallas_call(
        _route_kernel,
        grid=(t // tm,),
        in_specs=[pl.BlockSpec((tm, LANES), lambda i: (i, 0))],
        out_specs=[pl.BlockSpec((tm, LANES), lambda i: (i, 0)), pl.BlockSpec((1, LANES), lambda i: (0, 0))],
        out_shape=[jax.ShapeDtypeStruct((t, LANES), F32), jax.ShapeDtypeStruct((1, LANES), F32)],
        scratch_shapes=[pltpu.VMEM((1, LANES), F32)],
        compiler_params=_cparams(("arbitrary",)),
        name="moe_route",
    )(logits)


def _expert_kernel(e0, be_ref, nxt_ref, nu_ref, x_ref, wup_hbm, bup_ref, wdn_hbm, bdn_ref, y_ref,
                   wup_st, wdn_st, wup_bf, wdn_bf, sems):
    i = pl.program_id(0)
    e = be_ref[i]
    prev = be_ref[jnp.maximum(i - 1, 0)]
    used = i < nu_ref[0]

    def weight_copies(expert):
        return (pltpu.make_async_copy(wup_hbm.at[e0 + expert], wup_st, sems.at[0]),
                pltpu.make_async_copy(wdn_hbm.at[e0 + expert], wdn_st, sems.at[1]))

    @pl.when(i == 0)
    def _():
        for cp in weight_copies(e):
            cp.start()

    @pl.when(used & ((i == 0) | (e != prev)))
    def _():
        for cp in weight_copies(e):
            cp.wait()
        wup_bf[...] = wup_st[...].astype(BF16)
        wdn_bf[...] = wdn_st[...].astype(BF16)

        @pl.when(nxt_ref[i] >= 0)
        def _():
            for cp in weight_copies(nxt_ref[i]):
                cp.start()

    @pl.when(used)
    def _():
        xb = _unpack_bf16_pairs(x_ref[...]).astype(BF16)
        hb = _dot(xb, wup_bf[...]) + bup_ref[0]
        x_glu = jnp.minimum(hb[:, :EXPERT_DIM], SWIGLU_LIMIT)
        x_lin = jnp.clip(hb[:, EXPERT_DIM:], -SWIGLU_LIMIT, SWIGLU_LIMIT)
        act = x_glu * _sigmoid(SWIGLU_ALPHA * x_glu) * (x_lin + 1.0)
        y = _dot(act.astype(BF16), wdn_bf[...]) + bdn_ref[0]
        y_ref[...] = _pack_bf16_pairs(y)

    @pl.when(i >= nu_ref[0])
    def _():
        y_ref[...] = jnp.zeros_like(y_ref)


def _expert_ffn(xbuf, block_e, next_e, n_used, w_up, b_up, w_down, b_down, layer):
    p, dh = xbuf.shape
    d = 2 * dh
    bm = EXPERT_BLOCK
    ne, _, n_up = w_up.shape
    e0 = layer * N_EXPERTS
    grid_spec = pltpu.PrefetchScalarGridSpec(
        num_scalar_prefetch=3,
        grid=(p // bm,),
        in_specs=[
            pl.BlockSpec((bm, dh), lambda i, be, nx, nu: (i, 0)),
            pl.BlockSpec(memory_space=pl.ANY),
            pl.BlockSpec((1, 1, n_up), lambda i, be, nx, nu: (e0 + be[i], 0, 0)),
            pl.BlockSpec(memory_space=pl.ANY),
            pl.BlockSpec((1, 1, d), lambda i, be, nx, nu: (e0 + be[i], 0, 0)),
        ],
        out_specs=pl.BlockSpec((bm, dh), lambda i, be, nx, nu: (i, 0)),
        scratch_shapes=[pltpu.VMEM((d, n_up), F32), pltpu.VMEM((EXPERT_DIM, d), F32),
                        pltpu.VMEM((d, n_up), BF16), pltpu.VMEM((EXPERT_DIM, d), BF16),
                        pltpu.SemaphoreType.DMA((2,))],
    )
    return pl.pallas_call(
        functools.partial(_expert_kernel, e0),
        grid_spec=grid_spec,
        out_shape=jax.ShapeDtypeStruct((p, dh), jnp.int32),
        compiler_params=_cparams(("arbitrary",)),
        name="expert_ffn",
    )(block_e, next_e, n_used, xbuf, w_up, b_up, w_down, b_down)


def _combine_kernel(x1_ref, mod_ref, yg_ref, info_ref, lng_ref, lnb_ref, o_ref):
    info = info_ref[...]
    y = jnp.zeros(x1_ref.shape, F32)
    for k in range(TOP_K):
        gate = info[:, 2 * TOP_K + k:2 * TOP_K + k + 1]
        y = y + gate * _unpack_bf16_pairs(yg_ref[k])
    g2 = mod_ref[0, 5:6, :]
    o_ref[...] = _layer_norm(DEEPNORM_ALPHA * x1_ref[...] + g2 * y, lng_ref[...], lnb_ref[...])


def _combine(x1, mod, yg, info, ln_g, ln_b, seq):
    t, d = x1.shape
    tm = ROW_TILE
    row = lambda i: (i, 0)
    fixed = lambda i: (0, 0)
    return pl.pallas_call(
        _combine_kernel,
        grid=(t // tm,),
        in_specs=[
            pl.BlockSpec((tm, d), row),
            pl.BlockSpec((1, 6, d), lambda i: ((i * tm) // seq, 0, 0)),
            pl.BlockSpec((TOP_K, tm, d // 2), lambda i: (0, i, 0)),
            pl.BlockSpec((tm, LANES), row),
            pl.BlockSpec((1, d), fixed),
            pl.BlockSpec((1, d), fixed),
        ],
        out_specs=pl.BlockSpec((tm, d), row),
        out_shape=jax.ShapeDtypeStruct((t, d), F32),
        compiler_params=_cparams(("arbitrary",)),
        name="moe_combine_ln",
    )(x1, mod, yg, info, ln_g.reshape(1, d), ln_b.reshape(1, d))


def _sc_workers():
    info = plsc.get_sparse_core_info()
    return info.num_cores, info.num_cores * info.num_subcores


def _sc_scatter_rows(rows, idx, n_out):
    t, w = rows.shape
    kk = idx.shape[0]
    n_cores, n_workers = _sc_workers()
    ch = SC_CHUNK
    assert t % (n_workers * ch) == 0
    n_chunk = t // (n_workers * ch)
    idx_c = jnp.transpose(idx.reshape(kk, t // ch, ch), (1, 0, 2))

    @functools.partial(
        pl.kernel,
        mesh=plsc.VectorSubcoreMesh(core_axis_name="c", subcore_axis_name="s"),
        out_type=jax.ShapeDtypeStruct((n_out, w), rows.dtype),
        scratch_types=[pltpu.VMEM((kk, ch), jnp.int32), pltpu.VMEM((ch, w), rows.dtype)],
        name="sc_dispatch_scatter",
    )
    def scatter_kernel(rows_hbm, idx_hbm, out_hbm, idx_v, rows_v):
        wid = lax.axis_index("s") * n_cores + lax.axis_index("c")

        @pl.loop(0, n_chunk)
        def _(j):
            cidx = wid * n_chunk + j
            pltpu.sync_copy(idx_hbm.at[cidx], idx_v)
            pltpu.sync_copy(rows_hbm.at[pl.ds(cidx * ch, ch)], rows_v)
            for q in range(kk):
                pltpu.sync_copy(rows_v, out_hbm.at[idx_v.at[q]])

    return scatter_kernel(rows, idx_c)


def _sc_gather_rows(table, idx):
    m = idx.shape[0]
    w = table.shape[1]
    n_cores, n_workers = _sc_workers()
    ch = SC_CHUNK
    assert m % (n_workers * ch) == 0
    n_chunk = m // (n_workers * ch)
    idx_c = idx.reshape(m // ch, 1, ch)

    @functools.partial(
        pl.kernel,
        mesh=plsc.VectorSubcoreMesh(core_axis_name="c", subcore_axis_name="s"),
        out_type=jax.ShapeDtypeStruct((m, w), table.dtype),
        scratch_types=[pltpu.VMEM((1, ch), jnp.int32), pltpu.VMEM((ch, w), table.dtype)],
        name="sc_combine_gather",
    )
    def gather_kernel(table_hbm, idx_hbm, out_hbm, idx_v, rows_v):
        wid = lax.axis_index("s") * n_cores + lax.axis_index("c")

        @pl.loop(0, n_chunk)
        def _(j):
            cidx = wid * n_chunk + j
            pltpu.sync_copy(idx_hbm.at[cidx], idx_v)
            pltpu.sync_copy(table_hbm.at[idx_v.at[0]], rows_v)
            pltpu.sync_copy(rows_v, out_hbm.at[pl.ds(cidx * ch, ch)])

    return gather_kernel(table, idx_c)


def _lane_vector(vals, offset):
    return jnp.zeros((1, LANES), F32).at[0, offset:offset + vals.shape[0]].set(vals.astype(F32))


def _moe(h2, logits, x1, mod, ln_g, ln_b, w_up, b_up, w_down, b_down, layer, seq):
    t, dh = h2.shape
    a = t * TOP_K
    bm = EXPERT_BLOCK
    info, cnt = _routing(logits)
    e_idx = info[:, 0:TOP_K].astype(jnp.int32)
    rank = info[:, TOP_K:2 * TOP_K].astype(jnp.int32)
    counts = cnt[0, :N_EXPERTS].astype(jnp.int32)
    padded = ((counts + bm - 1) // bm) * bm
    pcum = jnp.cumsum(padded)
    pstart = pcum - padded
    dest = jnp.transpose(pstart[e_idx] + rank)
    n_blocks = -(-a // bm) + N_EXPERTS
    block_e = jnp.minimum(jnp.sum(pcum[None, :] <= (jnp.arange(n_blocks) * bm)[:, None], axis=1),
                          N_EXPERTS - 1).astype(jnp.int32)
    n_used = (pcum[-1] // bm).astype(jnp.int32).reshape(1)
    group_end = jnp.sum(block_e[None, :] <= block_e[:, None], axis=1)
    next_e = jnp.where(group_end < n_used[0], block_e[jnp.minimum(group_end, n_blocks - 1)], -1).astype(jnp.int32)
    xbuf = _sc_scatter_rows(h2, dest, n_blocks * bm)
    ybuf = _expert_ffn(xbuf, block_e, next_e, n_used, w_up, b_up, w_down, b_down, layer)
    yg = _sc_gather_rows(ybuf, dest.reshape(a)).reshape(TOP_K, t, dh)
    return _combine(x1, mod, yg, info, ln_g, ln_b, seq)


def kernel(x, c, rel_bias, w_in, w_out, w_ada, b_ada, ln1_g, ln1_b, ln2_g, ln2_b, pool_w, pool_scale,
           attn_sinks, conv_w, gdn_a_log, gdn_dt_bias, gdn_norm_w, router_w, router_b,
           exp_w_up, exp_b_up, exp_w_down, exp_b_down):
    bsz, seq, d = x.shape
    depth = w_in.shape[0]
    t = bsz * seq
    assert d == D_MODEL and w_in.shape[2] == IN_DIM
    assert seq % GDN_SUPER == 0 and seq % (ATT_BLOCKS * WINDOW) == 0
    assert t % ROW_TILE == 0 and seq % ROW_TILE == 0

    mod_all = _modulation(c, w_ada, b_ada).reshape(depth, bsz, 6, d)
    bias = _band_bias(rel_bias)

    w_up_all = exp_w_up.reshape((depth * N_EXPERTS,) + exp_w_up.shape[2:])
    b_up_all = exp_b_up.reshape(depth * N_EXPERTS, 1, exp_b_up.shape[2])
    w_down_all = exp_w_down.reshape((depth * N_EXPERTS,) + exp_w_down.shape[2:])
    b_down_all = exp_b_down.reshape(depth * N_EXPERTS, 1, exp_b_down.shape[2])

    x2d = x.reshape(t, d)
    for l in range(depth):
        mod = mod_all[l]
        w_in_p = _take_cols(w_in[l], _IN_PERM).astype(BF16)
        w_out_p = _take_static(w_out[l], _OUT_PERM, 0).astype(BF16)
        ident = jnp.zeros((CONV_WIDTH, 1), F32).at[CONV_WIDTH - 1, 0].set(1.0)
        conv_p = jnp.where(jnp.asarray(_GDN_CONV_SRC >= 0), _take_cols(conv_w[l].astype(F32), _GDN_CONV_SRC),
                           ident)
        pool_bd = jnp.zeros((POOL_DIM, POOL_DIM), F32)
        for gi in range(len(POOL_WINDOWS)):
            sl = slice(gi * POOL_GROUP, (gi + 1) * POOL_GROUP)
            pool_bd = pool_bd.at[sl, sl].set(pool_w[l, gi].astype(F32))
        alog_v = _lane_vector(gdn_a_log[l], N_GDN_HEADS)
        dtb_v = _lane_vector(gdn_dt_bias[l], N_GDN_HEADS)
        nw_v = jnp.tile(gdn_norm_w[l].astype(F32), 2).reshape(1, LANES)
        rw = jnp.zeros((d, LANES), F32).at[:, :N_EXPERTS].set(router_w[l].astype(F32))
        rw_hi, rw_lo = _split_bf16(rw)
        rb = jnp.full((1, LANES), NEG_INF, F32).at[0, :N_EXPERTS].set(router_b[l].astype(F32))

        u_pool, aq, akv, gdn, ba = _in_projection(x2d, mod, w_in_p, seq)
        y_pool = _pool_mixer(u_pool, pool_bd.astype(BF16), pool_scale[l].astype(F32), seq)
        y_att = _swa_attention(aq, akv, bias, attn_sinks[l].astype(F32), seq)
        y_gdn = _gdn_mixer(gdn, ba, conv_p, alog_v, dtb_v, nw_v, seq)
        x1, h2, logits = _out_projection(x2d, mod, y_pool, y_att, y_gdn, w_out_p, ln1_g[l], ln1_b[l],
                                         rw_hi, rw_lo, rb, seq)
        x2d = _moe(h2, logits, x1, mod, ln2_g[l], ln2_b[l], w_up_all, b_up_all, w_down_all, b_down_all,
                   l, seq)
    return x2d.reshape(bsz, seq, d)
```

```python
import functools

import numpy as np
import jax
import jax.numpy as jnp
from jax import lax
from jax.experimental import pallas as pl
from jax.experimental.pallas import tpu as pltpu
from jax.experimental.pallas import tpu_sc as plsc

F32 = jnp.float32
BF16 = jnp.bfloat16

D_MODEL = 1024
HEAD_DIM = 64
POOL_DIM = 256
POOL_WINDOWS = (2, 4, 8, 16)
POOL_GROUP = 64
N_ATT_HEADS = 6
N_KV_HEADS = 2
ATT_DIM = 384
KV_DIM = 128
WINDOW = 128
N_BUCKETS = 32
MAX_DISTANCE = 128
N_GDN_HEADS = 6
GDN_DIM = 384
CONV_WIDTH = 4
GDN_CHUNK = 64
N_EXPERTS = 32
TOP_K = 4
EXPERT_DIM = 1024
SWIGLU_ALPHA = 1.702
SWIGLU_LIMIT = 7.0
DEPTH = 2
DEEPNORM_ALPHA = (2 * DEPTH) ** 0.25
LN_EPS = 1e-5
NORM_EPS = 1e-6
NEG_INF = -1e30

LANES = 128
SUBLANES = 8
VMEM_LIMIT = 56 * 1024 * 1024

ROW_TILE = 512
ATT_BLOCKS = 2
GDN_SUPER = 256
ROUTE_TILE = 256
EXPERT_BLOCK = 512
SC_CHUNK = 64

_OFF_AQ = POOL_DIM
_OFF_AK = _OFF_AQ + ATT_DIM
_OFF_AV = _OFF_AK + KV_DIM
_OFF_GQ = _OFF_AV + KV_DIM
_OFF_GK = _OFF_GQ + GDN_DIM
_OFF_GV = _OFF_GK + GDN_DIM
_OFF_GZ = _OFF_GV + GDN_DIM
_OFF_GB = _OFF_GZ + GDN_DIM
_OFF_GA = _OFF_GB + N_GDN_HEADS
IN_DIM = _OFF_GA + N_GDN_HEADS

P_POOL = (0, POOL_DIM)
P_Q = (P_POOL[1], P_POOL[1] + ATT_DIM)
P_KV = (P_Q[1], P_Q[1] + 2 * KV_DIM)
P_GDN = (P_KV[1], P_KV[1] + 4 * GDN_DIM)
P_BA = (P_GDN[1], P_GDN[1] + LANES)
P_TOTAL = P_BA[1]


def _head_cols(off, h):
    return list(range(off + HEAD_DIM * h, off + HEAD_DIM * (h + 1)))


def _build_in_perm():
    cols = list(range(POOL_DIM))
    for p in range(N_ATT_HEADS // 2):
        cols += _head_cols(_OFF_AQ, p) + _head_cols(_OFF_AQ, p + 3)
    cols += list(range(_OFF_AK, _OFF_AK + 2 * KV_DIM))
    gdn_src = []
    for p in range(N_GDN_HEADS // 2):
        e, o = 2 * p, 2 * p + 1
        grp = (_head_cols(_OFF_GK, e) + _head_cols(_OFF_GQ, e)
               + _head_cols(_OFF_GQ, o) + _head_cols(_OFF_GK, o)
               + _head_cols(_OFF_GV, o) + _head_cols(_OFF_GV, e)
               + _head_cols(_OFF_GZ, o) + _head_cols(_OFF_GZ, e))
        cols += grp
        gdn_src += [c - _OFF_GQ if c < _OFF_GZ else -1 for c in grp]
    cols += list(range(_OFF_GB, _OFF_GB + 2 * N_GDN_HEADS))
    cols += [-1] * (LANES - 2 * N_GDN_HEADS)
    assert len(cols) == P_TOTAL
    return np.asarray(cols, np.int32), np.asarray(gdn_src, np.int32)


_IN_PERM, _GDN_CONV_SRC = _build_in_perm()


def _build_out_perm():
    rows = list(range(POOL_DIM))
    for p in range(N_ATT_HEADS // 2):
        rows += _head_cols(POOL_DIM, p) + _head_cols(POOL_DIM, p + 3)
    for p in range(N_GDN_HEADS // 2):
        rows += _head_cols(POOL_DIM + ATT_DIM, 2 * p + 1) + _head_cols(POOL_DIM + ATT_DIM, 2 * p)
    return np.asarray(rows, np.int32)


_OUT_PERM = _build_out_perm()


def _t5_bucket_line():
    n = np.maximum(2 * WINDOW - 1 - np.arange(3 * WINDOW - 1), 0)
    max_exact = N_BUCKETS // 2
    nf = np.maximum(n, 1).astype(np.float32)
    large = max_exact + (np.log(nf / max_exact) / np.float32(np.log(MAX_DISTANCE / max_exact))
                         * (N_BUCKETS - max_exact)).astype(np.int32)
    large = np.minimum(large, N_BUCKETS - 1)
    return np.where(n < max_exact, n, large).astype(np.int32)


_BUCKET_LINE = _t5_bucket_line()


def _band_bias(rel_bias):
    n_line = 3 * WINDOW - 1
    line = jnp.take(rel_bias.astype(F32), jnp.asarray(_BUCKET_LINE), axis=0).T
    heads = line.shape[0]
    padded = jnp.concatenate([line, jnp.zeros((heads, 1), F32)], axis=1)
    skew = jnp.tile(padded, (1, WINDOW))[:, :WINDOW * n_line].reshape(heads, WINDOW, n_line)
    return skew[:, :, WINDOW - 1:3 * WINDOW - 1]


def _take_static(w, perm, axis):
    parts = []
    start = 0
    for i in range(1, len(perm) + 1):
        run_ends = (i == len(perm) or ((perm[i] < 0) != (perm[i - 1] < 0))
                    or (perm[i] >= 0 and perm[i] != perm[i - 1] + 1))
        if run_ends:
            if perm[start] < 0:
                shape = list(w.shape)
                shape[axis] = i - start
                parts.append(jnp.zeros(shape, w.dtype))
            else:
                parts.append(lax.slice_in_dim(w, int(perm[start]), int(perm[start]) + (i - start), axis=axis))
            start = i
    return jnp.concatenate(parts, axis=axis)


def _take_cols(w, perm):
    return _take_static(w, perm, w.ndim - 1)


def _split_bf16(x):
    hi = x.astype(BF16)
    lo = (x - hi.astype(F32)).astype(BF16)
    return hi, lo


def _pack_bf16_pairs(x):
    n = x.shape[1] // 2
    bits = pltpu.bitcast(x.astype(BF16).astype(F32), jnp.int32)
    return lax.shift_right_logical(bits[:, :n], 16) | bits[:, n:]


def _unpack_bf16_pairs(u):
    lo = pltpu.bitcast(lax.shift_left(u, 16), F32)
    hi = pltpu.bitcast(u & jnp.int32(-65536), F32)
    return jnp.concatenate([lo, hi], axis=1)


def _dot(a, b):
    return jnp.dot(a, b, preferred_element_type=F32)


def _dot_nt(a, b):
    return lax.dot_general(a, b, (((1,), (1,)), ((), ())), preferred_element_type=F32)


def _dot_hi_exact_rhs(x, m_bf16):
    hi, lo = _split_bf16(x)
    return _dot(hi, m_bf16) + _dot(lo, m_bf16)


def _sigmoid(x):
    return 1.0 / (1.0 + jnp.exp(-x))


def _layer_norm(r, g, b):
    mu = jnp.mean(r, axis=-1, keepdims=True)
    d = r - mu
    var = jnp.mean(d * d, axis=-1, keepdims=True)
    return d * lax.rsqrt(var + LN_EPS) * g + b


def _cparams(sem):
    return pltpu.CompilerParams(dimension_semantics=sem, vmem_limit_bytes=VMEM_LIMIT)


def _mod_kernel(c_ref, w_ref, b_ref, o_ref):
    c = c_ref[...]
    ca = c * _sigmoid(c)
    ch, cl = _split_bf16(ca)
    wh, wl = _split_bf16(w_ref[0])
    o_ref[0] = _dot(ch, wh) + _dot(cl, wh) + _dot(ch, wl) + b_ref[0]


def _modulation(c, w_ada, b_ada):
    depth, d, n = w_ada.shape
    bsz = c.shape[0]
    tn = 512
    return pl.pallas_call(
        _mod_kernel,
        grid=(depth, n // tn),
        in_specs=[
            pl.BlockSpec((bsz, d), lambda l, j: (0, 0)),
            pl.BlockSpec((1, d, tn), lambda l, j: (l, 0, j)),
            pl.BlockSpec((1, 1, tn), lambda l, j: (l, 0, j)),
        ],
        out_specs=pl.BlockSpec((1, bsz, tn), lambda l, j: (l, 0, j)),
        out_shape=jax.ShapeDtypeStruct((depth, bsz, n), F32),
        compiler_params=_cparams(("arbitrary", "arbitrary")),
        name="adaln_mod",
    )(c, w_ada, b_ada.reshape(depth, 1, n))


def _inproj_kernel(x_ref, mod_ref, w_ref, pool_ref, q_ref, kv_ref, gdn_ref, ba_ref):
    sh = mod_ref[0, 0:1, :]
    sc = mod_ref[0, 1:2, :]
    h = (x_ref[...] * (1.0 + sc) + sh).astype(BF16)

    def mm(rng):
        return _dot(h, w_ref[0, :, rng[0]:rng[1]])

    pool_ref[...] = mm(P_POOL)
    q_ref[...] = mm(P_Q).astype(BF16)
    kv_ref[...] = mm(P_KV).astype(BF16)
    gdn_ref[...] = mm(P_GDN)
    ba_ref[...] = mm(P_BA)


def _in_projection(x2d, mod, w_in_all, layer, seq):
    t, d = x2d.shape
    tm = ROW_TILE
    widths = [r[1] - r[0] for r in (P_POOL, P_Q, P_KV, P_GDN, P_BA)]
    dtypes = [F32, BF16, BF16, F32, F32]
    return pl.pallas_call(
        _inproj_kernel,
        grid=(t // tm,),
        in_specs=[
            pl.BlockSpec((tm, d), lambda i: (i, 0)),
            pl.BlockSpec((1, 6, d), lambda i: ((i * tm) // seq, 0, 0)),
            pl.BlockSpec((1, d, P_TOTAL), lambda i: (layer, 0, 0)),
        ],
        out_specs=[pl.BlockSpec((tm, w), lambda i: (i, 0)) for w in widths],
        out_shape=[jax.ShapeDtypeStruct((t, w), dt) for w, dt in zip(widths, dtypes)],
        compiler_params=_cparams(("arbitrary",)),
        name="in_proj",
    )(x2d, mod, w_in_all)


def _pool_kernel(u_ref, w_ref, scale_ref, o_ref):
    u = u_ref[...]
    row = lax.broadcasted_iota(jnp.int32, u.shape, 0)
    lane = lax.broadcasted_iota(jnp.int32, u.shape, 1)

    def shifted(a, s):
        return jnp.where(row >= s, pltpu.roll(a, s, axis=0), 0.0)

    sums = []
    acc = u
    for wdt in POOL_WINDOWS:
        acc = acc + shifted(acc, wdt // 2)
        sums.append(acc)
    grp = lane // POOL_GROUP
    wsum = sums[-1]
    win = jnp.full(u.shape, POOL_WINDOWS[-1], jnp.int32)
    for gi in range(len(POOL_WINDOWS) - 2, -1, -1):
        wsum = jnp.where(grp == gi, sums[gi], wsum)
        win = jnp.where(grp == gi, POOL_WINDOWS[gi], win)
    cnt = jnp.minimum(row + 1, win).astype(F32)
    p = wsum / cnt - u
    y = _dot(p.astype(BF16), w_ref[...]) * scale_ref[...]
    o_ref[...] = y.astype(BF16)


def _pool_mixer(u, pool_w_bd, pool_scale, seq):
    t, c = u.shape
    return pl.pallas_call(
        _pool_kernel,
        grid=(t // seq,),
        in_specs=[
            pl.BlockSpec((seq, c), lambda b: (b, 0)),
            pl.BlockSpec((c, c), lambda b: (0, 0)),
            pl.BlockSpec((1, c), lambda b: (0, 0)),
        ],
        out_specs=pl.BlockSpec((seq, c), lambda b: (b, 0)),
        out_shape=jax.ShapeDtypeStruct((t, c), BF16),
        compiler_params=_cparams(("arbitrary",)),
        name="pool_mixer",
    )(u, pool_w_bd, pool_scale.reshape(1, c))


def _attn_kernel(sink_ref, q_ref, kvc_ref, kvp_ref, bias_ref, o_ref):
    step = pl.program_id(1)
    qi = lax.broadcasted_iota(jnp.int32, (WINDOW, 2 * WINDOW), 0)
    kj = lax.broadcasted_iota(jnp.int32, (WINDOW, 2 * WINDOW), 1)
    dist = qi + WINDOW - kj
    in_band = (dist >= 0) & (dist < WINDOW)
    lo = lax.broadcasted_iota(jnp.int32, (WINDOW, LANES), 1) < HEAD_DIM
    for sub in range(ATT_BLOCKS):
        r0 = sub * WINDOW
        prev = kvp_ref[...] if sub == 0 else kvc_ref[r0 - WINDOW:r0, :]
        kv = jnp.concatenate([prev, kvc_ref[r0:r0 + WINDOW, :]], axis=0)
        k = kv[:, :KV_DIM]
        v = kv[:, KV_DIM:]
        valid = in_band & ((kj >= WINDOW) | (step > 0)) if sub == 0 else in_band
        for p in range(N_ATT_HEADS // 2):
            qp = q_ref[r0:r0 + WINDOW, p * LANES:(p + 1) * LANES]
            halves = []
            for half in range(2):
                h = p + 3 * half
                qm = jnp.where(lo if half == 0 else jnp.logical_not(lo), qp, jnp.zeros_like(qp))
                s = _dot_nt(qm, k) * (HEAD_DIM ** -0.5)
                s = jnp.where(valid, s + bias_ref[h], NEG_INF)
                sink = sink_ref[h]
                m = jnp.maximum(jnp.max(s, axis=-1, keepdims=True), sink)
                pr = jnp.exp(s - m)
                den = jnp.sum(pr, axis=-1, keepdims=True) + jnp.exp(sink - m)
                halves.append(_dot(pr.astype(BF16), v) / den)
            o_ref[r0:r0 + WINDOW, p * LANES:(p + 1) * LANES] = (
                jnp.where(lo, halves[0], halves[1]).astype(BF16))


def _swa_attention(q, kv, bias, sinks, seq):
    t = q.shape[0]
    rows = ATT_BLOCKS * WINDOW
    nblk = seq // rows
    return pl.pallas_call(
        _attn_kernel,
        grid=(t // seq, nblk),
        in_specs=[
            pl.BlockSpec(memory_space=pltpu.SMEM),
            pl.BlockSpec((rows, ATT_DIM), lambda b, n: (b * nblk + n, 0)),
            pl.BlockSpec((rows, 2 * KV_DIM), lambda b, n: (b * nblk + n, 0)),
            pl.BlockSpec((WINDOW, 2 * KV_DIM),
                         lambda b, n: (jnp.maximum((b * nblk + n) * ATT_BLOCKS - 1, 0), 0)),
            pl.BlockSpec((N_ATT_HEADS, WINDOW, 2 * WINDOW), lambda b, n: (0, 0, 0)),
        ],
        out_specs=pl.BlockSpec((rows, ATT_DIM), lambda b, n: (b * nblk + n, 0)),
        out_shape=jax.ShapeDtypeStruct((t, ATT_DIM), BF16),
        compiler_params=_cparams(("arbitrary", "arbitrary")),
        name="swa_attention",
    )(sinks, q, kv, kv, bias)


def _gdn_kernel(x_ref, halo_ref, ba_ref, cw_ref, alog_ref, dtb_ref, nw_ref, y_ref, state_ref):
    sc_id = pl.program_id(1)
    rows = GDN_SUPER
    nchunk = rows // GDN_CHUNK
    c_sz = GDN_CHUNK

    @pl.when(sc_id == 0)
    def _():
        state_ref[...] = jnp.zeros_like(state_ref)

    x = x_ref[...]
    halo = jnp.where(sc_id == 0, 0.0, halo_ref[...])
    row8 = lax.broadcasted_iota(jnp.int32, halo.shape, 0)
    acc = x * cw_ref[CONV_WIDTH - 1:CONV_WIDTH, :]
    for s in range(1, CONV_WIDTH):
        xr = pltpu.roll(x, s, axis=0)
        hr = pltpu.roll(halo, s, axis=0)
        top = jnp.where(row8 < s, hr, xr[:SUBLANES])
        xs = jnp.concatenate([top, xr[SUBLANES:]], axis=0)
        acc = acc + xs * cw_ref[CONV_WIDTH - 1 - s:CONV_WIDTH - s, :]
    act = acc * _sigmoid(acc)

    ri = lax.broadcasted_iota(jnp.int32, (rows, rows), 0)
    ci = lax.broadcasted_iota(jnp.int32, (rows, rows), 1)
    same_chunk = (ri // c_sz) == (ci // c_sz)
    incl = same_chunk & (ri >= ci)
    strict = same_chunk & (ri > ci)
    tri_incl = jnp.where(incl, 1.0, 0.0).astype(BF16)
    eye = jnp.where(ri == ci, 1.0, 0.0)
    blk = []
    bsz = SUBLANES
    while bsz <= c_sz:
        blk.append((ri // bsz) == (ci // bsz))
        bsz *= 2
    li = lax.broadcasted_iota(jnp.int32, (LANES, LANES), 0)
    lj = lax.broadcasted_iota(jnp.int32, (LANES, LANES), 1)
    half_ones = jnp.where((li // HEAD_DIM) == (lj // HEAD_DIM), 1.0, 0.0).astype(BF16)
    lane_lo = lax.broadcasted_iota(jnp.int32, (rows, LANES), 1) < HEAD_DIM
    lane_lo_c = lax.broadcasted_iota(jnp.int32, (c_sz, LANES), 1) < HEAD_DIM

    ba = ba_ref[...]
    beta_all = _sigmoid(ba)
    sp_in = ba + dtb_ref[...]
    softplus = jnp.maximum(sp_in, 0.0) + jnp.log(1.0 + jnp.exp(-jnp.abs(sp_in)))
    g_all = -jnp.exp(alog_ref[...]) * softplus
    gcum = _dot_hi_exact_rhs_lhs(tri_incl, g_all)
    gcum_t = gcum.T

    heads = range(N_GDN_HEADS)
    lane_hi = jnp.logical_not(lane_lo)
    lane_hi_c = jnp.logical_not(lane_lo_c)
    mk = [lane_lo if h % 2 == 0 else lane_hi for h in heads]
    mk_c = [lane_lo_c if h % 2 == 0 else lane_hi_c for h in heads]
    scale = HEAD_DIM ** -0.5

    def bdot(a, b):
        return _dot(a.astype(BF16), b.astype(BF16))

    xk, xq, gn, gc_col, beta, eg = [], [], [], [], [], []
    for h in heads:
        base = 4 * LANES * (h // 2) + LANES * (h % 2)
        g = act[:, base:base + LANES]
        g = g * lax.rsqrt(_dot_hi_exact_rhs(g * g, half_ones) + NORM_EPS)
        gn.append(g)
        xk.append(jnp.where(mk[h], g, 0.0))
        xq.append(jnp.where(mk[h], pltpu.roll(g, HEAD_DIM, axis=1), 0.0) * scale)
        beta.append(beta_all[:, h:h + 1])
        gc_col.append(gcum[:, N_GDN_HEADS + h:N_GDN_HEADS + h + 1])
        eg.append(jnp.exp(gc_col[h]))

    l_mat, attn, rhs = [], [], []
    for h in heads:
        gc_row = gcum_t[N_GDN_HEADS + h:N_GDN_HEADS + h + 1, :]
        decay = jnp.exp(jnp.where(incl, gc_col[h] - gc_row, -jnp.inf))
        xk_b = xk[h].astype(BF16)
        kk = _dot_nt((xk[h] * beta[h]).astype(BF16), xk_b)
        l_mat.append(jnp.where(strict, kk * decay, 0.0))
        attn.append((_dot_nt(xq[h].astype(BF16), xk_b) * decay).astype(BF16))
        vv = act[:, 4 * LANES * (h // 2) + 2 * LANES:4 * LANES * (h // 2) + 3 * LANES]
        rhs.append(jnp.where(mk[h], gn[h] * eg[h], vv) * beta[h])

    a1 = [jnp.where(blk[0], -l_mat[h], 0.0).astype(BF16) for h in heads]
    a2 = [_dot(a1[h], a1[h]).astype(BF16) for h in heads]
    a4 = [_dot(a2[h], a2[h]).astype(BF16) for h in heads]
    inv = [eye + a1[h].astype(F32) for h in heads]
    inv = [inv[h] + _dot(a2[h], inv[h].astype(BF16)) for h in heads]
    inv = [inv[h] + _dot(a4[h], inv[h].astype(BF16)) for h in heads]
    for lvl in range(1, len(blk) - 1):
        band = blk[lvl] & jnp.logical_not(blk[lvl - 1])
        inv_b = [inv[h].astype(BF16) for h in heads]
        mid = [_dot(jnp.where(band, l_mat[h], 0.0).astype(BF16), inv_b[h]) for h in heads]
        inv = [inv[h] - _dot(inv_b[h], mid[h].astype(BF16)) for h in heads]
    band = blk[-1] & jnp.logical_not(blk[-2])
    inv_b = [inv[h].astype(BF16) for h in heads]
    half = [_dot(inv_b[h], rhs[h].astype(BF16)) for h in heads]
    mid = [_dot(jnp.where(band, l_mat[h], 0.0).astype(BF16), half[h].astype(BF16)) for h in heads]
    sol = [half[h] - _dot(inv_b[h], mid[h].astype(BF16)) for h in heads]

    st = [state_ref[h] for h in heads]
    vn_parts = [[] for _ in heads]
    qs_parts = [[] for _ in heads]
    for c in range(nchunk):
        r0 = c * c_sz
        for h in heads:
            sol_c = sol[h][r0:r0 + c_sz]
            glast = gcum[r0 + c_sz - 1:r0 + c_sz, N_GDN_HEADS + h:N_GDN_HEADS + h + 1]
            kd_t = (xk[h][r0:r0 + c_sz] * jnp.exp(glast - gc_col[h][r0:r0 + c_sz])).T
            qd = xq[h][r0:r0 + c_sz] * eg[h][r0:r0 + c_sz]
            lhs = jnp.concatenate([jnp.where(mk_c[h], sol_c, 0.0), qd], axis=0)
            m1 = bdot(lhs, st[h])
            vn = sol_c - m1[:c_sz]
            vn_parts[h].append(vn)
            qs_parts[h].append(m1[c_sz:])
            st[h] = st[h] * jnp.exp(glast) + bdot(kd_t, jnp.where(mk_c[h], 0.0, vn))
    for h in heads:
        state_ref[h] = st[h]

    for p in range(N_GDN_HEADS // 2):
        o_pair = []
        for h in (2 * p, 2 * p + 1):
            vn_all = jnp.concatenate(vn_parts[h], axis=0)
            o_pair.append(jnp.concatenate(qs_parts[h], axis=0) + _dot(attn[h], vn_all.astype(BF16)))
        o = jnp.where(lane_lo, o_pair[1], o_pair[0])
        ms = _dot_hi_exact_rhs(o * o, half_ones) * (1.0 / HEAD_DIM)
        zz = act[:, 4 * LANES * p + 3 * LANES:4 * LANES * (p + 1)]
        y = o * lax.rsqrt(ms + NORM_EPS) * nw_ref[...] * zz
        y_ref[:, p * LANES:(p + 1) * LANES] = y.astype(BF16)


def _dot_hi_exact_rhs_lhs(m_bf16, x):
    hi, lo = _split_bf16(x)
    return _dot(m_bf16, hi) + _dot(m_bf16, lo)


def _gdn_mixer(gdn, ba, conv_p, alog_v, dtb_v, nw_v, seq):
    t, c = gdn.shape
    rows = GDN_SUPER
    nsc = seq // rows
    hb = rows // SUBLANES
    return pl.pallas_call(
        _gdn_kernel,
        grid=(t // seq, nsc),
        in_specs=[
            pl.BlockSpec((rows, c), lambda b, s: (b * nsc + s, 0)),
            pl.BlockSpec((SUBLANES, c), lambda b, s: (jnp.maximum((b * nsc + s) * hb - 1, 0), 0)),
            pl.BlockSpec((rows, LANES), lambda b, s: (b * nsc + s, 0)),
            pl.BlockSpec((CONV_WIDTH, c), lambda b, s: (0, 0)),
            pl.BlockSpec((1, LANES), lambda b, s: (0, 0)),
            pl.BlockSpec((1, LANES), lambda b, s: (0, 0)),
            pl.BlockSpec((1, LANES), lambda b, s: (0, 0)),
        ],
        out_specs=pl.BlockSpec((rows, GDN_DIM), lambda b, s: (b * nsc + s, 0)),
        out_shape=jax.ShapeDtypeStruct((t, GDN_DIM), BF16),
        scratch_shapes=[pltpu.VMEM((N_GDN_HEADS, LANES, LANES), F32)],
        compiler_params=_cparams(("arbitrary", "arbitrary")),
        name="gdn_mixer",
    )(gdn, gdn, ba, conv_p, alog_v, dtb_v, nw_v)


def _outproj_kernel(x_ref, mod_ref, yp_ref, ya_ref, yg_ref, wp_ref, wa_ref, wg_ref, lng_ref, lnb_ref,
                    rwh_ref, rwl_ref, rb_ref, x1_ref, h2_ref, logit_ref):
    y = _dot(yp_ref[...], wp_ref[...]) + _dot(ya_ref[...], wa_ref[...]) + _dot(yg_ref[...], wg_ref[...])
    g1 = mod_ref[0, 2:3, :]
    sh2 = mod_ref[0, 3:4, :]
    sc2 = mod_ref[0, 4:5, :]
    x1 = _layer_norm(DEEPNORM_ALPHA * x_ref[...] + g1 * y, lng_ref[...], lnb_ref[...])
    x1_ref[...] = x1
    h2 = x1 * (1.0 + sc2) + sh2
    hh, hl = _split_bf16(h2)
    h2_ref[...] = _pack_bf16_pairs(h2)
    logit_ref[...] = (_dot(hh, rwh_ref[...]) + _dot(hl, rwh_ref[...]) + _dot(hh, rwl_ref[...])
                      + rb_ref[...])


def _out_projection(x2d, mod, yp, ya, yg, w_out_p, ln_g, ln_b, rw_hi, rw_lo, rb, seq):
    t, d = x2d.shape
    tm = ROW_TILE
    wp = w_out_p[:POOL_DIM]
    wa = w_out_p[POOL_DIM:POOL_DIM + ATT_DIM]
    wg = w_out_p[POOL_DIM + ATT_DIM:]
    row = lambda i: (i, 0)
    fixed = lambda i: (0, 0)
    return pl.pallas_call(
        _outproj_kernel,
        grid=(t // tm,),
        in_specs=[
            pl.BlockSpec((tm, d), row),
            pl.BlockSpec((1, 6, d), lambda i: ((i * tm) // seq, 0, 0)),
            pl.BlockSpec((tm, POOL_DIM), row),
            pl.BlockSpec((tm, ATT_DIM), row),
            pl.BlockSpec((tm, GDN_DIM), row),
            pl.BlockSpec((POOL_DIM, d), fixed),
            pl.BlockSpec((ATT_DIM, d), fixed),
            pl.BlockSpec((GDN_DIM, d), fixed),
            pl.BlockSpec((1, d), fixed),
            pl.BlockSpec((1, d), fixed),
            pl.BlockSpec((d, LANES), fixed),
            pl.BlockSpec((d, LANES), fixed),
            pl.BlockSpec((1, LANES), fixed),
        ],
        out_specs=[pl.BlockSpec((tm, d), row), pl.BlockSpec((tm, d // 2), row), pl.BlockSpec((tm, LANES), row)],
        out_shape=[jax.ShapeDtypeStruct((t, d), F32), jax.ShapeDtypeStruct((t, d // 2), jnp.int32),
                   jax.ShapeDtypeStruct((t, LANES), F32)],
        compiler_params=_cparams(("arbitrary",)),
        name="out_proj_ln",
    )(x2d, mod, yp, ya, yg, wp, wa, wg, ln_g.reshape(1, d), ln_b.reshape(1, d), rw_hi, rw_lo, rb)


def _route_kernel(logit_ref, info_ref, dest_ref, pcum_ref, info_s, carry_ref, pstart_ref):
    phase = pl.program_id(0)
    i = pl.program_id(1)
    tm = logit_ref.shape[0]
    shape = (tm, LANES)
    lane = lax.broadcasted_iota(jnp.int32, shape, 1).astype(F32)
    rows = pl.ds(pl.multiple_of(i * tm, tm), tm)

    @pl.when((phase == 0) & (i == 0))
    def _():
        carry_ref[...] = jnp.zeros_like(carry_ref)

    @pl.when(phase == 0)
    def _():
        work = logit_ref[...]
        vals, idxs = [], []
        for _k in range(TOP_K):
            m = jnp.max(work, axis=-1, keepdims=True)
            idx = jnp.min(jnp.where(work == m, lane, float(LANES)), axis=-1, keepdims=True)
            vals.append(m)
            idxs.append(idx)
            work = jnp.where(lane == idx, -jnp.inf, work)
        exps = [jnp.exp(v - vals[0]) for v in vals]
        den = exps[0] + exps[1] + exps[2] + exps[3]
        onehots = [lane == idx for idx in idxs]
        member = jnp.zeros(shape, F32)
        for oh in onehots:
            member = member + jnp.where(oh, 1.0, 0.0)
        ri = lax.broadcasted_iota(jnp.int32, (tm, tm), 0)
        ci = lax.broadcasted_iota(jnp.int32, (tm, tm), 1)
        before = jnp.where(ri > ci, 1.0, 0.0).astype(BF16)
        rank = _dot(before, member.astype(BF16)) + carry_ref[...]
        carry_ref[...] = carry_ref[...] + jnp.sum(member, axis=0, keepdims=True)
        info = jnp.zeros(shape, F32)
        for k in range(TOP_K):
            rank_k = jnp.sum(jnp.where(onehots[k], rank, 0.0), axis=-1, keepdims=True)
            info = jnp.where(lane == float(k), idxs[k], info)
            info = jnp.where(lane == float(TOP_K + k), rank_k, info)
            info = jnp.where(lane == float(2 * TOP_K + k), exps[k] / den, info)
        info_s[rows, :] = info

    @pl.when((phase == 1) & (i == 0))
    def _():
        cnt = jnp.broadcast_to(carry_ref[...], (SUBLANES, LANES))
        padded = jnp.floor((cnt + float(EXPERT_BLOCK - 1)) * (1.0 / EXPERT_BLOCK)) * float(EXPERT_BLOCK)
        lane8 = lax.broadcasted_iota(jnp.int32, (SUBLANES, LANES), 1)
        acc = padded
        step = 1
        while step < LANES:
            acc = acc + jnp.where(lane8 >= step, pltpu.roll(acc, step, axis=1), 0.0)
            step *= 2
        pstart_ref[...] = (acc - padded)[:1]
        pcum_ref[...] = acc[:1].astype(jnp.int32)

    @pl.when(phase == 1)
    def _():
        info = info_s[rows, :]
        slots = jnp.zeros(shape, F32)
        for k in range(TOP_K):
            onehot = lane == info[:, k:k + 1]
            start = jnp.sum(jnp.where(onehot, pstart_ref[...], 0.0), axis=-1, keepdims=True)
            slots = jnp.where(lane == float(k), start + info[:, TOP_K + k:TOP_K + k + 1], slots)
        info_ref[...] = info
        dest_ref[...] = slots.T[:SUBLANES].astype(jnp.int32)


def _routing(logits):
    t = logits.shape[0]
    tm = ROUTE_TILE
    return pl.pallas_call(
        _route_kernel,
        grid=(2, t // tm),
        in_specs=[pl.BlockSpec((tm, LANES), lambda p, i: (i * (1 - p), 0))],
        out_specs=[pl.BlockSpec((tm, LANES), lambda p, i: (i * p, 0)),
                   pl.BlockSpec((SUBLANES, tm), lambda p, i: (0, i * p)),
                   pl.BlockSpec((1, LANES), lambda p, i: (0, 0))],
        out_shape=[jax.ShapeDtypeStruct((t, LANES), F32), jax.ShapeDtypeStruct((SUBLANES, t), jnp.int32),
                   jax.ShapeDtypeStruct((1, LANES), jnp.int32)],
        scratch_shapes=[pltpu.VMEM((t, LANES), F32), pltpu.VMEM((1, LANES), F32), pltpu.VMEM((1, LANES), F32)],
        compiler_params=_cparams(("arbitrary", "arbitrary")),
        name="moe_route",
    )(logits)


def _expert_kernel(e0, be_ref, nxt_ref, nu_ref, x_ref, wup_hbm, bup_ref, wdn_hbm, bdn_ref, y_ref,
                   wup_st, wdn_st, wup_bf, wdn_bf, sems):
    i = pl.program_id(0)
    e = be_ref[i]
    prev = be_ref[jnp.maximum(i - 1, 0)]
    used = i < nu_ref[0]

    def weight_copies(expert):
        return (pltpu.make_async_copy(wup_hbm.at[e0 + expert], wup_st, sems.at[0]),
                pltpu.make_async_copy(wdn_hbm.at[e0 + expert], wdn_st, sems.at[1]))

    @pl.when(i == 0)
    def _():
        for cp in weight_copies(e):
            cp.start()

    @pl.when(used & ((i == 0) | (e != prev)))
    def _():
        for cp in weight_copies(e):
            cp.wait()
        wup_bf[...] = wup_st[...].astype(BF16)
        wdn_bf[...] = wdn_st[...].astype(BF16)

        @pl.when(nxt_ref[i] >= 0)
        def _():
            for cp in weight_copies(nxt_ref[i]):
                cp.start()

    @pl.when(used)
    def _():
        xb = _unpack_bf16_pairs(x_ref[...]).astype(BF16)
        hb = _dot(xb, wup_bf[...]) + bup_ref[0]
        x_glu = jnp.minimum(hb[:, :EXPERT_DIM], SWIGLU_LIMIT)
        x_lin = jnp.clip(hb[:, EXPERT_DIM:], -SWIGLU_LIMIT, SWIGLU_LIMIT)
        act = x_glu * _sigmoid(SWIGLU_ALPHA * x_glu) * (x_lin + 1.0)
        y = _dot(act.astype(BF16), wdn_bf[...]) + bdn_ref[0]
        y_ref[...] = _pack_bf16_pairs(y)

    @pl.when(i >= nu_ref[0])
    def _():
        y_ref[...] = jnp.zeros_like(y_ref)


def _expert_ffn(xbuf, block_e, next_e, n_used, w_up, b_up, w_down, b_down, layer):
    p, dh = xbuf.shape
    d = 2 * dh
    bm = EXPERT_BLOCK
    ne, _, n_up = w_up.shape
    e0 = layer * N_EXPERTS
    grid_spec = pltpu.PrefetchScalarGridSpec(
        num_scalar_prefetch=3,
        grid=(p // bm,),
        in_specs=[
            pl.BlockSpec((bm, dh), lambda i, be, nx, nu: (i, 0)),
            pl.BlockSpec(memory_space=pl.ANY),
            pl.BlockSpec((1, 1, n_up), lambda i, be, nx, nu: (e0 + be[i], 0, 0)),
            pl.BlockSpec(memory_space=pl.ANY),
            pl.BlockSpec((1, 1, d), lambda i, be, nx, nu: (e0 + be[i], 0, 0)),
        ],
        out_specs=pl.BlockSpec((bm, dh), lambda i, be, nx, nu: (i, 0)),
        scratch_shapes=[pltpu.VMEM((d, n_up), F32), pltpu.VMEM((EXPERT_DIM, d), F32),
                        pltpu.VMEM((d, n_up), BF16), pltpu.VMEM((EXPERT_DIM, d), BF16),
                        pltpu.SemaphoreType.DMA((2,))],
    )
    return pl.pallas_call(
        functools.partial(_expert_kernel, e0),
        grid_spec=grid_spec,
        out_shape=jax.ShapeDtypeStruct((p, dh), jnp.int32),
        compiler_params=_cparams(("arbitrary",)),
        name="expert_ffn",
    )(block_e, next_e, n_used, xbuf, w_up, b_up, w_down, b_down)


def _combine_kernel(x1_ref, mod_ref, yg_ref, info_ref, lng_ref, lnb_ref, o_ref):
    info = info_ref[...]
    y = jnp.zeros(x1_ref.shape, F32)
    for k in range(TOP_K):
        gate = info[:, 2 * TOP_K + k:2 * TOP_K + k + 1]
        y = y + gate * _unpack_bf16_pairs(yg_ref[k])
    g2 = mod_ref[0, 5:6, :]
    o_ref[...] = _layer_norm(DEEPNORM_ALPHA * x1_ref[...] + g2 * y, lng_ref[...], lnb_ref[...])


def _combine(x1, mod, yg, info, ln_g, ln_b, seq):
    t, d = x1.shape
    tm = ROW_TILE
    row = lambda i: (i, 0)
    fixed = lambda i: (0, 0)
    return pl.pallas_call(
        _combine_kernel,
        grid=(t // tm,),
        in_specs=[
            pl.BlockSpec((tm, d), row),
            pl.BlockSpec((1, 6, d), lambda i: ((i * tm) // seq, 0, 0)),
            pl.BlockSpec((TOP_K, tm, d // 2), lambda i: (0, i, 0)),
            pl.BlockSpec((tm, LANES), row),
            pl.BlockSpec((1, d), fixed),
            pl.BlockSpec((1, d), fixed),
        ],
        out_specs=pl.BlockSpec((tm, d), row),
        out_shape=jax.ShapeDtypeStruct((t, d), F32),
        compiler_params=_cparams(("arbitrary",)),
        name="moe_combine_ln",
    )(x1, mod, yg, info, ln_g.reshape(1, d), ln_b.reshape(1, d))


def _sc_workers():
    info = plsc.get_sparse_core_info()
    return info.num_cores, info.num_cores * info.num_subcores


def _sc_scatter_rows(rows, idx, n_out):
    t, w = rows.shape
    kk = idx.shape[0]
    n_cores, n_workers = _sc_workers()
    ch = SC_CHUNK
    assert t % (n_workers * ch) == 0
    n_chunk = t // (n_workers * ch)
    idx_c = jnp.transpose(idx.reshape(kk, t // ch, ch), (1, 0, 2))

    @functools.partial(
        pl.kernel,
        mesh=plsc.VectorSubcoreMesh(core_axis_name="c", subcore_axis_name="s"),
        out_type=jax.ShapeDtypeStruct((n_out, w), rows.dtype),
        scratch_types=[pltpu.VMEM((kk, ch), jnp.int32), pltpu.VMEM((ch, w), rows.dtype)],
        name="sc_dispatch_scatter",
    )
    def scatter_kernel(rows_hbm, idx_hbm, out_hbm, idx_v, rows_v):
        wid = lax.axis_index("s") * n_cores + lax.axis_index("c")

        @pl.loop(0, n_chunk)
        def _(j):
            cidx = wid * n_chunk + j
            pltpu.sync_copy(idx_hbm.at[cidx], idx_v)
            pltpu.sync_copy(rows_hbm.at[pl.ds(cidx * ch, ch)], rows_v)
            for q in range(kk):
                pltpu.sync_copy(rows_v, out_hbm.at[idx_v.at[q]])

    return scatter_kernel(rows, idx_c)


def _sc_gather_rows(table, idx):
    m = idx.shape[0]
    w = table.shape[1]
    n_cores, n_workers = _sc_workers()
    ch = SC_CHUNK
    assert m % (n_workers * ch) == 0
    n_chunk = m // (n_workers * ch)
    idx_c = idx.reshape(m // ch, 1, ch)

    @functools.partial(
        pl.kernel,
        mesh=plsc.VectorSubcoreMesh(core_axis_name="c", subcore_axis_name="s"),
        out_type=jax.ShapeDtypeStruct((m, w), table.dtype),
        scratch_types=[pltpu.VMEM((1, ch), jnp.int32), pltpu.VMEM((ch, w), table.dtype)],
        name="sc_combine_gather",
    )
    def gather_kernel(table_hbm, idx_hbm, out_hbm, idx_v, rows_v):
        wid = lax.axis_index("s") * n_cores + lax.axis_index("c")

        @pl.loop(0, n_chunk)
        def _(j):
            cidx = wid * n_chunk + j
            pltpu.sync_copy(idx_hbm.at[cidx], idx_v)
            pltpu.sync_copy(table_hbm.at[idx_v.at[0]], rows_v)
            pltpu.sync_copy(rows_v, out_hbm.at[pl.ds(cidx * ch, ch)])

    return gather_kernel(table, idx_c)


def _lane_vector(vals, offset):
    return jnp.zeros((1, LANES), F32).at[0, offset:offset + vals.shape[0]].set(vals.astype(F32))


def _moe(h2, logits, x1, mod, ln_g, ln_b, w_up, b_up, w_down, b_down, layer, seq):
    t, dh = h2.shape
    a = t * TOP_K
    bm = EXPERT_BLOCK
    info, slots, pcum_v = _routing(logits)
    pcum = pcum_v[0, :N_EXPERTS]
    dest = slots[:TOP_K]
    n_blocks = -(-a // bm) + N_EXPERTS
    block_e = jnp.minimum(jnp.sum(pcum[None, :] <= (jnp.arange(n_blocks) * bm)[:, None], axis=1),
                          N_EXPERTS - 1).astype(jnp.int32)
    n_used = (pcum[-1] // bm).astype(jnp.int32).reshape(1)
    group_end = jnp.sum(block_e[None, :] <= block_e[:, None], axis=1)
    next_e = jnp.where(group_end < n_used[0], block_e[jnp.minimum(group_end, n_blocks - 1)], -1).astype(jnp.int32)
    xbuf = _sc_scatter_rows(h2, dest, n_blocks * bm)
    ybuf = _expert_ffn(xbuf, block_e, next_e, n_used, w_up, b_up, w_down, b_down, layer)
    yg = _sc_gather_rows(ybuf, dest.reshape(a)).reshape(TOP_K, t, dh)
    return _combine(x1, mod, yg, info, ln_g, ln_b, seq)


def kernel(x, c, rel_bias, w_in, w_out, w_ada, b_ada, ln1_g, ln1_b, ln2_g, ln2_b, pool_w, pool_scale,
           attn_sinks, conv_w, gdn_a_log, gdn_dt_bias, gdn_norm_w, router_w, router_b,
           exp_w_up, exp_b_up, exp_w_down, exp_b_down):
    bsz, seq, d = x.shape
    depth = w_in.shape[0]
    t = bsz * seq
    assert d == D_MODEL and w_in.shape[2] == IN_DIM
    assert seq % GDN_SUPER == 0 and seq % (ATT_BLOCKS * WINDOW) == 0
    assert t % ROW_TILE == 0 and seq % ROW_TILE == 0

    mod_all = _modulation(c, w_ada, b_ada).reshape(depth, bsz, 6, d)
    bias = _band_bias(rel_bias)

    w_up_all = exp_w_up.reshape((depth * N_EXPERTS,) + exp_w_up.shape[2:])
    b_up_all = exp_b_up.reshape(depth * N_EXPERTS, 1, exp_b_up.shape[2])
    w_down_all = exp_w_down.reshape((depth * N_EXPERTS,) + exp_w_down.shape[2:])
    b_down_all = exp_b_down.reshape(depth * N_EXPERTS, 1, exp_b_down.shape[2])

    w_in_all = _take_cols(w_in, _IN_PERM).astype(BF16)

    x2d = x.reshape(t, d)
    for l in range(depth):
        mod = mod_all[l]
        w_out_p = _take_static(w_out[l], _OUT_PERM, 0).astype(BF16)
        ident = jnp.zeros((CONV_WIDTH, 1), F32).at[CONV_WIDTH - 1, 0].set(1.0)
        conv_p = jnp.where(jnp.asarray(_GDN_CONV_SRC >= 0), _take_cols(conv_w[l].astype(F32), _GDN_CONV_SRC),
                           ident)
        pool_bd = jnp.zeros((POOL_DIM, POOL_DIM), F32)
        for gi in range(len(POOL_WINDOWS)):
            sl = slice(gi * POOL_GROUP, (gi + 1) * POOL_GROUP)
            pool_bd = pool_bd.at[sl, sl].set(pool_w[l, gi].astype(F32))
        alog_v = _lane_vector(gdn_a_log[l], N_GDN_HEADS)
        dtb_v = _lane_vector(gdn_dt_bias[l], N_GDN_HEADS)
        nw_v = jnp.tile(gdn_norm_w[l].astype(F32), 2).reshape(1, LANES)
        rw = jnp.zeros((d, LANES), F32).at[:, :N_EXPERTS].set(router_w[l].astype(F32))
        rw_hi, rw_lo = _split_bf16(rw)
        rb = jnp.full((1, LANES), NEG_INF, F32).at[0, :N_EXPERTS].set(router_b[l].astype(F32))

        u_pool, aq, akv, gdn, ba = _in_projection(x2d, mod, w_in_all, l, seq)
        y_pool = _pool_mixer(u_pool, pool_bd.astype(BF16), pool_scale[l].astype(F32), seq)
        y_att = _swa_attention(aq, akv, bias, attn_sinks[l].astype(F32), seq)
        y_gdn = _gdn_mixer(gdn, ba, conv_p, alog_v, dtb_v, nw_v, seq)
        x1, h2, logits = _out_projection(x2d, mod, y_pool, y_att, y_gdn, w_out_p, ln1_g[l], ln1_b[l],
                                         rw_hi, rw_lo, rb, seq)
        x2d = _moe(h2, logits, x1, mod, ln2_g[l], ln2_b[l], w_up_all, b_up_all, w_down_all, b_down_all,
                   l, seq)
    return x2d.reshape(bsz, seq, d)
```

```python
import functools

import numpy as np
import jax
import jax.numpy as jnp
from jax import lax
from jax.experimental import pallas as pl
from jax.experimental.pallas import tpu as pltpu
from jax.experimental.pallas import tpu_sc as plsc

F32 = jnp.float32
BF16 = jnp.bfloat16

D_MODEL = 1024
HEAD_DIM = 64
POOL_DIM = 256
POOL_WINDOWS = (2, 4, 8, 16)
POOL_GROUP = 64
N_ATT_HEADS = 6
N_KV_HEADS = 2
ATT_DIM = 384
KV_DIM = 128
WINDOW = 128
N_BUCKETS = 32
MAX_DISTANCE = 128
N_GDN_HEADS = 6
GDN_DIM = 384
CONV_WIDTH = 4
GDN_CHUNK = 64
N_EXPERTS = 32
TOP_K = 4
EXPERT_DIM = 1024
SWIGLU_ALPHA = 1.702
SWIGLU_LIMIT = 7.0
DEPTH = 2
DEEPNORM_ALPHA = (2 * DEPTH) ** 0.25
LN_EPS = 1e-5
NORM_EPS = 1e-6
NEG_INF = -1e30

LANES = 128
SUBLANES = 8
VMEM_LIMIT = 56 * 1024 * 1024

ROW_TILE = 512
ATT_BLOCKS = 2
GDN_SUPER = 256
ROUTE_TILE = 2048
EXPERT_BLOCK = 512
SC_CHUNK = 64

_OFF_AQ = POOL_DIM
_OFF_AK = _OFF_AQ + ATT_DIM
_OFF_AV = _OFF_AK + KV_DIM
_OFF_GQ = _OFF_AV + KV_DIM
_OFF_GK = _OFF_GQ + GDN_DIM
_OFF_GV = _OFF_GK + GDN_DIM
_OFF_GZ = _OFF_GV + GDN_DIM
_OFF_GB = _OFF_GZ + GDN_DIM
_OFF_GA = _OFF_GB + N_GDN_HEADS
IN_DIM = _OFF_GA + N_GDN_HEADS

P_POOL = (0, POOL_DIM)
P_Q = (P_POOL[1], P_POOL[1] + ATT_DIM)
P_KV = (P_Q[1], P_Q[1] + 2 * KV_DIM)
P_GDN = (P_KV[1], P_KV[1] + 4 * GDN_DIM)
P_BA = (P_GDN[1], P_GDN[1] + LANES)
P_TOTAL = P_BA[1]


def _head_cols(off, h):
    return list(range(off + HEAD_DIM * h, off + HEAD_DIM * (h + 1)))


def _build_in_perm():
    cols = list(range(POOL_DIM))
    for p in range(N_ATT_HEADS // 2):
        cols += _head_cols(_OFF_AQ, p) + _head_cols(_OFF_AQ, p + 3)
    cols += list(range(_OFF_AK, _OFF_AK + 2 * KV_DIM))
    gdn_src = []
    for p in range(N_GDN_HEADS // 2):
        e, o = 2 * p, 2 * p + 1
        grp = (_head_cols(_OFF_GK, e) + _head_cols(_OFF_GQ, e)
               + _head_cols(_OFF_GQ, o) + _head_cols(_OFF_GK, o)
               + _head_cols(_OFF_GV, o) + _head_cols(_OFF_GV, e)
               + _head_cols(_OFF_GZ, o) + _head_cols(_OFF_GZ, e))
        cols += grp
        gdn_src += [c - _OFF_GQ if c < _OFF_GZ else -1 for c in grp]
    cols += list(range(_OFF_GB, _OFF_GB + 2 * N_GDN_HEADS))
    cols += [-1] * (LANES - 2 * N_GDN_HEADS)
    assert len(cols) == P_TOTAL
    return np.asarray(cols, np.int32), np.asarray(gdn_src, np.int32)


_IN_PERM, _GDN_CONV_SRC = _build_in_perm()


def _build_out_perm():
    rows = list(range(POOL_DIM))
    for p in range(N_ATT_HEADS // 2):
        rows += _head_cols(POOL_DIM, p) + _head_cols(POOL_DIM, p + 3)
    for p in range(N_GDN_HEADS // 2):
        rows += _head_cols(POOL_DIM + ATT_DIM, 2 * p + 1) + _head_cols(POOL_DIM + ATT_DIM, 2 * p)
    return np.asarray(rows, np.int32)


_OUT_PERM = _build_out_perm()


def _t5_bucket_line():
    n = np.maximum(2 * WINDOW - 1 - np.arange(3 * WINDOW - 1), 0)
    max_exact = N_BUCKETS // 2
    nf = np.maximum(n, 1).astype(np.float32)
    large = max_exact + (np.log(nf / max_exact) / np.float32(np.log(MAX_DISTANCE / max_exact))
                         * (N_BUCKETS - max_exact)).astype(np.int32)
    large = np.minimum(large, N_BUCKETS - 1)
    return np.where(n < max_exact, n, large).astype(np.int32)


_BUCKET_LINE = _t5_bucket_line()


def _band_bias(rel_bias):
    n_line = 3 * WINDOW - 1
    line = jnp.take(rel_bias.astype(F32), jnp.asarray(_BUCKET_LINE), axis=0).T
    heads = line.shape[0]
    padded = jnp.concatenate([line, jnp.zeros((heads, 1), F32)], axis=1)
    skew = jnp.tile(padded, (1, WINDOW))[:, :WINDOW * n_line].reshape(heads, WINDOW, n_line)
    return skew[:, :, WINDOW - 1:3 * WINDOW - 1]


def _take_static(w, perm, axis):
    parts = []
    start = 0
    for i in range(1, len(perm) + 1):
        run_ends = (i == len(perm) or ((perm[i] < 0) != (perm[i - 1] < 0))
                    or (perm[i] >= 0 and perm[i] != perm[i - 1] + 1))
        if run_ends:
            if perm[start] < 0:
                shape = list(w.shape)
                shape[axis] = i - start
                parts.append(jnp.zeros(shape, w.dtype))
            else:
                parts.append(lax.slice_in_dim(w, int(perm[start]), int(perm[start]) + (i - start), axis=axis))
            start = i
    return jnp.concatenate(parts, axis=axis)


def _take_cols(w, perm):
    return _take_static(w, perm, w.ndim - 1)


def _split_bf16(x):
    hi = x.astype(BF16)
    lo = (x - hi.astype(F32)).astype(BF16)
    return hi, lo


def _pack_bf16_pairs(x):
    n = x.shape[1] // 2
    bits = pltpu.bitcast(x.astype(BF16).astype(F32), jnp.int32)
    return lax.shift_right_logical(bits[:, :n], 16) | bits[:, n:]


def _unpack_bf16_pairs(u):
    lo = pltpu.bitcast(lax.shift_left(u, 16), F32)
    hi = pltpu.bitcast(u & jnp.int32(-65536), F32)
    return jnp.concatenate([lo, hi], axis=1)


def _dot(a, b):
    return jnp.dot(a, b, preferred_element_type=F32)


def _dot_nt(a, b):
    return lax.dot_general(a, b, (((1,), (1,)), ((), ())), preferred_element_type=F32)


def _dot_hi_exact_rhs(x, m_bf16):
    hi, lo = _split_bf16(x)
    return _dot(hi, m_bf16) + _dot(lo, m_bf16)


def _sigmoid(x):
    return 1.0 / (1.0 + jnp.exp(-x))


def _layer_norm(r, g, b):
    mu = jnp.mean(r, axis=-1, keepdims=True)
    d = r - mu
    var = jnp.mean(d * d, axis=-1, keepdims=True)
    return d * lax.rsqrt(var + LN_EPS) * g + b


def _cparams(sem):
    return pltpu.CompilerParams(dimension_semantics=sem, vmem_limit_bytes=VMEM_LIMIT)


def _mod_kernel(c_ref, w_ref, b_ref, o_ref):
    c = c_ref[...]
    ca = c * _sigmoid(c)
    ch, cl = _split_bf16(ca)
    wh, wl = _split_bf16(w_ref[0])
    o_ref[0] = _dot(ch, wh) + _dot(cl, wh) + _dot(ch, wl) + b_ref[0]


def _modulation(c, w_ada, b_ada):
    depth, d, n = w_ada.shape
    bsz = c.shape[0]
    tn = 512
    return pl.pallas_call(
        _mod_kernel,
        grid=(depth, n // tn),
        in_specs=[
            pl.BlockSpec((bsz, d), lambda l, j: (0, 0)),
            pl.BlockSpec((1, d, tn), lambda l, j: (l, 0, j)),
            pl.BlockSpec((1, 1, tn), lambda l, j: (l, 0, j)),
        ],
        out_specs=pl.BlockSpec((1, bsz, tn), lambda l, j: (l, 0, j)),
        out_shape=jax.ShapeDtypeStruct((depth, bsz, n), F32),
        compiler_params=_cparams(("arbitrary", "arbitrary")),
        name="adaln_mod",
    )(c, w_ada, b_ada.reshape(depth, 1, n))


def _inproj_kernel(x_ref, mod_ref, w_ref, pool_ref, q_ref, kv_ref, gdn_ref, ba_ref):
    sh = mod_ref[0, 0:1, :]
    sc = mod_ref[0, 1:2, :]
    h = (x_ref[...] * (1.0 + sc) + sh).astype(BF16)

    def mm(rng):
        return _dot(h, w_ref[0, :, rng[0]:rng[1]])

    pool_ref[...] = mm(P_POOL)
    q_ref[...] = mm(P_Q).astype(BF16)
    kv_ref[...] = mm(P_KV).astype(BF16)
    gdn_ref[...] = mm(P_GDN)
    ba_ref[...] = mm(P_BA)


def _in_projection(x2d, mod, w_in_all, layer, seq):
    t, d = x2d.shape
    tm = ROW_TILE
    widths = [r[1] - r[0] for r in (P_POOL, P_Q, P_KV, P_GDN, P_BA)]
    dtypes = [F32, BF16, BF16, F32, F32]
    return pl.pallas_call(
        _inproj_kernel,
        grid=(t // tm,),
        in_specs=[
            pl.BlockSpec((tm, d), lambda i: (i, 0)),
            pl.BlockSpec((1, 6, d), lambda i: ((i * tm) // seq, 0, 0)),
            pl.BlockSpec((1, d, P_TOTAL), lambda i: (layer, 0, 0)),
        ],
        out_specs=[pl.BlockSpec((tm, w), lambda i: (i, 0)) for w in widths],
        out_shape=[jax.ShapeDtypeStruct((t, w), dt) for w, dt in zip(widths, dtypes)],
        compiler_params=_cparams(("arbitrary",)),
        name="in_proj",
    )(x2d, mod, w_in_all)


def _pool_kernel(u_ref, w_ref, scale_ref, o_ref):
    u = u_ref[...]
    row = lax.broadcasted_iota(jnp.int32, u.shape, 0)
    lane = lax.broadcasted_iota(jnp.int32, u.shape, 1)

    def shifted(a, s):
        return jnp.where(row >= s, pltpu.roll(a, s, axis=0), 0.0)

    sums = []
    acc = u
    for wdt in POOL_WINDOWS:
        acc = acc + shifted(acc, wdt // 2)
        sums.append(acc)
    grp = lane // POOL_GROUP
    wsum = sums[-1]
    win = jnp.full(u.shape, POOL_WINDOWS[-1], jnp.int32)
    for gi in range(len(POOL_WINDOWS) - 2, -1, -1):
        wsum = jnp.where(grp == gi, sums[gi], wsum)
        win = jnp.where(grp == gi, POOL_WINDOWS[gi], win)
    cnt = jnp.minimum(row + 1, win).astype(F32)
    p = wsum / cnt - u
    y = _dot(p.astype(BF16), w_ref[...]) * scale_ref[...]
    o_ref[...] = y.astype(BF16)


def _pool_mixer(u, pool_w_bd, pool_scale, seq):
    t, c = u.shape
    return pl.pallas_call(
        _pool_kernel,
        grid=(t // seq,),
        in_specs=[
            pl.BlockSpec((seq, c), lambda b: (b, 0)),
            pl.BlockSpec((c, c), lambda b: (0, 0)),
            pl.BlockSpec((1, c), lambda b: (0, 0)),
        ],
        out_specs=pl.BlockSpec((seq, c), lambda b: (b, 0)),
        out_shape=jax.ShapeDtypeStruct((t, c), BF16),
        compiler_params=_cparams(("arbitrary",)),
        name="pool_mixer",
    )(u, pool_w_bd, pool_scale.reshape(1, c))


def _attn_kernel(sink_ref, q_ref, kvc_ref, kvp_ref, bias_ref, o_ref):
    step = pl.program_id(1)
    qi = lax.broadcasted_iota(jnp.int32, (WINDOW, 2 * WINDOW), 0)
    kj = lax.broadcasted_iota(jnp.int32, (WINDOW, 2 * WINDOW), 1)
    dist = qi + WINDOW - kj
    in_band = (dist >= 0) & (dist < WINDOW)
    lo = lax.broadcasted_iota(jnp.int32, (WINDOW, LANES), 1) < HEAD_DIM
    for sub in range(ATT_BLOCKS):
        r0 = sub * WINDOW
        prev = kvp_ref[...] if sub == 0 else kvc_ref[r0 - WINDOW:r0, :]
        kv = jnp.concatenate([prev, kvc_ref[r0:r0 + WINDOW, :]], axis=0)
        k = kv[:, :KV_DIM]
        v = kv[:, KV_DIM:]
        valid = in_band & ((kj >= WINDOW) | (step > 0)) if sub == 0 else in_band
        for p in range(N_ATT_HEADS // 2):
            qp = q_ref[r0:r0 + WINDOW, p * LANES:(p + 1) * LANES]
            halves = []
            for half in range(2):
                h = p + 3 * half
                qm = jnp.where(lo if half == 0 else jnp.logical_not(lo), qp, jnp.zeros_like(qp))
                s = _dot_nt(qm, k) * (HEAD_DIM ** -0.5)
                s = jnp.where(valid, s + bias_ref[h], NEG_INF)
                sink = sink_ref[h]
                m = jnp.maximum(jnp.max(s, axis=-1, keepdims=True), sink)
                pr = jnp.exp(s - m)
                den = jnp.sum(pr, axis=-1, keepdims=True) + jnp.exp(sink - m)
                halves.append(_dot(pr.astype(BF16), v) / den)
            o_ref[r0:r0 + WINDOW, p * LANES:(p + 1) * LANES] = (
                jnp.where(lo, halves[0], halves[1]).astype(BF16))


def _swa_attention(q, kv, bias, sinks, seq):
    t = q.shape[0]
    rows = ATT_BLOCKS * WINDOW
    nblk = seq // rows
    return pl.pallas_call(
        _attn_kernel,
        grid=(t // seq, nblk),
        in_specs=[
            pl.BlockSpec(memory_space=pltpu.SMEM),
            pl.BlockSpec((rows, ATT_DIM), lambda b, n: (b * nblk + n, 0)),
            pl.BlockSpec((rows, 2 * KV_DIM), lambda b, n: (b * nblk + n, 0)),
            pl.BlockSpec((WINDOW, 2 * KV_DIM),
                         lambda b, n: (jnp.maximum((b * nblk + n) * ATT_BLOCKS - 1, 0), 0)),
            pl.BlockSpec((N_ATT_HEADS, WINDOW, 2 * WINDOW), lambda b, n: (0, 0, 0)),
        ],
        out_specs=pl.BlockSpec((rows, ATT_DIM), lambda b, n: (b * nblk + n, 0)),
        out_shape=jax.ShapeDtypeStruct((t, ATT_DIM), BF16),
        compiler_params=_cparams(("arbitrary", "arbitrary")),
        name="swa_attention",
    )(sinks, q, kv, kv, bias)


def _gdn_kernel(x_ref, halo_ref, ba_ref, cw_ref, alog_ref, dtb_ref, nw_ref, y_ref, state_ref):
    sc_id = pl.program_id(1)
    rows = GDN_SUPER
    nchunk = rows // GDN_CHUNK
    c_sz = GDN_CHUNK

    @pl.when(sc_id == 0)
    def _():
        state_ref[...] = jnp.zeros_like(state_ref)

    x = x_ref[...]
    halo = jnp.where(sc_id == 0, 0.0, halo_ref[...])
    row8 = lax.broadcasted_iota(jnp.int32, halo.shape, 0)
    acc = x * cw_ref[CONV_WIDTH - 1:CONV_WIDTH, :]
    for s in range(1, CONV_WIDTH):
        xr = pltpu.roll(x, s, axis=0)
        hr = pltpu.roll(halo, s, axis=0)
        top = jnp.where(row8 < s, hr, xr[:SUBLANES])
        xs = jnp.concatenate([top, xr[SUBLANES:]], axis=0)
        acc = acc + xs * cw_ref[CONV_WIDTH - 1 - s:CONV_WIDTH - s, :]
    act = acc * _sigmoid(acc)

    ri = lax.broadcasted_iota(jnp.int32, (rows, rows), 0)
    ci = lax.broadcasted_iota(jnp.int32, (rows, rows), 1)
    same_chunk = (ri // c_sz) == (ci // c_sz)
    incl = same_chunk & (ri >= ci)
    strict = same_chunk & (ri > ci)
    tri_incl = jnp.where(incl, 1.0, 0.0).astype(BF16)
    eye = jnp.where(ri == ci, 1.0, 0.0)
    blk = []
    bsz = SUBLANES
    while bsz <= c_sz:
        blk.append((ri // bsz) == (ci // bsz))
        bsz *= 2
    li = lax.broadcasted_iota(jnp.int32, (LANES, LANES), 0)
    lj = lax.broadcasted_iota(jnp.int32, (LANES, LANES), 1)
    half_ones = jnp.where((li // HEAD_DIM) == (lj // HEAD_DIM), 1.0, 0.0).astype(BF16)
    lane_lo = lax.broadcasted_iota(jnp.int32, (rows, LANES), 1) < HEAD_DIM
    lane_lo_c = lax.broadcasted_iota(jnp.int32, (c_sz, LANES), 1) < HEAD_DIM

    ba = ba_ref[...]
    beta_all = _sigmoid(ba)
    sp_in = ba + dtb_ref[...]
    softplus = jnp.maximum(sp_in, 0.0) + jnp.log(1.0 + jnp.exp(-jnp.abs(sp_in)))
    g_all = -jnp.exp(alog_ref[...]) * softplus
    gcum = _dot_hi_exact_rhs_lhs(tri_incl, g_all)
    gcum_t = gcum.T

    heads = range(N_GDN_HEADS)
    lane_hi = jnp.logical_not(lane_lo)
    lane_hi_c = jnp.logical_not(lane_lo_c)
    mk = [lane_lo if h % 2 == 0 else lane_hi for h in heads]
    mk_c = [lane_lo_c if h % 2 == 0 else lane_hi_c for h in heads]
    scale = HEAD_DIM ** -0.5

    def bdot(a, b):
        return _dot(a.astype(BF16), b.astype(BF16))

    xk, xq, gn, gc_col, beta, eg = [], [], [], [], [], []
    for h in heads:
        base = 4 * LANES * (h // 2) + LANES * (h % 2)
        g = act[:, base:base + LANES]
        g = g * lax.rsqrt(_dot_hi_exact_rhs(g * g, half_ones) + NORM_EPS)
        gn.append(g)
        xk.append(jnp.where(mk[h], g, 0.0))
        xq.append(jnp.where(mk[h], pltpu.roll(g, HEAD_DIM, axis=1), 0.0) * scale)
        beta.append(beta_all[:, h:h + 1])
        gc_col.append(gcum[:, N_GDN_HEADS + h:N_GDN_HEADS + h + 1])
        eg.append(jnp.exp(gc_col[h]))

    l_mat, attn, rhs = [], [], []
    for h in heads:
        gc_row = gcum_t[N_GDN_HEADS + h:N_GDN_HEADS + h + 1, :]
        decay = jnp.exp(jnp.where(incl, gc_col[h] - gc_row, -jnp.inf))
        xk_b = xk[h].astype(BF16)
        kk = _dot_nt((xk[h] * beta[h]).astype(BF16), xk_b)
        l_mat.append(jnp.where(strict, kk * decay, 0.0))
        attn.append((_dot_nt(xq[h].astype(BF16), xk_b) * decay).astype(BF16))
        vv = act[:, 4 * LANES * (h // 2) + 2 * LANES:4 * LANES * (h // 2) + 3 * LANES]
        rhs.append(jnp.where(mk[h], gn[h] * eg[h], vv) * beta[h])

    a1 = [jnp.where(blk[0], -l_mat[h], 0.0).astype(BF16) for h in heads]
    a2 = [_dot(a1[h], a1[h]).astype(BF16) for h in heads]
    a4 = [_dot(a2[h], a2[h]).astype(BF16) for h in heads]
    inv = [eye + a1[h].astype(F32) for h in heads]
    inv = [inv[h] + _dot(a2[h], inv[h].astype(BF16)) for h in heads]
    inv = [inv[h] + _dot(a4[h], inv[h].astype(BF16)) for h in heads]
    for lvl in range(1, len(blk) - 1):
        band = blk[lvl] & jnp.logical_not(blk[lvl - 1])
        inv_b = [inv[h].astype(BF16) for h in heads]
        mid = [_dot(jnp.where(band, l_mat[h], 0.0).astype(BF16), inv_b[h]) for h in heads]
        inv = [inv[h] - _dot(inv_b[h], mid[h].astype(BF16)) for h in heads]
    band = blk[-1] & jnp.logical_not(blk[-2])
    inv_b = [inv[h].astype(BF16) for h in heads]
    half = [_dot(inv_b[h], rhs[h].astype(BF16)) for h in heads]
    mid = [_dot(jnp.where(band, l_mat[h], 0.0).astype(BF16), half[h].astype(BF16)) for h in heads]
    sol = [half[h] - _dot(inv_b[h], mid[h].astype(BF16)) for h in heads]

    st = [state_ref[h] for h in heads]
    vn_parts = [[] for _ in heads]
    qs_parts = [[] for _ in heads]
    for c in range(nchunk):
        r0 = c * c_sz
        for h in heads:
            sol_c = sol[h][r0:r0 + c_sz]
            glast = gcum[r0 + c_sz - 1:r0 + c_sz, N_GDN_HEADS + h:N_GDN_HEADS + h + 1]
            kd_t = (xk[h][r0:r0 + c_sz] * jnp.exp(glast - gc_col[h][r0:r0 + c_sz])).T
            qd = xq[h][r0:r0 + c_sz] * eg[h][r0:r0 + c_sz]
            lhs = jnp.concatenate([jnp.where(mk_c[h], sol_c, 0.0), qd], axis=0)
            m1 = bdot(lhs, st[h])
            vn = sol_c - m1[:c_sz]
            vn_parts[h].append(vn)
            qs_parts[h].append(m1[c_sz:])
            st[h] = st[h] * jnp.exp(glast) + bdot(kd_t, jnp.where(mk_c[h], 0.0, vn))
    for h in heads:
        state_ref[h] = st[h]

    for p in range(N_GDN_HEADS // 2):
        o_pair = []
        for h in (2 * p, 2 * p + 1):
            vn_all = jnp.concatenate(vn_parts[h], axis=0)
            o_pair.append(jnp.concatenate(qs_parts[h], axis=0) + _dot(attn[h], vn_all.astype(BF16)))
        o = jnp.where(lane_lo, o_pair[1], o_pair[0])
        ms = _dot_hi_exact_rhs(o * o, half_ones) * (1.0 / HEAD_DIM)
        zz = act[:, 4 * LANES * p + 3 * LANES:4 * LANES * (p + 1)]
        y = o * lax.rsqrt(ms + NORM_EPS) * nw_ref[...] * zz
        y_ref[:, p * LANES:(p + 1) * LANES] = y.astype(BF16)


def _dot_hi_exact_rhs_lhs(m_bf16, x):
    hi, lo = _split_bf16(x)
    return _dot(m_bf16, hi) + _dot(m_bf16, lo)


def _gdn_mixer(gdn, ba, conv_p, alog_v, dtb_v, nw_v, seq):
    t, c = gdn.shape
    rows = GDN_SUPER
    nsc = seq // rows
    hb = rows // SUBLANES
    return pl.pallas_call(
        _gdn_kernel,
        grid=(t // seq, nsc),
        in_specs=[
            pl.BlockSpec((rows, c), lambda b, s: (b * nsc + s, 0)),
            pl.BlockSpec((SUBLANES, c), lambda b, s: (jnp.maximum((b * nsc + s) * hb - 1, 0), 0)),
            pl.BlockSpec((rows, LANES), lambda b, s: (b * nsc + s, 0)),
            pl.BlockSpec((CONV_WIDTH, c), lambda b, s: (0, 0)),
            pl.BlockSpec((1, LANES), lambda b, s: (0, 0)),
            pl.BlockSpec((1, LANES), lambda b, s: (0, 0)),
            pl.BlockSpec((1, LANES), lambda b, s: (0, 0)),
        ],
        out_specs=pl.BlockSpec((rows, GDN_DIM), lambda b, s: (b * nsc + s, 0)),
        out_shape=jax.ShapeDtypeStruct((t, GDN_DIM), BF16),
        scratch_shapes=[pltpu.VMEM((N_GDN_HEADS, LANES, LANES), F32)],
        compiler_params=_cparams(("arbitrary", "arbitrary")),
        name="gdn_mixer",
    )(gdn, gdn, ba, conv_p, alog_v, dtb_v, nw_v)


def _route_tile(logits, before, carry_ref):
    shape = logits.shape
    lane = lax.broadcasted_iota(jnp.int32, shape, 1).astype(F32)
    work = logits
    vals, idxs = [], []
    for _k in range(TOP_K):
        m = jnp.max(work, axis=-1, keepdims=True)
        idx = jnp.min(jnp.where(work == m, lane, float(LANES)), axis=-1, keepdims=True)
        vals.append(m)
        idxs.append(idx)
        work = jnp.where(lane == idx, -jnp.inf, work)
    exps = [jnp.exp(v - vals[0]) for v in vals]
    den = exps[0] + exps[1] + exps[2] + exps[3]
    onehots = [lane == idx for idx in idxs]
    member = jnp.zeros(shape, F32)
    for oh in onehots:
        member = member + jnp.where(oh, 1.0, 0.0)
    rank = _dot(before, member.astype(BF16)) + carry_ref[...]
    carry_ref[...] = carry_ref[...] + jnp.sum(member, axis=0, keepdims=True)
    info = jnp.zeros(shape, F32)
    for k in range(TOP_K):
        rank_k = jnp.sum(jnp.where(onehots[k], rank, 0.0), axis=-1, keepdims=True)
        info = jnp.where(lane == float(k), idxs[k], info)
        info = jnp.where(lane == float(TOP_K + k), rank_k, info)
        info = jnp.where(lane == float(2 * TOP_K + k), exps[k] / den, info)
    return info


def _outproj_kernel(x_ref, mod_ref, yp_ref, ya_ref, yg_ref, wp_ref, wa_ref, wg_ref, lng_ref, lnb_ref,
                    rwh_ref, rwl_ref, rb_ref, before_ref, x1_ref, h2_ref, info_ref, cnt_ref, carry_ref):
    @pl.when(pl.program_id(0) == 0)
    def _():
        carry_ref[...] = jnp.zeros_like(carry_ref)

    y = _dot(yp_ref[...], wp_ref[...]) + _dot(ya_ref[...], wa_ref[...]) + _dot(yg_ref[...], wg_ref[...])
    g1 = mod_ref[0, 2:3, :]
    sh2 = mod_ref[0, 3:4, :]
    sc2 = mod_ref[0, 4:5, :]
    x1 = _layer_norm(DEEPNORM_ALPHA * x_ref[...] + g1 * y, lng_ref[...], lnb_ref[...])
    x1_ref[...] = x1
    h2 = x1 * (1.0 + sc2) + sh2
    hh, hl = _split_bf16(h2)
    h2_ref[...] = _pack_bf16_pairs(h2)
    logits = _dot(hh, rwh_ref[...]) + _dot(hl, rwh_ref[...]) + _dot(hh, rwl_ref[...]) + rb_ref[...]
    info_ref[...] = _route_tile(logits, before_ref[...], carry_ref)
    cnt_ref[...] = carry_ref[...]


def _out_projection(x2d, mod, yp, ya, yg, w_out_p, ln_g, ln_b, rw_hi, rw_lo, rb, seq):
    t, d = x2d.shape
    tm = ROW_TILE
    wp = w_out_p[:POOL_DIM]
    wa = w_out_p[POOL_DIM:POOL_DIM + ATT_DIM]
    wg = w_out_p[POOL_DIM + ATT_DIM:]
    row = lambda i: (i, 0)
    fixed = lambda i: (0, 0)
    return pl.pallas_call(
        _outproj_kernel,
        grid=(t // tm,),
        in_specs=[
            pl.BlockSpec((tm, d), row),
            pl.BlockSpec((1, 6, d), lambda i: ((i * tm) // seq, 0, 0)),
            pl.BlockSpec((tm, POOL_DIM), row),
            pl.BlockSpec((tm, ATT_DIM), row),
            pl.BlockSpec((tm, GDN_DIM), row),
            pl.BlockSpec((POOL_DIM, d), fixed),
            pl.BlockSpec((ATT_DIM, d), fixed),
            pl.BlockSpec((GDN_DIM, d), fixed),
            pl.BlockSpec((1, d), fixed),
            pl.BlockSpec((1, d), fixed),
            pl.BlockSpec((d, LANES), fixed),
            pl.BlockSpec((d, LANES), fixed),
            pl.BlockSpec((1, LANES), fixed),
            pl.BlockSpec((tm, tm), fixed),
        ],
        out_specs=[pl.BlockSpec((tm, d), row), pl.BlockSpec((tm, d // 2), row), pl.BlockSpec((tm, LANES), row),
                   pl.BlockSpec((1, LANES), fixed)],
        out_shape=[jax.ShapeDtypeStruct((t, d), F32), jax.ShapeDtypeStruct((t, d // 2), jnp.int32),
                   jax.ShapeDtypeStruct((t, LANES), F32), jax.ShapeDtypeStruct((1, LANES), F32)],
        scratch_shapes=[pltpu.VMEM((1, LANES), F32)],
        compiler_params=_cparams(("arbitrary",)),
        name="out_proj_ln_route",
    )(x2d, mod, yp, ya, yg, wp, wa, wg, ln_g.reshape(1, d), ln_b.reshape(1, d), rw_hi, rw_lo, rb,
      jnp.tril(jnp.ones((tm, tm), BF16), -1))


def _slot_kernel(info_ref, cnt_ref, dest_ref, pcum_ref, pstart_ref):
    shape = info_ref.shape
    lane = lax.broadcasted_iota(jnp.int32, shape, 1).astype(F32)

    @pl.when(pl.program_id(0) == 0)
    def _():
        cnt = jnp.broadcast_to(cnt_ref[...], (SUBLANES, LANES))
        padded = jnp.floor((cnt + float(EXPERT_BLOCK - 1)) * (1.0 / EXPERT_BLOCK)) * float(EXPERT_BLOCK)
        lane8 = lax.broadcasted_iota(jnp.int32, (SUBLANES, LANES), 1)
        acc = padded
        step = 1
        while step < LANES:
            acc = acc + jnp.where(lane8 >= step, pltpu.roll(acc, step, axis=1), 0.0)
            step *= 2
        pstart_ref[...] = (acc - padded)[:1]
        pcum_ref[...] = acc[:1].astype(jnp.int32)

    info = info_ref[...]
    slots = jnp.zeros(shape, F32)
    for k in range(TOP_K):
        onehot = lane == info[:, k:k + 1]
        start = jnp.sum(jnp.where(onehot, pstart_ref[...], 0.0), axis=-1, keepdims=True)
        slots = jnp.where(lane == float(k), start + info[:, TOP_K + k:TOP_K + k + 1], slots)
    dest_ref[...] = slots.T[:SUBLANES].astype(jnp.int32)


def _slots(info, cnt):
    t = info.shape[0]
    tm = min(ROUTE_TILE, t)
    assert t % tm == 0
    return pl.pallas_call(
        _slot_kernel,
        grid=(t // tm,),
        in_specs=[pl.BlockSpec((tm, LANES), lambda i: (i, 0)), pl.BlockSpec((1, LANES), lambda i: (0, 0))],
        out_specs=[pl.BlockSpec((SUBLANES, tm), lambda i: (0, i)), pl.BlockSpec((1, LANES), lambda i: (0, 0))],
        out_shape=[jax.ShapeDtypeStruct((SUBLANES, t), jnp.int32), jax.ShapeDtypeStruct((1, LANES), jnp.int32)],
        scratch_shapes=[pltpu.VMEM((1, LANES), F32)],
        compiler_params=_cparams(("arbitrary",)),
        name="moe_slots",
    )(info, cnt)


def _expert_kernel(e0, be_ref, nxt_ref, nu_ref, x_ref, wup_hbm, bup_ref, wdn_hbm, bdn_ref, y_ref,
                   wup_st, wdn_st, wup_bf, wdn_bf, sems):
    i = pl.program_id(0)
    e = be_ref[i]
    prev = be_ref[jnp.maximum(i - 1, 0)]
    used = i < nu_ref[0]

    def weight_copies(expert):
        return (pltpu.make_async_copy(wup_hbm.at[e0 + expert], wup_st, sems.at[0]),
                pltpu.make_async_copy(wdn_hbm.at[e0 + expert], wdn_st, sems.at[1]))

    @pl.when(i == 0)
    def _():
        for cp in weight_copies(e):
            cp.start()

    @pl.when(used & ((i == 0) | (e != prev)))
    def _():
        for cp in weight_copies(e):
            cp.wait()
        wup_bf[...] = wup_st[...].astype(BF16)
        wdn_bf[...] = wdn_st[...].astype(BF16)

        @pl.when(nxt_ref[i] >= 0)
        def _():
            for cp in weight_copies(nxt_ref[i]):
                cp.start()

    @pl.when(used)
    def _():
        xb = _unpack_bf16_pairs(x_ref[...]).astype(BF16)
        hb = _dot(xb, wup_bf[...]) + bup_ref[0]
        x_glu = jnp.minimum(hb[:, :EXPERT_DIM], SWIGLU_LIMIT)
        x_lin = jnp.clip(hb[:, EXPERT_DIM:], -SWIGLU_LIMIT, SWIGLU_LIMIT)
        act = x_glu * _sigmoid(SWIGLU_ALPHA * x_glu) * (x_lin + 1.0)
        y = _dot(act.astype(BF16), wdn_bf[...]) + bdn_ref[0]
        y_ref[...] = _pack_bf16_pairs(y)

    @pl.when(i >= nu_ref[0])
    def _():
        y_ref[...] = jnp.zeros_like(y_ref)


def _expert_ffn(xbuf, block_e, next_e, n_used, w_up, b_up, w_down, b_down, layer):
    p, dh = xbuf.shape
    d = 2 * dh
    bm = EXPERT_BLOCK
    ne, _, n_up = w_up.shape
    e0 = layer * N_EXPERTS
    grid_spec = pltpu.PrefetchScalarGridSpec(
        num_scalar_prefetch=3,
        grid=(p // bm,),
        in_specs=[
            pl.BlockSpec((bm, dh), lambda i, be, nx, nu: (i, 0)),
            pl.BlockSpec(memory_space=pl.ANY),
            pl.BlockSpec((1, 1, n_up), lambda i, be, nx, nu: (e0 + be[i], 0, 0)),
            pl.BlockSpec(memory_space=pl.ANY),
            pl.BlockSpec((1, 1, d), lambda i, be, nx, nu: (e0 + be[i], 0, 0)),
        ],
        out_specs=pl.BlockSpec((bm, dh), lambda i, be, nx, nu: (i, 0)),
        scratch_shapes=[pltpu.VMEM((d, n_up), F32), pltpu.VMEM((EXPERT_DIM, d), F32),
                        pltpu.VMEM((d, n_up), BF16), pltpu.VMEM((EXPERT_DIM, d), BF16),
                        pltpu.SemaphoreType.DMA((2,))],
    )
    return pl.pallas_call(
        functools.partial(_expert_kernel, e0),
        grid_spec=grid_spec,
        out_shape=jax.ShapeDtypeStruct((p, dh), jnp.int32),
        compiler_params=_cparams(("arbitrary",)),
        name="expert_ffn",
    )(block_e, next_e, n_used, xbuf, w_up, b_up, w_down, b_down)


def _combine_kernel(x1_ref, mod_ref, yg_ref, info_ref, lng_ref, lnb_ref, o_ref):
    info = info_ref[...]
    y = jnp.zeros(x1_ref.shape, F32)
    for k in range(TOP_K):
        gate = info[:, 2 * TOP_K + k:2 * TOP_K + k + 1]
        y = y + gate * _unpack_bf16_pairs(yg_ref[k])
    g2 = mod_ref[0, 5:6, :]
    o_ref[...] = _layer_norm(DEEPNORM_ALPHA * x1_ref[...] + g2 * y, lng_ref[...], lnb_ref[...])


def _combine(x1, mod, yg, info, ln_g, ln_b, seq):
    t, d = x1.shape
    tm = ROW_TILE
    row = lambda i: (i, 0)
    fixed = lambda i: (0, 0)
    return pl.pallas_call(
        _combine_kernel,
        grid=(t // tm,),
        in_specs=[
            pl.BlockSpec((tm, d), row),
            pl.BlockSpec((1, 6, d), lambda i: ((i * tm) // seq, 0, 0)),
            pl.BlockSpec((TOP_K, tm, d // 2), lambda i: (0, i, 0)),
            pl.BlockSpec((tm, LANES), row),
            pl.BlockSpec((1, d), fixed),
            pl.BlockSpec((1, d), fixed),
        ],
        out_specs=pl.BlockSpec((tm, d), row),
        out_shape=jax.ShapeDtypeStruct((t, d), F32),
        compiler_params=_cparams(("arbitrary",)),
        name="moe_combine_ln",
    )(x1, mod, yg, info, ln_g.reshape(1, d), ln_b.reshape(1, d))


def _sc_workers():
    info = plsc.get_sparse_core_info()
    return info.num_cores, info.num_cores * info.num_subcores


def _sc_scatter_rows(rows, idx, n_out):
    t, w = rows.shape
    kk = idx.shape[0]
    n_cores, n_workers = _sc_workers()
    ch = SC_CHUNK
    assert t % (n_workers * ch) == 0
    n_chunk = t // (n_workers * ch)
    idx_c = jnp.transpose(idx.reshape(kk, t // ch, ch), (1, 0, 2))

    @functools.partial(
        pl.kernel,
        mesh=plsc.VectorSubcoreMesh(core_axis_name="c", subcore_axis_name="s"),
        out_type=jax.ShapeDtypeStruct((n_out, w), rows.dtype),
        scratch_types=[pltpu.VMEM((kk, ch), jnp.int32), pltpu.VMEM((ch, w), rows.dtype)],
        name="sc_dispatch_scatter",
    )
    def scatter_kernel(rows_hbm, idx_hbm, out_hbm, idx_v, rows_v):
        wid = lax.axis_index("s") * n_cores + lax.axis_index("c")

        @pl.loop(0, n_chunk)
        def _(j):
            cidx = wid * n_chunk + j
            pltpu.sync_copy(idx_hbm.at[cidx], idx_v)
            pltpu.sync_copy(rows_hbm.at[pl.ds(cidx * ch, ch)], rows_v)
            for q in range(kk):
                pltpu.sync_copy(rows_v, out_hbm.at[idx_v.at[q]])

    return scatter_kernel(rows, idx_c)


def _sc_gather_rows(table, idx):
    m = idx.shape[0]
    w = table.shape[1]
    n_cores, n_workers = _sc_workers()
    ch = SC_CHUNK
    assert m % (n_workers * ch) == 0
    n_chunk = m // (n_workers * ch)
    idx_c = idx.reshape(m // ch, 1, ch)

    @functools.partial(
        pl.kernel,
        mesh=plsc.VectorSubcoreMesh(core_axis_name="c", subcore_axis_name="s"),
        out_type=jax.ShapeDtypeStruct((m, w), table.dtype),
        scratch_types=[pltpu.VMEM((1, ch), jnp.int32), pltpu.VMEM((ch, w), table.dtype)],
        name="sc_combine_gather",
    )
    def gather_kernel(table_hbm, idx_hbm, out_hbm, idx_v, rows_v):
        wid = lax.axis_index("s") * n_cores + lax.axis_index("c")

        @pl.loop(0, n_chunk)
        def _(j):
            cidx = wid * n_chunk + j
            pltpu.sync_copy(idx_hbm.at[cidx], idx_v)
            pltpu.sync_copy(table_hbm.at[idx_v.at[0]], rows_v)
            pltpu.sync_copy(rows_v, out_hbm.at[pl.ds(cidx * ch, ch)])

    return gather_kernel(table, idx_c)


def _lane_vector(vals, offset):
    return jnp.zeros((1, LANES), F32).at[0, offset:offset + vals.shape[0]].set(vals.astype(F32))


def _moe(h2, info, cnt, x1, mod, ln_g, ln_b, w_up, b_up, w_down, b_down, layer, seq):
    t, dh = h2.shape
    a = t * TOP_K
    bm = EXPERT_BLOCK
    slots, pcum_v = _slots(info, cnt)
    pcum = pcum_v[0, :N_EXPERTS]
    dest = slots[:TOP_K]
    n_blocks = -(-a // bm) + N_EXPERTS
    block_e = jnp.minimum(jnp.sum(pcum[None, :] <= (jnp.arange(n_blocks) * bm)[:, None], axis=1),
                          N_EXPERTS - 1).astype(jnp.int32)
    n_used = (pcum[-1] // bm).astype(jnp.int32).reshape(1)
    group_end = jnp.sum(block_e[None, :] <= block_e[:, None], axis=1)
    next_e = jnp.where(group_end < n_used[0], block_e[jnp.minimum(group_end, n_blocks - 1)], -1).astype(jnp.int32)
    xbuf = _sc_scatter_rows(h2, dest, n_blocks * bm)
    ybuf = _expert_ffn(xbuf, block_e, next_e, n_used, w_up, b_up, w_down, b_down, layer)
    yg = _sc_gather_rows(ybuf, dest.reshape(a)).reshape(TOP_K, t, dh)
    return _combine(x1, mod, yg, info, ln_g, ln_b, seq)


def kernel(x, c, rel_bias, w_in, w_out, w_ada, b_ada, ln1_g, ln1_b, ln2_g, ln2_b, pool_w, pool_scale,
           attn_sinks, conv_w, gdn_a_log, gdn_dt_bias, gdn_norm_w, router_w, router_b,
           exp_w_up, exp_b_up, exp_w_down, exp_b_down):
    bsz, seq, d = x.shape
    depth = w_in.shape[0]
    t = bsz * seq
    assert d == D_MODEL and w_in.shape[2] == IN_DIM
    assert seq % GDN_SUPER == 0 and seq % (ATT_BLOCKS * WINDOW) == 0
    assert t % ROW_TILE == 0 and seq % ROW_TILE == 0

    mod_all = _modulation(c, w_ada, b_ada).reshape(depth, bsz, 6, d)
    bias = _band_bias(rel_bias)

    w_up_all = exp_w_up.reshape((depth * N_EXPERTS,) + exp_w_up.shape[2:])
    b_up_all = exp_b_up.reshape(depth * N_EXPERTS, 1, exp_b_up.shape[2])
    w_down_all = exp_w_down.reshape((depth * N_EXPERTS,) + exp_w_down.shape[2:])
    b_down_all = exp_b_down.reshape(depth * N_EXPERTS, 1, exp_b_down.shape[2])

    w_in_all = _take_cols(w_in, _IN_PERM).astype(BF16)

    x2d = x.reshape(t, d)
    for l in range(depth):
        mod = mod_all[l]
        w_out_p = _take_static(w_out[l], _OUT_PERM, 0).astype(BF16)
        ident = jnp.zeros((CONV_WIDTH, 1), F32).at[CONV_WIDTH - 1, 0].set(1.0)
        conv_p = jnp.where(jnp.asarray(_GDN_CONV_SRC >= 0), _take_cols(conv_w[l].astype(F32), _GDN_CONV_SRC),
                           ident)
        pool_bd = jnp.zeros((POOL_DIM, POOL_DIM), F32)
        for gi in range(len(POOL_WINDOWS)):
            sl = slice(gi * POOL_GROUP, (gi + 1) * POOL_GROUP)
            pool_bd = pool_bd.at[sl, sl].set(pool_w[l, gi].astype(F32))
        alog_v = _lane_vector(gdn_a_log[l], N_GDN_HEADS)
        dtb_v = _lane_vector(gdn_dt_bias[l], N_GDN_HEADS)
        nw_v = jnp.tile(gdn_norm_w[l].astype(F32), 2).reshape(1, LANES)
        rw = jnp.zeros((d, LANES), F32).at[:, :N_EXPERTS].set(router_w[l].astype(F32))
        rw_hi, rw_lo = _split_bf16(rw)
        rb = jnp.full((1, LANES), NEG_INF, F32).at[0, :N_EXPERTS].set(router_b[l].astype(F32))

        u_pool, aq, akv, gdn, ba = _in_projection(x2d, mod, w_in_all, l, seq)
        y_pool = _pool_mixer(u_pool, pool_bd.astype(BF16), pool_scale[l].astype(F32), seq)
        y_att = _swa_attention(aq, akv, bias, attn_sinks[l].astype(F32), seq)
        y_gdn = _gdn_mixer(gdn, ba, conv_p, alog_v, dtb_v, nw_v, seq)
        x1, h2, info, cnt = _out_projection(x2d, mod, y_pool, y_att, y_gdn, w_out_p, ln1_g[l], ln1_b[l],
                                            rw_hi, rw_lo, rb, seq)
        x2d = _moe(h2, info, cnt, x1, mod, ln2_g[l], ln2_b[l], w_up_all, b_up_all, w_down_all, b_down_all,
                   l, seq)
    return x2d.reshape(bsz, seq, d)
```

```python
import functools

import numpy as np
import jax
import jax.numpy as jnp
from jax import lax
from jax.experimental import pallas as pl
from jax.experimental.pallas import tpu as pltpu
from jax.experimental.pallas import tpu_sc as plsc

F32 = jnp.float32
BF16 = jnp.bfloat16

D_MODEL = 1024
HEAD_DIM = 64
POOL_DIM = 256
POOL_WINDOWS = (2, 4, 8, 16)
POOL_GROUP = 64
N_ATT_HEADS = 6
N_KV_HEADS = 2
ATT_DIM = 384
KV_DIM = 128
WINDOW = 128
N_BUCKETS = 32
MAX_DISTANCE = 128
N_GDN_HEADS = 6
GDN_DIM = 384
CONV_WIDTH = 4
GDN_CHUNK = 64
N_EXPERTS = 32
TOP_K = 4
EXPERT_DIM = 1024
SWIGLU_ALPHA = 1.702
SWIGLU_LIMIT = 7.0
DEPTH = 2
DEEPNORM_ALPHA = (2 * DEPTH) ** 0.25
LN_EPS = 1e-5
NORM_EPS = 1e-6
NEG_INF = -1e30

LANES = 128
SUBLANES = 8
VMEM_LIMIT = 56 * 1024 * 1024

ROW_TILE = 512
ATT_BLOCKS = 4
GDN_SUPER = 256
ROUTE_TILE = 2048
EXPERT_BLOCK = 512
SC_CHUNK = 64

_OFF_AQ = POOL_DIM
_OFF_AK = _OFF_AQ + ATT_DIM
_OFF_AV = _OFF_AK + KV_DIM
_OFF_GQ = _OFF_AV + KV_DIM
_OFF_GK = _OFF_GQ + GDN_DIM
_OFF_GV = _OFF_GK + GDN_DIM
_OFF_GZ = _OFF_GV + GDN_DIM
_OFF_GB = _OFF_GZ + GDN_DIM
_OFF_GA = _OFF_GB + N_GDN_HEADS
IN_DIM = _OFF_GA + N_GDN_HEADS

P_POOL = (0, POOL_DIM)
P_Q = (P_POOL[1], P_POOL[1] + ATT_DIM)
P_KV = (P_Q[1], P_Q[1] + 2 * KV_DIM)
P_GDN = (P_KV[1], P_KV[1] + 4 * GDN_DIM)
P_BA = (P_GDN[1], P_GDN[1] + LANES)
P_TOTAL = P_BA[1]


def _head_cols(off, h):
    return list(range(off + HEAD_DIM * h, off + HEAD_DIM * (h + 1)))


def _build_in_perm():
    cols = list(range(POOL_DIM))
    for p in range(N_ATT_HEADS // 2):
        cols += _head_cols(_OFF_AQ, p) + _head_cols(_OFF_AQ, p + 3)
    cols += list(range(_OFF_AK, _OFF_AK + 2 * KV_DIM))
    gdn_src = []
    for p in range(N_GDN_HEADS // 2):
        e, o = 2 * p, 2 * p + 1
        grp = (_head_cols(_OFF_GK, e) + _head_cols(_OFF_GQ, e)
               + _head_cols(_OFF_GQ, o) + _head_cols(_OFF_GK, o)
               + _head_cols(_OFF_GV, o) + _head_cols(_OFF_GV, e)
               + _head_cols(_OFF_GZ, o) + _head_cols(_OFF_GZ, e))
        cols += grp
        gdn_src += [c - _OFF_GQ if c < _OFF_GZ else -1 for c in grp]
    cols += list(range(_OFF_GB, _OFF_GB + 2 * N_GDN_HEADS))
    cols += [-1] * (LANES - 2 * N_GDN_HEADS)
    assert len(cols) == P_TOTAL
    return np.asarray(cols, np.int32), np.asarray(gdn_src, np.int32)


_IN_PERM, _GDN_CONV_SRC = _build_in_perm()


def _build_out_perm():
    rows = list(range(POOL_DIM))
    for p in range(N_ATT_HEADS // 2):
        rows += _head_cols(POOL_DIM, p) + _head_cols(POOL_DIM, p + 3)
    for p in range(N_GDN_HEADS // 2):
        rows += _head_cols(POOL_DIM + ATT_DIM, 2 * p + 1) + _head_cols(POOL_DIM + ATT_DIM, 2 * p)
    return np.asarray(rows, np.int32)


_OUT_PERM = _build_out_perm()


def _t5_bucket_line():
    n = np.maximum(2 * WINDOW - 1 - np.arange(3 * WINDOW - 1), 0)
    max_exact = N_BUCKETS // 2
    nf = np.maximum(n, 1).astype(np.float32)
    large = max_exact + (np.log(nf / max_exact) / np.float32(np.log(MAX_DISTANCE / max_exact))
                         * (N_BUCKETS - max_exact)).astype(np.int32)
    large = np.minimum(large, N_BUCKETS - 1)
    return np.where(n < max_exact, n, large).astype(np.int32)


_BUCKET_LINE = _t5_bucket_line()


def _band_bias(rel_bias):
    n_line = 3 * WINDOW - 1
    line = jnp.take(rel_bias.astype(F32), jnp.asarray(_BUCKET_LINE), axis=0).T
    heads = line.shape[0]
    padded = jnp.concatenate([line, jnp.zeros((heads, 1), F32)], axis=1)
    skew = jnp.tile(padded, (1, WINDOW))[:, :WINDOW * n_line].reshape(heads, WINDOW, n_line)
    return skew[:, :, WINDOW - 1:3 * WINDOW - 1]


def _take_static(w, perm, axis):
    parts = []
    start = 0
    for i in range(1, len(perm) + 1):
        run_ends = (i == len(perm) or ((perm[i] < 0) != (perm[i - 1] < 0))
                    or (perm[i] >= 0 and perm[i] != perm[i - 1] + 1))
        if run_ends:
            if perm[start] < 0:
                shape = list(w.shape)
                shape[axis] = i - start
                parts.append(jnp.zeros(shape, w.dtype))
            else:
                parts.append(lax.slice_in_dim(w, int(perm[start]), int(perm[start]) + (i - start), axis=axis))
            start = i
    return jnp.concatenate(parts, axis=axis)


def _take_cols(w, perm):
    return _take_static(w, perm, w.ndim - 1)


def _split_bf16(x):
    hi = x.astype(BF16)
    lo = (x - hi.astype(F32)).astype(BF16)
    return hi, lo


def _pack_bf16_pairs(x):
    n = x.shape[1] // 2
    bits = pltpu.bitcast(x.astype(BF16).astype(F32), jnp.int32)
    return lax.shift_right_logical(bits[:, :n], 16) | bits[:, n:]


def _unpack_bf16_pairs(u):
    lo = pltpu.bitcast(lax.shift_left(u, 16), F32)
    hi = pltpu.bitcast(u & jnp.int32(-65536), F32)
    return jnp.concatenate([lo, hi], axis=1)


def _dot(a, b):
    return jnp.dot(a, b, preferred_element_type=F32)


def _dot_nt(a, b):
    return lax.dot_general(a, b, (((1,), (1,)), ((), ())), preferred_element_type=F32)


def _dot_hi_exact_rhs(x, m_bf16):
    hi, lo = _split_bf16(x)
    return _dot(hi, m_bf16) + _dot(lo, m_bf16)


def _sigmoid(x):
    return 1.0 / (1.0 + jnp.exp(-x))


def _layer_norm(r, g, b):
    mu = jnp.mean(r, axis=-1, keepdims=True)
    d = r - mu
    var = jnp.mean(d * d, axis=-1, keepdims=True)
    return d * lax.rsqrt(var + LN_EPS) * g + b


def _cparams(sem):
    return pltpu.CompilerParams(dimension_semantics=sem, vmem_limit_bytes=VMEM_LIMIT)


def _mod_kernel(c_ref, w_ref, b_ref, o_ref):
    c = c_ref[...]
    ca = c * _sigmoid(c)
    ch, cl = _split_bf16(ca)
    wh, wl = _split_bf16(w_ref[0])
    o_ref[0] = _dot(ch, wh) + _dot(cl, wh) + _dot(ch, wl) + b_ref[0]


def _modulation(c, w_ada, b_ada):
    depth, d, n = w_ada.shape
    bsz = c.shape[0]
    tn = 512
    return pl.pallas_call(
        _mod_kernel,
        grid=(depth, n // tn),
        in_specs=[
            pl.BlockSpec((bsz, d), lambda l, j: (0, 0)),
            pl.BlockSpec((1, d, tn), lambda l, j: (l, 0, j)),
            pl.BlockSpec((1, 1, tn), lambda l, j: (l, 0, j)),
        ],
        out_specs=pl.BlockSpec((1, bsz, tn), lambda l, j: (l, 0, j)),
        out_shape=jax.ShapeDtypeStruct((depth, bsz, n), F32),
        compiler_params=_cparams(("arbitrary", "arbitrary")),
        name="adaln_mod",
    )(c, w_ada, b_ada.reshape(depth, 1, n))


def _inproj_kernel(x_ref, mod_ref, w_ref, pool_ref, q_ref, kv_ref, gdn_ref, ba_ref):
    sh = mod_ref[0, 0:1, :]
    sc = mod_ref[0, 1:2, :]
    h = (x_ref[...] * (1.0 + sc) + sh).astype(BF16)

    def mm(rng):
        return _dot(h, w_ref[0, :, rng[0]:rng[1]])

    pool_ref[...] = mm(P_POOL)
    q_ref[...] = mm(P_Q).astype(BF16)
    kv_ref[...] = mm(P_KV).astype(BF16)
    gdn_ref[...] = mm(P_GDN)
    ba_ref[...] = mm(P_BA)


def _in_projection(x2d, mod, w_in_all, layer, seq):
    t, d = x2d.shape
    tm = ROW_TILE
    widths = [r[1] - r[0] for r in (P_POOL, P_Q, P_KV, P_GDN, P_BA)]
    dtypes = [F32, BF16, BF16, F32, F32]
    return pl.pallas_call(
        _inproj_kernel,
        grid=(t // tm,),
        in_specs=[
            pl.BlockSpec((tm, d), lambda i: (i, 0)),
            pl.BlockSpec((1, 6, d), lambda i: ((i * tm) // seq, 0, 0)),
            pl.BlockSpec((1, d, P_TOTAL), lambda i: (layer, 0, 0)),
        ],
        out_specs=[pl.BlockSpec((tm, w), lambda i: (i, 0)) for w in widths],
        out_shape=[jax.ShapeDtypeStruct((t, w), dt) for w, dt in zip(widths, dtypes)],
        compiler_params=_cparams(("arbitrary",)),
        name="in_proj",
    )(x2d, mod, w_in_all)


def _pool_kernel(u_ref, w_ref, scale_ref, o_ref):
    u = u_ref[...]
    row = lax.broadcasted_iota(jnp.int32, u.shape, 0)
    lane = lax.broadcasted_iota(jnp.int32, u.shape, 1)

    def shifted(a, s):
        return jnp.where(row >= s, pltpu.roll(a, s, axis=0), 0.0)

    sums = []
    acc = u
    for wdt in POOL_WINDOWS:
        acc = acc + shifted(acc, wdt // 2)
        sums.append(acc)
    grp = lane // POOL_GROUP
    wsum = sums[-1]
    win = jnp.full(u.shape, POOL_WINDOWS[-1], jnp.int32)
    for gi in range(len(POOL_WINDOWS) - 2, -1, -1):
        wsum = jnp.where(grp == gi, sums[gi], wsum)
        win = jnp.where(grp == gi, POOL_WINDOWS[gi], win)
    cnt = jnp.minimum(row + 1, win).astype(F32)
    p = wsum / cnt - u
    y = _dot(p.astype(BF16), w_ref[...]) * scale_ref[...]
    o_ref[...] = y.astype(BF16)


def _pool_mixer(u, pool_w_bd, pool_scale, seq):
    t, c = u.shape
    return pl.pallas_call(
        _pool_kernel,
        grid=(t // seq,),
        in_specs=[
            pl.BlockSpec((seq, c), lambda b: (b, 0)),
            pl.BlockSpec((c, c), lambda b: (0, 0)),
            pl.BlockSpec((1, c), lambda b: (0, 0)),
        ],
        out_specs=pl.BlockSpec((seq, c), lambda b: (b, 0)),
        out_shape=jax.ShapeDtypeStruct((t, c), BF16),
        compiler_params=_cparams(("arbitrary",)),
        name="pool_mixer",
    )(u, pool_w_bd, pool_scale.reshape(1, c))


def _attn_kernel(sink_ref, q_ref, kvc_ref, kvp_ref, bias_ref, o_ref):
    step = pl.program_id(1)
    qi = lax.broadcasted_iota(jnp.int32, (WINDOW, 2 * WINDOW), 0)
    kj = lax.broadcasted_iota(jnp.int32, (WINDOW, 2 * WINDOW), 1)
    dist = qi + WINDOW - kj
    in_band = (dist >= 0) & (dist < WINDOW)
    lo = lax.broadcasted_iota(jnp.int32, (WINDOW, LANES), 1) < HEAD_DIM
    for sub in range(ATT_BLOCKS):
        r0 = sub * WINDOW
        prev = kvp_ref[...] if sub == 0 else kvc_ref[r0 - WINDOW:r0, :]
        kv = jnp.concatenate([prev, kvc_ref[r0:r0 + WINDOW, :]], axis=0)
        k = kv[:, :KV_DIM]
        v = kv[:, KV_DIM:]
        valid = in_band & ((kj >= WINDOW) | (step > 0)) if sub == 0 else in_band
        for p in range(N_ATT_HEADS // 2):
            qp = q_ref[r0:r0 + WINDOW, p * LANES:(p + 1) * LANES]
            halves = []
            for half in range(2):
                h = p + 3 * half
                qm = jnp.where(lo if half == 0 else jnp.logical_not(lo), qp, jnp.zeros_like(qp))
                s = _dot_nt(qm, k) * (HEAD_DIM ** -0.5)
                s = jnp.where(valid, s + bias_ref[h], NEG_INF)
                sink = sink_ref[h]
                m = jnp.maximum(jnp.max(s, axis=-1, keepdims=True), sink)
                pr = jnp.exp(s - m)
                den = jnp.sum(pr, axis=-1, keepdims=True) + jnp.exp(sink - m)
                halves.append(_dot(pr.astype(BF16), v) / den)
            o_ref[r0:r0 + WINDOW, p * LANES:(p + 1) * LANES] = (
                jnp.where(lo, halves[0], halves[1]).astype(BF16))


def _swa_attention(q, kv, bias, sinks, seq):
    t = q.shape[0]
    rows = ATT_BLOCKS * WINDOW
    nblk = seq // rows
    return pl.pallas_call(
        _attn_kernel,
        grid=(t // seq, nblk),
        in_specs=[
            pl.BlockSpec(memory_space=pltpu.SMEM),
            pl.BlockSpec((rows, ATT_DIM), lambda b, n: (b * nblk + n, 0)),
            pl.BlockSpec((rows, 2 * KV_DIM), lambda b, n: (b * nblk + n, 0)),
            pl.BlockSpec((WINDOW, 2 * KV_DIM),
                         lambda b, n: (jnp.maximum((b * nblk + n) * ATT_BLOCKS - 1, 0), 0)),
            pl.BlockSpec((N_ATT_HEADS, WINDOW, 2 * WINDOW), lambda b, n: (0, 0, 0)),
        ],
        out_specs=pl.BlockSpec((rows, ATT_DIM), lambda b, n: (b * nblk + n, 0)),
        out_shape=jax.ShapeDtypeStruct((t, ATT_DIM), BF16),
        compiler_params=_cparams(("arbitrary", "arbitrary")),
        name="swa_attention",
    )(sinks, q, kv, kv, bias)


def _gdn_kernel(x_ref, halo_ref, ba_ref, cw_ref, alog_ref, dtb_ref, nw_ref, y_ref, state_ref, xs_ref):
    sc_id = pl.program_id(1)
    rows = GDN_SUPER
    nchunk = rows // GDN_CHUNK
    c_sz = GDN_CHUNK

    @pl.when(sc_id == 0)
    def _():
        state_ref[...] = jnp.zeros_like(state_ref)

    xs_ref[:SUBLANES, :] = jnp.where(sc_id == 0, 0.0, halo_ref[...])
    xs_ref[SUBLANES:, :] = x_ref[...]
    acc = x_ref[...] * cw_ref[CONV_WIDTH - 1:CONV_WIDTH, :]
    for s in range(1, CONV_WIDTH):
        acc = acc + xs_ref[SUBLANES - s:SUBLANES - s + rows, :] * cw_ref[CONV_WIDTH - 1 - s:CONV_WIDTH - s, :]
    act = acc * _sigmoid(acc)

    ri = lax.broadcasted_iota(jnp.int32, (rows, rows), 0)
    ci = lax.broadcasted_iota(jnp.int32, (rows, rows), 1)
    same_chunk = (ri // c_sz) == (ci // c_sz)
    incl = same_chunk & (ri >= ci)
    strict = same_chunk & (ri > ci)
    tri_incl = jnp.where(incl, 1.0, 0.0).astype(BF16)
    eye = jnp.where(ri == ci, 1.0, 0.0)
    blk = []
    bsz = SUBLANES
    while bsz <= c_sz:
        blk.append((ri // bsz) == (ci // bsz))
        bsz *= 2
    li = lax.broadcasted_iota(jnp.int32, (LANES, LANES), 0)
    lj = lax.broadcasted_iota(jnp.int32, (LANES, LANES), 1)
    half_ones = jnp.where((li // HEAD_DIM) == (lj // HEAD_DIM), 1.0, 0.0).astype(BF16)
    lane_lo = lax.broadcasted_iota(jnp.int32, (rows, LANES), 1) < HEAD_DIM
    lane_lo_c = lax.broadcasted_iota(jnp.int32, (c_sz, LANES), 1) < HEAD_DIM

    ba = ba_ref[...]
    beta_all = _sigmoid(ba)
    sp_in = ba + dtb_ref[...]
    softplus = jnp.maximum(sp_in, 0.0) + jnp.log(1.0 + jnp.exp(-jnp.abs(sp_in)))
    g_all = -jnp.exp(alog_ref[...]) * softplus
    gcum = _dot_hi_exact_rhs_lhs(tri_incl, g_all)
    gcum_t = gcum.T

    heads = range(N_GDN_HEADS)
    lane_hi = jnp.logical_not(lane_lo)
    lane_hi_c = jnp.logical_not(lane_lo_c)
    mk = [lane_lo if h % 2 == 0 else lane_hi for h in heads]
    mk_c = [lane_lo_c if h % 2 == 0 else lane_hi_c for h in heads]
    scale = HEAD_DIM ** -0.5

    def bdot(a, b):
        return _dot(a.astype(BF16), b.astype(BF16))

    xk, xq, gn, gc_col, beta, eg = [], [], [], [], [], []
    for h in heads:
        base = 4 * LANES * (h // 2) + LANES * (h % 2)
        g = act[:, base:base + LANES]
        g = g * lax.rsqrt(_dot_hi_exact_rhs(g * g, half_ones) + NORM_EPS)
        gn.append(g)
        xk.append(jnp.where(mk[h], g, 0.0))
        xq.append(jnp.where(mk[h], pltpu.roll(g, HEAD_DIM, axis=1), 0.0) * scale)
        beta.append(beta_all[:, h:h + 1])
        gc_col.append(gcum[:, N_GDN_HEADS + h:N_GDN_HEADS + h + 1])
        eg.append(jnp.exp(gc_col[h]))

    l_mat, attn, rhs = [], [], []
    for h in heads:
        gc_row = gcum_t[N_GDN_HEADS + h:N_GDN_HEADS + h + 1, :]
        decay = jnp.exp(jnp.where(incl, gc_col[h] - gc_row, -jnp.inf))
        xk_b = xk[h].astype(BF16)
        kk = _dot_nt((xk[h] * beta[h]).astype(BF16), xk_b)
        l_mat.append(jnp.where(strict, kk * decay, 0.0))
        attn.append((_dot_nt(xq[h].astype(BF16), xk_b) * decay).astype(BF16))
        vv = act[:, 4 * LANES * (h // 2) + 2 * LANES:4 * LANES * (h // 2) + 3 * LANES]
        rhs.append(jnp.where(mk[h], gn[h] * eg[h], vv) * beta[h])

    a1 = [jnp.where(blk[0], -l_mat[h], 0.0).astype(BF16) for h in heads]
    a2 = [_dot(a1[h], a1[h]).astype(BF16) for h in heads]
    a4 = [_dot(a2[h], a2[h]).astype(BF16) for h in heads]
    inv = [eye + a1[h].astype(F32) for h in heads]
    inv = [inv[h] + _dot(a2[h], inv[h].astype(BF16)) for h in heads]
    inv = [inv[h] + _dot(a4[h], inv[h].astype(BF16)) for h in heads]
    for lvl in range(1, len(blk) - 1):
        band = blk[lvl] & jnp.logical_not(blk[lvl - 1])
        inv_b = [inv[h].astype(BF16) for h in heads]
        mid = [_dot(jnp.where(band, l_mat[h], 0.0).astype(BF16), inv_b[h]) for h in heads]
        inv = [inv[h] - _dot(inv_b[h], mid[h].astype(BF16)) for h in heads]
    band = blk[-1] & jnp.logical_not(blk[-2])
    inv_b = [inv[h].astype(BF16) for h in heads]
    half = [_dot(inv_b[h], rhs[h].astype(BF16)) for h in heads]
    mid = [_dot(jnp.where(band, l_mat[h], 0.0).astype(BF16), half[h].astype(BF16)) for h in heads]
    sol = [half[h] - _dot(inv_b[h], mid[h].astype(BF16)) for h in heads]

    st = [state_ref[h] for h in heads]
    vn_parts = [[] for _ in heads]
    qs_parts = [[] for _ in heads]
    for c in range(nchunk):
        r0 = c * c_sz
        for h in heads:
            sol_c = sol[h][r0:r0 + c_sz]
            glast = gcum[r0 + c_sz - 1:r0 + c_sz, N_GDN_HEADS + h:N_GDN_HEADS + h + 1]
            kd_t = (xk[h][r0:r0 + c_sz] * jnp.exp(glast - gc_col[h][r0:r0 + c_sz])).T
            qd = xq[h][r0:r0 + c_sz] * eg[h][r0:r0 + c_sz]
            lhs = jnp.concatenate([jnp.where(mk_c[h], sol_c, 0.0), qd], axis=0)
            m1 = bdot(lhs, st[h])
            vn = sol_c - m1[:c_sz]
            vn_parts[h].append(vn)
            qs_parts[h].append(m1[c_sz:])
            st[h] = st[h] * jnp.exp(glast) + bdot(kd_t, jnp.where(mk_c[h], 0.0, vn))
    for h in heads:
        state_ref[h] = st[h]

    for p in range(N_GDN_HEADS // 2):
        o_pair = []
        for h in (2 * p, 2 * p + 1):
            vn_all = jnp.concatenate(vn_parts[h], axis=0)
            o_pair.append(jnp.concatenate(qs_parts[h], axis=0) + _dot(attn[h], vn_all.astype(BF16)))
        o = jnp.where(lane_lo, o_pair[1], o_pair[0])
        ms = _dot_hi_exact_rhs(o * o, half_ones) * (1.0 / HEAD_DIM)
        zz = act[:, 4 * LANES * p + 3 * LANES:4 * LANES * (p + 1)]
        y = o * lax.rsqrt(ms + NORM_EPS) * nw_ref[...] * zz
        y_ref[:, p * LANES:(p + 1) * LANES] = y.astype(BF16)


def _dot_hi_exact_rhs_lhs(m_bf16, x):
    hi, lo = _split_bf16(x)
    return _dot(m_bf16, hi) + _dot(m_bf16, lo)


def _gdn_mixer(gdn, ba, conv_p, alog_v, dtb_v, nw_v, seq):
    t, c = gdn.shape
    rows = GDN_SUPER
    nsc = seq // rows
    hb = rows // SUBLANES
    return pl.pallas_call(
        _gdn_kernel,
        grid=(t // seq, nsc),
        in_specs=[
            pl.BlockSpec((rows, c), lambda b, s: (b * nsc + s, 0)),
            pl.BlockSpec((SUBLANES, c), lambda b, s: (jnp.maximum((b * nsc + s) * hb - 1, 0), 0)),
            pl.BlockSpec((rows, LANES), lambda b, s: (b * nsc + s, 0)),
            pl.BlockSpec((CONV_WIDTH, c), lambda b, s: (0, 0)),
            pl.BlockSpec((1, LANES), lambda b, s: (0, 0)),
            pl.BlockSpec((1, LANES), lambda b, s: (0, 0)),
            pl.BlockSpec((1, LANES), lambda b, s: (0, 0)),
        ],
        out_specs=pl.BlockSpec((rows, GDN_DIM), lambda b, s: (b * nsc + s, 0)),
        out_shape=jax.ShapeDtypeStruct((t, GDN_DIM), BF16),
        scratch_shapes=[pltpu.VMEM((N_GDN_HEADS, LANES, LANES), F32), pltpu.VMEM((rows + SUBLANES, c), F32)],
        compiler_params=_cparams(("arbitrary", "arbitrary")),
        name="gdn_mixer",
    )(gdn, gdn, ba, conv_p, alog_v, dtb_v, nw_v)


def _route_tile(logits, before, carry_ref):
    shape = logits.shape
    lane = lax.broadcasted_iota(jnp.int32, shape, 1).astype(F32)
    work = logits
    vals, idxs = [], []
    for _k in range(TOP_K):
        m = jnp.max(work, axis=-1, keepdims=True)
        idx = jnp.min(jnp.where(work == m, lane, float(LANES)), axis=-1, keepdims=True)
        vals.append(m)
        idxs.append(idx)
        work = jnp.where(lane == idx, -jnp.inf, work)
    exps = [jnp.exp(v - vals[0]) for v in vals]
    den = exps[0] + exps[1] + exps[2] + exps[3]
    onehots = [lane == idx for idx in idxs]
    member = jnp.zeros(shape, F32)
    for oh in onehots:
        member = member + jnp.where(oh, 1.0, 0.0)
    rank = _dot(before, member.astype(BF16)) + carry_ref[...]
    carry_ref[...] = carry_ref[...] + jnp.sum(member, axis=0, keepdims=True)
    info = jnp.zeros(shape, F32)
    for k in range(TOP_K):
        rank_k = jnp.sum(jnp.where(onehots[k], rank, 0.0), axis=-1, keepdims=True)
        info = jnp.where(lane == float(k), idxs[k], info)
        info = jnp.where(lane == float(TOP_K + k), rank_k, info)
        info = jnp.where(lane == float(2 * TOP_K + k), exps[k] / den, info)
    return info


def _outproj_kernel(x_ref, mod_ref, yp_ref, ya_ref, yg_ref, wp_ref, wa_ref, wg_ref, lng_ref, lnb_ref,
                    rwh_ref, rwl_ref, rb_ref, before_ref, x1_ref, h2_ref, info_ref, cnt_ref, carry_ref):
    @pl.when(pl.program_id(0) == 0)
    def _():
        carry_ref[...] = jnp.zeros_like(carry_ref)

    y = _dot(yp_ref[...], wp_ref[...]) + _dot(ya_ref[...], wa_ref[...]) + _dot(yg_ref[...], wg_ref[...])
    g1 = mod_ref[0, 2:3, :]
    sh2 = mod_ref[0, 3:4, :]
    sc2 = mod_ref[0, 4:5, :]
    x1 = _layer_norm(DEEPNORM_ALPHA * x_ref[...] + g1 * y, lng_ref[...], lnb_ref[...])
    x1_ref[...] = x1
    h2 = x1 * (1.0 + sc2) + sh2
    hh, hl = _split_bf16(h2)
    h2_ref[...] = _pack_bf16_pairs(h2)
    logits = _dot(hh, rwh_ref[...]) + _dot(hl, rwh_ref[...]) + _dot(hh, rwl_ref[...]) + rb_ref[...]
    info_ref[...] = _route_tile(logits, before_ref[...], carry_ref)
    cnt_ref[...] = carry_ref[...]


def _out_projection(x2d, mod, yp, ya, yg, w_out_p, ln_g, ln_b, rw_hi, rw_lo, rb, seq):
    t, d = x2d.shape
    tm = ROW_TILE
    wp = w_out_p[:POOL_DIM]
    wa = w_out_p[POOL_DIM:POOL_DIM + ATT_DIM]
    wg = w_out_p[POOL_DIM + ATT_DIM:]
    row = lambda i: (i, 0)
    fixed = lambda i: (0, 0)
    return pl.pallas_call(
        _outproj_kernel,
        grid=(t // tm,),
        in_specs=[
            pl.BlockSpec((tm, d), row),
            pl.BlockSpec((1, 6, d), lambda i: ((i * tm) // seq, 0, 0)),
            pl.BlockSpec((tm, POOL_DIM), row),
            pl.BlockSpec((tm, ATT_DIM), row),
            pl.BlockSpec((tm, GDN_DIM), row),
            pl.BlockSpec((POOL_DIM, d), fixed),
            pl.BlockSpec((ATT_DIM, d), fixed),
            pl.BlockSpec((GDN_DIM, d), fixed),
            pl.BlockSpec((1, d), fixed),
            pl.BlockSpec((1, d), fixed),
            pl.BlockSpec((d, LANES), fixed),
            pl.BlockSpec((d, LANES), fixed),
            pl.BlockSpec((1, LANES), fixed),
            pl.BlockSpec((tm, tm), fixed),
        ],
        out_specs=[pl.BlockSpec((tm, d), row), pl.BlockSpec((tm, d // 2), row), pl.BlockSpec((tm, LANES), row),
                   pl.BlockSpec((1, LANES), fixed)],
        out_shape=[jax.ShapeDtypeStruct((t, d), F32), jax.ShapeDtypeStruct((t, d // 2), jnp.int32),
                   jax.ShapeDtypeStruct((t, LANES), F32), jax.ShapeDtypeStruct((1, LANES), F32)],
        scratch_shapes=[pltpu.VMEM((1, LANES), F32)],
        compiler_params=_cparams(("arbitrary",)),
        name="out_proj_ln_route",
    )(x2d, mod, yp, ya, yg, wp, wa, wg, ln_g.reshape(1, d), ln_b.reshape(1, d), rw_hi, rw_lo, rb,
      jnp.tril(jnp.ones((tm, tm), BF16), -1))


def _slot_kernel(info_ref, cnt_ref, dest_ref, pcum_ref, pstart_ref):
    shape = info_ref.shape
    lane = lax.broadcasted_iota(jnp.int32, shape, 1).astype(F32)

    @pl.when(pl.program_id(0) == 0)
    def _():
        cnt = jnp.broadcast_to(cnt_ref[...], (SUBLANES, LANES))
        padded = jnp.floor((cnt + float(EXPERT_BLOCK - 1)) * (1.0 / EXPERT_BLOCK)) * float(EXPERT_BLOCK)
        lane8 = lax.broadcasted_iota(jnp.int32, (SUBLANES, LANES), 1)
        acc = padded
        step = 1
        while step < LANES:
            acc = acc + jnp.where(lane8 >= step, pltpu.roll(acc, step, axis=1), 0.0)
            step *= 2
        pstart_ref[...] = (acc - padded)[:1]
        pcum_ref[...] = acc[:1].astype(jnp.int32)

    info = info_ref[...]
    slots = jnp.zeros(shape, F32)
    for k in range(TOP_K):
        onehot = lane == info[:, k:k + 1]
        start = jnp.sum(jnp.where(onehot, pstart_ref[...], 0.0), axis=-1, keepdims=True)
        slots = jnp.where(lane == float(k), start + info[:, TOP_K + k:TOP_K + k + 1], slots)
    dest_ref[...] = slots.T[:SUBLANES].astype(jnp.int32)


def _slots(info, cnt):
    t = info.shape[0]
    tm = min(ROUTE_TILE, t)
    assert t % tm == 0
    return pl.pallas_call(
        _slot_kernel,
        grid=(t // tm,),
        in_specs=[pl.BlockSpec((tm, LANES), lambda i: (i, 0)), pl.BlockSpec((1, LANES), lambda i: (0, 0))],
        out_specs=[pl.BlockSpec((SUBLANES, tm), lambda i: (0, i)), pl.BlockSpec((1, LANES), lambda i: (0, 0))],
        out_shape=[jax.ShapeDtypeStruct((SUBLANES, t), jnp.int32), jax.ShapeDtypeStruct((1, LANES), jnp.int32)],
        scratch_shapes=[pltpu.VMEM((1, LANES), F32)],
        compiler_params=_cparams(("arbitrary",)),
        name="moe_slots",
    )(info, cnt)


def _expert_kernel(e0, be_ref, nxt_ref, nu_ref, x_ref, wup_hbm, bup_ref, wdn_hbm, bdn_ref, y_ref,
                   wup_st, wdn_st, wup_bf, wdn_bf, sems):
    i = pl.program_id(0)
    e = be_ref[i]
    prev = be_ref[jnp.maximum(i - 1, 0)]
    used = i < nu_ref[0]

    def weight_copies(expert):
        return (pltpu.make_async_copy(wup_hbm.at[e0 + expert], wup_st, sems.at[0]),
                pltpu.make_async_copy(wdn_hbm.at[e0 + expert], wdn_st, sems.at[1]))

    @pl.when(i == 0)
    def _():
        for cp in weight_copies(e):
            cp.start()

    @pl.when(used & ((i == 0) | (e != prev)))
    def _():
        for cp in weight_copies(e):
            cp.wait()
        wup_bf[...] = wup_st[...].astype(BF16)
        wdn_bf[...] = wdn_st[...].astype(BF16)

        @pl.when(nxt_ref[i] >= 0)
        def _():
            for cp in weight_copies(nxt_ref[i]):
                cp.start()

    @pl.when(used)
    def _():
        xb = _unpack_bf16_pairs(x_ref[...]).astype(BF16)
        hb = _dot(xb, wup_bf[...]) + bup_ref[0]
        x_glu = jnp.minimum(hb[:, :EXPERT_DIM], SWIGLU_LIMIT)
        x_lin = jnp.clip(hb[:, EXPERT_DIM:], -SWIGLU_LIMIT, SWIGLU_LIMIT)
        act = x_glu * _sigmoid(SWIGLU_ALPHA * x_glu) * (x_lin + 1.0)
        y = _dot(act.astype(BF16), wdn_bf[...]) + bdn_ref[0]
        y_ref[...] = _pack_bf16_pairs(y)

    @pl.when(i >= nu_ref[0])
    def _():
        y_ref[...] = jnp.zeros_like(y_ref)


def _expert_ffn(xbuf, block_e, next_e, n_used, w_up, b_up, w_down, b_down, layer):
    p, dh = xbuf.shape
    d = 2 * dh
    bm = EXPERT_BLOCK
    ne, _, n_up = w_up.shape
    e0 = layer * N_EXPERTS
    grid_spec = pltpu.PrefetchScalarGridSpec(
        num_scalar_prefetch=3,
        grid=(p // bm,),
        in_specs=[
            pl.BlockSpec((bm, dh), lambda i, be, nx, nu: (i, 0)),
            pl.BlockSpec(memory_space=pl.ANY),
            pl.BlockSpec((1, 1, n_up), lambda i, be, nx, nu: (e0 + be[i], 0, 0)),
            pl.BlockSpec(memory_space=pl.ANY),
            pl.BlockSpec((1, 1, d), lambda i, be, nx, nu: (e0 + be[i], 0, 0)),
        ],
        out_specs=pl.BlockSpec((bm, dh), lambda i, be, nx, nu: (i, 0)),
        scratch_shapes=[pltpu.VMEM((d, n_up), F32), pltpu.VMEM((EXPERT_DIM, d), F32),
                        pltpu.VMEM((d, n_up), BF16), pltpu.VMEM((EXPERT_DIM, d), BF16),
                        pltpu.SemaphoreType.DMA((2,))],
    )
    return pl.pallas_call(
        functools.partial(_expert_kernel, e0),
        grid_spec=grid_spec,
        out_shape=jax.ShapeDtypeStruct((p, dh), jnp.int32),
        compiler_params=_cparams(("arbitrary",)),
        name="expert_ffn",
    )(block_e, next_e, n_used, xbuf, w_up, b_up, w_down, b_down)


def _combine_kernel(x1_ref, mod_ref, yg_ref, info_ref, lng_ref, lnb_ref, o_ref):
    info = info_ref[...]
    y = jnp.zeros(x1_ref.shape, F32)
    for k in range(TOP_K):
        gate = info[:, 2 * TOP_K + k:2 * TOP_K + k + 1]
        y = y + gate * _unpack_bf16_pairs(yg_ref[k])
    g2 = mod_ref[0, 5:6, :]
    o_ref[...] = _layer_norm(DEEPNORM_ALPHA * x1_ref[...] + g2 * y, lng_ref[...], lnb_ref[...])


def _combine(x1, mod, yg, info, ln_g, ln_b, seq):
    t, d = x1.shape
    tm = ROW_TILE
    row = lambda i: (i, 0)
    fixed = lambda i: (0, 0)
    return pl.pallas_call(
        _combine_kernel,
        grid=(t // tm,),
        in_specs=[
            pl.BlockSpec((tm, d), row),
            pl.BlockSpec((1, 6, d), lambda i: ((i * tm) // seq, 0, 0)),
            pl.BlockSpec((TOP_K, tm, d // 2), lambda i: (0, i, 0)),
            pl.BlockSpec((tm, LANES), row),
            pl.BlockSpec((1, d), fixed),
            pl.BlockSpec((1, d), fixed),
        ],
        out_specs=pl.BlockSpec((tm, d), row),
        out_shape=jax.ShapeDtypeStruct((t, d), F32),
        compiler_params=_cparams(("arbitrary",)),
        name="moe_combine_ln",
    )(x1, mod, yg, info, ln_g.reshape(1, d), ln_b.reshape(1, d))


def _sc_workers():
    info = plsc.get_sparse_core_info()
    return info.num_cores, info.num_cores * info.num_subcores


def _sc_scatter_rows(rows, idx, n_out):
    t, w = rows.shape
    kk = idx.shape[0]
    n_cores, n_workers = _sc_workers()
    ch = SC_CHUNK
    assert t % (2 * n_workers * ch) == 0
    n_chunk = t // (n_workers * ch)
    idx_c = jnp.transpose(idx.reshape(kk, t // ch, ch), (1, 0, 2))

    @functools.partial(
        pl.kernel,
        mesh=plsc.VectorSubcoreMesh(core_axis_name="c", subcore_axis_name="s"),
        out_type=jax.ShapeDtypeStruct((n_out, w), rows.dtype),
        scratch_types=[pltpu.VMEM((2, kk, ch), jnp.int32), pltpu.VMEM((2, ch, w), rows.dtype),
                       pltpu.SemaphoreType.DMA((2,)), pltpu.SemaphoreType.DMA((2,))],
        name="sc_dispatch_scatter",
    )
    def scatter_kernel(rows_hbm, idx_hbm, out_hbm, idx_v, rows_v, load_sem, scat_sem):
        base = (lax.axis_index("s") * n_cores + lax.axis_index("c")) * n_chunk

        def load(j, b):
            return pltpu.make_async_copy(rows_hbm.at[pl.ds((base + j) * ch, ch)], rows_v.at[b], load_sem.at[b])

        def scatters(b):
            return [pltpu.make_async_copy(rows_v.at[b], out_hbm.at[idx_v.at[b, q]], scat_sem.at[b])
                    for q in range(kk)]

        pltpu.sync_copy(idx_hbm.at[base], idx_v.at[0])
        load(0, 0).start()

        @pl.loop(0, n_chunk, step=2)
        def _(j0):
            for b in range(2):
                j = j0 + b
                other = 1 - b

                @pl.when(j >= 1)
                def _():
                    for cp in scatters(other):
                        cp.wait()

                @pl.when(j + 1 < n_chunk)
                def _():
                    pltpu.sync_copy(idx_hbm.at[base + j + 1], idx_v.at[other])
                    load(j + 1, other).start()

                load(j, b).wait()
                for cp in scatters(b):
                    cp.start()

        for cp in scatters((n_chunk - 1) % 2):
            cp.wait()

    return scatter_kernel(rows, idx_c)


def _sc_gather_rows(table, idx):
    m = idx.shape[0]
    w = table.shape[1]
    n_cores, n_workers = _sc_workers()
    ch = SC_CHUNK
    assert m % (2 * n_workers * ch) == 0
    n_chunk = m // (n_workers * ch)
    idx_c = idx.reshape(m // ch, 1, ch)

    @functools.partial(
        pl.kernel,
        mesh=plsc.VectorSubcoreMesh(core_axis_name="c", subcore_axis_name="s"),
        out_type=jax.ShapeDtypeStruct((m, w), table.dtype),
        scratch_types=[pltpu.VMEM((2, 1, ch), jnp.int32), pltpu.VMEM((2, ch, w), table.dtype),
                       pltpu.SemaphoreType.DMA((2,)), pltpu.SemaphoreType.DMA((2,))],
        name="sc_combine_gather",
    )
    def gather_kernel(table_hbm, idx_hbm, out_hbm, idx_v, rows_v, gather_sem, write_sem):
        base = (lax.axis_index("s") * n_cores + lax.axis_index("c")) * n_chunk

        def gather(b):
            return pltpu.make_async_copy(table_hbm.at[idx_v.at[b, 0]], rows_v.at[b], gather_sem.at[b])

        def write(j, b):
            return pltpu.make_async_copy(rows_v.at[b], out_hbm.at[pl.ds((base + j) * ch, ch)], write_sem.at[b])

        pltpu.sync_copy(idx_hbm.at[base], idx_v.at[0])
        gather(0).start()

        @pl.loop(0, n_chunk, step=2)
        def _(j0):
            for b in range(2):
                j = j0 + b
                other = 1 - b

                @pl.when(j >= 1)
                def _():
                    write(j - 1, other).wait()

                @pl.when(j + 1 < n_chunk)
                def _():
                    pltpu.sync_copy(idx_hbm.at[base + j + 1], idx_v.at[other])
                    gather(other).start()

                gather(b).wait()
                write(j, b).start()

        write(n_chunk - 1, (n_chunk - 1) % 2).wait()

    return gather_kernel(table, idx_c)


def _lane_vector(vals, offset):
    return jnp.zeros((1, LANES), F32).at[0, offset:offset + vals.shape[0]].set(vals.astype(F32))


def _moe(h2, info, cnt, x1, mod, ln_g, ln_b, w_up, b_up, w_down, b_down, layer, seq):
    t, dh = h2.shape
    a = t * TOP_K
    bm = EXPERT_BLOCK
    slots, pcum_v = _slots(info, cnt)
    pcum = pcum_v[0, :N_EXPERTS]
    dest = slots[:TOP_K]
    n_blocks = -(-a // bm) + N_EXPERTS
    block_e = jnp.minimum(jnp.sum(pcum[None, :] <= (jnp.arange(n_blocks) * bm)[:, None], axis=1),
                          N_EXPERTS - 1).astype(jnp.int32)
    n_used = (pcum[-1] // bm).astype(jnp.int32).reshape(1)
    group_end = jnp.sum(block_e[None, :] <= block_e[:, None], axis=1)
    next_e = jnp.where(group_end < n_used[0], block_e[jnp.minimum(group_end, n_blocks - 1)], -1).astype(jnp.int32)
    xbuf = _sc_scatter_rows(h2, dest, n_blocks * bm)
    ybuf = _expert_ffn(xbuf, block_e, next_e, n_used, w_up, b_up, w_down, b_down, layer)
    yg = _sc_gather_rows(ybuf, dest.reshape(a)).reshape(TOP_K, t, dh)
    return _combine(x1, mod, yg, info, ln_g, ln_b, seq)


def kernel(x, c, rel_bias, w_in, w_out, w_ada, b_ada, ln1_g, ln1_b, ln2_g, ln2_b, pool_w, pool_scale,
           attn_sinks, conv_w, gdn_a_log, gdn_dt_bias, gdn_norm_w, router_w, router_b,
           exp_w_up, exp_b_up, exp_w_down, exp_b_down):
    bsz, seq, d = x.shape
    depth = w_in.shape[0]
    t = bsz * seq
    assert d == D_MODEL and w_in.shape[2] == IN_DIM
    assert seq % GDN_SUPER == 0 and seq % (ATT_BLOCKS * WINDOW) == 0
    assert t % ROW_TILE == 0 and seq % ROW_TILE == 0

    mod_all = _modulation(c, w_ada, b_ada).reshape(depth, bsz, 6, d)
    bias = _band_bias(rel_bias)

    w_up_all = exp_w_up.reshape((depth * N_EXPERTS,) + exp_w_up.shape[2:])
    b_up_all = exp_b_up.reshape(depth * N_EXPERTS, 1, exp_b_up.shape[2])
    w_down_all = exp_w_down.reshape((depth * N_EXPERTS,) + exp_w_down.shape[2:])
    b_down_all = exp_b_down.reshape(depth * N_EXPERTS, 1, exp_b_down.shape[2])

    w_in_all = _take_cols(w_in, _IN_PERM).astype(BF16)

    x2d = x.reshape(t, d)
    for l in range(depth):
        mod = mod_all[l]
        w_out_p = _take_static(w_out[l], _OUT_PERM, 0).astype(BF16)
        ident = jnp.zeros((CONV_WIDTH, 1), F32).at[CONV_WIDTH - 1, 0].set(1.0)
        conv_p = jnp.where(jnp.asarray(_GDN_CONV_SRC >= 0), _take_cols(conv_w[l].astype(F32), _GDN_CONV_SRC),
                           ident)
        pool_bd = jnp.zeros((POOL_DIM, POOL_DIM), F32)
        for gi in range(len(POOL_WINDOWS)):
            sl = slice(gi * POOL_GROUP, (gi + 1) * POOL_GROUP)
            pool_bd = pool_bd.at[sl, sl].set(pool_w[l, gi].astype(F32))
        alog_v = _lane_vector(gdn_a_log[l], N_GDN_HEADS)
        dtb_v = _lane_vector(gdn_dt_bias[l], N_GDN_HEADS)
        nw_v = jnp.tile(gdn_norm_w[l].astype(F32), 2).reshape(1, LANES)
        rw = jnp.zeros((d, LANES), F32).at[:, :N_EXPERTS].set(router_w[l].astype(F32))
        rw_hi, rw_lo = _split_bf16(rw)
        rb = jnp.full((1, LANES), NEG_INF, F32).at[0, :N_EXPERTS].set(router_b[l].astype(F32))

        u_pool, aq, akv, gdn, ba = _in_projection(x2d, mod, w_in_all, l, seq)
        y_pool = _pool_mixer(u_pool, pool_bd.astype(BF16), pool_scale[l].astype(F32), seq)
        y_att = _swa_attention(aq, akv, bias, attn_sinks[l].astype(F32), seq)
        y_gdn = _gdn_mixer(gdn, ba, conv_p, alog_v, dtb_v, nw_v, seq)
        x1, h2, info, cnt = _out_projection(x2d, mod, y_pool, y_att, y_gdn, w_out_p, ln1_g[l], ln1_b[l],
                                            rw_hi, rw_lo, rb, seq)
        x2d = _moe(h2, info, cnt, x1, mod, ln2_g[l], ln2_b[l], w_up_all, b_up_all, w_down_all, b_down_all,
                   l, seq)
    return x2d.reshape(bsz, seq, d)
```

```python
import functools

import numpy as np
import jax
import jax.numpy as jnp
from jax import lax
from jax.experimental import pallas as pl
from jax.experimental.pallas import tpu as pltpu
from jax.experimental.pallas import tpu_sc as plsc

F32 = jnp.float32
BF16 = jnp.bfloat16

D_MODEL = 1024
HEAD_DIM = 64
POOL_DIM = 256
POOL_WINDOWS = (2, 4, 8, 16)
POOL_GROUP = 64
N_ATT_HEADS = 6
N_KV_HEADS = 2
ATT_DIM = 384
KV_DIM = 128
WINDOW = 128
N_BUCKETS = 32
MAX_DISTANCE = 128
N_GDN_HEADS = 6
GDN_DIM = 384
CONV_WIDTH = 4
GDN_CHUNK = 64
N_EXPERTS = 32
TOP_K = 4
EXPERT_DIM = 1024
SWIGLU_ALPHA = 1.702
SWIGLU_LIMIT = 7.0
DEPTH = 2
DEEPNORM_ALPHA = (2 * DEPTH) ** 0.25
LN_EPS = 1e-5
NORM_EPS = 1e-6
NEG_INF = -1e30

LANES = 128
SUBLANES = 8
VMEM_LIMIT = 56 * 1024 * 1024

ROW_TILE = 512
ATT_BLOCKS = 4
GDN_SUPER = 256
ROUTE_TILE = 2048
EXPERT_BLOCK = 512
SC_CHUNK = 64

_OFF_AQ = POOL_DIM
_OFF_AK = _OFF_AQ + ATT_DIM
_OFF_AV = _OFF_AK + KV_DIM
_OFF_GQ = _OFF_AV + KV_DIM
_OFF_GK = _OFF_GQ + GDN_DIM
_OFF_GV = _OFF_GK + GDN_DIM
_OFF_GZ = _OFF_GV + GDN_DIM
_OFF_GB = _OFF_GZ + GDN_DIM
_OFF_GA = _OFF_GB + N_GDN_HEADS
IN_DIM = _OFF_GA + N_GDN_HEADS

P_POOL = (0, POOL_DIM)
P_Q = (P_POOL[1], P_POOL[1] + ATT_DIM)
P_KV = (P_Q[1], P_Q[1] + 2 * KV_DIM)
P_GDN = (P_KV[1], P_KV[1] + 4 * GDN_DIM)
P_BA = (P_GDN[1], P_GDN[1] + LANES)
P_TOTAL = P_BA[1]


def _head_cols(off, h):
    return list(range(off + HEAD_DIM * h, off + HEAD_DIM * (h + 1)))


def _build_in_perm():
    cols = list(range(POOL_DIM))
    for p in range(N_ATT_HEADS // 2):
        cols += _head_cols(_OFF_AQ, p) + _head_cols(_OFF_AQ, p + 3)
    cols += list(range(_OFF_AK, _OFF_AK + 2 * KV_DIM))
    gdn_src = []
    for p in range(N_GDN_HEADS // 2):
        e, o = 2 * p, 2 * p + 1
        grp = (_head_cols(_OFF_GK, e) + _head_cols(_OFF_GQ, e)
               + _head_cols(_OFF_GQ, o) + _head_cols(_OFF_GK, o)
               + _head_cols(_OFF_GV, o) + _head_cols(_OFF_GV, e)
               + _head_cols(_OFF_GZ, o) + _head_cols(_OFF_GZ, e))
        cols += grp
        gdn_src += [c - _OFF_GQ if c < _OFF_GZ else -1 for c in grp]
    cols += list(range(_OFF_GB, _OFF_GB + 2 * N_GDN_HEADS))
    cols += [-1] * (LANES - 2 * N_GDN_HEADS)
    assert len(cols) == P_TOTAL
    return np.asarray(cols, np.int32), np.asarray(gdn_src, np.int32)


_IN_PERM, _GDN_CONV_SRC = _build_in_perm()


def _build_out_perm():
    rows = list(range(POOL_DIM))
    for p in range(N_ATT_HEADS // 2):
        rows += _head_cols(POOL_DIM, p) + _head_cols(POOL_DIM, p + 3)
    for p in range(N_GDN_HEADS // 2):
        rows += _head_cols(POOL_DIM + ATT_DIM, 2 * p + 1) + _head_cols(POOL_DIM + ATT_DIM, 2 * p)
    return np.asarray(rows, np.int32)


_OUT_PERM = _build_out_perm()


def _t5_bucket_line():
    n = np.maximum(2 * WINDOW - 1 - np.arange(3 * WINDOW - 1), 0)
    max_exact = N_BUCKETS // 2
    nf = np.maximum(n, 1).astype(np.float32)
    large = max_exact + (np.log(nf / max_exact) / np.float32(np.log(MAX_DISTANCE / max_exact))
                         * (N_BUCKETS - max_exact)).astype(np.int32)
    large = np.minimum(large, N_BUCKETS - 1)
    return np.where(n < max_exact, n, large).astype(np.int32)


_BUCKET_LINE = _t5_bucket_line()


def _band_bias(rel_bias):
    n_line = 3 * WINDOW - 1
    line = jnp.take(rel_bias.astype(F32), jnp.asarray(_BUCKET_LINE), axis=0).T
    heads = line.shape[0]
    padded = jnp.concatenate([line, jnp.zeros((heads, 1), F32)], axis=1)
    skew = jnp.tile(padded, (1, WINDOW))[:, :WINDOW * n_line].reshape(heads, WINDOW, n_line)
    return skew[:, :, WINDOW - 1:3 * WINDOW - 1]


def _take_static(w, perm, axis):
    parts = []
    start = 0
    for i in range(1, len(perm) + 1):
        run_ends = (i == len(perm) or ((perm[i] < 0) != (perm[i - 1] < 0))
                    or (perm[i] >= 0 and perm[i] != perm[i - 1] + 1))
        if run_ends:
            if perm[start] < 0:
                shape = list(w.shape)
                shape[axis] = i - start
                parts.append(jnp.zeros(shape, w.dtype))
            else:
                parts.append(lax.slice_in_dim(w, int(perm[start]), int(perm[start]) + (i - start), axis=axis))
            start = i
    return jnp.concatenate(parts, axis=axis)


def _take_cols(w, perm):
    return _take_static(w, perm, w.ndim - 1)


def _split_bf16(x):
    hi = x.astype(BF16)
    lo = (x - hi.astype(F32)).astype(BF16)
    return hi, lo


def _pack_bf16_pairs(x):
    n = x.shape[1] // 2
    bits = pltpu.bitcast(x.astype(BF16).astype(F32), jnp.int32)
    return lax.shift_right_logical(bits[:, :n], 16) | bits[:, n:]


def _unpack_bf16_pairs(u):
    lo = pltpu.bitcast(lax.shift_left(u, 16), F32)
    hi = pltpu.bitcast(u & jnp.int32(-65536), F32)
    return jnp.concatenate([lo, hi], axis=1)


def _dot(a, b):
    return jnp.dot(a, b, preferred_element_type=F32)


def _dot_nt(a, b):
    return lax.dot_general(a, b, (((1,), (1,)), ((), ())), preferred_element_type=F32)


def _dot_hi_exact_rhs(x, m_bf16):
    hi, lo = _split_bf16(x)
    return _dot(hi, m_bf16) + _dot(lo, m_bf16)


def _sigmoid(x):
    return 1.0 / (1.0 + jnp.exp(-x))


def _layer_norm(r, g, b):
    mu = jnp.mean(r, axis=-1, keepdims=True)
    d = r - mu
    var = jnp.mean(d * d, axis=-1, keepdims=True)
    return d * lax.rsqrt(var + LN_EPS) * g + b


def _cparams(sem):
    return pltpu.CompilerParams(dimension_semantics=sem, vmem_limit_bytes=VMEM_LIMIT)


def _mod_kernel(c_ref, w_ref, b_ref, o_ref):
    c = c_ref[...]
    ca = c * _sigmoid(c)
    ch, cl = _split_bf16(ca)
    wh, wl = _split_bf16(w_ref[0])
    o_ref[0] = _dot(ch, wh) + _dot(cl, wh) + _dot(ch, wl) + b_ref[0]


def _modulation(c, w_ada, b_ada):
    depth, d, n = w_ada.shape
    bsz = c.shape[0]
    tn = 512
    return pl.pallas_call(
        _mod_kernel,
        grid=(depth, n // tn),
        in_specs=[
            pl.BlockSpec((bsz, d), lambda l, j: (0, 0)),
            pl.BlockSpec((1, d, tn), lambda l, j: (l, 0, j)),
            pl.BlockSpec((1, 1, tn), lambda l, j: (l, 0, j)),
        ],
        out_specs=pl.BlockSpec((1, bsz, tn), lambda l, j: (l, 0, j)),
        out_shape=jax.ShapeDtypeStruct((depth, bsz, n), F32),
        compiler_params=_cparams(("arbitrary", "arbitrary")),
        name="adaln_mod",
    )(c, w_ada, b_ada.reshape(depth, 1, n))


def _inproj_kernel(x_ref, mod_ref, w_ref, pool_ref, q_ref, kv_ref, gdn_ref, ba_ref):
    sh = mod_ref[0, 0:1, :]
    sc = mod_ref[0, 1:2, :]
    h = (x_ref[...] * (1.0 + sc) + sh).astype(BF16)

    def mm(rng):
        return _dot(h, w_ref[0, :, rng[0]:rng[1]])

    pool_ref[...] = mm(P_POOL)
    q_ref[...] = mm(P_Q).astype(BF16)
    kv_ref[...] = mm(P_KV).astype(BF16)
    gdn_ref[...] = mm(P_GDN)
    ba_ref[...] = mm(P_BA)


def _in_projection(x2d, mod, w_in_all, layer, seq):
    t, d = x2d.shape
    tm = ROW_TILE
    widths = [r[1] - r[0] for r in (P_POOL, P_Q, P_KV, P_GDN, P_BA)]
    dtypes = [F32, BF16, BF16, F32, F32]
    return pl.pallas_call(
        _inproj_kernel,
        grid=(t // tm,),
        in_specs=[
            pl.BlockSpec((tm, d), lambda i: (i, 0)),
            pl.BlockSpec((1, 6, d), lambda i: ((i * tm) // seq, 0, 0)),
            pl.BlockSpec((1, d, P_TOTAL), lambda i: (layer, 0, 0)),
        ],
        out_specs=[pl.BlockSpec((tm, w), lambda i: (i, 0)) for w in widths],
        out_shape=[jax.ShapeDtypeStruct((t, w), dt) for w, dt in zip(widths, dtypes)],
        compiler_params=_cparams(("arbitrary",)),
        name="in_proj",
    )(x2d, mod, w_in_all)


def _pool_kernel(u_ref, w_ref, scale_ref, o_ref):
    u = u_ref[...]
    row = lax.broadcasted_iota(jnp.int32, u.shape, 0)
    lane = lax.broadcasted_iota(jnp.int32, u.shape, 1)

    def shifted(a, s):
        return jnp.where(row >= s, pltpu.roll(a, s, axis=0), 0.0)

    sums = []
    acc = u
    for wdt in POOL_WINDOWS:
        acc = acc + shifted(acc, wdt // 2)
        sums.append(acc)
    grp = lane // POOL_GROUP
    wsum = sums[-1]
    win = jnp.full(u.shape, POOL_WINDOWS[-1], jnp.int32)
    for gi in range(len(POOL_WINDOWS) - 2, -1, -1):
        wsum = jnp.where(grp == gi, sums[gi], wsum)
        win = jnp.where(grp == gi, POOL_WINDOWS[gi], win)
    cnt = jnp.minimum(row + 1, win).astype(F32)
    p = wsum / cnt - u
    y = _dot(p.astype(BF16), w_ref[...]) * scale_ref[...]
    o_ref[...] = y.astype(BF16)


def _pool_mixer(u, pool_w_bd, pool_scale, seq):
    t, c = u.shape
    return pl.pallas_call(
        _pool_kernel,
        grid=(t // seq,),
        in_specs=[
            pl.BlockSpec((seq, c), lambda b: (b, 0)),
            pl.BlockSpec((c, c), lambda b: (0, 0)),
            pl.BlockSpec((1, c), lambda b: (0, 0)),
        ],
        out_specs=pl.BlockSpec((seq, c), lambda b: (b, 0)),
        out_shape=jax.ShapeDtypeStruct((t, c), BF16),
        compiler_params=_cparams(("arbitrary",)),
        name="pool_mixer",
    )(u, pool_w_bd, pool_scale.reshape(1, c))


def _attn_kernel(sink_ref, q_ref, kvc_ref, kvp_ref, bias_ref, o_ref):
    step = pl.program_id(1)
    qi = lax.broadcasted_iota(jnp.int32, (WINDOW, 2 * WINDOW), 0)
    kj = lax.broadcasted_iota(jnp.int32, (WINDOW, 2 * WINDOW), 1)
    dist = qi + WINDOW - kj
    in_band = (dist >= 0) & (dist < WINDOW)
    lo = lax.broadcasted_iota(jnp.int32, (WINDOW, LANES), 1) < HEAD_DIM
    for sub in range(ATT_BLOCKS):
        r0 = sub * WINDOW
        prev = kvp_ref[...] if sub == 0 else kvc_ref[r0 - WINDOW:r0, :]
        kv = jnp.concatenate([prev, kvc_ref[r0:r0 + WINDOW, :]], axis=0)
        k = kv[:, :KV_DIM]
        v = kv[:, KV_DIM:]
        valid = in_band & ((kj >= WINDOW) | (step > 0)) if sub == 0 else in_band
        for p in range(N_ATT_HEADS // 2):
            qp = q_ref[r0:r0 + WINDOW, p * LANES:(p + 1) * LANES]
            halves = []
            for half in range(2):
                h = p + 3 * half
                qm = jnp.where(lo if half == 0 else jnp.logical_not(lo), qp, jnp.zeros_like(qp))
                s = _dot_nt(qm, k) * (HEAD_DIM ** -0.5)
                s = jnp.where(valid, s + bias_ref[h], NEG_INF)
                sink = sink_ref[h]
                m = jnp.maximum(jnp.max(s, axis=-1, keepdims=True), sink)
                pr = jnp.exp(s - m)
                den = jnp.sum(pr, axis=-1, keepdims=True) + jnp.exp(sink - m)
                halves.append(_dot(pr.astype(BF16), v) / den)
            o_ref[r0:r0 + WINDOW, p * LANES:(p + 1) * LANES] = (
                jnp.where(lo, halves[0], halves[1]).astype(BF16))


def _swa_attention(q, kv, bias, sinks, seq):
    t = q.shape[0]
    rows = ATT_BLOCKS * WINDOW
    nblk = seq // rows
    return pl.pallas_call(
        _attn_kernel,
        grid=(t // seq, nblk),
        in_specs=[
            pl.BlockSpec(memory_space=pltpu.SMEM),
            pl.BlockSpec((rows, ATT_DIM), lambda b, n: (b * nblk + n, 0)),
            pl.BlockSpec((rows, 2 * KV_DIM), lambda b, n: (b * nblk + n, 0)),
            pl.BlockSpec((WINDOW, 2 * KV_DIM),
                         lambda b, n: (jnp.maximum((b * nblk + n) * ATT_BLOCKS - 1, 0), 0)),
            pl.BlockSpec((N_ATT_HEADS, WINDOW, 2 * WINDOW), lambda b, n: (0, 0, 0)),
        ],
        out_specs=pl.BlockSpec((rows, ATT_DIM), lambda b, n: (b * nblk + n, 0)),
        out_shape=jax.ShapeDtypeStruct((t, ATT_DIM), BF16),
        compiler_params=_cparams(("arbitrary", "arbitrary")),
        name="swa_attention",
    )(sinks, q, kv, kv, bias)


def _gdn_kernel(x_ref, halo_ref, ba_ref, cw_ref, alog_ref, dtb_ref, nw_ref, y_ref, state_ref, xs_ref):
    sc_id = pl.program_id(1)
    rows = GDN_SUPER
    nchunk = rows // GDN_CHUNK
    c_sz = GDN_CHUNK

    @pl.when(sc_id == 0)
    def _():
        state_ref[...] = jnp.zeros_like(state_ref)

    xs_ref[:SUBLANES, :] = jnp.where(sc_id == 0, 0.0, halo_ref[...])
    xs_ref[SUBLANES:, :] = x_ref[...]
    acc = x_ref[...] * cw_ref[CONV_WIDTH - 1:CONV_WIDTH, :]
    for s in range(1, CONV_WIDTH):
        acc = acc + xs_ref[SUBLANES - s:SUBLANES - s + rows, :] * cw_ref[CONV_WIDTH - 1 - s:CONV_WIDTH - s, :]
    act = acc * _sigmoid(acc)

    ri = lax.broadcasted_iota(jnp.int32, (rows, rows), 0)
    ci = lax.broadcasted_iota(jnp.int32, (rows, rows), 1)
    same_chunk = (ri // c_sz) == (ci // c_sz)
    incl = same_chunk & (ri >= ci)
    strict = same_chunk & (ri > ci)
    tri_incl = jnp.where(incl, 1.0, 0.0).astype(BF16)
    eye = jnp.where(ri == ci, 1.0, 0.0)
    blk = []
    bsz = SUBLANES
    while bsz <= c_sz:
        blk.append((ri // bsz) == (ci // bsz))
        bsz *= 2
    li = lax.broadcasted_iota(jnp.int32, (LANES, LANES), 0)
    lj = lax.broadcasted_iota(jnp.int32, (LANES, LANES), 1)
    half_ones = jnp.where((li // HEAD_DIM) == (lj // HEAD_DIM), 1.0, 0.0).astype(BF16)
    lane_lo = lax.broadcasted_iota(jnp.int32, (rows, LANES), 1) < HEAD_DIM
    lane_lo_c = lax.broadcasted_iota(jnp.int32, (c_sz, LANES), 1) < HEAD_DIM

    ba = ba_ref[...]
    beta_all = _sigmoid(ba)
    sp_in = ba + dtb_ref[...]
    softplus = jnp.maximum(sp_in, 0.0) + jnp.log(1.0 + jnp.exp(-jnp.abs(sp_in)))
    g_all = -jnp.exp(alog_ref[...]) * softplus
    gcum = _dot_hi_exact_rhs_lhs(tri_incl, g_all)
    gcum_t = gcum.T

    heads = range(N_GDN_HEADS)
    lane_hi = jnp.logical_not(lane_lo)
    lane_hi_c = jnp.logical_not(lane_lo_c)
    mk = [lane_lo if h % 2 == 0 else lane_hi for h in heads]
    mk_c = [lane_lo_c if h % 2 == 0 else lane_hi_c for h in heads]
    scale = HEAD_DIM ** -0.5

    def bdot(a, b):
        return _dot(a.astype(BF16), b.astype(BF16))

    xk, xq, gn, gc_col, beta, eg = [], [], [], [], [], []
    for h in heads:
        base = 4 * LANES * (h // 2) + LANES * (h % 2)
        g = act[:, base:base + LANES]
        g = g * lax.rsqrt(_dot_hi_exact_rhs(g * g, half_ones) + NORM_EPS)
        gn.append(g)
        xk.append(jnp.where(mk[h], g, 0.0))
        xq.append(jnp.where(mk[h], pltpu.roll(g, HEAD_DIM, axis=1), 0.0) * scale)
        beta.append(beta_all[:, h:h + 1])
        gc_col.append(gcum[:, N_GDN_HEADS + h:N_GDN_HEADS + h + 1])
        eg.append(jnp.exp(gc_col[h]))

    l_mat, attn, rhs = [], [], []
    for h in heads:
        gc_row = gcum_t[N_GDN_HEADS + h:N_GDN_HEADS + h + 1, :]
        decay = jnp.exp(jnp.where(incl, gc_col[h] - gc_row, -jnp.inf))
        xk_b = xk[h].astype(BF16)
        kk = _dot_nt((xk[h] * beta[h]).astype(BF16), xk_b)
        l_mat.append(jnp.where(strict, kk * decay, 0.0))
        attn.append((_dot_nt(xq[h].astype(BF16), xk_b) * decay).astype(BF16))
        vv = act[:, 4 * LANES * (h // 2) + 2 * LANES:4 * LANES * (h // 2) + 3 * LANES]
        rhs.append(jnp.where(mk[h], gn[h] * eg[h], vv) * beta[h])

    a1 = [jnp.where(blk[0], -l_mat[h], 0.0).astype(BF16) for h in heads]
    a2 = [_dot(a1[h], a1[h]).astype(BF16) for h in heads]
    a4 = [_dot(a2[h], a2[h]).astype(BF16) for h in heads]
    inv = [eye + a1[h].astype(F32) for h in heads]
    inv = [inv[h] + _dot(a2[h], inv[h].astype(BF16)) for h in heads]
    inv = [inv[h] + _dot(a4[h], inv[h].astype(BF16)) for h in heads]
    for lvl in range(1, len(blk) - 1):
        band = blk[lvl] & jnp.logical_not(blk[lvl - 1])
        inv_b = [inv[h].astype(BF16) for h in heads]
        mid = [_dot(jnp.where(band, l_mat[h], 0.0).astype(BF16), inv_b[h]) for h in heads]
        inv = [inv[h] - _dot(inv_b[h], mid[h].astype(BF16)) for h in heads]
    band = blk[-1] & jnp.logical_not(blk[-2])
    inv_b = [inv[h].astype(BF16) for h in heads]
    half = [_dot(inv_b[h], rhs[h].astype(BF16)) for h in heads]
    mid = [_dot(jnp.where(band, l_mat[h], 0.0).astype(BF16), half[h].astype(BF16)) for h in heads]
    sol = [half[h] - _dot(inv_b[h], mid[h].astype(BF16)) for h in heads]

    st = [state_ref[h] for h in heads]
    vn_parts = [[] for _ in heads]
    qs_parts = [[] for _ in heads]
    for c in range(nchunk):
        r0 = c * c_sz
        for h in heads:
            sol_c = sol[h][r0:r0 + c_sz]
            glast = gcum[r0 + c_sz - 1:r0 + c_sz, N_GDN_HEADS + h:N_GDN_HEADS + h + 1]
            kd_t = (xk[h][r0:r0 + c_sz] * jnp.exp(glast - gc_col[h][r0:r0 + c_sz])).T
            qd = xq[h][r0:r0 + c_sz] * eg[h][r0:r0 + c_sz]
            lhs = jnp.concatenate([jnp.where(mk_c[h], sol_c, 0.0), qd], axis=0)
            m1 = bdot(lhs, st[h])
            vn = sol_c - m1[:c_sz]
            vn_parts[h].append(vn)
            qs_parts[h].append(m1[c_sz:])
            st[h] = st[h] * jnp.exp(glast) + bdot(kd_t, jnp.where(mk_c[h], 0.0, vn))
    for h in heads:
        state_ref[h] = st[h]

    for p in range(N_GDN_HEADS // 2):
        o_pair = []
        for h in (2 * p, 2 * p + 1):
            vn_all = jnp.concatenate(vn_parts[h], axis=0)
            o_pair.append(jnp.concatenate(qs_parts[h], axis=0) + _dot(attn[h], vn_all.astype(BF16)))
        o = jnp.where(lane_lo, o_pair[1], o_pair[0])
        ms = _dot_hi_exact_rhs(o * o, half_ones) * (1.0 / HEAD_DIM)
        zz = act[:, 4 * LANES * p + 3 * LANES:4 * LANES * (p + 1)]
        y = o * lax.rsqrt(ms + NORM_EPS) * nw_ref[...] * zz
        y_ref[:, p * LANES:(p + 1) * LANES] = y.astype(BF16)


def _dot_hi_exact_rhs_lhs(m_bf16, x):
    hi, lo = _split_bf16(x)
    return _dot(m_bf16, hi) + _dot(m_bf16, lo)


def _gdn_mixer(gdn, ba, conv_p, alog_v, dtb_v, nw_v, seq):
    t, c = gdn.shape
    rows = GDN_SUPER
    nsc = seq // rows
    hb = rows // SUBLANES
    return pl.pallas_call(
        _gdn_kernel,
        grid=(t // seq, nsc),
        in_specs=[
            pl.BlockSpec((rows, c), lambda b, s: (b * nsc + s, 0)),
            pl.BlockSpec((SUBLANES, c), lambda b, s: (jnp.maximum((b * nsc + s) * hb - 1, 0), 0)),
            pl.BlockSpec((rows, LANES), lambda b, s: (b * nsc + s, 0)),
            pl.BlockSpec((CONV_WIDTH, c), lambda b, s: (0, 0)),
            pl.BlockSpec((1, LANES), lambda b, s: (0, 0)),
            pl.BlockSpec((1, LANES), lambda b, s: (0, 0)),
            pl.BlockSpec((1, LANES), lambda b, s: (0, 0)),
        ],
        out_specs=pl.BlockSpec((rows, GDN_DIM), lambda b, s: (b * nsc + s, 0)),
        out_shape=jax.ShapeDtypeStruct((t, GDN_DIM), BF16),
        scratch_shapes=[pltpu.VMEM((N_GDN_HEADS, LANES, LANES), F32), pltpu.VMEM((rows + SUBLANES, c), F32)],
        compiler_params=_cparams(("arbitrary", "arbitrary")),
        name="gdn_mixer",
    )(gdn, gdn, ba, conv_p, alog_v, dtb_v, nw_v)


def _route_tile(logits, before, carry_ref, live):
    shape = logits.shape
    lane = lax.broadcasted_iota(jnp.int32, shape, 1).astype(F32)
    work = logits
    vals, idxs = [], []
    for _k in range(TOP_K):
        m = jnp.max(work, axis=-1, keepdims=True)
        idx = jnp.min(jnp.where(work == m, lane, float(LANES)), axis=-1, keepdims=True)
        vals.append(m)
        idxs.append(idx)
        work = jnp.where(lane == idx, -jnp.inf, work)
    exps = [jnp.exp(v - vals[0]) for v in vals]
    den = exps[0] + exps[1] + exps[2] + exps[3]
    onehots = [lane == idx for idx in idxs]
    member = jnp.zeros(shape, F32)
    for oh in onehots:
        member = member + jnp.where(oh, 1.0, 0.0)
    rank = _dot(before, member.astype(BF16)) + carry_ref[...]
    carry_ref[...] = carry_ref[...] + live * jnp.sum(member, axis=0, keepdims=True)
    info = jnp.zeros(shape, F32)
    for k in range(TOP_K):
        rank_k = jnp.sum(jnp.where(onehots[k], rank, 0.0), axis=-1, keepdims=True)
        info = jnp.where(lane == float(k), idxs[k], info)
        info = jnp.where(lane == float(TOP_K + k), rank_k, info)
        info = jnp.where(lane == float(2 * TOP_K + k), exps[k] / den, info)
    return info


def _outproj_kernel(x_ref, mod_ref, yp_ref, ya_ref, yg_ref, wp_ref, wa_ref, wg_ref, lng_ref, lnb_ref,
                    rwh_ref, rwl_ref, rb_ref, before_ref, x1_ref, h2_ref, info_ref, cnt_ref,
                    carry_ref, logit_s):
    step = pl.program_id(0)

    @pl.when(step == 0)
    def _():
        carry_ref[...] = jnp.zeros_like(carry_ref)
        logit_s[...] = jnp.zeros_like(logit_s)

    y = _dot(yp_ref[...], wp_ref[...]) + _dot(ya_ref[...], wa_ref[...]) + _dot(yg_ref[...], wg_ref[...])
    live = jnp.where(step > 0, 1.0, 0.0)
    info_ref[...] = _route_tile(logit_s[...], before_ref[...], carry_ref, live)
    cnt_ref[...] = carry_ref[...]
    g1 = mod_ref[0, 2:3, :]
    sh2 = mod_ref[0, 3:4, :]
    sc2 = mod_ref[0, 4:5, :]
    x1 = _layer_norm(DEEPNORM_ALPHA * x_ref[...] + g1 * y, lng_ref[...], lnb_ref[...])
    x1_ref[...] = x1
    h2 = x1 * (1.0 + sc2) + sh2
    hh, hl = _split_bf16(h2)
    h2_ref[...] = _pack_bf16_pairs(h2)
    logit_s[...] = _dot(hh, rwh_ref[...]) + _dot(hl, rwh_ref[...]) + _dot(hh, rwl_ref[...]) + rb_ref[...]


def _out_projection(x2d, mod, yp, ya, yg, w_out_p, ln_g, ln_b, rw_hi, rw_lo, rb, seq):
    t, d = x2d.shape
    tm = ROW_TILE
    wp = w_out_p[:POOL_DIM]
    wa = w_out_p[POOL_DIM:POOL_DIM + ATT_DIM]
    wg = w_out_p[POOL_DIM + ATT_DIM:]
    last = t // tm - 1
    row = lambda i: (jnp.minimum(i, last), 0)
    routed = lambda i: (jnp.maximum(i - 1, 0), 0)
    fixed = lambda i: (0, 0)
    return pl.pallas_call(
        _outproj_kernel,
        grid=(t // tm + 1,),
        in_specs=[
            pl.BlockSpec((tm, d), row),
            pl.BlockSpec((1, 6, d), lambda i: ((jnp.minimum(i, last) * tm) // seq, 0, 0)),
            pl.BlockSpec((tm, POOL_DIM), row),
            pl.BlockSpec((tm, ATT_DIM), row),
            pl.BlockSpec((tm, GDN_DIM), row),
            pl.BlockSpec((POOL_DIM, d), fixed),
            pl.BlockSpec((ATT_DIM, d), fixed),
            pl.BlockSpec((GDN_DIM, d), fixed),
            pl.BlockSpec((1, d), fixed),
            pl.BlockSpec((1, d), fixed),
            pl.BlockSpec((d, LANES), fixed),
            pl.BlockSpec((d, LANES), fixed),
            pl.BlockSpec((1, LANES), fixed),
            pl.BlockSpec((tm, tm), fixed),
        ],
        out_specs=[pl.BlockSpec((tm, d), row), pl.BlockSpec((tm, d // 2), row), pl.BlockSpec((tm, LANES), routed),
                   pl.BlockSpec((1, LANES), fixed)],
        out_shape=[jax.ShapeDtypeStruct((t, d), F32), jax.ShapeDtypeStruct((t, d // 2), jnp.int32),
                   jax.ShapeDtypeStruct((t, LANES), F32), jax.ShapeDtypeStruct((1, LANES), F32)],
        scratch_shapes=[pltpu.VMEM((1, LANES), F32), pltpu.VMEM((tm, LANES), F32)],
        compiler_params=_cparams(("arbitrary",)),
        name="out_proj_ln_route",
    )(x2d, mod, yp, ya, yg, wp, wa, wg, ln_g.reshape(1, d), ln_b.reshape(1, d), rw_hi, rw_lo, rb,
      jnp.tril(jnp.ones((tm, tm), BF16), -1))


def _slot_kernel(info_ref, cnt_ref, dest_ref, pcum_ref, pstart_ref):
    shape = info_ref.shape
    lane = lax.broadcasted_iota(jnp.int32, shape, 1).astype(F32)

    @pl.when(pl.program_id(0) == 0)
    def _():
        cnt = jnp.broadcast_to(cnt_ref[...], (SUBLANES, LANES))
        padded = jnp.floor((cnt + float(EXPERT_BLOCK - 1)) * (1.0 / EXPERT_BLOCK)) * float(EXPERT_BLOCK)
        lane8 = lax.broadcasted_iota(jnp.int32, (SUBLANES, LANES), 1)
        acc = padded
        step = 1
        while step < LANES:
            acc = acc + jnp.where(lane8 >= step, pltpu.roll(acc, step, axis=1), 0.0)
            step *= 2
        pstart_ref[...] = (acc - padded)[:1]
        pcum_ref[...] = acc[:1].astype(jnp.int32)

    info = info_ref[...]
    slots = jnp.zeros(shape, F32)
    for k in range(TOP_K):
        onehot = lane == info[:, k:k + 1]
        start = jnp.sum(jnp.where(onehot, pstart_ref[...], 0.0), axis=-1, keepdims=True)
        slots = jnp.where(lane == float(k), start + info[:, TOP_K + k:TOP_K + k + 1], slots)
    dest_ref[...] = slots.T[:SUBLANES].astype(jnp.int32)


def _slots(info, cnt):
    t = info.shape[0]
    tm = min(ROUTE_TILE, t)
    assert t % tm == 0
    return pl.pallas_call(
        _slot_kernel,
        grid=(t // tm,),
        in_specs=[pl.BlockSpec((tm, LANES), lambda i: (i, 0)), pl.BlockSpec((1, LANES), lambda i: (0, 0))],
        out_specs=[pl.BlockSpec((SUBLANES, tm), lambda i: (0, i)), pl.BlockSpec((1, LANES), lambda i: (0, 0))],
        out_shape=[jax.ShapeDtypeStruct((SUBLANES, t), jnp.int32), jax.ShapeDtypeStruct((1, LANES), jnp.int32)],
        scratch_shapes=[pltpu.VMEM((1, LANES), F32)],
        compiler_params=_cparams(("arbitrary",)),
        name="moe_slots",
    )(info, cnt)


def _expert_kernel(e0, be_ref, nxt_ref, val_ref, nu_ref, x_ref, wup_hbm, bup_ref, wdn_hbm, bdn_ref, y_ref,
                   wup_st, wdn_st, wup_bf, wdn_bf, sems):
    i = pl.program_id(0)
    e = be_ref[i]
    prev = be_ref[jnp.maximum(i - 1, 0)]
    used = i < nu_ref[0]

    def weight_copies(expert):
        return (pltpu.make_async_copy(wup_hbm.at[e0 + expert], wup_st, sems.at[0]),
                pltpu.make_async_copy(wdn_hbm.at[e0 + expert], wdn_st, sems.at[1]))

    @pl.when(i == 0)
    def _():
        for cp in weight_copies(e):
            cp.start()

    @pl.when(used & ((i == 0) | (e != prev)))
    def _():
        for cp in weight_copies(e):
            cp.wait()
        wup_bf[...] = wup_st[...].astype(BF16)
        wdn_bf[...] = wdn_st[...].astype(BF16)

        @pl.when(nxt_ref[i] >= 0)
        def _():
            for cp in weight_copies(nxt_ref[i]):
                cp.start()

    def ffn(rows):
        xb = _unpack_bf16_pairs(x_ref[:rows, :]).astype(BF16)
        hb = _dot(xb, wup_bf[...]) + bup_ref[0]
        x_glu = jnp.minimum(hb[:, :EXPERT_DIM], SWIGLU_LIMIT)
        x_lin = jnp.clip(hb[:, EXPERT_DIM:], -SWIGLU_LIMIT, SWIGLU_LIMIT)
        act = x_glu * _sigmoid(SWIGLU_ALPHA * x_glu) * (x_lin + 1.0)
        y = _dot(act.astype(BF16), wdn_bf[...]) + bdn_ref[0]
        y_ref[:rows, :] = _pack_bf16_pairs(y)

    half_rows = x_ref.shape[0] // 2
    small = val_ref[i] <= half_rows

    @pl.when(used & jnp.logical_not(small))
    def _():
        ffn(x_ref.shape[0])

    @pl.when(used & small)
    def _():
        ffn(half_rows)
        y_ref[half_rows:, :] = jnp.zeros((x_ref.shape[0] - half_rows, y_ref.shape[1]), y_ref.dtype)

    @pl.when(i >= nu_ref[0])
    def _():
        y_ref[...] = jnp.zeros_like(y_ref)


def _expert_ffn(xbuf, block_e, next_e, valid, n_used, w_up, b_up, w_down, b_down, layer):
    p, dh = xbuf.shape
    d = 2 * dh
    bm = EXPERT_BLOCK
    ne, _, n_up = w_up.shape
    e0 = layer * N_EXPERTS
    grid_spec = pltpu.PrefetchScalarGridSpec(
        num_scalar_prefetch=4,
        grid=(p // bm,),
        in_specs=[
            pl.BlockSpec((bm, dh), lambda i, be, nx, vl, nu: (i, 0)),
            pl.BlockSpec(memory_space=pl.ANY),
            pl.BlockSpec((1, 1, n_up), lambda i, be, nx, vl, nu: (e0 + be[i], 0, 0)),
            pl.BlockSpec(memory_space=pl.ANY),
            pl.BlockSpec((1, 1, d), lambda i, be, nx, vl, nu: (e0 + be[i], 0, 0)),
        ],
        out_specs=pl.BlockSpec((bm, dh), lambda i, be, nx, vl, nu: (i, 0)),
        scratch_shapes=[pltpu.VMEM((d, n_up), F32), pltpu.VMEM((EXPERT_DIM, d), F32),
                        pltpu.VMEM((d, n_up), BF16), pltpu.VMEM((EXPERT_DIM, d), BF16),
                        pltpu.SemaphoreType.DMA((2,))],
    )
    return pl.pallas_call(
        functools.partial(_expert_kernel, e0),
        grid_spec=grid_spec,
        out_shape=jax.ShapeDtypeStruct((p, dh), jnp.int32),
        compiler_params=_cparams(("arbitrary",)),
        name="expert_ffn",
    )(block_e, next_e, valid, n_used, xbuf, w_up, b_up, w_down, b_down)


def _combine_kernel(x1_ref, mod_ref, yg_ref, info_ref, lng_ref, lnb_ref, o_ref):
    info = info_ref[...]
    y = jnp.zeros(x1_ref.shape, F32)
    for k in range(TOP_K):
        gate = info[:, 2 * TOP_K + k:2 * TOP_K + k + 1]
        y = y + gate * _unpack_bf16_pairs(yg_ref[k])
    g2 = mod_ref[0, 5:6, :]
    o_ref[...] = _layer_norm(DEEPNORM_ALPHA * x1_ref[...] + g2 * y, lng_ref[...], lnb_ref[...])


def _combine(x1, mod, yg, info, ln_g, ln_b, seq):
    t, d = x1.shape
    tm = ROW_TILE
    row = lambda i: (i, 0)
    fixed = lambda i: (0, 0)
    return pl.pallas_call(
        _combine_kernel,
        grid=(t // tm,),
        in_specs=[
            pl.BlockSpec((tm, d), row),
            pl.BlockSpec((1, 6, d), lambda i: ((i * tm) // seq, 0, 0)),
            pl.BlockSpec((TOP_K, tm, d // 2), lambda i: (0, i, 0)),
            pl.BlockSpec((tm, LANES), row),
            pl.BlockSpec((1, d), fixed),
            pl.BlockSpec((1, d), fixed),
        ],
        out_specs=pl.BlockSpec((tm, d), row),
        out_shape=jax.ShapeDtypeStruct((t, d), F32),
        compiler_params=_cparams(("arbitrary",)),
        name="moe_combine_ln",
    )(x1, mod, yg, info, ln_g.reshape(1, d), ln_b.reshape(1, d))


def _sc_workers():
    info = plsc.get_sparse_core_info()
    return info.num_cores, info.num_cores * info.num_subcores


def _sc_scatter_rows(rows, idx, n_out):
    t, w = rows.shape
    kk = idx.shape[0]
    n_cores, n_workers = _sc_workers()
    ch = SC_CHUNK
    assert t % (2 * n_workers * ch) == 0
    n_chunk = t // (n_workers * ch)
    idx_c = jnp.transpose(idx.reshape(kk, t // ch, ch), (1, 0, 2))

    @functools.partial(
        pl.kernel,
        mesh=plsc.VectorSubcoreMesh(core_axis_name="c", subcore_axis_name="s"),
        out_type=jax.ShapeDtypeStruct((n_out, w), rows.dtype),
        scratch_types=[pltpu.VMEM((2, kk, ch), jnp.int32), pltpu.VMEM((2, ch, w), rows.dtype),
                       pltpu.SemaphoreType.DMA((2,)), pltpu.SemaphoreType.DMA((2,))],
        name="sc_dispatch_scatter",
    )
    def scatter_kernel(rows_hbm, idx_hbm, out_hbm, idx_v, rows_v, load_sem, scat_sem):
        base = (lax.axis_index("s") * n_cores + lax.axis_index("c")) * n_chunk

        def load(j, b):
            return pltpu.make_async_copy(rows_hbm.at[pl.ds((base + j) * ch, ch)], rows_v.at[b], load_sem.at[b])

        def scatters(b):
            return [pltpu.make_async_copy(rows_v.at[b], out_hbm.at[idx_v.at[b, q]], scat_sem.at[b])
                    for q in range(kk)]

        pltpu.sync_copy(idx_hbm.at[base], idx_v.at[0])
        load(0, 0).start()

        @pl.loop(0, n_chunk, step=2)
        def _(j0):
            for b in range(2):
                j = j0 + b
                other = 1 - b

                @pl.when(j >= 1)
                def _():
                    for cp in scatters(other):
                        cp.wait()

                @pl.when(j + 1 < n_chunk)
                def _():
                    pltpu.sync_copy(idx_hbm.at[base + j + 1], idx_v.at[other])
                    load(j + 1, other).start()

                load(j, b).wait()
                for cp in scatters(b):
                    cp.start()

        for cp in scatters((n_chunk - 1) % 2):
            cp.wait()

    return scatter_kernel(rows, idx_c)


def _sc_gather_rows(table, idx):
    m = idx.shape[0]
    w = table.shape[1]
    n_cores, n_workers = _sc_workers()
    ch = SC_CHUNK
    assert m % (2 * n_workers * ch) == 0
    n_chunk = m // (n_workers * ch)
    idx_c = idx.reshape(m // ch, 1, ch)

    @functools.partial(
        pl.kernel,
        mesh=plsc.VectorSubcoreMesh(core_axis_name="c", subcore_axis_name="s"),
        out_type=jax.ShapeDtypeStruct((m, w), table.dtype),
        scratch_types=[pltpu.VMEM((2, 1, ch), jnp.int32), pltpu.VMEM((2, ch, w), table.dtype),
                       pltpu.SemaphoreType.DMA((2,)), pltpu.SemaphoreType.DMA((2,))],
        name="sc_combine_gather",
    )
    def gather_kernel(table_hbm, idx_hbm, out_hbm, idx_v, rows_v, gather_sem, write_sem):
        base = (lax.axis_index("s") * n_cores + lax.axis_index("c")) * n_chunk

        def gather(b):
            return pltpu.make_async_copy(table_hbm.at[idx_v.at[b, 0]], rows_v.at[b], gather_sem.at[b])

        def write(j, b):
            return pltpu.make_async_copy(rows_v.at[b], out_hbm.at[pl.ds((base + j) * ch, ch)], write_sem.at[b])

        pltpu.sync_copy(idx_hbm.at[base], idx_v.at[0])
        gather(0).start()

        @pl.loop(0, n_chunk, step=2)
        def _(j0):
            for b in range(2):
                j = j0 + b
                other = 1 - b

                @pl.when(j >= 1)
                def _():
                    write(j - 1, other).wait()

                @pl.when(j + 1 < n_chunk)
                def _():
                    pltpu.sync_copy(idx_hbm.at[base + j + 1], idx_v.at[other])
                    gather(other).start()

                gather(b).wait()
                write(j, b).start()

        write(n_chunk - 1, (n_chunk - 1) % 2).wait()

    return gather_kernel(table, idx_c)


def _lane_vector(vals, offset):
    return jnp.zeros((1, LANES), F32).at[0, offset:offset + vals.shape[0]].set(vals.astype(F32))


def _moe(h2, info, cnt, x1, mod, ln_g, ln_b, w_up, b_up, w_down, b_down, layer, seq):
    t, dh = h2.shape
    a = t * TOP_K
    bm = EXPERT_BLOCK
    slots, pcum_v = _slots(info, cnt)
    pcum = pcum_v[0, :N_EXPERTS]
    dest = slots[:TOP_K]
    n_blocks = -(-a // bm) + N_EXPERTS
    block_e = jnp.minimum(jnp.sum(pcum[None, :] <= (jnp.arange(n_blocks) * bm)[:, None], axis=1),
                          N_EXPERTS - 1).astype(jnp.int32)
    n_used = (pcum[-1] // bm).astype(jnp.int32).reshape(1)
    group_end = jnp.sum(block_e[None, :] <= block_e[:, None], axis=1)
    next_e = jnp.where(group_end < n_used[0], block_e[jnp.minimum(group_end, n_blocks - 1)], -1).astype(jnp.int32)
    counts = cnt[0, :N_EXPERTS].astype(jnp.int32)
    pstart = pcum - ((counts + bm - 1) // bm) * bm
    valid = jnp.clip(counts[block_e] - (jnp.arange(n_blocks) * bm - pstart[block_e]), 0, bm).astype(jnp.int32)
    xbuf = _sc_scatter_rows(h2, dest, n_blocks * bm)
    ybuf = _expert_ffn(xbuf, block_e, next_e, valid, n_used, w_up, b_up, w_down, b_down, layer)
    yg = _sc_gather_rows(ybuf, dest.reshape(a)).reshape(TOP_K, t, dh)
    return _combine(x1, mod, yg, info, ln_g, ln_b, seq)


def kernel(x, c, rel_bias, w_in, w_out, w_ada, b_ada, ln1_g, ln1_b, ln2_g, ln2_b, pool_w, pool_scale,
           attn_sinks, conv_w, gdn_a_log, gdn_dt_bias, gdn_norm_w, router_w, router_b,
           exp_w_up, exp_b_up, exp_w_down, exp_b_down):
    bsz, seq, d = x.shape
    depth = w_in.shape[0]
    t = bsz * seq
    assert d == D_MODEL and w_in.shape[2] == IN_DIM
    assert seq % GDN_SUPER == 0 and seq % (ATT_BLOCKS * WINDOW) == 0
    assert t % ROW_TILE == 0 and seq % ROW_TILE == 0

    mod_all = _modulation(c, w_ada, b_ada).reshape(depth, bsz, 6, d)
    bias = _band_bias(rel_bias)

    w_up_all = exp_w_up.reshape((depth * N_EXPERTS,) + exp_w_up.shape[2:])
    b_up_all = exp_b_up.reshape(depth * N_EXPERTS, 1, exp_b_up.shape[2])
    w_down_all = exp_w_down.reshape((depth * N_EXPERTS,) + exp_w_down.shape[2:])
    b_down_all = exp_b_down.reshape(depth * N_EXPERTS, 1, exp_b_down.shape[2])

    w_in_all = _take_cols(w_in, _IN_PERM).astype(BF16)

    x2d = x.reshape(t, d)
    for l in range(depth):
        mod = mod_all[l]
        w_out_p = _take_static(w_out[l], _OUT_PERM, 0).astype(BF16)
        ident = jnp.zeros((CONV_WIDTH, 1), F32).at[CONV_WIDTH - 1, 0].set(1.0)
        conv_p = jnp.where(jnp.asarray(_GDN_CONV_SRC >= 0), _take_cols(conv_w[l].astype(F32), _GDN_CONV_SRC),
                           ident)
        pool_bd = jnp.zeros((POOL_DIM, POOL_DIM), F32)
        for gi in range(len(POOL_WINDOWS)):
            sl = slice(gi * POOL_GROUP, (gi + 1) * POOL_GROUP)
            pool_bd = pool_bd.at[sl, sl].set(pool_w[l, gi].astype(F32))
        alog_v = _lane_vector(gdn_a_log[l], N_GDN_HEADS)
        dtb_v = _lane_vector(gdn_dt_bias[l], N_GDN_HEADS)
        nw_v = jnp.tile(gdn_norm_w[l].astype(F32), 2).reshape(1, LANES)
        rw = jnp.zeros((d, LANES), F32).at[:, :N_EXPERTS].set(router_w[l].astype(F32))
        rw_hi, rw_lo = _split_bf16(rw)
        rb = jnp.full((1, LANES), NEG_INF, F32).at[0, :N_EXPERTS].set(router_b[l].astype(F32))

        u_pool, aq, akv, gdn, ba = _in_projection(x2d, mod, w_in_all, l, seq)
        y_pool = _pool_mixer(u_pool, pool_bd.astype(BF16), pool_scale[l].astype(F32), seq)
        y_att = _swa_attention(aq, akv, bias, attn_sinks[l].astype(F32), seq)
        y_gdn = _gdn_mixer(gdn, ba, conv_p, alog_v, dtb_v, nw_v, seq)
        x1, h2, info, cnt = _out_projection(x2d, mod, y_pool, y_att, y_gdn, w_out_p, ln1_g[l], ln1_b[l],
                                            rw_hi, rw_lo, rb, seq)
        x2d = _moe(h2, info, cnt, x1, mod, ln2_g[l], ln2_b[l], w_up_all, b_up_all, w_down_all, b_down_all,
                   l, seq)
    return x2d.reshape(bsz, seq, d)
```

```python
import functools

import numpy as np
import jax
import jax.numpy as jnp
from jax import lax
from jax.experimental import pallas as pl
from jax.experimental.pallas import tpu as pltpu
from jax.experimental.pallas import tpu_sc as plsc

F32 = jnp.float32
BF16 = jnp.bfloat16

D_MODEL = 1024
HEAD_DIM = 64
POOL_DIM = 256
POOL_WINDOWS = (2, 4, 8, 16)
POOL_GROUP = 64
N_ATT_HEADS = 6
N_KV_HEADS = 2
ATT_DIM = 384
KV_DIM = 128
WINDOW = 128
N_BUCKETS = 32
MAX_DISTANCE = 128
N_GDN_HEADS = 6
GDN_DIM = 384
CONV_WIDTH = 4
GDN_CHUNK = 64
N_EXPERTS = 32
TOP_K = 4
EXPERT_DIM = 1024
SWIGLU_ALPHA = 1.702
SWIGLU_LIMIT = 7.0
DEPTH = 2
DEEPNORM_ALPHA = (2 * DEPTH) ** 0.25
LN_EPS = 1e-5
NORM_EPS = 1e-6
NEG_INF = -1e30

LANES = 128
SUBLANES = 8
VMEM_LIMIT = 56 * 1024 * 1024

ROW_TILE = 512
ATT_BLOCKS = 4
GDN_SUPER = 256
ROUTE_TILE = 2048
EXPERT_BLOCK = 512
SC_CHUNK = 64

_OFF_AQ = POOL_DIM
_OFF_AK = _OFF_AQ + ATT_DIM
_OFF_AV = _OFF_AK + KV_DIM
_OFF_GQ = _OFF_AV + KV_DIM
_OFF_GK = _OFF_GQ + GDN_DIM
_OFF_GV = _OFF_GK + GDN_DIM
_OFF_GZ = _OFF_GV + GDN_DIM
_OFF_GB = _OFF_GZ + GDN_DIM
_OFF_GA = _OFF_GB + N_GDN_HEADS
IN_DIM = _OFF_GA + N_GDN_HEADS

P_POOL = (0, POOL_DIM)
P_Q = (P_POOL[1], P_POOL[1] + ATT_DIM)
P_KV = (P_Q[1], P_Q[1] + 2 * KV_DIM)
P_GDN = (P_KV[1], P_KV[1] + 4 * GDN_DIM)
P_BA = (P_GDN[1], P_GDN[1] + LANES)
P_TOTAL = P_BA[1]


def _head_cols(off, h):
    return list(range(off + HEAD_DIM * h, off + HEAD_DIM * (h + 1)))


def _build_in_perm():
    cols = list(range(POOL_DIM))
    for p in range(N_ATT_HEADS // 2):
        cols += _head_cols(_OFF_AQ, p) + _head_cols(_OFF_AQ, p + 3)
    cols += list(range(_OFF_AK, _OFF_AK + 2 * KV_DIM))
    gdn_src = []
    for p in range(N_GDN_HEADS // 2):
        e, o = 2 * p, 2 * p + 1
        grp = (_head_cols(_OFF_GK, e) + _head_cols(_OFF_GQ, e)
               + _head_cols(_OFF_GQ, o) + _head_cols(_OFF_GK, o)
               + _head_cols(_OFF_GV, o) + _head_cols(_OFF_GV, e)
               + _head_cols(_OFF_GZ, o) + _head_cols(_OFF_GZ, e))
        cols += grp
        gdn_src += [c - _OFF_GQ if c < _OFF_GZ else -1 for c in grp]
    cols += list(range(_OFF_GB, _OFF_GB + 2 * N_GDN_HEADS))
    cols += [-1] * (LANES - 2 * N_GDN_HEADS)
    assert len(cols) == P_TOTAL
    return np.asarray(cols, np.int32), np.asarray(gdn_src, np.int32)


_IN_PERM, _GDN_CONV_SRC = _build_in_perm()


def _build_out_perm():
    rows = list(range(POOL_DIM))
    for p in range(N_ATT_HEADS // 2):
        rows += _head_cols(POOL_DIM, p) + _head_cols(POOL_DIM, p + 3)
    for p in range(N_GDN_HEADS // 2):
        rows += _head_cols(POOL_DIM + ATT_DIM, 2 * p + 1) + _head_cols(POOL_DIM + ATT_DIM, 2 * p)
    return np.asarray(rows, np.int32)


_OUT_PERM = _build_out_perm()


def _t5_bucket_line():
    n = np.maximum(2 * WINDOW - 1 - np.arange(3 * WINDOW - 1), 0)
    max_exact = N_BUCKETS // 2
    nf = np.maximum(n, 1).astype(np.float32)
    large = max_exact + (np.log(nf / max_exact) / np.float32(np.log(MAX_DISTANCE / max_exact))
                         * (N_BUCKETS - max_exact)).astype(np.int32)
    large = np.minimum(large, N_BUCKETS - 1)
    return np.where(n < max_exact, n, large).astype(np.int32)


_BUCKET_LINE = _t5_bucket_line()


def _band_bias(rel_bias):
    n_line = 3 * WINDOW - 1
    line = jnp.take(rel_bias.astype(F32), jnp.asarray(_BUCKET_LINE), axis=0).T
    heads = line.shape[0]
    padded = jnp.concatenate([line, jnp.zeros((heads, 1), F32)], axis=1)
    skew = jnp.tile(padded, (1, WINDOW))[:, :WINDOW * n_line].reshape(heads, WINDOW, n_line)
    return skew[:, :, WINDOW - 1:3 * WINDOW - 1]


def _take_static(w, perm, axis):
    parts = []
    start = 0
    for i in range(1, len(perm) + 1):
        run_ends = (i == len(perm) or ((perm[i] < 0) != (perm[i - 1] < 0))
                    or (perm[i] >= 0 and perm[i] != perm[i - 1] + 1))
        if run_ends:
            if perm[start] < 0:
                shape = list(w.shape)
                shape[axis] = i - start
                parts.append(jnp.zeros(shape, w.dtype))
            else:
                parts.append(lax.slice_in_dim(w, int(perm[start]), int(perm[start]) + (i - start), axis=axis))
            start = i
    return jnp.concatenate(parts, axis=axis)


def _take_cols(w, perm):
    return _take_static(w, perm, w.ndim - 1)


def _split_bf16(x):
    hi = x.astype(BF16)
    lo = (x - hi.astype(F32)).astype(BF16)
    return hi, lo


def _pack_bf16_pairs(x):
    n = x.shape[1] // 2
    bits = pltpu.bitcast(x.astype(BF16).astype(F32), jnp.int32)
    return lax.shift_right_logical(bits[:, :n], 16) | bits[:, n:]


def _unpack_bf16_pairs(u):
    lo = pltpu.bitcast(lax.shift_left(u, 16), F32)
    hi = pltpu.bitcast(u & jnp.int32(-65536), F32)
    return jnp.concatenate([lo, hi], axis=1)


def _dot(a, b):
    return jnp.dot(a, b, preferred_element_type=F32)


def _dot_nt(a, b):
    return lax.dot_general(a, b, (((1,), (1,)), ((), ())), preferred_element_type=F32)


def _dot_hi_exact_rhs(x, m_bf16):
    hi, lo = _split_bf16(x)
    return _dot(hi, m_bf16) + _dot(lo, m_bf16)


def _sigmoid(x):
    return 1.0 / (1.0 + jnp.exp(-x))


def _layer_norm(r, g, b):
    mu = jnp.mean(r, axis=-1, keepdims=True)
    d = r - mu
    var = jnp.mean(d * d, axis=-1, keepdims=True)
    return d * lax.rsqrt(var + LN_EPS) * g + b


def _cparams(sem):
    return pltpu.CompilerParams(dimension_semantics=sem, vmem_limit_bytes=VMEM_LIMIT)


def _mod_kernel(c_ref, w_ref, b_ref, o_ref):
    c = c_ref[...]
    ca = c * _sigmoid(c)
    ch, cl = _split_bf16(ca)
    wh, wl = _split_bf16(w_ref[0])
    o_ref[0] = _dot(ch, wh) + _dot(cl, wh) + _dot(ch, wl) + b_ref[0]


def _modulation(c, w_ada, b_ada):
    depth, d, n = w_ada.shape
    bsz = c.shape[0]
    tn = 512
    return pl.pallas_call(
        _mod_kernel,
        grid=(depth, n // tn),
        in_specs=[
            pl.BlockSpec((bsz, d), lambda l, j: (0, 0)),
            pl.BlockSpec((1, d, tn), lambda l, j: (l, 0, j)),
            pl.BlockSpec((1, 1, tn), lambda l, j: (l, 0, j)),
        ],
        out_specs=pl.BlockSpec((1, bsz, tn), lambda l, j: (l, 0, j)),
        out_shape=jax.ShapeDtypeStruct((depth, bsz, n), F32),
        compiler_params=_cparams(("arbitrary", "arbitrary")),
        name="adaln_mod",
    )(c, w_ada, b_ada.reshape(depth, 1, n))


def _inproj_kernel(x_ref, mod_ref, w_ref, pool_ref, q_ref, kv_ref, gdn_ref, ba_ref):
    sh = mod_ref[0, 0:1, :]
    sc = mod_ref[0, 1:2, :]
    h = (x_ref[...] * (1.0 + sc) + sh).astype(BF16)

    def mm(rng):
        return _dot(h, w_ref[0, :, rng[0]:rng[1]])

    pool_ref[...] = mm(P_POOL)
    q_ref[...] = mm(P_Q).astype(BF16)
    kv_ref[...] = mm(P_KV).astype(BF16)
    gdn_ref[...] = mm(P_GDN)
    ba_ref[...] = mm(P_BA)


def _in_projection(x2d, mod, w_in_all, layer, seq):
    t, d = x2d.shape
    tm = ROW_TILE
    widths = [r[1] - r[0] for r in (P_POOL, P_Q, P_KV, P_GDN, P_BA)]
    dtypes = [F32, BF16, BF16, F32, F32]
    return pl.pallas_call(
        _inproj_kernel,
        grid=(t // tm,),
        in_specs=[
            pl.BlockSpec((tm, d), lambda i: (i, 0)),
            pl.BlockSpec((1, 6, d), lambda i: ((i * tm) // seq, 0, 0)),
            pl.BlockSpec((1, d, P_TOTAL), lambda i: (layer, 0, 0)),
        ],
        out_specs=[pl.BlockSpec((tm, w), lambda i: (i, 0)) for w in widths],
        out_shape=[jax.ShapeDtypeStruct((t, w), dt) for w, dt in zip(widths, dtypes)],
        compiler_params=_cparams(("arbitrary",)),
        name="in_proj",
    )(x2d, mod, w_in_all)


def _pool_kernel(u_ref, w_ref, scale_ref, o_ref):
    u = u_ref[...]
    row = lax.broadcasted_iota(jnp.int32, u.shape, 0)
    lane = lax.broadcasted_iota(jnp.int32, u.shape, 1)

    def shifted(a, s):
        return jnp.where(row >= s, pltpu.roll(a, s, axis=0), 0.0)

    sums = []
    acc = u
    for wdt in POOL_WINDOWS:
        acc = acc + shifted(acc, wdt // 2)
        sums.append(acc)
    grp = lane // POOL_GROUP
    wsum = sums[-1]
    win = jnp.full(u.shape, POOL_WINDOWS[-1], jnp.int32)
    for gi in range(len(POOL_WINDOWS) - 2, -1, -1):
        wsum = jnp.where(grp == gi, sums[gi], wsum)
        win = jnp.where(grp == gi, POOL_WINDOWS[gi], win)
    cnt = jnp.minimum(row + 1, win).astype(F32)
    p = wsum / cnt - u
    y = _dot(p.astype(BF16), w_ref[...]) * scale_ref[...]
    o_ref[...] = y.astype(BF16)


def _pool_mixer(u, pool_w_bd, pool_scale, seq):
    t, c = u.shape
    return pl.pallas_call(
        _pool_kernel,
        grid=(t // seq,),
        in_specs=[
            pl.BlockSpec((seq, c), lambda b: (b, 0)),
            pl.BlockSpec((c, c), lambda b: (0, 0)),
            pl.BlockSpec((1, c), lambda b: (0, 0)),
        ],
        out_specs=pl.BlockSpec((seq, c), lambda b: (b, 0)),
        out_shape=jax.ShapeDtypeStruct((t, c), BF16),
        compiler_params=_cparams(("arbitrary",)),
        name="pool_mixer",
    )(u, pool_w_bd, pool_scale.reshape(1, c))


def _attn_kernel(sink_ref, q_ref, kvc_ref, kvp_ref, bias_ref, o_ref):
    step = pl.program_id(1)
    qi = lax.broadcasted_iota(jnp.int32, (WINDOW, 2 * WINDOW), 0)
    kj = lax.broadcasted_iota(jnp.int32, (WINDOW, 2 * WINDOW), 1)
    dist = qi + WINDOW - kj
    in_band = (dist >= 0) & (dist < WINDOW)
    lo = lax.broadcasted_iota(jnp.int32, (WINDOW, LANES), 1) < HEAD_DIM
    for sub in range(ATT_BLOCKS):
        r0 = sub * WINDOW
        prev = kvp_ref[...] if sub == 0 else kvc_ref[r0 - WINDOW:r0, :]
        kv = jnp.concatenate([prev, kvc_ref[r0:r0 + WINDOW, :]], axis=0)
        k = kv[:, :KV_DIM]
        v = kv[:, KV_DIM:]
        valid = in_band & ((kj >= WINDOW) | (step > 0)) if sub == 0 else in_band
        for p in range(N_ATT_HEADS // 2):
            qp = q_ref[r0:r0 + WINDOW, p * LANES:(p + 1) * LANES]
            halves = []
            for half in range(2):
                h = p + 3 * half
                qm = jnp.where(lo if half == 0 else jnp.logical_not(lo), qp, jnp.zeros_like(qp))
                s = _dot_nt(qm, k) * (HEAD_DIM ** -0.5)
                s = jnp.where(valid, s + bias_ref[h], NEG_INF)
                sink = sink_ref[h]
                m = jnp.maximum(jnp.max(s, axis=-1, keepdims=True), sink)
                pr = jnp.exp(s - m)
                den = jnp.sum(pr, axis=-1, keepdims=True) + jnp.exp(sink - m)
                halves.append(_dot(pr.astype(BF16), v) / den)
            o_ref[r0:r0 + WINDOW, p * LANES:(p + 1) * LANES] = (
                jnp.where(lo, halves[0], halves[1]).astype(BF16))


def _swa_attention(q, kv, bias, sinks, seq):
    t = q.shape[0]
    rows = ATT_BLOCKS * WINDOW
    nblk = seq // rows
    return pl.pallas_call(
        _attn_kernel,
        grid=(t // seq, nblk),
        in_specs=[
            pl.BlockSpec(memory_space=pltpu.SMEM),
            pl.BlockSpec((rows, ATT_DIM), lambda b, n: (b * nblk + n, 0)),
            pl.BlockSpec((rows, 2 * KV_DIM), lambda b, n: (b * nblk + n, 0)),
            pl.BlockSpec((WINDOW, 2 * KV_DIM),
                         lambda b, n: (jnp.maximum((b * nblk + n) * ATT_BLOCKS - 1, 0), 0)),
            pl.BlockSpec((N_ATT_HEADS, WINDOW, 2 * WINDOW), lambda b, n: (0, 0, 0)),
        ],
        out_specs=pl.BlockSpec((rows, ATT_DIM), lambda b, n: (b * nblk + n, 0)),
        out_shape=jax.ShapeDtypeStruct((t, ATT_DIM), BF16),
        compiler_params=_cparams(("arbitrary", "arbitrary")),
        name="swa_attention",
    )(sinks, q, kv, kv, bias)


_GDN_BASE = SUBLANES
_GDN_LEVELS = int(np.log2(GDN_CHUNK // _GDN_BASE))


def _gdn_masks():
    r = np.arange(GDN_SUPER)
    ri, ci = r[:, None], r[None, :]
    same_chunk = (ri // GDN_CHUNK) == (ci // GDN_CHUNK)
    incl = same_chunk & (ri >= ci)
    planes = [incl, ri == ci]
    base = ((ri // _GDN_BASE) == (ci // _GDN_BASE)) & (ri > ci)
    planes.append(base)
    for lvl in range(_GDN_LEVELS):
        small = _GDN_BASE << lvl
        planes.append(((ri // (2 * small)) == (ci // (2 * small))) & ((ri // small) != (ci // small)) & (ri > ci))
    bmask = np.stack(planes).astype(np.float32)
    bmask[2] = -bmask[2]
    negmask = np.where(incl, 0.0, -np.inf).astype(np.float32)
    return negmask, bmask


_GDN_NEGMASK, _GDN_BMASK = _gdn_masks()


def _gdn_kernel(x_ref, halo_ref, ba_ref, cw_ref, alog_ref, dtb_ref, nw_ref, negmask_ref, bmask_ref,
                y_ref, state_ref, xs_ref):
    sc_id = pl.program_id(1)
    rows = GDN_SUPER
    nchunk = rows // GDN_CHUNK
    c_sz = GDN_CHUNK

    @pl.when(sc_id == 0)
    def _():
        state_ref[...] = jnp.zeros_like(state_ref)

    xs_ref[:SUBLANES, :] = jnp.where(sc_id == 0, 0.0, halo_ref[...])
    xs_ref[SUBLANES:, :] = x_ref[...]
    act = []
    for g in range(x_ref.shape[1] // LANES):
        cols = slice(g * LANES, (g + 1) * LANES)
        acc = x_ref[:, cols] * cw_ref[CONV_WIDTH - 1:CONV_WIDTH, cols]
        if g % 4 != 3:
            for s in range(1, CONV_WIDTH):
                acc = acc + (xs_ref[SUBLANES - s:SUBLANES - s + rows, cols]
                             * cw_ref[CONV_WIDTH - 1 - s:CONV_WIDTH - s, cols])
        act.append(acc * _sigmoid(acc))

    negmask = negmask_ref[...]
    tri_incl = bmask_ref[0]
    eye_b = bmask_ref[1]
    base_neg = bmask_ref[2]
    bands = [bmask_ref[3 + lvl] for lvl in range(_GDN_LEVELS)]
    li = lax.broadcasted_iota(jnp.int32, (LANES, LANES), 0)
    lj = lax.broadcasted_iota(jnp.int32, (LANES, LANES), 1)
    half_ones = jnp.where((li // HEAD_DIM) == (lj // HEAD_DIM), 1.0, 0.0).astype(BF16)
    lane_lo = lax.broadcasted_iota(jnp.int32, (rows, LANES), 1) < HEAD_DIM
    lane_lo_c = lax.broadcasted_iota(jnp.int32, (c_sz, LANES), 1) < HEAD_DIM

    ba = ba_ref[...]
    beta_all = _sigmoid(ba)
    sp_in = ba + dtb_ref[...]
    softplus = jnp.maximum(sp_in, 0.0) + jnp.log(1.0 + jnp.exp(-jnp.abs(sp_in)))
    g_all = -jnp.exp(alog_ref[...]) * softplus
    gcum = _dot_hi_exact_rhs_lhs(tri_incl, g_all)
    gcum_t = gcum.T

    heads = range(N_GDN_HEADS)
    lane_hi = jnp.logical_not(lane_lo)
    lane_hi_c = jnp.logical_not(lane_lo_c)
    mk = [lane_lo if h % 2 == 0 else lane_hi for h in heads]
    mk_c = [lane_lo_c if h % 2 == 0 else lane_hi_c for h in heads]
    scale = HEAD_DIM ** -0.5

    def bdot(a, b):
        return _dot(a.astype(BF16), b.astype(BF16))

    xk, xq, gn, gc_col, beta, eg = [], [], [], [], [], []
    for h in heads:
        g = act[4 * (h // 2) + (h % 2)]
        g = g * lax.rsqrt(_dot_hi_exact_rhs(g * g, half_ones) + NORM_EPS)
        gn.append(g)
        xk.append(jnp.where(mk[h], g, 0.0))
        xq.append(jnp.where(mk[h], pltpu.roll(g, HEAD_DIM, axis=1), 0.0) * scale)
        beta.append(beta_all[:, h:h + 1])
        gc_col.append(gcum[:, N_GDN_HEADS + h:N_GDN_HEADS + h + 1])
        eg.append(jnp.exp(gc_col[h]))

    l_b, attn, rhs = [], [], []
    for h in heads:
        gc_row = gcum_t[N_GDN_HEADS + h:N_GDN_HEADS + h + 1, :]
        decay = jnp.exp(gc_col[h] - gc_row + negmask)
        xk_b = xk[h].astype(BF16)
        kk = _dot_nt((xk[h] * beta[h]).astype(BF16), xk_b)
        l_b.append((kk * decay).astype(BF16))
        attn.append((_dot_nt(xq[h].astype(BF16), xk_b) * decay).astype(BF16))
        vv = act[4 * (h // 2) + 2]
        rhs.append(jnp.where(mk[h], gn[h] * eg[h], vv) * beta[h])

    a1 = [l_b[h] * base_neg for h in heads]
    a2 = [_dot(a1[h], a1[h]).astype(BF16) for h in heads]
    a4 = [_dot(a2[h], a2[h]).astype(BF16) for h in heads]
    inv0 = [eye_b + a1[h] for h in heads]
    acc1 = [inv0[h].astype(F32) + _dot(a2[h], inv0[h]) for h in heads]
    inv_b = [(acc1[h] + _dot(a4[h], acc1[h].astype(BF16))).astype(BF16) for h in heads]
    for lvl in range(_GDN_LEVELS - 1):
        mid = [_dot(l_b[h] * bands[lvl], inv_b[h]).astype(BF16) for h in heads]
        inv_b = [inv_b[h] - _dot(inv_b[h], mid[h]).astype(BF16) for h in heads]
    half = [_dot(inv_b[h], rhs[h].astype(BF16)) for h in heads]
    mid = [_dot(l_b[h] * bands[_GDN_LEVELS - 1], half[h].astype(BF16)) for h in heads]
    sol = [half[h] - _dot(inv_b[h], mid[h].astype(BF16)) for h in heads]

    lane_lo_s = lax.broadcasted_iota(jnp.int32, (LANES, LANES), 1) < HEAD_DIM
    mk_s = [lane_lo_s if h % 2 == 0 else jnp.logical_not(lane_lo_s) for h in heads]
    sol_b = [sol[h].astype(BF16) for h in heads]
    attn_sol = [_dot(attn[h], sol_b[h]) for h in heads]
    q_eff = [(xq[h] * eg[h] - jnp.where(mk[h], attn_sol[h], 0.0)).astype(BF16) for h in heads]
    o_free = [jnp.where(mk[h], 0.0, attn_sol[h]) for h in heads]
    kw = [[] for _ in heads]
    ku = [[] for _ in heads]
    cdec = [[] for _ in heads]
    for c in range(nchunk):
        r0 = c * c_sz
        for h in heads:
            glast = gcum[r0 + c_sz - 1:r0 + c_sz, N_GDN_HEADS + h:N_GDN_HEADS + h + 1]
            kd_t = (xk[h][r0:r0 + c_sz] * jnp.exp(glast - gc_col[h][r0:r0 + c_sz])).T
            both = _dot(kd_t.astype(BF16), sol_b[h][r0:r0 + c_sz])
            kw[h].append(jnp.where(mk_s[h], both, 0.0).astype(BF16))
            ku[h].append(jnp.where(mk_s[h], 0.0, both))
            cdec[h].append(jnp.exp(glast))
    st = [state_ref[h] for h in heads]
    o_parts = [[] for _ in heads]
    for c in range(nchunk):
        r0 = c * c_sz
        for h in heads:
            lhs = jnp.concatenate([kw[h][c], q_eff[h][r0:r0 + c_sz]], axis=0)
            prod = _dot(lhs, st[h].astype(BF16))
            o_parts[h].append(prod[LANES:] + o_free[h][r0:r0 + c_sz])
            st[h] = st[h] * cdec[h][c] + ku[h][c] - prod[:LANES]
    for h in heads:
        state_ref[h] = st[h]

    for p in range(N_GDN_HEADS // 2):
        o_pair = [jnp.concatenate(o_parts[h], axis=0) for h in (2 * p, 2 * p + 1)]
        o = jnp.where(lane_lo, o_pair[1], o_pair[0])
        ms = _dot_hi_exact_rhs(o * o, half_ones) * (1.0 / HEAD_DIM)
        zz = act[4 * p + 3]
        y = o * lax.rsqrt(ms + NORM_EPS) * nw_ref[...] * zz
        y_ref[:, p * LANES:(p + 1) * LANES] = y.astype(BF16)


def _dot_hi_exact_rhs_lhs(m_bf16, x):
    hi, lo = _split_bf16(x)
    return _dot(m_bf16, hi) + _dot(m_bf16, lo)


def _gdn_mixer(gdn, ba, conv_p, alog_v, dtb_v, nw_v, seq):
    t, c = gdn.shape
    rows = GDN_SUPER
    nsc = seq // rows
    hb = rows // SUBLANES
    return pl.pallas_call(
        _gdn_kernel,
        grid=(t // seq, nsc),
        in_specs=[
            pl.BlockSpec((rows, c), lambda b, s: (b * nsc + s, 0)),
            pl.BlockSpec((SUBLANES, c), lambda b, s: (jnp.maximum((b * nsc + s) * hb - 1, 0), 0)),
            pl.BlockSpec((rows, LANES), lambda b, s: (b * nsc + s, 0)),
            pl.BlockSpec((CONV_WIDTH, c), lambda b, s: (0, 0)),
            pl.BlockSpec((1, LANES), lambda b, s: (0, 0)),
            pl.BlockSpec((1, LANES), lambda b, s: (0, 0)),
            pl.BlockSpec((1, LANES), lambda b, s: (0, 0)),
            pl.BlockSpec((rows, rows), lambda b, s: (0, 0)),
            pl.BlockSpec((3 + _GDN_LEVELS, rows, rows), lambda b, s: (0, 0, 0)),
        ],
        out_specs=pl.BlockSpec((rows, GDN_DIM), lambda b, s: (b * nsc + s, 0)),
        out_shape=jax.ShapeDtypeStruct((t, GDN_DIM), BF16),
        scratch_shapes=[pltpu.VMEM((N_GDN_HEADS, LANES, LANES), F32), pltpu.VMEM((rows + SUBLANES, c), F32)],
        compiler_params=_cparams(("arbitrary", "arbitrary")),
        name="gdn_mixer",
    )(gdn, gdn, ba, conv_p, alog_v, dtb_v, nw_v, jnp.asarray(_GDN_NEGMASK), jnp.asarray(_GDN_BMASK, BF16))


def _route_tile(logits, before, carry_ref, live):
    shape = logits.shape
    lane = lax.broadcasted_iota(jnp.int32, shape, 1).astype(F32)
    work = logits
    vals, idxs = [], []
    for _k in range(TOP_K):
        m = jnp.max(work, axis=-1, keepdims=True)
        idx = jnp.min(jnp.where(work == m, lane, float(LANES)), axis=-1, keepdims=True)
        vals.append(m)
        idxs.append(idx)
        work = jnp.where(lane == idx, -jnp.inf, work)
    exps = [jnp.exp(v - vals[0]) for v in vals]
    den = exps[0] + exps[1] + exps[2] + exps[3]
    onehots = [lane == idx for idx in idxs]
    member = jnp.zeros(shape, F32)
    for oh in onehots:
        member = member + jnp.where(oh, 1.0, 0.0)
    rank = _dot(before, member.astype(BF16)) + carry_ref[...]
    carry_ref[...] = carry_ref[...] + live * jnp.sum(member, axis=0, keepdims=True)
    info = jnp.zeros(shape, F32)
    for k in range(TOP_K):
        rank_k = jnp.sum(jnp.where(onehots[k], rank, 0.0), axis=-1, keepdims=True)
        info = jnp.where(lane == float(k), idxs[k], info)
        info = jnp.where(lane == float(TOP_K + k), rank_k, info)
        info = jnp.where(lane == float(2 * TOP_K + k), exps[k] / den, info)
    return info


def _outproj_kernel(x_ref, mod_ref, yp_ref, ya_ref, yg_ref, wp_ref, wa_ref, wg_ref, lng_ref, lnb_ref,
                    rwh_ref, rwl_ref, rb_ref, before_ref, x1_ref, h2_ref, info_ref, cnt_ref,
                    carry_ref, logit_s):
    step = pl.program_id(0)

    @pl.when(step == 0)
    def _():
        carry_ref[...] = jnp.zeros_like(carry_ref)
        logit_s[...] = jnp.zeros_like(logit_s)

    y = _dot(yp_ref[...], wp_ref[...]) + _dot(ya_ref[...], wa_ref[...]) + _dot(yg_ref[...], wg_ref[...])
    live = jnp.where(step > 0, 1.0, 0.0)
    info_ref[...] = _route_tile(logit_s[...], before_ref[...], carry_ref, live)
    cnt_ref[...] = carry_ref[...]
    g1 = mod_ref[0, 2:3, :]
    sh2 = mod_ref[0, 3:4, :]
    sc2 = mod_ref[0, 4:5, :]
    x1 = _layer_norm(DEEPNORM_ALPHA * x_ref[...] + g1 * y, lng_ref[...], lnb_ref[...])
    x1_ref[...] = x1
    h2 = x1 * (1.0 + sc2) + sh2
    hh, hl = _split_bf16(h2)
    h2_ref[...] = _pack_bf16_pairs(h2)
    logit_s[...] = _dot(hh, rwh_ref[...]) + _dot(hl, rwh_ref[...]) + _dot(hh, rwl_ref[...]) + rb_ref[...]


def _out_projection(x2d, mod, yp, ya, yg, w_out_p, ln_g, ln_b, rw_hi, rw_lo, rb, seq):
    t, d = x2d.shape
    tm = ROW_TILE
    wp = w_out_p[:POOL_DIM]
    wa = w_out_p[POOL_DIM:POOL_DIM + ATT_DIM]
    wg = w_out_p[POOL_DIM + ATT_DIM:]
    last = t // tm - 1
    row = lambda i: (jnp.minimum(i, last), 0)
    routed = lambda i: (jnp.maximum(i - 1, 0), 0)
    fixed = lambda i: (0, 0)
    return pl.pallas_call(
        _outproj_kernel,
        grid=(t // tm + 1,),
        in_specs=[
            pl.BlockSpec((tm, d), row),
            pl.BlockSpec((1, 6, d), lambda i: ((jnp.minimum(i, last) * tm) // seq, 0, 0)),
            pl.BlockSpec((tm, POOL_DIM), row),
            pl.BlockSpec((tm, ATT_DIM), row),
            pl.BlockSpec((tm, GDN_DIM), row),
            pl.BlockSpec((POOL_DIM, d), fixed),
            pl.BlockSpec((ATT_DIM, d), fixed),
            pl.BlockSpec((GDN_DIM, d), fixed),
            pl.BlockSpec((1, d), fixed),
            pl.BlockSpec((1, d), fixed),
            pl.BlockSpec((d, LANES), fixed),
            pl.BlockSpec((d, LANES), fixed),
            pl.BlockSpec((1, LANES), fixed),
            pl.BlockSpec((tm, tm), fixed),
        ],
        out_specs=[pl.BlockSpec((tm, d), row), pl.BlockSpec((tm, d // 2), row), pl.BlockSpec((tm, LANES), routed),
                   pl.BlockSpec((1, LANES), fixed)],
        out_shape=[jax.ShapeDtypeStruct((t, d), F32), jax.ShapeDtypeStruct((t, d // 2), jnp.int32),
                   jax.ShapeDtypeStruct((t, LANES), F32), jax.ShapeDtypeStruct((1, LANES), F32)],
        scratch_shapes=[pltpu.VMEM((1, LANES), F32), pltpu.VMEM((tm, LANES), F32)],
        compiler_params=_cparams(("arbitrary",)),
        name="out_proj_ln_route",
    )(x2d, mod, yp, ya, yg, wp, wa, wg, ln_g.reshape(1, d), ln_b.reshape(1, d), rw_hi, rw_lo, rb,
      jnp.tril(jnp.ones((tm, tm), BF16), -1))


def _slot_kernel(info_ref, cnt_ref, dest_ref, pcum_ref, pstart_ref):
    shape = info_ref.shape
    lane = lax.broadcasted_iota(jnp.int32, shape, 1).astype(F32)

    @pl.when(pl.program_id(0) == 0)
    def _():
        cnt = jnp.broadcast_to(cnt_ref[...], (SUBLANES, LANES))
        padded = jnp.floor((cnt + float(EXPERT_BLOCK - 1)) * (1.0 / EXPERT_BLOCK)) * float(EXPERT_BLOCK)
        lane8 = lax.broadcasted_iota(jnp.int32, (SUBLANES, LANES), 1)
        acc = padded
        step = 1
        while step < LANES:
            acc = acc + jnp.where(lane8 >= step, pltpu.roll(acc, step, axis=1), 0.0)
            step *= 2
        pstart_ref[...] = (acc - padded)[:1]
        pcum_ref[...] = acc[:1].astype(jnp.int32)

    info = info_ref[...]
    slots = jnp.zeros(shape, F32)
    for k in range(TOP_K):
        onehot = lane == info[:, k:k + 1]
        start = jnp.sum(jnp.where(onehot, pstart_ref[...], 0.0), axis=-1, keepdims=True)
        slots = jnp.where(lane == float(k), start + info[:, TOP_K + k:TOP_K + k + 1], slots)
    dest_ref[...] = slots.T[:SUBLANES].astype(jnp.int32)


def _slots(info, cnt):
    t = info.shape[0]
    tm = min(ROUTE_TILE, t)
    assert t % tm == 0
    return pl.pallas_call(
        _slot_kernel,
        grid=(t // tm,),
        in_specs=[pl.BlockSpec((tm, LANES), lambda i: (i, 0)), pl.BlockSpec((1, LANES), lambda i: (0, 0))],
        out_specs=[pl.BlockSpec((SUBLANES, tm), lambda i: (0, i)), pl.BlockSpec((1, LANES), lambda i: (0, 0))],
        out_shape=[jax.ShapeDtypeStruct((SUBLANES, t), jnp.int32), jax.ShapeDtypeStruct((1, LANES), jnp.int32)],
        scratch_shapes=[pltpu.VMEM((1, LANES), F32)],
        compiler_params=_cparams(("arbitrary",)),
        name="moe_slots",
    )(info, cnt)


def _expert_kernel(e0, be_ref, nxt_ref, val_ref, nu_ref, x_ref, wup_hbm, bup_ref, wdn_hbm, bdn_ref, y_ref,
                   wup_st, wdn_st, wup_bf, wdn_bf, sems):
    i = pl.program_id(0)
    e = be_ref[i]
    prev = be_ref[jnp.maximum(i - 1, 0)]
    used = i < nu_ref[0]

    def weight_copies(expert):
        return (pltpu.make_async_copy(wup_hbm.at[e0 + expert], wup_st, sems.at[0]),
                pltpu.make_async_copy(wdn_hbm.at[e0 + expert], wdn_st, sems.at[1]))

    @pl.when(i == 0)
    def _():
        for cp in weight_copies(e):
            cp.start()

    @pl.when(used & ((i == 0) | (e != prev)))
    def _():
        for cp in weight_copies(e):
            cp.wait()
        wup_bf[...] = wup_st[...].astype(BF16)
        wdn_bf[...] = wdn_st[...].astype(BF16)

        @pl.when(nxt_ref[i] >= 0)
        def _():
            for cp in weight_copies(nxt_ref[i]):
                cp.start()

    def ffn(rows):
        xb = _unpack_bf16_pairs(x_ref[:rows, :]).astype(BF16)
        hb = _dot(xb, wup_bf[...]) + bup_ref[0]
        x_glu = jnp.minimum(hb[:, :EXPERT_DIM], SWIGLU_LIMIT)
        x_lin = jnp.clip(hb[:, EXPERT_DIM:], -SWIGLU_LIMIT, SWIGLU_LIMIT)
        act = x_glu * _sigmoid(SWIGLU_ALPHA * x_glu) * (x_lin + 1.0)
        y = _dot(act.astype(BF16), wdn_bf[...]) + bdn_ref[0]
        y_ref[:rows, :] = _pack_bf16_pairs(y)

    half_rows = x_ref.shape[0] // 2
    small = val_ref[i] <= half_rows

    @pl.when(used & jnp.logical_not(small))
    def _():
        ffn(x_ref.shape[0])

    @pl.when(used & small)
    def _():
        ffn(half_rows)
        y_ref[half_rows:, :] = jnp.zeros((x_ref.shape[0] - half_rows, y_ref.shape[1]), y_ref.dtype)

    @pl.when(i >= nu_ref[0])
    def _():
        y_ref[...] = jnp.zeros_like(y_ref)


def _expert_ffn(xbuf, block_e, next_e, valid, n_used, w_up, b_up, w_down, b_down, layer):
    p, dh = xbuf.shape
    d = 2 * dh
    bm = EXPERT_BLOCK
    ne, _, n_up = w_up.shape
    e0 = layer * N_EXPERTS
    grid_spec = pltpu.PrefetchScalarGridSpec(
        num_scalar_prefetch=4,
        grid=(p // bm,),
        in_specs=[
            pl.BlockSpec((bm, dh), lambda i, be, nx, vl, nu: (i, 0)),
            pl.BlockSpec(memory_space=pl.ANY),
            pl.BlockSpec((1, 1, n_up), lambda i, be, nx, vl, nu: (e0 + be[i], 0, 0)),
            pl.BlockSpec(memory_space=pl.ANY),
            pl.BlockSpec((1, 1, d), lambda i, be, nx, vl, nu: (e0 + be[i], 0, 0)),
        ],
        out_specs=pl.BlockSpec((bm, dh), lambda i, be, nx, vl, nu: (i, 0)),
        scratch_shapes=[pltpu.VMEM((d, n_up), F32), pltpu.VMEM((EXPERT_DIM, d), F32),
                        pltpu.VMEM((d, n_up), BF16), pltpu.VMEM((EXPERT_DIM, d), BF16),
                        pltpu.SemaphoreType.DMA((2,))],
    )
    return pl.pallas_call(
        functools.partial(_expert_kernel, e0),
        grid_spec=grid_spec,
        out_shape=jax.ShapeDtypeStruct((p, dh), jnp.int32),
        compiler_params=_cparams(("arbitrary",)),
        name="expert_ffn",
    )(block_e, next_e, valid, n_used, xbuf, w_up, b_up, w_down, b_down)


def _combine_kernel(x1_ref, mod_ref, yg_ref, info_ref, lng_ref, lnb_ref, o_ref):
    info = info_ref[...]
    y = jnp.zeros(x1_ref.shape, F32)
    for k in range(TOP_K):
        gate = info[:, 2 * TOP_K + k:2 * TOP_K + k + 1]
        y = y + gate * _unpack_bf16_pairs(yg_ref[k])
    g2 = mod_ref[0, 5:6, :]
    o_ref[...] = _layer_norm(DEEPNORM_ALPHA * x1_ref[...] + g2 * y, lng_ref[...], lnb_ref[...])


def _combine(x1, mod, yg, info, ln_g, ln_b, seq):
    t, d = x1.shape
    tm = ROW_TILE
    row = lambda i: (i, 0)
    fixed = lambda i: (0, 0)
    return pl.pallas_call(
        _combine_kernel,
        grid=(t // tm,),
        in_specs=[
            pl.BlockSpec((tm, d), row),
            pl.BlockSpec((1, 6, d), lambda i: ((i * tm) // seq, 0, 0)),
            pl.BlockSpec((TOP_K, tm, d // 2), lambda i: (0, i, 0)),
            pl.BlockSpec((tm, LANES), row),
            pl.BlockSpec((1, d), fixed),
            pl.BlockSpec((1, d), fixed),
        ],
        out_specs=pl.BlockSpec((tm, d), row),
        out_shape=jax.ShapeDtypeStruct((t, d), F32),
        compiler_params=_cparams(("arbitrary",)),
        name="moe_combine_ln",
    )(x1, mod, yg, info, ln_g.reshape(1, d), ln_b.reshape(1, d))


def _sc_workers():
    info = plsc.get_sparse_core_info()
    return info.num_cores, info.num_cores * info.num_subcores


def _sc_scatter_rows(rows, idx, n_out):
    t, w = rows.shape
    kk = idx.shape[0]
    n_cores, n_workers = _sc_workers()
    ch = SC_CHUNK
    assert t % (2 * n_workers * ch) == 0
    n_chunk = t // (n_workers * ch)
    idx_c = jnp.transpose(idx.reshape(kk, t // ch, ch), (1, 0, 2))

    @functools.partial(
        pl.kernel,
        mesh=plsc.VectorSubcoreMesh(core_axis_name="c", subcore_axis_name="s"),
        out_type=jax.ShapeDtypeStruct((n_out, w), rows.dtype),
        scratch_types=[pltpu.VMEM((2, kk, ch), jnp.int32), pltpu.VMEM((2, ch, w), rows.dtype),
                       pltpu.SemaphoreType.DMA((2,)), pltpu.SemaphoreType.DMA((2,))],
        name="sc_dispatch_scatter",
    )
    def scatter_kernel(rows_hbm, idx_hbm, out_hbm, idx_v, rows_v, load_sem, scat_sem):
        base = (lax.axis_index("s") * n_cores + lax.axis_index("c")) * n_chunk

        def load(j, b):
            return pltpu.make_async_copy(rows_hbm.at[pl.ds((base + j) * ch, ch)], rows_v.at[b], load_sem.at[b])

        def scatters(b):
            return [pltpu.make_async_copy(rows_v.at[b], out_hbm.at[idx_v.at[b, q]], scat_sem.at[b])
                    for q in range(kk)]

        pltpu.sync_copy(idx_hbm.at[base], idx_v.at[0])
        load(0, 0).start()

        @pl.loop(0, n_chunk, step=2)
        def _(j0):
            for b in range(2):
                j = j0 + b
                other = 1 - b

                @pl.when(j >= 1)
                def _():
                    for cp in scatters(other):
                        cp.wait()

                @pl.when(j + 1 < n_chunk)
                def _():
                    pltpu.sync_copy(idx_hbm.at[base + j + 1], idx_v.at[other])
                    load(j + 1, other).start()

                load(j, b).wait()
                for cp in scatters(b):
                    cp.start()

        for cp in scatters((n_chunk - 1) % 2):
            cp.wait()

    return scatter_kernel(rows, idx_c)


def _sc_gather_rows(table, idx):
    m = idx.shape[0]
    w = table.shape[1]
    n_cores, n_workers = _sc_workers()
    ch = SC_CHUNK
    assert m % (2 * n_workers * ch) == 0
    n_chunk = m // (n_workers * ch)
    idx_c = idx.reshape(m // ch, 1, ch)

    @functools.partial(
        pl.kernel,
        mesh=plsc.VectorSubcoreMesh(core_axis_name="c", subcore_axis_name="s"),
        out_type=jax.ShapeDtypeStruct((m, w), table.dtype),
        scratch_types=[pltpu.VMEM((2, 1, ch), jnp.int32), pltpu.VMEM((2, ch, w), table.dtype),
                       pltpu.SemaphoreType.DMA((2,)), pltpu.SemaphoreType.DMA((2,))],
        name="sc_combine_gather",
    )
    def gather_kernel(table_hbm, idx_hbm, out_hbm, idx_v, rows_v, gather_sem, write_sem):
        base = (lax.axis_index("s") * n_cores + lax.axis_index("c")) * n_chunk

        def gather(b):
            return pltpu.make_async_copy(table_hbm.at[idx_v.at[b, 0]], rows_v.at[b], gather_sem.at[b])

        def write(j, b):
            return pltpu.make_async_copy(rows_v.at[b], out_hbm.at[pl.ds((base + j) * ch, ch)], write_sem.at[b])

        pltpu.sync_copy(idx_hbm.at[base], idx_v.at[0])
        gather(0).start()

        @pl.loop(0, n_chunk, step=2)
        def _(j0):
            for b in range(2):
                j = j0 + b
                other = 1 - b

                @pl.when(j >= 1)
                def _():
                    write(j - 1, other).wait()

                @pl.when(j + 1 < n_chunk)
                def _():
                    pltpu.sync_copy(idx_hbm.at[base + j + 1], idx_v.at[other])
                    gather(other).start()

                gather(b).wait()
                write(j, b).start()

        write(n_chunk - 1, (n_chunk - 1) % 2).wait()

    return gather_kernel(table, idx_c)


def _lane_vector(vals, offset):
    return jnp.zeros((1, LANES), F32).at[0, offset:offset + vals.shape[0]].set(vals.astype(F32))


def _moe(h2, info, cnt, x1, mod, ln_g, ln_b, w_up, b_up, w_down, b_down, layer, seq):
    t, dh = h2.shape
    a = t * TOP_K
    bm = EXPERT_BLOCK
    slots, pcum_v = _slots(info, cnt)
    pcum = pcum_v[0, :N_EXPERTS]
    dest = slots[:TOP_K]
    n_blocks = -(-a // bm) + N_EXPERTS
    starts = jnp.arange(n_blocks, dtype=jnp.int32) * bm
    block_e = jnp.minimum(jnp.sum(pcum[None, :] <= starts[:, None], axis=1), N_EXPERTS - 1).astype(jnp.int32)
    n_used = (pcum[-1] // bm).astype(jnp.int32).reshape(1)
    later = block_e[None, :] > block_e[:, None]
    group_end = n_blocks - jnp.sum(later, axis=1)
    next_e = jnp.min(jnp.where(later, block_e[None, :], N_EXPERTS), axis=1)
    next_e = jnp.where(group_end < n_used[0], next_e, -1).astype(jnp.int32)
    counts = cnt[0, :N_EXPERTS].astype(jnp.int32)
    pstart = pcum - ((counts + bm - 1) // bm) * bm
    mine = block_e[:, None] == jnp.arange(N_EXPERTS, dtype=jnp.int32)[None, :]
    count_b = jnp.sum(jnp.where(mine, counts[None, :], 0), axis=1)
    pstart_b = jnp.sum(jnp.where(mine, pstart[None, :], 0), axis=1)
    valid = jnp.clip(count_b - (starts - pstart_b), 0, bm).astype(jnp.int32)
    xbuf = _sc_scatter_rows(h2, dest, n_blocks * bm)
    ybuf = _expert_ffn(xbuf, block_e, next_e, valid, n_used, w_up, b_up, w_down, b_down, layer)
    yg = _sc_gather_rows(ybuf, dest.reshape(a)).reshape(TOP_K, t, dh)
    return _combine(x1, mod, yg, info, ln_g, ln_b, seq)


def kernel(x, c, rel_bias, w_in, w_out, w_ada, b_ada, ln1_g, ln1_b, ln2_g, ln2_b, pool_w, pool_scale,
           attn_sinks, conv_w, gdn_a_log, gdn_dt_bias, gdn_norm_w, router_w, router_b,
           exp_w_up, exp_b_up, exp_w_down, exp_b_down):
    bsz, seq, d = x.shape
    depth = w_in.shape[0]
    t = bsz * seq
    assert d == D_MODEL and w_in.shape[2] == IN_DIM
    assert seq % GDN_SUPER == 0 and seq % (ATT_BLOCKS * WINDOW) == 0
    assert t % ROW_TILE == 0 and seq % ROW_TILE == 0

    mod_all = _modulation(c, w_ada, b_ada).reshape(depth, bsz, 6, d)
    bias = _band_bias(rel_bias)

    w_up_all = exp_w_up.reshape((depth * N_EXPERTS,) + exp_w_up.shape[2:])
    b_up_all = exp_b_up.reshape(depth * N_EXPERTS, 1, exp_b_up.shape[2])
    w_down_all = exp_w_down.reshape((depth * N_EXPERTS,) + exp_w_down.shape[2:])
    b_down_all = exp_b_down.reshape(depth * N_EXPERTS, 1, exp_b_down.shape[2])

    w_in_all = _take_cols(w_in, _IN_PERM).astype(BF16)

    x2d = x.reshape(t, d)
    for l in range(depth):
        mod = mod_all[l]
        w_out_p = _take_static(w_out[l], _OUT_PERM, 0).astype(BF16)
        ident = jnp.zeros((CONV_WIDTH, 1), F32).at[CONV_WIDTH - 1, 0].set(1.0)
        conv_p = jnp.where(jnp.asarray(_GDN_CONV_SRC >= 0), _take_cols(conv_w[l].astype(F32), _GDN_CONV_SRC),
                           ident)
        pool_bd = jnp.zeros((POOL_DIM, POOL_DIM), F32)
        for gi in range(len(POOL_WINDOWS)):
            sl = slice(gi * POOL_GROUP, (gi + 1) * POOL_GROUP)
            pool_bd = pool_bd.at[sl, sl].set(pool_w[l, gi].astype(F32))
        alog_v = _lane_vector(gdn_a_log[l], N_GDN_HEADS)
        dtb_v = _lane_vector(gdn_dt_bias[l], N_GDN_HEADS)
        nw_v = jnp.tile(gdn_norm_w[l].astype(F32), 2).reshape(1, LANES)
        rw = jnp.zeros((d, LANES), F32).at[:, :N_EXPERTS].set(router_w[l].astype(F32))
        rw_hi, rw_lo = _split_bf16(rw)
        rb = jnp.full((1, LANES), NEG_INF, F32).at[0, :N_EXPERTS].set(router_b[l].astype(F32))

        u_pool, aq, akv, gdn, ba = _in_projection(x2d, mod, w_in_all, l, seq)
        y_pool = _pool_mixer(u_pool, pool_bd.astype(BF16), pool_scale[l].astype(F32), seq)
        y_att = _swa_attention(aq, akv, bias, attn_sinks[l].astype(F32), seq)
        y_gdn = _gdn_mixer(gdn, ba, conv_p, alog_v, dtb_v, nw_v, seq)
        x1, h2, info, cnt = _out_projection(x2d, mod, y_pool, y_att, y_gdn, w_out_p, ln1_g[l], ln1_b[l],
                                            rw_hi, rw_lo, rb, seq)
        x2d = _moe(h2, info, cnt, x1, mod, ln2_g[l], ln2_b[l], w_up_all, b_up_all, w_down_all, b_down_all,
                   l, seq)
    return x2d.reshape(bsz, seq, d)
```

```python
import functools

import numpy as np
import jax
import jax.numpy as jnp
from jax import lax
from jax.experimental import pallas as pl
from jax.experimental.pallas import tpu as pltpu
from jax.experimental.pallas import tpu_sc as plsc

F32 = jnp.float32
BF16 = jnp.bfloat16

D_MODEL = 1024
HEAD_DIM = 64
POOL_DIM = 256
POOL_WINDOWS = (2, 4, 8, 16)
POOL_GROUP = 64
N_ATT_HEADS = 6
N_KV_HEADS = 2
ATT_DIM = 384
KV_DIM = 128
WINDOW = 128
N_BUCKETS = 32
MAX_DISTANCE = 128
N_GDN_HEADS = 6
GDN_DIM = 384
CONV_WIDTH = 4
GDN_CHUNK = 64
N_EXPERTS = 32
TOP_K = 4
EXPERT_DIM = 1024
SWIGLU_ALPHA = 1.702
SWIGLU_LIMIT = 7.0
DEPTH = 2
DEEPNORM_ALPHA = (2 * DEPTH) ** 0.25
LN_EPS = 1e-5
NORM_EPS = 1e-6
NEG_INF = -1e30

LANES = 128
SUBLANES = 8
VMEM_LIMIT = 56 * 1024 * 1024

ROW_TILE = 512
IN_TILE = 1024
ATT_BLOCKS = 4
GDN_SUPER = 256
ROUTE_TILE = 2048
EXPERT_BLOCK = 512
SC_CHUNK = 64

_OFF_AQ = POOL_DIM
_OFF_AK = _OFF_AQ + ATT_DIM
_OFF_AV = _OFF_AK + KV_DIM
_OFF_GQ = _OFF_AV + KV_DIM
_OFF_GK = _OFF_GQ + GDN_DIM
_OFF_GV = _OFF_GK + GDN_DIM
_OFF_GZ = _OFF_GV + GDN_DIM
_OFF_GB = _OFF_GZ + GDN_DIM
_OFF_GA = _OFF_GB + N_GDN_HEADS
IN_DIM = _OFF_GA + N_GDN_HEADS

P_POOL = (0, POOL_DIM)
P_Q = (P_POOL[1], P_POOL[1] + ATT_DIM)
P_KV = (P_Q[1], P_Q[1] + 2 * KV_DIM)
P_GDN = (P_KV[1], P_KV[1] + 4 * GDN_DIM)
P_BA = (P_GDN[1], P_GDN[1] + LANES)
P_TOTAL = P_BA[1]


def _head_cols(off, h):
    return list(range(off + HEAD_DIM * h, off + HEAD_DIM * (h + 1)))


def _build_in_perm():
    cols = list(range(POOL_DIM))
    for p in range(N_ATT_HEADS // 2):
        cols += _head_cols(_OFF_AQ, p) + _head_cols(_OFF_AQ, p + 3)
    cols += list(range(_OFF_AK, _OFF_AK + 2 * KV_DIM))
    gdn_src = []
    for p in range(N_GDN_HEADS // 2):
        e, o = 2 * p, 2 * p + 1
        grp = (_head_cols(_OFF_GK, e) + _head_cols(_OFF_GQ, e)
               + _head_cols(_OFF_GQ, o) + _head_cols(_OFF_GK, o)
               + _head_cols(_OFF_GV, o) + _head_cols(_OFF_GV, e)
               + _head_cols(_OFF_GZ, o) + _head_cols(_OFF_GZ, e))
        cols += grp
        gdn_src += [c - _OFF_GQ if c < _OFF_GZ else -1 for c in grp]
    cols += list(range(_OFF_GB, _OFF_GB + 2 * N_GDN_HEADS))
    cols += [-1] * (LANES - 2 * N_GDN_HEADS)
    assert len(cols) == P_TOTAL
    return np.asarray(cols, np.int32), np.asarray(gdn_src, np.int32)


_IN_PERM, _GDN_CONV_SRC = _build_in_perm()


def _build_out_perm():
    rows = list(range(POOL_DIM))
    for p in range(N_ATT_HEADS // 2):
        rows += _head_cols(POOL_DIM, p) + _head_cols(POOL_DIM, p + 3)
    for p in range(N_GDN_HEADS // 2):
        rows += _head_cols(POOL_DIM + ATT_DIM, 2 * p + 1) + _head_cols(POOL_DIM + ATT_DIM, 2 * p)
    return np.asarray(rows, np.int32)


_OUT_PERM = _build_out_perm()


def _t5_bucket_line():
    n = np.maximum(2 * WINDOW - 1 - np.arange(3 * WINDOW - 1), 0)
    max_exact = N_BUCKETS // 2
    nf = np.maximum(n, 1).astype(np.float32)
    large = max_exact + (np.log(nf / max_exact) / np.float32(np.log(MAX_DISTANCE / max_exact))
                         * (N_BUCKETS - max_exact)).astype(np.int32)
    large = np.minimum(large, N_BUCKETS - 1)
    return np.where(n < max_exact, n, large).astype(np.int32)


_BUCKET_LINE = _t5_bucket_line()


def _band_bias(rel_bias):
    n_line = 3 * WINDOW - 1
    line = jnp.take(rel_bias.astype(F32), jnp.asarray(_BUCKET_LINE), axis=0).T
    heads = line.shape[0]
    padded = jnp.concatenate([line, jnp.zeros((heads, 1), F32)], axis=1)
    skew = jnp.tile(padded, (1, WINDOW))[:, :WINDOW * n_line].reshape(heads, WINDOW, n_line)
    return skew[:, :, WINDOW - 1:3 * WINDOW - 1]


def _take_static(w, perm, axis):
    parts = []
    start = 0
    for i in range(1, len(perm) + 1):
        run_ends = (i == len(perm) or ((perm[i] < 0) != (perm[i - 1] < 0))
                    or (perm[i] >= 0 and perm[i] != perm[i - 1] + 1))
        if run_ends:
            if perm[start] < 0:
                shape = list(w.shape)
                shape[axis] = i - start
                parts.append(jnp.zeros(shape, w.dtype))
            else:
                parts.append(lax.slice_in_dim(w, int(perm[start]), int(perm[start]) + (i - start), axis=axis))
            start = i
    return jnp.concatenate(parts, axis=axis)


def _take_cols(w, perm):
    return _take_static(w, perm, w.ndim - 1)


def _split_bf16(x):
    hi = x.astype(BF16)
    lo = (x - hi.astype(F32)).astype(BF16)
    return hi, lo


def _pack_bf16_pairs(x):
    n = x.shape[1] // 2
    bits = pltpu.bitcast(x.astype(BF16).astype(F32), jnp.int32)
    return lax.shift_right_logical(bits[:, :n], 16) | bits[:, n:]


def _unpack_bf16_pairs(u):
    lo = pltpu.bitcast(lax.shift_left(u, 16), F32)
    hi = pltpu.bitcast(u & jnp.int32(-65536), F32)
    return jnp.concatenate([lo, hi], axis=1)


def _dot(a, b):
    return jnp.dot(a, b, preferred_element_type=F32)


def _dot_nt(a, b):
    return lax.dot_general(a, b, (((1,), (1,)), ((), ())), preferred_element_type=F32)


def _dot_hi_exact_rhs(x, m_bf16):
    hi, lo = _split_bf16(x)
    return _dot(hi, m_bf16) + _dot(lo, m_bf16)


def _sigmoid(x):
    return 1.0 / (1.0 + jnp.exp(-x))


def _layer_norm(r, g, b):
    mu = jnp.mean(r, axis=-1, keepdims=True)
    d = r - mu
    var = jnp.mean(d * d, axis=-1, keepdims=True)
    return d * lax.rsqrt(var + LN_EPS) * g + b


def _cparams(sem):
    return pltpu.CompilerParams(dimension_semantics=sem, vmem_limit_bytes=VMEM_LIMIT)


def _mod_kernel(c_ref, w_ref, b_ref, o_ref):
    c = c_ref[...]
    ca = c * _sigmoid(c)
    ch, cl = _split_bf16(ca)
    wh, wl = _split_bf16(w_ref[0])
    o_ref[0] = _dot(ch, wh) + _dot(cl, wh) + _dot(ch, wl) + b_ref[0]


def _modulation(c, w_ada, b_ada):
    depth, d, n = w_ada.shape
    bsz = c.shape[0]
    tn = 512
    return pl.pallas_call(
        _mod_kernel,
        grid=(depth, n // tn),
        in_specs=[
            pl.BlockSpec((bsz, d), lambda l, j: (0, 0)),
            pl.BlockSpec((1, d, tn), lambda l, j: (l, 0, j)),
            pl.BlockSpec((1, 1, tn), lambda l, j: (l, 0, j)),
        ],
        out_specs=pl.BlockSpec((1, bsz, tn), lambda l, j: (l, 0, j)),
        out_shape=jax.ShapeDtypeStruct((depth, bsz, n), F32),
        compiler_params=_cparams(("arbitrary", "arbitrary")),
        name="adaln_mod",
    )(c, w_ada, b_ada.reshape(depth, 1, n))


def _perm_runs(perm):
    runs = []
    start = 0
    for i in range(1, len(perm) + 1):
        run_ends = (i == len(perm) or ((perm[i] < 0) != (perm[i - 1] < 0))
                    or (perm[i] >= 0 and perm[i] != perm[i - 1] + 1))
        if run_ends:
            runs.append((int(perm[start]) if perm[start] >= 0 else -1, i - start, start))
            start = i
    return runs


_IN_RUNS = _perm_runs(_IN_PERM)


def _inproj_kernel(layer, x_ref, mod_ref, w_hbm, pool_ref, q_ref, kv_ref, gdn_ref, ba_ref, w_f32, w_ref, sem):
    @pl.when(pl.program_id(0) == 0)
    def _():
        fetch = pltpu.make_async_copy(w_hbm.at[layer], w_f32, sem)
        fetch.start()
        fetch.wait()
        for src, n, dst in _IN_RUNS:
            if src < 0:
                w_ref[:, dst:dst + n] = jnp.zeros((w_ref.shape[0], n), BF16)
            else:
                w_ref[:, dst:dst + n] = w_f32[:, src:src + n].astype(BF16)
        q_cols = w_ref[:, P_Q[0]:P_Q[1]].astype(F32) * (HEAD_DIM ** -0.5)
        w_ref[:, P_Q[0]:P_Q[1]] = q_cols.astype(BF16)

    sh = mod_ref[0, 0:1, :]
    sc = mod_ref[0, 1:2, :]
    h = (x_ref[...] * (1.0 + sc) + sh).astype(BF16)

    def mm(rng):
        return _dot(h, w_ref[:, rng[0]:rng[1]])

    pool_ref[...] = mm(P_POOL)
    q_ref[...] = mm(P_Q).astype(BF16)
    kv_ref[...] = mm(P_KV).astype(BF16)
    gdn_ref[...] = mm(P_GDN)
    ba_ref[...] = mm(P_BA)


def _in_projection(x2d, mod, w_in, layer, seq):
    t, d = x2d.shape
    tm = min(IN_TILE, seq)
    widths = [r[1] - r[0] for r in (P_POOL, P_Q, P_KV, P_GDN, P_BA)]
    dtypes = [F32, BF16, BF16, F32, F32]
    return pl.pallas_call(
        functools.partial(_inproj_kernel, layer),
        grid=(t // tm,),
        in_specs=[
            pl.BlockSpec((tm, d), lambda i: (i, 0)),
            pl.BlockSpec((1, 6, d), lambda i: ((i * tm) // seq, 0, 0)),
            pl.BlockSpec(memory_space=pl.ANY),
        ],
        out_specs=[pl.BlockSpec((tm, w), lambda i: (i, 0)) for w in widths],
        out_shape=[jax.ShapeDtypeStruct((t, w), dt) for w, dt in zip(widths, dtypes)],
        scratch_shapes=[pltpu.VMEM((d, w_in.shape[2]), F32), pltpu.VMEM((d, P_TOTAL), BF16),
                        pltpu.SemaphoreType.DMA(())],
        compiler_params=_cparams(("arbitrary",)),
        name="in_proj",
    )(x2d, mod, w_in)


def _pool_kernel(u_ref, w_ref, scale_ref, o_ref):
    u = u_ref[...]
    row = lax.broadcasted_iota(jnp.int32, u.shape, 0)
    lane = lax.broadcasted_iota(jnp.int32, u.shape, 1)

    def shifted(a, s):
        return jnp.where(row >= s, pltpu.roll(a, s, axis=0), 0.0)

    sums = []
    acc = u
    for wdt in POOL_WINDOWS:
        acc = acc + shifted(acc, wdt // 2)
        sums.append(acc)
    grp = lane // POOL_GROUP
    wsum = sums[-1]
    win = jnp.full(u.shape, POOL_WINDOWS[-1], jnp.int32)
    for gi in range(len(POOL_WINDOWS) - 2, -1, -1):
        wsum = jnp.where(grp == gi, sums[gi], wsum)
        win = jnp.where(grp == gi, POOL_WINDOWS[gi], win)
    cnt = jnp.minimum(row + 1, win).astype(F32)
    p = wsum / cnt - u
    y = _dot(p.astype(BF16), w_ref[...]) * scale_ref[...]
    o_ref[...] = y.astype(BF16)


def _pool_mixer(u, pool_w_bd, pool_scale, seq):
    t, c = u.shape
    return pl.pallas_call(
        _pool_kernel,
        grid=(t // seq,),
        in_specs=[
            pl.BlockSpec((seq, c), lambda b: (b, 0)),
            pl.BlockSpec((c, c), lambda b: (0, 0)),
            pl.BlockSpec((1, c), lambda b: (0, 0)),
        ],
        out_specs=pl.BlockSpec((seq, c), lambda b: (b, 0)),
        out_shape=jax.ShapeDtypeStruct((t, c), BF16),
        compiler_params=_cparams(("arbitrary",)),
        name="pool_mixer",
    )(u, pool_w_bd, pool_scale.reshape(1, c))


def _attn_kernel(sink_ref, q_ref, kvc_ref, kvp_ref, bias_ref, o_ref):
    step = pl.program_id(1)
    qi = lax.broadcasted_iota(jnp.int32, (WINDOW, 2 * WINDOW), 0)
    kj = lax.broadcasted_iota(jnp.int32, (WINDOW, 2 * WINDOW), 1)
    dist = qi + WINDOW - kj
    in_band = (dist >= 0) & (dist < WINDOW)
    lo = lax.broadcasted_iota(jnp.int32, (WINDOW, LANES), 1) < HEAD_DIM
    for sub in range(ATT_BLOCKS):
        r0 = sub * WINDOW
        prev = kvp_ref[...] if sub == 0 else kvc_ref[r0 - WINDOW:r0, :]
        kv = jnp.concatenate([prev, kvc_ref[r0:r0 + WINDOW, :]], axis=0)
        k = kv[:, :KV_DIM]
        v = kv[:, KV_DIM:]
        valid = in_band & ((kj >= WINDOW) | (step > 0)) if sub == 0 else in_band
        for p in range(N_ATT_HEADS // 2):
            qp = q_ref[r0:r0 + WINDOW, p * LANES:(p + 1) * LANES]
            halves = []
            for half in range(2):
                h = p + 3 * half
                qm = jnp.where(lo if half == 0 else jnp.logical_not(lo), qp, jnp.zeros_like(qp))
                s = jnp.where(valid, _dot_nt(qm, k) + bias_ref[h], NEG_INF)
                sink = sink_ref[h]
                m = jnp.maximum(jnp.max(s, axis=-1, keepdims=True), sink)
                pr = jnp.exp(s - m)
                den = jnp.sum(pr, axis=-1, keepdims=True) + jnp.exp(sink - m)
                halves.append(_dot(pr.astype(BF16), v) / den)
            o_ref[r0:r0 + WINDOW, p * LANES:(p + 1) * LANES] = (
                jnp.where(lo, halves[0], halves[1]).astype(BF16))


def _swa_attention(q, kv, bias, sinks, seq):
    t = q.shape[0]
    rows = ATT_BLOCKS * WINDOW
    nblk = seq // rows
    return pl.pallas_call(
        _attn_kernel,
        grid=(t // seq, nblk),
        in_specs=[
            pl.BlockSpec(memory_space=pltpu.SMEM),
            pl.BlockSpec((rows, ATT_DIM), lambda b, n: (b * nblk + n, 0)),
            pl.BlockSpec((rows, 2 * KV_DIM), lambda b, n: (b * nblk + n, 0)),
            pl.BlockSpec((WINDOW, 2 * KV_DIM),
                         lambda b, n: (jnp.maximum((b * nblk + n) * ATT_BLOCKS - 1, 0), 0)),
            pl.BlockSpec((N_ATT_HEADS, WINDOW, 2 * WINDOW), lambda b, n: (0, 0, 0)),
        ],
        out_specs=pl.BlockSpec((rows, ATT_DIM), lambda b, n: (b * nblk + n, 0)),
        out_shape=jax.ShapeDtypeStruct((t, ATT_DIM), BF16),
        compiler_params=_cparams(("arbitrary", "arbitrary")),
        name="swa_attention",
    )(sinks, q, kv, kv, bias)


_GDN_BASE = SUBLANES
_GDN_LEVELS = int(np.log2(GDN_CHUNK // _GDN_BASE))


def _gdn_masks():
    r = np.arange(GDN_SUPER)
    ri, ci = r[:, None], r[None, :]
    same_chunk = (ri // GDN_CHUNK) == (ci // GDN_CHUNK)
    incl = same_chunk & (ri >= ci)
    planes = [incl, ri == ci]
    base = ((ri // _GDN_BASE) == (ci // _GDN_BASE)) & (ri > ci)
    planes.append(base)
    for lvl in range(_GDN_LEVELS):
        small = _GDN_BASE << lvl
        planes.append(((ri // (2 * small)) == (ci // (2 * small))) & ((ri // small) != (ci // small)) & (ri > ci))
    bmask = np.stack(planes).astype(np.float32)
    bmask[2] = -bmask[2]
    negmask = np.where(incl, 0.0, -np.inf).astype(np.float32)
    return negmask, bmask


_GDN_NEGMASK, _GDN_BMASK = _gdn_masks()


def _gdn_kernel(x_ref, halo_ref, ba_ref, cw_ref, alog_ref, dtb_ref, nw_ref, negmask_ref, bmask_ref,
                y_ref, state_ref, xs_ref):
    sc_id = pl.program_id(1)
    rows = GDN_SUPER
    nchunk = rows // GDN_CHUNK
    c_sz = GDN_CHUNK

    @pl.when(sc_id == 0)
    def _():
        state_ref[...] = jnp.zeros_like(state_ref)

    xs_ref[:SUBLANES, :] = jnp.where(sc_id == 0, 0.0, halo_ref[...])
    xs_ref[SUBLANES:, :] = x_ref[...]
    act = []
    for g in range(x_ref.shape[1] // LANES):
        cols = slice(g * LANES, (g + 1) * LANES)
        acc = x_ref[:, cols] * cw_ref[CONV_WIDTH - 1:CONV_WIDTH, cols]
        if g % 4 != 3:
            for s in range(1, CONV_WIDTH):
                acc = acc + (xs_ref[SUBLANES - s:SUBLANES - s + rows, cols]
                             * cw_ref[CONV_WIDTH - 1 - s:CONV_WIDTH - s, cols])
        act.append(acc * _sigmoid(acc))

    negmask = negmask_ref[...]
    tri_incl = bmask_ref[0]
    eye_b = bmask_ref[1]
    base_neg = bmask_ref[2]
    bands = [bmask_ref[3 + lvl] for lvl in range(_GDN_LEVELS)]
    li = lax.broadcasted_iota(jnp.int32, (LANES, LANES), 0)
    lj = lax.broadcasted_iota(jnp.int32, (LANES, LANES), 1)
    half_ones = jnp.where((li // HEAD_DIM) == (lj // HEAD_DIM), 1.0, 0.0).astype(BF16)
    lane_lo = lax.broadcasted_iota(jnp.int32, (rows, LANES), 1) < HEAD_DIM
    lane_lo_c = lax.broadcasted_iota(jnp.int32, (c_sz, LANES), 1) < HEAD_DIM

    ba = ba_ref[...]
    beta_all = _sigmoid(ba)
    sp_in = ba + dtb_ref[...]
    softplus = jnp.maximum(sp_in, 0.0) + jnp.log(1.0 + jnp.exp(-jnp.abs(sp_in)))
    g_all = -jnp.exp(alog_ref[...]) * softplus
    gcum = _dot_hi_exact_rhs_lhs(tri_incl, g_all)
    gcum_t = gcum.T

    heads = range(N_GDN_HEADS)
    lane_hi = jnp.logical_not(lane_lo)
    lane_hi_c = jnp.logical_not(lane_lo_c)
    mk = [lane_lo if h % 2 == 0 else lane_hi for h in heads]
    mk_c = [lane_lo_c if h % 2 == 0 else lane_hi_c for h in heads]
    scale = HEAD_DIM ** -0.5

    def bdot(a, b):
        return _dot(a.astype(BF16), b.astype(BF16))

    xk, xq, gn, gc_col, beta, eg = [], [], [], [], [], []
    for h in heads:
        g = act[4 * (h // 2) + (h % 2)]
        g = g * lax.rsqrt(_dot_hi_exact_rhs(g * g, half_ones) + NORM_EPS)
        gn.append(g)
        xk.append(jnp.where(mk[h], g, 0.0))
        xq.append(jnp.where(mk[h], pltpu.roll(g, HEAD_DIM, axis=1), 0.0) * scale)
        beta.append(beta_all[:, h:h + 1])
        gc_col.append(gcum[:, N_GDN_HEADS + h:N_GDN_HEADS + h + 1])
        eg.append(jnp.exp(gc_col[h]))

    l_b, attn, rhs = [], [], []
    for h in heads:
        gc_row = gcum_t[N_GDN_HEADS + h:N_GDN_HEADS + h + 1, :]
        decay = jnp.exp(gc_col[h] - gc_row + negmask)
        xk_b = xk[h].astype(BF16)
        kk = _dot_nt((xk[h] * beta[h]).astype(BF16), xk_b)
        l_b.append((kk * decay).astype(BF16))
        attn.append((_dot_nt(xq[h].astype(BF16), xk_b) * decay).astype(BF16))
        vv = act[4 * (h // 2) + 2]
        rhs.append(jnp.where(mk[h], gn[h] * eg[h], vv) * beta[h])

    a1 = [l_b[h] * base_neg for h in heads]
    a2 = [_dot(a1[h], a1[h]).astype(BF16) for h in heads]
    a4 = [_dot(a2[h], a2[h]).astype(BF16) for h in heads]
    inv0 = [eye_b + a1[h] for h in heads]
    acc1 = [inv0[h].astype(F32) + _dot(a2[h], inv0[h]) for h in heads]
    inv_b = [(acc1[h] + _dot(a4[h], acc1[h].astype(BF16))).astype(BF16) for h in heads]
    for lvl in range(_GDN_LEVELS - 1):
        mid = [_dot(l_b[h] * bands[lvl], inv_b[h]).astype(BF16) for h in heads]
        inv_b = [inv_b[h] - _dot(inv_b[h], mid[h]).astype(BF16) for h in heads]
    half = [_dot(inv_b[h], rhs[h].astype(BF16)) for h in heads]
    mid = [_dot(l_b[h] * bands[_GDN_LEVELS - 1], half[h].astype(BF16)) for h in heads]
    sol = [half[h] - _dot(inv_b[h], mid[h].astype(BF16)) for h in heads]

    lane_lo_s = lax.broadcasted_iota(jnp.int32, (LANES, LANES), 1) < HEAD_DIM
    mk_s = [lane_lo_s if h % 2 == 0 else jnp.logical_not(lane_lo_s) for h in heads]
    sol_b = [sol[h].astype(BF16) for h in heads]
    attn_sol = [_dot(attn[h], sol_b[h]) for h in heads]
    q_eff = [(xq[h] * eg[h] - jnp.where(mk[h], attn_sol[h], 0.0)).astype(BF16) for h in heads]
    o_free = [jnp.where(mk[h], 0.0, attn_sol[h]) for h in heads]
    kw = [[] for _ in heads]
    ku = [[] for _ in heads]
    cdec = [[] for _ in heads]
    for c in range(nchunk):
        r0 = c * c_sz
        for h in heads:
            glast = gcum[r0 + c_sz - 1:r0 + c_sz, N_GDN_HEADS + h:N_GDN_HEADS + h + 1]
            kd_t = (xk[h][r0:r0 + c_sz] * jnp.exp(glast - gc_col[h][r0:r0 + c_sz])).T
            both = _dot(kd_t.astype(BF16), sol_b[h][r0:r0 + c_sz])
            kw[h].append(jnp.where(mk_s[h], both, 0.0).astype(BF16))
            ku[h].append(jnp.where(mk_s[h], 0.0, both))
            cdec[h].append(jnp.exp(glast))
    st = [state_ref[h] for h in heads]
    o_parts = [[] for _ in heads]
    for c in range(nchunk):
        r0 = c * c_sz
        for h in heads:
            lhs = jnp.concatenate([kw[h][c], q_eff[h][r0:r0 + c_sz]], axis=0)
            prod = _dot(lhs, st[h].astype(BF16))
            o_parts[h].append(prod[LANES:] + o_free[h][r0:r0 + c_sz])
            st[h] = st[h] * cdec[h][c] + ku[h][c] - prod[:LANES]
    for h in heads:
        state_ref[h] = st[h]

    for p in range(N_GDN_HEADS // 2):
        o_pair = [jnp.concatenate(o_parts[h], axis=0) for h in (2 * p, 2 * p + 1)]
        o = jnp.where(lane_lo, o_pair[1], o_pair[0])
        ms = _dot_hi_exact_rhs(o * o, half_ones) * (1.0 / HEAD_DIM)
        zz = act[4 * p + 3]
        y = o * lax.rsqrt(ms + NORM_EPS) * nw_ref[...] * zz
        y_ref[:, p * LANES:(p + 1) * LANES] = y.astype(BF16)


def _dot_hi_exact_rhs_lhs(m_bf16, x):
    hi, lo = _split_bf16(x)
    return _dot(m_bf16, hi) + _dot(m_bf16, lo)


def _gdn_mixer(gdn, ba, conv_p, alog_v, dtb_v, nw_v, seq):
    t, c = gdn.shape
    rows = GDN_SUPER
    nsc = seq // rows
    hb = rows // SUBLANES
    return pl.pallas_call(
        _gdn_kernel,
        grid=(t // seq, nsc),
        in_specs=[
            pl.BlockSpec((rows, c), lambda b, s: (b * nsc + s, 0)),
            pl.BlockSpec((SUBLANES, c), lambda b, s: (jnp.maximum((b * nsc + s) * hb - 1, 0), 0)),
            pl.BlockSpec((rows, LANES), lambda b, s: (b * nsc + s, 0)),
            pl.BlockSpec((CONV_WIDTH, c), lambda b, s: (0, 0)),
            pl.BlockSpec((1, LANES), lambda b, s: (0, 0)),
            pl.BlockSpec((1, LANES), lambda b, s: (0, 0)),
            pl.BlockSpec((1, LANES), lambda b, s: (0, 0)),
            pl.BlockSpec((rows, rows), lambda b, s: (0, 0)),
            pl.BlockSpec((3 + _GDN_LEVELS, rows, rows), lambda b, s: (0, 0, 0)),
        ],
        out_specs=pl.BlockSpec((rows, GDN_DIM), lambda b, s: (b * nsc + s, 0)),
        out_shape=jax.ShapeDtypeStruct((t, GDN_DIM), BF16),
        scratch_shapes=[pltpu.VMEM((N_GDN_HEADS, LANES, LANES), F32), pltpu.VMEM((rows + SUBLANES, c), F32)],
        compiler_params=_cparams(("arbitrary", "arbitrary")),
        name="gdn_mixer",
    )(gdn, gdn, ba, conv_p, alog_v, dtb_v, nw_v, jnp.asarray(_GDN_NEGMASK), jnp.asarray(_GDN_BMASK, BF16))


def _route_tile(logits, before, carry_ref, live):
    shape = logits.shape
    lane = lax.broadcasted_iota(jnp.int32, shape, 1).astype(F32)
    work = logits
    vals, idxs = [], []
    for _k in range(TOP_K):
        m = jnp.max(work, axis=-1, keepdims=True)
        idx = jnp.min(jnp.where(work == m, lane, float(LANES)), axis=-1, keepdims=True)
        vals.append(m)
        idxs.append(idx)
        work = jnp.where(lane == idx, -jnp.inf, work)
    exps = [jnp.exp(v - vals[0]) for v in vals]
    den = exps[0] + exps[1] + exps[2] + exps[3]
    onehots = [lane == idx for idx in idxs]
    member = jnp.zeros(shape, F32)
    for oh in onehots:
        member = member + jnp.where(oh, 1.0, 0.0)
    rank = _dot(before, member.astype(BF16)) + carry_ref[...]
    carry_ref[...] = carry_ref[...] + live * jnp.sum(member, axis=0, keepdims=True)
    info = jnp.zeros(shape, F32)
    for k in range(TOP_K):
        rank_k = jnp.sum(jnp.where(onehots[k], rank, 0.0), axis=-1, keepdims=True)
        info = jnp.where(lane == float(k), idxs[k], info)
        info = jnp.where(lane == float(TOP_K + k), rank_k, info)
        info = jnp.where(lane == float(2 * TOP_K + k), exps[k] / den, info)
    return info


def _outproj_kernel(x_ref, mod_ref, yp_ref, ya_ref, yg_ref, wp_ref, wa_ref, wg_ref, lng_ref, lnb_ref,
                    rwh_ref, rwl_ref, rb_ref, before_ref, x1_ref, h2_ref, info_ref, cnt_ref,
                    carry_ref, logit_s):
    step = pl.program_id(0)

    @pl.when(step == 0)
    def _():
        carry_ref[...] = jnp.zeros_like(carry_ref)
        logit_s[...] = jnp.zeros_like(logit_s)

    y = _dot(yp_ref[...], wp_ref[...]) + _dot(ya_ref[...], wa_ref[...]) + _dot(yg_ref[...], wg_ref[...])
    live = jnp.where(step > 0, 1.0, 0.0)
    info_ref[...] = _route_tile(logit_s[...], before_ref[...], carry_ref, live)
    cnt_ref[...] = carry_ref[...]
    g1 = mod_ref[0, 2:3, :]
    sh2 = mod_ref[0, 3:4, :]
    sc2 = mod_ref[0, 4:5, :]
    x1 = _layer_norm(DEEPNORM_ALPHA * x_ref[...] + g1 * y, lng_ref[...], lnb_ref[...])
    x1_ref[...] = x1
    h2 = x1 * (1.0 + sc2) + sh2
    hh, hl = _split_bf16(h2)
    h2_ref[...] = _pack_bf16_pairs(h2)
    logit_s[...] = _dot(hh, rwh_ref[...]) + _dot(hl, rwh_ref[...]) + _dot(hh, rwl_ref[...]) + rb_ref[...]


def _out_projection(x2d, mod, yp, ya, yg, w_out_p, ln_g, ln_b, rw_hi, rw_lo, rb, seq):
    t, d = x2d.shape
    tm = ROW_TILE
    wp = w_out_p[:POOL_DIM]
    wa = w_out_p[POOL_DIM:POOL_DIM + ATT_DIM]
    wg = w_out_p[POOL_DIM + ATT_DIM:]
    last = t // tm - 1
    row = lambda i: (jnp.minimum(i, last), 0)
    routed = lambda i: (jnp.maximum(i - 1, 0), 0)
    fixed = lambda i: (0, 0)
    return pl.pallas_call(
        _outproj_kernel,
        grid=(t // tm + 1,),
        in_specs=[
            pl.BlockSpec((tm, d), row),
            pl.BlockSpec((1, 6, d), lambda i: ((jnp.minimum(i, last) * tm) // seq, 0, 0)),
            pl.BlockSpec((tm, POOL_DIM), row),
            pl.BlockSpec((tm, ATT_DIM), row),
            pl.BlockSpec((tm, GDN_DIM), row),
            pl.BlockSpec((POOL_DIM, d), fixed),
            pl.BlockSpec((ATT_DIM, d), fixed),
            pl.BlockSpec((GDN_DIM, d), fixed),
            pl.BlockSpec((1, d), fixed),
            pl.BlockSpec((1, d), fixed),
            pl.BlockSpec((d, LANES), fixed),
            pl.BlockSpec((d, LANES), fixed),
            pl.BlockSpec((1, LANES), fixed),
            pl.BlockSpec((tm, tm), fixed),
        ],
        out_specs=[pl.BlockSpec((tm, d), row), pl.BlockSpec((tm, d // 2), row), pl.BlockSpec((tm, LANES), routed),
                   pl.BlockSpec((1, LANES), fixed)],
        out_shape=[jax.ShapeDtypeStruct((t, d), F32), jax.ShapeDtypeStruct((t, d // 2), jnp.int32),
                   jax.ShapeDtypeStruct((t, LANES), F32), jax.ShapeDtypeStruct((1, LANES), F32)],
        scratch_shapes=[pltpu.VMEM((1, LANES), F32), pltpu.VMEM((tm, LANES), F32)],
        compiler_params=_cparams(("arbitrary",)),
        name="out_proj_ln_route",
    )(x2d, mod, yp, ya, yg, wp, wa, wg, ln_g.reshape(1, d), ln_b.reshape(1, d), rw_hi, rw_lo, rb,
      jnp.tril(jnp.ones((tm, tm), BF16), -1))


def _slot_kernel(info_ref, cnt_ref, dest_ref, pcum_ref, pstart_ref):
    shape = info_ref.shape
    lane = lax.broadcasted_iota(jnp.int32, shape, 1).astype(F32)

    @pl.when(pl.program_id(0) == 0)
    def _():
        cnt = jnp.broadcast_to(cnt_ref[...], (SUBLANES, LANES))
        padded = jnp.floor((cnt + float(EXPERT_BLOCK - 1)) * (1.0 / EXPERT_BLOCK)) * float(EXPERT_BLOCK)
        lane8 = lax.broadcasted_iota(jnp.int32, (SUBLANES, LANES), 1)
        acc = padded
        step = 1
        while step < LANES:
            acc = acc + jnp.where(lane8 >= step, pltpu.roll(acc, step, axis=1), 0.0)
            step *= 2
        pstart_ref[...] = (acc - padded)[:1]
        pcum_ref[...] = acc[:1].astype(jnp.int32)

    info = info_ref[...]
    slots = jnp.zeros(shape, F32)
    for k in range(TOP_K):
        onehot = lane == info[:, k:k + 1]
        start = jnp.sum(jnp.where(onehot, pstart_ref[...], 0.0), axis=-1, keepdims=True)
        slots = jnp.where(lane == float(k), start + info[:, TOP_K + k:TOP_K + k + 1], slots)
    dest_ref[...] = slots.T[:SUBLANES].astype(jnp.int32)


def _slots(info, cnt):
    t = info.shape[0]
    tm = min(ROUTE_TILE, t)
    assert t % tm == 0
    return pl.pallas_call(
        _slot_kernel,
        grid=(t // tm,),
        in_specs=[pl.BlockSpec((tm, LANES), lambda i: (i, 0)), pl.BlockSpec((1, LANES), lambda i: (0, 0))],
        out_specs=[pl.BlockSpec((SUBLANES, tm), lambda i: (0, i)), pl.BlockSpec((1, LANES), lambda i: (0, 0))],
        out_shape=[jax.ShapeDtypeStruct((SUBLANES, t), jnp.int32), jax.ShapeDtypeStruct((1, LANES), jnp.int32)],
        scratch_shapes=[pltpu.VMEM((1, LANES), F32)],
        compiler_params=_cparams(("arbitrary",)),
        name="moe_slots",
    )(info, cnt)


def _expert_kernel(e0, be_ref, nxt_ref, val_ref, nu_ref, x_ref, wup_hbm, bup_ref, wdn_hbm, bdn_ref, y_ref,
                   wup_st, wdn_st, wup_bf, wdn_bf, sems):
    i = pl.program_id(0)
    e = be_ref[i]
    prev = be_ref[jnp.maximum(i - 1, 0)]
    used = i < nu_ref[0]

    def weight_copies(expert):
        return (pltpu.make_async_copy(wup_hbm.at[e0 + expert], wup_st, sems.at[0]),
                pltpu.make_async_copy(wdn_hbm.at[e0 + expert], wdn_st, sems.at[1]))

    @pl.when(i == 0)
    def _():
        for cp in weight_copies(e):
            cp.start()

    @pl.when(used & ((i == 0) | (e != prev)))
    def _():
        for cp in weight_copies(e):
            cp.wait()
        wup_bf[...] = wup_st[...].astype(BF16)
        wdn_bf[...] = wdn_st[...].astype(BF16)

        @pl.when(nxt_ref[i] >= 0)
        def _():
            for cp in weight_copies(nxt_ref[i]):
                cp.start()

    def ffn(rows):
        xb = _unpack_bf16_pairs(x_ref[:rows, :]).astype(BF16)
        hb = _dot(xb, wup_bf[...]) + bup_ref[0]
        x_glu = jnp.minimum(hb[:, :EXPERT_DIM], SWIGLU_LIMIT)
        x_lin = jnp.clip(hb[:, EXPERT_DIM:], -SWIGLU_LIMIT, SWIGLU_LIMIT)
        act = x_glu * _sigmoid(SWIGLU_ALPHA * x_glu) * (x_lin + 1.0)
        y = _dot(act.astype(BF16), wdn_bf[...]) + bdn_ref[0]
        y_ref[:rows, :] = _pack_bf16_pairs(y)

    half_rows = x_ref.shape[0] // 2
    small = val_ref[i] <= half_rows

    @pl.when(used & jnp.logical_not(small))
    def _():
        ffn(x_ref.shape[0])

    @pl.when(used & small)
    def _():
        ffn(half_rows)
        y_ref[half_rows:, :] = jnp.zeros((x_ref.shape[0] - half_rows, y_ref.shape[1]), y_ref.dtype)

    @pl.when(i >= nu_ref[0])
    def _():
        y_ref[...] = jnp.zeros_like(y_ref)


def _expert_ffn(xbuf, block_e, next_e, valid, n_used, w_up, b_up, w_down, b_down, layer):
    p, dh = xbuf.shape
    d = 2 * dh
    bm = EXPERT_BLOCK
    ne, _, n_up = w_up.shape
    e0 = layer * N_EXPERTS
    grid_spec = pltpu.PrefetchScalarGridSpec(
        num_scalar_prefetch=4,
        grid=(p // bm,),
        in_specs=[
            pl.BlockSpec((bm, dh), lambda i, be, nx, vl, nu: (i, 0)),
            pl.BlockSpec(memory_space=pl.ANY),
            pl.BlockSpec((1, 1, n_up), lambda i, be, nx, vl, nu: (e0 + be[i], 0, 0)),
            pl.BlockSpec(memory_space=pl.ANY),
            pl.BlockSpec((1, 1, d), lambda i, be, nx, vl, nu: (e0 + be[i], 0, 0)),
        ],
        out_specs=pl.BlockSpec((bm, dh), lambda i, be, nx, vl, nu: (i, 0)),
        scratch_shapes=[pltpu.VMEM((d, n_up), F32), pltpu.VMEM((EXPERT_DIM, d), F32),
                        pltpu.VMEM((d, n_up), BF16), pltpu.VMEM((EXPERT_DIM, d), BF16),
                        pltpu.SemaphoreType.DMA((2,))],
    )
    return pl.pallas_call(
        functools.partial(_expert_kernel, e0),
        grid_spec=grid_spec,
        out_shape=jax.ShapeDtypeStruct((p, dh), jnp.int32),
        compiler_params=_cparams(("arbitrary",)),
        name="expert_ffn",
    )(block_e, next_e, valid, n_used, xbuf, w_up, b_up, w_down, b_down)


def _combine_kernel(x1_ref, mod_ref, yg_ref, info_ref, lng_ref, lnb_ref, o_ref):
    info = info_ref[...]
    y = jnp.zeros(x1_ref.shape, F32)
    for k in range(TOP_K):
        gate = info[:, 2 * TOP_K + k:2 * TOP_K + k + 1]
        y = y + gate * _unpack_bf16_pairs(yg_ref[k])
    g2 = mod_ref[0, 5:6, :]
    o_ref[...] = _layer_norm(DEEPNORM_ALPHA * x1_ref[...] + g2 * y, lng_ref[...], lnb_ref[...])


def _combine(x1, mod, yg, info, ln_g, ln_b, seq):
    t, d = x1.shape
    tm = ROW_TILE
    row = lambda i: (i, 0)
    fixed = lambda i: (0, 0)
    return pl.pallas_call(
        _combine_kernel,
        grid=(t // tm,),
        in_specs=[
            pl.BlockSpec((tm, d), row),
            pl.BlockSpec((1, 6, d), lambda i: ((i * tm) // seq, 0, 0)),
            pl.BlockSpec((TOP_K, tm, d // 2), lambda i: (0, i, 0)),
            pl.BlockSpec((tm, LANES), row),
            pl.BlockSpec((1, d), fixed),
            pl.BlockSpec((1, d), fixed),
        ],
        out_specs=pl.BlockSpec((tm, d), row),
        out_shape=jax.ShapeDtypeStruct((t, d), F32),
        compiler_params=_cparams(("arbitrary",)),
        name="moe_combine_ln",
    )(x1, mod, yg, info, ln_g.reshape(1, d), ln_b.reshape(1, d))


def _sc_workers():
    info = plsc.get_sparse_core_info()
    return info.num_cores, info.num_cores * info.num_subcores


def _sc_scatter_rows(rows, idx, n_out):
    t, w = rows.shape
    kk = idx.shape[0]
    n_cores, n_workers = _sc_workers()
    ch = SC_CHUNK
    assert t % (2 * n_workers * ch) == 0
    n_chunk = t // (n_workers * ch)
    idx_c = jnp.transpose(idx.reshape(kk, t // ch, ch), (1, 0, 2))

    @functools.partial(
        pl.kernel,
        mesh=plsc.VectorSubcoreMesh(core_axis_name="c", subcore_axis_name="s"),
        out_type=jax.ShapeDtypeStruct((n_out, w), rows.dtype),
        scratch_types=[pltpu.VMEM((2, kk, ch), jnp.int32), pltpu.VMEM((2, ch, w), rows.dtype),
                       pltpu.SemaphoreType.DMA((2,)), pltpu.SemaphoreType.DMA((2,))],
        name="sc_dispatch_scatter",
    )
    def scatter_kernel(rows_hbm, idx_hbm, out_hbm, idx_v, rows_v, load_sem, scat_sem):
        base = (lax.axis_index("s") * n_cores + lax.axis_index("c")) * n_chunk

        def load(j, b):
            return pltpu.make_async_copy(rows_hbm.at[pl.ds((base + j) * ch, ch)], rows_v.at[b], load_sem.at[b])

        def scatters(b):
            return [pltpu.make_async_copy(rows_v.at[b], out_hbm.at[idx_v.at[b, q]], scat_sem.at[b])
                    for q in range(kk)]

        pltpu.sync_copy(idx_hbm.at[base], idx_v.at[0])
        load(0, 0).start()

        @pl.loop(0, n_chunk, step=2)
        def _(j0):
            for b in range(2):
                j = j0 + b
                other = 1 - b

                @pl.when(j >= 1)
                def _():
                    for cp in scatters(other):
                        cp.wait()

                @pl.when(j + 1 < n_chunk)
                def _():
                    pltpu.sync_copy(idx_hbm.at[base + j + 1], idx_v.at[other])
                    load(j + 1, other).start()

                load(j, b).wait()
                for cp in scatters(b):
                    cp.start()

        for cp in scatters((n_chunk - 1) % 2):
            cp.wait()

    return scatter_kernel(rows, idx_c)


def _sc_gather_rows(table, idx):
    m = idx.shape[0]
    w = table.shape[1]
    n_cores, n_workers = _sc_workers()
    ch = SC_CHUNK
    assert m % (2 * n_workers * ch) == 0
    n_chunk = m // (n_workers * ch)
    idx_c = idx.reshape(m // ch, 1, ch)

    @functools.partial(
        pl.kernel,
        mesh=plsc.VectorSubcoreMesh(core_axis_name="c", subcore_axis_name="s"),
        out_type=jax.ShapeDtypeStruct((m, w), table.dtype),
        scratch_types=[pltpu.VMEM((2, 1, ch), jnp.int32), pltpu.VMEM((2, ch, w), table.dtype),
                       pltpu.SemaphoreType.DMA((2,)), pltpu.SemaphoreType.DMA((2,))],
        name="sc_combine_gather",
    )
    def gather_kernel(table_hbm, idx_hbm, out_hbm, idx_v, rows_v, gather_sem, write_sem):
        base = (lax.axis_index("s") * n_cores + lax.axis_index("c")) * n_chunk

        def gather(b):
            return pltpu.make_async_copy(table_hbm.at[idx_v.at[b, 0]], rows_v.at[b], gather_sem.at[b])

        def write(j, b):
            return pltpu.make_async_copy(rows_v.at[b], out_hbm.at[pl.ds((base + j) * ch, ch)], write_sem.at[b])

        pltpu.sync_copy(idx_hbm.at[base], idx_v.at[0])
        gather(0).start()

        @pl.loop(0, n_chunk, step=2)
        def _(j0):
            for b in range(2):
                j = j0 + b
                other = 1 - b

                @pl.when(j >= 1)
                def _():
                    write(j - 1, other).wait()

                @pl.when(j + 1 < n_chunk)
                def _():
                    pltpu.sync_copy(idx_hbm.at[base + j + 1], idx_v.at[other])
                    gather(other).start()

                gather(b).wait()
                write(j, b).start()

        write(n_chunk - 1, (n_chunk - 1) % 2).wait()

    return gather_kernel(table, idx_c)


def _lane_vector(vals, offset):
    return jnp.zeros((1, LANES), F32).at[0, offset:offset + vals.shape[0]].set(vals.astype(F32))


def _moe(h2, info, cnt, x1, mod, ln_g, ln_b, w_up, b_up, w_down, b_down, layer, seq):
    t, dh = h2.shape
    a = t * TOP_K
    bm = EXPERT_BLOCK
    slots, pcum_v = _slots(info, cnt)
    pcum = pcum_v[0, :N_EXPERTS]
    dest = slots[:TOP_K]
    n_blocks = -(-a // bm) + N_EXPERTS
    starts = jnp.arange(n_blocks, dtype=jnp.int32) * bm
    block_e = jnp.minimum(jnp.sum(pcum[None, :] <= starts[:, None], axis=1), N_EXPERTS - 1).astype(jnp.int32)
    n_used = (pcum[-1] // bm).astype(jnp.int32).reshape(1)
    later = block_e[None, :] > block_e[:, None]
    group_end = n_blocks - jnp.sum(later, axis=1)
    next_e = jnp.min(jnp.where(later, block_e[None, :], N_EXPERTS), axis=1)
    next_e = jnp.where(group_end < n_used[0], next_e, -1).astype(jnp.int32)
    counts = cnt[0, :N_EXPERTS].astype(jnp.int32)
    pstart = pcum - ((counts + bm - 1) // bm) * bm
    mine = block_e[:, None] == jnp.arange(N_EXPERTS, dtype=jnp.int32)[None, :]
    count_b = jnp.sum(jnp.where(mine, counts[None, :], 0), axis=1)
    pstart_b = jnp.sum(jnp.where(mine, pstart[None, :], 0), axis=1)
    valid = jnp.clip(count_b - (starts - pstart_b), 0, bm).astype(jnp.int32)
    xbuf = _sc_scatter_rows(h2, dest, n_blocks * bm)
    ybuf = _expert_ffn(xbuf, block_e, next_e, valid, n_used, w_up, b_up, w_down, b_down, layer)
    yg = _sc_gather_rows(ybuf, dest.reshape(a)).reshape(TOP_K, t, dh)
    return _combine(x1, mod, yg, info, ln_g, ln_b, seq)


def kernel(x, c, rel_bias, w_in, w_out, w_ada, b_ada, ln1_g, ln1_b, ln2_g, ln2_b, pool_w, pool_scale,
           attn_sinks, conv_w, gdn_a_log, gdn_dt_bias, gdn_norm_w, router_w, router_b,
           exp_w_up, exp_b_up, exp_w_down, exp_b_down):
    bsz, seq, d = x.shape
    depth = w_in.shape[0]
    t = bsz * seq
    assert d == D_MODEL and w_in.shape[2] == IN_DIM
    assert seq % GDN_SUPER == 0 and seq % (ATT_BLOCKS * WINDOW) == 0
    assert t % ROW_TILE == 0 and seq % ROW_TILE == 0

    mod_all = _modulation(c, w_ada, b_ada).reshape(depth, bsz, 6, d)
    bias = _band_bias(rel_bias)

    w_up_all = exp_w_up.reshape((depth * N_EXPERTS,) + exp_w_up.shape[2:])
    b_up_all = exp_b_up.reshape(depth * N_EXPERTS, 1, exp_b_up.shape[2])
    w_down_all = exp_w_down.reshape((depth * N_EXPERTS,) + exp_w_down.shape[2:])
    b_down_all = exp_b_down.reshape(depth * N_EXPERTS, 1, exp_b_down.shape[2])

    x2d = x.reshape(t, d)
    for l in range(depth):
        mod = mod_all[l]
        w_out_p = _take_static(w_out[l], _OUT_PERM, 0).astype(BF16)
        ident = jnp.zeros((CONV_WIDTH, 1), F32).at[CONV_WIDTH - 1, 0].set(1.0)
        conv_p = jnp.where(jnp.asarray(_GDN_CONV_SRC >= 0), _take_cols(conv_w[l].astype(F32), _GDN_CONV_SRC),
                           ident)
        pool_bd = jnp.zeros((POOL_DIM, POOL_DIM), F32)
        for gi in range(len(POOL_WINDOWS)):
            sl = slice(gi * POOL_GROUP, (gi + 1) * POOL_GROUP)
            pool_bd = pool_bd.at[sl, sl].set(pool_w[l, gi].astype(F32))
        alog_v = _lane_vector(gdn_a_log[l], N_GDN_HEADS)
        dtb_v = _lane_vector(gdn_dt_bias[l], N_GDN_HEADS)
        nw_v = jnp.tile(gdn_norm_w[l].astype(F32), 2).reshape(1, LANES)
        rw = jnp.zeros((d, LANES), F32).at[:, :N_EXPERTS].set(router_w[l].astype(F32))
        rw_hi, rw_lo = _split_bf16(rw)
        rb = jnp.full((1, LANES), NEG_INF, F32).at[0, :N_EXPERTS].set(router_b[l].astype(F32))

        u_pool, aq, akv, gdn, ba = _in_projection(x2d, mod, w_in.astype(F32), l, seq)
        y_pool = _pool_mixer(u_pool, pool_bd.astype(BF16), pool_scale[l].astype(F32), seq)
        y_att = _swa_attention(aq, akv, bias, attn_sinks[l].astype(F32), seq)
        y_gdn = _gdn_mixer(gdn, ba, conv_p, alog_v, dtb_v, nw_v, seq)
        x1, h2, info, cnt = _out_projection(x2d, mod, y_pool, y_att, y_gdn, w_out_p, ln1_g[l], ln1_b[l],
                                            rw_hi, rw_lo, rb, seq)
        x2d = _moe(h2, info, cnt, x1, mod, ln2_g[l], ln2_b[l], w_up_all, b_up_all, w_down_all, b_down_all,
                   l, seq)
    return x2d.reshape(bsz, seq, d)
```

```python
import functools

import numpy as np
import jax
import jax.numpy as jnp
from jax import lax
from jax.experimental import pallas as pl
from jax.experimental.pallas import tpu as pltpu
from jax.experimental.pallas import tpu_sc as plsc

F32 = jnp.float32
BF16 = jnp.bfloat16

D_MODEL = 1024
HEAD_DIM = 64
POOL_DIM = 256
POOL_WINDOWS = (2, 4, 8, 16)
POOL_GROUP = 64
N_ATT_HEADS = 6
N_KV_HEADS = 2
ATT_DIM = 384
KV_DIM = 128
WINDOW = 128
N_BUCKETS = 32
MAX_DISTANCE = 128
N_GDN_HEADS = 6
GDN_DIM = 384
CONV_WIDTH = 4
GDN_CHUNK = 64
N_EXPERTS = 32
TOP_K = 4
EXPERT_DIM = 1024
SWIGLU_ALPHA = 1.702
SWIGLU_LIMIT = 7.0
DEPTH = 2
DEEPNORM_ALPHA = (2 * DEPTH) ** 0.25
LN_EPS = 1e-5
NORM_EPS = 1e-6
NEG_INF = -1e30

LANES = 128
SUBLANES = 8
VMEM_LIMIT = 56 * 1024 * 1024

ROW_TILE = 512
IN_TILE = 1024
ATT_BLOCKS = 4
GDN_SUPER = 256
EXPERT_BLOCK = 512
SC_CHUNK = 64

_OFF_AQ = POOL_DIM
_OFF_AK = _OFF_AQ + ATT_DIM
_OFF_AV = _OFF_AK + KV_DIM
_OFF_GQ = _OFF_AV + KV_DIM
_OFF_GK = _OFF_GQ + GDN_DIM
_OFF_GV = _OFF_GK + GDN_DIM
_OFF_GZ = _OFF_GV + GDN_DIM
_OFF_GB = _OFF_GZ + GDN_DIM
_OFF_GA = _OFF_GB + N_GDN_HEADS
IN_DIM = _OFF_GA + N_GDN_HEADS

P_POOL = (0, POOL_DIM)
P_Q = (P_POOL[1], P_POOL[1] + ATT_DIM)
P_KV = (P_Q[1], P_Q[1] + 2 * KV_DIM)
P_GDN = (P_KV[1], P_KV[1] + 4 * GDN_DIM)
P_BA = (P_GDN[1], P_GDN[1] + LANES)
P_TOTAL = P_BA[1]


def _head_cols(off, h):
    return list(range(off + HEAD_DIM * h, off + HEAD_DIM * (h + 1)))


def _build_in_perm():
    cols = list(range(POOL_DIM))
    for p in range(N_ATT_HEADS // 2):
        cols += _head_cols(_OFF_AQ, p) + _head_cols(_OFF_AQ, p + 3)
    cols += list(range(_OFF_AK, _OFF_AK + 2 * KV_DIM))
    gdn_src = []
    for p in range(N_GDN_HEADS // 2):
        e, o = 2 * p, 2 * p + 1
        grp = (_head_cols(_OFF_GK, e) + _head_cols(_OFF_GQ, e)
               + _head_cols(_OFF_GQ, o) + _head_cols(_OFF_GK, o)
               + _head_cols(_OFF_GV, o) + _head_cols(_OFF_GV, e)
               + _head_cols(_OFF_GZ, o) + _head_cols(_OFF_GZ, e))
        cols += grp
        gdn_src += [c - _OFF_GQ if c < _OFF_GZ else -1 for c in grp]
    cols += list(range(_OFF_GB, _OFF_GB + 2 * N_GDN_HEADS))
    cols += [-1] * (LANES - 2 * N_GDN_HEADS)
    assert len(cols) == P_TOTAL
    return np.asarray(cols, np.int32), np.asarray(gdn_src, np.int32)


_IN_PERM, _GDN_CONV_SRC = _build_in_perm()


def _build_out_perm():
    rows = list(range(POOL_DIM))
    for p in range(N_ATT_HEADS // 2):
        rows += _head_cols(POOL_DIM, p) + _head_cols(POOL_DIM, p + 3)
    for p in range(N_GDN_HEADS // 2):
        rows += _head_cols(POOL_DIM + ATT_DIM, 2 * p + 1) + _head_cols(POOL_DIM + ATT_DIM, 2 * p)
    return np.asarray(rows, np.int32)


_OUT_PERM = _build_out_perm()


def _t5_bucket_line():
    n = np.maximum(2 * WINDOW - 1 - np.arange(3 * WINDOW - 1), 0)
    max_exact = N_BUCKETS // 2
    nf = np.maximum(n, 1).astype(np.float32)
    large = max_exact + (np.log(nf / max_exact) / np.float32(np.log(MAX_DISTANCE / max_exact))
                         * (N_BUCKETS - max_exact)).astype(np.int32)
    large = np.minimum(large, N_BUCKETS - 1)
    return np.where(n < max_exact, n, large).astype(np.int32)


_BUCKET_LINE = _t5_bucket_line()


def _band_bias(rel_bias):
    n_line = 3 * WINDOW - 1
    line = jnp.take(rel_bias.astype(F32), jnp.asarray(_BUCKET_LINE), axis=0).T
    heads = line.shape[0]
    padded = jnp.concatenate([line, jnp.zeros((heads, 1), F32)], axis=1)
    skew = jnp.tile(padded, (1, WINDOW))[:, :WINDOW * n_line].reshape(heads, WINDOW, n_line)
    return skew[:, :, WINDOW - 1:3 * WINDOW - 1]


def _take_static(w, perm, axis):
    parts = []
    start = 0
    for i in range(1, len(perm) + 1):
        run_ends = (i == len(perm) or ((perm[i] < 0) != (perm[i - 1] < 0))
                    or (perm[i] >= 0 and perm[i] != perm[i - 1] + 1))
        if run_ends:
            if perm[start] < 0:
                shape = list(w.shape)
                shape[axis] = i - start
                parts.append(jnp.zeros(shape, w.dtype))
            else:
                parts.append(lax.slice_in_dim(w, int(perm[start]), int(perm[start]) + (i - start), axis=axis))
            start = i
    return jnp.concatenate(parts, axis=axis)


def _take_cols(w, perm):
    return _take_static(w, perm, w.ndim - 1)


def _split_bf16(x):
    hi = x.astype(BF16)
    lo = (x - hi.astype(F32)).astype(BF16)
    return hi, lo


def _pack_bf16_pairs(x):
    n = x.shape[1] // 2
    bits = pltpu.bitcast(x.astype(BF16).astype(F32), jnp.int32)
    return lax.shift_right_logical(bits[:, :n], 16) | bits[:, n:]


def _unpack_bf16_pairs(u):
    lo = pltpu.bitcast(lax.shift_left(u, 16), F32)
    hi = pltpu.bitcast(u & jnp.int32(-65536), F32)
    return jnp.concatenate([lo, hi], axis=1)


def _dot(a, b):
    return jnp.dot(a, b, preferred_element_type=F32)


def _dot_nt(a, b):
    return lax.dot_general(a, b, (((1,), (1,)), ((), ())), preferred_element_type=F32)


def _dot_hi_exact_rhs(x, m_bf16):
    hi, lo = _split_bf16(x)
    return _dot(hi, m_bf16) + _dot(lo, m_bf16)


def _sigmoid(x):
    return 1.0 / (1.0 + jnp.exp(-x))


def _layer_norm(r, g, b):
    mu = jnp.mean(r, axis=-1, keepdims=True)
    d = r - mu
    var = jnp.mean(d * d, axis=-1, keepdims=True)
    return d * lax.rsqrt(var + LN_EPS) * g + b


def _cparams(sem):
    return pltpu.CompilerParams(dimension_semantics=sem, vmem_limit_bytes=VMEM_LIMIT)


def _mod_kernel(c_ref, w_ref, b_ref, o_ref):
    c = c_ref[...]
    ca = c * _sigmoid(c)
    ch, cl = _split_bf16(ca)
    wh, wl = _split_bf16(w_ref[0])
    o_ref[0] = _dot(ch, wh) + _dot(cl, wh) + _dot(ch, wl) + b_ref[0]


def _modulation(c, w_ada, b_ada):
    depth, d, n = w_ada.shape
    bsz = c.shape[0]
    tn = 512
    return pl.pallas_call(
        _mod_kernel,
        grid=(depth, n // tn),
        in_specs=[
            pl.BlockSpec((bsz, d), lambda l, j: (0, 0)),
            pl.BlockSpec((1, d, tn), lambda l, j: (l, 0, j)),
            pl.BlockSpec((1, 1, tn), lambda l, j: (l, 0, j)),
        ],
        out_specs=pl.BlockSpec((1, bsz, tn), lambda l, j: (l, 0, j)),
        out_shape=jax.ShapeDtypeStruct((depth, bsz, n), F32),
        compiler_params=_cparams(("arbitrary", "arbitrary")),
        name="adaln_mod",
    )(c, w_ada, b_ada.reshape(depth, 1, n))


def _perm_runs(perm):
    runs = []
    start = 0
    for i in range(1, len(perm) + 1):
        run_ends = (i == len(perm) or ((perm[i] < 0) != (perm[i - 1] < 0))
                    or (perm[i] >= 0 and perm[i] != perm[i - 1] + 1))
        if run_ends:
            runs.append((int(perm[start]) if perm[start] >= 0 else -1, i - start, start))
            start = i
    return runs


_IN_RUNS = _perm_runs(_IN_PERM)


def _inproj_kernel(layer, x_ref, mod_ref, w_hbm, pool_ref, q_ref, kv_ref, gdn_ref, ba_ref, w_f32, w_ref, sem):
    @pl.when(pl.program_id(0) == 0)
    def _():
        fetch = pltpu.make_async_copy(w_hbm.at[layer], w_f32, sem)
        fetch.start()
        fetch.wait()
        for src, n, dst in _IN_RUNS:
            if src < 0:
                w_ref[:, dst:dst + n] = jnp.zeros((w_ref.shape[0], n), BF16)
            else:
                w_ref[:, dst:dst + n] = w_f32[:, src:src + n].astype(BF16)
        q_cols = w_ref[:, P_Q[0]:P_Q[1]].astype(F32) * (HEAD_DIM ** -0.5)
        w_ref[:, P_Q[0]:P_Q[1]] = q_cols.astype(BF16)

    sh = mod_ref[0, 0:1, :]
    sc = mod_ref[0, 1:2, :]
    h = (x_ref[...] * (1.0 + sc) + sh).astype(BF16)

    def mm(rng):
        return _dot(h, w_ref[:, rng[0]:rng[1]])

    pool_ref[...] = mm(P_POOL)
    q_ref[...] = mm(P_Q).astype(BF16)
    kv_ref[...] = mm(P_KV).astype(BF16)
    gdn_ref[...] = mm(P_GDN)
    ba_ref[...] = mm(P_BA)


def _in_projection(x2d, mod, w_in, layer, seq):
    t, d = x2d.shape
    tm = min(IN_TILE, seq)
    widths = [r[1] - r[0] for r in (P_POOL, P_Q, P_KV, P_GDN, P_BA)]
    dtypes = [F32, BF16, BF16, F32, F32]
    return pl.pallas_call(
        functools.partial(_inproj_kernel, layer),
        grid=(t // tm,),
        in_specs=[
            pl.BlockSpec((tm, d), lambda i: (i, 0)),
            pl.BlockSpec((1, 6, d), lambda i: ((i * tm) // seq, 0, 0)),
            pl.BlockSpec(memory_space=pl.ANY),
        ],
        out_specs=[pl.BlockSpec((tm, w), lambda i: (i, 0)) for w in widths],
        out_shape=[jax.ShapeDtypeStruct((t, w), dt) for w, dt in zip(widths, dtypes)],
        scratch_shapes=[pltpu.VMEM((d, w_in.shape[2]), F32), pltpu.VMEM((d, P_TOTAL), BF16),
                        pltpu.SemaphoreType.DMA(())],
        compiler_params=_cparams(("arbitrary",)),
        name="in_proj",
    )(x2d, mod, w_in)


def _pool_kernel(u_ref, w_ref, scale_ref, o_ref):
    u = u_ref[...]
    row = lax.broadcasted_iota(jnp.int32, u.shape, 0)
    lane = lax.broadcasted_iota(jnp.int32, u.shape, 1)

    def shifted(a, s):
        return jnp.where(row >= s, pltpu.roll(a, s, axis=0), 0.0)

    sums = []
    acc = u
    for wdt in POOL_WINDOWS:
        acc = acc + shifted(acc, wdt // 2)
        sums.append(acc)
    grp = lane // POOL_GROUP
    wsum = sums[-1]
    win = jnp.full(u.shape, POOL_WINDOWS[-1], jnp.int32)
    for gi in range(len(POOL_WINDOWS) - 2, -1, -1):
        wsum = jnp.where(grp == gi, sums[gi], wsum)
        win = jnp.where(grp == gi, POOL_WINDOWS[gi], win)
    cnt = jnp.minimum(row + 1, win).astype(F32)
    p = wsum / cnt - u
    y = _dot(p.astype(BF16), w_ref[...]) * scale_ref[...]
    o_ref[...] = y.astype(BF16)


def _pool_mixer(u, pool_w_bd, pool_scale, seq):
    t, c = u.shape
    return pl.pallas_call(
        _pool_kernel,
        grid=(t // seq,),
        in_specs=[
            pl.BlockSpec((seq, c), lambda b: (b, 0)),
            pl.BlockSpec((c, c), lambda b: (0, 0)),
            pl.BlockSpec((1, c), lambda b: (0, 0)),
        ],
        out_specs=pl.BlockSpec((seq, c), lambda b: (b, 0)),
        out_shape=jax.ShapeDtypeStruct((t, c), BF16),
        compiler_params=_cparams(("arbitrary",)),
        name="pool_mixer",
    )(u, pool_w_bd, pool_scale.reshape(1, c))


def _attn_kernel(sink_ref, q_ref, kvc_ref, kvp_ref, bias_ref, o_ref):
    step = pl.program_id(1)
    qi = lax.broadcasted_iota(jnp.int32, (WINDOW, 2 * WINDOW), 0)
    kj = lax.broadcasted_iota(jnp.int32, (WINDOW, 2 * WINDOW), 1)
    dist = qi + WINDOW - kj
    in_band = (dist >= 0) & (dist < WINDOW)
    lo = lax.broadcasted_iota(jnp.int32, (WINDOW, LANES), 1) < HEAD_DIM
    for sub in range(ATT_BLOCKS):
        r0 = sub * WINDOW
        prev = kvp_ref[...] if sub == 0 else kvc_ref[r0 - WINDOW:r0, :]
        kv = jnp.concatenate([prev, kvc_ref[r0:r0 + WINDOW, :]], axis=0)
        k = kv[:, :KV_DIM]
        v = kv[:, KV_DIM:]
        valid = in_band & ((kj >= WINDOW) | (step > 0)) if sub == 0 else in_band
        for p in range(N_ATT_HEADS // 2):
            qp = q_ref[r0:r0 + WINDOW, p * LANES:(p + 1) * LANES]
            halves = []
            for half in range(2):
                h = p + 3 * half
                qm = jnp.where(lo if half == 0 else jnp.logical_not(lo), qp, jnp.zeros_like(qp))
                s = jnp.where(valid, _dot_nt(qm, k) + bias_ref[h], NEG_INF)
                sink = sink_ref[h]
                m = jnp.maximum(jnp.max(s, axis=-1, keepdims=True), sink)
                pr = jnp.exp(s - m)
                den = jnp.sum(pr, axis=-1, keepdims=True) + jnp.exp(sink - m)
                halves.append(_dot(pr.astype(BF16), v) / den)
            o_ref[r0:r0 + WINDOW, p * LANES:(p + 1) * LANES] = (
                jnp.where(lo, halves[0], halves[1]).astype(BF16))


def _swa_attention(q, kv, bias, sinks, seq):
    t = q.shape[0]
    rows = ATT_BLOCKS * WINDOW
    nblk = seq // rows
    return pl.pallas_call(
        _attn_kernel,
        grid=(t // seq, nblk),
        in_specs=[
            pl.BlockSpec(memory_space=pltpu.SMEM),
            pl.BlockSpec((rows, ATT_DIM), lambda b, n: (b * nblk + n, 0)),
            pl.BlockSpec((rows, 2 * KV_DIM), lambda b, n: (b * nblk + n, 0)),
            pl.BlockSpec((WINDOW, 2 * KV_DIM),
                         lambda b, n: (jnp.maximum((b * nblk + n) * ATT_BLOCKS - 1, 0), 0)),
            pl.BlockSpec((N_ATT_HEADS, WINDOW, 2 * WINDOW), lambda b, n: (0, 0, 0)),
        ],
        out_specs=pl.BlockSpec((rows, ATT_DIM), lambda b, n: (b * nblk + n, 0)),
        out_shape=jax.ShapeDtypeStruct((t, ATT_DIM), BF16),
        compiler_params=_cparams(("arbitrary", "arbitrary")),
        name="swa_attention",
    )(sinks, q, kv, kv, bias)


_GDN_BASE = SUBLANES
_GDN_LEVELS = int(np.log2(GDN_CHUNK // _GDN_BASE))


def _gdn_masks():
    r = np.arange(GDN_SUPER)
    ri, ci = r[:, None], r[None, :]
    same_chunk = (ri // GDN_CHUNK) == (ci // GDN_CHUNK)
    incl = same_chunk & (ri >= ci)
    planes = [incl, ri == ci]
    base = ((ri // _GDN_BASE) == (ci // _GDN_BASE)) & (ri > ci)
    planes.append(base)
    for lvl in range(_GDN_LEVELS):
        small = _GDN_BASE << lvl
        planes.append(((ri // (2 * small)) == (ci // (2 * small))) & ((ri // small) != (ci // small)) & (ri > ci))
    bmask = np.stack(planes).astype(np.float32)
    bmask[2] = -bmask[2]
    negmask = np.where(incl, 0.0, -np.inf).astype(np.float32)
    return negmask, bmask


_GDN_NEGMASK, _GDN_BMASK = _gdn_masks()


def _gdn_kernel(x_ref, halo_ref, ba_ref, cw_ref, alog_ref, dtb_ref, nw_ref, negmask_ref, bmask_ref,
                y_ref, state_ref, xs_ref):
    sc_id = pl.program_id(1)
    rows = GDN_SUPER
    nchunk = rows // GDN_CHUNK
    c_sz = GDN_CHUNK

    @pl.when(sc_id == 0)
    def _():
        state_ref[...] = jnp.zeros_like(state_ref)

    xs_ref[:SUBLANES, :] = jnp.where(sc_id == 0, 0.0, halo_ref[...])
    xs_ref[SUBLANES:, :] = x_ref[...]
    act = []
    for g in range(x_ref.shape[1] // LANES):
        cols = slice(g * LANES, (g + 1) * LANES)
        acc = x_ref[:, cols] * cw_ref[CONV_WIDTH - 1:CONV_WIDTH, cols]
        if g % 4 != 3:
            for s in range(1, CONV_WIDTH):
                acc = acc + (xs_ref[SUBLANES - s:SUBLANES - s + rows, cols]
                             * cw_ref[CONV_WIDTH - 1 - s:CONV_WIDTH - s, cols])
        act.append(acc * _sigmoid(acc))

    negmask = negmask_ref[...]
    tri_incl = bmask_ref[0]
    eye_b = bmask_ref[1]
    base_neg = bmask_ref[2]
    bands = [bmask_ref[3 + lvl] for lvl in range(_GDN_LEVELS)]
    li = lax.broadcasted_iota(jnp.int32, (LANES, LANES), 0)
    lj = lax.broadcasted_iota(jnp.int32, (LANES, LANES), 1)
    half_ones = jnp.where((li // HEAD_DIM) == (lj // HEAD_DIM), 1.0, 0.0).astype(BF16)
    lane_lo = lax.broadcasted_iota(jnp.int32, (rows, LANES), 1) < HEAD_DIM
    lane_lo_c = lax.broadcasted_iota(jnp.int32, (c_sz, LANES), 1) < HEAD_DIM

    ba = ba_ref[...]
    beta_all = _sigmoid(ba)
    sp_in = ba + dtb_ref[...]
    softplus = jnp.maximum(sp_in, 0.0) + jnp.log(1.0 + jnp.exp(-jnp.abs(sp_in)))
    g_all = -jnp.exp(alog_ref[...]) * softplus
    gcum = _dot_hi_exact_rhs_lhs(tri_incl, g_all)
    gcum_t = gcum.T

    heads = range(N_GDN_HEADS)
    lane_hi = jnp.logical_not(lane_lo)
    lane_hi_c = jnp.logical_not(lane_lo_c)
    mk = [lane_lo if h % 2 == 0 else lane_hi for h in heads]
    mk_c = [lane_lo_c if h % 2 == 0 else lane_hi_c for h in heads]
    scale = HEAD_DIM ** -0.5

    def bdot(a, b):
        return _dot(a.astype(BF16), b.astype(BF16))

    xk, xq, gn, gc_col, beta, eg = [], [], [], [], [], []
    for h in heads:
        g = act[4 * (h // 2) + (h % 2)]
        g = g * lax.rsqrt(_dot_hi_exact_rhs(g * g, half_ones) + NORM_EPS)
        gn.append(g)
        xk.append(jnp.where(mk[h], g, 0.0))
        xq.append(jnp.where(mk[h], pltpu.roll(g, HEAD_DIM, axis=1), 0.0) * scale)
        beta.append(beta_all[:, h:h + 1])
        gc_col.append(gcum[:, N_GDN_HEADS + h:N_GDN_HEADS + h + 1])
        eg.append(jnp.exp(gc_col[h]))

    l_b, attn, rhs = [], [], []
    for h in heads:
        gc_row = gcum_t[N_GDN_HEADS + h:N_GDN_HEADS + h + 1, :]
        decay = jnp.exp(gc_col[h] - gc_row + negmask)
        xk_b = xk[h].astype(BF16)
        kk = _dot_nt((xk[h] * beta[h]).astype(BF16), xk_b)
        l_b.append((kk * decay).astype(BF16))
        attn.append((_dot_nt(xq[h].astype(BF16), xk_b) * decay).astype(BF16))
        vv = act[4 * (h // 2) + 2]
        rhs.append(jnp.where(mk[h], gn[h] * eg[h], vv) * beta[h])

    a1 = [l_b[h] * base_neg for h in heads]
    a2 = [_dot(a1[h], a1[h]).astype(BF16) for h in heads]
    a4 = [_dot(a2[h], a2[h]).astype(BF16) for h in heads]
    inv0 = [eye_b + a1[h] for h in heads]
    acc1 = [inv0[h].astype(F32) + _dot(a2[h], inv0[h]) for h in heads]
    inv_b = [(acc1[h] + _dot(a4[h], acc1[h].astype(BF16))).astype(BF16) for h in heads]
    for lvl in range(_GDN_LEVELS - 1):
        mid = [_dot(l_b[h] * bands[lvl], inv_b[h]).astype(BF16) for h in heads]
        inv_b = [inv_b[h] - _dot(inv_b[h], mid[h]).astype(BF16) for h in heads]
    half = [_dot(inv_b[h], rhs[h].astype(BF16)) for h in heads]
    mid = [_dot(l_b[h] * bands[_GDN_LEVELS - 1], half[h].astype(BF16)) for h in heads]
    sol = [half[h] - _dot(inv_b[h], mid[h].astype(BF16)) for h in heads]

    lane_lo_s = lax.broadcasted_iota(jnp.int32, (LANES, LANES), 1) < HEAD_DIM
    mk_s = [lane_lo_s if h % 2 == 0 else jnp.logical_not(lane_lo_s) for h in heads]
    sol_b = [sol[h].astype(BF16) for h in heads]
    attn_sol = [_dot(attn[h], sol_b[h]) for h in heads]
    q_eff = [(xq[h] * eg[h] - jnp.where(mk[h], attn_sol[h], 0.0)).astype(BF16) for h in heads]
    o_free = [jnp.where(mk[h], 0.0, attn_sol[h]) for h in heads]
    kw = [[] for _ in heads]
    ku = [[] for _ in heads]
    cdec = [[] for _ in heads]
    for c in range(nchunk):
        r0 = c * c_sz
        for h in heads:
            glast = gcum[r0 + c_sz - 1:r0 + c_sz, N_GDN_HEADS + h:N_GDN_HEADS + h + 1]
            kd_t = (xk[h][r0:r0 + c_sz] * jnp.exp(glast - gc_col[h][r0:r0 + c_sz])).T
            both = _dot(kd_t.astype(BF16), sol_b[h][r0:r0 + c_sz])
            kw[h].append(jnp.where(mk_s[h], both, 0.0).astype(BF16))
            ku[h].append(jnp.where(mk_s[h], 0.0, both))
            cdec[h].append(jnp.exp(glast))
    st = [state_ref[h] for h in heads]
    o_parts = [[] for _ in heads]
    for c in range(nchunk):
        r0 = c * c_sz
        for h in heads:
            lhs = jnp.concatenate([kw[h][c], q_eff[h][r0:r0 + c_sz]], axis=0)
            prod = _dot(lhs, st[h].astype(BF16))
            o_parts[h].append(prod[LANES:] + o_free[h][r0:r0 + c_sz])
            st[h] = st[h] * cdec[h][c] + ku[h][c] - prod[:LANES]
    for h in heads:
        state_ref[h] = st[h]

    for p in range(N_GDN_HEADS // 2):
        o_pair = [jnp.concatenate(o_parts[h], axis=0) for h in (2 * p, 2 * p + 1)]
        o = jnp.where(lane_lo, o_pair[1], o_pair[0])
        ms = _dot_hi_exact_rhs(o * o, half_ones) * (1.0 / HEAD_DIM)
        zz = act[4 * p + 3]
        y = o * lax.rsqrt(ms + NORM_EPS) * nw_ref[...] * zz
        y_ref[:, p * LANES:(p + 1) * LANES] = y.astype(BF16)


def _dot_hi_exact_rhs_lhs(m_bf16, x):
    hi, lo = _split_bf16(x)
    return _dot(m_bf16, hi) + _dot(m_bf16, lo)


def _gdn_mixer(gdn, ba, conv_p, alog_v, dtb_v, nw_v, seq):
    t, c = gdn.shape
    rows = GDN_SUPER
    nsc = seq // rows
    hb = rows // SUBLANES
    return pl.pallas_call(
        _gdn_kernel,
        grid=(t // seq, nsc),
        in_specs=[
            pl.BlockSpec((rows, c), lambda b, s: (b * nsc + s, 0)),
            pl.BlockSpec((SUBLANES, c), lambda b, s: (jnp.maximum((b * nsc + s) * hb - 1, 0), 0)),
            pl.BlockSpec((rows, LANES), lambda b, s: (b * nsc + s, 0)),
            pl.BlockSpec((CONV_WIDTH, c), lambda b, s: (0, 0)),
            pl.BlockSpec((1, LANES), lambda b, s: (0, 0)),
            pl.BlockSpec((1, LANES), lambda b, s: (0, 0)),
            pl.BlockSpec((1, LANES), lambda b, s: (0, 0)),
            pl.BlockSpec((rows, rows), lambda b, s: (0, 0)),
            pl.BlockSpec((3 + _GDN_LEVELS, rows, rows), lambda b, s: (0, 0, 0)),
        ],
        out_specs=pl.BlockSpec((rows, GDN_DIM), lambda b, s: (b * nsc + s, 0)),
        out_shape=jax.ShapeDtypeStruct((t, GDN_DIM), BF16),
        scratch_shapes=[pltpu.VMEM((N_GDN_HEADS, LANES, LANES), F32), pltpu.VMEM((rows + SUBLANES, c), F32)],
        compiler_params=_cparams(("arbitrary", "arbitrary")),
        name="gdn_mixer",
    )(gdn, gdn, ba, conv_p, alog_v, dtb_v, nw_v, jnp.asarray(_GDN_NEGMASK), jnp.asarray(_GDN_BMASK, BF16))


def _route_tile(logits, before, carry_ref, live):
    shape = logits.shape
    lane = lax.broadcasted_iota(jnp.int32, shape, 1).astype(F32)
    work = logits
    vals, idxs = [], []
    for _k in range(TOP_K):
        m = jnp.max(work, axis=-1, keepdims=True)
        idx = jnp.min(jnp.where(work == m, lane, float(LANES)), axis=-1, keepdims=True)
        vals.append(m)
        idxs.append(idx)
        work = jnp.where(lane == idx, -jnp.inf, work)
    exps = [jnp.exp(v - vals[0]) for v in vals]
    den = exps[0] + exps[1] + exps[2] + exps[3]
    onehots = [lane == idx for idx in idxs]
    member = jnp.zeros(shape, F32)
    for oh in onehots:
        member = member + jnp.where(oh, 1.0, 0.0)
    rank = _dot(before, member.astype(BF16)) + carry_ref[...]
    carry_ref[...] = carry_ref[...] + live * jnp.sum(member, axis=0, keepdims=True)
    info = jnp.zeros(shape, F32)
    for k in range(TOP_K):
        rank_k = jnp.sum(jnp.where(onehots[k], rank, 0.0), axis=-1, keepdims=True)
        info = jnp.where(lane == float(k), idxs[k], info)
        info = jnp.where(lane == float(TOP_K + k), rank_k, info)
        info = jnp.where(lane == float(2 * TOP_K + k), exps[k] / den, info)
    return info


def _outproj_kernel(x_ref, mod_ref, yp_ref, ya_ref, yg_ref, wp_ref, wa_ref, wg_ref, lng_ref, lnb_ref,
                    rwh_ref, rwl_ref, rb_ref, before_ref, x1_ref, h2_ref, info_ref, er_ref, cnt_ref,
                    carry_ref, logit_s):
    step = pl.program_id(0)

    @pl.when(step == 0)
    def _():
        carry_ref[...] = jnp.zeros_like(carry_ref)
        logit_s[...] = jnp.zeros_like(logit_s)

    y = _dot(yp_ref[...], wp_ref[...]) + _dot(ya_ref[...], wa_ref[...]) + _dot(yg_ref[...], wg_ref[...])
    live = jnp.where(step > 0, 1.0, 0.0)
    info = _route_tile(logit_s[...], before_ref[...], carry_ref, live)
    info_ref[...] = info
    er_ref[...] = info.T[:SUBLANES]
    cnt_ref[...] = carry_ref[...]
    g1 = mod_ref[0, 2:3, :]
    sh2 = mod_ref[0, 3:4, :]
    sc2 = mod_ref[0, 4:5, :]
    x1 = _layer_norm(DEEPNORM_ALPHA * x_ref[...] + g1 * y, lng_ref[...], lnb_ref[...])
    x1_ref[...] = x1
    h2 = x1 * (1.0 + sc2) + sh2
    hh, hl = _split_bf16(h2)
    h2_ref[...] = _pack_bf16_pairs(h2)
    logit_s[...] = _dot(hh, rwh_ref[...]) + _dot(hl, rwh_ref[...]) + _dot(hh, rwl_ref[...]) + rb_ref[...]


def _out_projection(x2d, mod, yp, ya, yg, w_out_p, ln_g, ln_b, rw_hi, rw_lo, rb, seq):
    t, d = x2d.shape
    tm = ROW_TILE
    wp = w_out_p[:POOL_DIM]
    wa = w_out_p[POOL_DIM:POOL_DIM + ATT_DIM]
    wg = w_out_p[POOL_DIM + ATT_DIM:]
    last = t // tm - 1
    row = lambda i: (jnp.minimum(i, last), 0)
    routed = lambda i: (jnp.maximum(i - 1, 0), 0)
    fixed = lambda i: (0, 0)
    return pl.pallas_call(
        _outproj_kernel,
        grid=(t // tm + 1,),
        in_specs=[
            pl.BlockSpec((tm, d), row),
            pl.BlockSpec((1, 6, d), lambda i: ((jnp.minimum(i, last) * tm) // seq, 0, 0)),
            pl.BlockSpec((tm, POOL_DIM), row),
            pl.BlockSpec((tm, ATT_DIM), row),
            pl.BlockSpec((tm, GDN_DIM), row),
            pl.BlockSpec((POOL_DIM, d), fixed),
            pl.BlockSpec((ATT_DIM, d), fixed),
            pl.BlockSpec((GDN_DIM, d), fixed),
            pl.BlockSpec((1, d), fixed),
            pl.BlockSpec((1, d), fixed),
            pl.BlockSpec((d, LANES), fixed),
            pl.BlockSpec((d, LANES), fixed),
            pl.BlockSpec((1, LANES), fixed),
            pl.BlockSpec((tm, tm), fixed),
        ],
        out_specs=[pl.BlockSpec((tm, d), row), pl.BlockSpec((tm, d // 2), row), pl.BlockSpec((tm, LANES), routed),
                   pl.BlockSpec((SUBLANES, tm), lambda i: (0, jnp.maximum(i - 1, 0))),
                   pl.BlockSpec((1, LANES), fixed)],
        out_shape=[jax.ShapeDtypeStruct((t, d), F32), jax.ShapeDtypeStruct((t, d // 2), jnp.int32),
                   jax.ShapeDtypeStruct((t, LANES), F32), jax.ShapeDtypeStruct((SUBLANES, t), F32),
                   jax.ShapeDtypeStruct((1, LANES), F32)],
        scratch_shapes=[pltpu.VMEM((1, LANES), F32), pltpu.VMEM((tm, LANES), F32)],
        compiler_params=_cparams(("arbitrary",)),
        name="out_proj_ln_route",
    )(x2d, mod, yp, ya, yg, wp, wa, wg, ln_g.reshape(1, d), ln_b.reshape(1, d), rw_hi, rw_lo, rb,
      jnp.tril(jnp.ones((tm, tm), BF16), -1))


def _slot_kernel(er_ref, cnt_ref, dest_ref, pcum_ref):
    cnt = jnp.broadcast_to(cnt_ref[...], (SUBLANES, LANES))
    padded = jnp.floor((cnt + float(EXPERT_BLOCK - 1)) * (1.0 / EXPERT_BLOCK)) * float(EXPERT_BLOCK)
    lane8 = lax.broadcasted_iota(jnp.int32, (SUBLANES, LANES), 1)
    acc = padded
    step = 1
    while step < LANES:
        acc = acc + jnp.where(lane8 >= step, pltpu.roll(acc, step, axis=1), 0.0)
        step *= 2
    pcum_ref[...] = acc[:1].astype(jnp.int32)
    pstart = acc - padded

    er = er_ref[...]
    start = jnp.zeros(er.shape, F32)
    for e in range(N_EXPERTS):
        offset = jnp.sum(jnp.where(lane8 == e, pstart, 0.0), axis=-1, keepdims=True)
        start = jnp.where(er == float(e), offset, start)
    row = lax.broadcasted_iota(jnp.int32, er.shape, 0)
    slots = jnp.where(row < TOP_K, start + pltpu.roll(er, TOP_K, axis=0), 0.0)
    dest_ref[...] = slots.astype(jnp.int32)


def _slots(er, cnt):
    t = er.shape[1]
    return pl.pallas_call(
        _slot_kernel,
        grid=(1,),
        in_specs=[pl.BlockSpec((SUBLANES, t), lambda i: (0, 0)), pl.BlockSpec((1, LANES), lambda i: (0, 0))],
        out_specs=[pl.BlockSpec((SUBLANES, t), lambda i: (0, 0)), pl.BlockSpec((1, LANES), lambda i: (0, 0))],
        out_shape=[jax.ShapeDtypeStruct((SUBLANES, t), jnp.int32), jax.ShapeDtypeStruct((1, LANES), jnp.int32)],
        compiler_params=_cparams(("arbitrary",)),
        name="moe_slots",
    )(er, cnt)


def _expert_kernel(e0, be_ref, nxt_ref, val_ref, nu_ref, x_ref, wup_hbm, bup_ref, wdn_hbm, bdn_ref, y_ref,
                   wup_st, wdn_st, wup_bf, wdn_bf, sems):
    i = pl.program_id(0)
    e = be_ref[i]
    prev = be_ref[jnp.maximum(i - 1, 0)]
    used = i < nu_ref[0]

    def weight_copies(expert):
        return (pltpu.make_async_copy(wup_hbm.at[e0 + expert], wup_st, sems.at[0]),
                pltpu.make_async_copy(wdn_hbm.at[e0 + expert], wdn_st, sems.at[1]))

    @pl.when(i == 0)
    def _():
        for cp in weight_copies(e):
            cp.start()

    @pl.when(used & ((i == 0) | (e != prev)))
    def _():
        for cp in weight_copies(e):
            cp.wait()
        wup_bf[...] = wup_st[...].astype(BF16)
        wdn_bf[...] = wdn_st[...].astype(BF16)

        @pl.when(nxt_ref[i] >= 0)
        def _():
            for cp in weight_copies(nxt_ref[i]):
                cp.start()

    def ffn(rows):
        xb = _unpack_bf16_pairs(x_ref[:rows, :]).astype(BF16)
        hb = _dot(xb, wup_bf[...]) + bup_ref[0]
        x_glu = jnp.minimum(hb[:, :EXPERT_DIM], SWIGLU_LIMIT)
        x_lin = jnp.clip(hb[:, EXPERT_DIM:], -SWIGLU_LIMIT, SWIGLU_LIMIT)
        act = x_glu * _sigmoid(SWIGLU_ALPHA * x_glu) * (x_lin + 1.0)
        y = _dot(act.astype(BF16), wdn_bf[...]) + bdn_ref[0]
        y_ref[:rows, :] = _pack_bf16_pairs(y)

    half_rows = x_ref.shape[0] // 2
    small = val_ref[i] <= half_rows

    @pl.when(used & jnp.logical_not(small))
    def _():
        ffn(x_ref.shape[0])

    @pl.when(used & small)
    def _():
        ffn(half_rows)
        y_ref[half_rows:, :] = jnp.zeros((x_ref.shape[0] - half_rows, y_ref.shape[1]), y_ref.dtype)

    @pl.when(i >= nu_ref[0])
    def _():
        y_ref[...] = jnp.zeros_like(y_ref)


def _expert_ffn(xbuf, block_e, next_e, valid, n_used, w_up, b_up, w_down, b_down, layer):
    p, dh = xbuf.shape
    d = 2 * dh
    bm = EXPERT_BLOCK
    ne, _, n_up = w_up.shape
    e0 = layer * N_EXPERTS
    grid_spec = pltpu.PrefetchScalarGridSpec(
        num_scalar_prefetch=4,
        grid=(p // bm,),
        in_specs=[
            pl.BlockSpec((bm, dh), lambda i, be, nx, vl, nu: (i, 0)),
            pl.BlockSpec(memory_space=pl.ANY),
            pl.BlockSpec((1, 1, n_up), lambda i, be, nx, vl, nu: (e0 + be[i], 0, 0)),
            pl.BlockSpec(memory_space=pl.ANY),
            pl.BlockSpec((1, 1, d), lambda i, be, nx, vl, nu: (e0 + be[i], 0, 0)),
        ],
        out_specs=pl.BlockSpec((bm, dh), lambda i, be, nx, vl, nu: (i, 0)),
        scratch_shapes=[pltpu.VMEM((d, n_up), F32), pltpu.VMEM((EXPERT_DIM, d), F32),
                        pltpu.VMEM((d, n_up), BF16), pltpu.VMEM((EXPERT_DIM, d), BF16),
                        pltpu.SemaphoreType.DMA((2,))],
    )
    return pl.pallas_call(
        functools.partial(_expert_kernel, e0),
        grid_spec=grid_spec,
        out_shape=jax.ShapeDtypeStruct((p, dh), jnp.int32),
        compiler_params=_cparams(("arbitrary",)),
        name="expert_ffn",
    )(block_e, next_e, valid, n_used, xbuf, w_up, b_up, w_down, b_down)


def _combine_kernel(x1_ref, mod_ref, yg_ref, info_ref, lng_ref, lnb_ref, o_ref):
    info = info_ref[...]
    y = jnp.zeros(x1_ref.shape, F32)
    for k in range(TOP_K):
        gate = info[:, 2 * TOP_K + k:2 * TOP_K + k + 1]
        y = y + gate * _unpack_bf16_pairs(yg_ref[k])
    g2 = mod_ref[0, 5:6, :]
    o_ref[...] = _layer_norm(DEEPNORM_ALPHA * x1_ref[...] + g2 * y, lng_ref[...], lnb_ref[...])


def _combine(x1, mod, yg, info, ln_g, ln_b, seq):
    t, d = x1.shape
    tm = ROW_TILE
    row = lambda i: (i, 0)
    fixed = lambda i: (0, 0)
    return pl.pallas_call(
        _combine_kernel,
        grid=(t // tm,),
        in_specs=[
            pl.BlockSpec((tm, d), row),
            pl.BlockSpec((1, 6, d), lambda i: ((i * tm) // seq, 0, 0)),
            pl.BlockSpec((TOP_K, tm, d // 2), lambda i: (0, i, 0)),
            pl.BlockSpec((tm, LANES), row),
            pl.BlockSpec((1, d), fixed),
            pl.BlockSpec((1, d), fixed),
        ],
        out_specs=pl.BlockSpec((tm, d), row),
        out_shape=jax.ShapeDtypeStruct((t, d), F32),
        compiler_params=_cparams(("arbitrary",)),
        name="moe_combine_ln",
    )(x1, mod, yg, info, ln_g.reshape(1, d), ln_b.reshape(1, d))


def _sc_workers():
    info = plsc.get_sparse_core_info()
    return info.num_cores, info.num_cores * info.num_subcores


def _sc_scatter_rows(rows, idx, n_out):
    t, w = rows.shape
    kk = idx.shape[0]
    n_cores, n_workers = _sc_workers()
    ch = SC_CHUNK
    assert t % (2 * n_workers * ch) == 0
    n_chunk = t // (n_workers * ch)
    idx_c = jnp.transpose(idx.reshape(kk, t // ch, ch), (1, 0, 2))

    @functools.partial(
        pl.kernel,
        mesh=plsc.VectorSubcoreMesh(core_axis_name="c", subcore_axis_name="s"),
        out_type=jax.ShapeDtypeStruct((n_out, w), rows.dtype),
        scratch_types=[pltpu.VMEM((2, kk, ch), jnp.int32), pltpu.VMEM((2, ch, w), rows.dtype),
                       pltpu.SemaphoreType.DMA((2,)), pltpu.SemaphoreType.DMA((2,))],
        name="sc_dispatch_scatter",
    )
    def scatter_kernel(rows_hbm, idx_hbm, out_hbm, idx_v, rows_v, load_sem, scat_sem):
        base = (lax.axis_index("s") * n_cores + lax.axis_index("c")) * n_chunk

        def load(j, b):
            return pltpu.make_async_copy(rows_hbm.at[pl.ds((base + j) * ch, ch)], rows_v.at[b], load_sem.at[b])

        def scatters(b):
            return [pltpu.make_async_copy(rows_v.at[b], out_hbm.at[idx_v.at[b, q]], scat_sem.at[b])
                    for q in range(kk)]

        pltpu.sync_copy(idx_hbm.at[base], idx_v.at[0])
        load(0, 0).start()

        @pl.loop(0, n_chunk, step=2)
        def _(j0):
            for b in range(2):
                j = j0 + b
                other = 1 - b

                @pl.when(j >= 1)
                def _():
                    for cp in scatters(other):
                        cp.wait()

                @pl.when(j + 1 < n_chunk)
                def _():
                    pltpu.sync_copy(idx_hbm.at[base + j + 1], idx_v.at[other])
                    load(j + 1, other).start()

                load(j, b).wait()
                for cp in scatters(b):
                    cp.start()

        for cp in scatters((n_chunk - 1) % 2):
            cp.wait()

    return scatter_kernel(rows, idx_c)


def _sc_gather_rows(table, idx):
    m = idx.shape[0]
    w = table.shape[1]
    n_cores, n_workers = _sc_workers()
    ch = SC_CHUNK
    assert m % (2 * n_workers * ch) == 0
    n_chunk = m // (n_workers * ch)
    idx_c = idx.reshape(m // ch, 1, ch)

    @functools.partial(
        pl.kernel,
        mesh=plsc.VectorSubcoreMesh(core_axis_name="c", subcore_axis_name="s"),
        out_type=jax.ShapeDtypeStruct((m, w), table.dtype),
        scratch_types=[pltpu.VMEM((2, 1, ch), jnp.int32), pltpu.VMEM((2, ch, w), table.dtype),
                       pltpu.SemaphoreType.DMA((2,)), pltpu.SemaphoreType.DMA((2,))],
        name="sc_combine_gather",
    )
    def gather_kernel(table_hbm, idx_hbm, out_hbm, idx_v, rows_v, gather_sem, write_sem):
        base = (lax.axis_index("s") * n_cores + lax.axis_index("c")) * n_chunk

        def gather(b):
            return pltpu.make_async_copy(table_hbm.at[idx_v.at[b, 0]], rows_v.at[b], gather_sem.at[b])

        def write(j, b):
            return pltpu.make_async_copy(rows_v.at[b], out_hbm.at[pl.ds((base + j) * ch, ch)], write_sem.at[b])

        pltpu.sync_copy(idx_hbm.at[base], idx_v.at[0])
        gather(0).start()

        @pl.loop(0, n_chunk, step=2)
        def _(j0):
            for b in range(2):
                j = j0 + b
                other = 1 - b

                @pl.when(j >= 1)
                def _():
                    write(j - 1, other).wait()

                @pl.when(j + 1 < n_chunk)
                def _():
                    pltpu.sync_copy(idx_hbm.at[base + j + 1], idx_v.at[other])
                    gather(other).start()

                gather(b).wait()
                write(j, b).start()

        write(n_chunk - 1, (n_chunk - 1) % 2).wait()

    return gather_kernel(table, idx_c)


def _lane_vector(vals, offset):
    return jnp.zeros((1, LANES), F32).at[0, offset:offset + vals.shape[0]].set(vals.astype(F32))


def _moe(h2, info, er, cnt, x1, mod, ln_g, ln_b, w_up, b_up, w_down, b_down, layer, seq):
    t, dh = h2.shape
    a = t * TOP_K
    bm = EXPERT_BLOCK
    slots, pcum_v = _slots(er, cnt)
    pcum = pcum_v[0, :N_EXPERTS]
    dest = slots[:TOP_K]
    n_blocks = -(-a // bm) + N_EXPERTS
    starts = jnp.arange(n_blocks, dtype=jnp.int32) * bm
    block_e = jnp.minimum(jnp.sum(pcum[None, :] <= starts[:, None], axis=1), N_EXPERTS - 1).astype(jnp.int32)
    n_used = (pcum[-1] // bm).astype(jnp.int32).reshape(1)
    later = block_e[None, :] > block_e[:, None]
    group_end = n_blocks - jnp.sum(later, axis=1)
    next_e = jnp.min(jnp.where(later, block_e[None, :], N_EXPERTS), axis=1)
    next_e = jnp.where(group_end < n_used[0], next_e, -1).astype(jnp.int32)
    counts = cnt[0, :N_EXPERTS].astype(jnp.int32)
    pstart = pcum - ((counts + bm - 1) // bm) * bm
    mine = block_e[:, None] == jnp.arange(N_EXPERTS, dtype=jnp.int32)[None, :]
    count_b = jnp.sum(jnp.where(mine, counts[None, :], 0), axis=1)
    pstart_b = jnp.sum(jnp.where(mine, pstart[None, :], 0), axis=1)
    valid = jnp.clip(count_b - (starts - pstart_b), 0, bm).astype(jnp.int32)
    xbuf = _sc_scatter_rows(h2, dest, n_blocks * bm)
    ybuf = _expert_ffn(xbuf, block_e, next_e, valid, n_used, w_up, b_up, w_down, b_down, layer)
    yg = _sc_gather_rows(ybuf, dest.reshape(a)).reshape(TOP_K, t, dh)
    return _combine(x1, mod, yg, info, ln_g, ln_b, seq)


def kernel(x, c, rel_bias, w_in, w_out, w_ada, b_ada, ln1_g, ln1_b, ln2_g, ln2_b, pool_w, pool_scale,
           attn_sinks, conv_w, gdn_a_log, gdn_dt_bias, gdn_norm_w, router_w, router_b,
           exp_w_up, exp_b_up, exp_w_down, exp_b_down):
    bsz, seq, d = x.shape
    depth = w_in.shape[0]
    t = bsz * seq
    assert d == D_MODEL and w_in.shape[2] == IN_DIM
    assert seq % GDN_SUPER == 0 and seq % (ATT_BLOCKS * WINDOW) == 0
    assert t % ROW_TILE == 0 and seq % ROW_TILE == 0

    mod_all = _modulation(c, w_ada, b_ada).reshape(depth, bsz, 6, d)
    bias = _band_bias(rel_bias)

    w_up_all = exp_w_up.reshape((depth * N_EXPERTS,) + exp_w_up.shape[2:])
    b_up_all = exp_b_up.reshape(depth * N_EXPERTS, 1, exp_b_up.shape[2])
    w_down_all = exp_w_down.reshape((depth * N_EXPERTS,) + exp_w_down.shape[2:])
    b_down_all = exp_b_down.reshape(depth * N_EXPERTS, 1, exp_b_down.shape[2])

    x2d = x.reshape(t, d)
    for l in range(depth):
        mod = mod_all[l]
        w_out_p = _take_static(w_out[l], _OUT_PERM, 0).astype(BF16)
        ident = jnp.zeros((CONV_WIDTH, 1), F32).at[CONV_WIDTH - 1, 0].set(1.0)
        conv_p = jnp.where(jnp.asarray(_GDN_CONV_SRC >= 0), _take_cols(conv_w[l].astype(F32), _GDN_CONV_SRC),
                           ident)
        pool_bd = jnp.zeros((POOL_DIM, POOL_DIM), F32)
        for gi in range(len(POOL_WINDOWS)):
            sl = slice(gi * POOL_GROUP, (gi + 1) * POOL_GROUP)
            pool_bd = pool_bd.at[sl, sl].set(pool_w[l, gi].astype(F32))
        alog_v = _lane_vector(gdn_a_log[l], N_GDN_HEADS)
        dtb_v = _lane_vector(gdn_dt_bias[l], N_GDN_HEADS)
        nw_v = jnp.tile(gdn_norm_w[l].astype(F32), 2).reshape(1, LANES)
        rw = jnp.zeros((d, LANES), F32).at[:, :N_EXPERTS].set(router_w[l].astype(F32))
        rw_hi, rw_lo = _split_bf16(rw)
        rb = jnp.full((1, LANES), NEG_INF, F32).at[0, :N_EXPERTS].set(router_b[l].astype(F32))

        u_pool, aq, akv, gdn, ba = _in_projection(x2d, mod, w_in.astype(F32), l, seq)
        y_pool = _pool_mixer(u_pool, pool_bd.astype(BF16), pool_scale[l].astype(F32), seq)
        y_att = _swa_attention(aq, akv, bias, attn_sinks[l].astype(F32), seq)
        y_gdn = _gdn_mixer(gdn, ba, conv_p, alog_v, dtb_v, nw_v, seq)
        x1, h2, info, er, cnt = _out_projection(x2d, mod, y_pool, y_att, y_gdn, w_out_p, ln1_g[l], ln1_b[l],
                                            rw_hi, rw_lo, rb, seq)
        x2d = _moe(h2, info, er, cnt, x1, mod, ln2_g[l], ln2_b[l], w_up_all, b_up_all, w_down_all, b_down_all,
                   l, seq)
    return x2d.reshape(bsz, seq, d)
```

```python
import functools

import numpy as np
import jax
import jax.numpy as jnp
from jax import lax
from jax.experimental import pallas as pl
from jax.experimental.pallas import tpu as pltpu
from jax.experimental.pallas import tpu_sc as plsc

F32 = jnp.float32
BF16 = jnp.bfloat16

D_MODEL = 1024
HEAD_DIM = 64
POOL_DIM = 256
POOL_WINDOWS = (2, 4, 8, 16)
POOL_GROUP = 64
N_ATT_HEADS = 6
N_KV_HEADS = 2
ATT_DIM = 384
KV_DIM = 128
WINDOW = 128
N_BUCKETS = 32
MAX_DISTANCE = 128
N_GDN_HEADS = 6
GDN_DIM = 384
CONV_WIDTH = 4
GDN_CHUNK = 64
N_EXPERTS = 32
TOP_K = 4
EXPERT_DIM = 1024
SWIGLU_ALPHA = 1.702
SWIGLU_LIMIT = 7.0
DEPTH = 2
DEEPNORM_ALPHA = (2 * DEPTH) ** 0.25
LN_EPS = 1e-5
NORM_EPS = 1e-6
NEG_INF = -1e30

LANES = 128
SUBLANES = 8
VMEM_LIMIT = 56 * 1024 * 1024

ROW_TILE = 512
IN_TILE = 1024
ATT_BLOCKS = 4
ATT_GROUP = 6
GDN_SUPER = 256
EXPERT_BLOCK = 512
SC_CHUNK = 64

_OFF_AQ = POOL_DIM
_OFF_AK = _OFF_AQ + ATT_DIM
_OFF_AV = _OFF_AK + KV_DIM
_OFF_GQ = _OFF_AV + KV_DIM
_OFF_GK = _OFF_GQ + GDN_DIM
_OFF_GV = _OFF_GK + GDN_DIM
_OFF_GZ = _OFF_GV + GDN_DIM
_OFF_GB = _OFF_GZ + GDN_DIM
_OFF_GA = _OFF_GB + N_GDN_HEADS
IN_DIM = _OFF_GA + N_GDN_HEADS

P_POOL = (0, POOL_DIM)
P_Q = (P_POOL[1], P_POOL[1] + ATT_DIM)
P_KV = (P_Q[1], P_Q[1] + 2 * KV_DIM)
P_GDN = (P_KV[1], P_KV[1] + 4 * GDN_DIM)
P_BA = (P_GDN[1], P_GDN[1] + LANES)
P_TOTAL = P_BA[1]


def _head_cols(off, h):
    return list(range(off + HEAD_DIM * h, off + HEAD_DIM * (h + 1)))


def _build_in_perm():
    cols = list(range(POOL_DIM))
    for p in range(N_ATT_HEADS // 2):
        cols += _head_cols(_OFF_AQ, p) + _head_cols(_OFF_AQ, p + 3)
    cols += list(range(_OFF_AK, _OFF_AK + 2 * KV_DIM))
    gdn_src = []
    for p in range(N_GDN_HEADS // 2):
        e, o = 2 * p, 2 * p + 1
        grp = (_head_cols(_OFF_GK, e) + _head_cols(_OFF_GQ, e)
               + _head_cols(_OFF_GQ, o) + _head_cols(_OFF_GK, o)
               + _head_cols(_OFF_GV, o) + _head_cols(_OFF_GV, e)
               + _head_cols(_OFF_GZ, o) + _head_cols(_OFF_GZ, e))
        cols += grp
        gdn_src += [c - _OFF_GQ if c < _OFF_GZ else -1 for c in grp]
    cols += list(range(_OFF_GB, _OFF_GB + 2 * N_GDN_HEADS))
    cols += [-1] * (LANES - 2 * N_GDN_HEADS)
    assert len(cols) == P_TOTAL
    return np.asarray(cols, np.int32), np.asarray(gdn_src, np.int32)


_IN_PERM, _GDN_CONV_SRC = _build_in_perm()


def _build_out_perm():
    rows = list(range(POOL_DIM))
    for p in range(N_ATT_HEADS // 2):
        rows += _head_cols(POOL_DIM, p) + _head_cols(POOL_DIM, p + 3)
    for p in range(N_GDN_HEADS // 2):
        rows += _head_cols(POOL_DIM + ATT_DIM, 2 * p + 1) + _head_cols(POOL_DIM + ATT_DIM, 2 * p)
    return np.asarray(rows, np.int32)


_OUT_PERM = _build_out_perm()


def _t5_bucket_line():
    n = np.maximum(2 * WINDOW - 1 - np.arange(3 * WINDOW - 1), 0)
    max_exact = N_BUCKETS // 2
    nf = np.maximum(n, 1).astype(np.float32)
    large = max_exact + (np.log(nf / max_exact) / np.float32(np.log(MAX_DISTANCE / max_exact))
                         * (N_BUCKETS - max_exact)).astype(np.int32)
    large = np.minimum(large, N_BUCKETS - 1)
    return np.where(n < max_exact, n, large).astype(np.int32)


_BUCKET_LINE = _t5_bucket_line()


def _band_bias(rel_bias):
    n_line = 3 * WINDOW - 1
    line = jnp.take(rel_bias.astype(F32), jnp.asarray(_BUCKET_LINE), axis=0).T
    heads = line.shape[0]
    padded = jnp.concatenate([line, jnp.zeros((heads, 1), F32)], axis=1)
    skew = jnp.tile(padded, (1, WINDOW))[:, :WINDOW * n_line].reshape(heads, WINDOW, n_line)
    return skew[:, :, WINDOW - 1:3 * WINDOW - 1]


def _take_static(w, perm, axis):
    parts = []
    start = 0
    for i in range(1, len(perm) + 1):
        run_ends = (i == len(perm) or ((perm[i] < 0) != (perm[i - 1] < 0))
                    or (perm[i] >= 0 and perm[i] != perm[i - 1] + 1))
        if run_ends:
            if perm[start] < 0:
                shape = list(w.shape)
                shape[axis] = i - start
                parts.append(jnp.zeros(shape, w.dtype))
            else:
                parts.append(lax.slice_in_dim(w, int(perm[start]), int(perm[start]) + (i - start), axis=axis))
            start = i
    return jnp.concatenate(parts, axis=axis)


def _take_cols(w, perm):
    return _take_static(w, perm, w.ndim - 1)


def _split_bf16(x):
    hi = x.astype(BF16)
    lo = (x - hi.astype(F32)).astype(BF16)
    return hi, lo


def _pack_bf16_pairs(x):
    n = x.shape[1] // 2
    bits = pltpu.bitcast(x.astype(BF16).astype(F32), jnp.int32)
    return lax.shift_right_logical(bits[:, :n], 16) | bits[:, n:]


def _unpack_bf16_pairs(u):
    lo = pltpu.bitcast(lax.shift_left(u, 16), F32)
    hi = pltpu.bitcast(u & jnp.int32(-65536), F32)
    return jnp.concatenate([lo, hi], axis=1)


def _dot(a, b):
    return jnp.dot(a, b, preferred_element_type=F32)


def _dot_nt(a, b):
    return lax.dot_general(a, b, (((1,), (1,)), ((), ())), preferred_element_type=F32)


def _dot_hi_exact_rhs(x, m_bf16):
    hi, lo = _split_bf16(x)
    return _dot(hi, m_bf16) + _dot(lo, m_bf16)


def _sigmoid(x):
    return 1.0 / (1.0 + jnp.exp(-x))


def _layer_norm(r, g, b):
    mu = jnp.mean(r, axis=-1, keepdims=True)
    d = r - mu
    var = jnp.mean(d * d, axis=-1, keepdims=True)
    return d * lax.rsqrt(var + LN_EPS) * g + b


def _cparams(sem):
    return pltpu.CompilerParams(dimension_semantics=sem, vmem_limit_bytes=VMEM_LIMIT)


def _mod_kernel(c_ref, w_ref, b_ref, o_ref):
    c = c_ref[...]
    ca = c * _sigmoid(c)
    ch, cl = _split_bf16(ca)
    wh, wl = _split_bf16(w_ref[0])
    o_ref[0] = _dot(ch, wh) + _dot(cl, wh) + _dot(ch, wl) + b_ref[0]


def _modulation(c, w_ada, b_ada):
    depth, d, n = w_ada.shape
    bsz = c.shape[0]
    tn = 512
    return pl.pallas_call(
        _mod_kernel,
        grid=(depth, n // tn),
        in_specs=[
            pl.BlockSpec((bsz, d), lambda l, j: (0, 0)),
            pl.BlockSpec((1, d, tn), lambda l, j: (l, 0, j)),
            pl.BlockSpec((1, 1, tn), lambda l, j: (l, 0, j)),
        ],
        out_specs=pl.BlockSpec((1, bsz, tn), lambda l, j: (l, 0, j)),
        out_shape=jax.ShapeDtypeStruct((depth, bsz, n), F32),
        compiler_params=_cparams(("arbitrary", "arbitrary")),
        name="adaln_mod",
    )(c, w_ada, b_ada.reshape(depth, 1, n))


def _perm_runs(perm):
    runs = []
    start = 0
    for i in range(1, len(perm) + 1):
        run_ends = (i == len(perm) or ((perm[i] < 0) != (perm[i - 1] < 0))
                    or (perm[i] >= 0 and perm[i] != perm[i - 1] + 1))
        if run_ends:
            runs.append((int(perm[start]) if perm[start] >= 0 else -1, i - start, start))
            start = i
    return runs


_IN_RUNS = _perm_runs(_IN_PERM)


def _inproj_kernel(layer, x_ref, mod_ref, w_hbm, pool_ref, q_ref, kv_ref, gdn_ref, ba_ref, w_f32, w_ref, sem):
    @pl.when(pl.program_id(0) == 0)
    def _():
        fetch = pltpu.make_async_copy(w_hbm.at[layer], w_f32, sem)
        fetch.start()
        fetch.wait()
        for src, n, dst in _IN_RUNS:
            if src < 0:
                w_ref[:, dst:dst + n] = jnp.zeros((w_ref.shape[0], n), BF16)
            else:
                w_ref[:, dst:dst + n] = w_f32[:, src:src + n].astype(BF16)
        q_cols = w_ref[:, P_Q[0]:P_Q[1]].astype(F32) * (HEAD_DIM ** -0.5)
        w_ref[:, P_Q[0]:P_Q[1]] = q_cols.astype(BF16)

    sh = mod_ref[0, 0:1, :]
    sc = mod_ref[0, 1:2, :]
    h = (x_ref[...] * (1.0 + sc) + sh).astype(BF16)

    def mm(rng):
        return _dot(h, w_ref[:, rng[0]:rng[1]])

    pool_ref[...] = mm(P_POOL)
    q_ref[...] = mm(P_Q).astype(BF16)
    kv_ref[...] = mm(P_KV).astype(BF16)
    gdn_ref[...] = mm(P_GDN)
    ba_ref[...] = mm(P_BA)


def _in_projection(x2d, mod, w_in, layer, seq):
    t, d = x2d.shape
    tm = min(IN_TILE, seq)
    widths = [r[1] - r[0] for r in (P_POOL, P_Q, P_KV, P_GDN, P_BA)]
    dtypes = [F32, BF16, BF16, F32, F32]
    return pl.pallas_call(
        functools.partial(_inproj_kernel, layer),
        grid=(t // tm,),
        in_specs=[
            pl.BlockSpec((tm, d), lambda i: (i, 0)),
            pl.BlockSpec((1, 6, d), lambda i: ((i * tm) // seq, 0, 0)),
            pl.BlockSpec(memory_space=pl.ANY),
        ],
        out_specs=[pl.BlockSpec((tm, w), lambda i: (i, 0)) for w in widths],
        out_shape=[jax.ShapeDtypeStruct((t, w), dt) for w, dt in zip(widths, dtypes)],
        scratch_shapes=[pltpu.VMEM((d, w_in.shape[2]), F32), pltpu.VMEM((d, P_TOTAL), BF16),
                        pltpu.SemaphoreType.DMA(())],
        compiler_params=_cparams(("arbitrary",)),
        name="in_proj",
    )(x2d, mod, w_in)


def _pool_kernel(u_ref, w_ref, scale_ref, o_ref):
    u = u_ref[...]
    row = lax.broadcasted_iota(jnp.int32, u.shape, 0)
    lane = lax.broadcasted_iota(jnp.int32, u.shape, 1)

    def shifted(a, s):
        return jnp.where(row >= s, pltpu.roll(a, s, axis=0), 0.0)

    sums = []
    acc = u
    for wdt in POOL_WINDOWS:
        acc = acc + shifted(acc, wdt // 2)
        sums.append(acc)
    grp = lane // POOL_GROUP
    wsum = sums[-1]
    win = jnp.full(u.shape, POOL_WINDOWS[-1], jnp.int32)
    for gi in range(len(POOL_WINDOWS) - 2, -1, -1):
        wsum = jnp.where(grp == gi, sums[gi], wsum)
        win = jnp.where(grp == gi, POOL_WINDOWS[gi], win)
    cnt = jnp.minimum(row + 1, win).astype(F32)
    p = wsum / cnt - u
    y = _dot(p.astype(BF16), w_ref[...]) * scale_ref[...]
    o_ref[...] = y.astype(BF16)


def _pool_mixer(u, pool_w_bd, pool_scale, seq):
    t, c = u.shape
    return pl.pallas_call(
        _pool_kernel,
        grid=(t // seq,),
        in_specs=[
            pl.BlockSpec((seq, c), lambda b: (b, 0)),
            pl.BlockSpec((c, c), lambda b: (0, 0)),
            pl.BlockSpec((1, c), lambda b: (0, 0)),
        ],
        out_specs=pl.BlockSpec((seq, c), lambda b: (b, 0)),
        out_shape=jax.ShapeDtypeStruct((t, c), BF16),
        compiler_params=_cparams(("arbitrary",)),
        name="pool_mixer",
    )(u, pool_w_bd, pool_scale.reshape(1, c))


def _attn_kernel(sink_ref, q_ref, kvc_ref, kvp_ref, bias_ref, o_ref):
    step = pl.program_id(1)
    qi = lax.broadcasted_iota(jnp.int32, (WINDOW, 2 * WINDOW), 0)
    kj = lax.broadcasted_iota(jnp.int32, (WINDOW, 2 * WINDOW), 1)
    dist = qi + WINDOW - kj
    in_band = (dist >= 0) & (dist < WINDOW)
    lo = lax.broadcasted_iota(jnp.int32, (WINDOW, LANES), 1) < HEAD_DIM
    heads = [(p, half) for p in range(N_ATT_HEADS // 2) for half in range(2)]
    sinks = [sink_ref[p + 3 * half] for p, half in heads]
    for sub in range(ATT_BLOCKS):
        r0 = sub * WINDOW
        prev = kvp_ref[...] if sub == 0 else kvc_ref[r0 - WINDOW:r0, :]
        kv = jnp.concatenate([prev, kvc_ref[r0:r0 + WINDOW, :]], axis=0)
        k = kv[:, :KV_DIM]
        v = kv[:, KV_DIM:]
        valid = in_band & ((kj >= WINDOW) | (step > 0)) if sub == 0 else in_band
        for g0 in range(0, len(heads), ATT_GROUP):
            group = heads[g0:g0 + ATT_GROUP]
            sk_g = sinks[g0:g0 + ATT_GROUP]
            scores = []
            for p, half in group:
                qp = q_ref[r0:r0 + WINDOW, p * LANES:(p + 1) * LANES]
                qm = jnp.where(lo if half == 0 else jnp.logical_not(lo), qp, jnp.zeros_like(qp))
                scores.append(jnp.where(valid, _dot_nt(qm, k) + bias_ref[p + 3 * half], NEG_INF))
            tops = [jnp.maximum(jnp.max(s, axis=-1, keepdims=True), sk) for s, sk in zip(scores, sk_g)]
            probs = [jnp.exp(s - m) for s, m in zip(scores, tops)]
            dens = [jnp.sum(pr, axis=-1, keepdims=True) + jnp.exp(sk - m)
                    for pr, sk, m in zip(probs, sk_g, tops)]
            outs = [_dot(pr.astype(BF16), v) / den for pr, den in zip(probs, dens)]
            for idx in range(0, len(group), 2):
                p = group[idx][0]
                o_ref[r0:r0 + WINDOW, p * LANES:(p + 1) * LANES] = (
                    jnp.where(lo, outs[idx], outs[idx + 1]).astype(BF16))


def _swa_attention(q, kv, bias, sinks, seq):
    t = q.shape[0]
    rows = ATT_BLOCKS * WINDOW
    nblk = seq // rows
    return pl.pallas_call(
        _attn_kernel,
        grid=(t // seq, nblk),
        in_specs=[
            pl.BlockSpec(memory_space=pltpu.SMEM),
            pl.BlockSpec((rows, ATT_DIM), lambda b, n: (b * nblk + n, 0)),
            pl.BlockSpec((rows, 2 * KV_DIM), lambda b, n: (b * nblk + n, 0)),
            pl.BlockSpec((WINDOW, 2 * KV_DIM),
                         lambda b, n: (jnp.maximum((b * nblk + n) * ATT_BLOCKS - 1, 0), 0)),
            pl.BlockSpec((N_ATT_HEADS, WINDOW, 2 * WINDOW), lambda b, n: (0, 0, 0)),
        ],
        out_specs=pl.BlockSpec((rows, ATT_DIM), lambda b, n: (b * nblk + n, 0)),
        out_shape=jax.ShapeDtypeStruct((t, ATT_DIM), BF16),
        compiler_params=_cparams(("arbitrary", "arbitrary")),
        name="swa_attention",
    )(sinks, q, kv, kv, bias)


_GDN_BASE = SUBLANES
_GDN_LEVELS = int(np.log2(GDN_CHUNK // _GDN_BASE))


def _gdn_masks():
    r = np.arange(GDN_SUPER)
    ri, ci = r[:, None], r[None, :]
    same_chunk = (ri // GDN_CHUNK) == (ci // GDN_CHUNK)
    incl = same_chunk & (ri >= ci)
    planes = [incl, ri == ci]
    base = ((ri // _GDN_BASE) == (ci // _GDN_BASE)) & (ri > ci)
    planes.append(base)
    for lvl in range(_GDN_LEVELS):
        small = _GDN_BASE << lvl
        planes.append(((ri // (2 * small)) == (ci // (2 * small))) & ((ri // small) != (ci // small)) & (ri > ci))
    bmask = np.stack(planes).astype(np.float32)
    bmask[2] = -bmask[2]
    negmask = np.where(incl, 0.0, -np.inf).astype(np.float32)
    return negmask, bmask


_GDN_NEGMASK, _GDN_BMASK = _gdn_masks()


def _gdn_kernel(x_ref, halo_ref, ba_ref, cw_ref, alog_ref, dtb_ref, nw_ref, negmask_ref, bmask_ref,
                y_ref, state_ref, xs_ref):
    sc_id = pl.program_id(1)
    rows = GDN_SUPER
    nchunk = rows // GDN_CHUNK
    c_sz = GDN_CHUNK

    @pl.when(sc_id == 0)
    def _():
        state_ref[...] = jnp.zeros_like(state_ref)

    xs_ref[:SUBLANES, :] = jnp.where(sc_id == 0, 0.0, halo_ref[...])
    xs_ref[SUBLANES:, :] = x_ref[...]
    act = []
    for g in range(x_ref.shape[1] // LANES):
        cols = slice(g * LANES, (g + 1) * LANES)
        acc = x_ref[:, cols] * cw_ref[CONV_WIDTH - 1:CONV_WIDTH, cols]
        if g % 4 != 3:
            for s in range(1, CONV_WIDTH):
                acc = acc + (xs_ref[SUBLANES - s:SUBLANES - s + rows, cols]
                             * cw_ref[CONV_WIDTH - 1 - s:CONV_WIDTH - s, cols])
        act.append(acc * _sigmoid(acc))

    negmask = negmask_ref[...]
    tri_incl = bmask_ref[0]
    eye_b = bmask_ref[1]
    base_neg = bmask_ref[2]
    bands = [bmask_ref[3 + lvl] for lvl in range(_GDN_LEVELS)]
    li = lax.broadcasted_iota(jnp.int32, (LANES, LANES), 0)
    lj = lax.broadcasted_iota(jnp.int32, (LANES, LANES), 1)
    half_ones = jnp.where((li // HEAD_DIM) == (lj // HEAD_DIM), 1.0, 0.0).astype(BF16)
    lane_lo = lax.broadcasted_iota(jnp.int32, (rows, LANES), 1) < HEAD_DIM
    lane_lo_c = lax.broadcasted_iota(jnp.int32, (c_sz, LANES), 1) < HEAD_DIM

    ba = ba_ref[...]
    beta_all = _sigmoid(ba)
    sp_in = ba + dtb_ref[...]
    softplus = jnp.maximum(sp_in, 0.0) + jnp.log(1.0 + jnp.exp(-jnp.abs(sp_in)))
    g_all = -jnp.exp(alog_ref[...]) * softplus
    gcum = _dot_hi_exact_rhs_lhs(tri_incl, g_all)
    gcum_t = gcum.T

    heads = range(N_GDN_HEADS)
    lane_hi = jnp.logical_not(lane_lo)
    lane_hi_c = jnp.logical_not(lane_lo_c)
    mk = [lane_lo if h % 2 == 0 else lane_hi for h in heads]
    mk_c = [lane_lo_c if h % 2 == 0 else lane_hi_c for h in heads]
    scale = HEAD_DIM ** -0.5

    def bdot(a, b):
        return _dot(a.astype(BF16), b.astype(BF16))

    xk, xq, gn, gc_col, beta, eg = [], [], [], [], [], []
    for h in heads:
        g = act[4 * (h // 2) + (h % 2)]
        g = g * lax.rsqrt(_dot_hi_exact_rhs(g * g, half_ones) + NORM_EPS)
        gn.append(g)
        xk.append(jnp.where(mk[h], g, 0.0))
        xq.append(jnp.where(mk[h], pltpu.roll(g, HEAD_DIM, axis=1), 0.0) * scale)
        beta.append(beta_all[:, h:h + 1])
        gc_col.append(gcum[:, N_GDN_HEADS + h:N_GDN_HEADS + h + 1])
        eg.append(jnp.exp(gc_col[h]))

    l_b, attn, rhs = [], [], []
    for h in heads:
        gc_row = gcum_t[N_GDN_HEADS + h:N_GDN_HEADS + h + 1, :]
        decay = jnp.exp(gc_col[h] - gc_row + negmask)
        xk_b = xk[h].astype(BF16)
        kk = _dot_nt((xk[h] * beta[h]).astype(BF16), xk_b)
        l_b.append((kk * decay).astype(BF16))
        attn.append((_dot_nt(xq[h].astype(BF16), xk_b) * decay).astype(BF16))
        vv = act[4 * (h // 2) + 2]
        rhs.append(jnp.where(mk[h], gn[h] * eg[h], vv) * beta[h])

    a1 = [l_b[h] * base_neg for h in heads]
    a2 = [_dot(a1[h], a1[h]).astype(BF16) for h in heads]
    a4 = [_dot(a2[h], a2[h]).astype(BF16) for h in heads]
    inv0 = [eye_b + a1[h] for h in heads]
    acc1 = [inv0[h].astype(F32) + _dot(a2[h], inv0[h]) for h in heads]
    inv_b = [(acc1[h] + _dot(a4[h], acc1[h].astype(BF16))).astype(BF16) for h in heads]
    for lvl in range(_GDN_LEVELS - 1):
        mid = [_dot(l_b[h] * bands[lvl], inv_b[h]).astype(BF16) for h in heads]
        inv_b = [inv_b[h] - _dot(inv_b[h], mid[h]).astype(BF16) for h in heads]
    half = [_dot(inv_b[h], rhs[h].astype(BF16)) for h in heads]
    mid = [_dot(l_b[h] * bands[_GDN_LEVELS - 1], half[h].astype(BF16)) for h in heads]
    sol = [half[h] - _dot(inv_b[h], mid[h].astype(BF16)) for h in heads]

    lane_lo_s = lax.broadcasted_iota(jnp.int32, (LANES, LANES), 1) < HEAD_DIM
    mk_s = [lane_lo_s if h % 2 == 0 else jnp.logical_not(lane_lo_s) for h in heads]
    sol_b = [sol[h].astype(BF16) for h in heads]
    attn_sol = [_dot(attn[h], sol_b[h]) for h in heads]
    q_eff = [(xq[h] * eg[h] - jnp.where(mk[h], attn_sol[h], 0.0)).astype(BF16) for h in heads]
    o_free = [jnp.where(mk[h], 0.0, attn_sol[h]) for h in heads]
    kw = [[] for _ in heads]
    ku = [[] for _ in heads]
    cdec = [[] for _ in heads]
    for c in range(nchunk):
        r0 = c * c_sz
        for h in heads:
            glast = gcum[r0 + c_sz - 1:r0 + c_sz, N_GDN_HEADS + h:N_GDN_HEADS + h + 1]
            kd_t = (xk[h][r0:r0 + c_sz] * jnp.exp(glast - gc_col[h][r0:r0 + c_sz])).T
            both = _dot(kd_t.astype(BF16), sol_b[h][r0:r0 + c_sz])
            kw[h].append(jnp.where(mk_s[h], both, 0.0).astype(BF16))
            ku[h].append(jnp.where(mk_s[h], 0.0, both))
            cdec[h].append(jnp.exp(glast))
    st = [state_ref[h] for h in heads]
    o_parts = [[] for _ in heads]
    for c in range(nchunk):
        r0 = c * c_sz
        for h in heads:
            lhs = jnp.concatenate([kw[h][c], q_eff[h][r0:r0 + c_sz]], axis=0)
            prod = _dot(lhs, st[h].astype(BF16))
            o_parts[h].append(prod[LANES:] + o_free[h][r0:r0 + c_sz])
            st[h] = st[h] * cdec[h][c] + ku[h][c] - prod[:LANES]
    for h in heads:
        state_ref[h] = st[h]

    for p in range(N_GDN_HEADS // 2):
        o_pair = [jnp.concatenate(o_parts[h], axis=0) for h in (2 * p, 2 * p + 1)]
        o = jnp.where(lane_lo, o_pair[1], o_pair[0])
        ms = _dot_hi_exact_rhs(o * o, half_ones) * (1.0 / HEAD_DIM)
        zz = act[4 * p + 3]
        y = o * lax.rsqrt(ms + NORM_EPS) * nw_ref[...] * zz
        y_ref[:, p * LANES:(p + 1) * LANES] = y.astype(BF16)


def _dot_hi_exact_rhs_lhs(m_bf16, x):
    hi, lo = _split_bf16(x)
    return _dot(m_bf16, hi) + _dot(m_bf16, lo)


def _gdn_mixer(gdn, ba, conv_p, alog_v, dtb_v, nw_v, seq):
    t, c = gdn.shape
    rows = GDN_SUPER
    nsc = seq // rows
    hb = rows // SUBLANES
    return pl.pallas_call(
        _gdn_kernel,
        grid=(t // seq, nsc),
        in_specs=[
            pl.BlockSpec((rows, c), lambda b, s: (b * nsc + s, 0)),
            pl.BlockSpec((SUBLANES, c), lambda b, s: (jnp.maximum((b * nsc + s) * hb - 1, 0), 0)),
            pl.BlockSpec((rows, LANES), lambda b, s: (b * nsc + s, 0)),
            pl.BlockSpec((CONV_WIDTH, c), lambda b, s: (0, 0)),
            pl.BlockSpec((1, LANES), lambda b, s: (0, 0)),
            pl.BlockSpec((1, LANES), lambda b, s: (0, 0)),
            pl.BlockSpec((1, LANES), lambda b, s: (0, 0)),
            pl.BlockSpec((rows, rows), lambda b, s: (0, 0)),
            pl.BlockSpec((3 + _GDN_LEVELS, rows, rows), lambda b, s: (0, 0, 0)),
        ],
        out_specs=pl.BlockSpec((rows, GDN_DIM), lambda b, s: (b * nsc + s, 0)),
        out_shape=jax.ShapeDtypeStruct((t, GDN_DIM), BF16),
        scratch_shapes=[pltpu.VMEM((N_GDN_HEADS, LANES, LANES), F32), pltpu.VMEM((rows + SUBLANES, c), F32)],
        compiler_params=_cparams(("arbitrary", "arbitrary")),
        name="gdn_mixer",
    )(gdn, gdn, ba, conv_p, alog_v, dtb_v, nw_v, jnp.asarray(_GDN_NEGMASK), jnp.asarray(_GDN_BMASK, BF16))


def _route_tile(logits, before, carry_ref, live):
    shape = logits.shape
    lane = lax.broadcasted_iota(jnp.int32, shape, 1).astype(F32)
    work = logits
    vals, idxs = [], []
    for _k in range(TOP_K):
        m = jnp.max(work, axis=-1, keepdims=True)
        idx = jnp.min(jnp.where(work == m, lane, float(LANES)), axis=-1, keepdims=True)
        vals.append(m)
        idxs.append(idx)
        work = jnp.where(lane == idx, -jnp.inf, work)
    exps = [jnp.exp(v - vals[0]) for v in vals]
    den = exps[0] + exps[1] + exps[2] + exps[3]
    onehots = [lane == idx for idx in idxs]
    member = jnp.zeros(shape, F32)
    for oh in onehots:
        member = member + jnp.where(oh, 1.0, 0.0)
    rank = _dot(before, member.astype(BF16)) + carry_ref[...]
    carry_ref[...] = carry_ref[...] + live * jnp.sum(member, axis=0, keepdims=True)
    info = jnp.zeros(shape, F32)
    for k in range(TOP_K):
        rank_k = jnp.sum(jnp.where(onehots[k], rank, 0.0), axis=-1, keepdims=True)
        info = jnp.where(lane == float(k), idxs[k], info)
        info = jnp.where(lane == float(TOP_K + k), rank_k, info)
        info = jnp.where(lane == float(2 * TOP_K + k), exps[k] / den, info)
    return info


def _outproj_kernel(x_ref, mod_ref, yp_ref, ya_ref, yg_ref, wp_ref, wa_ref, wg_ref, lng_ref, lnb_ref,
                    rwh_ref, rwl_ref, rb_ref, before_ref, x1_ref, h2_ref, info_ref, er_ref, cnt_ref,
                    carry_ref, logit_s):
    step = pl.program_id(0)

    @pl.when(step == 0)
    def _():
        carry_ref[...] = jnp.zeros_like(carry_ref)
        logit_s[...] = jnp.zeros_like(logit_s)

    y = _dot(yp_ref[...], wp_ref[...]) + _dot(ya_ref[...], wa_ref[...]) + _dot(yg_ref[...], wg_ref[...])
    live = jnp.where(step > 0, 1.0, 0.0)
    info = _route_tile(logit_s[...], before_ref[...], carry_ref, live)
    info_ref[...] = info
    er_ref[...] = info.T[:SUBLANES]
    cnt_ref[...] = carry_ref[...]
    g1 = mod_ref[0, 2:3, :]
    sh2 = mod_ref[0, 3:4, :]
    sc2 = mod_ref[0, 4:5, :]
    x1 = _layer_norm(DEEPNORM_ALPHA * x_ref[...] + g1 * y, lng_ref[...], lnb_ref[...])
    x1_ref[...] = x1
    h2 = x1 * (1.0 + sc2) + sh2
    hh, hl = _split_bf16(h2)
    h2_ref[...] = _pack_bf16_pairs(h2)
    logit_s[...] = _dot(hh, rwh_ref[...]) + _dot(hl, rwh_ref[...]) + _dot(hh, rwl_ref[...]) + rb_ref[...]


def _out_projection(x2d, mod, yp, ya, yg, w_out_p, ln_g, ln_b, rw_hi, rw_lo, rb, seq):
    t, d = x2d.shape
    tm = ROW_TILE
    wp = w_out_p[:POOL_DIM]
    wa = w_out_p[POOL_DIM:POOL_DIM + ATT_DIM]
    wg = w_out_p[POOL_DIM + ATT_DIM:]
    last = t // tm - 1
    row = lambda i: (jnp.minimum(i, last), 0)
    routed = lambda i: (jnp.maximum(i - 1, 0), 0)
    fixed = lambda i: (0, 0)
    return pl.pallas_call(
        _outproj_kernel,
        grid=(t // tm + 1,),
        in_specs=[
            pl.BlockSpec((tm, d), row),
            pl.BlockSpec((1, 6, d), lambda i: ((jnp.minimum(i, last) * tm) // seq, 0, 0)),
            pl.BlockSpec((tm, POOL_DIM), row),
            pl.BlockSpec((tm, ATT_DIM), row),
            pl.BlockSpec((tm, GDN_DIM), row),
            pl.BlockSpec((POOL_DIM, d), fixed),
            pl.BlockSpec((ATT_DIM, d), fixed),
            pl.BlockSpec((GDN_DIM, d), fixed),
            pl.BlockSpec((1, d), fixed),
            pl.BlockSpec((1, d), fixed),
            pl.BlockSpec((d, LANES), fixed),
            pl.BlockSpec((d, LANES), fixed),
            pl.BlockSpec((1, LANES), fixed),
            pl.BlockSpec((tm, tm), fixed),
        ],
        out_specs=[pl.BlockSpec((tm, d), row), pl.BlockSpec((tm, d // 2), row), pl.BlockSpec((tm, LANES), routed),
                   pl.BlockSpec((SUBLANES, tm), lambda i: (0, jnp.maximum(i - 1, 0))),
                   pl.BlockSpec((1, LANES), fixed)],
        out_shape=[jax.ShapeDtypeStruct((t, d), F32), jax.ShapeDtypeStruct((t, d // 2), jnp.int32),
                   jax.ShapeDtypeStruct((t, LANES), F32), jax.ShapeDtypeStruct((SUBLANES, t), F32),
                   jax.ShapeDtypeStruct((1, LANES), F32)],
        scratch_shapes=[pltpu.VMEM((1, LANES), F32), pltpu.VMEM((tm, LANES), F32)],
        compiler_params=_cparams(("arbitrary",)),
        name="out_proj_ln_route",
    )(x2d, mod, yp, ya, yg, wp, wa, wg, ln_g.reshape(1, d), ln_b.reshape(1, d), rw_hi, rw_lo, rb,
      jnp.tril(jnp.ones((tm, tm), BF16), -1))


def _slot_kernel(er_ref, cnt_ref, dest_ref, pcum_ref):
    cnt = jnp.broadcast_to(cnt_ref[...], (SUBLANES, LANES))
    padded = jnp.floor((cnt + float(EXPERT_BLOCK - 1)) * (1.0 / EXPERT_BLOCK)) * float(EXPERT_BLOCK)
    lane8 = lax.broadcasted_iota(jnp.int32, (SUBLANES, LANES), 1)
    acc = padded
    step = 1
    while step < LANES:
        acc = acc + jnp.where(lane8 >= step, pltpu.roll(acc, step, axis=1), 0.0)
        step *= 2
    pcum_ref[...] = acc[:1].astype(jnp.int32)
    pstart = acc - padded

    er = er_ref[...]
    start = jnp.zeros(er.shape, F32)
    for e in range(N_EXPERTS):
        offset = jnp.sum(jnp.where(lane8 == e, pstart, 0.0), axis=-1, keepdims=True)
        start = jnp.where(er == float(e), offset, start)
    row = lax.broadcasted_iota(jnp.int32, er.shape, 0)
    slots = jnp.where(row < TOP_K, start + pltpu.roll(er, TOP_K, axis=0), 0.0)
    dest_ref[...] = slots.astype(jnp.int32)


def _slots(er, cnt):
    t = er.shape[1]
    return pl.pallas_call(
        _slot_kernel,
        grid=(1,),
        in_specs=[pl.BlockSpec((SUBLANES, t), lambda i: (0, 0)), pl.BlockSpec((1, LANES), lambda i: (0, 0))],
        out_specs=[pl.BlockSpec((SUBLANES, t), lambda i: (0, 0)), pl.BlockSpec((1, LANES), lambda i: (0, 0))],
        out_shape=[jax.ShapeDtypeStruct((SUBLANES, t), jnp.int32), jax.ShapeDtypeStruct((1, LANES), jnp.int32)],
        compiler_params=_cparams(("arbitrary",)),
        name="moe_slots",
    )(er, cnt)


def _expert_kernel(e0, be_ref, nxt_ref, val_ref, nu_ref, x_ref, wup_hbm, bup_ref, wdn_hbm, bdn_ref, y_ref,
                   wup_st, wdn_st, wup_bf, wdn_bf, sems):
    i = pl.program_id(0)
    e = be_ref[i]
    prev = be_ref[jnp.maximum(i - 1, 0)]
    used = i < nu_ref[0]

    def weight_copies(expert):
        return (pltpu.make_async_copy(wup_hbm.at[e0 + expert], wup_st, sems.at[0]),
                pltpu.make_async_copy(wdn_hbm.at[e0 + expert], wdn_st, sems.at[1]))

    @pl.when(i == 0)
    def _():
        for cp in weight_copies(e):
            cp.start()

    @pl.when(used & ((i == 0) | (e != prev)))
    def _():
        for cp in weight_copies(e):
            cp.wait()
        wup_bf[...] = wup_st[...].astype(BF16)
        wdn_bf[...] = wdn_st[...].astype(BF16)

        @pl.when(nxt_ref[i] >= 0)
        def _():
            for cp in weight_copies(nxt_ref[i]):
                cp.start()

    def ffn(rows):
        xb = _unpack_bf16_pairs(x_ref[:rows, :]).astype(BF16)
        hb = _dot(xb, wup_bf[...]) + bup_ref[0]
        x_glu = jnp.minimum(hb[:, :EXPERT_DIM], SWIGLU_LIMIT)
        x_lin = jnp.clip(hb[:, EXPERT_DIM:], -SWIGLU_LIMIT, SWIGLU_LIMIT)
        act = x_glu * _sigmoid(SWIGLU_ALPHA * x_glu) * (x_lin + 1.0)
        y = _dot(act.astype(BF16), wdn_bf[...]) + bdn_ref[0]
        y_ref[:rows, :] = _pack_bf16_pairs(y)

    half_rows = x_ref.shape[0] // 2
    small = val_ref[i] <= half_rows

    @pl.when(used & jnp.logical_not(small))
    def _():
        ffn(x_ref.shape[0])

    @pl.when(used & small)
    def _():
        ffn(half_rows)
        y_ref[half_rows:, :] = jnp.zeros((x_ref.shape[0] - half_rows, y_ref.shape[1]), y_ref.dtype)

    @pl.when(i >= nu_ref[0])
    def _():
        y_ref[...] = jnp.zeros_like(y_ref)


def _expert_ffn(xbuf, block_e, next_e, valid, n_used, w_up, b_up, w_down, b_down, layer):
    p, dh = xbuf.shape
    d = 2 * dh
    bm = EXPERT_BLOCK
    ne, _, n_up = w_up.shape
    e0 = layer * N_EXPERTS
    grid_spec = pltpu.PrefetchScalarGridSpec(
        num_scalar_prefetch=4,
        grid=(p // bm,),
        in_specs=[
            pl.BlockSpec((bm, dh), lambda i, be, nx, vl, nu: (i, 0)),
            pl.BlockSpec(memory_space=pl.ANY),
            pl.BlockSpec((1, 1, n_up), lambda i, be, nx, vl, nu: (e0 + be[i], 0, 0)),
            pl.BlockSpec(memory_space=pl.ANY),
            pl.BlockSpec((1, 1, d), lambda i, be, nx, vl, nu: (e0 + be[i], 0, 0)),
        ],
        out_specs=pl.BlockSpec((bm, dh), lambda i, be, nx, vl, nu: (i, 0)),
        scratch_shapes=[pltpu.VMEM((d, n_up), F32), pltpu.VMEM((EXPERT_DIM, d), F32),
                        pltpu.VMEM((d, n_up), BF16), pltpu.VMEM((EXPERT_DIM, d), BF16),
                        pltpu.SemaphoreType.DMA((2,))],
    )
    return pl.pallas_call(
        functools.partial(_expert_kernel, e0),
        grid_spec=grid_spec,
        out_shape=jax.ShapeDtypeStruct((p, dh), jnp.int32),
        compiler_params=_cparams(("arbitrary",)),
        name="expert_ffn",
    )(block_e, next_e, valid, n_used, xbuf, w_up, b_up, w_down, b_down)


def _combine_kernel(x1_ref, mod_ref, yg_ref, info_ref, lng_ref, lnb_ref, o_ref):
    info = info_ref[...]
    y = jnp.zeros(x1_ref.shape, F32)
    for k in range(TOP_K):
        gate = info[:, 2 * TOP_K + k:2 * TOP_K + k + 1]
        y = y + gate * _unpack_bf16_pairs(yg_ref[k])
    g2 = mod_ref[0, 5:6, :]
    o_ref[...] = _layer_norm(DEEPNORM_ALPHA * x1_ref[...] + g2 * y, lng_ref[...], lnb_ref[...])


def _combine(x1, mod, yg, info, ln_g, ln_b, seq):
    t, d = x1.shape
    tm = min(IN_TILE, seq)
    row = lambda i: (i, 0)
    fixed = lambda i: (0, 0)
    return pl.pallas_call(
        _combine_kernel,
        grid=(t // tm,),
        in_specs=[
            pl.BlockSpec((tm, d), row),
            pl.BlockSpec((1, 6, d), lambda i: ((i * tm) // seq, 0, 0)),
            pl.BlockSpec((TOP_K, tm, d // 2), lambda i: (0, i, 0)),
            pl.BlockSpec((tm, LANES), row),
            pl.BlockSpec((1, d), fixed),
            pl.BlockSpec((1, d), fixed),
        ],
        out_specs=pl.BlockSpec((tm, d), row),
        out_shape=jax.ShapeDtypeStruct((t, d), F32),
        compiler_params=_cparams(("arbitrary",)),
        name="moe_combine_ln",
    )(x1, mod, yg, info, ln_g.reshape(1, d), ln_b.reshape(1, d))


def _sc_workers():
    info = plsc.get_sparse_core_info()
    return info.num_cores, info.num_cores * info.num_subcores


def _sc_scatter_rows(rows, idx, n_out):
    t, w = rows.shape
    kk = idx.shape[0]
    n_cores, n_workers = _sc_workers()
    ch = SC_CHUNK
    assert t % (2 * n_workers * ch) == 0
    n_chunk = t // (n_workers * ch)
    idx_c = jnp.transpose(idx.reshape(kk, t // ch, ch), (1, 0, 2))

    @functools.partial(
        pl.kernel,
        mesh=plsc.VectorSubcoreMesh(core_axis_name="c", subcore_axis_name="s"),
        out_type=jax.ShapeDtypeStruct((n_out, w), rows.dtype),
        scratch_types=[pltpu.VMEM((2, kk, ch), jnp.int32), pltpu.VMEM((2, ch, w), rows.dtype),
                       pltpu.SemaphoreType.DMA((2,)), pltpu.SemaphoreType.DMA((2,))],
        name="sc_dispatch_scatter",
    )
    def scatter_kernel(rows_hbm, idx_hbm, out_hbm, idx_v, rows_v, load_sem, scat_sem):
        base = (lax.axis_index("s") * n_cores + lax.axis_index("c")) * n_chunk

        def load(j, b):
            return pltpu.make_async_copy(rows_hbm.at[pl.ds((base + j) * ch, ch)], rows_v.at[b], load_sem.at[b])

        def scatters(b):
            return [pltpu.make_async_copy(rows_v.at[b], out_hbm.at[idx_v.at[b, q]], scat_sem.at[b])
                    for q in range(kk)]

        pltpu.sync_copy(idx_hbm.at[base], idx_v.at[0])
        load(0, 0).start()

        @pl.loop(0, n_chunk, step=2)
        def _(j0):
            for b in range(2):
                j = j0 + b
                other = 1 - b

                @pl.when(j >= 1)
                def _():
                    for cp in scatters(other):
                        cp.wait()

                @pl.when(j + 1 < n_chunk)
                def _():
                    pltpu.sync_copy(idx_hbm.at[base + j + 1], idx_v.at[other])
                    load(j + 1, other).start()

                load(j, b).wait()
                for cp in scatters(b):
                    cp.start()

        for cp in scatters((n_chunk - 1) % 2):
            cp.wait()

    return scatter_kernel(rows, idx_c)


def _sc_gather_rows(table, idx):
    m = idx.shape[0]
    w = table.shape[1]
    n_cores, n_workers = _sc_workers()
    ch = SC_CHUNK
    assert m % (2 * n_workers * ch) == 0
    n_chunk = m // (n_workers * ch)
    idx_c = idx.reshape(m // ch, 1, ch)

    @functools.partial(
        pl.kernel,
        mesh=plsc.VectorSubcoreMesh(core_axis_name="c", subcore_axis_name="s"),
        out_type=jax.ShapeDtypeStruct((m, w), table.dtype),
        scratch_types=[pltpu.VMEM((2, 1, ch), jnp.int32), pltpu.VMEM((2, ch, w), table.dtype),
                       pltpu.SemaphoreType.DMA((2,)), pltpu.SemaphoreType.DMA((2,))],
        name="sc_combine_gather",
    )
    def gather_kernel(table_hbm, idx_hbm, out_hbm, idx_v, rows_v, gather_sem, write_sem):
        base = (lax.axis_index("s") * n_cores + lax.axis_index("c")) * n_chunk

        def gather(b):
            return pltpu.make_async_copy(table_hbm.at[idx_v.at[b, 0]], rows_v.at[b], gather_sem.at[b])

        def write(j, b):
            return pltpu.make_async_copy(rows_v.at[b], out_hbm.at[pl.ds((base + j) * ch, ch)], write_sem.at[b])

        pltpu.sync_copy(idx_hbm.at[base], idx_v.at[0])
        gather(0).start()

        @pl.loop(0, n_chunk, step=2)
        def _(j0):
            for b in range(2):
                j = j0 + b
                other = 1 - b

                @pl.when(j >= 1)
                def _():
                    write(j - 1, other).wait()

                @pl.when(j + 1 < n_chunk)
                def _():
                    pltpu.sync_copy(idx_hbm.at[base + j + 1], idx_v.at[other])
                    gather(other).start()

                gather(b).wait()
                write(j, b).start()

        write(n_chunk - 1, (n_chunk - 1) % 2).wait()

    return gather_kernel(table, idx_c)


def _lane_vector(vals, offset):
    return jnp.zeros((1, LANES), F32).at[0, offset:offset + vals.shape[0]].set(vals.astype(F32))


def _moe(h2, info, er, cnt, x1, mod, ln_g, ln_b, w_up, b_up, w_down, b_down, layer, seq):
    t, dh = h2.shape
    a = t * TOP_K
    bm = EXPERT_BLOCK
    slots, pcum_v = _slots(er, cnt)
    pcum = pcum_v[0, :N_EXPERTS]
    dest = slots[:TOP_K]
    n_blocks = -(-a // bm) + N_EXPERTS
    starts = jnp.arange(n_blocks, dtype=jnp.int32) * bm
    block_e = jnp.minimum(jnp.sum(pcum[None, :] <= starts[:, None], axis=1), N_EXPERTS - 1).astype(jnp.int32)
    n_used = (pcum[-1] // bm).astype(jnp.int32).reshape(1)
    later = block_e[None, :] > block_e[:, None]
    group_end = n_blocks - jnp.sum(later, axis=1)
    next_e = jnp.min(jnp.where(later, block_e[None, :], N_EXPERTS), axis=1)
    next_e = jnp.where(group_end < n_used[0], next_e, -1).astype(jnp.int32)
    counts = cnt[0, :N_EXPERTS].astype(jnp.int32)
    pstart = pcum - ((counts + bm - 1) // bm) * bm
    mine = block_e[:, None] == jnp.arange(N_EXPERTS, dtype=jnp.int32)[None, :]
    count_b = jnp.sum(jnp.where(mine, counts[None, :], 0), axis=1)
    pstart_b = jnp.sum(jnp.where(mine, pstart[None, :], 0), axis=1)
    valid = jnp.clip(count_b - (starts - pstart_b), 0, bm).astype(jnp.int32)
    xbuf = _sc_scatter_rows(h2, dest, n_blocks * bm)
    ybuf = _expert_ffn(xbuf, block_e, next_e, valid, n_used, w_up, b_up, w_down, b_down, layer)
    yg = _sc_gather_rows(ybuf, dest.reshape(a)).reshape(TOP_K, t, dh)
    return _combine(x1, mod, yg, info, ln_g, ln_b, seq)


def kernel(x, c, rel_bias, w_in, w_out, w_ada, b_ada, ln1_g, ln1_b, ln2_g, ln2_b, pool_w, pool_scale,
           attn_sinks, conv_w, gdn_a_log, gdn_dt_bias, gdn_norm_w, router_w, router_b,
           exp_w_up, exp_b_up, exp_w_down, exp_b_down):
    bsz, seq, d = x.shape
    depth = w_in.shape[0]
    t = bsz * seq
    assert d == D_MODEL and w_in.shape[2] == IN_DIM
    assert seq % GDN_SUPER == 0 and seq % (ATT_BLOCKS * WINDOW) == 0
    assert t % ROW_TILE == 0 and seq % ROW_TILE == 0

    mod_all = _modulation(c, w_ada, b_ada).reshape(depth, bsz, 6, d)
    bias = _band_bias(rel_bias)

    w_up_all = exp_w_up.reshape((depth * N_EXPERTS,) + exp_w_up.shape[2:])
    b_up_all = exp_b_up.reshape(depth * N_EXPERTS, 1, exp_b_up.shape[2])
    w_down_all = exp_w_down.reshape((depth * N_EXPERTS,) + exp_w_down.shape[2:])
    b_down_all = exp_b_down.reshape(depth * N_EXPERTS, 1, exp_b_down.shape[2])

    x2d = x.reshape(t, d)
    for l in range(depth):
        mod = mod_all[l]
        w_out_p = _take_static(w_out[l], _OUT_PERM, 0).astype(BF16)
        ident = jnp.zeros((CONV_WIDTH, 1), F32).at[CONV_WIDTH - 1, 0].set(1.0)
        conv_p = jnp.where(jnp.asarray(_GDN_CONV_SRC >= 0), _take_cols(conv_w[l].astype(F32), _GDN_CONV_SRC),
                           ident)
        pool_bd = jnp.zeros((POOL_DIM, POOL_DIM), F32)
        for gi in range(len(POOL_WINDOWS)):
            sl = slice(gi * POOL_GROUP, (gi + 1) * POOL_GROUP)
            pool_bd = pool_bd.at[sl, sl].set(pool_w[l, gi].astype(F32))
        alog_v = _lane_vector(gdn_a_log[l], N_GDN_HEADS)
        dtb_v = _lane_vector(gdn_dt_bias[l], N_GDN_HEADS)
        nw_v = jnp.tile(gdn_norm_w[l].astype(F32), 2).reshape(1, LANES)
        rw = jnp.zeros((d, LANES), F32).at[:, :N_EXPERTS].set(router_w[l].astype(F32))
        rw_hi, rw_lo = _split_bf16(rw)
        rb = jnp.full((1, LANES), NEG_INF, F32).at[0, :N_EXPERTS].set(router_b[l].astype(F32))

        u_pool, aq, akv, gdn, ba = _in_projection(x2d, mod, w_in.astype(F32), l, seq)
        y_pool = _pool_mixer(u_pool, pool_bd.astype(BF16), pool_scale[l].astype(F32), seq)
        y_att = _swa_attention(aq, akv, bias, attn_sinks[l].astype(F32), seq)
        y_gdn = _gdn_mixer(gdn, ba, conv_p, alog_v, dtb_v, nw_v, seq)
        x1, h2, info, er, cnt = _out_projection(x2d, mod, y_pool, y_att, y_gdn, w_out_p, ln1_g[l], ln1_b[l],
                                            rw_hi, rw_lo, rb, seq)
        x2d = _moe(h2, info, er, cnt, x1, mod, ln2_g[l], ln2_b[l], w_up_all, b_up_all, w_down_all, b_down_all,
                   l, seq)
    return x2d.reshape(bsz, seq, d)
```

```python
import functools

import numpy as np
import jax
import jax.numpy as jnp
from jax import lax
from jax.experimental import pallas as pl
from jax.experimental.pallas import tpu as pltpu
from jax.experimental.pallas import tpu_sc as plsc

F32 = jnp.float32
BF16 = jnp.bfloat16

D_MODEL = 1024
HEAD_DIM = 64
POOL_DIM = 256
POOL_WINDOWS = (2, 4, 8, 16)
POOL_GROUP = 64
N_ATT_HEADS = 6
N_KV_HEADS = 2
ATT_DIM = 384
KV_DIM = 128
WINDOW = 128
N_BUCKETS = 32
MAX_DISTANCE = 128
N_GDN_HEADS = 6
GDN_DIM = 384
CONV_WIDTH = 4
GDN_CHUNK = 64
N_EXPERTS = 32
TOP_K = 4
EXPERT_DIM = 1024
SWIGLU_ALPHA = 1.702
SWIGLU_LIMIT = 7.0
DEPTH = 2
DEEPNORM_ALPHA = (2 * DEPTH) ** 0.25
LN_EPS = 1e-5
NORM_EPS = 1e-6
NEG_INF = -1e30

LANES = 128
SUBLANES = 8
VMEM_LIMIT = 56 * 1024 * 1024

ROW_TILE = 512
IN_TILE = 1024
ATT_BLOCKS = 4
ATT_GROUP = 6
GDN_SUPER = 256
EXPERT_BLOCK = 512
SC_CHUNK = 64

_OFF_AQ = POOL_DIM
_OFF_AK = _OFF_AQ + ATT_DIM
_OFF_AV = _OFF_AK + KV_DIM
_OFF_GQ = _OFF_AV + KV_DIM
_OFF_GK = _OFF_GQ + GDN_DIM
_OFF_GV = _OFF_GK + GDN_DIM
_OFF_GZ = _OFF_GV + GDN_DIM
_OFF_GB = _OFF_GZ + GDN_DIM
_OFF_GA = _OFF_GB + N_GDN_HEADS
IN_DIM = _OFF_GA + N_GDN_HEADS

P_POOL = (0, POOL_DIM)
P_Q = (P_POOL[1], P_POOL[1] + ATT_DIM)
P_KV = (P_Q[1], P_Q[1] + 2 * KV_DIM)
P_GDN = (P_KV[1], P_KV[1] + 4 * GDN_DIM)
P_BA = (P_GDN[1], P_GDN[1] + LANES)
P_TOTAL = P_BA[1]


def _head_cols(off, h):
    return list(range(off + HEAD_DIM * h, off + HEAD_DIM * (h + 1)))


def _build_in_perm():
    cols = list(range(POOL_DIM))
    for p in range(N_ATT_HEADS // 2):
        cols += _head_cols(_OFF_AQ, p) + _head_cols(_OFF_AQ, p + 3)
    cols += list(range(_OFF_AK, _OFF_AK + 2 * KV_DIM))
    gdn_src = []
    for p in range(N_GDN_HEADS // 2):
        e, o = 2 * p, 2 * p + 1
        grp = (_head_cols(_OFF_GK, e) + _head_cols(_OFF_GQ, e)
               + _head_cols(_OFF_GQ, o) + _head_cols(_OFF_GK, o)
               + _head_cols(_OFF_GV, o) + _head_cols(_OFF_GV, e)
               + _head_cols(_OFF_GZ, o) + _head_cols(_OFF_GZ, e))
        cols += grp
        gdn_src += [c - _OFF_GQ if c < _OFF_GZ else -1 for c in grp]
    cols += list(range(_OFF_GB, _OFF_GB + 2 * N_GDN_HEADS))
    cols += [-1] * (LANES - 2 * N_GDN_HEADS)
    assert len(cols) == P_TOTAL
    return np.asarray(cols, np.int32), np.asarray(gdn_src, np.int32)


_IN_PERM, _GDN_CONV_SRC = _build_in_perm()


def _build_out_perm():
    rows = list(range(POOL_DIM))
    for p in range(N_ATT_HEADS // 2):
        rows += _head_cols(POOL_DIM, p) + _head_cols(POOL_DIM, p + 3)
    for p in range(N_GDN_HEADS // 2):
        rows += _head_cols(POOL_DIM + ATT_DIM, 2 * p + 1) + _head_cols(POOL_DIM + ATT_DIM, 2 * p)
    return np.asarray(rows, np.int32)


_OUT_PERM = _build_out_perm()


def _t5_bucket_line():
    n = np.maximum(2 * WINDOW - 1 - np.arange(3 * WINDOW - 1), 0)
    max_exact = N_BUCKETS // 2
    nf = np.maximum(n, 1).astype(np.float32)
    large = max_exact + (np.log(nf / max_exact) / np.float32(np.log(MAX_DISTANCE / max_exact))
                         * (N_BUCKETS - max_exact)).astype(np.int32)
    large = np.minimum(large, N_BUCKETS - 1)
    return np.where(n < max_exact, n, large).astype(np.int32)


_BUCKET_LINE = _t5_bucket_line()


def _band_bias(rel_bias):
    n_line = 3 * WINDOW - 1
    line = jnp.take(rel_bias.astype(F32), jnp.asarray(_BUCKET_LINE), axis=0).T
    heads = line.shape[0]
    padded = jnp.concatenate([line, jnp.zeros((heads, 1), F32)], axis=1)
    skew = jnp.tile(padded, (1, WINDOW))[:, :WINDOW * n_line].reshape(heads, WINDOW, n_line)
    return skew[:, :, WINDOW - 1:3 * WINDOW - 1]


def _take_static(w, perm, axis):
    parts = []
    start = 0
    for i in range(1, len(perm) + 1):
        run_ends = (i == len(perm) or ((perm[i] < 0) != (perm[i - 1] < 0))
                    or (perm[i] >= 0 and perm[i] != perm[i - 1] + 1))
        if run_ends:
            if perm[start] < 0:
                shape = list(w.shape)
                shape[axis] = i - start
                parts.append(jnp.zeros(shape, w.dtype))
            else:
                parts.append(lax.slice_in_dim(w, int(perm[start]), int(perm[start]) + (i - start), axis=axis))
            start = i
    return jnp.concatenate(parts, axis=axis)


def _take_cols(w, perm):
    return _take_static(w, perm, w.ndim - 1)


def _split_bf16(x):
    hi = x.astype(BF16)
    lo = (x - hi.astype(F32)).astype(BF16)
    return hi, lo


def _pack_bf16_pairs(x):
    n = x.shape[1] // 2
    bits = pltpu.bitcast(x.astype(BF16).astype(F32), jnp.int32)
    return lax.shift_right_logical(bits[:, :n], 16) | bits[:, n:]


def _unpack_bf16_pairs(u):
    lo = pltpu.bitcast(lax.shift_left(u, 16), F32)
    hi = pltpu.bitcast(u & jnp.int32(-65536), F32)
    return jnp.concatenate([lo, hi], axis=1)


def _dot(a, b):
    return jnp.dot(a, b, preferred_element_type=F32)


def _dot_nt(a, b):
    return lax.dot_general(a, b, (((1,), (1,)), ((), ())), preferred_element_type=F32)


def _dot_hi_exact_rhs(x, m_bf16):
    hi, lo = _split_bf16(x)
    return _dot(hi, m_bf16) + _dot(lo, m_bf16)


def _sigmoid(x):
    return 1.0 / (1.0 + jnp.exp(-x))


def _layer_norm(r, g, b):
    mu = jnp.mean(r, axis=-1, keepdims=True)
    d = r - mu
    var = jnp.mean(d * d, axis=-1, keepdims=True)
    return d * lax.rsqrt(var + LN_EPS) * g + b


def _cparams(sem):
    return pltpu.CompilerParams(dimension_semantics=sem, vmem_limit_bytes=VMEM_LIMIT)


def _mod_kernel(c_ref, w_ref, b_ref, o_ref):
    c = c_ref[...]
    ca = c * _sigmoid(c)
    ch, cl = _split_bf16(ca)
    wh, wl = _split_bf16(w_ref[0])
    o_ref[0] = _dot(ch, wh) + _dot(cl, wh) + _dot(ch, wl) + b_ref[0]


def _modulation(c, w_ada, b_ada):
    depth, d, n = w_ada.shape
    bsz = c.shape[0]
    tn = 512
    return pl.pallas_call(
        _mod_kernel,
        grid=(depth, n // tn),
        in_specs=[
            pl.BlockSpec((bsz, d), lambda l, j: (0, 0)),
            pl.BlockSpec((1, d, tn), lambda l, j: (l, 0, j)),
            pl.BlockSpec((1, 1, tn), lambda l, j: (l, 0, j)),
        ],
        out_specs=pl.BlockSpec((1, bsz, tn), lambda l, j: (l, 0, j)),
        out_shape=jax.ShapeDtypeStruct((depth, bsz, n), F32),
        compiler_params=_cparams(("arbitrary", "arbitrary")),
        name="adaln_mod",
    )(c, w_ada, b_ada.reshape(depth, 1, n))


def _perm_runs(perm):
    runs = []
    start = 0
    for i in range(1, len(perm) + 1):
        run_ends = (i == len(perm) or ((perm[i] < 0) != (perm[i - 1] < 0))
                    or (perm[i] >= 0 and perm[i] != perm[i - 1] + 1))
        if run_ends:
            runs.append((int(perm[start]) if perm[start] >= 0 else -1, i - start, start))
            start = i
    return runs


_IN_RUNS = _perm_runs(_IN_PERM)


def _inproj_kernel(layer, x_ref, mod_ref, wt_hbm, pool_ref, q_ref, kv_ref, gdn_ref, ba_ref,
                   wt_f32, tail_ref, wt_ref, sem):
    @pl.when(pl.program_id(0) == 0)
    def _():
        n_real = wt_hbm.shape[0]
        pad0 = (n_real // SUBLANES) * SUBLANES
        bulk = pltpu.make_async_copy(wt_hbm.at[pl.ds(0, pad0), layer, :], wt_f32.at[pl.ds(0, pad0)], sem.at[0])
        tail = pltpu.make_async_copy(wt_hbm.at[pl.ds(n_real - SUBLANES, SUBLANES), layer, :], tail_ref, sem.at[1])
        bulk.start()
        tail.start()
        bulk.wait()
        tail.wait()
        row8 = lax.broadcasted_iota(jnp.int32, tail_ref.shape, 0)
        left = n_real - pad0
        wt_f32[pad0:pad0 + SUBLANES, :] = jnp.where(row8 < left, pltpu.roll(tail_ref[...], left, axis=0), 0.0)
        wt_f32[pad0 + SUBLANES:, :] = jnp.zeros((wt_f32.shape[0] - pad0 - SUBLANES, wt_f32.shape[1]), F32)
        for src, n, dst in _IN_RUNS:
            if src >= 0:
                rows = n if src + n < n_real else wt_ref.shape[0] - dst
                wt_ref[dst:dst + rows, :] = wt_f32[src:src + rows, :].astype(BF16)
        q_rows = wt_ref[P_Q[0]:P_Q[1], :].astype(F32) * (HEAD_DIM ** -0.5)
        wt_ref[P_Q[0]:P_Q[1], :] = q_rows.astype(BF16)

    sh = mod_ref[0, 0:1, :]
    sc = mod_ref[0, 1:2, :]
    h = (x_ref[...] * (1.0 + sc) + sh).astype(BF16)

    def mm(rng):
        return _dot_nt(h, wt_ref[rng[0]:rng[1], :])

    pool_ref[...] = mm(P_POOL)
    q_ref[...] = mm(P_Q).astype(BF16)
    kv_ref[...] = mm(P_KV).astype(BF16)
    gdn_ref[...] = mm(P_GDN)
    ba_ref[...] = mm(P_BA)


def _in_projection(x2d, mod, w_in, layer, seq):
    t, d = x2d.shape
    tm = min(IN_TILE, seq)
    widths = [r[1] - r[0] for r in (P_POOL, P_Q, P_KV, P_GDN, P_BA)]
    dtypes = [F32, BF16, BF16, F32, F32]
    return pl.pallas_call(
        functools.partial(_inproj_kernel, layer),
        grid=(t // tm,),
        in_specs=[
            pl.BlockSpec((tm, d), lambda i: (i, 0)),
            pl.BlockSpec((1, 6, d), lambda i: ((i * tm) // seq, 0, 0)),
            pl.BlockSpec(memory_space=pl.ANY),
        ],
        out_specs=[pl.BlockSpec((tm, w), lambda i: (i, 0)) for w in widths],
        out_shape=[jax.ShapeDtypeStruct((t, w), dt) for w, dt in zip(widths, dtypes)],
        scratch_shapes=[pltpu.VMEM((P_TOTAL, d), F32), pltpu.VMEM((SUBLANES, d), F32),
                        pltpu.VMEM((P_TOTAL, d), BF16), pltpu.SemaphoreType.DMA((2,))],
        compiler_params=_cparams(("arbitrary",)),
        name="in_proj",
    )(x2d, mod, jnp.transpose(w_in, (2, 0, 1)))


def _pool_kernel(u_ref, w_ref, scale_ref, o_ref):
    u = u_ref[...]
    row = lax.broadcasted_iota(jnp.int32, u.shape, 0)
    lane = lax.broadcasted_iota(jnp.int32, u.shape, 1)

    def shifted(a, s):
        return jnp.where(row >= s, pltpu.roll(a, s, axis=0), 0.0)

    sums = []
    acc = u
    for wdt in POOL_WINDOWS:
        acc = acc + shifted(acc, wdt // 2)
        sums.append(acc)
    grp = lane // POOL_GROUP
    wsum = sums[-1]
    win = jnp.full(u.shape, POOL_WINDOWS[-1], jnp.int32)
    for gi in range(len(POOL_WINDOWS) - 2, -1, -1):
        wsum = jnp.where(grp == gi, sums[gi], wsum)
        win = jnp.where(grp == gi, POOL_WINDOWS[gi], win)
    cnt = jnp.minimum(row + 1, win).astype(F32)
    p = wsum / cnt - u
    y = _dot(p.astype(BF16), w_ref[...]) * scale_ref[...]
    o_ref[...] = y.astype(BF16)


def _pool_mixer(u, pool_w_bd, pool_scale, seq):
    t, c = u.shape
    return pl.pallas_call(
        _pool_kernel,
        grid=(t // seq,),
        in_specs=[
            pl.BlockSpec((seq, c), lambda b: (b, 0)),
            pl.BlockSpec((c, c), lambda b: (0, 0)),
            pl.BlockSpec((1, c), lambda b: (0, 0)),
        ],
        out_specs=pl.BlockSpec((seq, c), lambda b: (b, 0)),
        out_shape=jax.ShapeDtypeStruct((t, c), BF16),
        compiler_params=_cparams(("arbitrary",)),
        name="pool_mixer",
    )(u, pool_w_bd, pool_scale.reshape(1, c))


def _attn_kernel(sink_ref, q_ref, kvc_ref, kvp_ref, bias_ref, o_ref):
    step = pl.program_id(1)
    qi = lax.broadcasted_iota(jnp.int32, (WINDOW, 2 * WINDOW), 0)
    kj = lax.broadcasted_iota(jnp.int32, (WINDOW, 2 * WINDOW), 1)
    dist = qi + WINDOW - kj
    in_band = (dist >= 0) & (dist < WINDOW)
    lo = lax.broadcasted_iota(jnp.int32, (WINDOW, LANES), 1) < HEAD_DIM
    heads = [(p, half) for p in range(N_ATT_HEADS // 2) for half in range(2)]
    sinks = [sink_ref[p + 3 * half] for p, half in heads]
    for sub in range(ATT_BLOCKS):
        r0 = sub * WINDOW
        prev = kvp_ref[...] if sub == 0 else kvc_ref[r0 - WINDOW:r0, :]
        kv = jnp.concatenate([prev, kvc_ref[r0:r0 + WINDOW, :]], axis=0)
        k = kv[:, :KV_DIM]
        v = kv[:, KV_DIM:]
        valid = in_band & ((kj >= WINDOW) | (step > 0)) if sub == 0 else in_band
        for g0 in range(0, len(heads), ATT_GROUP):
            group = heads[g0:g0 + ATT_GROUP]
            sk_g = sinks[g0:g0 + ATT_GROUP]
            scores = []
            for p, half in group:
                qp = q_ref[r0:r0 + WINDOW, p * LANES:(p + 1) * LANES]
                qm = jnp.where(lo if half == 0 else jnp.logical_not(lo), qp, jnp.zeros_like(qp))
                scores.append(jnp.where(valid, _dot_nt(qm, k) + bias_ref[p + 3 * half], NEG_INF))
            tops = [jnp.maximum(jnp.max(s, axis=-1, keepdims=True), sk) for s, sk in zip(scores, sk_g)]
            probs = [jnp.exp(s - m) for s, m in zip(scores, tops)]
            dens = [jnp.sum(pr, axis=-1, keepdims=True) + jnp.exp(sk - m)
                    for pr, sk, m in zip(probs, sk_g, tops)]
            outs = [_dot(pr.astype(BF16), v) / den for pr, den in zip(probs, dens)]
            for idx in range(0, len(group), 2):
                p = group[idx][0]
                o_ref[r0:r0 + WINDOW, p * LANES:(p + 1) * LANES] = (
                    jnp.where(lo, outs[idx], outs[idx + 1]).astype(BF16))


def _swa_attention(q, kv, bias, sinks, seq):
    t = q.shape[0]
    rows = ATT_BLOCKS * WINDOW
    nblk = seq // rows
    return pl.pallas_call(
        _attn_kernel,
        grid=(t // seq, nblk),
        in_specs=[
            pl.BlockSpec(memory_space=pltpu.SMEM),
            pl.BlockSpec((rows, ATT_DIM), lambda b, n: (b * nblk + n, 0)),
            pl.BlockSpec((rows, 2 * KV_DIM), lambda b, n: (b * nblk + n, 0)),
            pl.BlockSpec((WINDOW, 2 * KV_DIM),
                         lambda b, n: (jnp.maximum((b * nblk + n) * ATT_BLOCKS - 1, 0), 0)),
            pl.BlockSpec((N_ATT_HEADS, WINDOW, 2 * WINDOW), lambda b, n: (0, 0, 0)),
        ],
        out_specs=pl.BlockSpec((rows, ATT_DIM), lambda b, n: (b * nblk + n, 0)),
        out_shape=jax.ShapeDtypeStruct((t, ATT_DIM), BF16),
        compiler_params=_cparams(("arbitrary", "arbitrary")),
        name="swa_attention",
    )(sinks, q, kv, kv, bias)


_GDN_BASE = SUBLANES
_GDN_LEVELS = int(np.log2(GDN_CHUNK // _GDN_BASE))


def _gdn_masks():
    r = np.arange(GDN_SUPER)
    ri, ci = r[:, None], r[None, :]
    same_chunk = (ri // GDN_CHUNK) == (ci // GDN_CHUNK)
    incl = same_chunk & (ri >= ci)
    planes = [incl, ri == ci]
    base = ((ri // _GDN_BASE) == (ci // _GDN_BASE)) & (ri > ci)
    planes.append(base)
    for lvl in range(_GDN_LEVELS):
        small = _GDN_BASE << lvl
        planes.append(((ri // (2 * small)) == (ci // (2 * small))) & ((ri // small) != (ci // small)) & (ri > ci))
    bmask = np.stack(planes).astype(np.float32)
    bmask[2] = -bmask[2]
    negmask = np.where(incl, 0.0, -np.inf).astype(np.float32)
    return negmask, bmask


_GDN_NEGMASK, _GDN_BMASK = _gdn_masks()


def _gdn_kernel(x_ref, halo_ref, ba_ref, cw_ref, alog_ref, dtb_ref, nw_ref, negmask_ref, bmask_ref,
                y_ref, state_ref, xs_ref):
    sc_id = pl.program_id(1)
    rows = GDN_SUPER
    nchunk = rows // GDN_CHUNK
    c_sz = GDN_CHUNK

    @pl.when(sc_id == 0)
    def _():
        state_ref[...] = jnp.zeros_like(state_ref)

    xs_ref[:SUBLANES, :] = jnp.where(sc_id == 0, 0.0, halo_ref[...])
    xs_ref[SUBLANES:, :] = x_ref[...]
    act = []
    for g in range(x_ref.shape[1] // LANES):
        cols = slice(g * LANES, (g + 1) * LANES)
        acc = x_ref[:, cols] * cw_ref[CONV_WIDTH - 1:CONV_WIDTH, cols]
        if g % 4 != 3:
            for s in range(1, CONV_WIDTH):
                acc = acc + (xs_ref[SUBLANES - s:SUBLANES - s + rows, cols]
                             * cw_ref[CONV_WIDTH - 1 - s:CONV_WIDTH - s, cols])
        act.append(acc * _sigmoid(acc))

    negmask = negmask_ref[...]
    tri_incl = bmask_ref[0]
    eye_b = bmask_ref[1]
    base_neg = bmask_ref[2]
    bands = [bmask_ref[3 + lvl] for lvl in range(_GDN_LEVELS)]
    li = lax.broadcasted_iota(jnp.int32, (LANES, LANES), 0)
    lj = lax.broadcasted_iota(jnp.int32, (LANES, LANES), 1)
    half_ones = jnp.where((li // HEAD_DIM) == (lj // HEAD_DIM), 1.0, 0.0).astype(BF16)
    lane_lo = lax.broadcasted_iota(jnp.int32, (rows, LANES), 1) < HEAD_DIM
    lane_lo_c = lax.broadcasted_iota(jnp.int32, (c_sz, LANES), 1) < HEAD_DIM

    ba = ba_ref[...]
    beta_all = _sigmoid(ba)
    sp_in = ba + dtb_ref[...]
    softplus = jnp.maximum(sp_in, 0.0) + jnp.log(1.0 + jnp.exp(-jnp.abs(sp_in)))
    g_all = -jnp.exp(alog_ref[...]) * softplus
    gcum = _dot_hi_exact_rhs_lhs(tri_incl, g_all)
    gcum_t = gcum.T

    heads = range(N_GDN_HEADS)
    lane_hi = jnp.logical_not(lane_lo)
    lane_hi_c = jnp.logical_not(lane_lo_c)
    mk = [lane_lo if h % 2 == 0 else lane_hi for h in heads]
    mk_c = [lane_lo_c if h % 2 == 0 else lane_hi_c for h in heads]
    scale = HEAD_DIM ** -0.5

    def bdot(a, b):
        return _dot(a.astype(BF16), b.astype(BF16))

    xk, xq, gn, gc_col, beta, eg = [], [], [], [], [], []
    for h in heads:
        g = act[4 * (h // 2) + (h % 2)]
        g = g * lax.rsqrt(_dot_hi_exact_rhs(g * g, half_ones) + NORM_EPS)
        gn.append(g)
        xk.append(jnp.where(mk[h], g, 0.0))
        xq.append(jnp.where(mk[h], pltpu.roll(g, HEAD_DIM, axis=1), 0.0) * scale)
        beta.append(beta_all[:, h:h + 1])
        gc_col.append(gcum[:, N_GDN_HEADS + h:N_GDN_HEADS + h + 1])
        eg.append(jnp.exp(gc_col[h]))

    l_b, attn, rhs = [], [], []
    for h in heads:
        gc_row = gcum_t[N_GDN_HEADS + h:N_GDN_HEADS + h + 1, :]
        decay = jnp.exp(gc_col[h] - gc_row + negmask)
        xk_b = xk[h].astype(BF16)
        kk = _dot_nt((xk[h] * beta[h]).astype(BF16), xk_b)
        l_b.append((kk * decay).astype(BF16))
        attn.append((_dot_nt(xq[h].astype(BF16), xk_b) * decay).astype(BF16))
        vv = act[4 * (h // 2) + 2]
        rhs.append(jnp.where(mk[h], gn[h] * eg[h], vv) * beta[h])

    a1 = [l_b[h] * base_neg for h in heads]
    a2 = [_dot(a1[h], a1[h]).astype(BF16) for h in heads]
    a4 = [_dot(a2[h], a2[h]).astype(BF16) for h in heads]
    inv0 = [eye_b + a1[h] for h in heads]
    acc1 = [inv0[h].astype(F32) + _dot(a2[h], inv0[h]) for h in heads]
    inv_b = [(acc1[h] + _dot(a4[h], acc1[h].astype(BF16))).astype(BF16) for h in heads]
    for lvl in range(_GDN_LEVELS - 1):
        mid = [_dot(l_b[h] * bands[lvl], inv_b[h]).astype(BF16) for h in heads]
        inv_b = [inv_b[h] - _dot(inv_b[h], mid[h]).astype(BF16) for h in heads]
    half = [_dot(inv_b[h], rhs[h].astype(BF16)) for h in heads]
    mid = [_dot(l_b[h] * bands[_GDN_LEVELS - 1], half[h].astype(BF16)) for h in heads]
    sol = [half[h] - _dot(inv_b[h], mid[h].astype(BF16)) for h in heads]

    lane_lo_s = lax.broadcasted_iota(jnp.int32, (LANES, LANES), 1) < HEAD_DIM
    mk_s = [lane_lo_s if h % 2 == 0 else jnp.logical_not(lane_lo_s) for h in heads]
    sol_b = [sol[h].astype(BF16) for h in heads]
    attn_sol = [_dot(attn[h], sol_b[h]) for h in heads]
    q_eff = [(xq[h] * eg[h] - jnp.where(mk[h], attn_sol[h], 0.0)).astype(BF16) for h in heads]
    o_free = [jnp.where(mk[h], 0.0, attn_sol[h]) for h in heads]
    kw = [[] for _ in heads]
    ku = [[] for _ in heads]
    cdec = [[] for _ in heads]
    for c in range(nchunk):
        r0 = c * c_sz
        for h in heads:
            glast = gcum[r0 + c_sz - 1:r0 + c_sz, N_GDN_HEADS + h:N_GDN_HEADS + h + 1]
            kd_t = (xk[h][r0:r0 + c_sz] * jnp.exp(glast - gc_col[h][r0:r0 + c_sz])).T
            both = _dot(kd_t.astype(BF16), sol_b[h][r0:r0 + c_sz])
            kw[h].append(jnp.where(mk_s[h], both, 0.0).astype(BF16))
            ku[h].append(jnp.where(mk_s[h], 0.0, both))
            cdec[h].append(jnp.exp(glast))
    st = [state_ref[h] for h in heads]
    o_parts = [[] for _ in heads]
    for c in range(nchunk):
        r0 = c * c_sz
        for h in heads:
            lhs = jnp.concatenate([kw[h][c], q_eff[h][r0:r0 + c_sz]], axis=0)
            prod = _dot(lhs, st[h].astype(BF16))
            o_parts[h].append(prod[LANES:] + o_free[h][r0:r0 + c_sz])
            st[h] = st[h] * cdec[h][c] + ku[h][c] - prod[:LANES]
    for h in heads:
        state_ref[h] = st[h]

    for p in range(N_GDN_HEADS // 2):
        o_pair = [jnp.concatenate(o_parts[h], axis=0) for h in (2 * p, 2 * p + 1)]
        o = jnp.where(lane_lo, o_pair[1], o_pair[0])
        ms = _dot_hi_exact_rhs(o * o, half_ones) * (1.0 / HEAD_DIM)
        zz = act[4 * p + 3]
        y = o * lax.rsqrt(ms + NORM_EPS) * nw_ref[...] * zz
        y_ref[:, p * LANES:(p + 1) * LANES] = y.astype(BF16)


def _dot_hi_exact_rhs_lhs(m_bf16, x):
    hi, lo = _split_bf16(x)
    return _dot(m_bf16, hi) + _dot(m_bf16, lo)


def _gdn_mixer(gdn, ba, conv_p, alog_v, dtb_v, nw_v, seq):
    t, c = gdn.shape
    rows = GDN_SUPER
    nsc = seq // rows
    hb = rows // SUBLANES
    return pl.pallas_call(
        _gdn_kernel,
        grid=(t // seq, nsc),
        in_specs=[
            pl.BlockSpec((rows, c), lambda b, s: (b * nsc + s, 0)),
            pl.BlockSpec((SUBLANES, c), lambda b, s: (jnp.maximum((b * nsc + s) * hb - 1, 0), 0)),
            pl.BlockSpec((rows, LANES), lambda b, s: (b * nsc + s, 0)),
            pl.BlockSpec((CONV_WIDTH, c), lambda b, s: (0, 0)),
            pl.BlockSpec((1, LANES), lambda b, s: (0, 0)),
            pl.BlockSpec((1, LANES), lambda b, s: (0, 0)),
            pl.BlockSpec((1, LANES), lambda b, s: (0, 0)),
            pl.BlockSpec((rows, rows), lambda b, s: (0, 0)),
            pl.BlockSpec((3 + _GDN_LEVELS, rows, rows), lambda b, s: (0, 0, 0)),
        ],
        out_specs=pl.BlockSpec((rows, GDN_DIM), lambda b, s: (b * nsc + s, 0)),
        out_shape=jax.ShapeDtypeStruct((t, GDN_DIM), BF16),
        scratch_shapes=[pltpu.VMEM((N_GDN_HEADS, LANES, LANES), F32), pltpu.VMEM((rows + SUBLANES, c), F32)],
        compiler_params=_cparams(("arbitrary", "arbitrary")),
        name="gdn_mixer",
    )(gdn, gdn, ba, conv_p, alog_v, dtb_v, nw_v, jnp.asarray(_GDN_NEGMASK), jnp.asarray(_GDN_BMASK, BF16))


def _route_tile(logits, before, carry_ref, live):
    shape = logits.shape
    lane = lax.broadcasted_iota(jnp.int32, shape, 1).astype(F32)
    work = logits
    vals, idxs = [], []
    for _k in range(TOP_K):
        m = jnp.max(work, axis=-1, keepdims=True)
        idx = jnp.min(jnp.where(work == m, lane, float(LANES)), axis=-1, keepdims=True)
        vals.append(m)
        idxs.append(idx)
        work = jnp.where(lane == idx, -jnp.inf, work)
    exps = [jnp.exp(v - vals[0]) for v in vals]
    den = exps[0] + exps[1] + exps[2] + exps[3]
    onehots = [lane == idx for idx in idxs]
    member = jnp.zeros(shape, F32)
    for oh in onehots:
        member = member + jnp.where(oh, 1.0, 0.0)
    rank = _dot(before, member.astype(BF16)) + carry_ref[...]
    carry_ref[...] = carry_ref[...] + live * jnp.sum(member, axis=0, keepdims=True)
    info = jnp.zeros(shape, F32)
    for k in range(TOP_K):
        rank_k = jnp.sum(jnp.where(onehots[k], rank, 0.0), axis=-1, keepdims=True)
        info = jnp.where(lane == float(k), idxs[k], info)
        info = jnp.where(lane == float(TOP_K + k), rank_k, info)
        info = jnp.where(lane == float(2 * TOP_K + k), exps[k] / den, info)
    return info


def _outproj_kernel(x_ref, mod_ref, yp_ref, ya_ref, yg_ref, wp_ref, wa_ref, wg_ref, lng_ref, lnb_ref,
                    rwh_ref, rwl_ref, rb_ref, before_ref, x1_ref, h2_ref, info_ref, er_ref, cnt_ref,
                    carry_ref, logit_s):
    step = pl.program_id(0)

    @pl.when(step == 0)
    def _():
        carry_ref[...] = jnp.zeros_like(carry_ref)
        logit_s[...] = jnp.zeros_like(logit_s)

    y = _dot(yp_ref[...], wp_ref[...]) + _dot(ya_ref[...], wa_ref[...]) + _dot(yg_ref[...], wg_ref[...])
    live = jnp.where(step > 0, 1.0, 0.0)
    info = _route_tile(logit_s[...], before_ref[...], carry_ref, live)
    info_ref[...] = info
    er_ref[...] = info.T[:SUBLANES]
    cnt_ref[...] = carry_ref[...]
    g1 = mod_ref[0, 2:3, :]
    sh2 = mod_ref[0, 3:4, :]
    sc2 = mod_ref[0, 4:5, :]
    x1 = _layer_norm(DEEPNORM_ALPHA * x_ref[...] + g1 * y, lng_ref[...], lnb_ref[...])
    x1_ref[...] = x1
    h2 = x1 * (1.0 + sc2) + sh2
    hh, hl = _split_bf16(h2)
    h2_ref[...] = _pack_bf16_pairs(h2)
    logit_s[...] = _dot(hh, rwh_ref[...]) + _dot(hl, rwh_ref[...]) + _dot(hh, rwl_ref[...]) + rb_ref[...]


def _out_projection(x2d, mod, yp, ya, yg, w_out_p, ln_g, ln_b, rw_hi, rw_lo, rb, seq):
    t, d = x2d.shape
    tm = ROW_TILE
    wp = w_out_p[:POOL_DIM]
    wa = w_out_p[POOL_DIM:POOL_DIM + ATT_DIM]
    wg = w_out_p[POOL_DIM + ATT_DIM:]
    last = t // tm - 1
    row = lambda i: (jnp.minimum(i, last), 0)
    routed = lambda i: (jnp.maximum(i - 1, 0), 0)
    fixed = lambda i: (0, 0)
    return pl.pallas_call(
        _outproj_kernel,
        grid=(t // tm + 1,),
        in_specs=[
            pl.BlockSpec((tm, d), row),
            pl.BlockSpec((1, 6, d), lambda i: ((jnp.minimum(i, last) * tm) // seq, 0, 0)),
            pl.BlockSpec((tm, POOL_DIM), row),
            pl.BlockSpec((tm, ATT_DIM), row),
            pl.BlockSpec((tm, GDN_DIM), row),
            pl.BlockSpec((POOL_DIM, d), fixed),
            pl.BlockSpec((ATT_DIM, d), fixed),
            pl.BlockSpec((GDN_DIM, d), fixed),
            pl.BlockSpec((1, d), fixed),
            pl.BlockSpec((1, d), fixed),
            pl.BlockSpec((d, LANES), fixed),
            pl.BlockSpec((d, LANES), fixed),
            pl.BlockSpec((1, LANES), fixed),
            pl.BlockSpec((tm, tm), fixed),
        ],
        out_specs=[pl.BlockSpec((tm, d), row), pl.BlockSpec((tm, d // 2), row), pl.BlockSpec((tm, LANES), routed),
                   pl.BlockSpec((SUBLANES, tm), lambda i: (0, jnp.maximum(i - 1, 0))),
                   pl.BlockSpec((1, LANES), fixed)],
        out_shape=[jax.ShapeDtypeStruct((t, d), F32), jax.ShapeDtypeStruct((t, d // 2), jnp.int32),
                   jax.ShapeDtypeStruct((t, LANES), F32), jax.ShapeDtypeStruct((SUBLANES, t), F32),
                   jax.ShapeDtypeStruct((1, LANES), F32)],
        scratch_shapes=[pltpu.VMEM((1, LANES), F32), pltpu.VMEM((tm, LANES), F32)],
        compiler_params=_cparams(("arbitrary",)),
        name="out_proj_ln_route",
    )(x2d, mod, yp, ya, yg, wp, wa, wg, ln_g.reshape(1, d), ln_b.reshape(1, d), rw_hi, rw_lo, rb,
      jnp.tril(jnp.ones((tm, tm), BF16), -1))


def _slot_kernel(er_ref, cnt_ref, dest_ref, pcum_ref):
    cnt = jnp.broadcast_to(cnt_ref[...], (SUBLANES, LANES))
    padded = jnp.floor((cnt + float(EXPERT_BLOCK - 1)) * (1.0 / EXPERT_BLOCK)) * float(EXPERT_BLOCK)
    lane8 = lax.broadcasted_iota(jnp.int32, (SUBLANES, LANES), 1)
    acc = padded
    step = 1
    while step < LANES:
        acc = acc + jnp.where(lane8 >= step, pltpu.roll(acc, step, axis=1), 0.0)
        step *= 2
    pcum_ref[...] = acc[:1].astype(jnp.int32)
    pstart = acc - padded

    er = er_ref[...]
    start = jnp.zeros(er.shape, F32)
    for e in range(N_EXPERTS):
        offset = jnp.sum(jnp.where(lane8 == e, pstart, 0.0), axis=-1, keepdims=True)
        start = jnp.where(er == float(e), offset, start)
    row = lax.broadcasted_iota(jnp.int32, er.shape, 0)
    slots = jnp.where(row < TOP_K, start + pltpu.roll(er, TOP_K, axis=0), 0.0)
    dest_ref[...] = slots.astype(jnp.int32)


def _slots(er, cnt):
    t = er.shape[1]
    return pl.pallas_call(
        _slot_kernel,
        grid=(1,),
        in_specs=[pl.BlockSpec((SUBLANES, t), lambda i: (0, 0)), pl.BlockSpec((1, LANES), lambda i: (0, 0))],
        out_specs=[pl.BlockSpec((SUBLANES, t), lambda i: (0, 0)), pl.BlockSpec((1, LANES), lambda i: (0, 0))],
        out_shape=[jax.ShapeDtypeStruct((SUBLANES, t), jnp.int32), jax.ShapeDtypeStruct((1, LANES), jnp.int32)],
        compiler_params=_cparams(("arbitrary",)),
        name="moe_slots",
    )(er, cnt)


def _expert_kernel(e0, be_ref, nxt_ref, val_ref, nu_ref, x_ref, wup_hbm, bup_ref, wdn_hbm, bdn_ref, y_ref,
                   wup_st, wdn_st, wup_bf, wdn_bf, sems):
    i = pl.program_id(0)
    e = be_ref[i]
    prev = be_ref[jnp.maximum(i - 1, 0)]
    used = i < nu_ref[0]

    def weight_copies(expert):
        return (pltpu.make_async_copy(wup_hbm.at[e0 + expert], wup_st, sems.at[0]),
                pltpu.make_async_copy(wdn_hbm.at[e0 + expert], wdn_st, sems.at[1]))

    @pl.when(i == 0)
    def _():
        for cp in weight_copies(e):
            cp.start()

    @pl.when(used & ((i == 0) | (e != prev)))
    def _():
        for cp in weight_copies(e):
            cp.wait()
        wup_bf[...] = wup_st[...].astype(BF16)
        wdn_bf[...] = wdn_st[...].astype(BF16)

        @pl.when(nxt_ref[i] >= 0)
        def _():
            for cp in weight_copies(nxt_ref[i]):
                cp.start()

    def ffn(rows):
        xb = _unpack_bf16_pairs(x_ref[:rows, :]).astype(BF16)
        hb = _dot(xb, wup_bf[...]) + bup_ref[0]
        x_glu = jnp.minimum(hb[:, :EXPERT_DIM], SWIGLU_LIMIT)
        x_lin = jnp.clip(hb[:, EXPERT_DIM:], -SWIGLU_LIMIT, SWIGLU_LIMIT)
        act = x_glu * _sigmoid(SWIGLU_ALPHA * x_glu) * (x_lin + 1.0)
        y = _dot(act.astype(BF16), wdn_bf[...]) + bdn_ref[0]
        y_ref[:rows, :] = _pack_bf16_pairs(y)

    half_rows = x_ref.shape[0] // 2
    small = val_ref[i] <= half_rows

    @pl.when(used & jnp.logical_not(small))
    def _():
        ffn(x_ref.shape[0])

    @pl.when(used & small)
    def _():
        ffn(half_rows)
        y_ref[half_rows:, :] = jnp.zeros((x_ref.shape[0] - half_rows, y_ref.shape[1]), y_ref.dtype)

    @pl.when(i >= nu_ref[0])
    def _():
        y_ref[...] = jnp.zeros_like(y_ref)


def _expert_ffn(xbuf, block_e, next_e, valid, n_used, w_up, b_up, w_down, b_down, layer):
    p, dh = xbuf.shape
    d = 2 * dh
    bm = EXPERT_BLOCK
    ne, _, n_up = w_up.shape
    e0 = layer * N_EXPERTS
    grid_spec = pltpu.PrefetchScalarGridSpec(
        num_scalar_prefetch=4,
        grid=(p // bm,),
        in_specs=[
            pl.BlockSpec((bm, dh), lambda i, be, nx, vl, nu: (i, 0)),
            pl.BlockSpec(memory_space=pl.ANY),
            pl.BlockSpec((1, 1, n_up), lambda i, be, nx, vl, nu: (e0 + be[i], 0, 0)),
            pl.BlockSpec(memory_space=pl.ANY),
            pl.BlockSpec((1, 1, d), lambda i, be, nx, vl, nu: (e0 + be[i], 0, 0)),
        ],
        out_specs=pl.BlockSpec((bm, dh), lambda i, be, nx, vl, nu: (i, 0)),
        scratch_shapes=[pltpu.VMEM((d, n_up), F32), pltpu.VMEM((EXPERT_DIM, d), F32),
                        pltpu.VMEM((d, n_up), BF16), pltpu.VMEM((EXPERT_DIM, d), BF16),
                        pltpu.SemaphoreType.DMA((2,))],
    )
    return pl.pallas_call(
        functools.partial(_expert_kernel, e0),
        grid_spec=grid_spec,
        out_shape=jax.ShapeDtypeStruct((p, dh), jnp.int32),
        compiler_params=_cparams(("arbitrary",)),
        name="expert_ffn",
    )(block_e, next_e, valid, n_used, xbuf, w_up, b_up, w_down, b_down)


def _combine_kernel(x1_ref, mod_ref, yg_ref, info_ref, lng_ref, lnb_ref, o_ref):
    info = info_ref[...]
    y = jnp.zeros(x1_ref.shape, F32)
    for k in range(TOP_K):
        gate = info[:, 2 * TOP_K + k:2 * TOP_K + k + 1]
        y = y + gate * _unpack_bf16_pairs(yg_ref[k])
    g2 = mod_ref[0, 5:6, :]
    o_ref[...] = _layer_norm(DEEPNORM_ALPHA * x1_ref[...] + g2 * y, lng_ref[...], lnb_ref[...])


def _combine(x1, mod, yg, info, ln_g, ln_b, seq):
    t, d = x1.shape
    tm = min(IN_TILE, seq)
    row = lambda i: (i, 0)
    fixed = lambda i: (0, 0)
    return pl.pallas_call(
        _combine_kernel,
        grid=(t // tm,),
        in_specs=[
            pl.BlockSpec((tm, d), row),
            pl.BlockSpec((1, 6, d), lambda i: ((i * tm) // seq, 0, 0)),
            pl.BlockSpec((TOP_K, tm, d // 2), lambda i: (0, i, 0)),
            pl.BlockSpec((tm, LANES), row),
            pl.BlockSpec((1, d), fixed),
            pl.BlockSpec((1, d), fixed),
        ],
        out_specs=pl.BlockSpec((tm, d), row),
        out_shape=jax.ShapeDtypeStruct((t, d), F32),
        compiler_params=_cparams(("arbitrary",)),
        name="moe_combine_ln",
    )(x1, mod, yg, info, ln_g.reshape(1, d), ln_b.reshape(1, d))


def _sc_workers():
    info = plsc.get_sparse_core_info()
    return info.num_cores, info.num_cores * info.num_subcores


def _sc_scatter_rows(rows, idx, n_out):
    t, w = rows.shape
    kk = idx.shape[0]
    n_cores, n_workers = _sc_workers()
    ch = SC_CHUNK
    assert t % (2 * n_workers * ch) == 0
    n_chunk = t // (n_workers * ch)
    idx_c = jnp.transpose(idx.reshape(kk, t // ch, ch), (1, 0, 2))

    @functools.partial(
        pl.kernel,
        mesh=plsc.VectorSubcoreMesh(core_axis_name="c", subcore_axis_name="s"),
        out_type=jax.ShapeDtypeStruct((n_out, w), rows.dtype),
        scratch_types=[pltpu.VMEM((2, kk, ch), jnp.int32), pltpu.VMEM((2, ch, w), rows.dtype),
                       pltpu.SemaphoreType.DMA((2,)), pltpu.SemaphoreType.DMA((2,))],
        name="sc_dispatch_scatter",
    )
    def scatter_kernel(rows_hbm, idx_hbm, out_hbm, idx_v, rows_v, load_sem, scat_sem):
        base = (lax.axis_index("s") * n_cores + lax.axis_index("c")) * n_chunk

        def load(j, b):
            return pltpu.make_async_copy(rows_hbm.at[pl.ds((base + j) * ch, ch)], rows_v.at[b], load_sem.at[b])

        def scatters(b):
            return [pltpu.make_async_copy(rows_v.at[b], out_hbm.at[idx_v.at[b, q]], scat_sem.at[b])
                    for q in range(kk)]

        pltpu.sync_copy(idx_hbm.at[base], idx_v.at[0])
        load(0, 0).start()

        @pl.loop(0, n_chunk, step=2)
        def _(j0):
            for b in range(2):
                j = j0 + b
                other = 1 - b

                @pl.when(j >= 1)
                def _():
                    for cp in scatters(other):
                        cp.wait()

                @pl.when(j + 1 < n_chunk)
                def _():
                    pltpu.sync_copy(idx_hbm.at[base + j + 1], idx_v.at[other])
                    load(j + 1, other).start()

                load(j, b).wait()
                for cp in scatters(b):
                    cp.start()

        for cp in scatters((n_chunk - 1) % 2):
            cp.wait()

    return scatter_kernel(rows, idx_c)


def _sc_gather_rows(table, idx):
    m = idx.shape[0]
    w = table.shape[1]
    n_cores, n_workers = _sc_workers()
    ch = SC_CHUNK
    assert m % (2 * n_workers * ch) == 0
    n_chunk = m // (n_workers * ch)
    idx_c = idx.reshape(m // ch, 1, ch)

    @functools.partial(
        pl.kernel,
        mesh=plsc.VectorSubcoreMesh(core_axis_name="c", subcore_axis_name="s"),
        out_type=jax.ShapeDtypeStruct((m, w), table.dtype),
        scratch_types=[pltpu.VMEM((2, 1, ch), jnp.int32), pltpu.VMEM((2, ch, w), table.dtype),
                       pltpu.SemaphoreType.DMA((2,)), pltpu.SemaphoreType.DMA((2,))],
        name="sc_combine_gather",
    )
    def gather_kernel(table_hbm, idx_hbm, out_hbm, idx_v, rows_v, gather_sem, write_sem):
        base = (lax.axis_index("s") * n_cores + lax.axis_index("c")) * n_chunk

        def gather(b):
            return pltpu.make_async_copy(table_hbm.at[idx_v.at[b, 0]], rows_v.at[b], gather_sem.at[b])

        def write(j, b):
            return pltpu.make_async_copy(rows_v.at[b], out_hbm.at[pl.ds((base + j) * ch, ch)], write_sem.at[b])

        pltpu.sync_copy(idx_hbm.at[base], idx_v.at[0])
        gather(0).start()

        @pl.loop(0, n_chunk, step=2)
        def _(j0):
            for b in range(2):
                j = j0 + b
                other = 1 - b

                @pl.when(j >= 1)
                def _():
                    write(j - 1, other).wait()

                @pl.when(j + 1 < n_chunk)
                def _():
                    pltpu.sync_copy(idx_hbm.at[base + j + 1], idx_v.at[other])
                    gather(other).start()

                gather(b).wait()
                write(j, b).start()

        write(n_chunk - 1, (n_chunk - 1) % 2).wait()

    return gather_kernel(table, idx_c)


def _lane_vector(vals, offset):
    return jnp.zeros((1, LANES), F32).at[0, offset:offset + vals.shape[0]].set(vals.astype(F32))


def _moe(h2, info, er, cnt, x1, mod, ln_g, ln_b, w_up, b_up, w_down, b_down, layer, seq):
    t, dh = h2.shape
    a = t * TOP_K
    bm = EXPERT_BLOCK
    slots, pcum_v = _slots(er, cnt)
    pcum = pcum_v[0, :N_EXPERTS]
    dest = slots[:TOP_K]
    n_blocks = -(-a // bm) + N_EXPERTS
    starts = jnp.arange(n_blocks, dtype=jnp.int32) * bm
    block_e = jnp.minimum(jnp.sum(pcum[None, :] <= starts[:, None], axis=1), N_EXPERTS - 1).astype(jnp.int32)
    n_used = (pcum[-1] // bm).astype(jnp.int32).reshape(1)
    later = block_e[None, :] > block_e[:, None]
    group_end = n_blocks - jnp.sum(later, axis=1)
    next_e = jnp.min(jnp.where(later, block_e[None, :], N_EXPERTS), axis=1)
    next_e = jnp.where(group_end < n_used[0], next_e, -1).astype(jnp.int32)
    counts = cnt[0, :N_EXPERTS].astype(jnp.int32)
    pstart = pcum - ((counts + bm - 1) // bm) * bm
    mine = block_e[:, None] == jnp.arange(N_EXPERTS, dtype=jnp.int32)[None, :]
    count_b = jnp.sum(jnp.where(mine, counts[None, :], 0), axis=1)
    pstart_b = jnp.sum(jnp.where(mine, pstart[None, :], 0), axis=1)
    valid = jnp.clip(count_b - (starts - pstart_b), 0, bm).astype(jnp.int32)
    xbuf = _sc_scatter_rows(h2, dest, n_blocks * bm)
    ybuf = _expert_ffn(xbuf, block_e, next_e, valid, n_used, w_up, b_up, w_down, b_down, layer)
    yg = _sc_gather_rows(ybuf, dest.reshape(a)).reshape(TOP_K, t, dh)
    return _combine(x1, mod, yg, info, ln_g, ln_b, seq)


def kernel(x, c, rel_bias, w_in, w_out, w_ada, b_ada, ln1_g, ln1_b, ln2_g, ln2_b, pool_w, pool_scale,
           attn_sinks, conv_w, gdn_a_log, gdn_dt_bias, gdn_norm_w, router_w, router_b,
           exp_w_up, exp_b_up, exp_w_down, exp_b_down):
    bsz, seq, d = x.shape
    depth = w_in.shape[0]
    t = bsz * seq
    assert d == D_MODEL and w_in.shape[2] == IN_DIM
    assert seq % GDN_SUPER == 0 and seq % (ATT_BLOCKS * WINDOW) == 0
    assert t % ROW_TILE == 0 and seq % ROW_TILE == 0

    mod_all = _modulation(c, w_ada, b_ada).reshape(depth, bsz, 6, d)
    bias = _band_bias(rel_bias)

    w_up_all = exp_w_up.reshape((depth * N_EXPERTS,) + exp_w_up.shape[2:])
    b_up_all = exp_b_up.reshape(depth * N_EXPERTS, 1, exp_b_up.shape[2])
    w_down_all = exp_w_down.reshape((depth * N_EXPERTS,) + exp_w_down.shape[2:])
    b_down_all = exp_b_down.reshape(depth * N_EXPERTS, 1, exp_b_down.shape[2])

    x2d = x.reshape(t, d)
    for l in range(depth):
        mod = mod_all[l]
        w_out_p = _take_static(w_out[l], _OUT_PERM, 0).astype(BF16)
        ident = jnp.zeros((CONV_WIDTH, 1), F32).at[CONV_WIDTH - 1, 0].set(1.0)
        conv_p = jnp.where(jnp.asarray(_GDN_CONV_SRC >= 0), _take_cols(conv_w[l].astype(F32), _GDN_CONV_SRC),
                           ident)
        pool_bd = jnp.zeros((POOL_DIM, POOL_DIM), F32)
        for gi in range(len(POOL_WINDOWS)):
            sl = slice(gi * POOL_GROUP, (gi + 1) * POOL_GROUP)
            pool_bd = pool_bd.at[sl, sl].set(pool_w[l, gi].astype(F32))
        alog_v = _lane_vector(gdn_a_log[l], N_GDN_HEADS)
        dtb_v = _lane_vector(gdn_dt_bias[l], N_GDN_HEADS)
        nw_v = jnp.tile(gdn_norm_w[l].astype(F32), 2).reshape(1, LANES)
        rw = jnp.zeros((d, LANES), F32).at[:, :N_EXPERTS].set(router_w[l].astype(F32))
        rw_hi, rw_lo = _split_bf16(rw)
        rb = jnp.full((1, LANES), NEG_INF, F32).at[0, :N_EXPERTS].set(router_b[l].astype(F32))

        u_pool, aq, akv, gdn, ba = _in_projection(x2d, mod, w_in.astype(F32), l, seq)
        y_pool = _pool_mixer(u_pool, pool_bd.astype(BF16), pool_scale[l].astype(F32), seq)
        y_att = _swa_attention(aq, akv, bias, attn_sinks[l].astype(F32), seq)
        y_gdn = _gdn_mixer(gdn, ba, conv_p, alog_v, dtb_v, nw_v, seq)
        x1, h2, info, er, cnt = _out_projection(x2d, mod, y_pool, y_att, y_gdn, w_out_p, ln1_g[l], ln1_b[l],
                                            rw_hi, rw_lo, rb, seq)
        x2d = _moe(h2, info, er, cnt, x1, mod, ln2_g[l], ln2_b[l], w_up_all, b_up_all, w_down_all, b_down_all,
                   l, seq)
    return x2d.reshape(bsz, seq, d)
```

```python
import functools

import numpy as np
import jax
import jax.numpy as jnp
from jax import lax
from jax.experimental import pallas as pl
from jax.experimental.pallas import tpu as pltpu
from jax.experimental.pallas import tpu_sc as plsc

F32 = jnp.float32
BF16 = jnp.bfloat16

D_MODEL = 1024
HEAD_DIM = 64
POOL_DIM = 256
POOL_WINDOWS = (2, 4, 8, 16)
POOL_GROUP = 64
N_ATT_HEADS = 6
N_KV_HEADS = 2
ATT_DIM = 384
KV_DIM = 128
WINDOW = 128
N_BUCKETS = 32
MAX_DISTANCE = 128
N_GDN_HEADS = 6
GDN_DIM = 384
CONV_WIDTH = 4
GDN_CHUNK = 64
N_EXPERTS = 32
TOP_K = 4
EXPERT_DIM = 1024
SWIGLU_ALPHA = 1.702
SWIGLU_LIMIT = 7.0
DEPTH = 2
DEEPNORM_ALPHA = (2 * DEPTH) ** 0.25
LN_EPS = 1e-5
NORM_EPS = 1e-6
NEG_INF = -1e30

LANES = 128
SUBLANES = 8
VMEM_LIMIT = 56 * 1024 * 1024

ROW_TILE = 512
OUT_SPLIT = 2
IN_TILE = 1024
ATT_BLOCKS = 4
ATT_GROUP = 6
GDN_SUPER = 256
EXPERT_BLOCK = 512
SC_CHUNK = 64

_OFF_AQ = POOL_DIM
_OFF_AK = _OFF_AQ + ATT_DIM
_OFF_AV = _OFF_AK + KV_DIM
_OFF_GQ = _OFF_AV + KV_DIM
_OFF_GK = _OFF_GQ + GDN_DIM
_OFF_GV = _OFF_GK + GDN_DIM
_OFF_GZ = _OFF_GV + GDN_DIM
_OFF_GB = _OFF_GZ + GDN_DIM
_OFF_GA = _OFF_GB + N_GDN_HEADS
IN_DIM = _OFF_GA + N_GDN_HEADS

P_POOL = (0, POOL_DIM)
P_Q = (P_POOL[1], P_POOL[1] + ATT_DIM)
P_KV = (P_Q[1], P_Q[1] + 2 * KV_DIM)
P_GDN = (P_KV[1], P_KV[1] + 4 * GDN_DIM)
P_BA = (P_GDN[1], P_GDN[1] + LANES)
P_TOTAL = P_BA[1]


def _head_cols(off, h):
    return list(range(off + HEAD_DIM * h, off + HEAD_DIM * (h + 1)))


def _build_in_perm():
    cols = list(range(POOL_DIM))
    for p in range(N_ATT_HEADS // 2):
        cols += _head_cols(_OFF_AQ, p) + _head_cols(_OFF_AQ, p + 3)
    cols += list(range(_OFF_AK, _OFF_AK + 2 * KV_DIM))
    gdn_src = []
    for p in range(N_GDN_HEADS // 2):
        e, o = 2 * p, 2 * p + 1
        grp = (_head_cols(_OFF_GK, e) + _head_cols(_OFF_GQ, e)
               + _head_cols(_OFF_GQ, o) + _head_cols(_OFF_GK, o)
               + _head_cols(_OFF_GV, o) + _head_cols(_OFF_GV, e)
               + _head_cols(_OFF_GZ, o) + _head_cols(_OFF_GZ, e))
        cols += grp
        gdn_src += [c - _OFF_GQ if c < _OFF_GZ else -1 for c in grp]
    cols += list(range(_OFF_GB, _OFF_GB + 2 * N_GDN_HEADS))
    cols += [-1] * (LANES - 2 * N_GDN_HEADS)
    assert len(cols) == P_TOTAL
    return np.asarray(cols, np.int32), np.asarray(gdn_src, np.int32)


_IN_PERM, _GDN_CONV_SRC = _build_in_perm()


def _build_out_perm():
    rows = list(range(POOL_DIM))
    for p in range(N_ATT_HEADS // 2):
        rows += _head_cols(POOL_DIM, p) + _head_cols(POOL_DIM, p + 3)
    for p in range(N_GDN_HEADS // 2):
        rows += _head_cols(POOL_DIM + ATT_DIM, 2 * p + 1) + _head_cols(POOL_DIM + ATT_DIM, 2 * p)
    return np.asarray(rows, np.int32)


_OUT_PERM = _build_out_perm()


def _t5_bucket_line():
    n = np.maximum(2 * WINDOW - 1 - np.arange(3 * WINDOW - 1), 0)
    max_exact = N_BUCKETS // 2
    nf = np.maximum(n, 1).astype(np.float32)
    large = max_exact + (np.log(nf / max_exact) / np.float32(np.log(MAX_DISTANCE / max_exact))
                         * (N_BUCKETS - max_exact)).astype(np.int32)
    large = np.minimum(large, N_BUCKETS - 1)
    return np.where(n < max_exact, n, large).astype(np.int32)


_BUCKET_LINE = _t5_bucket_line()


def _band_bias(rel_bias):
    n_line = 3 * WINDOW - 1
    line = jnp.take(rel_bias.astype(F32), jnp.asarray(_BUCKET_LINE), axis=0).T
    heads = line.shape[0]
    padded = jnp.concatenate([line, jnp.zeros((heads, 1), F32)], axis=1)
    skew = jnp.tile(padded, (1, WINDOW))[:, :WINDOW * n_line].reshape(heads, WINDOW, n_line)
    return skew[:, :, WINDOW - 1:3 * WINDOW - 1]


def _take_static(w, perm, axis):
    parts = []
    start = 0
    for i in range(1, len(perm) + 1):
        run_ends = (i == len(perm) or ((perm[i] < 0) != (perm[i - 1] < 0))
                    or (perm[i] >= 0 and perm[i] != perm[i - 1] + 1))
        if run_ends:
            if perm[start] < 0:
                shape = list(w.shape)
                shape[axis] = i - start
                parts.append(jnp.zeros(shape, w.dtype))
            else:
                parts.append(lax.slice_in_dim(w, int(perm[start]), int(perm[start]) + (i - start), axis=axis))
            start = i
    return jnp.concatenate(parts, axis=axis)


def _take_cols(w, perm):
    return _take_static(w, perm, w.ndim - 1)


def _split_bf16(x):
    hi = x.astype(BF16)
    lo = (x - hi.astype(F32)).astype(BF16)
    return hi, lo


def _pack_bf16_pairs(x):
    n = x.shape[1] // 2
    bits = pltpu.bitcast(x.astype(BF16).astype(F32), jnp.int32)
    return lax.shift_right_logical(bits[:, :n], 16) | bits[:, n:]


def _unpack_bf16_pairs(u):
    lo = pltpu.bitcast(lax.shift_left(u, 16), F32)
    hi = pltpu.bitcast(u & jnp.int32(-65536), F32)
    return jnp.concatenate([lo, hi], axis=1)


def _dot(a, b):
    return jnp.dot(a, b, preferred_element_type=F32)


def _dot_nt(a, b):
    return lax.dot_general(a, b, (((1,), (1,)), ((), ())), preferred_element_type=F32)


def _dot_hi_exact_rhs(x, m_bf16):
    hi, lo = _split_bf16(x)
    return _dot(hi, m_bf16) + _dot(lo, m_bf16)


def _sigmoid(x):
    return 1.0 / (1.0 + jnp.exp(-x))


def _layer_norm(r, g, b):
    mu = jnp.mean(r, axis=-1, keepdims=True)
    d = r - mu
    var = jnp.mean(d * d, axis=-1, keepdims=True)
    return d * lax.rsqrt(var + LN_EPS) * g + b


def _cparams(sem):
    return pltpu.CompilerParams(dimension_semantics=sem, vmem_limit_bytes=VMEM_LIMIT)


def _mod_kernel(c_ref, w_ref, b_ref, o_ref):
    c = c_ref[...]
    ca = c * _sigmoid(c)
    ch, cl = _split_bf16(ca)
    wh, wl = _split_bf16(w_ref[0])
    o_ref[0] = _dot(ch, wh) + _dot(cl, wh) + _dot(ch, wl) + b_ref[0]


def _modulation(c, w_ada, b_ada):
    depth, d, n = w_ada.shape
    bsz = c.shape[0]
    tn = 512
    return pl.pallas_call(
        _mod_kernel,
        grid=(depth, n // tn),
        in_specs=[
            pl.BlockSpec((bsz, d), lambda l, j: (0, 0)),
            pl.BlockSpec((1, d, tn), lambda l, j: (l, 0, j)),
            pl.BlockSpec((1, 1, tn), lambda l, j: (l, 0, j)),
        ],
        out_specs=pl.BlockSpec((1, bsz, tn), lambda l, j: (l, 0, j)),
        out_shape=jax.ShapeDtypeStruct((depth, bsz, n), F32),
        compiler_params=_cparams(("arbitrary", "arbitrary")),
        name="adaln_mod",
    )(c, w_ada, b_ada.reshape(depth, 1, n))


def _perm_runs(perm):
    runs = []
    start = 0
    for i in range(1, len(perm) + 1):
        run_ends = (i == len(perm) or ((perm[i] < 0) != (perm[i - 1] < 0))
                    or (perm[i] >= 0 and perm[i] != perm[i - 1] + 1))
        if run_ends:
            runs.append((int(perm[start]) if perm[start] >= 0 else -1, i - start, start))
            start = i
    return runs


_IN_RUNS = _perm_runs(_IN_PERM)


def _inproj_kernel(layer, x_ref, mod_ref, wt_hbm, pool_ref, q_ref, kv_ref, gdn_ref, ba_ref,
                   wt_f32, tail_ref, wt_ref, sem):
    @pl.when(pl.program_id(0) == 0)
    def _():
        n_real = wt_hbm.shape[0]
        pad0 = (n_real // SUBLANES) * SUBLANES
        bulk = pltpu.make_async_copy(wt_hbm.at[pl.ds(0, pad0), layer, :], wt_f32.at[pl.ds(0, pad0)], sem.at[0])
        tail = pltpu.make_async_copy(wt_hbm.at[pl.ds(n_real - SUBLANES, SUBLANES), layer, :], tail_ref, sem.at[1])
        bulk.start()
        tail.start()
        bulk.wait()
        tail.wait()
        row8 = lax.broadcasted_iota(jnp.int32, tail_ref.shape, 0)
        left = n_real - pad0
        wt_f32[pad0:pad0 + SUBLANES, :] = jnp.where(row8 < left, pltpu.roll(tail_ref[...], left, axis=0), 0.0)
        wt_f32[pad0 + SUBLANES:, :] = jnp.zeros((wt_f32.shape[0] - pad0 - SUBLANES, wt_f32.shape[1]), F32)
        for src, n, dst in _IN_RUNS:
            if src >= 0:
                rows = n if src + n < n_real else wt_ref.shape[0] - dst
                wt_ref[dst:dst + rows, :] = wt_f32[src:src + rows, :].astype(BF16)
        q_rows = wt_ref[P_Q[0]:P_Q[1], :].astype(F32) * (HEAD_DIM ** -0.5)
        wt_ref[P_Q[0]:P_Q[1], :] = q_rows.astype(BF16)

    sh = mod_ref[0, 0:1, :]
    sc = mod_ref[0, 1:2, :]
    h = (x_ref[...] * (1.0 + sc) + sh).astype(BF16)

    def mm(rng):
        return _dot_nt(h, wt_ref[rng[0]:rng[1], :])

    pool_ref[...] = mm(P_POOL)
    q_ref[...] = mm(P_Q).astype(BF16)
    kv_ref[...] = mm(P_KV).astype(BF16)
    gdn_ref[...] = mm(P_GDN)
    ba_ref[...] = mm(P_BA)


def _in_projection(x2d, mod, w_in, layer, seq):
    t, d = x2d.shape
    tm = min(IN_TILE, seq)
    widths = [r[1] - r[0] for r in (P_POOL, P_Q, P_KV, P_GDN, P_BA)]
    dtypes = [F32, BF16, BF16, F32, F32]
    return pl.pallas_call(
        functools.partial(_inproj_kernel, layer),
        grid=(t // tm,),
        in_specs=[
            pl.BlockSpec((tm, d), lambda i: (i, 0)),
            pl.BlockSpec((1, 6, d), lambda i: ((i * tm) // seq, 0, 0)),
            pl.BlockSpec(memory_space=pl.ANY),
        ],
        out_specs=[pl.BlockSpec((tm, w), lambda i: (i, 0)) for w in widths],
        out_shape=[jax.ShapeDtypeStruct((t, w), dt) for w, dt in zip(widths, dtypes)],
        scratch_shapes=[pltpu.VMEM((P_TOTAL, d), F32), pltpu.VMEM((SUBLANES, d), F32),
                        pltpu.VMEM((P_TOTAL, d), BF16), pltpu.SemaphoreType.DMA((2,))],
        compiler_params=_cparams(("arbitrary",)),
        name="in_proj",
    )(x2d, mod, jnp.transpose(w_in, (2, 0, 1)))


def _pool_kernel(u_ref, w_ref, scale_ref, o_ref):
    u = u_ref[...]
    row = lax.broadcasted_iota(jnp.int32, u.shape, 0)
    lane = lax.broadcasted_iota(jnp.int32, u.shape, 1)

    def shifted(a, s):
        return jnp.where(row >= s, pltpu.roll(a, s, axis=0), 0.0)

    sums = []
    acc = u
    for wdt in POOL_WINDOWS:
        acc = acc + shifted(acc, wdt // 2)
        sums.append(acc)
    grp = lane // POOL_GROUP
    wsum = sums[-1]
    win = jnp.full(u.shape, POOL_WINDOWS[-1], jnp.int32)
    for gi in range(len(POOL_WINDOWS) - 2, -1, -1):
        wsum = jnp.where(grp == gi, sums[gi], wsum)
        win = jnp.where(grp == gi, POOL_WINDOWS[gi], win)
    cnt = jnp.minimum(row + 1, win).astype(F32)
    p = wsum / cnt - u
    y = _dot(p.astype(BF16), w_ref[...]) * scale_ref[...]
    o_ref[...] = y.astype(BF16)


def _pool_mixer(u, pool_w_bd, pool_scale, seq):
    t, c = u.shape
    return pl.pallas_call(
        _pool_kernel,
        grid=(t // seq,),
        in_specs=[
            pl.BlockSpec((seq, c), lambda b: (b, 0)),
            pl.BlockSpec((c, c), lambda b: (0, 0)),
            pl.BlockSpec((1, c), lambda b: (0, 0)),
        ],
        out_specs=pl.BlockSpec((seq, c), lambda b: (b, 0)),
        out_shape=jax.ShapeDtypeStruct((t, c), BF16),
        compiler_params=_cparams(("arbitrary",)),
        name="pool_mixer",
    )(u, pool_w_bd, pool_scale.reshape(1, c))


def _attn_kernel(sink_ref, q_ref, kvc_ref, kvp_ref, bias_ref, o_ref):
    step = pl.program_id(1)
    qi = lax.broadcasted_iota(jnp.int32, (WINDOW, 2 * WINDOW), 0)
    kj = lax.broadcasted_iota(jnp.int32, (WINDOW, 2 * WINDOW), 1)
    dist = qi + WINDOW - kj
    in_band = (dist >= 0) & (dist < WINDOW)
    lo = lax.broadcasted_iota(jnp.int32, (WINDOW, LANES), 1) < HEAD_DIM
    heads = [(p, half) for p in range(N_ATT_HEADS // 2) for half in range(2)]
    sinks = [sink_ref[p + 3 * half] for p, half in heads]
    for sub in range(ATT_BLOCKS):
        r0 = sub * WINDOW
        prev = kvp_ref[...] if sub == 0 else kvc_ref[r0 - WINDOW:r0, :]
        kv = jnp.concatenate([prev, kvc_ref[r0:r0 + WINDOW, :]], axis=0)
        k = kv[:, :KV_DIM]
        v = kv[:, KV_DIM:]
        valid = in_band & ((kj >= WINDOW) | (step > 0)) if sub == 0 else in_band
        for g0 in range(0, len(heads), ATT_GROUP):
            group = heads[g0:g0 + ATT_GROUP]
            sk_g = sinks[g0:g0 + ATT_GROUP]
            scores = []
            for p, half in group:
                qp = q_ref[r0:r0 + WINDOW, p * LANES:(p + 1) * LANES]
                qm = jnp.where(lo if half == 0 else jnp.logical_not(lo), qp, jnp.zeros_like(qp))
                scores.append(jnp.where(valid, _dot_nt(qm, k) + bias_ref[p + 3 * half], NEG_INF))
            tops = [jnp.maximum(jnp.max(s, axis=-1, keepdims=True), sk) for s, sk in zip(scores, sk_g)]
            probs = [jnp.exp(s - m) for s, m in zip(scores, tops)]
            dens = [jnp.sum(pr, axis=-1, keepdims=True) + jnp.exp(sk - m)
                    for pr, sk, m in zip(probs, sk_g, tops)]
            outs = [_dot(pr.astype(BF16), v) / den for pr, den in zip(probs, dens)]
            for idx in range(0, len(group), 2):
                p = group[idx][0]
                o_ref[r0:r0 + WINDOW, p * LANES:(p + 1) * LANES] = (
                    jnp.where(lo, outs[idx], outs[idx + 1]).astype(BF16))


def _swa_attention(q, kv, bias, sinks, seq):
    t = q.shape[0]
    rows = ATT_BLOCKS * WINDOW
    nblk = seq // rows
    return pl.pallas_call(
        _attn_kernel,
        grid=(t // seq, nblk),
        in_specs=[
            pl.BlockSpec(memory_space=pltpu.SMEM),
            pl.BlockSpec((rows, ATT_DIM), lambda b, n: (b * nblk + n, 0)),
            pl.BlockSpec((rows, 2 * KV_DIM), lambda b, n: (b * nblk + n, 0)),
            pl.BlockSpec((WINDOW, 2 * KV_DIM),
                         lambda b, n: (jnp.maximum((b * nblk + n) * ATT_BLOCKS - 1, 0), 0)),
            pl.BlockSpec((N_ATT_HEADS, WINDOW, 2 * WINDOW), lambda b, n: (0, 0, 0)),
        ],
        out_specs=pl.BlockSpec((rows, ATT_DIM), lambda b, n: (b * nblk + n, 0)),
        out_shape=jax.ShapeDtypeStruct((t, ATT_DIM), BF16),
        compiler_params=_cparams(("arbitrary", "arbitrary")),
        name="swa_attention",
    )(sinks, q, kv, kv, bias)


_GDN_BASE = SUBLANES
_GDN_LEVELS = int(np.log2(GDN_CHUNK // _GDN_BASE))


def _gdn_masks():
    r = np.arange(GDN_SUPER)
    ri, ci = r[:, None], r[None, :]
    same_chunk = (ri // GDN_CHUNK) == (ci // GDN_CHUNK)
    incl = same_chunk & (ri >= ci)
    planes = [incl, ri == ci]
    base = ((ri // _GDN_BASE) == (ci // _GDN_BASE)) & (ri > ci)
    planes.append(base)
    for lvl in range(_GDN_LEVELS):
        small = _GDN_BASE << lvl
        planes.append(((ri // (2 * small)) == (ci // (2 * small))) & ((ri // small) != (ci // small)) & (ri > ci))
    bmask = np.stack(planes).astype(np.float32)
    bmask[2] = -bmask[2]
    negmask = np.where(incl, 0.0, -np.inf).astype(np.float32)
    return negmask, bmask


_GDN_NEGMASK, _GDN_BMASK = _gdn_masks()


def _gdn_kernel(x_ref, halo_ref, ba_ref, cw_ref, alog_ref, dtb_ref, nw_ref, negmask_ref, bmask_ref,
                y_ref, state_ref, xs_ref):
    sc_id = pl.program_id(1)
    rows = GDN_SUPER
    nchunk = rows // GDN_CHUNK
    c_sz = GDN_CHUNK

    @pl.when(sc_id == 0)
    def _():
        state_ref[...] = jnp.zeros_like(state_ref)

    xs_ref[:SUBLANES, :] = jnp.where(sc_id == 0, 0.0, halo_ref[...])
    xs_ref[SUBLANES:, :] = x_ref[...]
    act = []
    for g in range(x_ref.shape[1] // LANES):
        cols = slice(g * LANES, (g + 1) * LANES)
        acc = x_ref[:, cols] * cw_ref[CONV_WIDTH - 1:CONV_WIDTH, cols]
        if g % 4 != 3:
            for s in range(1, CONV_WIDTH):
                acc = acc + (xs_ref[SUBLANES - s:SUBLANES - s + rows, cols]
                             * cw_ref[CONV_WIDTH - 1 - s:CONV_WIDTH - s, cols])
        act.append(acc * _sigmoid(acc))

    negmask = negmask_ref[...]
    tri_incl = bmask_ref[0]
    eye_b = bmask_ref[1]
    base_neg = bmask_ref[2]
    bands = [bmask_ref[3 + lvl] for lvl in range(_GDN_LEVELS)]
    li = lax.broadcasted_iota(jnp.int32, (LANES, LANES), 0)
    lj = lax.broadcasted_iota(jnp.int32, (LANES, LANES), 1)
    half_ones = jnp.where((li // HEAD_DIM) == (lj // HEAD_DIM), 1.0, 0.0).astype(BF16)
    lane_lo = lax.broadcasted_iota(jnp.int32, (rows, LANES), 1) < HEAD_DIM

    ba = ba_ref[...]
    beta_all = _sigmoid(ba)
    sp_in = ba + dtb_ref[...]
    softplus = jnp.maximum(sp_in, 0.0) + jnp.log(1.0 + jnp.exp(-jnp.abs(sp_in)))
    g_all = -jnp.exp(alog_ref[...]) * softplus
    gcum = _dot_hi_exact_rhs_lhs(tri_incl, g_all)
    gcum_t = gcum.T

    heads = range(N_GDN_HEADS)
    lane_hi = jnp.logical_not(lane_lo)
    mk = [lane_lo if h % 2 == 0 else lane_hi for h in heads]
    scale = HEAD_DIM ** -0.5

    xk, xq, gn, gc_col, beta, eg = [], [], [], [], [], []
    for h in heads:
        g = act[4 * (h // 2) + (h % 2)]
        g = g * lax.rsqrt(_dot_hi_exact_rhs(g * g, half_ones) + NORM_EPS)
        gn.append(g)
        xk.append(jnp.where(mk[h], g, 0.0))
        xq.append(jnp.where(mk[h], pltpu.roll(g, HEAD_DIM, axis=1), 0.0) * scale)
        beta.append(beta_all[:, h:h + 1])
        gc_col.append(gcum[:, N_GDN_HEADS + h:N_GDN_HEADS + h + 1])
        eg.append(jnp.exp(gc_col[h]))

    l_b, attn, rhs = [], [], []
    for h in heads:
        gc_row = gcum_t[N_GDN_HEADS + h:N_GDN_HEADS + h + 1, :]
        decay = jnp.exp(gc_col[h] - gc_row + negmask)
        xk_b = xk[h].astype(BF16)
        kk = _dot_nt((xk[h] * beta[h]).astype(BF16), xk_b)
        l_b.append((kk * decay).astype(BF16))
        attn.append((_dot_nt(xq[h].astype(BF16), xk_b) * decay).astype(BF16))
        vv = act[4 * (h // 2) + 2]
        rhs.append(jnp.where(mk[h], gn[h] * eg[h], vv) * beta[h])

    a1 = [l_b[h] * base_neg for h in heads]
    a2 = [_dot(a1[h], a1[h]).astype(BF16) for h in heads]
    a4 = [_dot(a2[h], a2[h]).astype(BF16) for h in heads]
    inv0 = [eye_b + a1[h] for h in heads]
    acc1 = [inv0[h].astype(F32) + _dot(a2[h], inv0[h]) for h in heads]
    inv_b = [(acc1[h] + _dot(a4[h], acc1[h].astype(BF16))).astype(BF16) for h in heads]
    for lvl in range(_GDN_LEVELS - 1):
        mid = [_dot(l_b[h] * bands[lvl], inv_b[h]).astype(BF16) for h in heads]
        inv_b = [inv_b[h] - _dot(inv_b[h], mid[h]).astype(BF16) for h in heads]
    half = [_dot(inv_b[h], rhs[h].astype(BF16)) for h in heads]
    mid = [_dot(l_b[h] * bands[_GDN_LEVELS - 1], half[h].astype(BF16)) for h in heads]
    sol = [half[h] - _dot(inv_b[h], mid[h].astype(BF16)) for h in heads]

    lane_lo_s = lax.broadcasted_iota(jnp.int32, (LANES, LANES), 1) < HEAD_DIM
    mk_s = [lane_lo_s if h % 2 == 0 else jnp.logical_not(lane_lo_s) for h in heads]
    sol_b = [sol[h].astype(BF16) for h in heads]
    attn_sol = [_dot(attn[h], sol_b[h]) for h in heads]
    q_eff = [(xq[h] * eg[h] - jnp.where(mk[h], attn_sol[h], 0.0)).astype(BF16) for h in heads]
    o_free = [jnp.where(mk[h], 0.0, attn_sol[h]) for h in heads]
    kw = [[] for _ in heads]
    ku = [[] for _ in heads]
    cdec = [[] for _ in heads]
    for c in range(nchunk):
        r0 = c * c_sz
        for h in heads:
            glast = gcum[r0 + c_sz - 1:r0 + c_sz, N_GDN_HEADS + h:N_GDN_HEADS + h + 1]
            kd_t = (xk[h][r0:r0 + c_sz] * jnp.exp(glast - gc_col[h][r0:r0 + c_sz])).T
            both = _dot(kd_t.astype(BF16), sol_b[h][r0:r0 + c_sz])
            kw[h].append(jnp.where(mk_s[h], both, 0.0).astype(BF16))
            ku[h].append(jnp.where(mk_s[h], 0.0, both))
            cdec[h].append(jnp.exp(glast))
    st = [state_ref[h] for h in heads]
    o_parts = [[] for _ in heads]
    for c in range(nchunk):
        r0 = c * c_sz
        for h in heads:
            lhs = jnp.concatenate([kw[h][c], q_eff[h][r0:r0 + c_sz]], axis=0)
            prod = _dot(lhs, st[h].astype(BF16))
            o_parts[h].append(prod[LANES:] + o_free[h][r0:r0 + c_sz])
            st[h] = st[h] * cdec[h][c] + ku[h][c] - prod[:LANES]
    for h in heads:
        state_ref[h] = st[h]

    for p in range(N_GDN_HEADS // 2):
        o_pair = [jnp.concatenate(o_parts[h], axis=0) for h in (2 * p, 2 * p + 1)]
        o = jnp.where(lane_lo, o_pair[1], o_pair[0])
        ms = _dot_hi_exact_rhs(o * o, half_ones) * (1.0 / HEAD_DIM)
        zz = act[4 * p + 3]
        y = o * lax.rsqrt(ms + NORM_EPS) * nw_ref[...] * zz
        y_ref[:, p * LANES:(p + 1) * LANES] = y.astype(BF16)


def _dot_hi_exact_rhs_lhs(m_bf16, x):
    hi, lo = _split_bf16(x)
    return _dot(m_bf16, hi) + _dot(m_bf16, lo)


def _gdn_mixer(gdn, ba, conv_p, alog_v, dtb_v, nw_v, seq):
    t, c = gdn.shape
    rows = GDN_SUPER
    nsc = seq // rows
    hb = rows // SUBLANES
    return pl.pallas_call(
        _gdn_kernel,
        grid=(t // seq, nsc),
        in_specs=[
            pl.BlockSpec((rows, c), lambda b, s: (b * nsc + s, 0)),
            pl.BlockSpec((SUBLANES, c), lambda b, s: (jnp.maximum((b * nsc + s) * hb - 1, 0), 0)),
            pl.BlockSpec((rows, LANES), lambda b, s: (b * nsc + s, 0)),
            pl.BlockSpec((CONV_WIDTH, c), lambda b, s: (0, 0)),
            pl.BlockSpec((1, LANES), lambda b, s: (0, 0)),
            pl.BlockSpec((1, LANES), lambda b, s: (0, 0)),
            pl.BlockSpec((1, LANES), lambda b, s: (0, 0)),
            pl.BlockSpec((rows, rows), lambda b, s: (0, 0)),
            pl.BlockSpec((3 + _GDN_LEVELS, rows, rows), lambda b, s: (0, 0, 0)),
        ],
        out_specs=pl.BlockSpec((rows, GDN_DIM), lambda b, s: (b * nsc + s, 0)),
        out_shape=jax.ShapeDtypeStruct((t, GDN_DIM), BF16),
        scratch_shapes=[pltpu.VMEM((N_GDN_HEADS, LANES, LANES), F32), pltpu.VMEM((rows + SUBLANES, c), F32)],
        compiler_params=_cparams(("arbitrary", "arbitrary")),
        name="gdn_mixer",
    )(gdn, gdn, ba, conv_p, alog_v, dtb_v, nw_v, jnp.asarray(_GDN_NEGMASK), jnp.asarray(_GDN_BMASK, BF16))


def _route_tile(logits, before, carry_ref, live):
    shape = logits.shape
    lane = lax.broadcasted_iota(jnp.int32, shape, 1).astype(F32)
    work = logits
    vals, idxs = [], []
    for _k in range(TOP_K):
        m = jnp.max(work, axis=-1, keepdims=True)
        idx = jnp.min(jnp.where(work == m, lane, float(LANES)), axis=-1, keepdims=True)
        vals.append(m)
        idxs.append(idx)
        work = jnp.where(lane == idx, -jnp.inf, work)
    exps = [jnp.exp(v - vals[0]) for v in vals]
    den = exps[0] + exps[1] + exps[2] + exps[3]
    onehots = [lane == idx for idx in idxs]
    member = jnp.zeros(shape, F32)
    for oh in onehots:
        member = member + jnp.where(oh, 1.0, 0.0)
    rank = _dot(before, member.astype(BF16)) + carry_ref[...]
    carry_ref[...] = carry_ref[...] + live * jnp.sum(member, axis=0, keepdims=True)
    info = jnp.zeros(shape, F32)
    for k in range(TOP_K):
        rank_k = jnp.sum(jnp.where(onehots[k], rank, 0.0), axis=-1, keepdims=True)
        info = jnp.where(lane == float(k), idxs[k], info)
        info = jnp.where(lane == float(TOP_K + k), rank_k, info)
        info = jnp.where(lane == float(2 * TOP_K + k), exps[k] / den, info)
    return info


def _outproj_kernel(x_ref, mod_ref, yp_ref, ya_ref, yg_ref, wp_ref, wa_ref, wg_ref, lng_ref, lnb_ref,
                    rwh_ref, rwl_ref, rb_ref, before_ref, x1_ref, h2_ref, info_ref, er_ref, cnt_ref,
                    carry_ref, logit_s):
    step = pl.program_id(0)

    @pl.when(step == 0)
    def _():
        carry_ref[...] = jnp.zeros_like(carry_ref)
        logit_s[...] = jnp.zeros_like(logit_s)

    part = x_ref.shape[0] // OUT_SPLIT
    halves = [slice(j * part, (j + 1) * part) for j in range(OUT_SPLIT)]
    y = [_dot(yp_ref[r, :], wp_ref[...]) + _dot(ya_ref[r, :], wa_ref[...]) + _dot(yg_ref[r, :], wg_ref[...])
         for r in halves]
    live = jnp.where(step > 0, 1.0, 0.0)
    info = _route_tile(logit_s[...], before_ref[...], carry_ref, live)
    info_ref[...] = info
    er_ref[...] = info.T[:SUBLANES]
    cnt_ref[...] = carry_ref[...]
    g1 = mod_ref[0, 2:3, :]
    sh2 = mod_ref[0, 3:4, :]
    sc2 = mod_ref[0, 4:5, :]
    for r, y_r in zip(halves, y):
        x1 = _layer_norm(DEEPNORM_ALPHA * x_ref[r, :] + g1 * y_r, lng_ref[...], lnb_ref[...])
        x1_ref[r, :] = x1
        h2 = x1 * (1.0 + sc2) + sh2
        hh, hl = _split_bf16(h2)
        h2_ref[r, :] = _pack_bf16_pairs(h2)
        logit_s[r, :] = _dot(hh, rwh_ref[...]) + _dot(hl, rwh_ref[...]) + _dot(hh, rwl_ref[...]) + rb_ref[...]


def _out_projection(x2d, mod, yp, ya, yg, w_out_p, ln_g, ln_b, rw_hi, rw_lo, rb, seq):
    t, d = x2d.shape
    tm = ROW_TILE
    wp = w_out_p[:POOL_DIM]
    wa = w_out_p[POOL_DIM:POOL_DIM + ATT_DIM]
    wg = w_out_p[POOL_DIM + ATT_DIM:]
    last = t // tm - 1
    row = lambda i: (jnp.minimum(i, last), 0)
    routed = lambda i: (jnp.maximum(i - 1, 0), 0)
    fixed = lambda i: (0, 0)
    return pl.pallas_call(
        _outproj_kernel,
        grid=(t // tm + 1,),
        in_specs=[
            pl.BlockSpec((tm, d), row),
            pl.BlockSpec((1, 6, d), lambda i: ((jnp.minimum(i, last) * tm) // seq, 0, 0)),
            pl.BlockSpec((tm, POOL_DIM), row),
            pl.BlockSpec((tm, ATT_DIM), row),
            pl.BlockSpec((tm, GDN_DIM), row),
            pl.BlockSpec((POOL_DIM, d), fixed),
            pl.BlockSpec((ATT_DIM, d), fixed),
            pl.BlockSpec((GDN_DIM, d), fixed),
            pl.BlockSpec((1, d), fixed),
            pl.BlockSpec((1, d), fixed),
            pl.BlockSpec((d, LANES), fixed),
            pl.BlockSpec((d, LANES), fixed),
            pl.BlockSpec((1, LANES), fixed),
            pl.BlockSpec((tm, tm), fixed),
        ],
        out_specs=[pl.BlockSpec((tm, d), row), pl.BlockSpec((tm, d // 2), row), pl.BlockSpec((tm, LANES), routed),
                   pl.BlockSpec((SUBLANES, tm), lambda i: (0, jnp.maximum(i - 1, 0))),
                   pl.BlockSpec((1, LANES), fixed)],
        out_shape=[jax.ShapeDtypeStruct((t, d), F32), jax.ShapeDtypeStruct((t, d // 2), jnp.int32),
                   jax.ShapeDtypeStruct((t, LANES), F32), jax.ShapeDtypeStruct((SUBLANES, t), F32),
                   jax.ShapeDtypeStruct((1, LANES), F32)],
        scratch_shapes=[pltpu.VMEM((1, LANES), F32), pltpu.VMEM((tm, LANES), F32)],
        compiler_params=_cparams(("arbitrary",)),
        name="out_proj_ln_route",
    )(x2d, mod, yp, ya, yg, wp, wa, wg, ln_g.reshape(1, d), ln_b.reshape(1, d), rw_hi, rw_lo, rb,
      jnp.tril(jnp.ones((tm, tm), BF16), -1))


def _slot_kernel(er_ref, cnt_ref, dest_ref, pcum_ref):
    cnt = jnp.broadcast_to(cnt_ref[...], (SUBLANES, LANES))
    padded = jnp.floor((cnt + float(EXPERT_BLOCK - 1)) * (1.0 / EXPERT_BLOCK)) * float(EXPERT_BLOCK)
    lane8 = lax.broadcasted_iota(jnp.int32, (SUBLANES, LANES), 1)
    acc = padded
    step = 1
    while step < LANES:
        acc = acc + jnp.where(lane8 >= step, pltpu.roll(acc, step, axis=1), 0.0)
        step *= 2
    pcum_ref[...] = acc[:1].astype(jnp.int32)
    pstart = acc - padded

    er = er_ref[...]
    start = jnp.zeros(er.shape, F32)
    for e in range(N_EXPERTS):
        offset = jnp.sum(jnp.where(lane8 == e, pstart, 0.0), axis=-1, keepdims=True)
        start = jnp.where(er == float(e), offset, start)
    row = lax.broadcasted_iota(jnp.int32, er.shape, 0)
    slots = jnp.where(row < TOP_K, start + pltpu.roll(er, TOP_K, axis=0), 0.0)
    dest_ref[...] = slots.astype(jnp.int32)


def _slots(er, cnt):
    t = er.shape[1]
    return pl.pallas_call(
        _slot_kernel,
        grid=(1,),
        in_specs=[pl.BlockSpec((SUBLANES, t), lambda i: (0, 0)), pl.BlockSpec((1, LANES), lambda i: (0, 0))],
        out_specs=[pl.BlockSpec((SUBLANES, t), lambda i: (0, 0)), pl.BlockSpec((1, LANES), lambda i: (0, 0))],
        out_shape=[jax.ShapeDtypeStruct((SUBLANES, t), jnp.int32), jax.ShapeDtypeStruct((1, LANES), jnp.int32)],
        compiler_params=_cparams(("arbitrary",)),
        name="moe_slots",
    )(er, cnt)


def _expert_kernel(e0, be_ref, nxt_ref, val_ref, nu_ref, x_ref, wup_hbm, bup_ref, wdn_hbm, bdn_ref, y_ref,
                   wup_st, wdn_st, wup_bf, wdn_bf, sems):
    i = pl.program_id(0)
    e = be_ref[i]
    prev = be_ref[jnp.maximum(i - 1, 0)]
    used = i < nu_ref[0]

    def weight_copies(expert):
        return (pltpu.make_async_copy(wup_hbm.at[e0 + expert], wup_st, sems.at[0]),
                pltpu.make_async_copy(wdn_hbm.at[e0 + expert], wdn_st, sems.at[1]))

    @pl.when(i == 0)
    def _():
        for cp in weight_copies(e):
            cp.start()

    @pl.when(used & ((i == 0) | (e != prev)))
    def _():
        for cp in weight_copies(e):
            cp.wait()
        wup_bf[...] = wup_st[...].astype(BF16)
        wdn_bf[...] = wdn_st[...].astype(BF16)

        @pl.when(nxt_ref[i] >= 0)
        def _():
            for cp in weight_copies(nxt_ref[i]):
                cp.start()

    def ffn(rows):
        xb = _unpack_bf16_pairs(x_ref[:rows, :]).astype(BF16)
        hb = _dot(xb, wup_bf[...]) + bup_ref[0]
        x_glu = jnp.minimum(hb[:, :EXPERT_DIM], SWIGLU_LIMIT)
        x_lin = jnp.clip(hb[:, EXPERT_DIM:], -SWIGLU_LIMIT, SWIGLU_LIMIT)
        act = x_glu * _sigmoid(SWIGLU_ALPHA * x_glu) * (x_lin + 1.0)
        y = _dot(act.astype(BF16), wdn_bf[...]) + bdn_ref[0]
        y_ref[:rows, :] = _pack_bf16_pairs(y)

    half_rows = x_ref.shape[0] // 2
    small = val_ref[i] <= half_rows

    @pl.when(used & jnp.logical_not(small))
    def _():
        ffn(x_ref.shape[0])

    @pl.when(used & small)
    def _():
        ffn(half_rows)
        y_ref[half_rows:, :] = jnp.zeros((x_ref.shape[0] - half_rows, y_ref.shape[1]), y_ref.dtype)

    @pl.when(i >= nu_ref[0])
    def _():
        y_ref[...] = jnp.zeros_like(y_ref)


def _expert_ffn(xbuf, block_e, next_e, valid, n_used, w_up, b_up, w_down, b_down, layer):
    p, dh = xbuf.shape
    d = 2 * dh
    bm = EXPERT_BLOCK
    ne, _, n_up = w_up.shape
    e0 = layer * N_EXPERTS
    grid_spec = pltpu.PrefetchScalarGridSpec(
        num_scalar_prefetch=4,
        grid=(p // bm,),
        in_specs=[
            pl.BlockSpec((bm, dh), lambda i, be, nx, vl, nu: (i, 0)),
            pl.BlockSpec(memory_space=pl.ANY),
            pl.BlockSpec((1, 1, n_up), lambda i, be, nx, vl, nu: (e0 + be[i], 0, 0)),
            pl.BlockSpec(memory_space=pl.ANY),
            pl.BlockSpec((1, 1, d), lambda i, be, nx, vl, nu: (e0 + be[i], 0, 0)),
        ],
        out_specs=pl.BlockSpec((bm, dh), lambda i, be, nx, vl, nu: (i, 0)),
        scratch_shapes=[pltpu.VMEM((d, n_up), F32), pltpu.VMEM((EXPERT_DIM, d), F32),
                        pltpu.VMEM((d, n_up), BF16), pltpu.VMEM((EXPERT_DIM, d), BF16),
                        pltpu.SemaphoreType.DMA((2,))],
    )
    return pl.pallas_call(
        functools.partial(_expert_kernel, e0),
        grid_spec=grid_spec,
        out_shape=jax.ShapeDtypeStruct((p, dh), jnp.int32),
        compiler_params=_cparams(("arbitrary",)),
        name="expert_ffn",
    )(block_e, next_e, valid, n_used, xbuf, w_up, b_up, w_down, b_down)


def _combine_kernel(x1_ref, mod_ref, yg_ref, info_ref, lng_ref, lnb_ref, o_ref):
    info = info_ref[...]
    y = jnp.zeros(x1_ref.shape, F32)
    for k in range(TOP_K):
        gate = info[:, 2 * TOP_K + k:2 * TOP_K + k + 1]
        y = y + gate * _unpack_bf16_pairs(yg_ref[k])
    g2 = mod_ref[0, 5:6, :]
    o_ref[...] = _layer_norm(DEEPNORM_ALPHA * x1_ref[...] + g2 * y, lng_ref[...], lnb_ref[...])


def _combine(x1, mod, yg, info, ln_g, ln_b, seq):
    t, d = x1.shape
    tm = min(IN_TILE, seq)
    row = lambda i: (i, 0)
    fixed = lambda i: (0, 0)
    return pl.pallas_call(
        _combine_kernel,
        grid=(t // tm,),
        in_specs=[
            pl.BlockSpec((tm, d), row),
            pl.BlockSpec((1, 6, d), lambda i: ((i * tm) // seq, 0, 0)),
            pl.BlockSpec((TOP_K, tm, d // 2), lambda i: (0, i, 0)),
            pl.BlockSpec((tm, LANES), row),
            pl.BlockSpec((1, d), fixed),
            pl.BlockSpec((1, d), fixed),
        ],
        out_specs=pl.BlockSpec((tm, d), row),
        out_shape=jax.ShapeDtypeStruct((t, d), F32),
        compiler_params=_cparams(("arbitrary",)),
        name="moe_combine_ln",
    )(x1, mod, yg, info, ln_g.reshape(1, d), ln_b.reshape(1, d))


def _sc_workers():
    info = plsc.get_sparse_core_info()
    return info.num_cores, info.num_cores * info.num_subcores


def _sc_scatter_rows(rows, idx, n_out):
    t, w = rows.shape
    kk = idx.shape[0]
    n_cores, n_workers = _sc_workers()
    ch = SC_CHUNK
    assert t % (2 * n_workers * ch) == 0
    n_chunk = t // (n_workers * ch)
    idx_c = jnp.transpose(idx.reshape(kk, t // ch, ch), (1, 0, 2))

    @functools.partial(
        pl.kernel,
        mesh=plsc.VectorSubcoreMesh(core_axis_name="c", subcore_axis_name="s"),
        out_type=jax.ShapeDtypeStruct((n_out, w), rows.dtype),
        scratch_types=[pltpu.VMEM((2, kk, ch), jnp.int32), pltpu.VMEM((2, ch, w), rows.dtype),
                       pltpu.SemaphoreType.DMA((2,)), pltpu.SemaphoreType.DMA((2,))],
        name="sc_dispatch_scatter",
    )
    def scatter_kernel(rows_hbm, idx_hbm, out_hbm, idx_v, rows_v, load_sem, scat_sem):
        base = (lax.axis_index("s") * n_cores + lax.axis_index("c")) * n_chunk

        def load(j, b):
            return pltpu.make_async_copy(rows_hbm.at[pl.ds((base + j) * ch, ch)], rows_v.at[b], load_sem.at[b])

        def scatters(b):
            return [pltpu.make_async_copy(rows_v.at[b], out_hbm.at[idx_v.at[b, q]], scat_sem.at[b])
                    for q in range(kk)]

        pltpu.sync_copy(idx_hbm.at[base], idx_v.at[0])
        load(0, 0).start()

        @pl.loop(0, n_chunk, step=2)
        def _(j0):
            for b in range(2):
                j = j0 + b
                other = 1 - b

                @pl.when(j >= 1)
                def _():
                    for cp in scatters(other):
                        cp.wait()

                @pl.when(j + 1 < n_chunk)
                def _():
                    pltpu.sync_copy(idx_hbm.at[base + j + 1], idx_v.at[other])
                    load(j + 1, other).start()

                load(j, b).wait()
                for cp in scatters(b):
                    cp.start()

        for cp in scatters((n_chunk - 1) % 2):
            cp.wait()

    return scatter_kernel(rows, idx_c)


def _sc_gather_rows(table, idx):
    m = idx.shape[0]
    w = table.shape[1]
    n_cores, n_workers = _sc_workers()
    ch = SC_CHUNK
    assert m % (2 * n_workers * ch) == 0
    n_chunk = m // (n_workers * ch)
    idx_c = idx.reshape(m // ch, 1, ch)

    @functools.partial(
        pl.kernel,
        mesh=plsc.VectorSubcoreMesh(core_axis_name="c", subcore_axis_name="s"),
        out_type=jax.ShapeDtypeStruct((m, w), table.dtype),
        scratch_types=[pltpu.VMEM((2, 1, ch), jnp.int32), pltpu.VMEM((2, ch, w), table.dtype),
                       pltpu.SemaphoreType.DMA((2,)), pltpu.SemaphoreType.DMA((2,))],
        name="sc_combine_gather",
    )
    def gather_kernel(table_hbm, idx_hbm, out_hbm, idx_v, rows_v, gather_sem, write_sem):
        base = (lax.axis_index("s") * n_cores + lax.axis_index("c")) * n_chunk

        def gather(b):
            return pltpu.make_async_copy(table_hbm.at[idx_v.at[b, 0]], rows_v.at[b], gather_sem.at[b])

        def write(j, b):
            return pltpu.make_async_copy(rows_v.at[b], out_hbm.at[pl.ds((base + j) * ch, ch)], write_sem.at[b])

        pltpu.sync_copy(idx_hbm.at[base], idx_v.at[0])
        gather(0).start()

        @pl.loop(0, n_chunk, step=2)
        def _(j0):
            for b in range(2):
                j = j0 + b
                other = 1 - b

                @pl.when(j >= 1)
                def _():
                    write(j - 1, other).wait()

                @pl.when(j + 1 < n_chunk)
                def _():
                    pltpu.sync_copy(idx_hbm.at[base + j + 1], idx_v.at[other])
                    gather(other).start()

                gather(b).wait()
                write(j, b).start()

        write(n_chunk - 1, (n_chunk - 1) % 2).wait()

    return gather_kernel(table, idx_c)


def _lane_vector(vals, offset):
    return jnp.zeros((1, LANES), F32).at[0, offset:offset + vals.shape[0]].set(vals.astype(F32))


def _moe(h2, info, er, cnt, x1, mod, ln_g, ln_b, w_up, b_up, w_down, b_down, layer, seq):
    t, dh = h2.shape
    a = t * TOP_K
    bm = EXPERT_BLOCK
    slots, pcum_v = _slots(er, cnt)
    pcum = pcum_v[0, :N_EXPERTS]
    dest = slots[:TOP_K]
    n_blocks = -(-a // bm) + N_EXPERTS
    starts = jnp.arange(n_blocks, dtype=jnp.int32) * bm
    block_e = jnp.minimum(jnp.sum(pcum[None, :] <= starts[:, None], axis=1), N_EXPERTS - 1).astype(jnp.int32)
    n_used = (pcum[-1] // bm).astype(jnp.int32).reshape(1)
    later = block_e[None, :] > block_e[:, None]
    group_end = n_blocks - jnp.sum(later, axis=1)
    next_e = jnp.min(jnp.where(later, block_e[None, :], N_EXPERTS), axis=1)
    next_e = jnp.where(group_end < n_used[0], next_e, -1).astype(jnp.int32)
    counts = cnt[0, :N_EXPERTS].astype(jnp.int32)
    pstart = pcum - ((counts + bm - 1) // bm) * bm
    mine = block_e[:, None] == jnp.arange(N_EXPERTS, dtype=jnp.int32)[None, :]
    count_b = jnp.sum(jnp.where(mine, counts[None, :], 0), axis=1)
    pstart_b = jnp.sum(jnp.where(mine, pstart[None, :], 0), axis=1)
    valid = jnp.clip(count_b - (starts - pstart_b), 0, bm).astype(jnp.int32)
    xbuf = _sc_scatter_rows(h2, dest, n_blocks * bm)
    ybuf = _expert_ffn(xbuf, block_e, next_e, valid, n_used, w_up, b_up, w_down, b_down, layer)
    yg = _sc_gather_rows(ybuf, dest.reshape(a)).reshape(TOP_K, t, dh)
    return _combine(x1, mod, yg, info, ln_g, ln_b, seq)


def kernel(x, c, rel_bias, w_in, w_out, w_ada, b_ada, ln1_g, ln1_b, ln2_g, ln2_b, pool_w, pool_scale,
           attn_sinks, conv_w, gdn_a_log, gdn_dt_bias, gdn_norm_w, router_w, router_b,
           exp_w_up, exp_b_up, exp_w_down, exp_b_down):
    bsz, seq, d = x.shape
    depth = w_in.shape[0]
    t = bsz * seq
    assert d == D_MODEL and w_in.shape[2] == IN_DIM
    assert seq % GDN_SUPER == 0 and seq % (ATT_BLOCKS * WINDOW) == 0
    assert t % ROW_TILE == 0 and seq % ROW_TILE == 0

    mod_all = _modulation(c, w_ada, b_ada).reshape(depth, bsz, 6, d)
    bias = _band_bias(rel_bias)

    w_up_all = exp_w_up.reshape((depth * N_EXPERTS,) + exp_w_up.shape[2:])
    b_up_all = exp_b_up.reshape(depth * N_EXPERTS, 1, exp_b_up.shape[2])
    w_down_all = exp_w_down.reshape((depth * N_EXPERTS,) + exp_w_down.shape[2:])
    b_down_all = exp_b_down.reshape(depth * N_EXPERTS, 1, exp_b_down.shape[2])

    x2d = x.reshape(t, d)
    for l in range(depth):
        mod = mod_all[l]
        w_out_p = _take_static(w_out[l], _OUT_PERM, 0).astype(BF16)
        ident = jnp.zeros((CONV_WIDTH, 1), F32).at[CONV_WIDTH - 1, 0].set(1.0)
        conv_p = jnp.where(jnp.asarray(_GDN_CONV_SRC >= 0), _take_cols(conv_w[l].astype(F32), _GDN_CONV_SRC),
                           ident)
        pool_bd = jnp.zeros((POOL_DIM, POOL_DIM), F32)
        for gi in range(len(POOL_WINDOWS)):
            sl = slice(gi * POOL_GROUP, (gi + 1) * POOL_GROUP)
            pool_bd = pool_bd.at[sl, sl].set(pool_w[l, gi].astype(F32))
        alog_v = _lane_vector(gdn_a_log[l], N_GDN_HEADS)
        dtb_v = _lane_vector(gdn_dt_bias[l], N_GDN_HEADS)
        nw_v = jnp.tile(gdn_norm_w[l].astype(F32), 2).reshape(1, LANES)
        rw = jnp.zeros((d, LANES), F32).at[:, :N_EXPERTS].set(router_w[l].astype(F32))
        rw_hi, rw_lo = _split_bf16(rw)
        rb = jnp.full((1, LANES), NEG_INF, F32).at[0, :N_EXPERTS].set(router_b[l].astype(F32))

        u_pool, aq, akv, gdn, ba = _in_projection(x2d, mod, w_in.astype(F32), l, seq)
        y_pool = _pool_mixer(u_pool, pool_bd.astype(BF16), pool_scale[l].astype(F32), seq)
        y_att = _swa_attention(aq, akv, bias, attn_sinks[l].astype(F32), seq)
        y_gdn = _gdn_mixer(gdn, ba, conv_p, alog_v, dtb_v, nw_v, seq)
        x1, h2, info, er, cnt = _out_projection(x2d, mod, y_pool, y_att, y_gdn, w_out_p, ln1_g[l], ln1_b[l],
                                            rw_hi, rw_lo, rb, seq)
        x2d = _moe(h2, info, er, cnt, x1, mod, ln2_g[l], ln2_b[l], w_up_all, b_up_all, w_down_all, b_down_all,
                   l, seq)
    return x2d.reshape(bsz, seq, d)
```

```python
import functools

import numpy as np
import jax
import jax.numpy as jnp
from jax import lax
from jax.experimental import pallas as pl
from jax.experimental.pallas import tpu as pltpu
from jax.experimental.pallas import tpu_sc as plsc

F32 = jnp.float32
BF16 = jnp.bfloat16

D_MODEL = 1024
HEAD_DIM = 64
POOL_DIM = 256
POOL_WINDOWS = (2, 4, 8, 16)
POOL_GROUP = 64
N_ATT_HEADS = 6
N_KV_HEADS = 2
ATT_DIM = 384
KV_DIM = 128
WINDOW = 128
N_BUCKETS = 32
MAX_DISTANCE = 128
N_GDN_HEADS = 6
GDN_DIM = 384
CONV_WIDTH = 4
GDN_CHUNK = 64
N_EXPERTS = 32
TOP_K = 4
EXPERT_DIM = 1024
SWIGLU_ALPHA = 1.702
SWIGLU_LIMIT = 7.0
DEPTH = 2
DEEPNORM_ALPHA = (2 * DEPTH) ** 0.25
LN_EPS = 1e-5
NORM_EPS = 1e-6
NEG_INF = -1e30

LANES = 128
SUBLANES = 8
VMEM_LIMIT = 56 * 1024 * 1024

ROW_TILE = 512
OUT_SPLIT = 2
IN_TILE = 1024
ATT_BLOCKS = 8
ATT_GROUP = 6
GDN_SUPER = 256
EXPERT_BLOCK = 512
SC_CHUNK = 64

_OFF_AQ = POOL_DIM
_OFF_AK = _OFF_AQ + ATT_DIM
_OFF_AV = _OFF_AK + KV_DIM
_OFF_GQ = _OFF_AV + KV_DIM
_OFF_GK = _OFF_GQ + GDN_DIM
_OFF_GV = _OFF_GK + GDN_DIM
_OFF_GZ = _OFF_GV + GDN_DIM
_OFF_GB = _OFF_GZ + GDN_DIM
_OFF_GA = _OFF_GB + N_GDN_HEADS
IN_DIM = _OFF_GA + N_GDN_HEADS

P_POOL = (0, POOL_DIM)
P_Q = (P_POOL[1], P_POOL[1] + ATT_DIM)
P_KV = (P_Q[1], P_Q[1] + 2 * KV_DIM)
P_GDN = (P_KV[1], P_KV[1] + 4 * GDN_DIM)
P_BA = (P_GDN[1], P_GDN[1] + LANES)
P_TOTAL = P_BA[1]


def _head_cols(off, h):
    return list(range(off + HEAD_DIM * h, off + HEAD_DIM * (h + 1)))


def _build_in_perm():
    cols = list(range(POOL_DIM))
    for p in range(N_ATT_HEADS // 2):
        cols += _head_cols(_OFF_AQ, p) + _head_cols(_OFF_AQ, p + 3)
    cols += list(range(_OFF_AK, _OFF_AK + 2 * KV_DIM))
    gdn_src = []
    for p in range(N_GDN_HEADS // 2):
        e, o = 2 * p, 2 * p + 1
        grp = (_head_cols(_OFF_GK, e) + _head_cols(_OFF_GQ, e)
               + _head_cols(_OFF_GQ, o) + _head_cols(_OFF_GK, o)
               + _head_cols(_OFF_GV, o) + _head_cols(_OFF_GV, e)
               + _head_cols(_OFF_GZ, o) + _head_cols(_OFF_GZ, e))
        cols += grp
        gdn_src += [c - _OFF_GQ if c < _OFF_GZ else -1 for c in grp]
    cols += list(range(_OFF_GB, _OFF_GB + 2 * N_GDN_HEADS))
    cols += [-1] * (LANES - 2 * N_GDN_HEADS)
    assert len(cols) == P_TOTAL
    return np.asarray(cols, np.int32), np.asarray(gdn_src, np.int32)


_IN_PERM, _GDN_CONV_SRC = _build_in_perm()


def _build_out_perm():
    rows = list(range(POOL_DIM))
    for p in range(N_ATT_HEADS // 2):
        rows += _head_cols(POOL_DIM, p) + _head_cols(POOL_DIM, p + 3)
    for p in range(N_GDN_HEADS // 2):
        rows += _head_cols(POOL_DIM + ATT_DIM, 2 * p + 1) + _head_cols(POOL_DIM + ATT_DIM, 2 * p)
    return np.asarray(rows, np.int32)


_OUT_PERM = _build_out_perm()


def _t5_bucket_line():
    n = np.maximum(2 * WINDOW - 1 - np.arange(3 * WINDOW - 1), 0)
    max_exact = N_BUCKETS // 2
    nf = np.maximum(n, 1).astype(np.float32)
    large = max_exact + (np.log(nf / max_exact) / np.float32(np.log(MAX_DISTANCE / max_exact))
                         * (N_BUCKETS - max_exact)).astype(np.int32)
    large = np.minimum(large, N_BUCKETS - 1)
    return np.where(n < max_exact, n, large).astype(np.int32)


_BUCKET_LINE = _t5_bucket_line()


def _band_bias(rel_bias):
    n_line = 3 * WINDOW - 1
    line = jnp.take(rel_bias.astype(F32), jnp.asarray(_BUCKET_LINE), axis=0).T
    heads = line.shape[0]
    padded = jnp.concatenate([line, jnp.zeros((heads, 1), F32)], axis=1)
    skew = jnp.tile(padded, (1, WINDOW))[:, :WINDOW * n_line].reshape(heads, WINDOW, n_line)
    return skew[:, :, WINDOW - 1:3 * WINDOW - 1]


def _take_static(w, perm, axis):
    parts = []
    start = 0
    for i in range(1, len(perm) + 1):
        run_ends = (i == len(perm) or ((perm[i] < 0) != (perm[i - 1] < 0))
                    or (perm[i] >= 0 and perm[i] != perm[i - 1] + 1))
        if run_ends:
            if perm[start] < 0:
                shape = list(w.shape)
                shape[axis] = i - start
                parts.append(jnp.zeros(shape, w.dtype))
            else:
                parts.append(lax.slice_in_dim(w, int(perm[start]), int(perm[start]) + (i - start), axis=axis))
            start = i
    return jnp.concatenate(parts, axis=axis)


def _take_cols(w, perm):
    return _take_static(w, perm, w.ndim - 1)


def _split_bf16(x):
    hi = x.astype(BF16)
    lo = (x - hi.astype(F32)).astype(BF16)
    return hi, lo


def _pack_bf16_pairs(x):
    n = x.shape[1] // 2
    bits = pltpu.bitcast(x.astype(BF16).astype(F32), jnp.int32)
    return lax.shift_right_logical(bits[:, :n], 16) | bits[:, n:]


def _unpack_bf16_pairs(u):
    lo = pltpu.bitcast(lax.shift_left(u, 16), F32)
    hi = pltpu.bitcast(u & jnp.int32(-65536), F32)
    return jnp.concatenate([lo, hi], axis=1)


def _dot(a, b):
    return jnp.dot(a, b, preferred_element_type=F32)


def _dot_nt(a, b):
    return lax.dot_general(a, b, (((1,), (1,)), ((), ())), preferred_element_type=F32)


def _sigmoid(x):
    return 1.0 / (1.0 + jnp.exp(-x))


def _layer_norm(r, g, b):
    mu = jnp.mean(r, axis=-1, keepdims=True)
    d = r - mu
    var = jnp.mean(d * d, axis=-1, keepdims=True)
    return d * lax.rsqrt(var + LN_EPS) * g + b


def _cparams(sem):
    return pltpu.CompilerParams(dimension_semantics=sem, vmem_limit_bytes=VMEM_LIMIT)


def _mod_kernel(c_ref, w_ref, b_ref, o_ref):
    c = c_ref[...]
    ca = c * _sigmoid(c)
    ch, cl = _split_bf16(ca)
    wh, wl = _split_bf16(w_ref[0])
    o_ref[0] = _dot(ch, wh) + _dot(cl, wh) + _dot(ch, wl) + b_ref[0]


def _modulation(c, w_ada, b_ada):
    depth, d, n = w_ada.shape
    bsz = c.shape[0]
    tn = 512
    return pl.pallas_call(
        _mod_kernel,
        grid=(depth, n // tn),
        in_specs=[
            pl.BlockSpec((bsz, d), lambda l, j: (0, 0)),
            pl.BlockSpec((1, d, tn), lambda l, j: (l, 0, j)),
            pl.BlockSpec((1, 1, tn), lambda l, j: (l, 0, j)),
        ],
        out_specs=pl.BlockSpec((1, bsz, tn), lambda l, j: (l, 0, j)),
        out_shape=jax.ShapeDtypeStruct((depth, bsz, n), F32),
        compiler_params=_cparams(("arbitrary", "arbitrary")),
        name="adaln_mod",
    )(c, w_ada, b_ada.reshape(depth, 1, n))


def _perm_runs(perm):
    runs = []
    start = 0
    for i in range(1, len(perm) + 1):
        run_ends = (i == len(perm) or ((perm[i] < 0) != (perm[i - 1] < 0))
                    or (perm[i] >= 0 and perm[i] != perm[i - 1] + 1))
        if run_ends:
            runs.append((int(perm[start]) if perm[start] >= 0 else -1, i - start, start))
            start = i
    return runs


_IN_RUNS = _perm_runs(_IN_PERM)


def _inproj_kernel(layer, x_ref, mod_ref, wt_hbm, pool_ref, q_ref, kv_ref, gdn_ref, ba_ref,
                   wt_f32, tail_ref, wt_ref, sem):
    @pl.when(pl.program_id(0) == 0)
    def _():
        n_real = wt_hbm.shape[0]
        pad0 = (n_real // SUBLANES) * SUBLANES
        bulk = pltpu.make_async_copy(wt_hbm.at[pl.ds(0, pad0), layer, :], wt_f32.at[pl.ds(0, pad0)], sem.at[0])
        tail = pltpu.make_async_copy(wt_hbm.at[pl.ds(n_real - SUBLANES, SUBLANES), layer, :], tail_ref, sem.at[1])
        bulk.start()
        tail.start()
        bulk.wait()
        tail.wait()
        row8 = lax.broadcasted_iota(jnp.int32, tail_ref.shape, 0)
        left = n_real - pad0
        wt_f32[pad0:pad0 + SUBLANES, :] = jnp.where(row8 < left, pltpu.roll(tail_ref[...], left, axis=0), 0.0)
        wt_f32[pad0 + SUBLANES:, :] = jnp.zeros((wt_f32.shape[0] - pad0 - SUBLANES, wt_f32.shape[1]), F32)
        for src, n, dst in _IN_RUNS:
            if src >= 0:
                rows = n if src + n < n_real else wt_ref.shape[0] - dst
                wt_ref[dst:dst + rows, :] = wt_f32[src:src + rows, :].astype(BF16)
        q_rows = wt_ref[P_Q[0]:P_Q[1], :].astype(F32) * (HEAD_DIM ** -0.5)
        wt_ref[P_Q[0]:P_Q[1], :] = q_rows.astype(BF16)

    sh = mod_ref[0, 0:1, :]
    sc = mod_ref[0, 1:2, :]
    h = (x_ref[...] * (1.0 + sc) + sh).astype(BF16)

    def mm(rng):
        return _dot_nt(h, wt_ref[rng[0]:rng[1], :])

    pool_ref[...] = mm(P_POOL)
    q_ref[...] = mm(P_Q).astype(BF16)
    kv_ref[...] = mm(P_KV).astype(BF16)
    gdn_ref[...] = mm(P_GDN)
    ba_ref[...] = mm(P_BA)


def _in_projection(x2d, mod, w_in, layer, seq):
    t, d = x2d.shape
    tm = min(IN_TILE, seq)
    widths = [r[1] - r[0] for r in (P_POOL, P_Q, P_KV, P_GDN, P_BA)]
    dtypes = [F32, BF16, BF16, F32, F32]
    return pl.pallas_call(
        functools.partial(_inproj_kernel, layer),
        grid=(t // tm,),
        in_specs=[
            pl.BlockSpec((tm, d), lambda i: (i, 0)),
            pl.BlockSpec((1, 6, d), lambda i: ((i * tm) // seq, 0, 0)),
            pl.BlockSpec(memory_space=pl.ANY),
        ],
        out_specs=[pl.BlockSpec((tm, w), lambda i: (i, 0)) for w in widths],
        out_shape=[jax.ShapeDtypeStruct((t, w), dt) for w, dt in zip(widths, dtypes)],
        scratch_shapes=[pltpu.VMEM((P_TOTAL, d), F32), pltpu.VMEM((SUBLANES, d), F32),
                        pltpu.VMEM((P_TOTAL, d), BF16), pltpu.SemaphoreType.DMA((2,))],
        compiler_params=_cparams(("arbitrary",)),
        name="in_proj",
    )(x2d, mod, jnp.transpose(w_in, (2, 0, 1)))


def _pool_kernel(u_ref, w_ref, scale_ref, o_ref):
    u = u_ref[...]
    row = lax.broadcasted_iota(jnp.int32, u.shape, 0)
    lane = lax.broadcasted_iota(jnp.int32, u.shape, 1)

    def shifted(a, s):
        return jnp.where(row >= s, pltpu.roll(a, s, axis=0), 0.0)

    sums = []
    acc = u
    for wdt in POOL_WINDOWS:
        acc = acc + shifted(acc, wdt // 2)
        sums.append(acc)
    grp = lane // POOL_GROUP
    wsum = sums[-1]
    win = jnp.full(u.shape, POOL_WINDOWS[-1], jnp.int32)
    for gi in range(len(POOL_WINDOWS) - 2, -1, -1):
        wsum = jnp.where(grp == gi, sums[gi], wsum)
        win = jnp.where(grp == gi, POOL_WINDOWS[gi], win)
    cnt = jnp.minimum(row + 1, win).astype(F32)
    p = wsum / cnt - u
    y = _dot(p.astype(BF16), w_ref[...]) * scale_ref[...]
    o_ref[...] = y.astype(BF16)


def _pool_mixer(u, pool_w_bd, pool_scale, seq):
    t, c = u.shape
    return pl.pallas_call(
        _pool_kernel,
        grid=(t // seq,),
        in_specs=[
            pl.BlockSpec((seq, c), lambda b: (b, 0)),
            pl.BlockSpec((c, c), lambda b: (0, 0)),
            pl.BlockSpec((1, c), lambda b: (0, 0)),
        ],
        out_specs=pl.BlockSpec((seq, c), lambda b: (b, 0)),
        out_shape=jax.ShapeDtypeStruct((t, c), BF16),
        compiler_params=_cparams(("arbitrary",)),
        name="pool_mixer",
    )(u, pool_w_bd, pool_scale.reshape(1, c))


def _attn_kernel(sink_ref, q_ref, kvc_ref, kvp_ref, bias_ref, o_ref):
    step = pl.program_id(1)
    qi = lax.broadcasted_iota(jnp.int32, (WINDOW, 2 * WINDOW), 0)
    kj = lax.broadcasted_iota(jnp.int32, (WINDOW, 2 * WINDOW), 1)
    dist = qi + WINDOW - kj
    in_band = (dist >= 0) & (dist < WINDOW)
    lo = lax.broadcasted_iota(jnp.int32, (WINDOW, LANES), 1) < HEAD_DIM
    heads = [(p, half) for p in range(N_ATT_HEADS // 2) for half in range(2)]
    sinks = [sink_ref[p + 3 * half] for p, half in heads]
    for sub in range(ATT_BLOCKS):
        r0 = sub * WINDOW
        prev = kvp_ref[...] if sub == 0 else kvc_ref[r0 - WINDOW:r0, :]
        kv = jnp.concatenate([prev, kvc_ref[r0:r0 + WINDOW, :]], axis=0)
        k = kv[:, :KV_DIM]
        v = kv[:, KV_DIM:]
        valid = in_band & ((kj >= WINDOW) | (step > 0)) if sub == 0 else in_band
        for g0 in range(0, len(heads), ATT_GROUP):
            group = heads[g0:g0 + ATT_GROUP]
            sk_g = sinks[g0:g0 + ATT_GROUP]
            scores = []
            for p, half in group:
                qp = q_ref[r0:r0 + WINDOW, p * LANES:(p + 1) * LANES]
                qm = jnp.where(lo if half == 0 else jnp.logical_not(lo), qp, jnp.zeros_like(qp))
                scores.append(jnp.where(valid, _dot_nt(qm, k) + bias_ref[p + 3 * half], NEG_INF))
            tops = [jnp.maximum(jnp.max(s, axis=-1, keepdims=True), sk) for s, sk in zip(scores, sk_g)]
            probs = [jnp.exp(s - m) for s, m in zip(scores, tops)]
            dens = [jnp.sum(pr, axis=-1, keepdims=True) + jnp.exp(sk - m)
                    for pr, sk, m in zip(probs, sk_g, tops)]
            outs = [_dot(pr.astype(BF16), v) / den for pr, den in zip(probs, dens)]
            for idx in range(0, len(group), 2):
                p = group[idx][0]
                o_ref[r0:r0 + WINDOW, p * LANES:(p + 1) * LANES] = (
                    jnp.where(lo, outs[idx], outs[idx + 1]).astype(BF16))


def _swa_attention(q, kv, bias, sinks, seq):
    t = q.shape[0]
    rows = ATT_BLOCKS * WINDOW
    nblk = seq // rows
    return pl.pallas_call(
        _attn_kernel,
        grid=(t // seq, nblk),
        in_specs=[
            pl.BlockSpec(memory_space=pltpu.SMEM),
            pl.BlockSpec((rows, ATT_DIM), lambda b, n: (b * nblk + n, 0)),
            pl.BlockSpec((rows, 2 * KV_DIM), lambda b, n: (b * nblk + n, 0)),
            pl.BlockSpec((WINDOW, 2 * KV_DIM),
                         lambda b, n: (jnp.maximum((b * nblk + n) * ATT_BLOCKS - 1, 0), 0)),
            pl.BlockSpec((N_ATT_HEADS, WINDOW, 2 * WINDOW), lambda b, n: (0, 0, 0)),
        ],
        out_specs=pl.BlockSpec((rows, ATT_DIM), lambda b, n: (b * nblk + n, 0)),
        out_shape=jax.ShapeDtypeStruct((t, ATT_DIM), BF16),
        compiler_params=_cparams(("arbitrary", "arbitrary")),
        name="swa_attention",
    )(sinks, q, kv, kv, bias)


_GDN_BASE = SUBLANES
_GDN_LEVELS = int(np.log2(GDN_CHUNK // _GDN_BASE))


def _gdn_masks():
    r = np.arange(GDN_SUPER)
    ri, ci = r[:, None], r[None, :]
    same_chunk = (ri // GDN_CHUNK) == (ci // GDN_CHUNK)
    incl = same_chunk & (ri >= ci)
    planes = [incl, ri == ci]
    base = ((ri // _GDN_BASE) == (ci // _GDN_BASE)) & (ri > ci)
    planes.append(base)
    for lvl in range(_GDN_LEVELS):
        small = _GDN_BASE << lvl
        planes.append(((ri // (2 * small)) == (ci // (2 * small))) & ((ri // small) != (ci // small)) & (ri > ci))
    bmask = np.stack(planes).astype(np.float32)
    bmask[2] = -bmask[2]
    negmask = np.where(incl, 0.0, -np.inf).astype(np.float32)
    return negmask, bmask


_GDN_NEGMASK, _GDN_BMASK = _gdn_masks()


def _gdn_kernel(x_ref, halo_ref, ba_ref, cw_ref, alog_ref, dtb_ref, nw_ref, negmask_ref, bmask_ref,
                y_ref, state_ref, xs_ref):
    sc_id = pl.program_id(1)
    rows = GDN_SUPER
    nchunk = rows // GDN_CHUNK
    c_sz = GDN_CHUNK

    @pl.when(sc_id == 0)
    def _():
        state_ref[...] = jnp.zeros_like(state_ref)

    xs_ref[:SUBLANES, :] = jnp.where(sc_id == 0, 0.0, halo_ref[...])
    xs_ref[SUBLANES:, :] = x_ref[...]
    act = []
    for g in range(x_ref.shape[1] // LANES):
        cols = slice(g * LANES, (g + 1) * LANES)
        acc = x_ref[:, cols] * cw_ref[CONV_WIDTH - 1:CONV_WIDTH, cols]
        if g % 4 != 3:
            for s in range(1, CONV_WIDTH):
                acc = acc + (xs_ref[SUBLANES - s:SUBLANES - s + rows, cols]
                             * cw_ref[CONV_WIDTH - 1 - s:CONV_WIDTH - s, cols])
        act.append(acc * _sigmoid(acc))

    negmask = negmask_ref[...]
    tri_incl = bmask_ref[0]
    eye_b = bmask_ref[1]
    base_neg = bmask_ref[2]
    bands = [bmask_ref[3 + lvl] for lvl in range(_GDN_LEVELS)]
    li = lax.broadcasted_iota(jnp.int32, (LANES, LANES), 0)
    lj = lax.broadcasted_iota(jnp.int32, (LANES, LANES), 1)
    half_ones = jnp.where((li // HEAD_DIM) == (lj // HEAD_DIM), 1.0, 0.0).astype(BF16)
    lane_lo = lax.broadcasted_iota(jnp.int32, (rows, LANES), 1) < HEAD_DIM

    ba = ba_ref[...]
    beta_all = _sigmoid(ba)
    sp_in = ba + dtb_ref[...]
    softplus = jnp.maximum(sp_in, 0.0) + jnp.log(1.0 + jnp.exp(-jnp.abs(sp_in)))
    g_all = -jnp.exp(alog_ref[...]) * softplus
    gcum = _dot_hi_exact_rhs_lhs(tri_incl, g_all)
    gcum_t = gcum.T

    heads = range(N_GDN_HEADS)
    lane_hi = jnp.logical_not(lane_lo)
    mk = [lane_lo if h % 2 == 0 else lane_hi for h in heads]
    scale = HEAD_DIM ** -0.5

    xk, xq, gn, gc_col, beta, eg = [], [], [], [], [], []
    for h in heads:
        g = act[4 * (h // 2) + (h % 2)]
        g = g * lax.rsqrt(_dot((g * g).astype(BF16), half_ones) + NORM_EPS)
        gn.append(g)
        xk.append(jnp.where(mk[h], g, 0.0))
        xq.append(jnp.where(mk[h], pltpu.roll(g, HEAD_DIM, axis=1), 0.0) * scale)
        beta.append(beta_all[:, h:h + 1])
        gc_col.append(gcum[:, N_GDN_HEADS + h:N_GDN_HEADS + h + 1])
        eg.append(jnp.exp(gc_col[h]))

    l_b, attn, rhs = [], [], []
    for h in heads:
        gc_row = gcum_t[N_GDN_HEADS + h:N_GDN_HEADS + h + 1, :]
        decay = jnp.exp(gc_col[h] - gc_row + negmask)
        xk_b = xk[h].astype(BF16)
        kk = _dot_nt((xk[h] * beta[h]).astype(BF16), xk_b)
        l_b.append((kk * decay).astype(BF16))
        attn.append((_dot_nt(xq[h].astype(BF16), xk_b) * decay).astype(BF16))
        vv = act[4 * (h // 2) + 2]
        rhs.append(jnp.where(mk[h], gn[h] * eg[h], vv) * beta[h])

    a1 = [l_b[h] * base_neg for h in heads]
    a2 = [_dot(a1[h], a1[h]).astype(BF16) for h in heads]
    a4 = [_dot(a2[h], a2[h]).astype(BF16) for h in heads]
    inv0 = [eye_b + a1[h] for h in heads]
    acc1 = [inv0[h].astype(F32) + _dot(a2[h], inv0[h]) for h in heads]
    inv_b = [(acc1[h] + _dot(a4[h], acc1[h].astype(BF16))).astype(BF16) for h in heads]
    for lvl in range(_GDN_LEVELS - 1):
        mid = [_dot(l_b[h] * bands[lvl], inv_b[h]).astype(BF16) for h in heads]
        inv_b = [inv_b[h] - _dot(inv_b[h], mid[h]).astype(BF16) for h in heads]
    half = [_dot(inv_b[h], rhs[h].astype(BF16)) for h in heads]
    mid = [_dot(l_b[h] * bands[_GDN_LEVELS - 1], half[h].astype(BF16)) for h in heads]
    sol = [half[h] - _dot(inv_b[h], mid[h].astype(BF16)) for h in heads]

    lane_lo_s = lax.broadcasted_iota(jnp.int32, (LANES, LANES), 1) < HEAD_DIM
    mk_s = [lane_lo_s if h % 2 == 0 else jnp.logical_not(lane_lo_s) for h in heads]
    sol_b = [sol[h].astype(BF16) for h in heads]
    attn_sol = [_dot(attn[h], sol_b[h]) for h in heads]
    q_eff = [(xq[h] * eg[h] - jnp.where(mk[h], attn_sol[h], 0.0)).astype(BF16) for h in heads]
    o_free = [jnp.where(mk[h], 0.0, attn_sol[h]) for h in heads]
    kw = [[] for _ in heads]
    ku = [[] for _ in heads]
    cdec = [[] for _ in heads]
    for c in range(nchunk):
        r0 = c * c_sz
        for h in heads:
            glast = gcum[r0 + c_sz - 1:r0 + c_sz, N_GDN_HEADS + h:N_GDN_HEADS + h + 1]
            kd_t = (xk[h][r0:r0 + c_sz] * jnp.exp(glast - gc_col[h][r0:r0 + c_sz])).T
            both = _dot(kd_t.astype(BF16), sol_b[h][r0:r0 + c_sz])
            kw[h].append(jnp.where(mk_s[h], both, 0.0).astype(BF16))
            ku[h].append(jnp.where(mk_s[h], 0.0, both))
            cdec[h].append(jnp.exp(glast))
    st = [state_ref[h] for h in heads]
    o_parts = [[] for _ in heads]
    for c in range(nchunk):
        r0 = c * c_sz
        for h in heads:
            lhs = jnp.concatenate([kw[h][c], q_eff[h][r0:r0 + c_sz]], axis=0)
            prod = _dot(lhs, st[h].astype(BF16))
            o_parts[h].append(prod[LANES:] + o_free[h][r0:r0 + c_sz])
            st[h] = st[h] * cdec[h][c] + ku[h][c] - prod[:LANES]
    for h in heads:
        state_ref[h] = st[h]

    for p in range(N_GDN_HEADS // 2):
        o_pair = [jnp.concatenate(o_parts[h], axis=0) for h in (2 * p, 2 * p + 1)]
        o = jnp.where(lane_lo, o_pair[1], o_pair[0])
        ms = _dot((o * o).astype(BF16), half_ones) * (1.0 / HEAD_DIM)
        zz = act[4 * p + 3]
        y = o * lax.rsqrt(ms + NORM_EPS) * nw_ref[...] * zz
        y_ref[:, p * LANES:(p + 1) * LANES] = y.astype(BF16)


def _dot_hi_exact_rhs_lhs(m_bf16, x):
    hi, lo = _split_bf16(x)
    return _dot(m_bf16, hi) + _dot(m_bf16, lo)


def _gdn_mixer(gdn, ba, conv_p, alog_v, dtb_v, nw_v, seq):
    t, c = gdn.shape
    rows = GDN_SUPER
    nsc = seq // rows
    hb = rows // SUBLANES
    return pl.pallas_call(
        _gdn_kernel,
        grid=(t // seq, nsc),
        in_specs=[
            pl.BlockSpec((rows, c), lambda b, s: (b * nsc + s, 0)),
            pl.BlockSpec((SUBLANES, c), lambda b, s: (jnp.maximum((b * nsc + s) * hb - 1, 0), 0)),
            pl.BlockSpec((rows, LANES), lambda b, s: (b * nsc + s, 0)),
            pl.BlockSpec((CONV_WIDTH, c), lambda b, s: (0, 0)),
            pl.BlockSpec((1, LANES), lambda b, s: (0, 0)),
            pl.BlockSpec((1, LANES), lambda b, s: (0, 0)),
            pl.BlockSpec((1, LANES), lambda b, s: (0, 0)),
            pl.BlockSpec((rows, rows), lambda b, s: (0, 0)),
            pl.BlockSpec((3 + _GDN_LEVELS, rows, rows), lambda b, s: (0, 0, 0)),
        ],
        out_specs=pl.BlockSpec((rows, GDN_DIM), lambda b, s: (b * nsc + s, 0)),
        out_shape=jax.ShapeDtypeStruct((t, GDN_DIM), BF16),
        scratch_shapes=[pltpu.VMEM((N_GDN_HEADS, LANES, LANES), F32), pltpu.VMEM((rows + SUBLANES, c), F32)],
        compiler_params=_cparams(("arbitrary", "arbitrary")),
        name="gdn_mixer",
    )(gdn, gdn, ba, conv_p, alog_v, dtb_v, nw_v, jnp.asarray(_GDN_NEGMASK), jnp.asarray(_GDN_BMASK, BF16))


def _route_tile(logits, before, carry_ref, live):
    shape = logits.shape
    lane = lax.broadcasted_iota(jnp.int32, shape, 1).astype(F32)
    work = logits
    vals, idxs = [], []
    for _k in range(TOP_K):
        m = jnp.max(work, axis=-1, keepdims=True)
        idx = jnp.min(jnp.where(work == m, lane, float(LANES)), axis=-1, keepdims=True)
        vals.append(m)
        idxs.append(idx)
        work = jnp.where(lane == idx, -jnp.inf, work)
    exps = [jnp.exp(v - vals[0]) for v in vals]
    den = exps[0] + exps[1] + exps[2] + exps[3]
    onehots = [lane == idx for idx in idxs]
    member = jnp.zeros(shape, F32)
    for oh in onehots:
        member = member + jnp.where(oh, 1.0, 0.0)
    rank = _dot(before, member.astype(BF16)) + carry_ref[...]
    carry_ref[...] = carry_ref[...] + live * jnp.sum(member, axis=0, keepdims=True)
    info = jnp.zeros(shape, F32)
    for k in range(TOP_K):
        rank_k = jnp.sum(jnp.where(onehots[k], rank, 0.0), axis=-1, keepdims=True)
        info = jnp.where(lane == float(k), idxs[k], info)
        info = jnp.where(lane == float(TOP_K + k), rank_k, info)
        info = jnp.where(lane == float(2 * TOP_K + k), exps[k] / den, info)
    return info


def _outproj_kernel(x_ref, mod_ref, yp_ref, ya_ref, yg_ref, wp_ref, wa_ref, wg_ref, lng_ref, lnb_ref,
                    rwh_ref, rwl_ref, rb_ref, before_ref, x1_ref, h2_ref, info_ref, er_ref, cnt_ref,
                    carry_ref, logit_s):
    step = pl.program_id(0)

    @pl.when(step == 0)
    def _():
        carry_ref[...] = jnp.zeros_like(carry_ref)
        logit_s[...] = jnp.zeros_like(logit_s)

    part = x_ref.shape[0] // OUT_SPLIT
    halves = [slice(j * part, (j + 1) * part) for j in range(OUT_SPLIT)]
    y = [_dot(yp_ref[r, :], wp_ref[...]) + _dot(ya_ref[r, :], wa_ref[...]) + _dot(yg_ref[r, :], wg_ref[...])
         for r in halves]
    live = jnp.where(step > 0, 1.0, 0.0)
    info = _route_tile(logit_s[...], before_ref[...], carry_ref, live)
    info_ref[...] = info
    er_ref[...] = info.T[:SUBLANES]
    cnt_ref[...] = carry_ref[...]
    g1 = mod_ref[0, 2:3, :]
    sh2 = mod_ref[0, 3:4, :]
    sc2 = mod_ref[0, 4:5, :]
    for r, y_r in zip(halves, y):
        x1 = _layer_norm(DEEPNORM_ALPHA * x_ref[r, :] + g1 * y_r, lng_ref[...], lnb_ref[...])
        x1_ref[r, :] = x1
        h2 = x1 * (1.0 + sc2) + sh2
        hh, hl = _split_bf16(h2)
        h2_ref[r, :] = _pack_bf16_pairs(h2)
        logit_s[r, :] = _dot(hh, rwh_ref[...]) + _dot(hl, rwh_ref[...]) + _dot(hh, rwl_ref[...]) + rb_ref[...]


def _out_projection(x2d, mod, yp, ya, yg, w_out_p, ln_g, ln_b, rw_hi, rw_lo, rb, seq):
    t, d = x2d.shape
    tm = ROW_TILE
    wp = w_out_p[:POOL_DIM]
    wa = w_out_p[POOL_DIM:POOL_DIM + ATT_DIM]
    wg = w_out_p[POOL_DIM + ATT_DIM:]
    last = t // tm - 1
    row = lambda i: (jnp.minimum(i, last), 0)
    routed = lambda i: (jnp.maximum(i - 1, 0), 0)
    fixed = lambda i: (0, 0)
    return pl.pallas_call(
        _outproj_kernel,
        grid=(t // tm + 1,),
        in_specs=[
            pl.BlockSpec((tm, d), row),
            pl.BlockSpec((1, 6, d), lambda i: ((jnp.minimum(i, last) * tm) // seq, 0, 0)),
            pl.BlockSpec((tm, POOL_DIM), row),
            pl.BlockSpec((tm, ATT_DIM), row),
            pl.BlockSpec((tm, GDN_DIM), row),
            pl.BlockSpec((POOL_DIM, d), fixed),
            pl.BlockSpec((ATT_DIM, d), fixed),
            pl.BlockSpec((GDN_DIM, d), fixed),
            pl.BlockSpec((1, d), fixed),
            pl.BlockSpec((1, d), fixed),
            pl.BlockSpec((d, LANES), fixed),
            pl.BlockSpec((d, LANES), fixed),
            pl.BlockSpec((1, LANES), fixed),
            pl.BlockSpec((tm, tm), fixed),
        ],
        out_specs=[pl.BlockSpec((tm, d), row), pl.BlockSpec((tm, d // 2), row), pl.BlockSpec((tm, LANES), routed),
                   pl.BlockSpec((SUBLANES, tm), lambda i: (0, jnp.maximum(i - 1, 0))),
                   pl.BlockSpec((1, LANES), fixed)],
        out_shape=[jax.ShapeDtypeStruct((t, d), F32), jax.ShapeDtypeStruct((t, d // 2), jnp.int32),
                   jax.ShapeDtypeStruct((t, LANES), F32), jax.ShapeDtypeStruct((SUBLANES, t), F32),
                   jax.ShapeDtypeStruct((1, LANES), F32)],
        scratch_shapes=[pltpu.VMEM((1, LANES), F32), pltpu.VMEM((tm, LANES), F32)],
        compiler_params=_cparams(("arbitrary",)),
        name="out_proj_ln_route",
    )(x2d, mod, yp, ya, yg, wp, wa, wg, ln_g.reshape(1, d), ln_b.reshape(1, d), rw_hi, rw_lo, rb,
      jnp.tril(jnp.ones((tm, tm), BF16), -1))


def _slot_kernel(er_ref, cnt_ref, dest_ref, pcum_ref):
    cnt = jnp.broadcast_to(cnt_ref[...], (SUBLANES, LANES))
    padded = jnp.floor((cnt + float(EXPERT_BLOCK - 1)) * (1.0 / EXPERT_BLOCK)) * float(EXPERT_BLOCK)
    lane8 = lax.broadcasted_iota(jnp.int32, (SUBLANES, LANES), 1)
    acc = padded
    step = 1
    while step < LANES:
        acc = acc + jnp.where(lane8 >= step, pltpu.roll(acc, step, axis=1), 0.0)
        step *= 2
    pcum_ref[...] = acc[:1].astype(jnp.int32)
    pstart = acc - padded

    er = er_ref[...]
    start = jnp.zeros(er.shape, F32)
    for e in range(N_EXPERTS):
        offset = jnp.sum(jnp.where(lane8 == e, pstart, 0.0), axis=-1, keepdims=True)
        start = jnp.where(er == float(e), offset, start)
    row = lax.broadcasted_iota(jnp.int32, er.shape, 0)
    slots = jnp.where(row < TOP_K, start + pltpu.roll(er, TOP_K, axis=0), 0.0)
    dest_ref[...] = slots.astype(jnp.int32)


def _slots(er, cnt):
    t = er.shape[1]
    return pl.pallas_call(
        _slot_kernel,
        grid=(1,),
        in_specs=[pl.BlockSpec((SUBLANES, t), lambda i: (0, 0)), pl.BlockSpec((1, LANES), lambda i: (0, 0))],
        out_specs=[pl.BlockSpec((SUBLANES, t), lambda i: (0, 0)), pl.BlockSpec((1, LANES), lambda i: (0, 0))],
        out_shape=[jax.ShapeDtypeStruct((SUBLANES, t), jnp.int32), jax.ShapeDtypeStruct((1, LANES), jnp.int32)],
        compiler_params=_cparams(("arbitrary",)),
        name="moe_slots",
    )(er, cnt)


def _expert_kernel(e0, be_ref, nxt_ref, val_ref, nu_ref, x_ref, wup_hbm, bup_ref, wdn_hbm, bdn_ref, y_ref,
                   wup_st, wdn_st, wup_bf, wdn_bf, sems):
    i = pl.program_id(0)
    e = be_ref[i]
    prev = be_ref[jnp.maximum(i - 1, 0)]
    used = i < nu_ref[0]

    def weight_copies(expert):
        return (pltpu.make_async_copy(wup_hbm.at[e0 + expert], wup_st, sems.at[0]),
                pltpu.make_async_copy(wdn_hbm.at[e0 + expert], wdn_st, sems.at[1]))

    @pl.when(i == 0)
    def _():
        for cp in weight_copies(e):
            cp.start()

    @pl.when(used & ((i == 0) | (e != prev)))
    def _():
        for cp in weight_copies(e):
            cp.wait()
        wup_bf[...] = wup_st[...].astype(BF16)
        wdn_bf[...] = wdn_st[...].astype(BF16)

        @pl.when(nxt_ref[i] >= 0)
        def _():
            for cp in weight_copies(nxt_ref[i]):
                cp.start()

    def ffn(rows):
        xb = _unpack_bf16_pairs(x_ref[:rows, :]).astype(BF16)
        hb = _dot(xb, wup_bf[...]) + bup_ref[0]
        x_glu = jnp.minimum(hb[:, :EXPERT_DIM], SWIGLU_LIMIT)
        x_lin = jnp.clip(hb[:, EXPERT_DIM:], -SWIGLU_LIMIT, SWIGLU_LIMIT)
        act = x_glu * _sigmoid(SWIGLU_ALPHA * x_glu) * (x_lin + 1.0)
        y = _dot(act.astype(BF16), wdn_bf[...]) + bdn_ref[0]
        y_ref[:rows, :] = _pack_bf16_pairs(y)

    half_rows = x_ref.shape[0] // 2
    small = val_ref[i] <= half_rows

    @pl.when(used & jnp.logical_not(small))
    def _():
        ffn(x_ref.shape[0])

    @pl.when(used & small)
    def _():
        ffn(half_rows)
        y_ref[half_rows:, :] = jnp.zeros((x_ref.shape[0] - half_rows, y_ref.shape[1]), y_ref.dtype)

    @pl.when(i >= nu_ref[0])
    def _():
        y_ref[...] = jnp.zeros_like(y_ref)


def _expert_ffn(xbuf, block_e, next_e, valid, n_used, w_up, b_up, w_down, b_down, layer):
    p, dh = xbuf.shape
    d = 2 * dh
    bm = EXPERT_BLOCK
    ne, _, n_up = w_up.shape
    e0 = layer * N_EXPERTS
    grid_spec = pltpu.PrefetchScalarGridSpec(
        num_scalar_prefetch=4,
        grid=(p // bm,),
        in_specs=[
            pl.BlockSpec((bm, dh), lambda i, be, nx, vl, nu: (i, 0)),
            pl.BlockSpec(memory_space=pl.ANY),
            pl.BlockSpec((1, 1, n_up), lambda i, be, nx, vl, nu: (e0 + be[i], 0, 0)),
            pl.BlockSpec(memory_space=pl.ANY),
            pl.BlockSpec((1, 1, d), lambda i, be, nx, vl, nu: (e0 + be[i], 0, 0)),
        ],
        out_specs=pl.BlockSpec((bm, dh), lambda i, be, nx, vl, nu: (i, 0)),
        scratch_shapes=[pltpu.VMEM((d, n_up), F32), pltpu.VMEM((EXPERT_DIM, d), F32),
                        pltpu.VMEM((d, n_up), BF16), pltpu.VMEM((EXPERT_DIM, d), BF16),
                        pltpu.SemaphoreType.DMA((2,))],
    )
    return pl.pallas_call(
        functools.partial(_expert_kernel, e0),
        grid_spec=grid_spec,
        out_shape=jax.ShapeDtypeStruct((p, dh), jnp.int32),
        compiler_params=_cparams(("arbitrary",)),
        name="expert_ffn",
    )(block_e, next_e, valid, n_used, xbuf, w_up, b_up, w_down, b_down)


def _combine_kernel(x1_ref, mod_ref, yg_ref, info_ref, lng_ref, lnb_ref, o_ref):
    info = info_ref[...]
    y = jnp.zeros(x1_ref.shape, F32)
    for k in range(TOP_K):
        gate = info[:, 2 * TOP_K + k:2 * TOP_K + k + 1]
        y = y + gate * _unpack_bf16_pairs(yg_ref[k])
    g2 = mod_ref[0, 5:6, :]
    o_ref[...] = _layer_norm(DEEPNORM_ALPHA * x1_ref[...] + g2 * y, lng_ref[...], lnb_ref[...])


def _combine(x1, mod, yg, info, ln_g, ln_b, seq):
    t, d = x1.shape
    tm = min(IN_TILE, seq)
    row = lambda i: (i, 0)
    fixed = lambda i: (0, 0)
    return pl.pallas_call(
        _combine_kernel,
        grid=(t // tm,),
        in_specs=[
            pl.BlockSpec((tm, d), row),
            pl.BlockSpec((1, 6, d), lambda i: ((i * tm) // seq, 0, 0)),
            pl.BlockSpec((TOP_K, tm, d // 2), lambda i: (0, i, 0)),
            pl.BlockSpec((tm, LANES), row),
            pl.BlockSpec((1, d), fixed),
            pl.BlockSpec((1, d), fixed),
        ],
        out_specs=pl.BlockSpec((tm, d), row),
        out_shape=jax.ShapeDtypeStruct((t, d), F32),
        compiler_params=_cparams(("arbitrary",)),
        name="moe_combine_ln",
    )(x1, mod, yg, info, ln_g.reshape(1, d), ln_b.reshape(1, d))


def _sc_workers():
    info = plsc.get_sparse_core_info()
    return info.num_cores, info.num_cores * info.num_subcores


def _sc_scatter_rows(rows, idx, n_out):
    t, w = rows.shape
    kk = idx.shape[0]
    n_cores, n_workers = _sc_workers()
    ch = SC_CHUNK
    assert t % (2 * n_workers * ch) == 0
    n_chunk = t // (n_workers * ch)
    idx_c = jnp.transpose(idx.reshape(kk, t // ch, ch), (1, 0, 2))

    @functools.partial(
        pl.kernel,
        mesh=plsc.VectorSubcoreMesh(core_axis_name="c", subcore_axis_name="s"),
        out_type=jax.ShapeDtypeStruct((n_out, w), rows.dtype),
        scratch_types=[pltpu.VMEM((2, kk, ch), jnp.int32), pltpu.VMEM((2, ch, w), rows.dtype),
                       pltpu.SemaphoreType.DMA((2,)), pltpu.SemaphoreType.DMA((2,))],
        name="sc_dispatch_scatter",
    )
    def scatter_kernel(rows_hbm, idx_hbm, out_hbm, idx_v, rows_v, load_sem, scat_sem):
        base = (lax.axis_index("s") * n_cores + lax.axis_index("c")) * n_chunk

        def load(j, b):
            return pltpu.make_async_copy(rows_hbm.at[pl.ds((base + j) * ch, ch)], rows_v.at[b], load_sem.at[b])

        def scatters(b):
            return [pltpu.make_async_copy(rows_v.at[b], out_hbm.at[idx_v.at[b, q]], scat_sem.at[b])
                    for q in range(kk)]

        pltpu.sync_copy(idx_hbm.at[base], idx_v.at[0])
        load(0, 0).start()

        @pl.loop(0, n_chunk, step=2)
        def _(j0):
            for b in range(2):
                j = j0 + b
                other = 1 - b

                @pl.when(j >= 1)
                def _():
                    for cp in scatters(other):
                        cp.wait()

                @pl.when(j + 1 < n_chunk)
                def _():
                    pltpu.sync_copy(idx_hbm.at[base + j + 1], idx_v.at[other])
                    load(j + 1, other).start()

                load(j, b).wait()
                for cp in scatters(b):
                    cp.start()

        for cp in scatters((n_chunk - 1) % 2):
            cp.wait()

    return scatter_kernel(rows, idx_c)


def _sc_gather_rows(table, idx):
    m = idx.shape[0]
    w = table.shape[1]
    n_cores, n_workers = _sc_workers()
    ch = SC_CHUNK
    assert m % (2 * n_workers * ch) == 0
    n_chunk = m // (n_workers * ch)
    idx_c = idx.reshape(m // ch, 1, ch)

    @functools.partial(
        pl.kernel,
        mesh=plsc.VectorSubcoreMesh(core_axis_name="c", subcore_axis_name="s"),
        out_type=jax.ShapeDtypeStruct((m, w), table.dtype),
        scratch_types=[pltpu.VMEM((2, 1, ch), jnp.int32), pltpu.VMEM((2, ch, w), table.dtype),
                       pltpu.SemaphoreType.DMA((2,)), pltpu.SemaphoreType.DMA((2,))],
        name="sc_combine_gather",
    )
    def gather_kernel(table_hbm, idx_hbm, out_hbm, idx_v, rows_v, gather_sem, write_sem):
        base = (lax.axis_index("s") * n_cores + lax.axis_index("c")) * n_chunk

        def gather(b):
            return pltpu.make_async_copy(table_hbm.at[idx_v.at[b, 0]], rows_v.at[b], gather_sem.at[b])

        def write(j, b):
            return pltpu.make_async_copy(rows_v.at[b], out_hbm.at[pl.ds((base + j) * ch, ch)], write_sem.at[b])

        pltpu.sync_copy(idx_hbm.at[base], idx_v.at[0])
        gather(0).start()

        @pl.loop(0, n_chunk, step=2)
        def _(j0):
            for b in range(2):
                j = j0 + b
                other = 1 - b

                @pl.when(j >= 1)
                def _():
                    write(j - 1, other).wait()

                @pl.when(j + 1 < n_chunk)
                def _():
                    pltpu.sync_copy(idx_hbm.at[base + j + 1], idx_v.at[other])
                    gather(other).start()

                gather(b).wait()
                write(j, b).start()

        write(n_chunk - 1, (n_chunk - 1) % 2).wait()

    return gather_kernel(table, idx_c)


def _lane_vector(vals, offset):
    return jnp.zeros((1, LANES), F32).at[0, offset:offset + vals.shape[0]].set(vals.astype(F32))


def _moe(h2, info, er, cnt, x1, mod, ln_g, ln_b, w_up, b_up, w_down, b_down, layer, seq):
    t, dh = h2.shape
    a = t * TOP_K
    bm = EXPERT_BLOCK
    slots, pcum_v = _slots(er, cnt)
    pcum = pcum_v[0, :N_EXPERTS]
    dest = slots[:TOP_K]
    n_blocks = -(-a // bm) + N_EXPERTS
    starts = jnp.arange(n_blocks, dtype=jnp.int32) * bm
    block_e = jnp.minimum(jnp.sum(pcum[None, :] <= starts[:, None], axis=1), N_EXPERTS - 1).astype(jnp.int32)
    n_used = (pcum[-1] // bm).astype(jnp.int32).reshape(1)
    later = block_e[None, :] > block_e[:, None]
    group_end = n_blocks - jnp.sum(later, axis=1)
    next_e = jnp.min(jnp.where(later, block_e[None, :], N_EXPERTS), axis=1)
    next_e = jnp.where(group_end < n_used[0], next_e, -1).astype(jnp.int32)
    counts = cnt[0, :N_EXPERTS].astype(jnp.int32)
    pstart = pcum - ((counts + bm - 1) // bm) * bm
    mine = block_e[:, None] == jnp.arange(N_EXPERTS, dtype=jnp.int32)[None, :]
    count_b = jnp.sum(jnp.where(mine, counts[None, :], 0), axis=1)
    pstart_b = jnp.sum(jnp.where(mine, pstart[None, :], 0), axis=1)
    valid = jnp.clip(count_b - (starts - pstart_b), 0, bm).astype(jnp.int32)
    xbuf = _sc_scatter_rows(h2, dest, n_blocks * bm)
    ybuf = _expert_ffn(xbuf, block_e, next_e, valid, n_used, w_up, b_up, w_down, b_down, layer)
    yg = _sc_gather_rows(ybuf, dest.reshape(a)).reshape(TOP_K, t, dh)
    return _combine(x1, mod, yg, info, ln_g, ln_b, seq)


def kernel(x, c, rel_bias, w_in, w_out, w_ada, b_ada, ln1_g, ln1_b, ln2_g, ln2_b, pool_w, pool_scale,
           attn_sinks, conv_w, gdn_a_log, gdn_dt_bias, gdn_norm_w, router_w, router_b,
           exp_w_up, exp_b_up, exp_w_down, exp_b_down):
    bsz, seq, d = x.shape
    depth = w_in.shape[0]
    t = bsz * seq
    assert d == D_MODEL and w_in.shape[2] == IN_DIM
    assert seq % GDN_SUPER == 0 and seq % (ATT_BLOCKS * WINDOW) == 0
    assert t % ROW_TILE == 0 and seq % ROW_TILE == 0

    mod_all = _modulation(c, w_ada, b_ada).reshape(depth, bsz, 6, d)
    bias = _band_bias(rel_bias)

    w_up_all = exp_w_up.reshape((depth * N_EXPERTS,) + exp_w_up.shape[2:])
    b_up_all = exp_b_up.reshape(depth * N_EXPERTS, 1, exp_b_up.shape[2])
    w_down_all = exp_w_down.reshape((depth * N_EXPERTS,) + exp_w_down.shape[2:])
    b_down_all = exp_b_down.reshape(depth * N_EXPERTS, 1, exp_b_down.shape[2])

    x2d = x.reshape(t, d)
    for l in range(depth):
        mod = mod_all[l]
        w_out_p = _take_static(w_out[l], _OUT_PERM, 0).astype(BF16)
        ident = jnp.zeros((CONV_WIDTH, 1), F32).at[CONV_WIDTH - 1, 0].set(1.0)
        conv_p = jnp.where(jnp.asarray(_GDN_CONV_SRC >= 0), _take_cols(conv_w[l].astype(F32), _GDN_CONV_SRC),
                           ident)
        pool_bd = jnp.zeros((POOL_DIM, POOL_DIM), F32)
        for gi in range(len(POOL_WINDOWS)):
            sl = slice(gi * POOL_GROUP, (gi + 1) * POOL_GROUP)
            pool_bd = pool_bd.at[sl, sl].set(pool_w[l, gi].astype(F32))
        alog_v = _lane_vector(gdn_a_log[l], N_GDN_HEADS)
        dtb_v = _lane_vector(gdn_dt_bias[l], N_GDN_HEADS)
        nw_v = jnp.tile(gdn_norm_w[l].astype(F32), 2).reshape(1, LANES)
        rw = jnp.zeros((d, LANES), F32).at[:, :N_EXPERTS].set(router_w[l].astype(F32))
        rw_hi, rw_lo = _split_bf16(rw)
        rb = jnp.full((1, LANES), NEG_INF, F32).at[0, :N_EXPERTS].set(router_b[l].astype(F32))

        u_pool, aq, akv, gdn, ba = _in_projection(x2d, mod, w_in.astype(F32), l, seq)
        y_pool = _pool_mixer(u_pool, pool_bd.astype(BF16), pool_scale[l].astype(F32), seq)
        y_att = _swa_attention(aq, akv, bias, attn_sinks[l].astype(F32), seq)
        y_gdn = _gdn_mixer(gdn, ba, conv_p, alog_v, dtb_v, nw_v, seq)
        x1, h2, info, er, cnt = _out_projection(x2d, mod, y_pool, y_att, y_gdn, w_out_p, ln1_g[l], ln1_b[l],
                                            rw_hi, rw_lo, rb, seq)
        x2d = _moe(h2, info, er, cnt, x1, mod, ln2_g[l], ln2_b[l], w_up_all, b_up_all, w_down_all, b_down_all,
                   l, seq)
    return x2d.reshape(bsz, seq, d)
```

```python
import functools

import numpy as np
import jax
import jax.numpy as jnp
from jax import lax
from jax.experimental import pallas as pl
from jax.experimental.pallas import tpu as pltpu
from jax.experimental.pallas import tpu_sc as plsc

F32 = jnp.float32
BF16 = jnp.bfloat16

D_MODEL = 1024
HEAD_DIM = 64
POOL_DIM = 256
POOL_WINDOWS = (2, 4, 8, 16)
POOL_GROUP = 64
N_ATT_HEADS = 6
N_KV_HEADS = 2
ATT_DIM = 384
KV_DIM = 128
WINDOW = 128
N_BUCKETS = 32
MAX_DISTANCE = 128
N_GDN_HEADS = 6
GDN_DIM = 384
CONV_WIDTH = 4
GDN_CHUNK = 64
N_EXPERTS = 32
TOP_K = 4
EXPERT_DIM = 1024
SWIGLU_ALPHA = 1.702
SWIGLU_LIMIT = 7.0
DEPTH = 2
DEEPNORM_ALPHA = (2 * DEPTH) ** 0.25
LN_EPS = 1e-5
NORM_EPS = 1e-6
NEG_INF = -1e30

LANES = 128
SUBLANES = 8
VMEM_LIMIT = 56 * 1024 * 1024

ROW_TILE = 512
OUT_SPLIT = 2
IN_TILE = 1024
ATT_BLOCKS = 8
ATT_GROUP = 6
GDN_SUPER = 256
EXPERT_BLOCK = 512
SC_CHUNK = 64

_OFF_AQ = POOL_DIM
_OFF_AK = _OFF_AQ + ATT_DIM
_OFF_AV = _OFF_AK + KV_DIM
_OFF_GQ = _OFF_AV + KV_DIM
_OFF_GK = _OFF_GQ + GDN_DIM
_OFF_GV = _OFF_GK + GDN_DIM
_OFF_GZ = _OFF_GV + GDN_DIM
_OFF_GB = _OFF_GZ + GDN_DIM
_OFF_GA = _OFF_GB + N_GDN_HEADS
IN_DIM = _OFF_GA + N_GDN_HEADS

P_POOL = (0, POOL_DIM)
P_Q = (P_POOL[1], P_POOL[1] + ATT_DIM)
P_KV = (P_Q[1], P_Q[1] + 2 * KV_DIM)
P_GDN = (P_KV[1], P_KV[1] + 4 * GDN_DIM)
P_BA = (P_GDN[1], P_GDN[1] + LANES)
P_TOTAL = P_BA[1]


def _head_cols(off, h):
    return list(range(off + HEAD_DIM * h, off + HEAD_DIM * (h + 1)))


def _build_in_perm():
    cols = list(range(POOL_DIM))
    for p in range(N_ATT_HEADS // 2):
        cols += _head_cols(_OFF_AQ, p) + _head_cols(_OFF_AQ, p + 3)
    cols += list(range(_OFF_AK, _OFF_AK + 2 * KV_DIM))
    gdn_src = []
    for p in range(N_GDN_HEADS // 2):
        e, o = 2 * p, 2 * p + 1
        grp = (_head_cols(_OFF_GK, e) + _head_cols(_OFF_GQ, e)
               + _head_cols(_OFF_GQ, o) + _head_cols(_OFF_GK, o)
               + _head_cols(_OFF_GV, o) + _head_cols(_OFF_GV, e)
               + _head_cols(_OFF_GZ, o) + _head_cols(_OFF_GZ, e))
        cols += grp
        gdn_src += [c - _OFF_GQ if c < _OFF_GZ else -1 for c in grp]
    cols += list(range(_OFF_GB, _OFF_GB + 2 * N_GDN_HEADS))
    cols += [-1] * (LANES - 2 * N_GDN_HEADS)
    assert len(cols) == P_TOTAL
    return np.asarray(cols, np.int32), np.asarray(gdn_src, np.int32)


_IN_PERM, _GDN_CONV_SRC = _build_in_perm()


def _build_out_perm():
    rows = list(range(POOL_DIM))
    for p in range(N_ATT_HEADS // 2):
        rows += _head_cols(POOL_DIM, p) + _head_cols(POOL_DIM, p + 3)
    for p in range(N_GDN_HEADS // 2):
        rows += _head_cols(POOL_DIM + ATT_DIM, 2 * p + 1) + _head_cols(POOL_DIM + ATT_DIM, 2 * p)
    return np.asarray(rows, np.int32)


_OUT_PERM = _build_out_perm()


def _t5_bucket_line():
    n = np.maximum(2 * WINDOW - 1 - np.arange(3 * WINDOW - 1), 0)
    max_exact = N_BUCKETS // 2
    nf = np.maximum(n, 1).astype(np.float32)
    large = max_exact + (np.log(nf / max_exact) / np.float32(np.log(MAX_DISTANCE / max_exact))
                         * (N_BUCKETS - max_exact)).astype(np.int32)
    large = np.minimum(large, N_BUCKETS - 1)
    return np.where(n < max_exact, n, large).astype(np.int32)


_BUCKET_LINE = _t5_bucket_line()


def _band_bias(rel_bias):
    n_line = 3 * WINDOW - 1
    line = jnp.take(rel_bias.astype(F32), jnp.asarray(_BUCKET_LINE), axis=0).T
    heads = line.shape[0]
    padded = jnp.concatenate([line, jnp.zeros((heads, 1), F32)], axis=1)
    skew = jnp.tile(padded, (1, WINDOW))[:, :WINDOW * n_line].reshape(heads, WINDOW, n_line)
    return skew[:, :, WINDOW - 1:3 * WINDOW - 1]


def _take_static(w, perm, axis):
    parts = []
    start = 0
    for i in range(1, len(perm) + 1):
        run_ends = (i == len(perm) or ((perm[i] < 0) != (perm[i - 1] < 0))
                    or (perm[i] >= 0 and perm[i] != perm[i - 1] + 1))
        if run_ends:
            if perm[start] < 0:
                shape = list(w.shape)
                shape[axis] = i - start
                parts.append(jnp.zeros(shape, w.dtype))
            else:
                parts.append(lax.slice_in_dim(w, int(perm[start]), int(perm[start]) + (i - start), axis=axis))
            start = i
    return jnp.concatenate(parts, axis=axis)


def _take_cols(w, perm):
    return _take_static(w, perm, w.ndim - 1)


def _split_bf16(x):
    hi = x.astype(BF16)
    lo = (x - hi.astype(F32)).astype(BF16)
    return hi, lo


def _pack_bf16_pairs(x):
    n = x.shape[1] // 2
    bits = pltpu.bitcast(x.astype(BF16).astype(F32), jnp.int32)
    return lax.shift_right_logical(bits[:, :n], 16) | bits[:, n:]


def _unpack_bf16_pairs(u):
    lo = pltpu.bitcast(lax.shift_left(u, 16), F32)
    hi = pltpu.bitcast(u & jnp.int32(-65536), F32)
    return jnp.concatenate([lo, hi], axis=1)


def _dot(a, b):
    return jnp.dot(a, b, preferred_element_type=F32)


def _dot_nt(a, b):
    return lax.dot_general(a, b, (((1,), (1,)), ((), ())), preferred_element_type=F32)


def _sigmoid(x):
    return 1.0 / (1.0 + jnp.exp(-x))


def _layer_norm(r, g, b):
    mu = jnp.mean(r, axis=-1, keepdims=True)
    d = r - mu
    var = jnp.mean(d * d, axis=-1, keepdims=True)
    return d * lax.rsqrt(var + LN_EPS) * g + b


def _cparams(sem):
    return pltpu.CompilerParams(dimension_semantics=sem, vmem_limit_bytes=VMEM_LIMIT)


def _mod_kernel(c_ref, w_ref, b_ref, o_ref):
    c = c_ref[...]
    ca = c * _sigmoid(c)
    ch, cl = _split_bf16(ca)
    wh, wl = _split_bf16(w_ref[0])
    o_ref[0] = _dot(ch, wh) + _dot(cl, wh) + _dot(ch, wl) + b_ref[0]


def _modulation(c, w_ada, b_ada):
    depth, d, n = w_ada.shape
    bsz = c.shape[0]
    tn = 512
    return pl.pallas_call(
        _mod_kernel,
        grid=(depth, n // tn),
        in_specs=[
            pl.BlockSpec((bsz, d), lambda l, j: (0, 0)),
            pl.BlockSpec((1, d, tn), lambda l, j: (l, 0, j)),
            pl.BlockSpec((1, 1, tn), lambda l, j: (l, 0, j)),
        ],
        out_specs=pl.BlockSpec((1, bsz, tn), lambda l, j: (l, 0, j)),
        out_shape=jax.ShapeDtypeStruct((depth, bsz, n), F32),
        compiler_params=_cparams(("arbitrary", "arbitrary")),
        name="adaln_mod",
    )(c, w_ada, b_ada.reshape(depth, 1, n))


def _perm_runs(perm):
    runs = []
    start = 0
    for i in range(1, len(perm) + 1):
        run_ends = (i == len(perm) or ((perm[i] < 0) != (perm[i - 1] < 0))
                    or (perm[i] >= 0 and perm[i] != perm[i - 1] + 1))
        if run_ends:
            runs.append((int(perm[start]) if perm[start] >= 0 else -1, i - start, start))
            start = i
    return runs


_IN_RUNS = _perm_runs(_IN_PERM)


def _inproj_kernel(layer, x_ref, mod_ref, wt_hbm, pool_ref, q_ref, kv_ref, gdn_ref, ba_ref,
                   wt_f32, tail_ref, wt_ref, sem):
    @pl.when(pl.program_id(0) == 0)
    def _():
        n_real = wt_hbm.shape[0]
        pad0 = (n_real // SUBLANES) * SUBLANES
        bulk = pltpu.make_async_copy(wt_hbm.at[pl.ds(0, pad0), layer, :], wt_f32.at[pl.ds(0, pad0)], sem.at[0])
        tail = pltpu.make_async_copy(wt_hbm.at[pl.ds(n_real - SUBLANES, SUBLANES), layer, :], tail_ref, sem.at[1])
        bulk.start()
        tail.start()
        bulk.wait()
        tail.wait()
        row8 = lax.broadcasted_iota(jnp.int32, tail_ref.shape, 0)
        left = n_real - pad0
        wt_f32[pad0:pad0 + SUBLANES, :] = jnp.where(row8 < left, pltpu.roll(tail_ref[...], left, axis=0), 0.0)
        wt_f32[pad0 + SUBLANES:, :] = jnp.zeros((wt_f32.shape[0] - pad0 - SUBLANES, wt_f32.shape[1]), F32)
        for src, n, dst in _IN_RUNS:
            if src >= 0:
                rows = n if src + n < n_real else wt_ref.shape[0] - dst
                wt_ref[dst:dst + rows, :] = wt_f32[src:src + rows, :].astype(BF16)
        q_rows = wt_ref[P_Q[0]:P_Q[1], :].astype(F32) * (HEAD_DIM ** -0.5)
        wt_ref[P_Q[0]:P_Q[1], :] = q_rows.astype(BF16)

    sh = mod_ref[0, 0:1, :]
    sc = mod_ref[0, 1:2, :]
    h = (x_ref[...] * (1.0 + sc) + sh).astype(BF16)

    def mm(rng):
        return _dot_nt(h, wt_ref[rng[0]:rng[1], :])

    pool_ref[...] = mm(P_POOL)
    q_ref[...] = mm(P_Q).astype(BF16)
    kv_ref[...] = mm(P_KV).astype(BF16)
    gdn_ref[...] = mm(P_GDN)
    ba_ref[...] = mm(P_BA)


def _in_projection(x2d, mod, w_in, layer, seq):
    t, d = x2d.shape
    tm = min(IN_TILE, seq)
    widths = [r[1] - r[0] for r in (P_POOL, P_Q, P_KV, P_GDN, P_BA)]
    dtypes = [F32, BF16, BF16, F32, F32]
    return pl.pallas_call(
        functools.partial(_inproj_kernel, layer),
        grid=(t // tm,),
        in_specs=[
            pl.BlockSpec((tm, d), lambda i: (i, 0)),
            pl.BlockSpec((1, 6, d), lambda i: ((i * tm) // seq, 0, 0)),
            pl.BlockSpec(memory_space=pl.ANY),
        ],
        out_specs=[pl.BlockSpec((tm, w), lambda i: (i, 0)) for w in widths],
        out_shape=[jax.ShapeDtypeStruct((t, w), dt) for w, dt in zip(widths, dtypes)],
        scratch_shapes=[pltpu.VMEM((P_TOTAL, d), F32), pltpu.VMEM((SUBLANES, d), F32),
                        pltpu.VMEM((P_TOTAL, d), BF16), pltpu.SemaphoreType.DMA((2,))],
        compiler_params=_cparams(("arbitrary",)),
        name="in_proj",
    )(x2d, mod, jnp.transpose(w_in, (2, 0, 1)))


def _pool_kernel(u_ref, w_ref, scale_ref, o_ref):
    u = u_ref[...]
    row = lax.broadcasted_iota(jnp.int32, u.shape, 0)
    lane = lax.broadcasted_iota(jnp.int32, u.shape, 1)

    def shifted(a, s):
        return jnp.where(row >= s, pltpu.roll(a, s, axis=0), 0.0)

    sums = []
    acc = u
    for wdt in POOL_WINDOWS:
        acc = acc + shifted(acc, wdt // 2)
        sums.append(acc)
    grp = lane // POOL_GROUP
    wsum = sums[-1]
    win = jnp.full(u.shape, POOL_WINDOWS[-1], jnp.int32)
    for gi in range(len(POOL_WINDOWS) - 2, -1, -1):
        wsum = jnp.where(grp == gi, sums[gi], wsum)
        win = jnp.where(grp == gi, POOL_WINDOWS[gi], win)
    cnt = jnp.minimum(row + 1, win).astype(F32)
    p = wsum / cnt - u
    y = _dot(p.astype(BF16), w_ref[...]) * scale_ref[...]
    o_ref[...] = y.astype(BF16)


def _pool_mixer(u, pool_w_bd, pool_scale, seq):
    t, c = u.shape
    return pl.pallas_call(
        _pool_kernel,
        grid=(t // seq,),
        in_specs=[
            pl.BlockSpec((seq, c), lambda b: (b, 0)),
            pl.BlockSpec((c, c), lambda b: (0, 0)),
            pl.BlockSpec((1, c), lambda b: (0, 0)),
        ],
        out_specs=pl.BlockSpec((seq, c), lambda b: (b, 0)),
        out_shape=jax.ShapeDtypeStruct((t, c), BF16),
        compiler_params=_cparams(("arbitrary",)),
        name="pool_mixer",
    )(u, pool_w_bd, pool_scale.reshape(1, c))


def _attn_kernel(sink_ref, q_ref, kvc_ref, kvp_ref, bias_ref, o_ref):
    step = pl.program_id(1)
    qi = lax.broadcasted_iota(jnp.int32, (WINDOW, 2 * WINDOW), 0)
    kj = lax.broadcasted_iota(jnp.int32, (WINDOW, 2 * WINDOW), 1)
    dist = qi + WINDOW - kj
    in_band = (dist >= 0) & (dist < WINDOW)
    lo = lax.broadcasted_iota(jnp.int32, (WINDOW, LANES), 1) < HEAD_DIM
    heads = [(p, half) for p in range(N_ATT_HEADS // 2) for half in range(2)]
    sinks = [sink_ref[p + 3 * half] for p, half in heads]
    for sub in range(ATT_BLOCKS):
        r0 = sub * WINDOW
        prev = kvp_ref[...] if sub == 0 else kvc_ref[r0 - WINDOW:r0, :]
        kv = jnp.concatenate([prev, kvc_ref[r0:r0 + WINDOW, :]], axis=0)
        k = kv[:, :KV_DIM]
        v = kv[:, KV_DIM:]
        valid = in_band & ((kj >= WINDOW) | (step > 0)) if sub == 0 else in_band
        for g0 in range(0, len(heads), ATT_GROUP):
            group = heads[g0:g0 + ATT_GROUP]
            sk_g = sinks[g0:g0 + ATT_GROUP]
            scores = []
            for p, half in group:
                qp = q_ref[r0:r0 + WINDOW, p * LANES:(p + 1) * LANES]
                qm = jnp.where(lo if half == 0 else jnp.logical_not(lo), qp, jnp.zeros_like(qp))
                scores.append(jnp.where(valid, _dot_nt(qm, k) + bias_ref[p + 3 * half], NEG_INF))
            tops = [jnp.maximum(jnp.max(s, axis=-1, keepdims=True), sk) for s, sk in zip(scores, sk_g)]
            probs = [jnp.exp(s - m) for s, m in zip(scores, tops)]
            dens = [jnp.sum(pr, axis=-1, keepdims=True) + jnp.exp(sk - m)
                    for pr, sk, m in zip(probs, sk_g, tops)]
            outs = [_dot(pr.astype(BF16), v) / den for pr, den in zip(probs, dens)]
            for idx in range(0, len(group), 2):
                p = group[idx][0]
                o_ref[r0:r0 + WINDOW, p * LANES:(p + 1) * LANES] = (
                    jnp.where(lo, outs[idx], outs[idx + 1]).astype(BF16))


def _swa_attention(q, kv, bias, sinks, seq):
    t = q.shape[0]
    rows = ATT_BLOCKS * WINDOW
    nblk = seq // rows
    return pl.pallas_call(
        _attn_kernel,
        grid=(t // seq, nblk),
        in_specs=[
            pl.BlockSpec(memory_space=pltpu.SMEM),
            pl.BlockSpec((rows, ATT_DIM), lambda b, n: (b * nblk + n, 0)),
            pl.BlockSpec((rows, 2 * KV_DIM), lambda b, n: (b * nblk + n, 0)),
            pl.BlockSpec((WINDOW, 2 * KV_DIM),
                         lambda b, n: (jnp.maximum((b * nblk + n) * ATT_BLOCKS - 1, 0), 0)),
            pl.BlockSpec((N_ATT_HEADS, WINDOW, 2 * WINDOW), lambda b, n: (0, 0, 0)),
        ],
        out_specs=pl.BlockSpec((rows, ATT_DIM), lambda b, n: (b * nblk + n, 0)),
        out_shape=jax.ShapeDtypeStruct((t, ATT_DIM), BF16),
        compiler_params=_cparams(("arbitrary", "arbitrary")),
        name="swa_attention",
    )(sinks, q, kv, kv, bias)


_GDN_BASE = SUBLANES
_GDN_LEVELS = int(np.log2(GDN_CHUNK // _GDN_BASE))


def _gdn_masks():
    r = np.arange(GDN_SUPER)
    ri, ci = r[:, None], r[None, :]
    same_chunk = (ri // GDN_CHUNK) == (ci // GDN_CHUNK)
    incl = same_chunk & (ri >= ci)
    planes = [incl, ri == ci]
    base = ((ri // _GDN_BASE) == (ci // _GDN_BASE)) & (ri > ci)
    planes.append(base)
    for lvl in range(_GDN_LEVELS):
        small = _GDN_BASE << lvl
        planes.append(((ri // (2 * small)) == (ci // (2 * small))) & ((ri // small) != (ci // small)) & (ri > ci))
    bmask = np.stack(planes).astype(np.float32)
    bmask[2] = -bmask[2]
    negmask = np.where(incl, 0.0, -np.inf).astype(np.float32)
    return negmask, bmask


_GDN_NEGMASK, _GDN_BMASK = _gdn_masks()


def _gdn_kernel(x_ref, halo_ref, ba_ref, cw_ref, alog_ref, dtb_ref, nw_ref, negmask_ref, bmask_ref,
                y_ref, state_ref, xs_ref):
    sc_id = pl.program_id(1)
    rows = GDN_SUPER
    nchunk = rows // GDN_CHUNK
    c_sz = GDN_CHUNK

    @pl.when(sc_id == 0)
    def _():
        state_ref[...] = jnp.zeros_like(state_ref)

    xs_ref[:SUBLANES, :] = jnp.where(sc_id == 0, 0.0, halo_ref[...])
    xs_ref[SUBLANES:, :] = x_ref[...]
    act = []
    for g in range(x_ref.shape[1] // LANES):
        cols = slice(g * LANES, (g + 1) * LANES)
        acc = x_ref[:, cols] * cw_ref[CONV_WIDTH - 1:CONV_WIDTH, cols]
        if g % 4 != 3:
            for s in range(1, CONV_WIDTH):
                acc = acc + (xs_ref[SUBLANES - s:SUBLANES - s + rows, cols]
                             * cw_ref[CONV_WIDTH - 1 - s:CONV_WIDTH - s, cols])
        act.append(acc * _sigmoid(acc))

    negmask = negmask_ref[...]
    tri_incl = bmask_ref[0]
    eye_b = bmask_ref[1]
    base_neg = bmask_ref[2]
    bands = [bmask_ref[3 + lvl] for lvl in range(_GDN_LEVELS)]
    li = lax.broadcasted_iota(jnp.int32, (LANES, LANES), 0)
    lj = lax.broadcasted_iota(jnp.int32, (LANES, LANES), 1)
    half_ones = jnp.where((li // HEAD_DIM) == (lj // HEAD_DIM), 1.0, 0.0).astype(BF16)
    lane_lo = lax.broadcasted_iota(jnp.int32, (rows, LANES), 1) < HEAD_DIM

    ba = ba_ref[...]
    beta_all = _sigmoid(ba)
    sp_in = ba + dtb_ref[...]
    softplus = jnp.maximum(sp_in, 0.0) + jnp.log(1.0 + jnp.exp(-jnp.abs(sp_in)))
    g_all = -jnp.exp(alog_ref[...]) * softplus
    gcum = _dot_hi_exact_rhs_lhs(tri_incl, g_all)
    gcum_t = gcum.T

    heads = range(N_GDN_HEADS)
    lane_hi = jnp.logical_not(lane_lo)
    mk = [lane_lo if h % 2 == 0 else lane_hi for h in heads]
    scale = HEAD_DIM ** -0.5

    xk, xq, gn, gc_col, beta, eg = [], [], [], [], [], []
    for h in heads:
        g = act[4 * (h // 2) + (h % 2)]
        g = g * lax.rsqrt(_dot((g * g).astype(BF16), half_ones) + NORM_EPS)
        gn.append(g)
        xk.append(jnp.where(mk[h], g, 0.0))
        xq.append(jnp.where(mk[h], pltpu.roll(g, HEAD_DIM, axis=1), 0.0) * scale)
        beta.append(beta_all[:, h:h + 1])
        gc_col.append(gcum[:, N_GDN_HEADS + h:N_GDN_HEADS + h + 1])
        eg.append(jnp.exp(gc_col[h]))

    l_b, attn, rhs = [], [], []
    for h in heads:
        gc_row = gcum_t[N_GDN_HEADS + h:N_GDN_HEADS + h + 1, :]
        decay = jnp.exp(gc_col[h] - gc_row + negmask)
        xk_b = xk[h].astype(BF16)
        kk = _dot_nt((xk[h] * beta[h]).astype(BF16), xk_b)
        l_b.append((kk * decay).astype(BF16))
        attn.append((_dot_nt(xq[h].astype(BF16), xk_b) * decay).astype(BF16))
        vv = act[4 * (h // 2) + 2]
        rhs.append(jnp.where(mk[h], gn[h] * eg[h], vv) * beta[h])

    a1 = [l_b[h] * base_neg for h in heads]
    a2 = [_dot(a1[h], a1[h]).astype(BF16) for h in heads]
    a4 = [_dot(a2[h], a2[h]).astype(BF16) for h in heads]
    inv0 = [eye_b + a1[h] for h in heads]
    acc1 = [inv0[h].astype(F32) + _dot(a2[h], inv0[h]) for h in heads]
    inv_b = [(acc1[h] + _dot(a4[h], acc1[h].astype(BF16))).astype(BF16) for h in heads]
    for lvl in range(_GDN_LEVELS - 1):
        mid = [_dot(l_b[h] * bands[lvl], inv_b[h]).astype(BF16) for h in heads]
        inv_b = [inv_b[h] - _dot(inv_b[h], mid[h]).astype(BF16) for h in heads]
    half = [_dot(inv_b[h], rhs[h].astype(BF16)) for h in heads]
    mid = [_dot(l_b[h] * bands[_GDN_LEVELS - 1], half[h].astype(BF16)) for h in heads]
    sol = [half[h] - _dot(inv_b[h], mid[h].astype(BF16)) for h in heads]

    lane_lo_s = lax.broadcasted_iota(jnp.int32, (LANES, LANES), 1) < HEAD_DIM
    mk_s = [lane_lo_s if h % 2 == 0 else jnp.logical_not(lane_lo_s) for h in heads]
    sol_b = [sol[h].astype(BF16) for h in heads]
    attn_sol = [_dot(attn[h], sol_b[h]) for h in heads]
    q_eff = [(xq[h] * eg[h] - jnp.where(mk[h], attn_sol[h], 0.0)).astype(BF16) for h in heads]
    o_free = [jnp.where(mk[h], 0.0, attn_sol[h]) for h in heads]
    kw = [[] for _ in heads]
    ku = [[] for _ in heads]
    cdec = [[] for _ in heads]
    for c in range(nchunk):
        r0 = c * c_sz
        for h in heads:
            glast = gcum[r0 + c_sz - 1:r0 + c_sz, N_GDN_HEADS + h:N_GDN_HEADS + h + 1]
            kd_t = (xk[h][r0:r0 + c_sz] * jnp.exp(glast - gc_col[h][r0:r0 + c_sz])).T
            both = _dot(kd_t.astype(BF16), sol_b[h][r0:r0 + c_sz])
            kw[h].append(jnp.where(mk_s[h], both, 0.0).astype(BF16))
            ku[h].append(jnp.where(mk_s[h], 0.0, both))
            cdec[h].append(jnp.exp(glast))
    st = [state_ref[h] for h in heads]
    o_parts = [[] for _ in heads]
    for c in range(nchunk):
        r0 = c * c_sz
        for h in heads:
            lhs = jnp.concatenate([kw[h][c], q_eff[h][r0:r0 + c_sz]], axis=0)
            prod = _dot(lhs, st[h].astype(BF16))
            o_parts[h].append(prod[LANES:] + o_free[h][r0:r0 + c_sz])
            st[h] = st[h] * cdec[h][c] + ku[h][c] - prod[:LANES]
    for h in heads:
        state_ref[h] = st[h]

    for p in range(N_GDN_HEADS // 2):
        o_pair = [jnp.concatenate(o_parts[h], axis=0) for h in (2 * p, 2 * p + 1)]
        o = jnp.where(lane_lo, o_pair[1], o_pair[0])
        ms = _dot((o * o).astype(BF16), half_ones) * (1.0 / HEAD_DIM)
        zz = act[4 * p + 3]
        y = o * lax.rsqrt(ms + NORM_EPS) * nw_ref[...] * zz
        y_ref[:, p * LANES:(p + 1) * LANES] = y.astype(BF16)


def _dot_hi_exact_rhs_lhs(m_bf16, x):
    hi, lo = _split_bf16(x)
    return _dot(m_bf16, hi) + _dot(m_bf16, lo)


def _gdn_mixer(gdn, ba, conv_p, alog_v, dtb_v, nw_v, seq):
    t, c = gdn.shape
    rows = GDN_SUPER
    nsc = seq // rows
    hb = rows // SUBLANES
    return pl.pallas_call(
        _gdn_kernel,
        grid=(t // seq, nsc),
        in_specs=[
            pl.BlockSpec((rows, c), lambda b, s: (b * nsc + s, 0)),
            pl.BlockSpec((SUBLANES, c), lambda b, s: (jnp.maximum((b * nsc + s) * hb - 1, 0), 0)),
            pl.BlockSpec((rows, LANES), lambda b, s: (b * nsc + s, 0)),
            pl.BlockSpec((CONV_WIDTH, c), lambda b, s: (0, 0)),
            pl.BlockSpec((1, LANES), lambda b, s: (0, 0)),
            pl.BlockSpec((1, LANES), lambda b, s: (0, 0)),
            pl.BlockSpec((1, LANES), lambda b, s: (0, 0)),
            pl.BlockSpec((rows, rows), lambda b, s: (0, 0)),
            pl.BlockSpec((3 + _GDN_LEVELS, rows, rows), lambda b, s: (0, 0, 0)),
        ],
        out_specs=pl.BlockSpec((rows, GDN_DIM), lambda b, s: (b * nsc + s, 0)),
        out_shape=jax.ShapeDtypeStruct((t, GDN_DIM), BF16),
        scratch_shapes=[pltpu.VMEM((N_GDN_HEADS, LANES, LANES), F32), pltpu.VMEM((rows + SUBLANES, c), F32)],
        compiler_params=_cparams(("arbitrary", "arbitrary")),
        name="gdn_mixer",
    )(gdn, gdn, ba, conv_p, alog_v, dtb_v, nw_v, jnp.asarray(_GDN_NEGMASK), jnp.asarray(_GDN_BMASK, BF16))


def _route_tile(logits, before, carry_ref, live):
    shape = logits.shape
    lane = lax.broadcasted_iota(jnp.int32, shape, 1).astype(F32)
    work = logits
    vals, idxs = [], []
    for _k in range(TOP_K):
        m = jnp.max(work, axis=-1, keepdims=True)
        idx = jnp.min(jnp.where(work == m, lane, float(LANES)), axis=-1, keepdims=True)
        vals.append(m)
        idxs.append(idx)
        work = jnp.where(lane == idx, -jnp.inf, work)
    exps = [jnp.exp(v - vals[0]) for v in vals]
    den = exps[0] + exps[1] + exps[2] + exps[3]
    onehots = [lane == idx for idx in idxs]
    member = jnp.zeros(shape, F32)
    for oh in onehots:
        member = member + jnp.where(oh, 1.0, 0.0)
    rank = _dot(before, member.astype(BF16)) + carry_ref[...]
    carry_ref[...] = carry_ref[...] + live * jnp.sum(member, axis=0, keepdims=True)
    info = jnp.zeros(shape, F32)
    for k in range(TOP_K):
        rank_k = jnp.sum(jnp.where(onehots[k], rank, 0.0), axis=-1, keepdims=True)
        info = jnp.where(lane == float(k), idxs[k], info)
        info = jnp.where(lane == float(TOP_K + k), rank_k, info)
        info = jnp.where(lane == float(2 * TOP_K + k), exps[k] / den, info)
    return info


def _outproj_kernel(x_ref, mod_ref, yp_ref, ya_ref, yg_ref, wp_ref, wa_ref, wg_ref, lng_ref, lnb_ref,
                    rw_ref, rb_ref, before_ref, x1_ref, h2_ref, info_ref, er_ref, cnt_ref,
                    carry_ref, logit_s):
    step = pl.program_id(0)

    @pl.when(step == 0)
    def _():
        carry_ref[...] = jnp.zeros_like(carry_ref)
        logit_s[...] = jnp.zeros_like(logit_s)

    part = x_ref.shape[0] // OUT_SPLIT
    halves = [slice(j * part, (j + 1) * part) for j in range(OUT_SPLIT)]
    y = [_dot(yp_ref[r, :], wp_ref[...]) + _dot(ya_ref[r, :], wa_ref[...]) + _dot(yg_ref[r, :], wg_ref[...])
         for r in halves]
    live = jnp.where(step > 0, 1.0, 0.0)
    info = _route_tile(logit_s[...], before_ref[...], carry_ref, live)
    info_ref[...] = info
    er_ref[...] = info.T[:SUBLANES]
    cnt_ref[...] = carry_ref[...]
    g1 = mod_ref[0, 2:3, :]
    sh2 = mod_ref[0, 3:4, :]
    sc2 = mod_ref[0, 4:5, :]
    for r, y_r in zip(halves, y):
        x1 = _layer_norm(DEEPNORM_ALPHA * x_ref[r, :] + g1 * y_r, lng_ref[...], lnb_ref[...])
        x1_ref[r, :] = x1
        h2 = x1 * (1.0 + sc2) + sh2
        h2_ref[r, :] = _pack_bf16_pairs(h2)
        logit_s[r, :] = _dot(h2.astype(BF16), rw_ref[...]) + rb_ref[...]


def _out_projection(x2d, mod, yp, ya, yg, w_out_p, ln_g, ln_b, rw, rb, seq):
    t, d = x2d.shape
    tm = ROW_TILE
    wp = w_out_p[:POOL_DIM]
    wa = w_out_p[POOL_DIM:POOL_DIM + ATT_DIM]
    wg = w_out_p[POOL_DIM + ATT_DIM:]
    last = t // tm - 1
    row = lambda i: (jnp.minimum(i, last), 0)
    routed = lambda i: (jnp.maximum(i - 1, 0), 0)
    fixed = lambda i: (0, 0)
    return pl.pallas_call(
        _outproj_kernel,
        grid=(t // tm + 1,),
        in_specs=[
            pl.BlockSpec((tm, d), row),
            pl.BlockSpec((1, 6, d), lambda i: ((jnp.minimum(i, last) * tm) // seq, 0, 0)),
            pl.BlockSpec((tm, POOL_DIM), row),
            pl.BlockSpec((tm, ATT_DIM), row),
            pl.BlockSpec((tm, GDN_DIM), row),
            pl.BlockSpec((POOL_DIM, d), fixed),
            pl.BlockSpec((ATT_DIM, d), fixed),
            pl.BlockSpec((GDN_DIM, d), fixed),
            pl.BlockSpec((1, d), fixed),
            pl.BlockSpec((1, d), fixed),
            pl.BlockSpec((d, LANES), fixed),
            pl.BlockSpec((1, LANES), fixed),
            pl.BlockSpec((tm, tm), fixed),
        ],
        out_specs=[pl.BlockSpec((tm, d), row), pl.BlockSpec((tm, d // 2), row), pl.BlockSpec((tm, LANES), routed),
                   pl.BlockSpec((SUBLANES, tm), lambda i: (0, jnp.maximum(i - 1, 0))),
                   pl.BlockSpec((1, LANES), fixed)],
        out_shape=[jax.ShapeDtypeStruct((t, d), F32), jax.ShapeDtypeStruct((t, d // 2), jnp.int32),
                   jax.ShapeDtypeStruct((t, LANES), F32), jax.ShapeDtypeStruct((SUBLANES, t), F32),
                   jax.ShapeDtypeStruct((1, LANES), F32)],
        scratch_shapes=[pltpu.VMEM((1, LANES), F32), pltpu.VMEM((tm, LANES), F32)],
        compiler_params=_cparams(("arbitrary",)),
        name="out_proj_ln_route",
    )(x2d, mod, yp, ya, yg, wp, wa, wg, ln_g.reshape(1, d), ln_b.reshape(1, d), rw, rb,
      jnp.tril(jnp.ones((tm, tm), BF16), -1))


def _slot_kernel(er_ref, cnt_ref, dest_ref, pcum_ref):
    cnt = jnp.broadcast_to(cnt_ref[...], (SUBLANES, LANES))
    padded = jnp.floor((cnt + float(EXPERT_BLOCK - 1)) * (1.0 / EXPERT_BLOCK)) * float(EXPERT_BLOCK)
    lane8 = lax.broadcasted_iota(jnp.int32, (SUBLANES, LANES), 1)
    acc = padded
    step = 1
    while step < LANES:
        acc = acc + jnp.where(lane8 >= step, pltpu.roll(acc, step, axis=1), 0.0)
        step *= 2
    pcum_ref[...] = acc[:1].astype(jnp.int32)
    pstart = acc - padded

    er = er_ref[...]
    start = jnp.zeros(er.shape, F32)
    for e in range(N_EXPERTS):
        offset = jnp.sum(jnp.where(lane8 == e, pstart, 0.0), axis=-1, keepdims=True)
        start = jnp.where(er == float(e), offset, start)
    row = lax.broadcasted_iota(jnp.int32, er.shape, 0)
    slots = jnp.where(row < TOP_K, start + pltpu.roll(er, TOP_K, axis=0), 0.0)
    dest_ref[...] = slots.astype(jnp.int32)


def _slots(er, cnt):
    t = er.shape[1]
    return pl.pallas_call(
        _slot_kernel,
        grid=(1,),
        in_specs=[pl.BlockSpec((SUBLANES, t), lambda i: (0, 0)), pl.BlockSpec((1, LANES), lambda i: (0, 0))],
        out_specs=[pl.BlockSpec((SUBLANES, t), lambda i: (0, 0)), pl.BlockSpec((1, LANES), lambda i: (0, 0))],
        out_shape=[jax.ShapeDtypeStruct((SUBLANES, t), jnp.int32), jax.ShapeDtypeStruct((1, LANES), jnp.int32)],
        compiler_params=_cparams(("arbitrary",)),
        name="moe_slots",
    )(er, cnt)


def _expert_kernel(e0, be_ref, nxt_ref, val_ref, nu_ref, x_ref, wup_hbm, bup_ref, wdn_hbm, bdn_ref, y_ref,
                   wup_st, wdn_st, wup_bf, wdn_bf, sems):
    i = pl.program_id(0)
    e = be_ref[i]
    prev = be_ref[jnp.maximum(i - 1, 0)]
    used = i < nu_ref[0]

    def weight_copies(expert):
        return (pltpu.make_async_copy(wup_hbm.at[e0 + expert], wup_st, sems.at[0]),
                pltpu.make_async_copy(wdn_hbm.at[e0 + expert], wdn_st, sems.at[1]))

    @pl.when(i == 0)
    def _():
        for cp in weight_copies(e):
            cp.start()

    @pl.when(used & ((i == 0) | (e != prev)))
    def _():
        for cp in weight_copies(e):
            cp.wait()
        wup_bf[...] = wup_st[...].astype(BF16)
        wdn_bf[...] = wdn_st[...].astype(BF16)

        @pl.when(nxt_ref[i] >= 0)
        def _():
            for cp in weight_copies(nxt_ref[i]):
                cp.start()

    def ffn(rows):
        xb = _unpack_bf16_pairs(x_ref[:rows, :]).astype(BF16)
        hb = _dot(xb, wup_bf[...]) + bup_ref[0]
        x_glu = jnp.minimum(hb[:, :EXPERT_DIM], SWIGLU_LIMIT)
        x_lin = jnp.clip(hb[:, EXPERT_DIM:], -SWIGLU_LIMIT, SWIGLU_LIMIT)
        act = x_glu * _sigmoid(SWIGLU_ALPHA * x_glu) * (x_lin + 1.0)
        y = _dot(act.astype(BF16), wdn_bf[...]) + bdn_ref[0]
        y_ref[:rows, :] = _pack_bf16_pairs(y)

    half_rows = x_ref.shape[0] // 2
    small = val_ref[i] <= half_rows

    @pl.when(used & jnp.logical_not(small))
    def _():
        ffn(x_ref.shape[0])

    @pl.when(used & small)
    def _():
        ffn(half_rows)
        y_ref[half_rows:, :] = jnp.zeros((x_ref.shape[0] - half_rows, y_ref.shape[1]), y_ref.dtype)

    @pl.when(i >= nu_ref[0])
    def _():
        y_ref[...] = jnp.zeros_like(y_ref)


def _expert_ffn(xbuf, block_e, next_e, valid, n_used, w_up, b_up, w_down, b_down, layer):
    p, dh = xbuf.shape
    d = 2 * dh
    bm = EXPERT_BLOCK
    ne, _, n_up = w_up.shape
    e0 = layer * N_EXPERTS
    grid_spec = pltpu.PrefetchScalarGridSpec(
        num_scalar_prefetch=4,
        grid=(p // bm,),
        in_specs=[
            pl.BlockSpec((bm, dh), lambda i, be, nx, vl, nu: (i, 0)),
            pl.BlockSpec(memory_space=pl.ANY),
            pl.BlockSpec((1, 1, n_up), lambda i, be, nx, vl, nu: (e0 + be[i], 0, 0)),
            pl.BlockSpec(memory_space=pl.ANY),
            pl.BlockSpec((1, 1, d), lambda i, be, nx, vl, nu: (e0 + be[i], 0, 0)),
        ],
        out_specs=pl.BlockSpec((bm, dh), lambda i, be, nx, vl, nu: (i, 0)),
        scratch_shapes=[pltpu.VMEM((d, n_up), F32), pltpu.VMEM((EXPERT_DIM, d), F32),
                        pltpu.VMEM((d, n_up), BF16), pltpu.VMEM((EXPERT_DIM, d), BF16),
                        pltpu.SemaphoreType.DMA((2,))],
    )
    return pl.pallas_call(
        functools.partial(_expert_kernel, e0),
        grid_spec=grid_spec,
        out_shape=jax.ShapeDtypeStruct((p, dh), jnp.int32),
        compiler_params=_cparams(("arbitrary",)),
        name="expert_ffn",
    )(block_e, next_e, valid, n_used, xbuf, w_up, b_up, w_down, b_down)


def _combine_kernel(x1_ref, mod_ref, yg_ref, info_ref, lng_ref, lnb_ref, o_ref):
    info = info_ref[...]
    y = jnp.zeros(x1_ref.shape, F32)
    for k in range(TOP_K):
        gate = info[:, 2 * TOP_K + k:2 * TOP_K + k + 1]
        y = y + gate * _unpack_bf16_pairs(yg_ref[k])
    g2 = mod_ref[0, 5:6, :]
    o_ref[...] = _layer_norm(DEEPNORM_ALPHA * x1_ref[...] + g2 * y, lng_ref[...], lnb_ref[...])


def _combine(x1, mod, yg, info, ln_g, ln_b, seq):
    t, d = x1.shape
    tm = min(IN_TILE, seq)
    row = lambda i: (i, 0)
    fixed = lambda i: (0, 0)
    return pl.pallas_call(
        _combine_kernel,
        grid=(t // tm,),
        in_specs=[
            pl.BlockSpec((tm, d), row),
            pl.BlockSpec((1, 6, d), lambda i: ((i * tm) // seq, 0, 0)),
            pl.BlockSpec((TOP_K, tm, d // 2), lambda i: (0, i, 0)),
            pl.BlockSpec((tm, LANES), row),
            pl.BlockSpec((1, d), fixed),
            pl.BlockSpec((1, d), fixed),
        ],
        out_specs=pl.BlockSpec((tm, d), row),
        out_shape=jax.ShapeDtypeStruct((t, d), F32),
        compiler_params=_cparams(("arbitrary",)),
        name="moe_combine_ln",
    )(x1, mod, yg, info, ln_g.reshape(1, d), ln_b.reshape(1, d))


def _sc_workers():
    info = plsc.get_sparse_core_info()
    return info.num_cores, info.num_cores * info.num_subcores


def _sc_scatter_rows(rows, idx, n_out):
    t, w = rows.shape
    kk = idx.shape[0]
    n_cores, n_workers = _sc_workers()
    ch = SC_CHUNK
    assert t % (2 * n_workers * ch) == 0
    n_chunk = t // (n_workers * ch)
    idx_c = jnp.transpose(idx.reshape(kk, t // ch, ch), (1, 0, 2))

    @functools.partial(
        pl.kernel,
        mesh=plsc.VectorSubcoreMesh(core_axis_name="c", subcore_axis_name="s"),
        out_type=jax.ShapeDtypeStruct((n_out, w), rows.dtype),
        scratch_types=[pltpu.VMEM((2, kk, ch), jnp.int32), pltpu.VMEM((2, ch, w), rows.dtype),
                       pltpu.SemaphoreType.DMA((2,)), pltpu.SemaphoreType.DMA((2,))],
        name="sc_dispatch_scatter",
    )
    def scatter_kernel(rows_hbm, idx_hbm, out_hbm, idx_v, rows_v, load_sem, scat_sem):
        base = (lax.axis_index("s") * n_cores + lax.axis_index("c")) * n_chunk

        def load(j, b):
            return pltpu.make_async_copy(rows_hbm.at[pl.ds((base + j) * ch, ch)], rows_v.at[b], load_sem.at[b])

        def scatters(b):
            return [pltpu.make_async_copy(rows_v.at[b], out_hbm.at[idx_v.at[b, q]], scat_sem.at[b])
                    for q in range(kk)]

        pltpu.sync_copy(idx_hbm.at[base], idx_v.at[0])
        load(0, 0).start()

        @pl.loop(0, n_chunk, step=2)
        def _(j0):
            for b in range(2):
                j = j0 + b
                other = 1 - b

                @pl.when(j >= 1)
                def _():
                    for cp in scatters(other):
                        cp.wait()

                @pl.when(j + 1 < n_chunk)
                def _():
                    pltpu.sync_copy(idx_hbm.at[base + j + 1], idx_v.at[other])
                    load(j + 1, other).start()

                load(j, b).wait()
                for cp in scatters(b):
                    cp.start()

        for cp in scatters((n_chunk - 1) % 2):
            cp.wait()

    return scatter_kernel(rows, idx_c)


def _sc_gather_rows(table, idx):
    m = idx.shape[0]
    w = table.shape[1]
    n_cores, n_workers = _sc_workers()
    ch = SC_CHUNK
    assert m % (2 * n_workers * ch) == 0
    n_chunk = m // (n_workers * ch)
    idx_c = idx.reshape(m // ch, 1, ch)

    @functools.partial(
        pl.kernel,
        mesh=plsc.VectorSubcoreMesh(core_axis_name="c", subcore_axis_name="s"),
        out_type=jax.ShapeDtypeStruct((m, w), table.dtype),
        scratch_types=[pltpu.VMEM((2, 1, ch), jnp.int32), pltpu.VMEM((2, ch, w), table.dtype),
                       pltpu.SemaphoreType.DMA((2,)), pltpu.SemaphoreType.DMA((2,))],
        name="sc_combine_gather",
    )
    def gather_kernel(table_hbm, idx_hbm, out_hbm, idx_v, rows_v, gather_sem, write_sem):
        base = (lax.axis_index("s") * n_cores + lax.axis_index("c")) * n_chunk

        def gather(b):
            return pltpu.make_async_copy(table_hbm.at[idx_v.at[b, 0]], rows_v.at[b], gather_sem.at[b])

        def write(j, b):
            return pltpu.make_async_copy(rows_v.at[b], out_hbm.at[pl.ds((base + j) * ch, ch)], write_sem.at[b])

        pltpu.sync_copy(idx_hbm.at[base], idx_v.at[0])
        gather(0).start()

        @pl.loop(0, n_chunk, step=2)
        def _(j0):
            for b in range(2):
                j = j0 + b
                other = 1 - b

                @pl.when(j >= 1)
                def _():
                    write(j - 1, other).wait()

                @pl.when(j + 1 < n_chunk)
                def _():
                    pltpu.sync_copy(idx_hbm.at[base + j + 1], idx_v.at[other])
                    gather(other).start()

                gather(b).wait()
                write(j, b).start()

        write(n_chunk - 1, (n_chunk - 1) % 2).wait()

    return gather_kernel(table, idx_c)


def _lane_vector(vals, offset):
    return jnp.zeros((1, LANES), F32).at[0, offset:offset + vals.shape[0]].set(vals.astype(F32))


def _moe(h2, info, er, cnt, x1, mod, ln_g, ln_b, w_up, b_up, w_down, b_down, layer, seq):
    t, dh = h2.shape
    a = t * TOP_K
    bm = EXPERT_BLOCK
    slots, pcum_v = _slots(er, cnt)
    pcum = pcum_v[0, :N_EXPERTS]
    dest = slots[:TOP_K]
    n_blocks = -(-a // bm) + N_EXPERTS
    starts = jnp.arange(n_blocks, dtype=jnp.int32) * bm
    block_e = jnp.minimum(jnp.sum(pcum[None, :] <= starts[:, None], axis=1), N_EXPERTS - 1).astype(jnp.int32)
    n_used = (pcum[-1] // bm).astype(jnp.int32).reshape(1)
    later = block_e[None, :] > block_e[:, None]
    group_end = n_blocks - jnp.sum(later, axis=1)
    next_e = jnp.min(jnp.where(later, block_e[None, :], N_EXPERTS), axis=1)
    next_e = jnp.where(group_end < n_used[0], next_e, -1).astype(jnp.int32)
    counts = cnt[0, :N_EXPERTS].astype(jnp.int32)
    pstart = pcum - ((counts + bm - 1) // bm) * bm
    mine = block_e[:, None] == jnp.arange(N_EXPERTS, dtype=jnp.int32)[None, :]
    count_b = jnp.sum(jnp.where(mine, counts[None, :], 0), axis=1)
    pstart_b = jnp.sum(jnp.where(mine, pstart[None, :], 0), axis=1)
    valid = jnp.clip(count_b - (starts - pstart_b), 0, bm).astype(jnp.int32)
    xbuf = _sc_scatter_rows(h2, dest, n_blocks * bm)
    ybuf = _expert_ffn(xbuf, block_e, next_e, valid, n_used, w_up, b_up, w_down, b_down, layer)
    yg = _sc_gather_rows(ybuf, dest.reshape(a)).reshape(TOP_K, t, dh)
    return _combine(x1, mod, yg, info, ln_g, ln_b, seq)


def kernel(x, c, rel_bias, w_in, w_out, w_ada, b_ada, ln1_g, ln1_b, ln2_g, ln2_b, pool_w, pool_scale,
           attn_sinks, conv_w, gdn_a_log, gdn_dt_bias, gdn_norm_w, router_w, router_b,
           exp_w_up, exp_b_up, exp_w_down, exp_b_down):
    bsz, seq, d = x.shape
    depth = w_in.shape[0]
    t = bsz * seq
    assert d == D_MODEL and w_in.shape[2] == IN_DIM
    assert seq % GDN_SUPER == 0 and seq % (ATT_BLOCKS * WINDOW) == 0
    assert t % ROW_TILE == 0 and seq % ROW_TILE == 0

    mod_all = _modulation(c, w_ada, b_ada).reshape(depth, bsz, 6, d)
    bias = _band_bias(rel_bias)

    w_up_all = exp_w_up.reshape((depth * N_EXPERTS,) + exp_w_up.shape[2:])
    b_up_all = exp_b_up.reshape(depth * N_EXPERTS, 1, exp_b_up.shape[2])
    w_down_all = exp_w_down.reshape((depth * N_EXPERTS,) + exp_w_down.shape[2:])
    b_down_all = exp_b_down.reshape(depth * N_EXPERTS, 1, exp_b_down.shape[2])

    x2d = x.reshape(t, d)
    for l in range(depth):
        mod = mod_all[l]
        w_out_p = _take_static(w_out[l], _OUT_PERM, 0).astype(BF16)
        ident = jnp.zeros((CONV_WIDTH, 1), F32).at[CONV_WIDTH - 1, 0].set(1.0)
        conv_p = jnp.where(jnp.asarray(_GDN_CONV_SRC >= 0), _take_cols(conv_w[l].astype(F32), _GDN_CONV_SRC),
                           ident)
        pool_bd = jnp.zeros((POOL_DIM, POOL_DIM), F32)
        for gi in range(len(POOL_WINDOWS)):
            sl = slice(gi * POOL_GROUP, (gi + 1) * POOL_GROUP)
            pool_bd = pool_bd.at[sl, sl].set(pool_w[l, gi].astype(F32))
        alog_v = _lane_vector(gdn_a_log[l], N_GDN_HEADS)
        dtb_v = _lane_vector(gdn_dt_bias[l], N_GDN_HEADS)
        nw_v = jnp.tile(gdn_norm_w[l].astype(F32), 2).reshape(1, LANES)
        rw = jnp.zeros((d, LANES), BF16).at[:, :N_EXPERTS].set(router_w[l].astype(BF16))
        rb = jnp.full((1, LANES), NEG_INF, F32).at[0, :N_EXPERTS].set(router_b[l].astype(F32))

        u_pool, aq, akv, gdn, ba = _in_projection(x2d, mod, w_in.astype(F32), l, seq)
        y_pool = _pool_mixer(u_pool, pool_bd.astype(BF16), pool_scale[l].astype(F32), seq)
        y_att = _swa_attention(aq, akv, bias, attn_sinks[l].astype(F32), seq)
        y_gdn = _gdn_mixer(gdn, ba, conv_p, alog_v, dtb_v, nw_v, seq)
        x1, h2, info, er, cnt = _out_projection(x2d, mod, y_pool, y_att, y_gdn, w_out_p, ln1_g[l], ln1_b[l],
                                            rw, rb, seq)
        x2d = _moe(h2, info, er, cnt, x1, mod, ln2_g[l], ln2_b[l], w_up_all, b_up_all, w_down_all, b_down_all,
                   l, seq)
    return x2d.reshape(bsz, seq, d)
```

```python
import functools

import numpy as np
import jax
import jax.numpy as jnp
from jax import lax
from jax.experimental import pallas as pl
from jax.experimental.pallas import tpu as pltpu
from jax.experimental.pallas import tpu_sc as plsc

F32 = jnp.float32
BF16 = jnp.bfloat16

D_MODEL = 1024
HEAD_DIM = 64
POOL_DIM = 256
POOL_WINDOWS = (2, 4, 8, 16)
POOL_GROUP = 64
N_ATT_HEADS = 6
N_KV_HEADS = 2
ATT_DIM = 384
KV_DIM = 128
WINDOW = 128
N_BUCKETS = 32
MAX_DISTANCE = 128
N_GDN_HEADS = 6
GDN_DIM = 384
CONV_WIDTH = 4
GDN_CHUNK = 64
N_EXPERTS = 32
TOP_K = 4
EXPERT_DIM = 1024
SWIGLU_ALPHA = 1.702
SWIGLU_LIMIT = 7.0
DEPTH = 2
DEEPNORM_ALPHA = (2 * DEPTH) ** 0.25
LN_EPS = 1e-5
NORM_EPS = 1e-6
NEG_INF = -1e30

LANES = 128
SUBLANES = 8
VMEM_LIMIT = 56 * 1024 * 1024

MOD_TILE = 2048
ROW_TILE = 512
OUT_SPLIT = 2
IN_TILE = 1024
ATT_BLOCKS = 8
ATT_GROUP = 6
GDN_SUPER = 256
EXPERT_BLOCK = 512
SC_CHUNK = 64

_OFF_AQ = POOL_DIM
_OFF_AK = _OFF_AQ + ATT_DIM
_OFF_AV = _OFF_AK + KV_DIM
_OFF_GQ = _OFF_AV + KV_DIM
_OFF_GK = _OFF_GQ + GDN_DIM
_OFF_GV = _OFF_GK + GDN_DIM
_OFF_GZ = _OFF_GV + GDN_DIM
_OFF_GB = _OFF_GZ + GDN_DIM
_OFF_GA = _OFF_GB + N_GDN_HEADS
IN_DIM = _OFF_GA + N_GDN_HEADS

P_POOL = (0, POOL_DIM)
P_Q = (P_POOL[1], P_POOL[1] + ATT_DIM)
P_KV = (P_Q[1], P_Q[1] + 2 * KV_DIM)
P_GDN = (P_KV[1], P_KV[1] + 4 * GDN_DIM)
P_BA = (P_GDN[1], P_GDN[1] + LANES)
P_TOTAL = P_BA[1]


def _head_cols(off, h):
    return list(range(off + HEAD_DIM * h, off + HEAD_DIM * (h + 1)))


def _build_in_perm():
    cols = list(range(POOL_DIM))
    for p in range(N_ATT_HEADS // 2):
        cols += _head_cols(_OFF_AQ, p) + _head_cols(_OFF_AQ, p + 3)
    cols += list(range(_OFF_AK, _OFF_AK + 2 * KV_DIM))
    gdn_src = []
    for p in range(N_GDN_HEADS // 2):
        e, o = 2 * p, 2 * p + 1
        grp = (_head_cols(_OFF_GK, e) + _head_cols(_OFF_GQ, e)
               + _head_cols(_OFF_GQ, o) + _head_cols(_OFF_GK, o)
               + _head_cols(_OFF_GV, o) + _head_cols(_OFF_GV, e)
               + _head_cols(_OFF_GZ, o) + _head_cols(_OFF_GZ, e))
        cols += grp
        gdn_src += [c - _OFF_GQ if c < _OFF_GZ else -1 for c in grp]
    cols += list(range(_OFF_GB, _OFF_GB + 2 * N_GDN_HEADS))
    cols += [-1] * (LANES - 2 * N_GDN_HEADS)
    assert len(cols) == P_TOTAL
    return np.asarray(cols, np.int32), np.asarray(gdn_src, np.int32)


_IN_PERM, _GDN_CONV_SRC = _build_in_perm()


def _build_out_perm():
    rows = list(range(POOL_DIM))
    for p in range(N_ATT_HEADS // 2):
        rows += _head_cols(POOL_DIM, p) + _head_cols(POOL_DIM, p + 3)
    for p in range(N_GDN_HEADS // 2):
        rows += _head_cols(POOL_DIM + ATT_DIM, 2 * p + 1) + _head_cols(POOL_DIM + ATT_DIM, 2 * p)
    return np.asarray(rows, np.int32)


_OUT_PERM = _build_out_perm()


def _t5_bucket_line():
    n = np.maximum(2 * WINDOW - 1 - np.arange(3 * WINDOW - 1), 0)
    max_exact = N_BUCKETS // 2
    nf = np.maximum(n, 1).astype(np.float32)
    large = max_exact + (np.log(nf / max_exact) / np.float32(np.log(MAX_DISTANCE / max_exact))
                         * (N_BUCKETS - max_exact)).astype(np.int32)
    large = np.minimum(large, N_BUCKETS - 1)
    return np.where(n < max_exact, n, large).astype(np.int32)


_BUCKET_LINE = _t5_bucket_line()


def _band_bias(rel_bias):
    n_line = 3 * WINDOW - 1
    line = jnp.take(rel_bias.astype(F32), jnp.asarray(_BUCKET_LINE), axis=0).T
    heads = line.shape[0]
    padded = jnp.concatenate([line, jnp.zeros((heads, 1), F32)], axis=1)
    skew = jnp.tile(padded, (1, WINDOW))[:, :WINDOW * n_line].reshape(heads, WINDOW, n_line)
    return skew[:, :, WINDOW - 1:3 * WINDOW - 1]


def _take_static(w, perm, axis):
    parts = []
    start = 0
    for i in range(1, len(perm) + 1):
        run_ends = (i == len(perm) or ((perm[i] < 0) != (perm[i - 1] < 0))
                    or (perm[i] >= 0 and perm[i] != perm[i - 1] + 1))
        if run_ends:
            if perm[start] < 0:
                shape = list(w.shape)
                shape[axis] = i - start
                parts.append(jnp.zeros(shape, w.dtype))
            else:
                parts.append(lax.slice_in_dim(w, int(perm[start]), int(perm[start]) + (i - start), axis=axis))
            start = i
    return jnp.concatenate(parts, axis=axis)


def _take_cols(w, perm):
    return _take_static(w, perm, w.ndim - 1)


def _split_bf16(x):
    hi = x.astype(BF16)
    lo = (x - hi.astype(F32)).astype(BF16)
    return hi, lo


def _pack_bf16_pairs(x):
    n = x.shape[1] // 2
    bits = pltpu.bitcast(x.astype(BF16).astype(F32), jnp.int32)
    return lax.shift_right_logical(bits[:, :n], 16) | bits[:, n:]


def _unpack_bf16_pairs(u):
    lo = pltpu.bitcast(lax.shift_left(u, 16), F32)
    hi = pltpu.bitcast(u & jnp.int32(-65536), F32)
    return jnp.concatenate([lo, hi], axis=1)


def _dot(a, b):
    return jnp.dot(a, b, preferred_element_type=F32)


def _dot_nt(a, b):
    return lax.dot_general(a, b, (((1,), (1,)), ((), ())), preferred_element_type=F32)


def _sigmoid(x):
    return 1.0 / (1.0 + jnp.exp(-x))


def _layer_norm(r, g, b):
    mu = jnp.mean(r, axis=-1, keepdims=True)
    d = r - mu
    var = jnp.mean(d * d, axis=-1, keepdims=True)
    return d * lax.rsqrt(var + LN_EPS) * g + b


def _cparams(sem):
    return pltpu.CompilerParams(dimension_semantics=sem, vmem_limit_bytes=VMEM_LIMIT)


def _mod_kernel(c_ref, w_ref, b_ref, o_ref):
    c = c_ref[...]
    ca = c * _sigmoid(c)
    o_ref[0] = _dot(ca.astype(BF16), w_ref[0].astype(BF16)) + b_ref[0]


def _modulation(c, w_ada, b_ada):
    depth, d, n = w_ada.shape
    bsz = c.shape[0]
    tn = MOD_TILE
    return pl.pallas_call(
        _mod_kernel,
        grid=(depth, n // tn),
        in_specs=[
            pl.BlockSpec((bsz, d), lambda l, j: (0, 0)),
            pl.BlockSpec((1, d, tn), lambda l, j: (l, 0, j)),
            pl.BlockSpec((1, 1, tn), lambda l, j: (l, 0, j)),
        ],
        out_specs=pl.BlockSpec((1, bsz, tn), lambda l, j: (l, 0, j)),
        out_shape=jax.ShapeDtypeStruct((depth, bsz, n), F32),
        compiler_params=_cparams(("arbitrary", "arbitrary")),
        name="adaln_mod",
    )(c, w_ada, b_ada.reshape(depth, 1, n))


def _perm_runs(perm):
    runs = []
    start = 0
    for i in range(1, len(perm) + 1):
        run_ends = (i == len(perm) or ((perm[i] < 0) != (perm[i - 1] < 0))
                    or (perm[i] >= 0 and perm[i] != perm[i - 1] + 1))
        if run_ends:
            runs.append((int(perm[start]) if perm[start] >= 0 else -1, i - start, start))
            start = i
    return runs


_IN_RUNS = _perm_runs(_IN_PERM)


def _inproj_kernel(layer, x_ref, mod_ref, wt_hbm, pool_ref, q_ref, kv_ref, gdn_ref, ba_ref,
                   wt_f32, tail_ref, wt_ref, sem):
    @pl.when(pl.program_id(0) == 0)
    def _():
        n_real = wt_hbm.shape[0]
        pad0 = (n_real // SUBLANES) * SUBLANES
        bulk = pltpu.make_async_copy(wt_hbm.at[pl.ds(0, pad0), layer, :], wt_f32.at[pl.ds(0, pad0)], sem.at[0])
        tail = pltpu.make_async_copy(wt_hbm.at[pl.ds(n_real - SUBLANES, SUBLANES), layer, :], tail_ref, sem.at[1])
        bulk.start()
        tail.start()
        bulk.wait()
        tail.wait()
        row8 = lax.broadcasted_iota(jnp.int32, tail_ref.shape, 0)
        left = n_real - pad0
        wt_f32[pad0:pad0 + SUBLANES, :] = jnp.where(row8 < left, pltpu.roll(tail_ref[...], left, axis=0), 0.0)
        wt_f32[pad0 + SUBLANES:, :] = jnp.zeros((wt_f32.shape[0] - pad0 - SUBLANES, wt_f32.shape[1]), F32)
        for src, n, dst in _IN_RUNS:
            if src >= 0:
                rows = n if src + n < n_real else wt_ref.shape[0] - dst
                wt_ref[dst:dst + rows, :] = wt_f32[src:src + rows, :].astype(BF16)
        q_rows = wt_ref[P_Q[0]:P_Q[1], :].astype(F32) * (HEAD_DIM ** -0.5)
        wt_ref[P_Q[0]:P_Q[1], :] = q_rows.astype(BF16)

    sh = mod_ref[0, 0:1, :]
    sc = mod_ref[0, 1:2, :]
    h = (x_ref[...] * (1.0 + sc) + sh).astype(BF16)

    def mm(rng):
        return _dot_nt(h, wt_ref[rng[0]:rng[1], :])

    pool_ref[...] = mm(P_POOL)
    q_ref[...] = mm(P_Q).astype(BF16)
    kv_ref[...] = mm(P_KV).astype(BF16)
    gdn_ref[...] = mm(P_GDN)
    ba_ref[...] = mm(P_BA)


def _in_projection(x2d, mod, w_in, layer, seq):
    t, d = x2d.shape
    tm = min(IN_TILE, seq)
    widths = [r[1] - r[0] for r in (P_POOL, P_Q, P_KV, P_GDN, P_BA)]
    dtypes = [F32, BF16, BF16, F32, F32]
    return pl.pallas_call(
        functools.partial(_inproj_kernel, layer),
        grid=(t // tm,),
        in_specs=[
            pl.BlockSpec((tm, d), lambda i: (i, 0)),
            pl.BlockSpec((1, 6, d), lambda i: ((i * tm) // seq, 0, 0)),
            pl.BlockSpec(memory_space=pl.ANY),
        ],
        out_specs=[pl.BlockSpec((tm, w), lambda i: (i, 0)) for w in widths],
        out_shape=[jax.ShapeDtypeStruct((t, w), dt) for w, dt in zip(widths, dtypes)],
        scratch_shapes=[pltpu.VMEM((P_TOTAL, d), F32), pltpu.VMEM((SUBLANES, d), F32),
                        pltpu.VMEM((P_TOTAL, d), BF16), pltpu.SemaphoreType.DMA((2,))],
        compiler_params=_cparams(("arbitrary",)),
        name="in_proj",
    )(x2d, mod, jnp.transpose(w_in, (2, 0, 1)))


def _pool_kernel(u_ref, w_ref, scale_ref, o_ref):
    u = u_ref[...]
    row = lax.broadcasted_iota(jnp.int32, u.shape, 0)
    lane = lax.broadcasted_iota(jnp.int32, u.shape, 1)

    def shifted(a, s):
        return jnp.where(row >= s, pltpu.roll(a, s, axis=0), 0.0)

    sums = []
    acc = u
    for wdt in POOL_WINDOWS:
        acc = acc + shifted(acc, wdt // 2)
        sums.append(acc)
    grp = lane // POOL_GROUP
    wsum = sums[-1]
    win = jnp.full(u.shape, POOL_WINDOWS[-1], jnp.int32)
    for gi in range(len(POOL_WINDOWS) - 2, -1, -1):
        wsum = jnp.where(grp == gi, sums[gi], wsum)
        win = jnp.where(grp == gi, POOL_WINDOWS[gi], win)
    cnt = jnp.minimum(row + 1, win).astype(F32)
    p = wsum / cnt - u
    y = _dot(p.astype(BF16), w_ref[...]) * scale_ref[...]
    o_ref[...] = y.astype(BF16)


def _pool_mixer(u, pool_w_bd, pool_scale, seq):
    t, c = u.shape
    return pl.pallas_call(
        _pool_kernel,
        grid=(t // seq,),
        in_specs=[
            pl.BlockSpec((seq, c), lambda b: (b, 0)),
            pl.BlockSpec((c, c), lambda b: (0, 0)),
            pl.BlockSpec((1, c), lambda b: (0, 0)),
        ],
        out_specs=pl.BlockSpec((seq, c), lambda b: (b, 0)),
        out_shape=jax.ShapeDtypeStruct((t, c), BF16),
        compiler_params=_cparams(("arbitrary",)),
        name="pool_mixer",
    )(u, pool_w_bd, pool_scale.reshape(1, c))


def _attn_kernel(sink_ref, q_ref, kvc_ref, kvp_ref, bias_ref, o_ref):
    step = pl.program_id(1)
    qi = lax.broadcasted_iota(jnp.int32, (WINDOW, 2 * WINDOW), 0)
    kj = lax.broadcasted_iota(jnp.int32, (WINDOW, 2 * WINDOW), 1)
    dist = qi + WINDOW - kj
    in_band = (dist >= 0) & (dist < WINDOW)
    lo = lax.broadcasted_iota(jnp.int32, (WINDOW, LANES), 1) < HEAD_DIM
    heads = [(p, half) for p in range(N_ATT_HEADS // 2) for half in range(2)]
    sinks = [sink_ref[p + 3 * half] for p, half in heads]
    for sub in range(ATT_BLOCKS):
        r0 = sub * WINDOW
        prev = kvp_ref[...] if sub == 0 else kvc_ref[r0 - WINDOW:r0, :]
        kv = jnp.concatenate([prev, kvc_ref[r0:r0 + WINDOW, :]], axis=0)
        k = kv[:, :KV_DIM]
        v = kv[:, KV_DIM:]
        valid = in_band & ((kj >= WINDOW) | (step > 0)) if sub == 0 else in_band
        for g0 in range(0, len(heads), ATT_GROUP):
            group = heads[g0:g0 + ATT_GROUP]
            sk_g = sinks[g0:g0 + ATT_GROUP]
            scores = []
            for p, half in group:
                qp = q_ref[r0:r0 + WINDOW, p * LANES:(p + 1) * LANES]
                qm = jnp.where(lo if half == 0 else jnp.logical_not(lo), qp, jnp.zeros_like(qp))
                scores.append(jnp.where(valid, _dot_nt(qm, k) + bias_ref[p + 3 * half], NEG_INF))
            tops = [jnp.maximum(jnp.max(s, axis=-1, keepdims=True), sk) for s, sk in zip(scores, sk_g)]
            probs = [jnp.exp(s - m) for s, m in zip(scores, tops)]
            dens = [jnp.sum(pr, axis=-1, keepdims=True) + jnp.exp(sk - m)
                    for pr, sk, m in zip(probs, sk_g, tops)]
            outs = [_dot(pr.astype(BF16), v) / den for pr, den in zip(probs, dens)]
            for idx in range(0, len(group), 2):
                p = group[idx][0]
                o_ref[r0:r0 + WINDOW, p * LANES:(p + 1) * LANES] = (
                    jnp.where(lo, outs[idx], outs[idx + 1]).astype(BF16))


def _swa_attention(q, kv, bias, sinks, seq):
    t = q.shape[0]
    rows = ATT_BLOCKS * WINDOW
    nblk = seq // rows
    return pl.pallas_call(
        _attn_kernel,
        grid=(t // seq, nblk),
        in_specs=[
            pl.BlockSpec(memory_space=pltpu.SMEM),
            pl.BlockSpec((rows, ATT_DIM), lambda b, n: (b * nblk + n, 0)),
            pl.BlockSpec((rows, 2 * KV_DIM), lambda b, n: (b * nblk + n, 0)),
            pl.BlockSpec((WINDOW, 2 * KV_DIM),
                         lambda b, n: (jnp.maximum((b * nblk + n) * ATT_BLOCKS - 1, 0), 0)),
            pl.BlockSpec((N_ATT_HEADS, WINDOW, 2 * WINDOW), lambda b, n: (0, 0, 0)),
        ],
        out_specs=pl.BlockSpec((rows, ATT_DIM), lambda b, n: (b * nblk + n, 0)),
        out_shape=jax.ShapeDtypeStruct((t, ATT_DIM), BF16),
        compiler_params=_cparams(("arbitrary", "arbitrary")),
        name="swa_attention",
    )(sinks, q, kv, kv, bias)


_GDN_BASE = SUBLANES
_GDN_LEVELS = int(np.log2(GDN_CHUNK // _GDN_BASE))


def _gdn_masks():
    r = np.arange(GDN_SUPER)
    ri, ci = r[:, None], r[None, :]
    same_chunk = (ri // GDN_CHUNK) == (ci // GDN_CHUNK)
    incl = same_chunk & (ri >= ci)
    planes = [incl, ri == ci]
    base = ((ri // _GDN_BASE) == (ci // _GDN_BASE)) & (ri > ci)
    planes.append(base)
    for lvl in range(_GDN_LEVELS):
        small = _GDN_BASE << lvl
        planes.append(((ri // (2 * small)) == (ci // (2 * small))) & ((ri // small) != (ci // small)) & (ri > ci))
    bmask = np.stack(planes).astype(np.float32)
    bmask[2] = -bmask[2]
    negmask = np.where(incl, 0.0, -np.inf).astype(np.float32)
    return negmask, bmask


_GDN_NEGMASK, _GDN_BMASK = _gdn_masks()


def _gdn_kernel(x_ref, halo_ref, ba_ref, cw_ref, alog_ref, dtb_ref, nw_ref, negmask_ref, bmask_ref,
                y_ref, state_ref, xs_ref):
    sc_id = pl.program_id(1)
    rows = GDN_SUPER
    nchunk = rows // GDN_CHUNK
    c_sz = GDN_CHUNK

    @pl.when(sc_id == 0)
    def _():
        state_ref[...] = jnp.zeros_like(state_ref)

    xs_ref[:SUBLANES, :] = jnp.where(sc_id == 0, 0.0, halo_ref[...])
    xs_ref[SUBLANES:, :] = x_ref[...]
    act = []
    for g in range(x_ref.shape[1] // LANES):
        cols = slice(g * LANES, (g + 1) * LANES)
        acc = x_ref[:, cols] * cw_ref[CONV_WIDTH - 1:CONV_WIDTH, cols]
        if g % 4 != 3:
            for s in range(1, CONV_WIDTH):
                acc = acc + (xs_ref[SUBLANES - s:SUBLANES - s + rows, cols]
                             * cw_ref[CONV_WIDTH - 1 - s:CONV_WIDTH - s, cols])
        act.append(acc * _sigmoid(acc))

    negmask = negmask_ref[...]
    tri_incl = bmask_ref[0]
    eye_b = bmask_ref[1]
    base_neg = bmask_ref[2]
    bands = [bmask_ref[3 + lvl] for lvl in range(_GDN_LEVELS)]
    li = lax.broadcasted_iota(jnp.int32, (LANES, LANES), 0)
    lj = lax.broadcasted_iota(jnp.int32, (LANES, LANES), 1)
    half_ones = jnp.where((li // HEAD_DIM) == (lj // HEAD_DIM), 1.0, 0.0).astype(BF16)
    lane_lo = lax.broadcasted_iota(jnp.int32, (rows, LANES), 1) < HEAD_DIM

    ba = ba_ref[...]
    beta_all = _sigmoid(ba)
    sp_in = ba + dtb_ref[...]
    softplus = jnp.maximum(sp_in, 0.0) + jnp.log(1.0 + jnp.exp(-jnp.abs(sp_in)))
    g_all = -jnp.exp(alog_ref[...]) * softplus
    gcum = _dot_hi_exact_rhs_lhs(tri_incl, g_all)
    gcum_t = gcum.T

    heads = range(N_GDN_HEADS)
    lane_hi = jnp.logical_not(lane_lo)
    mk = [lane_lo if h % 2 == 0 else lane_hi for h in heads]
    scale = HEAD_DIM ** -0.5

    xk, xq, gn, gc_col, beta, eg = [], [], [], [], [], []
    for h in heads:
        g = act[4 * (h // 2) + (h % 2)]
        g = g * lax.rsqrt(_dot((g * g).astype(BF16), half_ones) + NORM_EPS)
        gn.append(g)
        xk.append(jnp.where(mk[h], g, 0.0))
        xq.append(jnp.where(mk[h], pltpu.roll(g, HEAD_DIM, axis=1), 0.0) * scale)
        beta.append(beta_all[:, h:h + 1])
        gc_col.append(gcum[:, N_GDN_HEADS + h:N_GDN_HEADS + h + 1])
        eg.append(jnp.exp(gc_col[h]))

    l_b, attn, rhs = [], [], []
    for h in heads:
        gc_row = gcum_t[N_GDN_HEADS + h:N_GDN_HEADS + h + 1, :]
        decay = jnp.exp(gc_col[h] - gc_row + negmask)
        xk_b = xk[h].astype(BF16)
        kk = _dot_nt((xk[h] * beta[h]).astype(BF16), xk_b)
        l_b.append((kk * decay).astype(BF16))
        attn.append((_dot_nt(xq[h].astype(BF16), xk_b) * decay).astype(BF16))
        vv = act[4 * (h // 2) + 2]
        rhs.append(jnp.where(mk[h], gn[h] * eg[h], vv) * beta[h])

    a1 = [l_b[h] * base_neg for h in heads]
    a2 = [_dot(a1[h], a1[h]).astype(BF16) for h in heads]
    a4 = [_dot(a2[h], a2[h]).astype(BF16) for h in heads]
    inv0 = [eye_b + a1[h] for h in heads]
    acc1 = [inv0[h].astype(F32) + _dot(a2[h], inv0[h]) for h in heads]
    inv_b = [(acc1[h] + _dot(a4[h], acc1[h].astype(BF16))).astype(BF16) for h in heads]
    for lvl in range(_GDN_LEVELS - 1):
        mid = [_dot(l_b[h] * bands[lvl], inv_b[h]).astype(BF16) for h in heads]
        inv_b = [inv_b[h] - _dot(inv_b[h], mid[h]).astype(BF16) for h in heads]
    half = [_dot(inv_b[h], rhs[h].astype(BF16)) for h in heads]
    mid = [_dot(l_b[h] * bands[_GDN_LEVELS - 1], half[h].astype(BF16)) for h in heads]
    sol = [half[h] - _dot(inv_b[h], mid[h].astype(BF16)) for h in heads]

    lane_lo_s = lax.broadcasted_iota(jnp.int32, (LANES, LANES), 1) < HEAD_DIM
    mk_s = [lane_lo_s if h % 2 == 0 else jnp.logical_not(lane_lo_s) for h in heads]
    sol_b = [sol[h].astype(BF16) for h in heads]
    attn_sol = [_dot(attn[h], sol_b[h]) for h in heads]
    q_eff = [(xq[h] * eg[h] - jnp.where(mk[h], attn_sol[h], 0.0)).astype(BF16) for h in heads]
    o_free = [jnp.where(mk[h], 0.0, attn_sol[h]) for h in heads]
    kw = [[] for _ in heads]
    ku = [[] for _ in heads]
    cdec = [[] for _ in heads]
    for c in range(nchunk):
        r0 = c * c_sz
        for h in heads:
            glast = gcum[r0 + c_sz - 1:r0 + c_sz, N_GDN_HEADS + h:N_GDN_HEADS + h + 1]
            kd_t = (xk[h][r0:r0 + c_sz] * jnp.exp(glast - gc_col[h][r0:r0 + c_sz])).T
            both = _dot(kd_t.astype(BF16), sol_b[h][r0:r0 + c_sz])
            kw[h].append(jnp.where(mk_s[h], both, 0.0).astype(BF16))
            ku[h].append(jnp.where(mk_s[h], 0.0, both))
            cdec[h].append(jnp.exp(glast))
    st = [state_ref[h] for h in heads]
    o_parts = [[] for _ in heads]
    for c in range(nchunk):
        r0 = c * c_sz
        for h in heads:
            lhs = jnp.concatenate([kw[h][c], q_eff[h][r0:r0 + c_sz]], axis=0)
            prod = _dot(lhs, st[h].astype(BF16))
            o_parts[h].append(prod[LANES:] + o_free[h][r0:r0 + c_sz])
            st[h] = st[h] * cdec[h][c] + ku[h][c] - prod[:LANES]
    for h in heads:
        state_ref[h] = st[h]

    for p in range(N_GDN_HEADS // 2):
        o_pair = [jnp.concatenate(o_parts[h], axis=0) for h in (2 * p, 2 * p + 1)]
        o = jnp.where(lane_lo, o_pair[1], o_pair[0])
        ms = _dot((o * o).astype(BF16), half_ones) * (1.0 / HEAD_DIM)
        zz = act[4 * p + 3]
        y = o * lax.rsqrt(ms + NORM_EPS) * nw_ref[...] * zz
        y_ref[:, p * LANES:(p + 1) * LANES] = y.astype(BF16)


def _dot_hi_exact_rhs_lhs(m_bf16, x):
    hi, lo = _split_bf16(x)
    return _dot(m_bf16, hi) + _dot(m_bf16, lo)


def _gdn_mixer(gdn, ba, conv_p, alog_v, dtb_v, nw_v, seq):
    t, c = gdn.shape
    rows = GDN_SUPER
    nsc = seq // rows
    hb = rows // SUBLANES
    return pl.pallas_call(
        _gdn_kernel,
        grid=(t // seq, nsc),
        in_specs=[
            pl.BlockSpec((rows, c), lambda b, s: (b * nsc + s, 0)),
            pl.BlockSpec((SUBLANES, c), lambda b, s: (jnp.maximum((b * nsc + s) * hb - 1, 0), 0)),
            pl.BlockSpec((rows, LANES), lambda b, s: (b * nsc + s, 0)),
            pl.BlockSpec((CONV_WIDTH, c), lambda b, s: (0, 0)),
            pl.BlockSpec((1, LANES), lambda b, s: (0, 0)),
            pl.BlockSpec((1, LANES), lambda b, s: (0, 0)),
            pl.BlockSpec((1, LANES), lambda b, s: (0, 0)),
            pl.BlockSpec((rows, rows), lambda b, s: (0, 0)),
            pl.BlockSpec((3 + _GDN_LEVELS, rows, rows), lambda b, s: (0, 0, 0)),
        ],
        out_specs=pl.BlockSpec((rows, GDN_DIM), lambda b, s: (b * nsc + s, 0)),
        out_shape=jax.ShapeDtypeStruct((t, GDN_DIM), BF16),
        scratch_shapes=[pltpu.VMEM((N_GDN_HEADS, LANES, LANES), F32), pltpu.VMEM((rows + SUBLANES, c), F32)],
        compiler_params=_cparams(("arbitrary", "arbitrary")),
        name="gdn_mixer",
    )(gdn, gdn, ba, conv_p, alog_v, dtb_v, nw_v, jnp.asarray(_GDN_NEGMASK), jnp.asarray(_GDN_BMASK, BF16))


def _route_tile(logits, before, carry_ref, live):
    shape = logits.shape
    lane = lax.broadcasted_iota(jnp.int32, shape, 1).astype(F32)
    work = logits
    vals, idxs = [], []
    for _k in range(TOP_K):
        m = jnp.max(work, axis=-1, keepdims=True)
        idx = jnp.min(jnp.where(work == m, lane, float(LANES)), axis=-1, keepdims=True)
        vals.append(m)
        idxs.append(idx)
        work = jnp.where(lane == idx, -jnp.inf, work)
    exps = [jnp.exp(v - vals[0]) for v in vals]
    den = exps[0] + exps[1] + exps[2] + exps[3]
    onehots = [lane == idx for idx in idxs]
    member = jnp.zeros(shape, F32)
    for oh in onehots:
        member = member + jnp.where(oh, 1.0, 0.0)
    rank = _dot(before, member.astype(BF16)) + carry_ref[...]
    carry_ref[...] = carry_ref[...] + live * jnp.sum(member, axis=0, keepdims=True)
    info = jnp.zeros(shape, F32)
    for k in range(TOP_K):
        rank_k = jnp.sum(jnp.where(onehots[k], rank, 0.0), axis=-1, keepdims=True)
        info = jnp.where(lane == float(k), idxs[k], info)
        info = jnp.where(lane == float(TOP_K + k), rank_k, info)
        info = jnp.where(lane == float(2 * TOP_K + k), exps[k] / den, info)
    return info


def _outproj_kernel(x_ref, mod_ref, yp_ref, ya_ref, yg_ref, wp_ref, wa_ref, wg_ref, lng_ref, lnb_ref,
                    rw_ref, rb_ref, before_ref, x1_ref, h2_ref, info_ref, er_ref, cnt_ref,
                    carry_ref, logit_s):
    step = pl.program_id(0)

    @pl.when(step == 0)
    def _():
        carry_ref[...] = jnp.zeros_like(carry_ref)
        logit_s[...] = jnp.zeros_like(logit_s)

    part = x_ref.shape[0] // OUT_SPLIT
    halves = [slice(j * part, (j + 1) * part) for j in range(OUT_SPLIT)]
    y = [_dot(yp_ref[r, :], wp_ref[...]) + _dot(ya_ref[r, :], wa_ref[...]) + _dot(yg_ref[r, :], wg_ref[...])
         for r in halves]
    live = jnp.where(step > 0, 1.0, 0.0)
    info = _route_tile(logit_s[...], before_ref[...], carry_ref, live)
    info_ref[...] = info
    er_ref[...] = info.T[:SUBLANES]
    cnt_ref[...] = carry_ref[...]
    g1 = mod_ref[0, 2:3, :]
    sh2 = mod_ref[0, 3:4, :]
    sc2 = mod_ref[0, 4:5, :]
    for r, y_r in zip(halves, y):
        x1 = _layer_norm(DEEPNORM_ALPHA * x_ref[r, :] + g1 * y_r, lng_ref[...], lnb_ref[...])
        x1_ref[r, :] = x1
        h2 = x1 * (1.0 + sc2) + sh2
        h2_ref[r, :] = _pack_bf16_pairs(h2)
        logit_s[r, :] = _dot(h2.astype(BF16), rw_ref[...]) + rb_ref[...]


def _out_projection(x2d, mod, yp, ya, yg, w_out_p, ln_g, ln_b, rw, rb, seq):
    t, d = x2d.shape
    tm = ROW_TILE
    wp = w_out_p[:POOL_DIM]
    wa = w_out_p[POOL_DIM:POOL_DIM + ATT_DIM]
    wg = w_out_p[POOL_DIM + ATT_DIM:]
    last = t // tm - 1
    row = lambda i: (jnp.minimum(i, last), 0)
    routed = lambda i: (jnp.maximum(i - 1, 0), 0)
    fixed = lambda i: (0, 0)
    return pl.pallas_call(
        _outproj_kernel,
        grid=(t // tm + 1,),
        in_specs=[
            pl.BlockSpec((tm, d), row),
            pl.BlockSpec((1, 6, d), lambda i: ((jnp.minimum(i, last) * tm) // seq, 0, 0)),
            pl.BlockSpec((tm, POOL_DIM), row),
            pl.BlockSpec((tm, ATT_DIM), row),
            pl.BlockSpec((tm, GDN_DIM), row),
            pl.BlockSpec((POOL_DIM, d), fixed),
            pl.BlockSpec((ATT_DIM, d), fixed),
            pl.BlockSpec((GDN_DIM, d), fixed),
            pl.BlockSpec((1, d), fixed),
            pl.BlockSpec((1, d), fixed),
            pl.BlockSpec((d, LANES), fixed),
            pl.BlockSpec((1, LANES), fixed),
            pl.BlockSpec((tm, tm), fixed),
        ],
        out_specs=[pl.BlockSpec((tm, d), row), pl.BlockSpec((tm, d // 2), row), pl.BlockSpec((tm, LANES), routed),
                   pl.BlockSpec((SUBLANES, tm), lambda i: (0, jnp.maximum(i - 1, 0))),
                   pl.BlockSpec((1, LANES), fixed)],
        out_shape=[jax.ShapeDtypeStruct((t, d), F32), jax.ShapeDtypeStruct((t, d // 2), jnp.int32),
                   jax.ShapeDtypeStruct((t, LANES), F32), jax.ShapeDtypeStruct((SUBLANES, t), F32),
                   jax.ShapeDtypeStruct((1, LANES), F32)],
        scratch_shapes=[pltpu.VMEM((1, LANES), F32), pltpu.VMEM((tm, LANES), F32)],
        compiler_params=_cparams(("arbitrary",)),
        name="out_proj_ln_route",
    )(x2d, mod, yp, ya, yg, wp, wa, wg, ln_g.reshape(1, d), ln_b.reshape(1, d), rw, rb,
      jnp.tril(jnp.ones((tm, tm), BF16), -1))


def _slot_kernel(er_ref, cnt_ref, dest_ref, pcum_ref):
    cnt = jnp.broadcast_to(cnt_ref[...], (SUBLANES, LANES))
    padded = jnp.floor((cnt + float(EXPERT_BLOCK - 1)) * (1.0 / EXPERT_BLOCK)) * float(EXPERT_BLOCK)
    lane8 = lax.broadcasted_iota(jnp.int32, (SUBLANES, LANES), 1)
    acc = padded
    step = 1
    while step < LANES:
        acc = acc + jnp.where(lane8 >= step, pltpu.roll(acc, step, axis=1), 0.0)
        step *= 2
    pcum_ref[...] = acc[:1].astype(jnp.int32)
    pstart = acc - padded

    er = er_ref[...]
    start = jnp.zeros(er.shape, F32)
    for e in range(N_EXPERTS):
        offset = jnp.sum(jnp.where(lane8 == e, pstart, 0.0), axis=-1, keepdims=True)
        start = jnp.where(er == float(e), offset, start)
    row = lax.broadcasted_iota(jnp.int32, er.shape, 0)
    slots = jnp.where(row < TOP_K, start + pltpu.roll(er, TOP_K, axis=0), 0.0)
    dest_ref[...] = slots.astype(jnp.int32)


def _slots(er, cnt):
    t = er.shape[1]
    return pl.pallas_call(
        _slot_kernel,
        grid=(1,),
        in_specs=[pl.BlockSpec((SUBLANES, t), lambda i: (0, 0)), pl.BlockSpec((1, LANES), lambda i: (0, 0))],
        out_specs=[pl.BlockSpec((SUBLANES, t), lambda i: (0, 0)), pl.BlockSpec((1, LANES), lambda i: (0, 0))],
        out_shape=[jax.ShapeDtypeStruct((SUBLANES, t), jnp.int32), jax.ShapeDtypeStruct((1, LANES), jnp.int32)],
        compiler_params=_cparams(("arbitrary",)),
        name="moe_slots",
    )(er, cnt)


def _expert_kernel(e0, be_ref, nxt_ref, val_ref, nu_ref, x_ref, wup_hbm, bup_ref, wdn_hbm, bdn_ref, y_ref,
                   wup_st, wdn_st, wup_bf, wdn_bf, sems):
    i = pl.program_id(0)
    e = be_ref[i]
    prev = be_ref[jnp.maximum(i - 1, 0)]
    used = i < nu_ref[0]

    def weight_copies(expert):
        return (pltpu.make_async_copy(wup_hbm.at[e0 + expert], wup_st, sems.at[0]),
                pltpu.make_async_copy(wdn_hbm.at[e0 + expert], wdn_st, sems.at[1]))

    @pl.when(i == 0)
    def _():
        for cp in weight_copies(e):
            cp.start()

    @pl.when(used & ((i == 0) | (e != prev)))
    def _():
        for cp in weight_copies(e):
            cp.wait()
        wup_bf[...] = wup_st[...].astype(BF16)
        wdn_bf[...] = wdn_st[...].astype(BF16)

        @pl.when(nxt_ref[i] >= 0)
        def _():
            for cp in weight_copies(nxt_ref[i]):
                cp.start()

    def ffn(rows):
        xb = _unpack_bf16_pairs(x_ref[:rows, :]).astype(BF16)
        hb = _dot(xb, wup_bf[...]) + bup_ref[0]
        x_glu = jnp.minimum(hb[:, :EXPERT_DIM], SWIGLU_LIMIT)
        x_lin = jnp.clip(hb[:, EXPERT_DIM:], -SWIGLU_LIMIT, SWIGLU_LIMIT)
        act = x_glu * _sigmoid(SWIGLU_ALPHA * x_glu) * (x_lin + 1.0)
        y = _dot(act.astype(BF16), wdn_bf[...]) + bdn_ref[0]
        y_ref[:rows, :] = _pack_bf16_pairs(y)

    half_rows = x_ref.shape[0] // 2
    small = val_ref[i] <= half_rows

    @pl.when(used & jnp.logical_not(small))
    def _():
        ffn(x_ref.shape[0])

    @pl.when(used & small)
    def _():
        ffn(half_rows)
        y_ref[half_rows:, :] = jnp.zeros((x_ref.shape[0] - half_rows, y_ref.shape[1]), y_ref.dtype)

    @pl.when(i >= nu_ref[0])
    def _():
        y_ref[...] = jnp.zeros_like(y_ref)


def _expert_ffn(xbuf, block_e, next_e, valid, n_used, w_up, b_up, w_down, b_down, layer):
    p, dh = xbuf.shape
    d = 2 * dh
    bm = EXPERT_BLOCK
    ne, _, n_up = w_up.shape
    e0 = layer * N_EXPERTS
    grid_spec = pltpu.PrefetchScalarGridSpec(
        num_scalar_prefetch=4,
        grid=(p // bm,),
        in_specs=[
            pl.BlockSpec((bm, dh), lambda i, be, nx, vl, nu: (i, 0)),
            pl.BlockSpec(memory_space=pl.ANY),
            pl.BlockSpec((1, 1, n_up), lambda i, be, nx, vl, nu: (e0 + be[i], 0, 0)),
            pl.BlockSpec(memory_space=pl.ANY),
            pl.BlockSpec((1, 1, d), lambda i, be, nx, vl, nu: (e0 + be[i], 0, 0)),
        ],
        out_specs=pl.BlockSpec((bm, dh), lambda i, be, nx, vl, nu: (i, 0)),
        scratch_shapes=[pltpu.VMEM((d, n_up), F32), pltpu.VMEM((EXPERT_DIM, d), F32),
                        pltpu.VMEM((d, n_up), BF16), pltpu.VMEM((EXPERT_DIM, d), BF16),
                        pltpu.SemaphoreType.DMA((2,))],
    )
    return pl.pallas_call(
        functools.partial(_expert_kernel, e0),
        grid_spec=grid_spec,
        out_shape=jax.ShapeDtypeStruct((p, dh), jnp.int32),
        compiler_params=_cparams(("arbitrary",)),
        name="expert_ffn",
    )(block_e, next_e, valid, n_used, xbuf, w_up, b_up, w_down, b_down)


def _combine_kernel(x1_ref, mod_ref, yg_ref, info_ref, lng_ref, lnb_ref, o_ref):
    info = info_ref[...]
    y = jnp.zeros(x1_ref.shape, F32)
    for k in range(TOP_K):
        gate = info[:, 2 * TOP_K + k:2 * TOP_K + k + 1]
        y = y + gate * _unpack_bf16_pairs(yg_ref[k])
    g2 = mod_ref[0, 5:6, :]
    o_ref[...] = _layer_norm(DEEPNORM_ALPHA * x1_ref[...] + g2 * y, lng_ref[...], lnb_ref[...])


def _combine(x1, mod, yg, info, ln_g, ln_b, seq):
    t, d = x1.shape
    tm = min(IN_TILE, seq)
    row = lambda i: (i, 0)
    fixed = lambda i: (0, 0)
    return pl.pallas_call(
        _combine_kernel,
        grid=(t // tm,),
        in_specs=[
            pl.BlockSpec((tm, d), row),
            pl.BlockSpec((1, 6, d), lambda i: ((i * tm) // seq, 0, 0)),
            pl.BlockSpec((TOP_K, tm, d // 2), lambda i: (0, i, 0)),
            pl.BlockSpec((tm, LANES), row),
            pl.BlockSpec((1, d), fixed),
            pl.BlockSpec((1, d), fixed),
        ],
        out_specs=pl.BlockSpec((tm, d), row),
        out_shape=jax.ShapeDtypeStruct((t, d), F32),
        compiler_params=_cparams(("arbitrary",)),
        name="moe_combine_ln",
    )(x1, mod, yg, info, ln_g.reshape(1, d), ln_b.reshape(1, d))


def _sc_workers():
    info = plsc.get_sparse_core_info()
    return info.num_cores, info.num_cores * info.num_subcores


def _sc_scatter_rows(rows, idx, n_out):
    t, w = rows.shape
    kk = idx.shape[0]
    n_cores, n_workers = _sc_workers()
    ch = SC_CHUNK
    assert t % (2 * n_workers * ch) == 0
    n_chunk = t // (n_workers * ch)
    idx_c = jnp.transpose(idx.reshape(kk, t // ch, ch), (1, 0, 2))

    @functools.partial(
        pl.kernel,
        mesh=plsc.VectorSubcoreMesh(core_axis_name="c", subcore_axis_name="s"),
        out_type=jax.ShapeDtypeStruct((n_out, w), rows.dtype),
        scratch_types=[pltpu.VMEM((2, kk, ch), jnp.int32), pltpu.VMEM((2, ch, w), rows.dtype),
                       pltpu.SemaphoreType.DMA((2,)), pltpu.SemaphoreType.DMA((2,))],
        name="sc_dispatch_scatter",
    )
    def scatter_kernel(rows_hbm, idx_hbm, out_hbm, idx_v, rows_v, load_sem, scat_sem):
        base = (lax.axis_index("s") * n_cores + lax.axis_index("c")) * n_chunk

        def load(j, b):
            return pltpu.make_async_copy(rows_hbm.at[pl.ds((base + j) * ch, ch)], rows_v.at[b], load_sem.at[b])

        def scatters(b):
            return [pltpu.make_async_copy(rows_v.at[b], out_hbm.at[idx_v.at[b, q]], scat_sem.at[b])
                    for q in range(kk)]

        pltpu.sync_copy(idx_hbm.at[base], idx_v.at[0])
        load(0, 0).start()

        @pl.loop(0, n_chunk, step=2)
        def _(j0):
            for b in range(2):
                j = j0 + b
                other = 1 - b

                @pl.when(j >= 1)
                def _():
                    for cp in scatters(other):
                        cp.wait()

                @pl.when(j + 1 < n_chunk)
                def _():
                    pltpu.sync_copy(idx_hbm.at[base + j + 1], idx_v.at[other])
                    load(j + 1, other).start()

                load(j, b).wait()
                for cp in scatters(b):
                    cp.start()

        for cp in scatters((n_chunk - 1) % 2):
            cp.wait()

    return scatter_kernel(rows, idx_c)


def _sc_gather_rows(table, idx):
    m = idx.shape[0]
    w = table.shape[1]
    n_cores, n_workers = _sc_workers()
    ch = SC_CHUNK
    assert m % (2 * n_workers * ch) == 0
    n_chunk = m // (n_workers * ch)
    idx_c = idx.reshape(m // ch, 1, ch)

    @functools.partial(
        pl.kernel,
        mesh=plsc.VectorSubcoreMesh(core_axis_name="c", subcore_axis_name="s"),
        out_type=jax.ShapeDtypeStruct((m, w), table.dtype),
        scratch_types=[pltpu.VMEM((2, 1, ch), jnp.int32), pltpu.VMEM((2, ch, w), table.dtype),
                       pltpu.SemaphoreType.DMA((2,)), pltpu.SemaphoreType.DMA((2,))],
        name="sc_combine_gather",
    )
    def gather_kernel(table_hbm, idx_hbm, out_hbm, idx_v, rows_v, gather_sem, write_sem):
        base = (lax.axis_index("s") * n_cores + lax.axis_index("c")) * n_chunk

        def gather(b):
            return pltpu.make_async_copy(table_hbm.at[idx_v.at[b, 0]], rows_v.at[b], gather_sem.at[b])

        def write(j, b):
            return pltpu.make_async_copy(rows_v.at[b], out_hbm.at[pl.ds((base + j) * ch, ch)], write_sem.at[b])

        pltpu.sync_copy(idx_hbm.at[base], idx_v.at[0])
        gather(0).start()

        @pl.loop(0, n_chunk, step=2)
        def _(j0):
            for b in range(2):
                j = j0 + b
                other = 1 - b

                @pl.when(j >= 1)
                def _():
                    write(j - 1, other).wait()

                @pl.when(j + 1 < n_chunk)
                def _():
                    pltpu.sync_copy(idx_hbm.at[base + j + 1], idx_v.at[other])
                    gather(other).start()

                gather(b).wait()
                write(j, b).start()

        write(n_chunk - 1, (n_chunk - 1) % 2).wait()

    return gather_kernel(table, idx_c)


def _lane_vector(vals, offset):
    return jnp.zeros((1, LANES), F32).at[0, offset:offset + vals.shape[0]].set(vals.astype(F32))


def _moe(h2, info, er, cnt, x1, mod, ln_g, ln_b, w_up, b_up, w_down, b_down, layer, seq):
    t, dh = h2.shape
    a = t * TOP_K
    bm = EXPERT_BLOCK
    slots, pcum_v = _slots(er, cnt)
    pcum = pcum_v[0, :N_EXPERTS]
    dest = slots[:TOP_K]
    n_blocks = -(-a // bm) + N_EXPERTS
    starts = jnp.arange(n_blocks, dtype=jnp.int32) * bm
    block_e = jnp.minimum(jnp.sum(pcum[None, :] <= starts[:, None], axis=1), N_EXPERTS - 1).astype(jnp.int32)
    n_used = (pcum[-1] // bm).astype(jnp.int32).reshape(1)
    later = block_e[None, :] > block_e[:, None]
    group_end = n_blocks - jnp.sum(later, axis=1)
    next_e = jnp.min(jnp.where(later, block_e[None, :], N_EXPERTS), axis=1)
    next_e = jnp.where(group_end < n_used[0], next_e, -1).astype(jnp.int32)
    counts = cnt[0, :N_EXPERTS].astype(jnp.int32)
    pstart = pcum - ((counts + bm - 1) // bm) * bm
    mine = block_e[:, None] == jnp.arange(N_EXPERTS, dtype=jnp.int32)[None, :]
    count_b = jnp.sum(jnp.where(mine, counts[None, :], 0), axis=1)
    pstart_b = jnp.sum(jnp.where(mine, pstart[None, :], 0), axis=1)
    valid = jnp.clip(count_b - (starts - pstart_b), 0, bm).astype(jnp.int32)
    xbuf = _sc_scatter_rows(h2, dest, n_blocks * bm)
    ybuf = _expert_ffn(xbuf, block_e, next_e, valid, n_used, w_up, b_up, w_down, b_down, layer)
    yg = _sc_gather_rows(ybuf, dest.reshape(a)).reshape(TOP_K, t, dh)
    return _combine(x1, mod, yg, info, ln_g, ln_b, seq)


def kernel(x, c, rel_bias, w_in, w_out, w_ada, b_ada, ln1_g, ln1_b, ln2_g, ln2_b, pool_w, pool_scale,
           attn_sinks, conv_w, gdn_a_log, gdn_dt_bias, gdn_norm_w, router_w, router_b,
           exp_w_up, exp_b_up, exp_w_down, exp_b_down):
    bsz, seq, d = x.shape
    depth = w_in.shape[0]
    t = bsz * seq
    assert d == D_MODEL and w_in.shape[2] == IN_DIM
    assert seq % GDN_SUPER == 0 and seq % (ATT_BLOCKS * WINDOW) == 0
    assert t % ROW_TILE == 0 and seq % ROW_TILE == 0

    mod_all = _modulation(c, w_ada, b_ada).reshape(depth, bsz, 6, d)
    bias = _band_bias(rel_bias)

    w_up_all = exp_w_up.reshape((depth * N_EXPERTS,) + exp_w_up.shape[2:])
    b_up_all = exp_b_up.reshape(depth * N_EXPERTS, 1, exp_b_up.shape[2])
    w_down_all = exp_w_down.reshape((depth * N_EXPERTS,) + exp_w_down.shape[2:])
    b_down_all = exp_b_down.reshape(depth * N_EXPERTS, 1, exp_b_down.shape[2])

    x2d = x.reshape(t, d)
    for l in range(depth):
        mod = mod_all[l]
        w_out_p = _take_static(w_out[l], _OUT_PERM, 0).astype(BF16)
        ident = jnp.zeros((CONV_WIDTH, 1), F32).at[CONV_WIDTH - 1, 0].set(1.0)
        conv_p = jnp.where(jnp.asarray(_GDN_CONV_SRC >= 0), _take_cols(conv_w[l].astype(F32), _GDN_CONV_SRC),
                           ident)
        pool_bd = jnp.zeros((POOL_DIM, POOL_DIM), F32)
        for gi in range(len(POOL_WINDOWS)):
            sl = slice(gi * POOL_GROUP, (gi + 1) * POOL_GROUP)
            pool_bd = pool_bd.at[sl, sl].set(pool_w[l, gi].astype(F32))
        alog_v = _lane_vector(gdn_a_log[l], N_GDN_HEADS)
        dtb_v = _lane_vector(gdn_dt_bias[l], N_GDN_HEADS)
        nw_v = jnp.tile(gdn_norm_w[l].astype(F32), 2).reshape(1, LANES)
        rw = jnp.zeros((d, LANES), BF16).at[:, :N_EXPERTS].set(router_w[l].astype(BF16))
        rb = jnp.full((1, LANES), NEG_INF, F32).at[0, :N_EXPERTS].set(router_b[l].astype(F32))

        u_pool, aq, akv, gdn, ba = _in_projection(x2d, mod, w_in.astype(F32), l, seq)
        y_pool = _pool_mixer(u_pool, pool_bd.astype(BF16), pool_scale[l].astype(F32), seq)
        y_att = _swa_attention(aq, akv, bias, attn_sinks[l].astype(F32), seq)
        y_gdn = _gdn_mixer(gdn, ba, conv_p, alog_v, dtb_v, nw_v, seq)
        x1, h2, info, er, cnt = _out_projection(x2d, mod, y_pool, y_att, y_gdn, w_out_p, ln1_g[l], ln1_b[l],
                                            rw, rb, seq)
        x2d = _moe(h2, info, er, cnt, x1, mod, ln2_g[l], ln2_b[l], w_up_all, b_up_all, w_down_all, b_down_all,
                   l, seq)
    return x2d.reshape(bsz, seq, d)
```

```python
import functools

import numpy as np
import jax
import jax.numpy as jnp
from jax import lax
from jax.experimental import pallas as pl
from jax.experimental.pallas import tpu as pltpu
from jax.experimental.pallas import tpu_sc as plsc

F32 = jnp.float32
BF16 = jnp.bfloat16

D_MODEL = 1024
HEAD_DIM = 64
POOL_DIM = 256
POOL_WINDOWS = (2, 4, 8, 16)
POOL_GROUP = 64
POOL_HALO = 16
N_ATT_HEADS = 6
N_KV_HEADS = 2
ATT_DIM = 384
KV_DIM = 128
WINDOW = 128
N_BUCKETS = 32
MAX_DISTANCE = 128
N_GDN_HEADS = 6
GDN_DIM = 384
CONV_WIDTH = 4
GDN_CHUNK = 64
N_EXPERTS = 32
TOP_K = 4
EXPERT_DIM = 1024
SWIGLU_ALPHA = 1.702
SWIGLU_LIMIT = 7.0
DEPTH = 2
DEEPNORM_ALPHA = (2 * DEPTH) ** 0.25
LN_EPS = 1e-5
NORM_EPS = 1e-6
NEG_INF = -1e30

LANES = 128
SUBLANES = 8
VMEM_LIMIT = 56 * 1024 * 1024

MOD_TILE = 2048
ROW_TILE = 512
OUT_SPLIT = 2
IN_TILE = 1024
ATT_BLOCKS = 8
ATT_GROUP = 6
GDN_SUPER = 256
EXPERT_BLOCK = 512
SC_CHUNK = 64

_OFF_AQ = POOL_DIM
_OFF_AK = _OFF_AQ + ATT_DIM
_OFF_AV = _OFF_AK + KV_DIM
_OFF_GQ = _OFF_AV + KV_DIM
_OFF_GK = _OFF_GQ + GDN_DIM
_OFF_GV = _OFF_GK + GDN_DIM
_OFF_GZ = _OFF_GV + GDN_DIM
_OFF_GB = _OFF_GZ + GDN_DIM
_OFF_GA = _OFF_GB + N_GDN_HEADS
IN_DIM = _OFF_GA + N_GDN_HEADS

P_POOL = (0, POOL_DIM)
P_Q = (P_POOL[1], P_POOL[1] + ATT_DIM)
P_KV = (P_Q[1], P_Q[1] + 2 * KV_DIM)
P_GDN = (P_KV[1], P_KV[1] + 4 * GDN_DIM)
P_BA = (P_GDN[1], P_GDN[1] + LANES)
P_TOTAL = P_BA[1]


def _head_cols(off, h):
    return list(range(off + HEAD_DIM * h, off + HEAD_DIM * (h + 1)))


def _build_in_perm():
    cols = list(range(POOL_DIM))
    for p in range(N_ATT_HEADS // 2):
        cols += _head_cols(_OFF_AQ, p) + _head_cols(_OFF_AQ, p + 3)
    cols += list(range(_OFF_AK, _OFF_AK + 2 * KV_DIM))
    gdn_src = []
    for p in range(N_GDN_HEADS // 2):
        e, o = 2 * p, 2 * p + 1
        grp = (_head_cols(_OFF_GK, e) + _head_cols(_OFF_GQ, e)
               + _head_cols(_OFF_GQ, o) + _head_cols(_OFF_GK, o)
               + _head_cols(_OFF_GV, o) + _head_cols(_OFF_GV, e)
               + _head_cols(_OFF_GZ, o) + _head_cols(_OFF_GZ, e))
        cols += grp
        gdn_src += [c - _OFF_GQ if c < _OFF_GZ else -1 for c in grp]
    cols += list(range(_OFF_GB, _OFF_GB + 2 * N_GDN_HEADS))
    cols += [-1] * (LANES - 2 * N_GDN_HEADS)
    assert len(cols) == P_TOTAL
    return np.asarray(cols, np.int32), np.asarray(gdn_src, np.int32)


_IN_PERM, _GDN_CONV_SRC = _build_in_perm()


def _build_out_perm():
    rows = list(range(POOL_DIM))
    for p in range(N_ATT_HEADS // 2):
        rows += _head_cols(POOL_DIM, p) + _head_cols(POOL_DIM, p + 3)
    for p in range(N_GDN_HEADS // 2):
        rows += _head_cols(POOL_DIM + ATT_DIM, 2 * p + 1) + _head_cols(POOL_DIM + ATT_DIM, 2 * p)
    return np.asarray(rows, np.int32)


_OUT_PERM = _build_out_perm()


def _t5_bucket_line():
    n = np.maximum(2 * WINDOW - 1 - np.arange(3 * WINDOW - 1), 0)
    max_exact = N_BUCKETS // 2
    nf = np.maximum(n, 1).astype(np.float32)
    large = max_exact + (np.log(nf / max_exact) / np.float32(np.log(MAX_DISTANCE / max_exact))
                         * (N_BUCKETS - max_exact)).astype(np.int32)
    large = np.minimum(large, N_BUCKETS - 1)
    return np.where(n < max_exact, n, large).astype(np.int32)


_BUCKET_LINE = _t5_bucket_line()


def _band_bias(rel_bias):
    n_line = 3 * WINDOW - 1
    line = jnp.take(rel_bias.astype(F32), jnp.asarray(_BUCKET_LINE), axis=0).T
    heads = line.shape[0]
    padded = jnp.concatenate([line, jnp.zeros((heads, 1), F32)], axis=1)
    skew = jnp.tile(padded, (1, WINDOW))[:, :WINDOW * n_line].reshape(heads, WINDOW, n_line)
    return skew[:, :, WINDOW - 1:3 * WINDOW - 1]


def _take_static(w, perm, axis):
    parts = []
    start = 0
    for i in range(1, len(perm) + 1):
        run_ends = (i == len(perm) or ((perm[i] < 0) != (perm[i - 1] < 0))
                    or (perm[i] >= 0 and perm[i] != perm[i - 1] + 1))
        if run_ends:
            if perm[start] < 0:
                shape = list(w.shape)
                shape[axis] = i - start
                parts.append(jnp.zeros(shape, w.dtype))
            else:
                parts.append(lax.slice_in_dim(w, int(perm[start]), int(perm[start]) + (i - start), axis=axis))
            start = i
    return jnp.concatenate(parts, axis=axis)


def _take_cols(w, perm):
    return _take_static(w, perm, w.ndim - 1)


def _split_bf16(x):
    hi = x.astype(BF16)
    lo = (x - hi.astype(F32)).astype(BF16)
    return hi, lo


def _pack_bf16_pairs(x):
    n = x.shape[1] // 2
    bits = pltpu.bitcast(x.astype(BF16).astype(F32), jnp.int32)
    return lax.shift_right_logical(bits[:, :n], 16) | bits[:, n:]


def _unpack_bf16_pairs(u):
    lo = pltpu.bitcast(lax.shift_left(u, 16), F32)
    hi = pltpu.bitcast(u & jnp.int32(-65536), F32)
    return jnp.concatenate([lo, hi], axis=1)


def _dot(a, b):
    return jnp.dot(a, b, preferred_element_type=F32)


def _dot_nt(a, b):
    return lax.dot_general(a, b, (((1,), (1,)), ((), ())), preferred_element_type=F32)


def _sigmoid(x):
    return 1.0 / (1.0 + jnp.exp(-x))


def _layer_norm(r, g, b):
    mu = jnp.mean(r, axis=-1, keepdims=True)
    d = r - mu
    var = jnp.mean(d * d, axis=-1, keepdims=True)
    return d * lax.rsqrt(var + LN_EPS) * g + b


def _cparams(sem):
    return pltpu.CompilerParams(dimension_semantics=sem, vmem_limit_bytes=VMEM_LIMIT)


def _mod_kernel(c_ref, w_ref, b_ref, o_ref):
    c = c_ref[...]
    ca = c * _sigmoid(c)
    o_ref[0] = _dot(ca.astype(BF16), w_ref[0].astype(BF16)) + b_ref[0]


def _modulation(c, w_ada, b_ada):
    depth, d, n = w_ada.shape
    bsz = c.shape[0]
    tn = MOD_TILE
    return pl.pallas_call(
        _mod_kernel,
        grid=(depth, n // tn),
        in_specs=[
            pl.BlockSpec((bsz, d), lambda l, j: (0, 0)),
            pl.BlockSpec((1, d, tn), lambda l, j: (l, 0, j)),
            pl.BlockSpec((1, 1, tn), lambda l, j: (l, 0, j)),
        ],
        out_specs=pl.BlockSpec((1, bsz, tn), lambda l, j: (l, 0, j)),
        out_shape=jax.ShapeDtypeStruct((depth, bsz, n), F32),
        compiler_params=_cparams(("arbitrary", "arbitrary")),
        name="adaln_mod",
    )(c, w_ada, b_ada.reshape(depth, 1, n))


def _perm_runs(perm):
    runs = []
    start = 0
    for i in range(1, len(perm) + 1):
        run_ends = (i == len(perm) or ((perm[i] < 0) != (perm[i - 1] < 0))
                    or (perm[i] >= 0 and perm[i] != perm[i - 1] + 1))
        if run_ends:
            runs.append((int(perm[start]) if perm[start] >= 0 else -1, i - start, start))
            start = i
    return runs


_IN_RUNS = _perm_runs(_IN_PERM)


def _pool_tile(ext, u, pos0, w_bd, scale):
    row = lax.broadcasted_iota(jnp.int32, ext.shape, 0)
    lane = lax.broadcasted_iota(jnp.int32, u.shape, 1)

    def shifted(a, s):
        return jnp.where(row >= s, pltpu.roll(a, s, axis=0), 0.0)

    sums = []
    acc = ext
    for wdt in POOL_WINDOWS:
        acc = acc + shifted(acc, wdt // 2)
        sums.append(acc[POOL_HALO:])
    grp = lane // POOL_GROUP
    wsum = sums[-1]
    win = jnp.full(u.shape, POOL_WINDOWS[-1], jnp.int32)
    for gi in range(len(POOL_WINDOWS) - 2, -1, -1):
        wsum = jnp.where(grp == gi, sums[gi], wsum)
        win = jnp.where(grp == gi, POOL_WINDOWS[gi], win)
    pos = pos0 + lax.broadcasted_iota(jnp.int32, u.shape, 0)
    cnt = jnp.minimum(pos + 1, win).astype(F32)
    p = wsum / cnt - u
    return (_dot(p.astype(BF16), w_bd) * scale).astype(BF16)


def _inproj_kernel(layer, seq, x_ref, mod_ref, wt_hbm, pw_ref, ps_ref, pool_ref, q_ref, kv_ref, gdn_ref, ba_ref,
                   wt_f32, tail_ref, wt_ref, halo_ref, sem):
    @pl.when(pl.program_id(0) == 0)
    def _():
        n_real = wt_hbm.shape[0]
        pad0 = (n_real // SUBLANES) * SUBLANES
        bulk = pltpu.make_async_copy(wt_hbm.at[pl.ds(0, pad0), layer, :], wt_f32.at[pl.ds(0, pad0)], sem.at[0])
        tail = pltpu.make_async_copy(wt_hbm.at[pl.ds(n_real - SUBLANES, SUBLANES), layer, :], tail_ref, sem.at[1])
        bulk.start()
        tail.start()
        bulk.wait()
        tail.wait()
        row8 = lax.broadcasted_iota(jnp.int32, tail_ref.shape, 0)
        left = n_real - pad0
        wt_f32[pad0:pad0 + SUBLANES, :] = jnp.where(row8 < left, pltpu.roll(tail_ref[...], left, axis=0), 0.0)
        wt_f32[pad0 + SUBLANES:, :] = jnp.zeros((wt_f32.shape[0] - pad0 - SUBLANES, wt_f32.shape[1]), F32)
        for src, n, dst in _IN_RUNS:
            if src >= 0:
                rows = n if src + n < n_real else wt_ref.shape[0] - dst
                wt_ref[dst:dst + rows, :] = wt_f32[src:src + rows, :].astype(BF16)
        q_rows = wt_ref[P_Q[0]:P_Q[1], :].astype(F32) * (HEAD_DIM ** -0.5)
        wt_ref[P_Q[0]:P_Q[1], :] = q_rows.astype(BF16)

    sh = mod_ref[0, 0:1, :]
    sc = mod_ref[0, 1:2, :]
    h = (x_ref[...] * (1.0 + sc) + sh).astype(BF16)

    def mm(rng):
        return _dot_nt(h, wt_ref[rng[0]:rng[1], :])

    tm = x_ref.shape[0]
    pos0 = (pl.program_id(0) * tm) % seq
    u = mm(P_POOL)
    ext = jnp.concatenate([jnp.where(pos0 == 0, 0.0, halo_ref[...]), u], axis=0)
    halo_ref[...] = u[tm - POOL_HALO:, :]
    gdn_ref[...] = mm(P_GDN)
    pool_ref[...] = _pool_tile(ext, u, pos0, pw_ref[...], ps_ref[...])
    q_ref[...] = mm(P_Q).astype(BF16)
    kv_ref[...] = mm(P_KV).astype(BF16)
    ba_ref[...] = mm(P_BA)


def _in_projection(x2d, mod, w_in, pool_w_bd, pool_scale, layer, seq):
    t, d = x2d.shape
    tm = min(IN_TILE, seq)
    assert seq % tm == 0 and tm > POOL_HALO
    widths = [r[1] - r[0] for r in (P_POOL, P_Q, P_KV, P_GDN, P_BA)]
    dtypes = [BF16, BF16, BF16, F32, F32]
    fixed = lambda i: (0, 0)
    return pl.pallas_call(
        functools.partial(_inproj_kernel, layer, seq),
        grid=(t // tm,),
        in_specs=[
            pl.BlockSpec((tm, d), lambda i: (i, 0)),
            pl.BlockSpec((1, 6, d), lambda i: ((i * tm) // seq, 0, 0)),
            pl.BlockSpec(memory_space=pl.ANY),
            pl.BlockSpec((POOL_DIM, POOL_DIM), fixed),
            pl.BlockSpec((1, POOL_DIM), fixed),
        ],
        out_specs=[pl.BlockSpec((tm, w), lambda i: (i, 0)) for w in widths],
        out_shape=[jax.ShapeDtypeStruct((t, w), dt) for w, dt in zip(widths, dtypes)],
        scratch_shapes=[pltpu.VMEM((P_TOTAL, d), F32), pltpu.VMEM((SUBLANES, d), F32),
                        pltpu.VMEM((P_TOTAL, d), BF16), pltpu.VMEM((POOL_HALO, POOL_DIM), F32),
                        pltpu.SemaphoreType.DMA((2,))],
        compiler_params=_cparams(("arbitrary",)),
        name="in_proj_pool",
    )(x2d, mod, jnp.transpose(w_in, (2, 0, 1)), pool_w_bd, pool_scale.reshape(1, POOL_DIM))


def _attn_kernel(sink_ref, q_ref, kvc_ref, kvp_ref, bias_ref, o_ref):
    step = pl.program_id(1)
    qi = lax.broadcasted_iota(jnp.int32, (WINDOW, 2 * WINDOW), 0)
    kj = lax.broadcasted_iota(jnp.int32, (WINDOW, 2 * WINDOW), 1)
    dist = qi + WINDOW - kj
    in_band = (dist >= 0) & (dist < WINDOW)
    lo = lax.broadcasted_iota(jnp.int32, (WINDOW, LANES), 1) < HEAD_DIM
    heads = [(p, half) for p in range(N_ATT_HEADS // 2) for half in range(2)]
    sinks = [sink_ref[p + 3 * half] for p, half in heads]
    for sub in range(ATT_BLOCKS):
        r0 = sub * WINDOW
        prev = kvp_ref[...] if sub == 0 else kvc_ref[r0 - WINDOW:r0, :]
        kv = jnp.concatenate([prev, kvc_ref[r0:r0 + WINDOW, :]], axis=0)
        k = kv[:, :KV_DIM]
        v = kv[:, KV_DIM:]
        valid = in_band & ((kj >= WINDOW) | (step > 0)) if sub == 0 else in_band
        for g0 in range(0, len(heads), ATT_GROUP):
            group = heads[g0:g0 + ATT_GROUP]
            sk_g = sinks[g0:g0 + ATT_GROUP]
            scores = []
            for p, half in group:
                qp = q_ref[r0:r0 + WINDOW, p * LANES:(p + 1) * LANES]
                qm = jnp.where(lo if half == 0 else jnp.logical_not(lo), qp, jnp.zeros_like(qp))
                scores.append(jnp.where(valid, _dot_nt(qm, k) + bias_ref[p + 3 * half], NEG_INF))
            tops = [jnp.maximum(jnp.max(s, axis=-1, keepdims=True), sk) for s, sk in zip(scores, sk_g)]
            probs = [jnp.exp(s - m) for s, m in zip(scores, tops)]
            dens = [jnp.sum(pr, axis=-1, keepdims=True) + jnp.exp(sk - m)
                    for pr, sk, m in zip(probs, sk_g, tops)]
            outs = [_dot(pr.astype(BF16), v) / den for pr, den in zip(probs, dens)]
            for idx in range(0, len(group), 2):
                p = group[idx][0]
                o_ref[r0:r0 + WINDOW, p * LANES:(p + 1) * LANES] = (
                    jnp.where(lo, outs[idx], outs[idx + 1]).astype(BF16))


def _swa_attention(q, kv, bias, sinks, seq):
    t = q.shape[0]
    rows = ATT_BLOCKS * WINDOW
    nblk = seq // rows
    return pl.pallas_call(
        _attn_kernel,
        grid=(t // seq, nblk),
        in_specs=[
            pl.BlockSpec(memory_space=pltpu.SMEM),
            pl.BlockSpec((rows, ATT_DIM), lambda b, n: (b * nblk + n, 0)),
            pl.BlockSpec((rows, 2 * KV_DIM), lambda b, n: (b * nblk + n, 0)),
            pl.BlockSpec((WINDOW, 2 * KV_DIM),
                         lambda b, n: (jnp.maximum((b * nblk + n) * ATT_BLOCKS - 1, 0), 0)),
            pl.BlockSpec((N_ATT_HEADS, WINDOW, 2 * WINDOW), lambda b, n: (0, 0, 0)),
        ],
        out_specs=pl.BlockSpec((rows, ATT_DIM), lambda b, n: (b * nblk + n, 0)),
        out_shape=jax.ShapeDtypeStruct((t, ATT_DIM), BF16),
        compiler_params=_cparams(("arbitrary", "arbitrary")),
        name="swa_attention",
    )(sinks, q, kv, kv, bias)


_GDN_BASE = SUBLANES
_GDN_LEVELS = int(np.log2(GDN_CHUNK // _GDN_BASE))


def _gdn_masks():
    r = np.arange(GDN_SUPER)
    ri, ci = r[:, None], r[None, :]
    same_chunk = (ri // GDN_CHUNK) == (ci // GDN_CHUNK)
    incl = same_chunk & (ri >= ci)
    planes = [incl, ri == ci]
    base = ((ri // _GDN_BASE) == (ci // _GDN_BASE)) & (ri > ci)
    planes.append(base)
    for lvl in range(_GDN_LEVELS):
        small = _GDN_BASE << lvl
        planes.append(((ri // (2 * small)) == (ci // (2 * small))) & ((ri // small) != (ci // small)) & (ri > ci))
    bmask = np.stack(planes).astype(np.float32)
    bmask[2] = -bmask[2]
    negmask = np.where(incl, 0.0, -np.inf).astype(np.float32)
    return negmask, bmask


_GDN_NEGMASK, _GDN_BMASK = _gdn_masks()


def _gdn_kernel(x_ref, halo_ref, ba_ref, cw_ref, alog_ref, dtb_ref, nw_ref, negmask_ref, bmask_ref,
                y_ref, state_ref, xs_ref):
    sc_id = pl.program_id(1)
    rows = GDN_SUPER
    nchunk = rows // GDN_CHUNK
    c_sz = GDN_CHUNK

    @pl.when(sc_id == 0)
    def _():
        state_ref[...] = jnp.zeros_like(state_ref)

    xs_ref[:SUBLANES, :] = jnp.where(sc_id == 0, 0.0, halo_ref[...])
    xs_ref[SUBLANES:, :] = x_ref[...]
    act = []
    for g in range(x_ref.shape[1] // LANES):
        cols = slice(g * LANES, (g + 1) * LANES)
        acc = x_ref[:, cols] * cw_ref[CONV_WIDTH - 1:CONV_WIDTH, cols]
        if g % 4 != 3:
            for s in range(1, CONV_WIDTH):
                acc = acc + (xs_ref[SUBLANES - s:SUBLANES - s + rows, cols]
                             * cw_ref[CONV_WIDTH - 1 - s:CONV_WIDTH - s, cols])
        act.append(acc * _sigmoid(acc))

    negmask = negmask_ref[...]
    tri_incl = bmask_ref[0]
    eye_b = bmask_ref[1]
    base_neg = bmask_ref[2]
    bands = [bmask_ref[3 + lvl] for lvl in range(_GDN_LEVELS)]
    li = lax.broadcasted_iota(jnp.int32, (LANES, LANES), 0)
    lj = lax.broadcasted_iota(jnp.int32, (LANES, LANES), 1)
    half_ones = jnp.where((li // HEAD_DIM) == (lj // HEAD_DIM), 1.0, 0.0).astype(BF16)
    lane_lo = lax.broadcasted_iota(jnp.int32, (rows, LANES), 1) < HEAD_DIM

    ba = ba_ref[...]
    beta_all = _sigmoid(ba)
    sp_in = ba + dtb_ref[...]
    softplus = jnp.maximum(sp_in, 0.0) + jnp.log(1.0 + jnp.exp(-jnp.abs(sp_in)))
    g_all = -jnp.exp(alog_ref[...]) * softplus
    gcum = _dot_hi_exact_rhs_lhs(tri_incl, g_all)
    gcum_t = gcum.T

    heads = range(N_GDN_HEADS)
    lane_hi = jnp.logical_not(lane_lo)
    mk = [lane_lo if h % 2 == 0 else lane_hi for h in heads]
    scale = HEAD_DIM ** -0.5

    xk, xq, gn, gc_col, beta, eg = [], [], [], [], [], []
    for h in heads:
        g = act[4 * (h // 2) + (h % 2)]
        g = g * lax.rsqrt(_dot((g * g).astype(BF16), half_ones) + NORM_EPS)
        gn.append(g)
        xk.append(jnp.where(mk[h], g, 0.0))
        xq.append(jnp.where(mk[h], pltpu.roll(g, HEAD_DIM, axis=1), 0.0) * scale)
        beta.append(beta_all[:, h:h + 1])
        gc_col.append(gcum[:, N_GDN_HEADS + h:N_GDN_HEADS + h + 1])
        eg.append(jnp.exp(gc_col[h]))

    l_b, attn, rhs = [], [], []
    for h in heads:
        gc_row = gcum_t[N_GDN_HEADS + h:N_GDN_HEADS + h + 1, :]
        decay = jnp.exp(gc_col[h] - gc_row + negmask)
        xk_b = xk[h].astype(BF16)
        kk = _dot_nt((xk[h] * beta[h]).astype(BF16), xk_b)
        l_b.append((kk * decay).astype(BF16))
        attn.append((_dot_nt(xq[h].astype(BF16), xk_b) * decay).astype(BF16))
        vv = act[4 * (h // 2) + 2]
        rhs.append(jnp.where(mk[h], gn[h] * eg[h], vv) * beta[h])

    a1 = [l_b[h] * base_neg for h in heads]
    a2 = [_dot(a1[h], a1[h]).astype(BF16) for h in heads]
    a4 = [_dot(a2[h], a2[h]).astype(BF16) for h in heads]
    inv0 = [eye_b + a1[h] for h in heads]
    acc1 = [inv0[h].astype(F32) + _dot(a2[h], inv0[h]) for h in heads]
    inv_b = [(acc1[h] + _dot(a4[h], acc1[h].astype(BF16))).astype(BF16) for h in heads]
    for lvl in range(_GDN_LEVELS - 1):
        mid = [_dot(l_b[h] * bands[lvl], inv_b[h]).astype(BF16) for h in heads]
        inv_b = [inv_b[h] - _dot(inv_b[h], mid[h]).astype(BF16) for h in heads]
    half = [_dot(inv_b[h], rhs[h].astype(BF16)) for h in heads]
    mid = [_dot(l_b[h] * bands[_GDN_LEVELS - 1], half[h].astype(BF16)) for h in heads]
    sol = [half[h] - _dot(inv_b[h], mid[h].astype(BF16)) for h in heads]

    lane_lo_s = lax.broadcasted_iota(jnp.int32, (LANES, LANES), 1) < HEAD_DIM
    mk_s = [lane_lo_s if h % 2 == 0 else jnp.logical_not(lane_lo_s) for h in heads]
    sol_b = [sol[h].astype(BF16) for h in heads]
    attn_sol = [_dot(attn[h], sol_b[h]) for h in heads]
    q_eff = [(xq[h] * eg[h] - jnp.where(mk[h], attn_sol[h], 0.0)).astype(BF16) for h in heads]
    o_free = [jnp.where(mk[h], 0.0, attn_sol[h]) for h in heads]
    kw = [[] for _ in heads]
    ku = [[] for _ in heads]
    cdec = [[] for _ in heads]
    for c in range(nchunk):
        r0 = c * c_sz
        for h in heads:
            glast = gcum[r0 + c_sz - 1:r0 + c_sz, N_GDN_HEADS + h:N_GDN_HEADS + h + 1]
            kd_t = (xk[h][r0:r0 + c_sz] * jnp.exp(glast - gc_col[h][r0:r0 + c_sz])).T
            both = _dot(kd_t.astype(BF16), sol_b[h][r0:r0 + c_sz])
            kw[h].append(jnp.where(mk_s[h], both, 0.0).astype(BF16))
            ku[h].append(jnp.where(mk_s[h], 0.0, both))
            cdec[h].append(jnp.exp(glast))
    st = [state_ref[h] for h in heads]
    o_parts = [[] for _ in heads]
    for c in range(nchunk):
        r0 = c * c_sz
        for h in heads:
            lhs = jnp.concatenate([kw[h][c], q_eff[h][r0:r0 + c_sz]], axis=0)
            prod = _dot(lhs, st[h].astype(BF16))
            o_parts[h].append(prod[LANES:] + o_free[h][r0:r0 + c_sz])
            st[h] = st[h] * cdec[h][c] + ku[h][c] - prod[:LANES]
    for h in heads:
        state_ref[h] = st[h]

    for p in range(N_GDN_HEADS // 2):
        o_pair = [jnp.concatenate(o_parts[h], axis=0) for h in (2 * p, 2 * p + 1)]
        o = jnp.where(lane_lo, o_pair[1], o_pair[0])
        ms = _dot((o * o).astype(BF16), half_ones) * (1.0 / HEAD_DIM)
        zz = act[4 * p + 3]
        y = o * lax.rsqrt(ms + NORM_EPS) * nw_ref[...] * zz
        y_ref[:, p * LANES:(p + 1) * LANES] = y.astype(BF16)


def _dot_hi_exact_rhs_lhs(m_bf16, x):
    hi, lo = _split_bf16(x)
    return _dot(m_bf16, hi) + _dot(m_bf16, lo)


def _gdn_mixer(gdn, ba, conv_p, alog_v, dtb_v, nw_v, seq):
    t, c = gdn.shape
    rows = GDN_SUPER
    nsc = seq // rows
    hb = rows // SUBLANES
    return pl.pallas_call(
        _gdn_kernel,
        grid=(t // seq, nsc),
        in_specs=[
            pl.BlockSpec((rows, c), lambda b, s: (b * nsc + s, 0)),
            pl.BlockSpec((SUBLANES, c), lambda b, s: (jnp.maximum((b * nsc + s) * hb - 1, 0), 0)),
            pl.BlockSpec((rows, LANES), lambda b, s: (b * nsc + s, 0)),
            pl.BlockSpec((CONV_WIDTH, c), lambda b, s: (0, 0)),
            pl.BlockSpec((1, LANES), lambda b, s: (0, 0)),
            pl.BlockSpec((1, LANES), lambda b, s: (0, 0)),
            pl.BlockSpec((1, LANES), lambda b, s: (0, 0)),
            pl.BlockSpec((rows, rows), lambda b, s: (0, 0)),
            pl.BlockSpec((3 + _GDN_LEVELS, rows, rows), lambda b, s: (0, 0, 0)),
        ],
        out_specs=pl.BlockSpec((rows, GDN_DIM), lambda b, s: (b * nsc + s, 0)),
        out_shape=jax.ShapeDtypeStruct((t, GDN_DIM), BF16),
        scratch_shapes=[pltpu.VMEM((N_GDN_HEADS, LANES, LANES), F32), pltpu.VMEM((rows + SUBLANES, c), F32)],
        compiler_params=_cparams(("arbitrary", "arbitrary")),
        name="gdn_mixer",
    )(gdn, gdn, ba, conv_p, alog_v, dtb_v, nw_v, jnp.asarray(_GDN_NEGMASK), jnp.asarray(_GDN_BMASK, BF16))


def _route_tile(logits, before, carry_ref, live):
    shape = logits.shape
    lane = lax.broadcasted_iota(jnp.int32, shape, 1).astype(F32)
    work = logits
    vals, idxs = [], []
    for _k in range(TOP_K):
        m = jnp.max(work, axis=-1, keepdims=True)
        idx = jnp.min(jnp.where(work == m, lane, float(LANES)), axis=-1, keepdims=True)
        vals.append(m)
        idxs.append(idx)
        work = jnp.where(lane == idx, -jnp.inf, work)
    exps = [jnp.exp(v - vals[0]) for v in vals]
    den = exps[0] + exps[1] + exps[2] + exps[3]
    onehots = [lane == idx for idx in idxs]
    member = jnp.zeros(shape, F32)
    for oh in onehots:
        member = member + jnp.where(oh, 1.0, 0.0)
    rank = _dot(before, member.astype(BF16)) + carry_ref[...]
    carry_ref[...] = carry_ref[...] + live * jnp.sum(member, axis=0, keepdims=True)
    info = jnp.zeros(shape, F32)
    for k in range(TOP_K):
        rank_k = jnp.sum(jnp.where(onehots[k], rank, 0.0), axis=-1, keepdims=True)
        info = jnp.where(lane == float(k), idxs[k], info)
        info = jnp.where(lane == float(TOP_K + k), rank_k, info)
        info = jnp.where(lane == float(2 * TOP_K + k), exps[k] / den, info)
    return info


def _outproj_kernel(x_ref, mod_ref, yp_ref, ya_ref, yg_ref, wp_ref, wa_ref, wg_ref, lng_ref, lnb_ref,
                    rw_ref, rb_ref, before_ref, x1_ref, h2_ref, info_ref, er_ref, cnt_ref,
                    carry_ref, logit_s):
    step = pl.program_id(0)

    @pl.when(step == 0)
    def _():
        carry_ref[...] = jnp.zeros_like(carry_ref)
        logit_s[...] = jnp.zeros_like(logit_s)

    part = x_ref.shape[0] // OUT_SPLIT
    halves = [slice(j * part, (j + 1) * part) for j in range(OUT_SPLIT)]
    y = [_dot(yp_ref[r, :], wp_ref[...]) + _dot(ya_ref[r, :], wa_ref[...]) + _dot(yg_ref[r, :], wg_ref[...])
         for r in halves]
    live = jnp.where(step > 0, 1.0, 0.0)
    info = _route_tile(logit_s[...], before_ref[...], carry_ref, live)
    info_ref[...] = info
    er_ref[...] = info.T[:SUBLANES]
    cnt_ref[...] = carry_ref[...]
    g1 = mod_ref[0, 2:3, :]
    sh2 = mod_ref[0, 3:4, :]
    sc2 = mod_ref[0, 4:5, :]
    for r, y_r in zip(halves, y):
        x1 = _layer_norm(DEEPNORM_ALPHA * x_ref[r, :] + g1 * y_r, lng_ref[...], lnb_ref[...])
        x1_ref[r, :] = x1
        h2 = x1 * (1.0 + sc2) + sh2
        h2_ref[r, :] = _pack_bf16_pairs(h2)
        logit_s[r, :] = _dot(h2.astype(BF16), rw_ref[...]) + rb_ref[...]


def _out_projection(x2d, mod, yp, ya, yg, w_out_p, ln_g, ln_b, rw, rb, seq):
    t, d = x2d.shape
    tm = ROW_TILE
    wp = w_out_p[:POOL_DIM]
    wa = w_out_p[POOL_DIM:POOL_DIM + ATT_DIM]
    wg = w_out_p[POOL_DIM + ATT_DIM:]
    last = t // tm - 1
    row = lambda i: (jnp.minimum(i, last), 0)
    routed = lambda i: (jnp.maximum(i - 1, 0), 0)
    fixed = lambda i: (0, 0)
    return pl.pallas_call(
        _outproj_kernel,
        grid=(t // tm + 1,),
        in_specs=[
            pl.BlockSpec((tm, d), row),
            pl.BlockSpec((1, 6, d), lambda i: ((jnp.minimum(i, last) * tm) // seq, 0, 0)),
            pl.BlockSpec((tm, POOL_DIM), row),
            pl.BlockSpec((tm, ATT_DIM), row),
            pl.BlockSpec((tm, GDN_DIM), row),
            pl.BlockSpec((POOL_DIM, d), fixed),
            pl.BlockSpec((ATT_DIM, d), fixed),
            pl.BlockSpec((GDN_DIM, d), fixed),
            pl.BlockSpec((1, d), fixed),
            pl.BlockSpec((1, d), fixed),
            pl.BlockSpec((d, LANES), fixed),
            pl.BlockSpec((1, LANES), fixed),
            pl.BlockSpec((tm, tm), fixed),
        ],
        out_specs=[pl.BlockSpec((tm, d), row), pl.BlockSpec((tm, d // 2), row), pl.BlockSpec((tm, LANES), routed),
                   pl.BlockSpec((SUBLANES, tm), lambda i: (0, jnp.maximum(i - 1, 0))),
                   pl.BlockSpec((1, LANES), fixed)],
        out_shape=[jax.ShapeDtypeStruct((t, d), F32), jax.ShapeDtypeStruct((t, d // 2), jnp.int32),
                   jax.ShapeDtypeStruct((t, LANES), F32), jax.ShapeDtypeStruct((SUBLANES, t), F32),
                   jax.ShapeDtypeStruct((1, LANES), F32)],
        scratch_shapes=[pltpu.VMEM((1, LANES), F32), pltpu.VMEM((tm, LANES), F32)],
        compiler_params=_cparams(("arbitrary",)),
        name="out_proj_ln_route",
    )(x2d, mod, yp, ya, yg, wp, wa, wg, ln_g.reshape(1, d), ln_b.reshape(1, d), rw, rb,
      jnp.tril(jnp.ones((tm, tm), BF16), -1))


def _slot_kernel(er_ref, cnt_ref, dest_ref, pcum_ref):
    cnt = jnp.broadcast_to(cnt_ref[...], (SUBLANES, LANES))
    padded = jnp.floor((cnt + float(EXPERT_BLOCK - 1)) * (1.0 / EXPERT_BLOCK)) * float(EXPERT_BLOCK)
    lane8 = lax.broadcasted_iota(jnp.int32, (SUBLANES, LANES), 1)
    acc = padded
    step = 1
    while step < LANES:
        acc = acc + jnp.where(lane8 >= step, pltpu.roll(acc, step, axis=1), 0.0)
        step *= 2
    pcum_ref[...] = acc[:1].astype(jnp.int32)
    pstart = acc - padded

    er = er_ref[...]
    start = jnp.zeros(er.shape, F32)
    for e in range(N_EXPERTS):
        offset = jnp.sum(jnp.where(lane8 == e, pstart, 0.0), axis=-1, keepdims=True)
        start = jnp.where(er == float(e), offset, start)
    row = lax.broadcasted_iota(jnp.int32, er.shape, 0)
    slots = jnp.where(row < TOP_K, start + pltpu.roll(er, TOP_K, axis=0), 0.0)
    dest_ref[...] = slots.astype(jnp.int32)


def _slots(er, cnt):
    t = er.shape[1]
    return pl.pallas_call(
        _slot_kernel,
        grid=(1,),
        in_specs=[pl.BlockSpec((SUBLANES, t), lambda i: (0, 0)), pl.BlockSpec((1, LANES), lambda i: (0, 0))],
        out_specs=[pl.BlockSpec((SUBLANES, t), lambda i: (0, 0)), pl.BlockSpec((1, LANES), lambda i: (0, 0))],
        out_shape=[jax.ShapeDtypeStruct((SUBLANES, t), jnp.int32), jax.ShapeDtypeStruct((1, LANES), jnp.int32)],
        compiler_params=_cparams(("arbitrary",)),
        name="moe_slots",
    )(er, cnt)


def _expert_kernel(e0, be_ref, nxt_ref, val_ref, nu_ref, x_ref, wup_hbm, bup_ref, wdn_hbm, bdn_ref, y_ref,
                   wup_st, wdn_st, wup_bf, wdn_bf, sems):
    i = pl.program_id(0)
    e = be_ref[i]
    prev = be_ref[jnp.maximum(i - 1, 0)]
    used = i < nu_ref[0]

    def weight_copies(expert):
        return (pltpu.make_async_copy(wup_hbm.at[e0 + expert], wup_st, sems.at[0]),
                pltpu.make_async_copy(wdn_hbm.at[e0 + expert], wdn_st, sems.at[1]))

    @pl.when(i == 0)
    def _():
        for cp in weight_copies(e):
            cp.start()

    @pl.when(used & ((i == 0) | (e != prev)))
    def _():
        for cp in weight_copies(e):
            cp.wait()
        wup_bf[...] = wup_st[...].astype(BF16)
        wdn_bf[...] = wdn_st[...].astype(BF16)

        @pl.when(nxt_ref[i] >= 0)
        def _():
            for cp in weight_copies(nxt_ref[i]):
                cp.start()

    def ffn(rows):
        xb = _unpack_bf16_pairs(x_ref[:rows, :]).astype(BF16)
        hb = _dot(xb, wup_bf[...]) + bup_ref[0]
        x_glu = jnp.minimum(hb[:, :EXPERT_DIM], SWIGLU_LIMIT)
        x_lin = jnp.clip(hb[:, EXPERT_DIM:], -SWIGLU_LIMIT, SWIGLU_LIMIT)
        act = x_glu * _sigmoid(SWIGLU_ALPHA * x_glu) * (x_lin + 1.0)
        y = _dot(act.astype(BF16), wdn_bf[...]) + bdn_ref[0]
        y_ref[:rows, :] = _pack_bf16_pairs(y)

    half_rows = x_ref.shape[0] // 2
    small = val_ref[i] <= half_rows

    @pl.when(used & jnp.logical_not(small))
    def _():
        ffn(x_ref.shape[0])

    @pl.when(used & small)
    def _():
        ffn(half_rows)
        y_ref[half_rows:, :] = jnp.zeros((x_ref.shape[0] - half_rows, y_ref.shape[1]), y_ref.dtype)

    @pl.when(i >= nu_ref[0])
    def _():
        y_ref[...] = jnp.zeros_like(y_ref)


def _expert_ffn(xbuf, block_e, next_e, valid, n_used, w_up, b_up, w_down, b_down, layer):
    p, dh = xbuf.shape
    d = 2 * dh
    bm = EXPERT_BLOCK
    ne, _, n_up = w_up.shape
    e0 = layer * N_EXPERTS
    grid_spec = pltpu.PrefetchScalarGridSpec(
        num_scalar_prefetch=4,
        grid=(p // bm,),
        in_specs=[
            pl.BlockSpec((bm, dh), lambda i, be, nx, vl, nu: (i, 0)),
            pl.BlockSpec(memory_space=pl.ANY),
            pl.BlockSpec((1, 1, n_up), lambda i, be, nx, vl, nu: (e0 + be[i], 0, 0)),
            pl.BlockSpec(memory_space=pl.ANY),
            pl.BlockSpec((1, 1, d), lambda i, be, nx, vl, nu: (e0 + be[i], 0, 0)),
        ],
        out_specs=pl.BlockSpec((bm, dh), lambda i, be, nx, vl, nu: (i, 0)),
        scratch_shapes=[pltpu.VMEM((d, n_up), F32), pltpu.VMEM((EXPERT_DIM, d), F32),
                        pltpu.VMEM((d, n_up), BF16), pltpu.VMEM((EXPERT_DIM, d), BF16),
                        pltpu.SemaphoreType.DMA((2,))],
    )
    return pl.pallas_call(
        functools.partial(_expert_kernel, e0),
        grid_spec=grid_spec,
        out_shape=jax.ShapeDtypeStruct((p, dh), jnp.int32),
        compiler_params=_cparams(("arbitrary",)),
        name="expert_ffn",
    )(block_e, next_e, valid, n_used, xbuf, w_up, b_up, w_down, b_down)


def _combine_kernel(x1_ref, mod_ref, yg_ref, info_ref, lng_ref, lnb_ref, o_ref):
    info = info_ref[...]
    y = jnp.zeros(x1_ref.shape, F32)
    for k in range(TOP_K):
        gate = info[:, 2 * TOP_K + k:2 * TOP_K + k + 1]
        y = y + gate * _unpack_bf16_pairs(yg_ref[k])
    g2 = mod_ref[0, 5:6, :]
    o_ref[...] = _layer_norm(DEEPNORM_ALPHA * x1_ref[...] + g2 * y, lng_ref[...], lnb_ref[...])


def _combine(x1, mod, yg, info, ln_g, ln_b, seq):
    t, d = x1.shape
    tm = min(IN_TILE, seq)
    row = lambda i: (i, 0)
    fixed = lambda i: (0, 0)
    return pl.pallas_call(
        _combine_kernel,
        grid=(t // tm,),
        in_specs=[
            pl.BlockSpec((tm, d), row),
            pl.BlockSpec((1, 6, d), lambda i: ((i * tm) // seq, 0, 0)),
            pl.BlockSpec((TOP_K, tm, d // 2), lambda i: (0, i, 0)),
            pl.BlockSpec((tm, LANES), row),
            pl.BlockSpec((1, d), fixed),
            pl.BlockSpec((1, d), fixed),
        ],
        out_specs=pl.BlockSpec((tm, d), row),
        out_shape=jax.ShapeDtypeStruct((t, d), F32),
        compiler_params=_cparams(("arbitrary",)),
        name="moe_combine_ln",
    )(x1, mod, yg, info, ln_g.reshape(1, d), ln_b.reshape(1, d))


def _sc_workers():
    info = plsc.get_sparse_core_info()
    return info.num_cores, info.num_cores * info.num_subcores


def _sc_scatter_rows(rows, idx, n_out):
    t, w = rows.shape
    kk = idx.shape[0]
    n_cores, n_workers = _sc_workers()
    ch = SC_CHUNK
    assert t % (2 * n_workers * ch) == 0
    n_chunk = t // (n_workers * ch)
    idx_c = jnp.transpose(idx.reshape(kk, t // ch, ch), (1, 0, 2))

    @functools.partial(
        pl.kernel,
        mesh=plsc.VectorSubcoreMesh(core_axis_name="c", subcore_axis_name="s"),
        out_type=jax.ShapeDtypeStruct((n_out, w), rows.dtype),
        scratch_types=[pltpu.VMEM((2, kk, ch), jnp.int32), pltpu.VMEM((2, ch, w), rows.dtype),
                       pltpu.SemaphoreType.DMA((2,)), pltpu.SemaphoreType.DMA((2,))],
        name="sc_dispatch_scatter",
    )
    def scatter_kernel(rows_hbm, idx_hbm, out_hbm, idx_v, rows_v, load_sem, scat_sem):
        base = (lax.axis_index("s") * n_cores + lax.axis_index("c")) * n_chunk

        def load(j, b):
            return pltpu.make_async_copy(rows_hbm.at[pl.ds((base + j) * ch, ch)], rows_v.at[b], load_sem.at[b])

        def scatters(b):
            return [pltpu.make_async_copy(rows_v.at[b], out_hbm.at[idx_v.at[b, q]], scat_sem.at[b])
                    for q in range(kk)]

        pltpu.sync_copy(idx_hbm.at[base], idx_v.at[0])
        load(0, 0).start()

        @pl.loop(0, n_chunk, step=2)
        def _(j0):
            for b in range(2):
                j = j0 + b
                other = 1 - b

                @pl.when(j >= 1)
                def _():
                    for cp in scatters(other):
                        cp.wait()

                @pl.when(j + 1 < n_chunk)
                def _():
                    pltpu.sync_copy(idx_hbm.at[base + j + 1], idx_v.at[other])
                    load(j + 1, other).start()

                load(j, b).wait()
                for cp in scatters(b):
                    cp.start()

        for cp in scatters((n_chunk - 1) % 2):
            cp.wait()

    return scatter_kernel(rows, idx_c)


def _sc_gather_rows(table, idx):
    m = idx.shape[0]
    w = table.shape[1]
    n_cores, n_workers = _sc_workers()
    ch = SC_CHUNK
    assert m % (2 * n_workers * ch) == 0
    n_chunk = m // (n_workers * ch)
    idx_c = idx.reshape(m // ch, 1, ch)

    @functools.partial(
        pl.kernel,
        mesh=plsc.VectorSubcoreMesh(core_axis_name="c", subcore_axis_name="s"),
        out_type=jax.ShapeDtypeStruct((m, w), table.dtype),
        scratch_types=[pltpu.VMEM((2, 1, ch), jnp.int32), pltpu.VMEM((2, ch, w), table.dtype),
                       pltpu.SemaphoreType.DMA((2,)), pltpu.SemaphoreType.DMA((2,))],
        name="sc_combine_gather",
    )
    def gather_kernel(table_hbm, idx_hbm, out_hbm, idx_v, rows_v, gather_sem, write_sem):
        base = (lax.axis_index("s") * n_cores + lax.axis_index("c")) * n_chunk

        def gather(b):
            return pltpu.make_async_copy(table_hbm.at[idx_v.at[b, 0]], rows_v.at[b], gather_sem.at[b])

        def write(j, b):
            return pltpu.make_async_copy(rows_v.at[b], out_hbm.at[pl.ds((base + j) * ch, ch)], write_sem.at[b])

        pltpu.sync_copy(idx_hbm.at[base], idx_v.at[0])
        gather(0).start()

        @pl.loop(0, n_chunk, step=2)
        def _(j0):
            for b in range(2):
                j = j0 + b
                other = 1 - b

                @pl.when(j >= 1)
                def _():
                    write(j - 1, other).wait()

                @pl.when(j + 1 < n_chunk)
                def _():
                    pltpu.sync_copy(idx_hbm.at[base + j + 1], idx_v.at[other])
                    gather(other).start()

                gather(b).wait()
                write(j, b).start()

        write(n_chunk - 1, (n_chunk - 1) % 2).wait()

    return gather_kernel(table, idx_c)


def _lane_vector(vals, offset):
    return jnp.zeros((1, LANES), F32).at[0, offset:offset + vals.shape[0]].set(vals.astype(F32))


def _moe(h2, info, er, cnt, x1, mod, ln_g, ln_b, w_up, b_up, w_down, b_down, layer, seq):
    t, dh = h2.shape
    a = t * TOP_K
    bm = EXPERT_BLOCK
    slots, pcum_v = _slots(er, cnt)
    pcum = pcum_v[0, :N_EXPERTS]
    dest = slots[:TOP_K]
    n_blocks = -(-a // bm) + N_EXPERTS
    starts = jnp.arange(n_blocks, dtype=jnp.int32) * bm
    block_e = jnp.minimum(jnp.sum(pcum[None, :] <= starts[:, None], axis=1), N_EXPERTS - 1).astype(jnp.int32)
    n_used = (pcum[-1] // bm).astype(jnp.int32).reshape(1)
    later = block_e[None, :] > block_e[:, None]
    group_end = n_blocks - jnp.sum(later, axis=1)
    next_e = jnp.min(jnp.where(later, block_e[None, :], N_EXPERTS), axis=1)
    next_e = jnp.where(group_end < n_used[0], next_e, -1).astype(jnp.int32)
    counts = cnt[0, :N_EXPERTS].astype(jnp.int32)
    pstart = pcum - ((counts + bm - 1) // bm) * bm
    mine = block_e[:, None] == jnp.arange(N_EXPERTS, dtype=jnp.int32)[None, :]
    count_b = jnp.sum(jnp.where(mine, counts[None, :], 0), axis=1)
    pstart_b = jnp.sum(jnp.where(mine, pstart[None, :], 0), axis=1)
    valid = jnp.clip(count_b - (starts - pstart_b), 0, bm).astype(jnp.int32)
    xbuf = _sc_scatter_rows(h2, dest, n_blocks * bm)
    ybuf = _expert_ffn(xbuf, block_e, next_e, valid, n_used, w_up, b_up, w_down, b_down, layer)
    yg = _sc_gather_rows(ybuf, dest.reshape(a)).reshape(TOP_K, t, dh)
    return _combine(x1, mod, yg, info, ln_g, ln_b, seq)


def kernel(x, c, rel_bias, w_in, w_out, w_ada, b_ada, ln1_g, ln1_b, ln2_g, ln2_b, pool_w, pool_scale,
           attn_sinks, conv_w, gdn_a_log, gdn_dt_bias, gdn_norm_w, router_w, router_b,
           exp_w_up, exp_b_up, exp_w_down, exp_b_down):
    bsz, seq, d = x.shape
    depth = w_in.shape[0]
    t = bsz * seq
    assert d == D_MODEL and w_in.shape[2] == IN_DIM and depth == DEPTH
    assert seq % GDN_SUPER == 0 and seq % (ATT_BLOCKS * WINDOW) == 0
    assert t % ROW_TILE == 0 and seq % ROW_TILE == 0

    mod_all = _modulation(c, w_ada, b_ada).reshape(depth, bsz, 6, d)
    bias = _band_bias(rel_bias)

    w_up_all = exp_w_up.reshape((depth * N_EXPERTS,) + exp_w_up.shape[2:])
    b_up_all = exp_b_up.reshape(depth * N_EXPERTS, 1, exp_b_up.shape[2])
    w_down_all = exp_w_down.reshape((depth * N_EXPERTS,) + exp_w_down.shape[2:])
    b_down_all = exp_b_down.reshape(depth * N_EXPERTS, 1, exp_b_down.shape[2])

    x2d = x.reshape(t, d)
    for l in range(depth):
        mod = mod_all[l]
        w_out_p = _take_static(w_out[l], _OUT_PERM, 0).astype(BF16)
        ident = jnp.zeros((CONV_WIDTH, 1), F32).at[CONV_WIDTH - 1, 0].set(1.0)
        conv_p = jnp.where(jnp.asarray(_GDN_CONV_SRC >= 0), _take_cols(conv_w[l].astype(F32), _GDN_CONV_SRC),
                           ident)
        pool_bd = jnp.zeros((POOL_DIM, POOL_DIM), F32)
        for gi in range(len(POOL_WINDOWS)):
            sl = slice(gi * POOL_GROUP, (gi + 1) * POOL_GROUP)
            pool_bd = pool_bd.at[sl, sl].set(pool_w[l, gi].astype(F32))
        alog_v = _lane_vector(gdn_a_log[l], N_GDN_HEADS)
        dtb_v = _lane_vector(gdn_dt_bias[l], N_GDN_HEADS)
        nw_v = jnp.tile(gdn_norm_w[l].astype(F32), 2).reshape(1, LANES)
        rw = jnp.zeros((d, LANES), BF16).at[:, :N_EXPERTS].set(router_w[l].astype(BF16))
        rb = jnp.full((1, LANES), NEG_INF, F32).at[0, :N_EXPERTS].set(router_b[l].astype(F32))

        y_pool, aq, akv, gdn, ba = _in_projection(x2d, mod, w_in.astype(F32), pool_bd.astype(BF16),
                                                  pool_scale[l].astype(F32), l, seq)
        y_att = _swa_attention(aq, akv, bias, attn_sinks[l].astype(F32), seq)
        y_gdn = _gdn_mixer(gdn, ba, conv_p, alog_v, dtb_v, nw_v, seq)
        x1, h2, info, er, cnt = _out_projection(x2d, mod, y_pool, y_att, y_gdn, w_out_p, ln1_g[l], ln1_b[l],
                                            rw, rb, seq)
        x2d = _moe(h2, info, er, cnt, x1, mod, ln2_g[l], ln2_b[l], w_up_all, b_up_all, w_down_all, b_down_all,
                   l, seq)
    return x2d.reshape(bsz, seq, d)
```

```python
import functools

import numpy as np
import jax
import jax.numpy as jnp
from jax import lax
from jax.experimental import pallas as pl
from jax.experimental.pallas import tpu as pltpu
from jax.experimental.pallas import tpu_sc as plsc

F32 = jnp.float32
BF16 = jnp.bfloat16

D_MODEL = 1024
HEAD_DIM = 64
POOL_DIM = 256
POOL_WINDOWS = (2, 4, 8, 16)
POOL_GROUP = 64
POOL_HALO = 16
N_ATT_HEADS = 6
N_KV_HEADS = 2
ATT_DIM = 384
KV_DIM = 128
WINDOW = 128
N_BUCKETS = 32
MAX_DISTANCE = 128
N_GDN_HEADS = 6
GDN_DIM = 384
CONV_WIDTH = 4
GDN_CHUNK = 64
N_EXPERTS = 32
TOP_K = 4
EXPERT_DIM = 1024
SWIGLU_ALPHA = 1.702
SWIGLU_LIMIT = 7.0
DEPTH = 2
DEEPNORM_ALPHA = (2 * DEPTH) ** 0.25
LN_EPS = 1e-5
NORM_EPS = 1e-6
NEG_INF = -1e30

LANES = 128
SUBLANES = 8
VMEM_LIMIT = 56 * 1024 * 1024

MOD_TILE = 2048
ROW_TILE = 512
OUT_SPLIT = 2
IN_TILE = 1024
ATT_BLOCKS = 8
ATT_GROUP = 6
GDN_SUPER = 256
EXPERT_BLOCK = 1024
EXPERT_GRAIN = 256
SC_CHUNK = 64

_OFF_AQ = POOL_DIM
_OFF_AK = _OFF_AQ + ATT_DIM
_OFF_AV = _OFF_AK + KV_DIM
_OFF_GQ = _OFF_AV + KV_DIM
_OFF_GK = _OFF_GQ + GDN_DIM
_OFF_GV = _OFF_GK + GDN_DIM
_OFF_GZ = _OFF_GV + GDN_DIM
_OFF_GB = _OFF_GZ + GDN_DIM
_OFF_GA = _OFF_GB + N_GDN_HEADS
IN_DIM = _OFF_GA + N_GDN_HEADS

P_POOL = (0, POOL_DIM)
P_Q = (P_POOL[1], P_POOL[1] + ATT_DIM)
P_KV = (P_Q[1], P_Q[1] + 2 * KV_DIM)
P_GDN = (P_KV[1], P_KV[1] + 4 * GDN_DIM)
P_BA = (P_GDN[1], P_GDN[1] + LANES)
P_TOTAL = P_BA[1]


def _head_cols(off, h):
    return list(range(off + HEAD_DIM * h, off + HEAD_DIM * (h + 1)))


def _build_in_perm():
    cols = list(range(POOL_DIM))
    for p in range(N_ATT_HEADS // 2):
        cols += _head_cols(_OFF_AQ, p) + _head_cols(_OFF_AQ, p + 3)
    cols += list(range(_OFF_AK, _OFF_AK + 2 * KV_DIM))
    gdn_src = []
    for p in range(N_GDN_HEADS // 2):
        e, o = 2 * p, 2 * p + 1
        grp = (_head_cols(_OFF_GK, e) + _head_cols(_OFF_GQ, e)
               + _head_cols(_OFF_GQ, o) + _head_cols(_OFF_GK, o)
               + _head_cols(_OFF_GV, o) + _head_cols(_OFF_GV, e)
               + _head_cols(_OFF_GZ, o) + _head_cols(_OFF_GZ, e))
        cols += grp
        gdn_src += [c - _OFF_GQ if c < _OFF_GZ else -1 for c in grp]
    cols += list(range(_OFF_GB, _OFF_GB + 2 * N_GDN_HEADS))
    cols += [-1] * (LANES - 2 * N_GDN_HEADS)
    assert len(cols) == P_TOTAL
    return np.asarray(cols, np.int32), np.asarray(gdn_src, np.int32)


_IN_PERM, _GDN_CONV_SRC = _build_in_perm()


def _build_out_perm():
    rows = list(range(POOL_DIM))
    for p in range(N_ATT_HEADS // 2):
        rows += _head_cols(POOL_DIM, p) + _head_cols(POOL_DIM, p + 3)
    for p in range(N_GDN_HEADS // 2):
        rows += _head_cols(POOL_DIM + ATT_DIM, 2 * p + 1) + _head_cols(POOL_DIM + ATT_DIM, 2 * p)
    return np.asarray(rows, np.int32)


_OUT_PERM = _build_out_perm()


def _t5_bucket_line():
    n = np.maximum(2 * WINDOW - 1 - np.arange(3 * WINDOW - 1), 0)
    max_exact = N_BUCKETS // 2
    nf = np.maximum(n, 1).astype(np.float32)
    large = max_exact + (np.log(nf / max_exact) / np.float32(np.log(MAX_DISTANCE / max_exact))
                         * (N_BUCKETS - max_exact)).astype(np.int32)
    large = np.minimum(large, N_BUCKETS - 1)
    return np.where(n < max_exact, n, large).astype(np.int32)


_BUCKET_LINE = _t5_bucket_line()


def _band_bias(rel_bias):
    n_line = 3 * WINDOW - 1
    line = jnp.take(rel_bias.astype(F32), jnp.asarray(_BUCKET_LINE), axis=0).T
    heads = line.shape[0]
    padded = jnp.concatenate([line, jnp.zeros((heads, 1), F32)], axis=1)
    skew = jnp.tile(padded, (1, WINDOW))[:, :WINDOW * n_line].reshape(heads, WINDOW, n_line)
    return skew[:, :, WINDOW - 1:3 * WINDOW - 1]


def _take_static(w, perm, axis):
    parts = []
    start = 0
    for i in range(1, len(perm) + 1):
        run_ends = (i == len(perm) or ((perm[i] < 0) != (perm[i - 1] < 0))
                    or (perm[i] >= 0 and perm[i] != perm[i - 1] + 1))
        if run_ends:
            if perm[start] < 0:
                shape = list(w.shape)
                shape[axis] = i - start
                parts.append(jnp.zeros(shape, w.dtype))
            else:
                parts.append(lax.slice_in_dim(w, int(perm[start]), int(perm[start]) + (i - start), axis=axis))
            start = i
    return jnp.concatenate(parts, axis=axis)


def _take_cols(w, perm):
    return _take_static(w, perm, w.ndim - 1)


def _split_bf16(x):
    hi = x.astype(BF16)
    lo = (x - hi.astype(F32)).astype(BF16)
    return hi, lo


def _pack_bf16_pairs(x):
    n = x.shape[1] // 2
    bits = pltpu.bitcast(x.astype(BF16).astype(F32), jnp.int32)
    return lax.shift_right_logical(bits[:, :n], 16) | bits[:, n:]


def _unpack_bf16_pairs(u):
    lo = pltpu.bitcast(lax.shift_left(u, 16), F32)
    hi = pltpu.bitcast(u & jnp.int32(-65536), F32)
    return jnp.concatenate([lo, hi], axis=1)


def _dot(a, b):
    return jnp.dot(a, b, preferred_element_type=F32)


def _dot_nt(a, b):
    return lax.dot_general(a, b, (((1,), (1,)), ((), ())), preferred_element_type=F32)


def _sigmoid(x):
    return 1.0 / (1.0 + jnp.exp(-x))


def _layer_norm(r, g, b):
    mu = jnp.mean(r, axis=-1, keepdims=True)
    d = r - mu
    var = jnp.mean(d * d, axis=-1, keepdims=True)
    return d * lax.rsqrt(var + LN_EPS) * g + b


def _cparams(sem):
    return pltpu.CompilerParams(dimension_semantics=sem, vmem_limit_bytes=VMEM_LIMIT)


def _mod_kernel(c_ref, w_ref, b_ref, o_ref):
    c = c_ref[...]
    ca = c * _sigmoid(c)
    o_ref[0] = _dot(ca.astype(BF16), w_ref[0].astype(BF16)) + b_ref[0]


def _modulation(c, w_ada, b_ada):
    depth, d, n = w_ada.shape
    bsz = c.shape[0]
    tn = MOD_TILE
    return pl.pallas_call(
        _mod_kernel,
        grid=(depth, n // tn),
        in_specs=[
            pl.BlockSpec((bsz, d), lambda l, j: (0, 0)),
            pl.BlockSpec((1, d, tn), lambda l, j: (l, 0, j)),
            pl.BlockSpec((1, 1, tn), lambda l, j: (l, 0, j)),
        ],
        out_specs=pl.BlockSpec((1, bsz, tn), lambda l, j: (l, 0, j)),
        out_shape=jax.ShapeDtypeStruct((depth, bsz, n), F32),
        compiler_params=_cparams(("arbitrary", "arbitrary")),
        name="adaln_mod",
    )(c, w_ada, b_ada.reshape(depth, 1, n))


def _perm_runs(perm):
    runs = []
    start = 0
    for i in range(1, len(perm) + 1):
        run_ends = (i == len(perm) or ((perm[i] < 0) != (perm[i - 1] < 0))
                    or (perm[i] >= 0 and perm[i] != perm[i - 1] + 1))
        if run_ends:
            runs.append((int(perm[start]) if perm[start] >= 0 else -1, i - start, start))
            start = i
    return runs


_IN_RUNS = _perm_runs(_IN_PERM)


def _pool_tile(ext, u, pos0, w_bd, scale):
    row = lax.broadcasted_iota(jnp.int32, ext.shape, 0)
    lane = lax.broadcasted_iota(jnp.int32, u.shape, 1)

    def shifted(a, s):
        return jnp.where(row >= s, pltpu.roll(a, s, axis=0), 0.0)

    sums = []
    acc = ext
    for wdt in POOL_WINDOWS:
        acc = acc + shifted(acc, wdt // 2)
        sums.append(acc[POOL_HALO:])
    grp = lane // POOL_GROUP
    wsum = sums[-1]
    win = jnp.full(u.shape, POOL_WINDOWS[-1], jnp.int32)
    for gi in range(len(POOL_WINDOWS) - 2, -1, -1):
        wsum = jnp.where(grp == gi, sums[gi], wsum)
        win = jnp.where(grp == gi, POOL_WINDOWS[gi], win)
    pos = pos0 + lax.broadcasted_iota(jnp.int32, u.shape, 0)
    cnt = jnp.minimum(pos + 1, win).astype(F32)
    p = wsum / cnt - u
    return (_dot(p.astype(BF16), w_bd) * scale).astype(BF16)


def _inproj_kernel(layer, seq, x_ref, mod_ref, wt_hbm, pw_ref, ps_ref, pool_ref, q_ref, kv_ref, gdn_ref, ba_ref,
                   wt_f32, tail_ref, wt_ref, halo_ref, sem):
    @pl.when(pl.program_id(0) == 0)
    def _():
        n_real = wt_hbm.shape[0]
        pad0 = (n_real // SUBLANES) * SUBLANES
        bulk = pltpu.make_async_copy(wt_hbm.at[pl.ds(0, pad0), layer, :], wt_f32.at[pl.ds(0, pad0)], sem.at[0])
        tail = pltpu.make_async_copy(wt_hbm.at[pl.ds(n_real - SUBLANES, SUBLANES), layer, :], tail_ref, sem.at[1])
        bulk.start()
        tail.start()
        bulk.wait()
        tail.wait()
        row8 = lax.broadcasted_iota(jnp.int32, tail_ref.shape, 0)
        left = n_real - pad0
        wt_f32[pad0:pad0 + SUBLANES, :] = jnp.where(row8 < left, pltpu.roll(tail_ref[...], left, axis=0), 0.0)
        wt_f32[pad0 + SUBLANES:, :] = jnp.zeros((wt_f32.shape[0] - pad0 - SUBLANES, wt_f32.shape[1]), F32)
        for src, n, dst in _IN_RUNS:
            if src >= 0:
                rows = n if src + n < n_real else wt_ref.shape[0] - dst
                wt_ref[dst:dst + rows, :] = wt_f32[src:src + rows, :].astype(BF16)
        q_rows = wt_ref[P_Q[0]:P_Q[1], :].astype(F32) * (HEAD_DIM ** -0.5)
        wt_ref[P_Q[0]:P_Q[1], :] = q_rows.astype(BF16)

    sh = mod_ref[0, 0:1, :]
    sc = mod_ref[0, 1:2, :]
    h = (x_ref[...] * (1.0 + sc) + sh).astype(BF16)

    def mm(rng):
        return _dot_nt(h, wt_ref[rng[0]:rng[1], :])

    tm = x_ref.shape[0]
    pos0 = (pl.program_id(0) * tm) % seq
    u = mm(P_POOL)
    ext = jnp.concatenate([jnp.where(pos0 == 0, 0.0, halo_ref[...]), u], axis=0)
    halo_ref[...] = u[tm - POOL_HALO:, :]
    gdn_ref[...] = mm(P_GDN)
    pool_ref[...] = _pool_tile(ext, u, pos0, pw_ref[...], ps_ref[...])
    q_ref[...] = mm(P_Q).astype(BF16)
    kv_ref[...] = mm(P_KV).astype(BF16)
    ba_ref[...] = mm(P_BA)


def _in_projection(x2d, mod, w_in, pool_w_bd, pool_scale, layer, seq):
    t, d = x2d.shape
    tm = min(IN_TILE, seq)
    assert seq % tm == 0 and tm > POOL_HALO
    widths = [r[1] - r[0] for r in (P_POOL, P_Q, P_KV, P_GDN, P_BA)]
    dtypes = [BF16, BF16, BF16, F32, F32]
    fixed = lambda i: (0, 0)
    return pl.pallas_call(
        functools.partial(_inproj_kernel, layer, seq),
        grid=(t // tm,),
        in_specs=[
            pl.BlockSpec((tm, d), lambda i: (i, 0)),
            pl.BlockSpec((1, 6, d), lambda i: ((i * tm) // seq, 0, 0)),
            pl.BlockSpec(memory_space=pl.ANY),
            pl.BlockSpec((POOL_DIM, POOL_DIM), fixed),
            pl.BlockSpec((1, POOL_DIM), fixed),
        ],
        out_specs=[pl.BlockSpec((tm, w), lambda i: (i, 0)) for w in widths],
        out_shape=[jax.ShapeDtypeStruct((t, w), dt) for w, dt in zip(widths, dtypes)],
        scratch_shapes=[pltpu.VMEM((P_TOTAL, d), F32), pltpu.VMEM((SUBLANES, d), F32),
                        pltpu.VMEM((P_TOTAL, d), BF16), pltpu.VMEM((POOL_HALO, POOL_DIM), F32),
                        pltpu.SemaphoreType.DMA((2,))],
        compiler_params=_cparams(("arbitrary",)),
        name="in_proj_pool",
    )(x2d, mod, jnp.transpose(w_in, (2, 0, 1)), pool_w_bd, pool_scale.reshape(1, POOL_DIM))


def _attn_kernel(sink_ref, q_ref, kvc_ref, kvp_ref, bias_ref, o_ref):
    step = pl.program_id(1)
    qi = lax.broadcasted_iota(jnp.int32, (WINDOW, 2 * WINDOW), 0)
    kj = lax.broadcasted_iota(jnp.int32, (WINDOW, 2 * WINDOW), 1)
    dist = qi + WINDOW - kj
    in_band = (dist >= 0) & (dist < WINDOW)
    lo = lax.broadcasted_iota(jnp.int32, (WINDOW, LANES), 1) < HEAD_DIM
    heads = [(p, half) for p in range(N_ATT_HEADS // 2) for half in range(2)]
    sinks = [sink_ref[p + 3 * half] for p, half in heads]
    for sub in range(ATT_BLOCKS):
        r0 = sub * WINDOW
        prev = kvp_ref[...] if sub == 0 else kvc_ref[r0 - WINDOW:r0, :]
        kv = jnp.concatenate([prev, kvc_ref[r0:r0 + WINDOW, :]], axis=0)
        k = kv[:, :KV_DIM]
        v = kv[:, KV_DIM:]
        valid = in_band & ((kj >= WINDOW) | (step > 0)) if sub == 0 else in_band
        for g0 in range(0, len(heads), ATT_GROUP):
            group = heads[g0:g0 + ATT_GROUP]
            sk_g = sinks[g0:g0 + ATT_GROUP]
            scores = []
            for p, half in group:
                qp = q_ref[r0:r0 + WINDOW, p * LANES:(p + 1) * LANES]
                qm = jnp.where(lo if half == 0 else jnp.logical_not(lo), qp, jnp.zeros_like(qp))
                scores.append(jnp.where(valid, _dot_nt(qm, k) + bias_ref[p + 3 * half], NEG_INF))
            tops = [jnp.maximum(jnp.max(s, axis=-1, keepdims=True), sk) for s, sk in zip(scores, sk_g)]
            probs = [jnp.exp(s - m) for s, m in zip(scores, tops)]
            dens = [jnp.sum(pr, axis=-1, keepdims=True) + jnp.exp(sk - m)
                    for pr, sk, m in zip(probs, sk_g, tops)]
            outs = [_dot(pr.astype(BF16), v) / den for pr, den in zip(probs, dens)]
            for idx in range(0, len(group), 2):
                p = group[idx][0]
                o_ref[r0:r0 + WINDOW, p * LANES:(p + 1) * LANES] = (
                    jnp.where(lo, outs[idx], outs[idx + 1]).astype(BF16))


def _swa_attention(q, kv, bias, sinks, seq):
    t = q.shape[0]
    rows = ATT_BLOCKS * WINDOW
    nblk = seq // rows
    return pl.pallas_call(
        _attn_kernel,
        grid=(t // seq, nblk),
        in_specs=[
            pl.BlockSpec(memory_space=pltpu.SMEM),
            pl.BlockSpec((rows, ATT_DIM), lambda b, n: (b * nblk + n, 0)),
            pl.BlockSpec((rows, 2 * KV_DIM), lambda b, n: (b * nblk + n, 0)),
            pl.BlockSpec((WINDOW, 2 * KV_DIM),
                         lambda b, n: (jnp.maximum((b * nblk + n) * ATT_BLOCKS - 1, 0), 0)),
            pl.BlockSpec((N_ATT_HEADS, WINDOW, 2 * WINDOW), lambda b, n: (0, 0, 0)),
        ],
        out_specs=pl.BlockSpec((rows, ATT_DIM), lambda b, n: (b * nblk + n, 0)),
        out_shape=jax.ShapeDtypeStruct((t, ATT_DIM), BF16),
        compiler_params=_cparams(("arbitrary", "arbitrary")),
        name="swa_attention",
    )(sinks, q, kv, kv, bias)


_GDN_BASE = SUBLANES
_GDN_LEVELS = int(np.log2(GDN_CHUNK // _GDN_BASE))


def _gdn_masks():
    r = np.arange(GDN_SUPER)
    ri, ci = r[:, None], r[None, :]
    same_chunk = (ri // GDN_CHUNK) == (ci // GDN_CHUNK)
    incl = same_chunk & (ri >= ci)
    planes = [incl, ri == ci]
    base = ((ri // _GDN_BASE) == (ci // _GDN_BASE)) & (ri > ci)
    planes.append(base)
    for lvl in range(_GDN_LEVELS):
        small = _GDN_BASE << lvl
        planes.append(((ri // (2 * small)) == (ci // (2 * small))) & ((ri // small) != (ci // small)) & (ri > ci))
    bmask = np.stack(planes).astype(np.float32)
    bmask[2] = -bmask[2]
    negmask = np.where(incl, 0.0, -np.inf).astype(np.float32)
    return negmask, bmask


_GDN_NEGMASK, _GDN_BMASK = _gdn_masks()


def _gdn_kernel(x_ref, halo_ref, ba_ref, cw_ref, alog_ref, dtb_ref, nw_ref, negmask_ref, bmask_ref,
                y_ref, state_ref, xs_ref):
    sc_id = pl.program_id(1)
    rows = GDN_SUPER
    nchunk = rows // GDN_CHUNK
    c_sz = GDN_CHUNK

    @pl.when(sc_id == 0)
    def _():
        state_ref[...] = jnp.zeros_like(state_ref)

    xs_ref[:SUBLANES, :] = jnp.where(sc_id == 0, 0.0, halo_ref[...])
    xs_ref[SUBLANES:, :] = x_ref[...]
    act = []
    for g in range(x_ref.shape[1] // LANES):
        cols = slice(g * LANES, (g + 1) * LANES)
        acc = x_ref[:, cols] * cw_ref[CONV_WIDTH - 1:CONV_WIDTH, cols]
        if g % 4 != 3:
            for s in range(1, CONV_WIDTH):
                acc = acc + (xs_ref[SUBLANES - s:SUBLANES - s + rows, cols]
                             * cw_ref[CONV_WIDTH - 1 - s:CONV_WIDTH - s, cols])
        act.append(acc * _sigmoid(acc))

    negmask = negmask_ref[...]
    tri_incl = bmask_ref[0]
    eye_b = bmask_ref[1]
    base_neg = bmask_ref[2]
    bands = [bmask_ref[3 + lvl] for lvl in range(_GDN_LEVELS)]
    li = lax.broadcasted_iota(jnp.int32, (LANES, LANES), 0)
    lj = lax.broadcasted_iota(jnp.int32, (LANES, LANES), 1)
    half_ones = jnp.where((li // HEAD_DIM) == (lj // HEAD_DIM), 1.0, 0.0).astype(BF16)
    lane_lo = lax.broadcasted_iota(jnp.int32, (rows, LANES), 1) < HEAD_DIM

    ba = ba_ref[...]
    beta_all = _sigmoid(ba)
    sp_in = ba + dtb_ref[...]
    softplus = jnp.maximum(sp_in, 0.0) + jnp.log(1.0 + jnp.exp(-jnp.abs(sp_in)))
    g_all = -jnp.exp(alog_ref[...]) * softplus
    gcum = _dot_hi_exact_rhs_lhs(tri_incl, g_all)
    gcum_t = gcum.T

    heads = range(N_GDN_HEADS)
    lane_hi = jnp.logical_not(lane_lo)
    mk = [lane_lo if h % 2 == 0 else lane_hi for h in heads]
    scale = HEAD_DIM ** -0.5

    xk, xq, gn, gc_col, beta, eg = [], [], [], [], [], []
    for h in heads:
        g = act[4 * (h // 2) + (h % 2)]
        g = g * lax.rsqrt(_dot((g * g).astype(BF16), half_ones) + NORM_EPS)
        gn.append(g)
        xk.append(jnp.where(mk[h], g, 0.0))
        xq.append(jnp.where(mk[h], pltpu.roll(g, HEAD_DIM, axis=1), 0.0) * scale)
        beta.append(beta_all[:, h:h + 1])
        gc_col.append(gcum[:, N_GDN_HEADS + h:N_GDN_HEADS + h + 1])
        eg.append(jnp.exp(gc_col[h]))

    l_b, attn, rhs = [], [], []
    for h in heads:
        gc_row = gcum_t[N_GDN_HEADS + h:N_GDN_HEADS + h + 1, :]
        decay = jnp.exp(gc_col[h] - gc_row + negmask)
        xk_b = xk[h].astype(BF16)
        kk = _dot_nt((xk[h] * beta[h]).astype(BF16), xk_b)
        l_b.append((kk * decay).astype(BF16))
        attn.append((_dot_nt(xq[h].astype(BF16), xk_b) * decay).astype(BF16))
        vv = act[4 * (h // 2) + 2]
        rhs.append(jnp.where(mk[h], gn[h] * eg[h], vv) * beta[h])

    a1 = [l_b[h] * base_neg for h in heads]
    a2 = [_dot(a1[h], a1[h]).astype(BF16) for h in heads]
    a4 = [_dot(a2[h], a2[h]).astype(BF16) for h in heads]
    inv0 = [eye_b + a1[h] for h in heads]
    acc1 = [inv0[h].astype(F32) + _dot(a2[h], inv0[h]) for h in heads]
    inv_b = [(acc1[h] + _dot(a4[h], acc1[h].astype(BF16))).astype(BF16) for h in heads]
    for lvl in range(_GDN_LEVELS - 1):
        mid = [_dot(l_b[h] * bands[lvl], inv_b[h]).astype(BF16) for h in heads]
        inv_b = [inv_b[h] - _dot(inv_b[h], mid[h]).astype(BF16) for h in heads]
    half = [_dot(inv_b[h], rhs[h].astype(BF16)) for h in heads]
    mid = [_dot(l_b[h] * bands[_GDN_LEVELS - 1], half[h].astype(BF16)) for h in heads]
    sol = [half[h] - _dot(inv_b[h], mid[h].astype(BF16)) for h in heads]

    lane_lo_s = lax.broadcasted_iota(jnp.int32, (LANES, LANES), 1) < HEAD_DIM
    mk_s = [lane_lo_s if h % 2 == 0 else jnp.logical_not(lane_lo_s) for h in heads]
    sol_b = [sol[h].astype(BF16) for h in heads]
    attn_sol = [_dot(attn[h], sol_b[h]) for h in heads]
    q_eff = [(xq[h] * eg[h] - jnp.where(mk[h], attn_sol[h], 0.0)).astype(BF16) for h in heads]
    o_free = [jnp.where(mk[h], 0.0, attn_sol[h]) for h in heads]
    kw = [[] for _ in heads]
    ku = [[] for _ in heads]
    cdec = [[] for _ in heads]
    for c in range(nchunk):
        r0 = c * c_sz
        for h in heads:
            glast = gcum[r0 + c_sz - 1:r0 + c_sz, N_GDN_HEADS + h:N_GDN_HEADS + h + 1]
            kd_t = (xk[h][r0:r0 + c_sz] * jnp.exp(glast - gc_col[h][r0:r0 + c_sz])).T
            both = _dot(kd_t.astype(BF16), sol_b[h][r0:r0 + c_sz])
            kw[h].append(jnp.where(mk_s[h], both, 0.0).astype(BF16))
            ku[h].append(jnp.where(mk_s[h], 0.0, both))
            cdec[h].append(jnp.exp(glast))
    st = [state_ref[h] for h in heads]
    o_parts = [[] for _ in heads]
    for c in range(nchunk):
        r0 = c * c_sz
        for h in heads:
            lhs = jnp.concatenate([kw[h][c], q_eff[h][r0:r0 + c_sz]], axis=0)
            prod = _dot(lhs, st[h].astype(BF16))
            o_parts[h].append(prod[LANES:] + o_free[h][r0:r0 + c_sz])
            st[h] = st[h] * cdec[h][c] + ku[h][c] - prod[:LANES]
    for h in heads:
        state_ref[h] = st[h]

    for p in range(N_GDN_HEADS // 2):
        o_pair = [jnp.concatenate(o_parts[h], axis=0) for h in (2 * p, 2 * p + 1)]
        o = jnp.where(lane_lo, o_pair[1], o_pair[0])
        ms = _dot((o * o).astype(BF16), half_ones) * (1.0 / HEAD_DIM)
        zz = act[4 * p + 3]
        y = o * lax.rsqrt(ms + NORM_EPS) * nw_ref[...] * zz
        y_ref[:, p * LANES:(p + 1) * LANES] = y.astype(BF16)


def _dot_hi_exact_rhs_lhs(m_bf16, x):
    hi, lo = _split_bf16(x)
    return _dot(m_bf16, hi) + _dot(m_bf16, lo)


def _gdn_mixer(gdn, ba, conv_p, alog_v, dtb_v, nw_v, seq):
    t, c = gdn.shape
    rows = GDN_SUPER
    nsc = seq // rows
    hb = rows // SUBLANES
    return pl.pallas_call(
        _gdn_kernel,
        grid=(t // seq, nsc),
        in_specs=[
            pl.BlockSpec((rows, c), lambda b, s: (b * nsc + s, 0)),
            pl.BlockSpec((SUBLANES, c), lambda b, s: (jnp.maximum((b * nsc + s) * hb - 1, 0), 0)),
            pl.BlockSpec((rows, LANES), lambda b, s: (b * nsc + s, 0)),
            pl.BlockSpec((CONV_WIDTH, c), lambda b, s: (0, 0)),
            pl.BlockSpec((1, LANES), lambda b, s: (0, 0)),
            pl.BlockSpec((1, LANES), lambda b, s: (0, 0)),
            pl.BlockSpec((1, LANES), lambda b, s: (0, 0)),
            pl.BlockSpec((rows, rows), lambda b, s: (0, 0)),
            pl.BlockSpec((3 + _GDN_LEVELS, rows, rows), lambda b, s: (0, 0, 0)),
        ],
        out_specs=pl.BlockSpec((rows, GDN_DIM), lambda b, s: (b * nsc + s, 0)),
        out_shape=jax.ShapeDtypeStruct((t, GDN_DIM), BF16),
        scratch_shapes=[pltpu.VMEM((N_GDN_HEADS, LANES, LANES), F32), pltpu.VMEM((rows + SUBLANES, c), F32)],
        compiler_params=_cparams(("arbitrary", "arbitrary")),
        name="gdn_mixer",
    )(gdn, gdn, ba, conv_p, alog_v, dtb_v, nw_v, jnp.asarray(_GDN_NEGMASK), jnp.asarray(_GDN_BMASK, BF16))


def _route_tile(logits, before, carry_ref, live):
    shape = logits.shape
    lane = lax.broadcasted_iota(jnp.int32, shape, 1).astype(F32)
    work = logits
    vals, idxs = [], []
    for _k in range(TOP_K):
        m = jnp.max(work, axis=-1, keepdims=True)
        idx = jnp.min(jnp.where(work == m, lane, float(LANES)), axis=-1, keepdims=True)
        vals.append(m)
        idxs.append(idx)
        work = jnp.where(lane == idx, -jnp.inf, work)
    exps = [jnp.exp(v - vals[0]) for v in vals]
    den = exps[0] + exps[1] + exps[2] + exps[3]
    onehots = [lane == idx for idx in idxs]
    member = jnp.zeros(shape, F32)
    for oh in onehots:
        member = member + jnp.where(oh, 1.0, 0.0)
    rank = _dot(before, member.astype(BF16)) + carry_ref[...]
    carry_ref[...] = carry_ref[...] + live * jnp.sum(member, axis=0, keepdims=True)
    info = jnp.zeros(shape, F32)
    for k in range(TOP_K):
        rank_k = jnp.sum(jnp.where(onehots[k], rank, 0.0), axis=-1, keepdims=True)
        info = jnp.where(lane == float(k), idxs[k], info)
        info = jnp.where(lane == float(TOP_K + k), rank_k, info)
        info = jnp.where(lane == float(2 * TOP_K + k), exps[k] / den, info)
    return info


def _outproj_kernel(x_ref, mod_ref, yp_ref, ya_ref, yg_ref, wp_ref, wa_ref, wg_ref, lng_ref, lnb_ref,
                    rw_ref, rb_ref, before_ref, x1_ref, h2_ref, info_ref, er_ref, cnt_ref,
                    carry_ref, logit_s):
    step = pl.program_id(0)

    @pl.when(step == 0)
    def _():
        carry_ref[...] = jnp.zeros_like(carry_ref)
        logit_s[...] = jnp.zeros_like(logit_s)

    part = x_ref.shape[0] // OUT_SPLIT
    halves = [slice(j * part, (j + 1) * part) for j in range(OUT_SPLIT)]
    y = [_dot(yp_ref[r, :], wp_ref[...]) + _dot(ya_ref[r, :], wa_ref[...]) + _dot(yg_ref[r, :], wg_ref[...])
         for r in halves]
    live = jnp.where(step > 0, 1.0, 0.0)
    info = _route_tile(logit_s[...], before_ref[...], carry_ref, live)
    info_ref[...] = info
    er_ref[...] = info.T[:SUBLANES]
    cnt_ref[...] = carry_ref[...]
    g1 = mod_ref[0, 2:3, :]
    sh2 = mod_ref[0, 3:4, :]
    sc2 = mod_ref[0, 4:5, :]
    for r, y_r in zip(halves, y):
        x1 = _layer_norm(DEEPNORM_ALPHA * x_ref[r, :] + g1 * y_r, lng_ref[...], lnb_ref[...])
        x1_ref[r, :] = x1
        h2 = x1 * (1.0 + sc2) + sh2
        h2_ref[r, :] = _pack_bf16_pairs(h2)
        logit_s[r, :] = _dot(h2.astype(BF16), rw_ref[...]) + rb_ref[...]


def _out_projection(x2d, mod, yp, ya, yg, w_out_p, ln_g, ln_b, rw, rb, seq):
    t, d = x2d.shape
    tm = ROW_TILE
    wp = w_out_p[:POOL_DIM]
    wa = w_out_p[POOL_DIM:POOL_DIM + ATT_DIM]
    wg = w_out_p[POOL_DIM + ATT_DIM:]
    last = t // tm - 1
    row = lambda i: (jnp.minimum(i, last), 0)
    routed = lambda i: (jnp.maximum(i - 1, 0), 0)
    fixed = lambda i: (0, 0)
    return pl.pallas_call(
        _outproj_kernel,
        grid=(t // tm + 1,),
        in_specs=[
            pl.BlockSpec((tm, d), row),
            pl.BlockSpec((1, 6, d), lambda i: ((jnp.minimum(i, last) * tm) // seq, 0, 0)),
            pl.BlockSpec((tm, POOL_DIM), row),
            pl.BlockSpec((tm, ATT_DIM), row),
            pl.BlockSpec((tm, GDN_DIM), row),
            pl.BlockSpec((POOL_DIM, d), fixed),
            pl.BlockSpec((ATT_DIM, d), fixed),
            pl.BlockSpec((GDN_DIM, d), fixed),
            pl.BlockSpec((1, d), fixed),
            pl.BlockSpec((1, d), fixed),
            pl.BlockSpec((d, LANES), fixed),
            pl.BlockSpec((1, LANES), fixed),
            pl.BlockSpec((tm, tm), fixed),
        ],
        out_specs=[pl.BlockSpec((tm, d), row), pl.BlockSpec((tm, d // 2), row), pl.BlockSpec((tm, LANES), routed),
                   pl.BlockSpec((SUBLANES, tm), lambda i: (0, jnp.maximum(i - 1, 0))),
                   pl.BlockSpec((1, LANES), fixed)],
        out_shape=[jax.ShapeDtypeStruct((t, d), F32), jax.ShapeDtypeStruct((t, d // 2), jnp.int32),
                   jax.ShapeDtypeStruct((t, LANES), F32), jax.ShapeDtypeStruct((SUBLANES, t), F32),
                   jax.ShapeDtypeStruct((1, LANES), F32)],
        scratch_shapes=[pltpu.VMEM((1, LANES), F32), pltpu.VMEM((tm, LANES), F32)],
        compiler_params=_cparams(("arbitrary",)),
        name="out_proj_ln_route",
    )(x2d, mod, yp, ya, yg, wp, wa, wg, ln_g.reshape(1, d), ln_b.reshape(1, d), rw, rb,
      jnp.tril(jnp.ones((tm, tm), BF16), -1))


def _slot_kernel(er_ref, cnt_ref, dest_ref, pcum_ref):
    cnt = jnp.broadcast_to(cnt_ref[...], (SUBLANES, LANES))
    padded = jnp.floor((cnt + float(EXPERT_BLOCK - 1)) * (1.0 / EXPERT_BLOCK)) * float(EXPERT_BLOCK)
    lane8 = lax.broadcasted_iota(jnp.int32, (SUBLANES, LANES), 1)
    acc = padded
    step = 1
    while step < LANES:
        acc = acc + jnp.where(lane8 >= step, pltpu.roll(acc, step, axis=1), 0.0)
        step *= 2
    pcum_ref[...] = acc[:1].astype(jnp.int32)
    pstart = acc - padded

    er = er_ref[...]
    start = jnp.zeros(er.shape, F32)
    for e in range(N_EXPERTS):
        offset = jnp.sum(jnp.where(lane8 == e, pstart, 0.0), axis=-1, keepdims=True)
        start = jnp.where(er == float(e), offset, start)
    row = lax.broadcasted_iota(jnp.int32, er.shape, 0)
    slots = jnp.where(row < TOP_K, start + pltpu.roll(er, TOP_K, axis=0), 0.0)
    dest_ref[...] = slots.astype(jnp.int32)


def _slots(er, cnt):
    t = er.shape[1]
    return pl.pallas_call(
        _slot_kernel,
        grid=(1,),
        in_specs=[pl.BlockSpec((SUBLANES, t), lambda i: (0, 0)), pl.BlockSpec((1, LANES), lambda i: (0, 0))],
        out_specs=[pl.BlockSpec((SUBLANES, t), lambda i: (0, 0)), pl.BlockSpec((1, LANES), lambda i: (0, 0))],
        out_shape=[jax.ShapeDtypeStruct((SUBLANES, t), jnp.int32), jax.ShapeDtypeStruct((1, LANES), jnp.int32)],
        compiler_params=_cparams(("arbitrary",)),
        name="moe_slots",
    )(er, cnt)


def _expert_kernel(e0, be_ref, nxt_ref, val_ref, nu_ref, x_ref, wup_hbm, bup_ref, wdn_hbm, bdn_ref, y_ref,
                   wup_st, wdn_st, wup_bf, wdn_bf, sems):
    i = pl.program_id(0)
    e = be_ref[i]
    prev = be_ref[jnp.maximum(i - 1, 0)]
    used = i < nu_ref[0]

    def weight_copies(expert):
        return (pltpu.make_async_copy(wup_hbm.at[e0 + expert], wup_st, sems.at[0]),
                pltpu.make_async_copy(wdn_hbm.at[e0 + expert], wdn_st, sems.at[1]))

    @pl.when(i == 0)
    def _():
        for cp in weight_copies(e):
            cp.start()

    @pl.when(used & ((i == 0) | (e != prev)))
    def _():
        for cp in weight_copies(e):
            cp.wait()
        wup_bf[...] = wup_st[...].astype(BF16)
        wdn_bf[...] = wdn_st[...].astype(BF16)

        @pl.when(nxt_ref[i] >= 0)
        def _():
            for cp in weight_copies(nxt_ref[i]):
                cp.start()

    def ffn(rows):
        xb = _unpack_bf16_pairs(x_ref[:rows, :]).astype(BF16)
        hb = _dot(xb, wup_bf[...]) + bup_ref[0]
        x_glu = jnp.minimum(hb[:, :EXPERT_DIM], SWIGLU_LIMIT)
        x_lin = jnp.clip(hb[:, EXPERT_DIM:], -SWIGLU_LIMIT, SWIGLU_LIMIT)
        act = x_glu * _sigmoid(SWIGLU_ALPHA * x_glu) * (x_lin + 1.0)
        y = _dot(act.astype(BF16), wdn_bf[...]) + bdn_ref[0]
        y_ref[:rows, :] = _pack_bf16_pairs(y)

    bm = x_ref.shape[0]
    pieces = (val_ref[i] + (EXPERT_GRAIN - 1)) // EXPERT_GRAIN
    for n_piece in range(1, bm // EXPERT_GRAIN + 1):
        rows_used = n_piece * EXPERT_GRAIN

        @pl.when(used & (pieces == n_piece))
        def _(rows_used=rows_used):
            ffn(rows_used)
            if rows_used < bm:
                y_ref[rows_used:, :] = jnp.zeros((bm - rows_used, y_ref.shape[1]), y_ref.dtype)

    @pl.when(i >= nu_ref[0])
    def _():
        y_ref[...] = jnp.zeros_like(y_ref)


def _expert_ffn(xbuf, block_e, next_e, valid, n_used, w_up, b_up, w_down, b_down, layer):
    p, dh = xbuf.shape
    d = 2 * dh
    bm = EXPERT_BLOCK
    ne, _, n_up = w_up.shape
    e0 = layer * N_EXPERTS
    grid_spec = pltpu.PrefetchScalarGridSpec(
        num_scalar_prefetch=4,
        grid=(p // bm,),
        in_specs=[
            pl.BlockSpec((bm, dh), lambda i, be, nx, vl, nu: (i, 0)),
            pl.BlockSpec(memory_space=pl.ANY),
            pl.BlockSpec((1, 1, n_up), lambda i, be, nx, vl, nu: (e0 + be[i], 0, 0)),
            pl.BlockSpec(memory_space=pl.ANY),
            pl.BlockSpec((1, 1, d), lambda i, be, nx, vl, nu: (e0 + be[i], 0, 0)),
        ],
        out_specs=pl.BlockSpec((bm, dh), lambda i, be, nx, vl, nu: (i, 0)),
        scratch_shapes=[pltpu.VMEM((d, n_up), F32), pltpu.VMEM((EXPERT_DIM, d), F32),
                        pltpu.VMEM((d, n_up), BF16), pltpu.VMEM((EXPERT_DIM, d), BF16),
                        pltpu.SemaphoreType.DMA((2,))],
    )
    return pl.pallas_call(
        functools.partial(_expert_kernel, e0),
        grid_spec=grid_spec,
        out_shape=jax.ShapeDtypeStruct((p, dh), jnp.int32),
        compiler_params=_cparams(("arbitrary",)),
        name="expert_ffn",
    )(block_e, next_e, valid, n_used, xbuf, w_up, b_up, w_down, b_down)


def _combine_kernel(x1_ref, mod_ref, yg_ref, info_ref, lng_ref, lnb_ref, o_ref):
    info = info_ref[...]
    y = jnp.zeros(x1_ref.shape, F32)
    for k in range(TOP_K):
        gate = info[:, 2 * TOP_K + k:2 * TOP_K + k + 1]
        y = y + gate * _unpack_bf16_pairs(yg_ref[k])
    g2 = mod_ref[0, 5:6, :]
    o_ref[...] = _layer_norm(DEEPNORM_ALPHA * x1_ref[...] + g2 * y, lng_ref[...], lnb_ref[...])


def _combine(x1, mod, yg, info, ln_g, ln_b, seq):
    t, d = x1.shape
    tm = min(IN_TILE, seq)
    row = lambda i: (i, 0)
    fixed = lambda i: (0, 0)
    return pl.pallas_call(
        _combine_kernel,
        grid=(t // tm,),
        in_specs=[
            pl.BlockSpec((tm, d), row),
            pl.BlockSpec((1, 6, d), lambda i: ((i * tm) // seq, 0, 0)),
            pl.BlockSpec((TOP_K, tm, d // 2), lambda i: (0, i, 0)),
            pl.BlockSpec((tm, LANES), row),
            pl.BlockSpec((1, d), fixed),
            pl.BlockSpec((1, d), fixed),
        ],
        out_specs=pl.BlockSpec((tm, d), row),
        out_shape=jax.ShapeDtypeStruct((t, d), F32),
        compiler_params=_cparams(("arbitrary",)),
        name="moe_combine_ln",
    )(x1, mod, yg, info, ln_g.reshape(1, d), ln_b.reshape(1, d))


def _sc_workers():
    info = plsc.get_sparse_core_info()
    return info.num_cores, info.num_cores * info.num_subcores


def _sc_scatter_rows(rows, idx, n_out):
    t, w = rows.shape
    kk = idx.shape[0]
    n_cores, n_workers = _sc_workers()
    ch = SC_CHUNK
    assert t % (2 * n_workers * ch) == 0
    n_chunk = t // (n_workers * ch)
    idx_c = jnp.transpose(idx.reshape(kk, t // ch, ch), (1, 0, 2))

    @functools.partial(
        pl.kernel,
        mesh=plsc.VectorSubcoreMesh(core_axis_name="c", subcore_axis_name="s"),
        out_type=jax.ShapeDtypeStruct((n_out, w), rows.dtype),
        scratch_types=[pltpu.VMEM((2, kk, ch), jnp.int32), pltpu.VMEM((2, ch, w), rows.dtype),
                       pltpu.SemaphoreType.DMA((2,)), pltpu.SemaphoreType.DMA((2,))],
        name="sc_dispatch_scatter",
    )
    def scatter_kernel(rows_hbm, idx_hbm, out_hbm, idx_v, rows_v, load_sem, scat_sem):
        base = (lax.axis_index("s") * n_cores + lax.axis_index("c")) * n_chunk

        def load(j, b):
            return pltpu.make_async_copy(rows_hbm.at[pl.ds((base + j) * ch, ch)], rows_v.at[b], load_sem.at[b])

        def scatters(b):
            return [pltpu.make_async_copy(rows_v.at[b], out_hbm.at[idx_v.at[b, q]], scat_sem.at[b])
                    for q in range(kk)]

        pltpu.sync_copy(idx_hbm.at[base], idx_v.at[0])
        load(0, 0).start()

        @pl.loop(0, n_chunk, step=2)
        def _(j0):
            for b in range(2):
                j = j0 + b
                other = 1 - b

                @pl.when(j >= 1)
                def _():
                    for cp in scatters(other):
                        cp.wait()

                @pl.when(j + 1 < n_chunk)
                def _():
                    pltpu.sync_copy(idx_hbm.at[base + j + 1], idx_v.at[other])
                    load(j + 1, other).start()

                load(j, b).wait()
                for cp in scatters(b):
                    cp.start()

        for cp in scatters((n_chunk - 1) % 2):
            cp.wait()

    return scatter_kernel(rows, idx_c)


def _sc_gather_rows(table, idx):
    m = idx.shape[0]
    w = table.shape[1]
    n_cores, n_workers = _sc_workers()
    ch = SC_CHUNK
    assert m % (2 * n_workers * ch) == 0
    n_chunk = m // (n_workers * ch)
    idx_c = idx.reshape(m // ch, 1, ch)

    @functools.partial(
        pl.kernel,
        mesh=plsc.VectorSubcoreMesh(core_axis_name="c", subcore_axis_name="s"),
        out_type=jax.ShapeDtypeStruct((m, w), table.dtype),
        scratch_types=[pltpu.VMEM((2, 1, ch), jnp.int32), pltpu.VMEM((2, ch, w), table.dtype),
                       pltpu.SemaphoreType.DMA((2,)), pltpu.SemaphoreType.DMA((2,))],
        name="sc_combine_gather",
    )
    def gather_kernel(table_hbm, idx_hbm, out_hbm, idx_v, rows_v, gather_sem, write_sem):
        base = (lax.axis_index("s") * n_cores + lax.axis_index("c")) * n_chunk

        def gather(b):
            return pltpu.make_async_copy(table_hbm.at[idx_v.at[b, 0]], rows_v.at[b], gather_sem.at[b])

        def write(j, b):
            return pltpu.make_async_copy(rows_v.at[b], out_hbm.at[pl.ds((base + j) * ch, ch)], write_sem.at[b])

        pltpu.sync_copy(idx_hbm.at[base], idx_v.at[0])
        gather(0).start()

        @pl.loop(0, n_chunk, step=2)
        def _(j0):
            for b in range(2):
                j = j0 + b
                other = 1 - b

                @pl.when(j >= 1)
                def _():
                    write(j - 1, other).wait()

                @pl.when(j + 1 < n_chunk)
                def _():
                    pltpu.sync_copy(idx_hbm.at[base + j + 1], idx_v.at[other])
                    gather(other).start()

                gather(b).wait()
                write(j, b).start()

        write(n_chunk - 1, (n_chunk - 1) % 2).wait()

    return gather_kernel(table, idx_c)


def _lane_vector(vals, offset):
    return jnp.zeros((1, LANES), F32).at[0, offset:offset + vals.shape[0]].set(vals.astype(F32))


def _moe(h2, info, er, cnt, x1, mod, ln_g, ln_b, w_up, b_up, w_down, b_down, layer, seq):
    t, dh = h2.shape
    a = t * TOP_K
    bm = EXPERT_BLOCK
    slots, pcum_v = _slots(er, cnt)
    pcum = pcum_v[0, :N_EXPERTS]
    dest = slots[:TOP_K]
    n_blocks = -(-a // bm) + N_EXPERTS
    starts = jnp.arange(n_blocks, dtype=jnp.int32) * bm
    block_e = jnp.minimum(jnp.sum(pcum[None, :] <= starts[:, None], axis=1), N_EXPERTS - 1).astype(jnp.int32)
    n_used = (pcum[-1] // bm).astype(jnp.int32).reshape(1)
    later = block_e[None, :] > block_e[:, None]
    group_end = n_blocks - jnp.sum(later, axis=1)
    next_e = jnp.min(jnp.where(later, block_e[None, :], N_EXPERTS), axis=1)
    next_e = jnp.where(group_end < n_used[0], next_e, -1).astype(jnp.int32)
    counts = cnt[0, :N_EXPERTS].astype(jnp.int32)
    pstart = pcum - ((counts + bm - 1) // bm) * bm
    mine = block_e[:, None] == jnp.arange(N_EXPERTS, dtype=jnp.int32)[None, :]
    count_b = jnp.sum(jnp.where(mine, counts[None, :], 0), axis=1)
    pstart_b = jnp.sum(jnp.where(mine, pstart[None, :], 0), axis=1)
    valid = jnp.clip(count_b - (starts - pstart_b), 0, bm).astype(jnp.int32)
    xbuf = _sc_scatter_rows(h2, dest, n_blocks * bm)
    ybuf = _expert_ffn(xbuf, block_e, next_e, valid, n_used, w_up, b_up, w_down, b_down, layer)
    yg = _sc_gather_rows(ybuf, dest.reshape(a)).reshape(TOP_K, t, dh)
    return _combine(x1, mod, yg, info, ln_g, ln_b, seq)


def kernel(x, c, rel_bias, w_in, w_out, w_ada, b_ada, ln1_g, ln1_b, ln2_g, ln2_b, pool_w, pool_scale,
           attn_sinks, conv_w, gdn_a_log, gdn_dt_bias, gdn_norm_w, router_w, router_b,
           exp_w_up, exp_b_up, exp_w_down, exp_b_down):
    bsz, seq, d = x.shape
    depth = w_in.shape[0]
    t = bsz * seq
    assert d == D_MODEL and w_in.shape[2] == IN_DIM and depth == DEPTH
    assert seq % GDN_SUPER == 0 and seq % (ATT_BLOCKS * WINDOW) == 0
    assert t % ROW_TILE == 0 and seq % ROW_TILE == 0

    mod_all = _modulation(c, w_ada, b_ada).reshape(depth, bsz, 6, d)
    bias = _band_bias(rel_bias)

    w_up_all = exp_w_up.reshape((depth * N_EXPERTS,) + exp_w_up.shape[2:])
    b_up_all = exp_b_up.reshape(depth * N_EXPERTS, 1, exp_b_up.shape[2])
    w_down_all = exp_w_down.reshape((depth * N_EXPERTS,) + exp_w_down.shape[2:])
    b_down_all = exp_b_down.reshape(depth * N_EXPERTS, 1, exp_b_down.shape[2])

    x2d = x.reshape(t, d)
    for l in range(depth):
        mod = mod_all[l]
        w_out_p = _take_static(w_out[l], _OUT_PERM, 0).astype(BF16)
        ident = jnp.zeros((CONV_WIDTH, 1), F32).at[CONV_WIDTH - 1, 0].set(1.0)
        conv_p = jnp.where(jnp.asarray(_GDN_CONV_SRC >= 0), _take_cols(conv_w[l].astype(F32), _GDN_CONV_SRC),
                           ident)
        pool_bd = jnp.zeros((POOL_DIM, POOL_DIM), F32)
        for gi in range(len(POOL_WINDOWS)):
            sl = slice(gi * POOL_GROUP, (gi + 1) * POOL_GROUP)
            pool_bd = pool_bd.at[sl, sl].set(pool_w[l, gi].astype(F32))
        alog_v = _lane_vector(gdn_a_log[l], N_GDN_HEADS)
        dtb_v = _lane_vector(gdn_dt_bias[l], N_GDN_HEADS)
        nw_v = jnp.tile(gdn_norm_w[l].astype(F32), 2).reshape(1, LANES)
        rw = jnp.zeros((d, LANES), BF16).at[:, :N_EXPERTS].set(router_w[l].astype(BF16))
        rb = jnp.full((1, LANES), NEG_INF, F32).at[0, :N_EXPERTS].set(router_b[l].astype(F32))

        y_pool, aq, akv, gdn, ba = _in_projection(x2d, mod, w_in.astype(F32), pool_bd.astype(BF16),
                                                  pool_scale[l].astype(F32), l, seq)
        y_att = _swa_attention(aq, akv, bias, attn_sinks[l].astype(F32), seq)
        y_gdn = _gdn_mixer(gdn, ba, conv_p, alog_v, dtb_v, nw_v, seq)
        x1, h2, info, er, cnt = _out_projection(x2d, mod, y_pool, y_att, y_gdn, w_out_p, ln1_g[l], ln1_b[l],
                                            rw, rb, seq)
        x2d = _moe(h2, info, er, cnt, x1, mod, ln2_g[l], ln2_b[l], w_up_all, b_up_all, w_down_all, b_down_all,
                   l, seq)
    return x2d.reshape(bsz, seq, d)
```

```python
import functools

import numpy as np
import jax
import jax.numpy as jnp
from jax import lax
from jax.experimental import pallas as pl
from jax.experimental.pallas import tpu as pltpu
from jax.experimental.pallas import tpu_sc as plsc

F32 = jnp.float32
BF16 = jnp.bfloat16

D_MODEL = 1024
HEAD_DIM = 64
POOL_DIM = 256
POOL_WINDOWS = (2, 4, 8, 16)
POOL_GROUP = 64
POOL_HALO = 16
N_ATT_HEADS = 6
N_KV_HEADS = 2
ATT_DIM = 384
KV_DIM = 128
WINDOW = 128
N_BUCKETS = 32
MAX_DISTANCE = 128
N_GDN_HEADS = 6
GDN_DIM = 384
CONV_WIDTH = 4
GDN_CHUNK = 64
N_EXPERTS = 32
TOP_K = 4
EXPERT_DIM = 1024
SWIGLU_ALPHA = 1.702
SWIGLU_LIMIT = 7.0
DEPTH = 2
DEEPNORM_ALPHA = (2 * DEPTH) ** 0.25
LN_EPS = 1e-5
NORM_EPS = 1e-6
NEG_INF = -1e30

LANES = 128
SUBLANES = 8
VMEM_LIMIT = 56 * 1024 * 1024

MOD_TILE = 2048
ROW_TILE = 512
OUT_SPLIT = 2
IN_TILE = 1024
ATT_BLOCKS = 8
ATT_GROUP = 6
GDN_SUPER = 256
EXPERT_BLOCK = 1024
EXPERT_GRAIN = 256
COMBINE_PARTS = 2
SC_CHUNK = 64

_OFF_AQ = POOL_DIM
_OFF_AK = _OFF_AQ + ATT_DIM
_OFF_AV = _OFF_AK + KV_DIM
_OFF_GQ = _OFF_AV + KV_DIM
_OFF_GK = _OFF_GQ + GDN_DIM
_OFF_GV = _OFF_GK + GDN_DIM
_OFF_GZ = _OFF_GV + GDN_DIM
_OFF_GB = _OFF_GZ + GDN_DIM
_OFF_GA = _OFF_GB + N_GDN_HEADS
IN_DIM = _OFF_GA + N_GDN_HEADS

P_POOL = (0, POOL_DIM)
P_Q = (P_POOL[1], P_POOL[1] + ATT_DIM)
P_KV = (P_Q[1], P_Q[1] + 2 * KV_DIM)
P_GDN = (P_KV[1], P_KV[1] + 4 * GDN_DIM)
P_BA = (P_GDN[1], P_GDN[1] + LANES)
P_TOTAL = P_BA[1]


def _head_cols(off, h):
    return list(range(off + HEAD_DIM * h, off + HEAD_DIM * (h + 1)))


def _build_in_perm():
    cols = list(range(POOL_DIM))
    for p in range(N_ATT_HEADS // 2):
        cols += _head_cols(_OFF_AQ, p) + _head_cols(_OFF_AQ, p + 3)
    cols += list(range(_OFF_AK, _OFF_AK + 2 * KV_DIM))
    gdn_src = []
    for p in range(N_GDN_HEADS // 2):
        e, o = 2 * p, 2 * p + 1
        grp = (_head_cols(_OFF_GK, e) + _head_cols(_OFF_GQ, e)
               + _head_cols(_OFF_GQ, o) + _head_cols(_OFF_GK, o)
               + _head_cols(_OFF_GV, o) + _head_cols(_OFF_GV, e)
               + _head_cols(_OFF_GZ, o) + _head_cols(_OFF_GZ, e))
        cols += grp
        gdn_src += [c - _OFF_GQ if c < _OFF_GZ else -1 for c in grp]
    cols += list(range(_OFF_GB, _OFF_GB + 2 * N_GDN_HEADS))
    cols += [-1] * (LANES - 2 * N_GDN_HEADS)
    assert len(cols) == P_TOTAL
    return np.asarray(cols, np.int32), np.asarray(gdn_src, np.int32)


_IN_PERM, _GDN_CONV_SRC = _build_in_perm()


def _build_out_perm():
    rows = list(range(POOL_DIM))
    for p in range(N_ATT_HEADS // 2):
        rows += _head_cols(POOL_DIM, p) + _head_cols(POOL_DIM, p + 3)
    for p in range(N_GDN_HEADS // 2):
        rows += _head_cols(POOL_DIM + ATT_DIM, 2 * p + 1) + _head_cols(POOL_DIM + ATT_DIM, 2 * p)
    return np.asarray(rows, np.int32)


_OUT_PERM = _build_out_perm()


def _t5_bucket_line():
    n = np.maximum(2 * WINDOW - 1 - np.arange(3 * WINDOW - 1), 0)
    max_exact = N_BUCKETS // 2
    nf = np.maximum(n, 1).astype(np.float32)
    large = max_exact + (np.log(nf / max_exact) / np.float32(np.log(MAX_DISTANCE / max_exact))
                         * (N_BUCKETS - max_exact)).astype(np.int32)
    large = np.minimum(large, N_BUCKETS - 1)
    return np.where(n < max_exact, n, large).astype(np.int32)


_BUCKET_LINE = _t5_bucket_line()


def _band_bias(rel_bias):
    n_line = 3 * WINDOW - 1
    line = jnp.take(rel_bias.astype(F32), jnp.asarray(_BUCKET_LINE), axis=0).T
    heads = line.shape[0]
    padded = jnp.concatenate([line, jnp.zeros((heads, 1), F32)], axis=1)
    skew = jnp.tile(padded, (1, WINDOW))[:, :WINDOW * n_line].reshape(heads, WINDOW, n_line)
    return skew[:, :, WINDOW - 1:3 * WINDOW - 1]


def _take_static(w, perm, axis):
    parts = []
    start = 0
    for i in range(1, len(perm) + 1):
        run_ends = (i == len(perm) or ((perm[i] < 0) != (perm[i - 1] < 0))
                    or (perm[i] >= 0 and perm[i] != perm[i - 1] + 1))
        if run_ends:
            if perm[start] < 0:
                shape = list(w.shape)
                shape[axis] = i - start
                parts.append(jnp.zeros(shape, w.dtype))
            else:
                parts.append(lax.slice_in_dim(w, int(perm[start]), int(perm[start]) + (i - start), axis=axis))
            start = i
    return jnp.concatenate(parts, axis=axis)


def _take_cols(w, perm):
    return _take_static(w, perm, w.ndim - 1)


def _split_bf16(x):
    hi = x.astype(BF16)
    lo = (x - hi.astype(F32)).astype(BF16)
    return hi, lo


def _pack_bf16_pairs(x):
    n = x.shape[1] // 2
    bits = pltpu.bitcast(x.astype(BF16).astype(F32), jnp.int32)
    return lax.shift_right_logical(bits[:, :n], 16) | bits[:, n:]


def _unpack_bf16_pairs(u):
    lo = pltpu.bitcast(lax.shift_left(u, 16), F32)
    hi = pltpu.bitcast(u & jnp.int32(-65536), F32)
    return jnp.concatenate([lo, hi], axis=1)


def _dot(a, b):
    return jnp.dot(a, b, preferred_element_type=F32)


def _dot_nt(a, b):
    return lax.dot_general(a, b, (((1,), (1,)), ((), ())), preferred_element_type=F32)


def _sigmoid(x):
    return 1.0 / (1.0 + jnp.exp(-x))


def _layer_norm(r, g, b):
    mu = jnp.mean(r, axis=-1, keepdims=True)
    d = r - mu
    var = jnp.mean(d * d, axis=-1, keepdims=True)
    return d * lax.rsqrt(var + LN_EPS) * g + b


def _cparams(sem):
    return pltpu.CompilerParams(dimension_semantics=sem, vmem_limit_bytes=VMEM_LIMIT)


def _mod_kernel(c_ref, w_ref, b_ref, o_ref):
    c = c_ref[...]
    ca = c * _sigmoid(c)
    o_ref[0] = _dot(ca.astype(BF16), w_ref[0].astype(BF16)) + b_ref[0]


def _modulation(c, w_ada, b_ada):
    depth, d, n = w_ada.shape
    bsz = c.shape[0]
    tn = MOD_TILE
    return pl.pallas_call(
        _mod_kernel,
        grid=(depth, n // tn),
        in_specs=[
            pl.BlockSpec((bsz, d), lambda l, j: (0, 0)),
            pl.BlockSpec((1, d, tn), lambda l, j: (l, 0, j)),
            pl.BlockSpec((1, 1, tn), lambda l, j: (l, 0, j)),
        ],
        out_specs=pl.BlockSpec((1, bsz, tn), lambda l, j: (l, 0, j)),
        out_shape=jax.ShapeDtypeStruct((depth, bsz, n), F32),
        compiler_params=_cparams(("arbitrary", "arbitrary")),
        name="adaln_mod",
    )(c, w_ada, b_ada.reshape(depth, 1, n))


def _perm_runs(perm):
    runs = []
    start = 0
    for i in range(1, len(perm) + 1):
        run_ends = (i == len(perm) or ((perm[i] < 0) != (perm[i - 1] < 0))
                    or (perm[i] >= 0 and perm[i] != perm[i - 1] + 1))
        if run_ends:
            runs.append((int(perm[start]) if perm[start] >= 0 else -1, i - start, start))
            start = i
    return runs


_IN_RUNS = _perm_runs(_IN_PERM)


def _pool_tile(ext, u, pos0, w_bd, scale):
    row = lax.broadcasted_iota(jnp.int32, ext.shape, 0)
    lane = lax.broadcasted_iota(jnp.int32, u.shape, 1)

    def shifted(a, s):
        return jnp.where(row >= s, pltpu.roll(a, s, axis=0), 0.0)

    sums = []
    acc = ext
    for wdt in POOL_WINDOWS:
        acc = acc + shifted(acc, wdt // 2)
        sums.append(acc[POOL_HALO:])
    grp = lane // POOL_GROUP
    wsum = sums[-1]
    win = jnp.full(u.shape, POOL_WINDOWS[-1], jnp.int32)
    for gi in range(len(POOL_WINDOWS) - 2, -1, -1):
        wsum = jnp.where(grp == gi, sums[gi], wsum)
        win = jnp.where(grp == gi, POOL_WINDOWS[gi], win)
    pos = pos0 + lax.broadcasted_iota(jnp.int32, u.shape, 0)
    cnt = jnp.minimum(pos + 1, win).astype(F32)
    p = wsum / cnt - u
    return (_dot(p.astype(BF16), w_bd) * scale).astype(BF16)


def _inproj_kernel(layer, seq, x_ref, mod_ref, wt_hbm, pw_ref, ps_ref, pool_ref, q_ref, kv_ref, gdn_ref, ba_ref,
                   wt_f32, tail_ref, wt_ref, halo_ref, sem):
    @pl.when(pl.program_id(0) == 0)
    def _():
        n_real = wt_hbm.shape[0]
        pad0 = (n_real // SUBLANES) * SUBLANES
        bulk = pltpu.make_async_copy(wt_hbm.at[pl.ds(0, pad0), layer, :], wt_f32.at[pl.ds(0, pad0)], sem.at[0])
        tail = pltpu.make_async_copy(wt_hbm.at[pl.ds(n_real - SUBLANES, SUBLANES), layer, :], tail_ref, sem.at[1])
        bulk.start()
        tail.start()
        bulk.wait()
        tail.wait()
        row8 = lax.broadcasted_iota(jnp.int32, tail_ref.shape, 0)
        left = n_real - pad0
        wt_f32[pad0:pad0 + SUBLANES, :] = jnp.where(row8 < left, pltpu.roll(tail_ref[...], left, axis=0), 0.0)
        wt_f32[pad0 + SUBLANES:, :] = jnp.zeros((wt_f32.shape[0] - pad0 - SUBLANES, wt_f32.shape[1]), F32)
        for src, n, dst in _IN_RUNS:
            if src >= 0:
                rows = n if src + n < n_real else wt_ref.shape[0] - dst
                wt_ref[dst:dst + rows, :] = wt_f32[src:src + rows, :].astype(BF16)
        q_rows = wt_ref[P_Q[0]:P_Q[1], :].astype(F32) * (HEAD_DIM ** -0.5)
        wt_ref[P_Q[0]:P_Q[1], :] = q_rows.astype(BF16)

    sh = mod_ref[0, 0:1, :]
    sc = mod_ref[0, 1:2, :]
    h = (x_ref[...] * (1.0 + sc) + sh).astype(BF16)

    def mm(rng):
        return _dot_nt(h, wt_ref[rng[0]:rng[1], :])

    tm = x_ref.shape[0]
    pos0 = (pl.program_id(0) * tm) % seq
    u = mm(P_POOL)
    ext = jnp.concatenate([jnp.where(pos0 == 0, 0.0, halo_ref[...]), u], axis=0)
    halo_ref[...] = u[tm - POOL_HALO:, :]
    gdn_ref[...] = mm(P_GDN)
    pool_ref[...] = _pool_tile(ext, u, pos0, pw_ref[...], ps_ref[...])
    q_ref[...] = mm(P_Q).astype(BF16)
    kv_ref[...] = mm(P_KV).astype(BF16)
    ba_ref[...] = mm(P_BA)


def _in_projection(x2d, mod, w_in, pool_w_bd, pool_scale, layer, seq):
    t, d = x2d.shape
    tm = min(IN_TILE, seq)
    assert seq % tm == 0 and tm > POOL_HALO
    widths = [r[1] - r[0] for r in (P_POOL, P_Q, P_KV, P_GDN, P_BA)]
    dtypes = [BF16, BF16, BF16, F32, F32]
    fixed = lambda i: (0, 0)
    return pl.pallas_call(
        functools.partial(_inproj_kernel, layer, seq),
        grid=(t // tm,),
        in_specs=[
            pl.BlockSpec((tm, d), lambda i: (i, 0)),
            pl.BlockSpec((1, 6, d), lambda i: ((i * tm) // seq, 0, 0)),
            pl.BlockSpec(memory_space=pl.ANY),
            pl.BlockSpec((POOL_DIM, POOL_DIM), fixed),
            pl.BlockSpec((1, POOL_DIM), fixed),
        ],
        out_specs=[pl.BlockSpec((tm, w), lambda i: (i, 0)) for w in widths],
        out_shape=[jax.ShapeDtypeStruct((t, w), dt) for w, dt in zip(widths, dtypes)],
        scratch_shapes=[pltpu.VMEM((P_TOTAL, d), F32), pltpu.VMEM((SUBLANES, d), F32),
                        pltpu.VMEM((P_TOTAL, d), BF16), pltpu.VMEM((POOL_HALO, POOL_DIM), F32),
                        pltpu.SemaphoreType.DMA((2,))],
        compiler_params=_cparams(("arbitrary",)),
        name="in_proj_pool",
    )(x2d, mod, jnp.transpose(w_in, (2, 0, 1)), pool_w_bd, pool_scale.reshape(1, POOL_DIM))


def _attn_kernel(sink_ref, q_ref, kvc_ref, kvp_ref, bias_ref, o_ref):
    step = pl.program_id(1)
    qi = lax.broadcasted_iota(jnp.int32, (WINDOW, 2 * WINDOW), 0)
    kj = lax.broadcasted_iota(jnp.int32, (WINDOW, 2 * WINDOW), 1)
    dist = qi + WINDOW - kj
    in_band = (dist >= 0) & (dist < WINDOW)
    lo = lax.broadcasted_iota(jnp.int32, (WINDOW, LANES), 1) < HEAD_DIM
    heads = [(p, half) for p in range(N_ATT_HEADS // 2) for half in range(2)]
    sinks = [sink_ref[p + 3 * half] for p, half in heads]
    for sub in range(ATT_BLOCKS):
        r0 = sub * WINDOW
        prev = kvp_ref[...] if sub == 0 else kvc_ref[r0 - WINDOW:r0, :]
        kv = jnp.concatenate([prev, kvc_ref[r0:r0 + WINDOW, :]], axis=0)
        k = kv[:, :KV_DIM]
        v = kv[:, KV_DIM:]
        valid = in_band & ((kj >= WINDOW) | (step > 0)) if sub == 0 else in_band
        for g0 in range(0, len(heads), ATT_GROUP):
            group = heads[g0:g0 + ATT_GROUP]
            sk_g = sinks[g0:g0 + ATT_GROUP]
            scores = []
            for p, half in group:
                qp = q_ref[r0:r0 + WINDOW, p * LANES:(p + 1) * LANES]
                qm = jnp.where(lo if half == 0 else jnp.logical_not(lo), qp, jnp.zeros_like(qp))
                scores.append(jnp.where(valid, _dot_nt(qm, k) + bias_ref[p + 3 * half], NEG_INF))
            tops = [jnp.maximum(jnp.max(s, axis=-1, keepdims=True), sk) for s, sk in zip(scores, sk_g)]
            probs = [jnp.exp(s - m) for s, m in zip(scores, tops)]
            dens = [jnp.sum(pr, axis=-1, keepdims=True) + jnp.exp(sk - m)
                    for pr, sk, m in zip(probs, sk_g, tops)]
            outs = [_dot(pr.astype(BF16), v) / den for pr, den in zip(probs, dens)]
            for idx in range(0, len(group), 2):
                p = group[idx][0]
                o_ref[r0:r0 + WINDOW, p * LANES:(p + 1) * LANES] = (
                    jnp.where(lo, outs[idx], outs[idx + 1]).astype(BF16))


def _swa_attention(q, kv, bias, sinks, seq):
    t = q.shape[0]
    rows = ATT_BLOCKS * WINDOW
    nblk = seq // rows
    return pl.pallas_call(
        _attn_kernel,
        grid=(t // seq, nblk),
        in_specs=[
            pl.BlockSpec(memory_space=pltpu.SMEM),
            pl.BlockSpec((rows, ATT_DIM), lambda b, n: (b * nblk + n, 0)),
            pl.BlockSpec((rows, 2 * KV_DIM), lambda b, n: (b * nblk + n, 0)),
            pl.BlockSpec((WINDOW, 2 * KV_DIM),
                         lambda b, n: (jnp.maximum((b * nblk + n) * ATT_BLOCKS - 1, 0), 0)),
            pl.BlockSpec((N_ATT_HEADS, WINDOW, 2 * WINDOW), lambda b, n: (0, 0, 0)),
        ],
        out_specs=pl.BlockSpec((rows, ATT_DIM), lambda b, n: (b * nblk + n, 0)),
        out_shape=jax.ShapeDtypeStruct((t, ATT_DIM), BF16),
        compiler_params=_cparams(("arbitrary", "arbitrary")),
        name="swa_attention",
    )(sinks, q, kv, kv, bias)


_GDN_BASE = SUBLANES
_GDN_LEVELS = int(np.log2(GDN_CHUNK // _GDN_BASE))


def _gdn_masks():
    r = np.arange(GDN_SUPER)
    ri, ci = r[:, None], r[None, :]
    same_chunk = (ri // GDN_CHUNK) == (ci // GDN_CHUNK)
    incl = same_chunk & (ri >= ci)
    planes = [incl, ri == ci]
    base = ((ri // _GDN_BASE) == (ci // _GDN_BASE)) & (ri > ci)
    planes.append(base)
    for lvl in range(_GDN_LEVELS):
        small = _GDN_BASE << lvl
        planes.append(((ri // (2 * small)) == (ci // (2 * small))) & ((ri // small) != (ci // small)) & (ri > ci))
    bmask = np.stack(planes).astype(np.float32)
    bmask[2] = -bmask[2]
    negmask = np.where(incl, 0.0, -np.inf).astype(np.float32)
    return negmask, bmask


_GDN_NEGMASK, _GDN_BMASK = _gdn_masks()


def _gdn_kernel(x_ref, halo_ref, ba_ref, cw_ref, alog_ref, dtb_ref, nw_ref, negmask_ref, bmask_ref,
                y_ref, state_ref, xs_ref):
    sc_id = pl.program_id(1)
    rows = GDN_SUPER
    nchunk = rows // GDN_CHUNK
    c_sz = GDN_CHUNK

    @pl.when(sc_id == 0)
    def _():
        state_ref[...] = jnp.zeros_like(state_ref)

    xs_ref[:SUBLANES, :] = jnp.where(sc_id == 0, 0.0, halo_ref[...])
    xs_ref[SUBLANES:, :] = x_ref[...]
    act = []
    for g in range(x_ref.shape[1] // LANES):
        cols = slice(g * LANES, (g + 1) * LANES)
        acc = x_ref[:, cols] * cw_ref[CONV_WIDTH - 1:CONV_WIDTH, cols]
        if g % 4 != 3:
            for s in range(1, CONV_WIDTH):
                acc = acc + (xs_ref[SUBLANES - s:SUBLANES - s + rows, cols]
                             * cw_ref[CONV_WIDTH - 1 - s:CONV_WIDTH - s, cols])
        act.append(acc * _sigmoid(acc))

    negmask = negmask_ref[...]
    tri_incl = bmask_ref[0]
    eye_b = bmask_ref[1]
    base_neg = bmask_ref[2]
    bands = [bmask_ref[3 + lvl] for lvl in range(_GDN_LEVELS)]
    li = lax.broadcasted_iota(jnp.int32, (LANES, LANES), 0)
    lj = lax.broadcasted_iota(jnp.int32, (LANES, LANES), 1)
    half_ones = jnp.where((li // HEAD_DIM) == (lj // HEAD_DIM), 1.0, 0.0).astype(BF16)
    lane_lo = lax.broadcasted_iota(jnp.int32, (rows, LANES), 1) < HEAD_DIM

    ba = ba_ref[...]
    beta_all = _sigmoid(ba)
    sp_in = ba + dtb_ref[...]
    softplus = jnp.maximum(sp_in, 0.0) + jnp.log(1.0 + jnp.exp(-jnp.abs(sp_in)))
    g_all = -jnp.exp(alog_ref[...]) * softplus
    gcum = _dot_hi_exact_rhs_lhs(tri_incl, g_all)
    gcum_t = gcum.T

    heads = range(N_GDN_HEADS)
    lane_hi = jnp.logical_not(lane_lo)
    mk = [lane_lo if h % 2 == 0 else lane_hi for h in heads]
    scale = HEAD_DIM ** -0.5

    xk, xq, gn, gc_col, beta, eg = [], [], [], [], [], []
    for h in heads:
        g = act[4 * (h // 2) + (h % 2)]
        g = g * lax.rsqrt(_dot((g * g).astype(BF16), half_ones) + NORM_EPS)
        gn.append(g)
        xk.append(jnp.where(mk[h], g, 0.0))
        xq.append(jnp.where(mk[h], pltpu.roll(g, HEAD_DIM, axis=1), 0.0) * scale)
        beta.append(beta_all[:, h:h + 1])
        gc_col.append(gcum[:, N_GDN_HEADS + h:N_GDN_HEADS + h + 1])
        eg.append(jnp.exp(gc_col[h]))

    l_b, attn, rhs = [], [], []
    for h in heads:
        gc_row = gcum_t[N_GDN_HEADS + h:N_GDN_HEADS + h + 1, :]
        decay = jnp.exp(gc_col[h] - gc_row + negmask)
        xk_b = xk[h].astype(BF16)
        kk = _dot_nt((xk[h] * beta[h]).astype(BF16), xk_b)
        l_b.append((kk * decay).astype(BF16))
        attn.append((_dot_nt(xq[h].astype(BF16), xk_b) * decay).astype(BF16))
        vv = act[4 * (h // 2) + 2]
        rhs.append(jnp.where(mk[h], gn[h] * eg[h], vv) * beta[h])

    a1 = [l_b[h] * base_neg for h in heads]
    a2 = [_dot(a1[h], a1[h]).astype(BF16) for h in heads]
    a4 = [_dot(a2[h], a2[h]).astype(BF16) for h in heads]
    inv0 = [eye_b + a1[h] for h in heads]
    acc1 = [inv0[h].astype(F32) + _dot(a2[h], inv0[h]) for h in heads]
    inv_b = [(acc1[h] + _dot(a4[h], acc1[h].astype(BF16))).astype(BF16) for h in heads]
    for lvl in range(_GDN_LEVELS - 1):
        mid = [_dot(l_b[h] * bands[lvl], inv_b[h]).astype(BF16) for h in heads]
        inv_b = [inv_b[h] - _dot(inv_b[h], mid[h]).astype(BF16) for h in heads]
    half = [_dot(inv_b[h], rhs[h].astype(BF16)) for h in heads]
    mid = [_dot(l_b[h] * bands[_GDN_LEVELS - 1], half[h].astype(BF16)) for h in heads]
    sol = [half[h] - _dot(inv_b[h], mid[h].astype(BF16)) for h in heads]

    lane_lo_s = lax.broadcasted_iota(jnp.int32, (LANES, LANES), 1) < HEAD_DIM
    mk_s = [lane_lo_s if h % 2 == 0 else jnp.logical_not(lane_lo_s) for h in heads]
    sol_b = [sol[h].astype(BF16) for h in heads]
    attn_sol = [_dot(attn[h], sol_b[h]) for h in heads]
    q_eff = [(xq[h] * eg[h] - jnp.where(mk[h], attn_sol[h], 0.0)).astype(BF16) for h in heads]
    o_free = [jnp.where(mk[h], 0.0, attn_sol[h]) for h in heads]
    kw = [[] for _ in heads]
    ku = [[] for _ in heads]
    cdec = [[] for _ in heads]
    for c in range(nchunk):
        r0 = c * c_sz
        for h in heads:
            glast = gcum[r0 + c_sz - 1:r0 + c_sz, N_GDN_HEADS + h:N_GDN_HEADS + h + 1]
            kd_t = (xk[h][r0:r0 + c_sz] * jnp.exp(glast - gc_col[h][r0:r0 + c_sz])).T
            both = _dot(kd_t.astype(BF16), sol_b[h][r0:r0 + c_sz])
            kw[h].append(jnp.where(mk_s[h], both, 0.0).astype(BF16))
            ku[h].append(jnp.where(mk_s[h], 0.0, both))
            cdec[h].append(jnp.exp(glast))
    st = [state_ref[h] for h in heads]
    o_parts = [[] for _ in heads]
    for c in range(nchunk):
        r0 = c * c_sz
        for h in heads:
            lhs = jnp.concatenate([kw[h][c], q_eff[h][r0:r0 + c_sz]], axis=0)
            prod = _dot(lhs, st[h].astype(BF16))
            o_parts[h].append(prod[LANES:] + o_free[h][r0:r0 + c_sz])
            st[h] = st[h] * cdec[h][c] + ku[h][c] - prod[:LANES]
    for h in heads:
        state_ref[h] = st[h]

    for p in range(N_GDN_HEADS // 2):
        o_pair = [jnp.concatenate(o_parts[h], axis=0) for h in (2 * p, 2 * p + 1)]
        o = jnp.where(lane_lo, o_pair[1], o_pair[0])
        ms = _dot((o * o).astype(BF16), half_ones) * (1.0 / HEAD_DIM)
        zz = act[4 * p + 3]
        y = o * lax.rsqrt(ms + NORM_EPS) * nw_ref[...] * zz
        y_ref[:, p * LANES:(p + 1) * LANES] = y.astype(BF16)


def _dot_hi_exact_rhs_lhs(m_bf16, x):
    hi, lo = _split_bf16(x)
    return _dot(m_bf16, hi) + _dot(m_bf16, lo)


def _gdn_mixer(gdn, ba, conv_p, alog_v, dtb_v, nw_v, seq):
    t, c = gdn.shape
    rows = GDN_SUPER
    nsc = seq // rows
    hb = rows // SUBLANES
    return pl.pallas_call(
        _gdn_kernel,
        grid=(t // seq, nsc),
        in_specs=[
            pl.BlockSpec((rows, c), lambda b, s: (b * nsc + s, 0)),
            pl.BlockSpec((SUBLANES, c), lambda b, s: (jnp.maximum((b * nsc + s) * hb - 1, 0), 0)),
            pl.BlockSpec((rows, LANES), lambda b, s: (b * nsc + s, 0)),
            pl.BlockSpec((CONV_WIDTH, c), lambda b, s: (0, 0)),
            pl.BlockSpec((1, LANES), lambda b, s: (0, 0)),
            pl.BlockSpec((1, LANES), lambda b, s: (0, 0)),
            pl.BlockSpec((1, LANES), lambda b, s: (0, 0)),
            pl.BlockSpec((rows, rows), lambda b, s: (0, 0)),
            pl.BlockSpec((3 + _GDN_LEVELS, rows, rows), lambda b, s: (0, 0, 0)),
        ],
        out_specs=pl.BlockSpec((rows, GDN_DIM), lambda b, s: (b * nsc + s, 0)),
        out_shape=jax.ShapeDtypeStruct((t, GDN_DIM), BF16),
        scratch_shapes=[pltpu.VMEM((N_GDN_HEADS, LANES, LANES), F32), pltpu.VMEM((rows + SUBLANES, c), F32)],
        compiler_params=_cparams(("arbitrary", "arbitrary")),
        name="gdn_mixer",
    )(gdn, gdn, ba, conv_p, alog_v, dtb_v, nw_v, jnp.asarray(_GDN_NEGMASK), jnp.asarray(_GDN_BMASK, BF16))


def _route_tile(logits, before, carry_ref, live):
    shape = logits.shape
    lane = lax.broadcasted_iota(jnp.int32, shape, 1).astype(F32)
    work = logits
    vals, idxs = [], []
    for _k in range(TOP_K):
        m = jnp.max(work, axis=-1, keepdims=True)
        idx = jnp.min(jnp.where(work == m, lane, float(LANES)), axis=-1, keepdims=True)
        vals.append(m)
        idxs.append(idx)
        work = jnp.where(lane == idx, -jnp.inf, work)
    exps = [jnp.exp(v - vals[0]) for v in vals]
    den = exps[0] + exps[1] + exps[2] + exps[3]
    onehots = [lane == idx for idx in idxs]
    member = jnp.zeros(shape, F32)
    for oh in onehots:
        member = member + jnp.where(oh, 1.0, 0.0)
    rank = _dot(before, member.astype(BF16)) + carry_ref[...]
    carry_ref[...] = carry_ref[...] + live * jnp.sum(member, axis=0, keepdims=True)
    info = jnp.zeros(shape, F32)
    for k in range(TOP_K):
        rank_k = jnp.sum(jnp.where(onehots[k], rank, 0.0), axis=-1, keepdims=True)
        info = jnp.where(lane == float(k), idxs[k], info)
        info = jnp.where(lane == float(TOP_K + k), rank_k, info)
        info = jnp.where(lane == float(2 * TOP_K + k), exps[k] / den, info)
    return info


def _outproj_kernel(x_ref, mod_ref, yp_ref, ya_ref, yg_ref, wp_ref, wa_ref, wg_ref, lng_ref, lnb_ref,
                    rw_ref, rb_ref, before_ref, x1_ref, h2_ref, info_ref, er_ref, cnt_ref,
                    carry_ref, logit_s):
    step = pl.program_id(0)

    @pl.when(step == 0)
    def _():
        carry_ref[...] = jnp.zeros_like(carry_ref)
        logit_s[...] = jnp.zeros_like(logit_s)

    part = x_ref.shape[0] // OUT_SPLIT
    halves = [slice(j * part, (j + 1) * part) for j in range(OUT_SPLIT)]
    y = [_dot(yp_ref[r, :], wp_ref[...]) + _dot(ya_ref[r, :], wa_ref[...]) + _dot(yg_ref[r, :], wg_ref[...])
         for r in halves]
    live = jnp.where(step > 0, 1.0, 0.0)
    info = _route_tile(logit_s[...], before_ref[...], carry_ref, live)
    info_ref[...] = info
    er_ref[...] = info.T[:SUBLANES]
    cnt_ref[...] = carry_ref[...]
    g1 = mod_ref[0, 2:3, :]
    sh2 = mod_ref[0, 3:4, :]
    sc2 = mod_ref[0, 4:5, :]
    for r, y_r in zip(halves, y):
        x1 = _layer_norm(DEEPNORM_ALPHA * x_ref[r, :] + g1 * y_r, lng_ref[...], lnb_ref[...])
        x1_ref[r, :] = x1
        h2 = x1 * (1.0 + sc2) + sh2
        h2_ref[r, :] = _pack_bf16_pairs(h2)
        logit_s[r, :] = _dot(h2.astype(BF16), rw_ref[...]) + rb_ref[...]


def _out_projection(x2d, mod, yp, ya, yg, w_out_p, ln_g, ln_b, rw, rb, seq):
    t, d = x2d.shape
    tm = ROW_TILE
    wp = w_out_p[:POOL_DIM]
    wa = w_out_p[POOL_DIM:POOL_DIM + ATT_DIM]
    wg = w_out_p[POOL_DIM + ATT_DIM:]
    last = t // tm - 1
    row = lambda i: (jnp.minimum(i, last), 0)
    routed = lambda i: (jnp.maximum(i - 1, 0), 0)
    fixed = lambda i: (0, 0)
    return pl.pallas_call(
        _outproj_kernel,
        grid=(t // tm + 1,),
        in_specs=[
            pl.BlockSpec((tm, d), row),
            pl.BlockSpec((1, 6, d), lambda i: ((jnp.minimum(i, last) * tm) // seq, 0, 0)),
            pl.BlockSpec((tm, POOL_DIM), row),
            pl.BlockSpec((tm, ATT_DIM), row),
            pl.BlockSpec((tm, GDN_DIM), row),
            pl.BlockSpec((POOL_DIM, d), fixed),
            pl.BlockSpec((ATT_DIM, d), fixed),
            pl.BlockSpec((GDN_DIM, d), fixed),
            pl.BlockSpec((1, d), fixed),
            pl.BlockSpec((1, d), fixed),
            pl.BlockSpec((d, LANES), fixed),
            pl.BlockSpec((1, LANES), fixed),
            pl.BlockSpec((tm, tm), fixed),
        ],
        out_specs=[pl.BlockSpec((tm, d), row), pl.BlockSpec((tm, d // 2), row), pl.BlockSpec((tm, LANES), routed),
                   pl.BlockSpec((SUBLANES, tm), lambda i: (0, jnp.maximum(i - 1, 0))),
                   pl.BlockSpec((1, LANES), fixed)],
        out_shape=[jax.ShapeDtypeStruct((t, d), F32), jax.ShapeDtypeStruct((t, d // 2), jnp.int32),
                   jax.ShapeDtypeStruct((t, LANES), F32), jax.ShapeDtypeStruct((SUBLANES, t), F32),
                   jax.ShapeDtypeStruct((1, LANES), F32)],
        scratch_shapes=[pltpu.VMEM((1, LANES), F32), pltpu.VMEM((tm, LANES), F32)],
        compiler_params=_cparams(("arbitrary",)),
        name="out_proj_ln_route",
    )(x2d, mod, yp, ya, yg, wp, wa, wg, ln_g.reshape(1, d), ln_b.reshape(1, d), rw, rb,
      jnp.tril(jnp.ones((tm, tm), BF16), -1))


def _slot_kernel(er_ref, cnt_ref, dest_ref, pcum_ref):
    cnt = jnp.broadcast_to(cnt_ref[...], (SUBLANES, LANES))
    padded = jnp.floor((cnt + float(EXPERT_BLOCK - 1)) * (1.0 / EXPERT_BLOCK)) * float(EXPERT_BLOCK)
    lane8 = lax.broadcasted_iota(jnp.int32, (SUBLANES, LANES), 1)
    acc = padded
    step = 1
    while step < LANES:
        acc = acc + jnp.where(lane8 >= step, pltpu.roll(acc, step, axis=1), 0.0)
        step *= 2
    pcum_ref[...] = acc[:1].astype(jnp.int32)
    pstart = acc - padded

    er = er_ref[...]
    start = jnp.zeros(er.shape, F32)
    for e in range(N_EXPERTS):
        offset = jnp.sum(jnp.where(lane8 == e, pstart, 0.0), axis=-1, keepdims=True)
        start = jnp.where(er == float(e), offset, start)
    row = lax.broadcasted_iota(jnp.int32, er.shape, 0)
    slots = jnp.where(row < TOP_K, start + pltpu.roll(er, TOP_K, axis=0), 0.0)
    dest_ref[...] = slots.astype(jnp.int32)


def _slots(er, cnt):
    t = er.shape[1]
    return pl.pallas_call(
        _slot_kernel,
        grid=(1,),
        in_specs=[pl.BlockSpec((SUBLANES, t), lambda i: (0, 0)), pl.BlockSpec((1, LANES), lambda i: (0, 0))],
        out_specs=[pl.BlockSpec((SUBLANES, t), lambda i: (0, 0)), pl.BlockSpec((1, LANES), lambda i: (0, 0))],
        out_shape=[jax.ShapeDtypeStruct((SUBLANES, t), jnp.int32), jax.ShapeDtypeStruct((1, LANES), jnp.int32)],
        compiler_params=_cparams(("arbitrary",)),
        name="moe_slots",
    )(er, cnt)


def _expert_kernel(e0, be_ref, nxt_ref, val_ref, nu_ref, x_ref, wup_hbm, bup_ref, wdn_hbm, bdn_ref, y_ref,
                   wup_st, wdn_st, wup_bf, wdn_bf, sems):
    i = pl.program_id(0)
    e = be_ref[i]
    prev = be_ref[jnp.maximum(i - 1, 0)]
    used = i < nu_ref[0]

    def weight_copies(expert):
        return (pltpu.make_async_copy(wup_hbm.at[e0 + expert], wup_st, sems.at[0]),
                pltpu.make_async_copy(wdn_hbm.at[e0 + expert], wdn_st, sems.at[1]))

    @pl.when(i == 0)
    def _():
        for cp in weight_copies(e):
            cp.start()

    @pl.when(used & ((i == 0) | (e != prev)))
    def _():
        for cp in weight_copies(e):
            cp.wait()
        wup_bf[...] = wup_st[...].astype(BF16)
        wdn_bf[...] = wdn_st[...].astype(BF16)

        @pl.when(nxt_ref[i] >= 0)
        def _():
            for cp in weight_copies(nxt_ref[i]):
                cp.start()

    def ffn(rows):
        xb = _unpack_bf16_pairs(x_ref[:rows, :]).astype(BF16)
        hb = _dot(xb, wup_bf[...]) + bup_ref[0]
        x_glu = jnp.minimum(hb[:, :EXPERT_DIM], SWIGLU_LIMIT)
        x_lin = jnp.clip(hb[:, EXPERT_DIM:], -SWIGLU_LIMIT, SWIGLU_LIMIT)
        act = x_glu * _sigmoid(SWIGLU_ALPHA * x_glu) * (x_lin + 1.0)
        y = _dot(act.astype(BF16), wdn_bf[...]) + bdn_ref[0]
        y_ref[:rows, :] = _pack_bf16_pairs(y)

    bm = x_ref.shape[0]
    pieces = (val_ref[i] + (EXPERT_GRAIN - 1)) // EXPERT_GRAIN
    for n_piece in range(1, bm // EXPERT_GRAIN + 1):
        rows_used = n_piece * EXPERT_GRAIN

        @pl.when(used & (pieces == n_piece))
        def _(rows_used=rows_used):
            ffn(rows_used)
            if rows_used < bm:
                y_ref[rows_used:, :] = jnp.zeros((bm - rows_used, y_ref.shape[1]), y_ref.dtype)

    @pl.when(i >= nu_ref[0])
    def _():
        y_ref[...] = jnp.zeros_like(y_ref)


def _expert_ffn(xbuf, block_e, next_e, valid, n_used, w_up, b_up, w_down, b_down, layer):
    p, dh = xbuf.shape
    d = 2 * dh
    bm = EXPERT_BLOCK
    ne, _, n_up = w_up.shape
    e0 = layer * N_EXPERTS
    grid_spec = pltpu.PrefetchScalarGridSpec(
        num_scalar_prefetch=4,
        grid=(p // bm,),
        in_specs=[
            pl.BlockSpec((bm, dh), lambda i, be, nx, vl, nu: (i, 0)),
            pl.BlockSpec(memory_space=pl.ANY),
            pl.BlockSpec((1, 1, n_up), lambda i, be, nx, vl, nu: (e0 + be[i], 0, 0)),
            pl.BlockSpec(memory_space=pl.ANY),
            pl.BlockSpec((1, 1, d), lambda i, be, nx, vl, nu: (e0 + be[i], 0, 0)),
        ],
        out_specs=pl.BlockSpec((bm, dh), lambda i, be, nx, vl, nu: (i, 0)),
        scratch_shapes=[pltpu.VMEM((d, n_up), F32), pltpu.VMEM((EXPERT_DIM, d), F32),
                        pltpu.VMEM((d, n_up), BF16), pltpu.VMEM((EXPERT_DIM, d), BF16),
                        pltpu.SemaphoreType.DMA((2,))],
    )
    return pl.pallas_call(
        functools.partial(_expert_kernel, e0),
        grid_spec=grid_spec,
        out_shape=jax.ShapeDtypeStruct((p, dh), jnp.int32),
        compiler_params=_cparams(("arbitrary",)),
        name="expert_ffn",
    )(block_e, next_e, valid, n_used, xbuf, w_up, b_up, w_down, b_down)


def _combine_kernel(x1_ref, mod_ref, yg_ref, info_ref, lng_ref, lnb_ref, *rest):
    o_ref = rest[-1]
    info = info_ref[...]
    y = jnp.zeros(x1_ref.shape, F32)
    for k in range(TOP_K):
        gate = info[:, 2 * TOP_K + k:2 * TOP_K + k + 1]
        y = y + gate * _unpack_bf16_pairs(yg_ref[k])
    g2 = mod_ref[0, 5:6, :]
    o_ref[...] = _layer_norm(DEEPNORM_ALPHA * x1_ref[...] + g2 * y, lng_ref[...], lnb_ref[...])


def _combine(x1, mod, yg, info, ln_g, ln_b, seq, part, n_parts, earlier):
    t, d = x1.shape
    tm = min(IN_TILE, seq)
    steps = t // tm // n_parts
    off = part * steps
    row = lambda i: (i + off, 0)
    fixed = lambda i: (0, 0)
    in_specs = [
        pl.BlockSpec((tm, d), row),
        pl.BlockSpec((1, 6, d), lambda i: (((i + off) * tm) // seq, 0, 0)),
        pl.BlockSpec((TOP_K, tm, d // 2), lambda i: (0, i, 0)),
        pl.BlockSpec((tm, LANES), row),
        pl.BlockSpec((1, d), fixed),
        pl.BlockSpec((1, d), fixed),
    ]
    args = [x1, mod, yg, info, ln_g.reshape(1, d), ln_b.reshape(1, d)]
    aliases = {}
    if earlier is not None:
        in_specs.append(pl.BlockSpec(memory_space=pl.ANY))
        args.append(earlier)
        aliases = {len(args) - 1: 0}
    return pl.pallas_call(
        _combine_kernel,
        grid=(steps,),
        in_specs=in_specs,
        out_specs=pl.BlockSpec((tm, d), row),
        out_shape=jax.ShapeDtypeStruct((t, d), F32),
        input_output_aliases=aliases,
        compiler_params=_cparams(("arbitrary",)),
        name="moe_combine_ln",
    )(*args)


def _sc_workers():
    info = plsc.get_sparse_core_info()
    return info.num_cores, info.num_cores * info.num_subcores


def _sc_scatter_rows(rows, idx, n_out):
    t, w = rows.shape
    kk = idx.shape[0]
    n_cores, n_workers = _sc_workers()
    ch = SC_CHUNK
    assert t % (2 * n_workers * ch) == 0
    n_chunk = t // (n_workers * ch)
    idx_c = jnp.transpose(idx.reshape(kk, t // ch, ch), (1, 0, 2))

    @functools.partial(
        pl.kernel,
        mesh=plsc.VectorSubcoreMesh(core_axis_name="c", subcore_axis_name="s"),
        out_type=jax.ShapeDtypeStruct((n_out, w), rows.dtype),
        scratch_types=[pltpu.VMEM((2, kk, ch), jnp.int32), pltpu.VMEM((2, ch, w), rows.dtype),
                       pltpu.SemaphoreType.DMA((2,)), pltpu.SemaphoreType.DMA((2,))],
        name="sc_dispatch_scatter",
    )
    def scatter_kernel(rows_hbm, idx_hbm, out_hbm, idx_v, rows_v, load_sem, scat_sem):
        base = (lax.axis_index("s") * n_cores + lax.axis_index("c")) * n_chunk

        def load(j, b):
            return pltpu.make_async_copy(rows_hbm.at[pl.ds((base + j) * ch, ch)], rows_v.at[b], load_sem.at[b])

        def scatters(b):
            return [pltpu.make_async_copy(rows_v.at[b], out_hbm.at[idx_v.at[b, q]], scat_sem.at[b])
                    for q in range(kk)]

        pltpu.sync_copy(idx_hbm.at[base], idx_v.at[0])
        load(0, 0).start()

        @pl.loop(0, n_chunk, step=2)
        def _(j0):
            for b in range(2):
                j = j0 + b
                other = 1 - b

                @pl.when(j >= 1)
                def _():
                    for cp in scatters(other):
                        cp.wait()

                @pl.when(j + 1 < n_chunk)
                def _():
                    pltpu.sync_copy(idx_hbm.at[base + j + 1], idx_v.at[other])
                    load(j + 1, other).start()

                load(j, b).wait()
                for cp in scatters(b):
                    cp.start()

        for cp in scatters((n_chunk - 1) % 2):
            cp.wait()

    return scatter_kernel(rows, idx_c)


def _sc_gather_rows(table, idx):
    m = idx.shape[0]
    w = table.shape[1]
    n_cores, n_workers = _sc_workers()
    ch = SC_CHUNK
    assert m % (2 * n_workers * ch) == 0
    n_chunk = m // (n_workers * ch)
    idx_c = idx.reshape(m // ch, 1, ch)

    @functools.partial(
        pl.kernel,
        mesh=plsc.VectorSubcoreMesh(core_axis_name="c", subcore_axis_name="s"),
        out_type=jax.ShapeDtypeStruct((m, w), table.dtype),
        scratch_types=[pltpu.VMEM((2, 1, ch), jnp.int32), pltpu.VMEM((2, ch, w), table.dtype),
                       pltpu.SemaphoreType.DMA((2,)), pltpu.SemaphoreType.DMA((2,))],
        name="sc_combine_gather",
    )
    def gather_kernel(table_hbm, idx_hbm, out_hbm, idx_v, rows_v, gather_sem, write_sem):
        base = (lax.axis_index("s") * n_cores + lax.axis_index("c")) * n_chunk

        def gather(b):
            return pltpu.make_async_copy(table_hbm.at[idx_v.at[b, 0]], rows_v.at[b], gather_sem.at[b])

        def write(j, b):
            return pltpu.make_async_copy(rows_v.at[b], out_hbm.at[pl.ds((base + j) * ch, ch)], write_sem.at[b])

        pltpu.sync_copy(idx_hbm.at[base], idx_v.at[0])
        gather(0).start()

        @pl.loop(0, n_chunk, step=2)
        def _(j0):
            for b in range(2):
                j = j0 + b
                other = 1 - b

                @pl.when(j >= 1)
                def _():
                    write(j - 1, other).wait()

                @pl.when(j + 1 < n_chunk)
                def _():
                    pltpu.sync_copy(idx_hbm.at[base + j + 1], idx_v.at[other])
                    gather(other).start()

                gather(b).wait()
                write(j, b).start()

        write(n_chunk - 1, (n_chunk - 1) % 2).wait()

    return gather_kernel(table, idx_c)


def _lane_vector(vals, offset):
    return jnp.zeros((1, LANES), F32).at[0, offset:offset + vals.shape[0]].set(vals.astype(F32))


def _moe(h2, info, er, cnt, x1, mod, ln_g, ln_b, w_up, b_up, w_down, b_down, layer, seq):
    t, dh = h2.shape
    a = t * TOP_K
    bm = EXPERT_BLOCK
    slots, pcum_v = _slots(er, cnt)
    pcum = pcum_v[0, :N_EXPERTS]
    dest = slots[:TOP_K]
    n_blocks = -(-a // bm) + N_EXPERTS
    starts = jnp.arange(n_blocks, dtype=jnp.int32) * bm
    block_e = jnp.minimum(jnp.sum(pcum[None, :] <= starts[:, None], axis=1), N_EXPERTS - 1).astype(jnp.int32)
    n_used = (pcum[-1] // bm).astype(jnp.int32).reshape(1)
    later = block_e[None, :] > block_e[:, None]
    group_end = n_blocks - jnp.sum(later, axis=1)
    next_e = jnp.min(jnp.where(later, block_e[None, :], N_EXPERTS), axis=1)
    next_e = jnp.where(group_end < n_used[0], next_e, -1).astype(jnp.int32)
    counts = cnt[0, :N_EXPERTS].astype(jnp.int32)
    pstart = pcum - ((counts + bm - 1) // bm) * bm
    mine = block_e[:, None] == jnp.arange(N_EXPERTS, dtype=jnp.int32)[None, :]
    count_b = jnp.sum(jnp.where(mine, counts[None, :], 0), axis=1)
    pstart_b = jnp.sum(jnp.where(mine, pstart[None, :], 0), axis=1)
    valid = jnp.clip(count_b - (starts - pstart_b), 0, bm).astype(jnp.int32)
    xbuf = _sc_scatter_rows(h2, dest, n_blocks * bm)
    ybuf = _expert_ffn(xbuf, block_e, next_e, valid, n_used, w_up, b_up, w_down, b_down, layer)
    out = None
    tp = t // COMBINE_PARTS
    for part in range(COMBINE_PARTS):
        idx = dest[:, part * tp:(part + 1) * tp].reshape(TOP_K * tp)
        yg = _sc_gather_rows(ybuf, idx).reshape(TOP_K, tp, dh)
        out = _combine(x1, mod, yg, info, ln_g, ln_b, seq, part, COMBINE_PARTS, out)
    return out


def kernel(x, c, rel_bias, w_in, w_out, w_ada, b_ada, ln1_g, ln1_b, ln2_g, ln2_b, pool_w, pool_scale,
           attn_sinks, conv_w, gdn_a_log, gdn_dt_bias, gdn_norm_w, router_w, router_b,
           exp_w_up, exp_b_up, exp_w_down, exp_b_down):
    bsz, seq, d = x.shape
    depth = w_in.shape[0]
    t = bsz * seq
    assert d == D_MODEL and w_in.shape[2] == IN_DIM and depth == DEPTH
    assert seq % GDN_SUPER == 0 and seq % (ATT_BLOCKS * WINDOW) == 0
    assert t % ROW_TILE == 0 and seq % ROW_TILE == 0

    mod_all = _modulation(c, w_ada, b_ada).reshape(depth, bsz, 6, d)
    bias = _band_bias(rel_bias)

    w_up_all = exp_w_up.reshape((depth * N_EXPERTS,) + exp_w_up.shape[2:])
    b_up_all = exp_b_up.reshape(depth * N_EXPERTS, 1, exp_b_up.shape[2])
    w_down_all = exp_w_down.reshape((depth * N_EXPERTS,) + exp_w_down.shape[2:])
    b_down_all = exp_b_down.reshape(depth * N_EXPERTS, 1, exp_b_down.shape[2])

    x2d = x.reshape(t, d)
    for l in range(depth):
        mod = mod_all[l]
        w_out_p = _take_static(w_out[l], _OUT_PERM, 0).astype(BF16)
        ident = jnp.zeros((CONV_WIDTH, 1), F32).at[CONV_WIDTH - 1, 0].set(1.0)
        conv_p = jnp.where(jnp.asarray(_GDN_CONV_SRC >= 0), _take_cols(conv_w[l].astype(F32), _GDN_CONV_SRC),
                           ident)
        pool_bd = jnp.zeros((POOL_DIM, POOL_DIM), F32)
        for gi in range(len(POOL_WINDOWS)):
            sl = slice(gi * POOL_GROUP, (gi + 1) * POOL_GROUP)
            pool_bd = pool_bd.at[sl, sl].set(pool_w[l, gi].astype(F32))
        alog_v = _lane_vector(gdn_a_log[l], N_GDN_HEADS)
        dtb_v = _lane_vector(gdn_dt_bias[l], N_GDN_HEADS)
        nw_v = jnp.tile(gdn_norm_w[l].astype(F32), 2).reshape(1, LANES)
        rw = jnp.zeros((d, LANES), BF16).at[:, :N_EXPERTS].set(router_w[l].astype(BF16))
        rb = jnp.full((1, LANES), NEG_INF, F32).at[0, :N_EXPERTS].set(router_b[l].astype(F32))

        y_pool, aq, akv, gdn, ba = _in_projection(x2d, mod, w_in.astype(F32), pool_bd.astype(BF16),
                                                  pool_scale[l].astype(F32), l, seq)
        y_att = _swa_attention(aq, akv, bias, attn_sinks[l].astype(F32), seq)
        y_gdn = _gdn_mixer(gdn, ba, conv_p, alog_v, dtb_v, nw_v, seq)
        x1, h2, info, er, cnt = _out_projection(x2d, mod, y_pool, y_att, y_gdn, w_out_p, ln1_g[l], ln1_b[l],
                                            rw, rb, seq)
        x2d = _moe(h2, info, er, cnt, x1, mod, ln2_g[l], ln2_b[l], w_up_all, b_up_all, w_down_all, b_down_all,
                   l, seq)
    return x2d.reshape(bsz, seq, d)
```

```python
import functools

import numpy as np
import jax
import jax.numpy as jnp
from jax import lax
from jax.experimental import pallas as pl
from jax.experimental.pallas import tpu as pltpu
from jax.experimental.pallas import tpu_sc as plsc

F32 = jnp.float32
BF16 = jnp.bfloat16

D_MODEL = 1024
HEAD_DIM = 64
POOL_DIM = 256
POOL_WINDOWS = (2, 4, 8, 16)
POOL_GROUP = 64
POOL_HALO = 16
N_ATT_HEADS = 6
N_KV_HEADS = 2
ATT_DIM = 384
KV_DIM = 128
WINDOW = 128
N_BUCKETS = 32
MAX_DISTANCE = 128
N_GDN_HEADS = 6
GDN_DIM = 384
CONV_WIDTH = 4
GDN_CHUNK = 64
N_EXPERTS = 32
TOP_K = 4
EXPERT_DIM = 1024
SWIGLU_ALPHA = 1.702
SWIGLU_LIMIT = 7.0
DEPTH = 2
DEEPNORM_ALPHA = (2 * DEPTH) ** 0.25
LN_EPS = 1e-5
NORM_EPS = 1e-6
NEG_INF = -1e30

LANES = 128
SUBLANES = 8
VMEM_LIMIT = 56 * 1024 * 1024

MOD_TILE = 2048
ROW_TILE = 512
OUT_SPLIT = 2
IN_TILE = 1024
ATT_BLOCKS = 8
ATT_GROUP = 6
GDN_SUPER = 256
EXPERT_BLOCK = 1024
EXPERT_GRAIN = 256
SC_CHUNK = 64

_OFF_AQ = POOL_DIM
_OFF_AK = _OFF_AQ + ATT_DIM
_OFF_AV = _OFF_AK + KV_DIM
_OFF_GQ = _OFF_AV + KV_DIM
_OFF_GK = _OFF_GQ + GDN_DIM
_OFF_GV = _OFF_GK + GDN_DIM
_OFF_GZ = _OFF_GV + GDN_DIM
_OFF_GB = _OFF_GZ + GDN_DIM
_OFF_GA = _OFF_GB + N_GDN_HEADS
IN_DIM = _OFF_GA + N_GDN_HEADS

P_POOL = (0, POOL_DIM)
P_Q = (P_POOL[1], P_POOL[1] + ATT_DIM)
P_KV = (P_Q[1], P_Q[1] + 2 * KV_DIM)
P_GDN = (P_KV[1], P_KV[1] + 4 * GDN_DIM)
P_BA = (P_GDN[1], P_GDN[1] + LANES)
P_TOTAL = P_BA[1]


def _head_cols(off, h):
    return list(range(off + HEAD_DIM * h, off + HEAD_DIM * (h + 1)))


def _build_in_perm():
    cols = list(range(POOL_DIM))
    for p in range(N_ATT_HEADS // 2):
        cols += _head_cols(_OFF_AQ, p) + _head_cols(_OFF_AQ, p + 3)
    cols += list(range(_OFF_AK, _OFF_AK + 2 * KV_DIM))
    gdn_src = []
    for p in range(N_GDN_HEADS // 2):
        e, o = 2 * p, 2 * p + 1
        grp = (_head_cols(_OFF_GK, e) + _head_cols(_OFF_GQ, e)
               + _head_cols(_OFF_GQ, o) + _head_cols(_OFF_GK, o)
               + _head_cols(_OFF_GV, o) + _head_cols(_OFF_GV, e)
               + _head_cols(_OFF_GZ, o) + _head_cols(_OFF_GZ, e))
        cols += grp
        gdn_src += [c - _OFF_GQ if c < _OFF_GZ else -1 for c in grp]
    cols += list(range(_OFF_GB, _OFF_GB + 2 * N_GDN_HEADS))
    cols += [-1] * (LANES - 2 * N_GDN_HEADS)
    assert len(cols) == P_TOTAL
    return np.asarray(cols, np.int32), np.asarray(gdn_src, np.int32)


_IN_PERM, _GDN_CONV_SRC = _build_in_perm()


def _build_out_perm():
    rows = list(range(POOL_DIM))
    for p in range(N_ATT_HEADS // 2):
        rows += _head_cols(POOL_DIM, p) + _head_cols(POOL_DIM, p + 3)
    for p in range(N_GDN_HEADS // 2):
        rows += _head_cols(POOL_DIM + ATT_DIM, 2 * p + 1) + _head_cols(POOL_DIM + ATT_DIM, 2 * p)
    return np.asarray(rows, np.int32)


_OUT_PERM = _build_out_perm()


def _t5_bucket_line():
    n = np.maximum(2 * WINDOW - 1 - np.arange(3 * WINDOW - 1), 0)
    max_exact = N_BUCKETS // 2
    nf = np.maximum(n, 1).astype(np.float32)
    large = max_exact + (np.log(nf / max_exact) / np.float32(np.log(MAX_DISTANCE / max_exact))
                         * (N_BUCKETS - max_exact)).astype(np.int32)
    large = np.minimum(large, N_BUCKETS - 1)
    return np.where(n < max_exact, n, large).astype(np.int32)


_BUCKET_LINE = _t5_bucket_line()


def _band_bias(rel_bias):
    n_line = 3 * WINDOW - 1
    line = jnp.take(rel_bias.astype(F32), jnp.asarray(_BUCKET_LINE), axis=0).T
    heads = line.shape[0]
    padded = jnp.concatenate([line, jnp.zeros((heads, 1), F32)], axis=1)
    skew = jnp.tile(padded, (1, WINDOW))[:, :WINDOW * n_line].reshape(heads, WINDOW, n_line)
    return skew[:, :, WINDOW - 1:3 * WINDOW - 1]


def _take_static(w, perm, axis):
    parts = []
    start = 0
    for i in range(1, len(perm) + 1):
        run_ends = (i == len(perm) or ((perm[i] < 0) != (perm[i - 1] < 0))
                    or (perm[i] >= 0 and perm[i] != perm[i - 1] + 1))
        if run_ends:
            if perm[start] < 0:
                shape = list(w.shape)
                shape[axis] = i - start
                parts.append(jnp.zeros(shape, w.dtype))
            else:
                parts.append(lax.slice_in_dim(w, int(perm[start]), int(perm[start]) + (i - start), axis=axis))
            start = i
    return jnp.concatenate(parts, axis=axis)


def _take_cols(w, perm):
    return _take_static(w, perm, w.ndim - 1)


def _split_bf16(x):
    hi = x.astype(BF16)
    lo = (x - hi.astype(F32)).astype(BF16)
    return hi, lo


def _pack_bf16_pairs(x):
    n = x.shape[1] // 2
    bits = pltpu.bitcast(x.astype(BF16).astype(F32), jnp.int32)
    return lax.shift_right_logical(bits[:, :n], 16) | bits[:, n:]


def _unpack_bf16_pairs(u):
    lo = pltpu.bitcast(lax.shift_left(u, 16), F32)
    hi = pltpu.bitcast(u & jnp.int32(-65536), F32)
    return jnp.concatenate([lo, hi], axis=1)


def _dot(a, b):
    return jnp.dot(a, b, preferred_element_type=F32)


def _dot_nt(a, b):
    return lax.dot_general(a, b, (((1,), (1,)), ((), ())), preferred_element_type=F32)


def _sigmoid(x):
    return 1.0 / (1.0 + jnp.exp(-x))


def _layer_norm(r, g, b):
    mu = jnp.mean(r, axis=-1, keepdims=True)
    d = r - mu
    var = jnp.mean(d * d, axis=-1, keepdims=True)
    return d * lax.rsqrt(var + LN_EPS) * g + b


def _cparams(sem):
    return pltpu.CompilerParams(dimension_semantics=sem, vmem_limit_bytes=VMEM_LIMIT)


def _mod_kernel(c_ref, w_ref, b_ref, o_ref):
    c = c_ref[...]
    ca = c * _sigmoid(c)
    o_ref[0] = _dot(ca.astype(BF16), w_ref[0].astype(BF16)) + b_ref[0]


def _modulation(c, w_ada, b_ada):
    depth, d, n = w_ada.shape
    bsz = c.shape[0]
    tn = MOD_TILE
    return pl.pallas_call(
        _mod_kernel,
        grid=(depth, n // tn),
        in_specs=[
            pl.BlockSpec((bsz, d), lambda l, j: (0, 0)),
            pl.BlockSpec((1, d, tn), lambda l, j: (l, 0, j)),
            pl.BlockSpec((1, 1, tn), lambda l, j: (l, 0, j)),
        ],
        out_specs=pl.BlockSpec((1, bsz, tn), lambda l, j: (l, 0, j)),
        out_shape=jax.ShapeDtypeStruct((depth, bsz, n), F32),
        compiler_params=_cparams(("arbitrary", "arbitrary")),
        name="adaln_mod",
    )(c, w_ada, b_ada.reshape(depth, 1, n))


def _perm_runs(perm):
    runs = []
    start = 0
    for i in range(1, len(perm) + 1):
        run_ends = (i == len(perm) or ((perm[i] < 0) != (perm[i - 1] < 0))
                    or (perm[i] >= 0 and perm[i] != perm[i - 1] + 1))
        if run_ends:
            runs.append((int(perm[start]) if perm[start] >= 0 else -1, i - start, start))
            start = i
    return runs


_IN_RUNS = _perm_runs(_IN_PERM)


def _pool_tile(ext, u, pos0, w_bd, scale):
    row = lax.broadcasted_iota(jnp.int32, ext.shape, 0)
    lane = lax.broadcasted_iota(jnp.int32, u.shape, 1)

    def shifted(a, s):
        return jnp.where(row >= s, pltpu.roll(a, s, axis=0), 0.0)

    sums = []
    acc = ext
    for wdt in POOL_WINDOWS:
        acc = acc + shifted(acc, wdt // 2)
        sums.append(acc[POOL_HALO:])
    grp = lane // POOL_GROUP
    wsum = sums[-1]
    win = jnp.full(u.shape, POOL_WINDOWS[-1], jnp.int32)
    for gi in range(len(POOL_WINDOWS) - 2, -1, -1):
        wsum = jnp.where(grp == gi, sums[gi], wsum)
        win = jnp.where(grp == gi, POOL_WINDOWS[gi], win)
    pos = pos0 + lax.broadcasted_iota(jnp.int32, u.shape, 0)
    cnt = jnp.minimum(pos + 1, win).astype(F32)
    p = wsum / cnt - u
    return (_dot(p.astype(BF16), w_bd) * scale).astype(BF16)


def _inproj_kernel(layer, seq, x_ref, mod_ref, wt_hbm, pw_ref, ps_ref, pool_ref, q_ref, kv_ref, gdn_ref, ba_ref,
                   wt_f32, tail_ref, wt_ref, halo_ref, sem):
    @pl.when(pl.program_id(0) == 0)
    def _():
        n_real = wt_hbm.shape[0]
        pad0 = (n_real // SUBLANES) * SUBLANES
        bulk = pltpu.make_async_copy(wt_hbm.at[pl.ds(0, pad0), layer, :], wt_f32.at[pl.ds(0, pad0)], sem.at[0])
        tail = pltpu.make_async_copy(wt_hbm.at[pl.ds(n_real - SUBLANES, SUBLANES), layer, :], tail_ref, sem.at[1])
        bulk.start()
        tail.start()
        bulk.wait()
        tail.wait()
        row8 = lax.broadcasted_iota(jnp.int32, tail_ref.shape, 0)
        left = n_real - pad0
        wt_f32[pad0:pad0 + SUBLANES, :] = jnp.where(row8 < left, pltpu.roll(tail_ref[...], left, axis=0), 0.0)
        wt_f32[pad0 + SUBLANES:, :] = jnp.zeros((wt_f32.shape[0] - pad0 - SUBLANES, wt_f32.shape[1]), F32)
        for src, n, dst in _IN_RUNS:
            if src >= 0:
                rows = n if src + n < n_real else wt_ref.shape[0] - dst
                wt_ref[dst:dst + rows, :] = wt_f32[src:src + rows, :].astype(BF16)
        q_rows = wt_ref[P_Q[0]:P_Q[1], :].astype(F32) * (HEAD_DIM ** -0.5)
        wt_ref[P_Q[0]:P_Q[1], :] = q_rows.astype(BF16)

    sh = mod_ref[0, 0:1, :]
    sc = mod_ref[0, 1:2, :]
    h = (x_ref[...] * (1.0 + sc) + sh).astype(BF16)

    def mm(rng):
        return _dot_nt(h, wt_ref[rng[0]:rng[1], :])

    tm = x_ref.shape[0]
    pos0 = (pl.program_id(0) * tm) % seq
    u = mm(P_POOL)
    ext = jnp.concatenate([jnp.where(pos0 == 0, 0.0, halo_ref[...]), u], axis=0)
    halo_ref[...] = u[tm - POOL_HALO:, :]
    gdn_ref[...] = mm(P_GDN)
    pool_ref[...] = _pool_tile(ext, u, pos0, pw_ref[...], ps_ref[...])
    q_ref[...] = mm(P_Q).astype(BF16)
    kv_ref[...] = mm(P_KV).astype(BF16)
    ba_ref[...] = mm(P_BA)


def _in_projection(x2d, mod, w_in, pool_w_bd, pool_scale, layer, seq):
    t, d = x2d.shape
    tm = min(IN_TILE, seq)
    assert seq % tm == 0 and tm > POOL_HALO
    widths = [r[1] - r[0] for r in (P_POOL, P_Q, P_KV, P_GDN, P_BA)]
    dtypes = [BF16, BF16, BF16, F32, F32]
    fixed = lambda i: (0, 0)
    return pl.pallas_call(
        functools.partial(_inproj_kernel, layer, seq),
        grid=(t // tm,),
        in_specs=[
            pl.BlockSpec((tm, d), lambda i: (i, 0)),
            pl.BlockSpec((1, 6, d), lambda i: ((i * tm) // seq, 0, 0)),
            pl.BlockSpec(memory_space=pl.ANY),
            pl.BlockSpec((POOL_DIM, POOL_DIM), fixed),
            pl.BlockSpec((1, POOL_DIM), fixed),
        ],
        out_specs=[pl.BlockSpec((tm, w), lambda i: (i, 0)) for w in widths],
        out_shape=[jax.ShapeDtypeStruct((t, w), dt) for w, dt in zip(widths, dtypes)],
        scratch_shapes=[pltpu.VMEM((P_TOTAL, d), F32), pltpu.VMEM((SUBLANES, d), F32),
                        pltpu.VMEM((P_TOTAL, d), BF16), pltpu.VMEM((POOL_HALO, POOL_DIM), F32),
                        pltpu.SemaphoreType.DMA((2,))],
        compiler_params=_cparams(("arbitrary",)),
        name="in_proj_pool",
    )(x2d, mod, jnp.transpose(w_in, (2, 0, 1)), pool_w_bd, pool_scale.reshape(1, POOL_DIM))


def _attn_kernel(sink_ref, q_ref, kvc_ref, kvp_ref, bias_ref, o_ref):
    step = pl.program_id(1)
    qi = lax.broadcasted_iota(jnp.int32, (WINDOW, 2 * WINDOW), 0)
    kj = lax.broadcasted_iota(jnp.int32, (WINDOW, 2 * WINDOW), 1)
    dist = qi + WINDOW - kj
    in_band = (dist >= 0) & (dist < WINDOW)
    lo = lax.broadcasted_iota(jnp.int32, (WINDOW, LANES), 1) < HEAD_DIM
    heads = [(p, half) for p in range(N_ATT_HEADS // 2) for half in range(2)]
    sinks = [sink_ref[p + 3 * half] for p, half in heads]
    for sub in range(ATT_BLOCKS):
        r0 = sub * WINDOW
        prev = kvp_ref[...] if sub == 0 else kvc_ref[r0 - WINDOW:r0, :]
        kv = jnp.concatenate([prev, kvc_ref[r0:r0 + WINDOW, :]], axis=0)
        k = kv[:, :KV_DIM]
        v = kv[:, KV_DIM:]
        valid = in_band & ((kj >= WINDOW) | (step > 0)) if sub == 0 else in_band
        for g0 in range(0, len(heads), ATT_GROUP):
            group = heads[g0:g0 + ATT_GROUP]
            sk_g = sinks[g0:g0 + ATT_GROUP]
            scores = []
            for p, half in group:
                qp = q_ref[r0:r0 + WINDOW, p * LANES:(p + 1) * LANES]
                qm = jnp.where(lo if half == 0 else jnp.logical_not(lo), qp, jnp.zeros_like(qp))
                scores.append(jnp.where(valid, _dot_nt(qm, k) + bias_ref[p + 3 * half], NEG_INF))
            tops = [jnp.maximum(jnp.max(s, axis=-1, keepdims=True), sk) for s, sk in zip(scores, sk_g)]
            probs = [jnp.exp(s - m) for s, m in zip(scores, tops)]
            dens = [jnp.sum(pr, axis=-1, keepdims=True) + jnp.exp(sk - m)
                    for pr, sk, m in zip(probs, sk_g, tops)]
            outs = [_dot(pr.astype(BF16), v) / den for pr, den in zip(probs, dens)]
            for idx in range(0, len(group), 2):
                p = group[idx][0]
                o_ref[r0:r0 + WINDOW, p * LANES:(p + 1) * LANES] = (
                    jnp.where(lo, outs[idx], outs[idx + 1]).astype(BF16))


def _swa_attention(q, kv, bias, sinks, seq):
    t = q.shape[0]
    rows = ATT_BLOCKS * WINDOW
    nblk = seq // rows
    return pl.pallas_call(
        _attn_kernel,
        grid=(t // seq, nblk),
        in_specs=[
            pl.BlockSpec(memory_space=pltpu.SMEM),
            pl.BlockSpec((rows, ATT_DIM), lambda b, n: (b * nblk + n, 0)),
            pl.BlockSpec((rows, 2 * KV_DIM), lambda b, n: (b * nblk + n, 0)),
            pl.BlockSpec((WINDOW, 2 * KV_DIM),
                         lambda b, n: (jnp.maximum((b * nblk + n) * ATT_BLOCKS - 1, 0), 0)),
            pl.BlockSpec((N_ATT_HEADS, WINDOW, 2 * WINDOW), lambda b, n: (0, 0, 0)),
        ],
        out_specs=pl.BlockSpec((rows, ATT_DIM), lambda b, n: (b * nblk + n, 0)),
        out_shape=jax.ShapeDtypeStruct((t, ATT_DIM), BF16),
        compiler_params=_cparams(("arbitrary", "arbitrary")),
        name="swa_attention",
    )(sinks, q, kv, kv, bias)


_GDN_BASE = SUBLANES
_GDN_LEVELS = int(np.log2(GDN_CHUNK // _GDN_BASE))


def _gdn_masks():
    r = np.arange(GDN_SUPER)
    ri, ci = r[:, None], r[None, :]
    same_chunk = (ri // GDN_CHUNK) == (ci // GDN_CHUNK)
    incl = same_chunk & (ri >= ci)
    planes = [incl, ri == ci]
    base = ((ri // _GDN_BASE) == (ci // _GDN_BASE)) & (ri > ci)
    planes.append(base)
    for lvl in range(_GDN_LEVELS):
        small = _GDN_BASE << lvl
        planes.append(((ri // (2 * small)) == (ci // (2 * small))) & ((ri // small) != (ci // small)) & (ri > ci))
    bmask = np.stack(planes).astype(np.float32)
    bmask[2] = -bmask[2]
    negmask = np.where(incl, 0.0, -np.inf).astype(np.float32)
    return negmask, bmask


_GDN_NEGMASK, _GDN_BMASK = _gdn_masks()


def _gdn_kernel(x_ref, halo_ref, ba_ref, cw_ref, alog_ref, dtb_ref, nw_ref, negmask_ref, bmask_ref,
                y_ref, state_ref, xs_ref):
    sc_id = pl.program_id(1)
    rows = GDN_SUPER
    nchunk = rows // GDN_CHUNK
    c_sz = GDN_CHUNK

    @pl.when(sc_id == 0)
    def _():
        state_ref[...] = jnp.zeros_like(state_ref)

    xs_ref[:SUBLANES, :] = jnp.where(sc_id == 0, 0.0, halo_ref[...])
    xs_ref[SUBLANES:, :] = x_ref[...]
    act = []
    for g in range(x_ref.shape[1] // LANES):
        cols = slice(g * LANES, (g + 1) * LANES)
        acc = x_ref[:, cols] * cw_ref[CONV_WIDTH - 1:CONV_WIDTH, cols]
        if g % 4 != 3:
            for s in range(1, CONV_WIDTH):
                acc = acc + (xs_ref[SUBLANES - s:SUBLANES - s + rows, cols]
                             * cw_ref[CONV_WIDTH - 1 - s:CONV_WIDTH - s, cols])
        act.append(acc * _sigmoid(acc))

    negmask = negmask_ref[...]
    tri_incl = bmask_ref[0]
    eye_b = bmask_ref[1]
    base_neg = bmask_ref[2]
    bands = [bmask_ref[3 + lvl] for lvl in range(_GDN_LEVELS)]
    li = lax.broadcasted_iota(jnp.int32, (LANES, LANES), 0)
    lj = lax.broadcasted_iota(jnp.int32, (LANES, LANES), 1)
    half_ones = jnp.where((li // HEAD_DIM) == (lj // HEAD_DIM), 1.0, 0.0).astype(BF16)
    lane_lo = lax.broadcasted_iota(jnp.int32, (rows, LANES), 1) < HEAD_DIM

    ba = ba_ref[...]
    beta_all = _sigmoid(ba)
    sp_in = ba + dtb_ref[...]
    softplus = jnp.maximum(sp_in, 0.0) + jnp.log(1.0 + jnp.exp(-jnp.abs(sp_in)))
    g_all = -jnp.exp(alog_ref[...]) * softplus
    gcum = _dot_hi_exact_rhs_lhs(tri_incl, g_all)
    gcum_t = gcum.T

    heads = range(N_GDN_HEADS)
    lane_hi = jnp.logical_not(lane_lo)
    mk = [lane_lo if h % 2 == 0 else lane_hi for h in heads]
    scale = HEAD_DIM ** -0.5

    xk, xq, gn, gc_col, beta, eg = [], [], [], [], [], []
    for h in heads:
        g = act[4 * (h // 2) + (h % 2)]
        g = g * lax.rsqrt(_dot((g * g).astype(BF16), half_ones) + NORM_EPS)
        gn.append(g)
        xk.append(jnp.where(mk[h], g, 0.0))
        xq.append(jnp.where(mk[h], pltpu.roll(g, HEAD_DIM, axis=1), 0.0) * scale)
        beta.append(beta_all[:, h:h + 1])
        gc_col.append(gcum[:, N_GDN_HEADS + h:N_GDN_HEADS + h + 1])
        eg.append(jnp.exp(gc_col[h]))

    l_b, attn, rhs = [], [], []
    for h in heads:
        gc_row = gcum_t[N_GDN_HEADS + h:N_GDN_HEADS + h + 1, :]
        decay = jnp.exp(gc_col[h] - gc_row + negmask)
        xk_b = xk[h].astype(BF16)
        kk = _dot_nt((xk[h] * beta[h]).astype(BF16), xk_b)
        l_b.append((kk * decay).astype(BF16))
        attn.append((_dot_nt(xq[h].astype(BF16), xk_b) * decay).astype(BF16))
        vv = act[4 * (h // 2) + 2]
        rhs.append(jnp.where(mk[h], gn[h] * eg[h], vv) * beta[h])

    a1 = [l_b[h] * base_neg for h in heads]
    a2 = [_dot(a1[h], a1[h]).astype(BF16) for h in heads]
    a4 = [_dot(a2[h], a2[h]).astype(BF16) for h in heads]
    inv0 = [eye_b + a1[h] for h in heads]
    acc1 = [inv0[h].astype(F32) + _dot(a2[h], inv0[h]) for h in heads]
    inv_b = [(acc1[h] + _dot(a4[h], acc1[h].astype(BF16))).astype(BF16) for h in heads]
    for lvl in range(_GDN_LEVELS - 1):
        mid = [_dot(l_b[h] * bands[lvl], inv_b[h]).astype(BF16) for h in heads]
        inv_b = [inv_b[h] - _dot(inv_b[h], mid[h]).astype(BF16) for h in heads]
    half = [_dot(inv_b[h], rhs[h].astype(BF16)) for h in heads]
    mid = [_dot(l_b[h] * bands[_GDN_LEVELS - 1], half[h].astype(BF16)) for h in heads]
    sol = [half[h] - _dot(inv_b[h], mid[h].astype(BF16)) for h in heads]

    lane_lo_s = lax.broadcasted_iota(jnp.int32, (LANES, LANES), 1) < HEAD_DIM
    mk_s = [lane_lo_s if h % 2 == 0 else jnp.logical_not(lane_lo_s) for h in heads]
    sol_b = [sol[h].astype(BF16) for h in heads]
    attn_sol = [_dot(attn[h], sol_b[h]) for h in heads]
    q_eff = [(xq[h] * eg[h] - jnp.where(mk[h], attn_sol[h], 0.0)).astype(BF16) for h in heads]
    o_free = [jnp.where(mk[h], 0.0, attn_sol[h]) for h in heads]
    kw = [[] for _ in heads]
    ku = [[] for _ in heads]
    cdec = [[] for _ in heads]
    for c in range(nchunk):
        r0 = c * c_sz
        for h in heads:
            glast = gcum[r0 + c_sz - 1:r0 + c_sz, N_GDN_HEADS + h:N_GDN_HEADS + h + 1]
            kd_t = (xk[h][r0:r0 + c_sz] * jnp.exp(glast - gc_col[h][r0:r0 + c_sz])).T
            both = _dot(kd_t.astype(BF16), sol_b[h][r0:r0 + c_sz])
            kw[h].append(jnp.where(mk_s[h], both, 0.0).astype(BF16))
            ku[h].append(jnp.where(mk_s[h], 0.0, both))
            cdec[h].append(jnp.exp(glast))
    st = [state_ref[h] for h in heads]
    o_parts = [[] for _ in heads]
    for c in range(nchunk):
        r0 = c * c_sz
        for h in heads:
            lhs = jnp.concatenate([kw[h][c], q_eff[h][r0:r0 + c_sz]], axis=0)
            prod = _dot(lhs, st[h].astype(BF16))
            o_parts[h].append(prod[LANES:] + o_free[h][r0:r0 + c_sz])
            st[h] = st[h] * cdec[h][c] + ku[h][c] - prod[:LANES]
    for h in heads:
        state_ref[h] = st[h]

    for p in range(N_GDN_HEADS // 2):
        o_pair = [jnp.concatenate(o_parts[h], axis=0) for h in (2 * p, 2 * p + 1)]
        o = jnp.where(lane_lo, o_pair[1], o_pair[0])
        ms = _dot((o * o).astype(BF16), half_ones) * (1.0 / HEAD_DIM)
        zz = act[4 * p + 3]
        y = o * lax.rsqrt(ms + NORM_EPS) * nw_ref[...] * zz
        y_ref[:, p * LANES:(p + 1) * LANES] = y.astype(BF16)


def _dot_hi_exact_rhs_lhs(m_bf16, x):
    hi, lo = _split_bf16(x)
    return _dot(m_bf16, hi) + _dot(m_bf16, lo)


def _gdn_mixer(gdn, ba, conv_p, alog_v, dtb_v, nw_v, seq):
    t, c = gdn.shape
    rows = GDN_SUPER
    nsc = seq // rows
    hb = rows // SUBLANES
    return pl.pallas_call(
        _gdn_kernel,
        grid=(t // seq, nsc),
        in_specs=[
            pl.BlockSpec((rows, c), lambda b, s: (b * nsc + s, 0)),
            pl.BlockSpec((SUBLANES, c), lambda b, s: (jnp.maximum((b * nsc + s) * hb - 1, 0), 0)),
            pl.BlockSpec((rows, LANES), lambda b, s: (b * nsc + s, 0)),
            pl.BlockSpec((CONV_WIDTH, c), lambda b, s: (0, 0)),
            pl.BlockSpec((1, LANES), lambda b, s: (0, 0)),
            pl.BlockSpec((1, LANES), lambda b, s: (0, 0)),
            pl.BlockSpec((1, LANES), lambda b, s: (0, 0)),
            pl.BlockSpec((rows, rows), lambda b, s: (0, 0)),
            pl.BlockSpec((3 + _GDN_LEVELS, rows, rows), lambda b, s: (0, 0, 0)),
        ],
        out_specs=pl.BlockSpec((rows, GDN_DIM), lambda b, s: (b * nsc + s, 0)),
        out_shape=jax.ShapeDtypeStruct((t, GDN_DIM), BF16),
        scratch_shapes=[pltpu.VMEM((N_GDN_HEADS, LANES, LANES), F32), pltpu.VMEM((rows + SUBLANES, c), F32)],
        compiler_params=_cparams(("arbitrary", "arbitrary")),
        name="gdn_mixer",
    )(gdn, gdn, ba, conv_p, alog_v, dtb_v, nw_v, jnp.asarray(_GDN_NEGMASK), jnp.asarray(_GDN_BMASK, BF16))


def _route_tile(logits, before, carry_ref, live):
    shape = logits.shape
    lane = lax.broadcasted_iota(jnp.int32, shape, 1).astype(F32)
    work = logits
    vals, idxs = [], []
    for _k in range(TOP_K):
        m = jnp.max(work, axis=-1, keepdims=True)
        idx = jnp.min(jnp.where(work == m, lane, float(LANES)), axis=-1, keepdims=True)
        vals.append(m)
        idxs.append(idx)
        work = jnp.where(lane == idx, -jnp.inf, work)
    exps = [jnp.exp(v - vals[0]) for v in vals]
    den = exps[0] + exps[1] + exps[2] + exps[3]
    onehots = [lane == idx for idx in idxs]
    member = jnp.zeros(shape, F32)
    for oh in onehots:
        member = member + jnp.where(oh, 1.0, 0.0)
    rank = _dot(before, member.astype(BF16)) + carry_ref[...]
    carry_ref[...] = carry_ref[...] + live * jnp.sum(member, axis=0, keepdims=True)
    info = jnp.zeros(shape, F32)
    for k in range(TOP_K):
        rank_k = jnp.sum(jnp.where(onehots[k], rank, 0.0), axis=-1, keepdims=True)
        info = jnp.where(lane == float(k), idxs[k], info)
        info = jnp.where(lane == float(TOP_K + k), rank_k, info)
        info = jnp.where(lane == float(2 * TOP_K + k), exps[k] / den, info)
    return info


def _outproj_kernel(x_ref, mod_ref, yp_ref, ya_ref, yg_ref, wp_ref, wa_ref, wg_ref, lng_ref, lnb_ref,
                    rw_ref, rb_ref, before_ref, x1_ref, h2_ref, info_ref, er_ref, cnt_ref,
                    carry_ref, logit_s):
    step = pl.program_id(0)

    @pl.when(step == 0)
    def _():
        carry_ref[...] = jnp.zeros_like(carry_ref)
        logit_s[...] = jnp.zeros_like(logit_s)

    part = x_ref.shape[0] // OUT_SPLIT
    halves = [slice(j * part, (j + 1) * part) for j in range(OUT_SPLIT)]
    y = [_dot(yp_ref[r, :], wp_ref[...]) + _dot(ya_ref[r, :], wa_ref[...]) + _dot(yg_ref[r, :], wg_ref[...])
         for r in halves]
    live = jnp.where(step > 0, 1.0, 0.0)
    info = _route_tile(logit_s[...], before_ref[...], carry_ref, live)
    info_ref[...] = info
    er_ref[...] = info.T[:SUBLANES]
    cnt_ref[...] = carry_ref[...]
    g1 = mod_ref[0, 2:3, :]
    sh2 = mod_ref[0, 3:4, :]
    sc2 = mod_ref[0, 4:5, :]
    for r, y_r in zip(halves, y):
        x1 = _layer_norm(DEEPNORM_ALPHA * x_ref[r, :] + g1 * y_r, lng_ref[...], lnb_ref[...])
        x1_ref[r, :] = x1
        h2 = x1 * (1.0 + sc2) + sh2
        h2_ref[r, :] = _pack_bf16_pairs(h2)
        logit_s[r, :] = _dot(h2.astype(BF16), rw_ref[...]) + rb_ref[...]


def _out_projection(x2d, mod, yp, ya, yg, w_out_p, ln_g, ln_b, rw, rb, seq):
    t, d = x2d.shape
    tm = ROW_TILE
    wp = w_out_p[:POOL_DIM]
    wa = w_out_p[POOL_DIM:POOL_DIM + ATT_DIM]
    wg = w_out_p[POOL_DIM + ATT_DIM:]
    last = t // tm - 1
    row = lambda i: (jnp.minimum(i, last), 0)
    routed = lambda i: (jnp.maximum(i - 1, 0), 0)
    fixed = lambda i: (0, 0)
    return pl.pallas_call(
        _outproj_kernel,
        grid=(t // tm + 1,),
        in_specs=[
            pl.BlockSpec((tm, d), row),
            pl.BlockSpec((1, 6, d), lambda i: ((jnp.minimum(i, last) * tm) // seq, 0, 0)),
            pl.BlockSpec((tm, POOL_DIM), row),
            pl.BlockSpec((tm, ATT_DIM), row),
            pl.BlockSpec((tm, GDN_DIM), row),
            pl.BlockSpec((POOL_DIM, d), fixed),
            pl.BlockSpec((ATT_DIM, d), fixed),
            pl.BlockSpec((GDN_DIM, d), fixed),
            pl.BlockSpec((1, d), fixed),
            pl.BlockSpec((1, d), fixed),
            pl.BlockSpec((d, LANES), fixed),
            pl.BlockSpec((1, LANES), fixed),
            pl.BlockSpec((tm, tm), fixed),
        ],
        out_specs=[pl.BlockSpec((tm, d), row), pl.BlockSpec((tm, d // 2), row), pl.BlockSpec((tm, LANES), routed),
                   pl.BlockSpec((SUBLANES, tm), lambda i: (0, jnp.maximum(i - 1, 0))),
                   pl.BlockSpec((1, LANES), fixed)],
        out_shape=[jax.ShapeDtypeStruct((t, d), F32), jax.ShapeDtypeStruct((t, d // 2), jnp.int32),
                   jax.ShapeDtypeStruct((t, LANES), F32), jax.ShapeDtypeStruct((SUBLANES, t), F32),
                   jax.ShapeDtypeStruct((1, LANES), F32)],
        scratch_shapes=[pltpu.VMEM((1, LANES), F32), pltpu.VMEM((tm, LANES), F32)],
        compiler_params=_cparams(("arbitrary",)),
        name="out_proj_ln_route",
    )(x2d, mod, yp, ya, yg, wp, wa, wg, ln_g.reshape(1, d), ln_b.reshape(1, d), rw, rb,
      jnp.tril(jnp.ones((tm, tm), BF16), -1))


def _slot_kernel(er_ref, cnt_ref, dest_ref, pcum_ref):
    cnt = jnp.broadcast_to(cnt_ref[...], (SUBLANES, LANES))
    padded = jnp.floor((cnt + float(EXPERT_BLOCK - 1)) * (1.0 / EXPERT_BLOCK)) * float(EXPERT_BLOCK)
    lane8 = lax.broadcasted_iota(jnp.int32, (SUBLANES, LANES), 1)
    acc = padded
    step = 1
    while step < LANES:
        acc = acc + jnp.where(lane8 >= step, pltpu.roll(acc, step, axis=1), 0.0)
        step *= 2
    pcum_ref[...] = acc[:1].astype(jnp.int32)
    pstart = acc - padded

    er = er_ref[...]
    start = jnp.zeros(er.shape, F32)
    for e in range(N_EXPERTS):
        offset = jnp.sum(jnp.where(lane8 == e, pstart, 0.0), axis=-1, keepdims=True)
        start = jnp.where(er == float(e), offset, start)
    row = lax.broadcasted_iota(jnp.int32, er.shape, 0)
    slots = jnp.where(row < TOP_K, start + pltpu.roll(er, TOP_K, axis=0), 0.0)
    dest_ref[...] = slots.astype(jnp.int32)


def _slots(er, cnt):
    t = er.shape[1]
    return pl.pallas_call(
        _slot_kernel,
        grid=(1,),
        in_specs=[pl.BlockSpec((SUBLANES, t), lambda i: (0, 0)), pl.BlockSpec((1, LANES), lambda i: (0, 0))],
        out_specs=[pl.BlockSpec((SUBLANES, t), lambda i: (0, 0)), pl.BlockSpec((1, LANES), lambda i: (0, 0))],
        out_shape=[jax.ShapeDtypeStruct((SUBLANES, t), jnp.int32), jax.ShapeDtypeStruct((1, LANES), jnp.int32)],
        compiler_params=_cparams(("arbitrary",)),
        name="moe_slots",
    )(er, cnt)


def _expert_kernel(e0, be_ref, nxt_ref, val_ref, nu_ref, x_ref, wup_hbm, bup_ref, wdn_hbm, bdn_ref, y_ref,
                   wup_st, wdn_st, wup_bf, wdn_bf, sems):
    i = pl.program_id(0)
    e = be_ref[i]
    prev = be_ref[jnp.maximum(i - 1, 0)]
    used = i < nu_ref[0]

    def weight_copies(expert):
        return (pltpu.make_async_copy(wup_hbm.at[e0 + expert], wup_st, sems.at[0]),
                pltpu.make_async_copy(wdn_hbm.at[e0 + expert], wdn_st, sems.at[1]))

    @pl.when(i == 0)
    def _():
        for cp in weight_copies(e):
            cp.start()

    @pl.when(used & ((i == 0) | (e != prev)))
    def _():
        for cp in weight_copies(e):
            cp.wait()
        wup_bf[...] = wup_st[...].astype(BF16)
        wdn_bf[...] = wdn_st[...].astype(BF16)

        @pl.when(nxt_ref[i] >= 0)
        def _():
            for cp in weight_copies(nxt_ref[i]):
                cp.start()

    def ffn(rows):
        xb = _unpack_bf16_pairs(x_ref[:rows, :]).astype(BF16)
        hb = _dot(xb, wup_bf[...]) + bup_ref[0]
        x_glu = jnp.minimum(hb[:, :EXPERT_DIM], SWIGLU_LIMIT)
        x_lin = jnp.clip(hb[:, EXPERT_DIM:], -SWIGLU_LIMIT, SWIGLU_LIMIT)
        act = x_glu * _sigmoid(SWIGLU_ALPHA * x_glu) * (x_lin + 1.0)
        y = _dot(act.astype(BF16), wdn_bf[...]) + bdn_ref[0]
        y_ref[:rows, :] = _pack_bf16_pairs(y)

    bm = x_ref.shape[0]
    pieces = (val_ref[i] + (EXPERT_GRAIN - 1)) // EXPERT_GRAIN
    for n_piece in range(1, bm // EXPERT_GRAIN + 1):
        rows_used = n_piece * EXPERT_GRAIN

        @pl.when(used & (pieces == n_piece))
        def _(rows_used=rows_used):
            ffn(rows_used)
            if rows_used < bm:
                y_ref[rows_used:, :] = jnp.zeros((bm - rows_used, y_ref.shape[1]), y_ref.dtype)

    @pl.when(i >= nu_ref[0])
    def _():
        y_ref[...] = jnp.zeros_like(y_ref)


def _expert_ffn(xbuf, block_e, next_e, valid, n_used, w_up, b_up, w_down, b_down, layer):
    p, dh = xbuf.shape
    d = 2 * dh
    bm = EXPERT_BLOCK
    ne, _, n_up = w_up.shape
    e0 = layer * N_EXPERTS
    grid_spec = pltpu.PrefetchScalarGridSpec(
        num_scalar_prefetch=4,
        grid=(p // bm,),
        in_specs=[
            pl.BlockSpec((bm, dh), lambda i, be, nx, vl, nu: (jnp.minimum(i, nu[0] - 1), 0)),
            pl.BlockSpec(memory_space=pl.ANY),
            pl.BlockSpec((1, 1, n_up), lambda i, be, nx, vl, nu: (e0 + be[i], 0, 0)),
            pl.BlockSpec(memory_space=pl.ANY),
            pl.BlockSpec((1, 1, d), lambda i, be, nx, vl, nu: (e0 + be[i], 0, 0)),
        ],
        out_specs=pl.BlockSpec((bm, dh), lambda i, be, nx, vl, nu: (i, 0)),
        scratch_shapes=[pltpu.VMEM((d, n_up), F32), pltpu.VMEM((EXPERT_DIM, d), F32),
                        pltpu.VMEM((d, n_up), BF16), pltpu.VMEM((EXPERT_DIM, d), BF16),
                        pltpu.SemaphoreType.DMA((2,))],
    )
    return pl.pallas_call(
        functools.partial(_expert_kernel, e0),
        grid_spec=grid_spec,
        out_shape=jax.ShapeDtypeStruct((p, dh), jnp.int32),
        compiler_params=_cparams(("arbitrary",)),
        name="expert_ffn",
    )(block_e, next_e, valid, n_used, xbuf, w_up, b_up, w_down, b_down)


def _combine_kernel(x1_ref, mod_ref, yg_ref, info_ref, lng_ref, lnb_ref, o_ref):
    info = info_ref[...]
    y = jnp.zeros(x1_ref.shape, F32)
    for k in range(TOP_K):
        gate = info[:, 2 * TOP_K + k:2 * TOP_K + k + 1]
        y = y + gate * _unpack_bf16_pairs(yg_ref[k])
    g2 = mod_ref[0, 5:6, :]
    o_ref[...] = _layer_norm(DEEPNORM_ALPHA * x1_ref[...] + g2 * y, lng_ref[...], lnb_ref[...])


def _combine(x1, mod, yg, info, ln_g, ln_b, seq):
    t, d = x1.shape
    tm = min(IN_TILE, seq)
    row = lambda i: (i, 0)
    fixed = lambda i: (0, 0)
    return pl.pallas_call(
        _combine_kernel,
        grid=(t // tm,),
        in_specs=[
            pl.BlockSpec((tm, d), row),
            pl.BlockSpec((1, 6, d), lambda i: ((i * tm) // seq, 0, 0)),
            pl.BlockSpec((TOP_K, tm, d // 2), lambda i: (0, i, 0)),
            pl.BlockSpec((tm, LANES), row),
            pl.BlockSpec((1, d), fixed),
            pl.BlockSpec((1, d), fixed),
        ],
        out_specs=pl.BlockSpec((tm, d), row),
        out_shape=jax.ShapeDtypeStruct((t, d), F32),
        compiler_params=_cparams(("arbitrary",)),
        name="moe_combine_ln",
    )(x1, mod, yg, info, ln_g.reshape(1, d), ln_b.reshape(1, d))


def _sc_workers():
    info = plsc.get_sparse_core_info()
    return info.num_cores, info.num_cores * info.num_subcores


def _sc_scatter_rows(rows, idx, n_out):
    t, w = rows.shape
    kk = idx.shape[0]
    n_cores, n_workers = _sc_workers()
    ch = SC_CHUNK
    assert t % (2 * n_workers * ch) == 0
    n_chunk = t // (n_workers * ch)
    idx_c = jnp.transpose(idx.reshape(kk, t // ch, ch), (1, 0, 2))

    @functools.partial(
        pl.kernel,
        mesh=plsc.VectorSubcoreMesh(core_axis_name="c", subcore_axis_name="s"),
        out_type=jax.ShapeDtypeStruct((n_out, w), rows.dtype),
        scratch_types=[pltpu.VMEM((2, kk, ch), jnp.int32), pltpu.VMEM((2, ch, w), rows.dtype),
                       pltpu.SemaphoreType.DMA((2,)), pltpu.SemaphoreType.DMA((2,))],
        name="sc_dispatch_scatter",
    )
    def scatter_kernel(rows_hbm, idx_hbm, out_hbm, idx_v, rows_v, load_sem, scat_sem):
        base = (lax.axis_index("s") * n_cores + lax.axis_index("c")) * n_chunk

        def load(j, b):
            return pltpu.make_async_copy(rows_hbm.at[pl.ds((base + j) * ch, ch)], rows_v.at[b], load_sem.at[b])

        def scatters(b):
            return [pltpu.make_async_copy(rows_v.at[b], out_hbm.at[idx_v.at[b, q]], scat_sem.at[b])
                    for q in range(kk)]

        pltpu.sync_copy(idx_hbm.at[base], idx_v.at[0])
        load(0, 0).start()

        @pl.loop(0, n_chunk, step=2)
        def _(j0):
            for b in range(2):
                j = j0 + b
                other = 1 - b

                @pl.when(j >= 1)
                def _():
                    for cp in scatters(other):
                        cp.wait()

                @pl.when(j + 1 < n_chunk)
                def _():
                    pltpu.sync_copy(idx_hbm.at[base + j + 1], idx_v.at[other])
                    load(j + 1, other).start()

                load(j, b).wait()
                for cp in scatters(b):
                    cp.start()

        for cp in scatters((n_chunk - 1) % 2):
            cp.wait()

    return scatter_kernel(rows, idx_c)


def _sc_gather_rows(table, idx):
    m = idx.shape[0]
    w = table.shape[1]
    n_cores, n_workers = _sc_workers()
    ch = SC_CHUNK
    assert m % (2 * n_workers * ch) == 0
    n_chunk = m // (n_workers * ch)
    idx_c = idx.reshape(m // ch, 1, ch)

    @functools.partial(
        pl.kernel,
        mesh=plsc.VectorSubcoreMesh(core_axis_name="c", subcore_axis_name="s"),
        out_type=jax.ShapeDtypeStruct((m, w), table.dtype),
        scratch_types=[pltpu.VMEM((2, 1, ch), jnp.int32), pltpu.VMEM((2, ch, w), table.dtype),
                       pltpu.SemaphoreType.DMA((2,)), pltpu.SemaphoreType.DMA((2,))],
        name="sc_combine_gather",
    )
    def gather_kernel(table_hbm, idx_hbm, out_hbm, idx_v, rows_v, gather_sem, write_sem):
        base = (lax.axis_index("s") * n_cores + lax.axis_index("c")) * n_chunk

        def gather(b):
            return pltpu.make_async_copy(table_hbm.at[idx_v.at[b, 0]], rows_v.at[b], gather_sem.at[b])

        def write(j, b):
            return pltpu.make_async_copy(rows_v.at[b], out_hbm.at[pl.ds((base + j) * ch, ch)], write_sem.at[b])

        pltpu.sync_copy(idx_hbm.at[base], idx_v.at[0])
        gather(0).start()

        @pl.loop(0, n_chunk, step=2)
        def _(j0):
            for b in range(2):
                j = j0 + b
                other = 1 - b

                @pl.when(j >= 1)
                def _():
                    write(j - 1, other).wait()

                @pl.when(j + 1 < n_chunk)
                def _():
                    pltpu.sync_copy(idx_hbm.at[base + j + 1], idx_v.at[other])
                    gather(other).start()

                gather(b).wait()
                write(j, b).start()

        write(n_chunk - 1, (n_chunk - 1) % 2).wait()

    return gather_kernel(table, idx_c)


def _lane_vector(vals, offset):
    return jnp.zeros((1, LANES), F32).at[0, offset:offset + vals.shape[0]].set(vals.astype(F32))


def _moe(h2, info, er, cnt, x1, mod, ln_g, ln_b, w_up, b_up, w_down, b_down, layer, seq):
    t, dh = h2.shape
    a = t * TOP_K
    bm = EXPERT_BLOCK
    slots, pcum_v = _slots(er, cnt)
    pcum = pcum_v[0, :N_EXPERTS]
    dest = slots[:TOP_K]
    n_blocks = -(-a // bm) + N_EXPERTS
    starts = jnp.arange(n_blocks, dtype=jnp.int32) * bm
    block_e = jnp.minimum(jnp.sum(pcum[None, :] <= starts[:, None], axis=1), N_EXPERTS - 1).astype(jnp.int32)
    n_used = (pcum[-1] // bm).astype(jnp.int32).reshape(1)
    later = block_e[None, :] > block_e[:, None]
    group_end = n_blocks - jnp.sum(later, axis=1)
    next_e = jnp.min(jnp.where(later, block_e[None, :], N_EXPERTS), axis=1)
    next_e = jnp.where(group_end < n_used[0], next_e, -1).astype(jnp.int32)
    counts = cnt[0, :N_EXPERTS].astype(jnp.int32)
    pstart = pcum - ((counts + bm - 1) // bm) * bm
    mine = block_e[:, None] == jnp.arange(N_EXPERTS, dtype=jnp.int32)[None, :]
    count_b = jnp.sum(jnp.where(mine, counts[None, :], 0), axis=1)
    pstart_b = jnp.sum(jnp.where(mine, pstart[None, :], 0), axis=1)
    valid = jnp.clip(count_b - (starts - pstart_b), 0, bm).astype(jnp.int32)
    xbuf = _sc_scatter_rows(h2, dest, n_blocks * bm)
    ybuf = _expert_ffn(xbuf, block_e, next_e, valid, n_used, w_up, b_up, w_down, b_down, layer)
    yg = _sc_gather_rows(ybuf, dest.reshape(a)).reshape(TOP_K, t, dh)
    return _combine(x1, mod, yg, info, ln_g, ln_b, seq)


def kernel(x, c, rel_bias, w_in, w_out, w_ada, b_ada, ln1_g, ln1_b, ln2_g, ln2_b, pool_w, pool_scale,
           attn_sinks, conv_w, gdn_a_log, gdn_dt_bias, gdn_norm_w, router_w, router_b,
           exp_w_up, exp_b_up, exp_w_down, exp_b_down):
    bsz, seq, d = x.shape
    depth = w_in.shape[0]
    t = bsz * seq
    assert d == D_MODEL and w_in.shape[2] == IN_DIM and depth == DEPTH
    assert seq % GDN_SUPER == 0 and seq % (ATT_BLOCKS * WINDOW) == 0
    assert t % ROW_TILE == 0 and seq % ROW_TILE == 0

    mod_all = _modulation(c, w_ada, b_ada).reshape(depth, bsz, 6, d)
    bias = _band_bias(rel_bias)

    w_up_all = exp_w_up.reshape((depth * N_EXPERTS,) + exp_w_up.shape[2:])
    b_up_all = exp_b_up.reshape(depth * N_EXPERTS, 1, exp_b_up.shape[2])
    w_down_all = exp_w_down.reshape((depth * N_EXPERTS,) + exp_w_down.shape[2:])
    b_down_all = exp_b_down.reshape(depth * N_EXPERTS, 1, exp_b_down.shape[2])

    x2d = x.reshape(t, d)
    for l in range(depth):
        mod = mod_all[l]
        w_out_p = _take_static(w_out[l], _OUT_PERM, 0).astype(BF16)
        ident = jnp.zeros((CONV_WIDTH, 1), F32).at[CONV_WIDTH - 1, 0].set(1.0)
        conv_p = jnp.where(jnp.asarray(_GDN_CONV_SRC >= 0), _take_cols(conv_w[l].astype(F32), _GDN_CONV_SRC),
                           ident)
        pool_bd = jnp.zeros((POOL_DIM, POOL_DIM), F32)
        for gi in range(len(POOL_WINDOWS)):
            sl = slice(gi * POOL_GROUP, (gi + 1) * POOL_GROUP)
            pool_bd = pool_bd.at[sl, sl].set(pool_w[l, gi].astype(F32))
        alog_v = _lane_vector(gdn_a_log[l], N_GDN_HEADS)
        dtb_v = _lane_vector(gdn_dt_bias[l], N_GDN_HEADS)
        nw_v = jnp.tile(gdn_norm_w[l].astype(F32), 2).reshape(1, LANES)
        rw = jnp.zeros((d, LANES), BF16).at[:, :N_EXPERTS].set(router_w[l].astype(BF16))
        rb = jnp.full((1, LANES), NEG_INF, F32).at[0, :N_EXPERTS].set(router_b[l].astype(F32))

        y_pool, aq, akv, gdn, ba = _in_projection(x2d, mod, w_in.astype(F32), pool_bd.astype(BF16),
                                                  pool_scale[l].astype(F32), l, seq)
        y_att = _swa_attention(aq, akv, bias, attn_sinks[l].astype(F32), seq)
        y_gdn = _gdn_mixer(gdn, ba, conv_p, alog_v, dtb_v, nw_v, seq)
        x1, h2, info, er, cnt = _out_projection(x2d, mod, y_pool, y_att, y_gdn, w_out_p, ln1_g[l], ln1_b[l],
                                            rw, rb, seq)
        x2d = _moe(h2, info, er, cnt, x1, mod, ln2_g[l], ln2_b[l], w_up_all, b_up_all, w_down_all, b_down_all,
                   l, seq)
    return x2d.reshape(bsz, seq, d)
```

```python
import functools

import numpy as np
import jax
import jax.numpy as jnp
from jax import lax
from jax.experimental import pallas as pl
from jax.experimental.pallas import tpu as pltpu
from jax.experimental.pallas import tpu_sc as plsc

F32 = jnp.float32
BF16 = jnp.bfloat16

D_MODEL = 1024
HEAD_DIM = 64
POOL_DIM = 256
POOL_WINDOWS = (2, 4, 8, 16)
POOL_GROUP = 64
POOL_HALO = 16
N_ATT_HEADS = 6
N_KV_HEADS = 2
ATT_DIM = 384
KV_DIM = 128
WINDOW = 128
N_BUCKETS = 32
MAX_DISTANCE = 128
N_GDN_HEADS = 6
GDN_DIM = 384
CONV_WIDTH = 4
GDN_CHUNK = 64
N_EXPERTS = 32
TOP_K = 4
EXPERT_DIM = 1024
SWIGLU_ALPHA = 1.702
SWIGLU_LIMIT = 7.0
DEPTH = 2
DEEPNORM_ALPHA = (2 * DEPTH) ** 0.25
LN_EPS = 1e-5
NORM_EPS = 1e-6
NEG_INF = -1e30

LANES = 128
SUBLANES = 8
VMEM_LIMIT = 56 * 1024 * 1024

MOD_TILE = 2048
ROW_TILE = 512
OUT_SPLIT = 2
IN_TILE = 1024
ATT_BLOCKS = 8
ATT_GROUP = 6
GDN_SUPER = 256
EXPERT_BLOCK = 1024
EXPERT_GRAIN = 256
COMBINE_SLOTS = 3
SC_CHUNK = 64

_OFF_AQ = POOL_DIM
_OFF_AK = _OFF_AQ + ATT_DIM
_OFF_AV = _OFF_AK + KV_DIM
_OFF_GQ = _OFF_AV + KV_DIM
_OFF_GK = _OFF_GQ + GDN_DIM
_OFF_GV = _OFF_GK + GDN_DIM
_OFF_GZ = _OFF_GV + GDN_DIM
_OFF_GB = _OFF_GZ + GDN_DIM
_OFF_GA = _OFF_GB + N_GDN_HEADS
IN_DIM = _OFF_GA + N_GDN_HEADS

P_POOL = (0, POOL_DIM)
P_Q = (P_POOL[1], P_POOL[1] + ATT_DIM)
P_KV = (P_Q[1], P_Q[1] + 2 * KV_DIM)
P_GDN = (P_KV[1], P_KV[1] + 4 * GDN_DIM)
P_BA = (P_GDN[1], P_GDN[1] + LANES)
P_TOTAL = P_BA[1]


def _head_cols(off, h):
    return list(range(off + HEAD_DIM * h, off + HEAD_DIM * (h + 1)))


def _build_in_perm():
    cols = list(range(POOL_DIM))
    for p in range(N_ATT_HEADS // 2):
        cols += _head_cols(_OFF_AQ, p) + _head_cols(_OFF_AQ, p + 3)
    cols += list(range(_OFF_AK, _OFF_AK + 2 * KV_DIM))
    gdn_src = []
    for p in range(N_GDN_HEADS // 2):
        e, o = 2 * p, 2 * p + 1
        grp = (_head_cols(_OFF_GK, e) + _head_cols(_OFF_GQ, e)
               + _head_cols(_OFF_GQ, o) + _head_cols(_OFF_GK, o)
               + _head_cols(_OFF_GV, o) + _head_cols(_OFF_GV, e)
               + _head_cols(_OFF_GZ, o) + _head_cols(_OFF_GZ, e))
        cols += grp
        gdn_src += [c - _OFF_GQ if c < _OFF_GZ else -1 for c in grp]
    cols += list(range(_OFF_GB, _OFF_GB + 2 * N_GDN_HEADS))
    cols += [-1] * (LANES - 2 * N_GDN_HEADS)
    assert len(cols) == P_TOTAL
    return np.asarray(cols, np.int32), np.asarray(gdn_src, np.int32)


_IN_PERM, _GDN_CONV_SRC = _build_in_perm()


def _build_out_perm():
    rows = list(range(POOL_DIM))
    for p in range(N_ATT_HEADS // 2):
        rows += _head_cols(POOL_DIM, p) + _head_cols(POOL_DIM, p + 3)
    for p in range(N_GDN_HEADS // 2):
        rows += _head_cols(POOL_DIM + ATT_DIM, 2 * p + 1) + _head_cols(POOL_DIM + ATT_DIM, 2 * p)
    return np.asarray(rows, np.int32)


_OUT_PERM = _build_out_perm()


def _t5_bucket_line():
    n = np.maximum(2 * WINDOW - 1 - np.arange(3 * WINDOW - 1), 0)
    max_exact = N_BUCKETS // 2
    nf = np.maximum(n, 1).astype(np.float32)
    large = max_exact + (np.log(nf / max_exact) / np.float32(np.log(MAX_DISTANCE / max_exact))
                         * (N_BUCKETS - max_exact)).astype(np.int32)
    large = np.minimum(large, N_BUCKETS - 1)
    return np.where(n < max_exact, n, large).astype(np.int32)


_BUCKET_LINE = _t5_bucket_line()


def _band_bias(rel_bias):
    n_line = 3 * WINDOW - 1
    line = jnp.take(rel_bias.astype(F32), jnp.asarray(_BUCKET_LINE), axis=0).T
    heads = line.shape[0]
    padded = jnp.concatenate([line, jnp.zeros((heads, 1), F32)], axis=1)
    skew = jnp.tile(padded, (1, WINDOW))[:, :WINDOW * n_line].reshape(heads, WINDOW, n_line)
    return skew[:, :, WINDOW - 1:3 * WINDOW - 1]


def _take_static(w, perm, axis):
    parts = []
    start = 0
    for i in range(1, len(perm) + 1):
        run_ends = (i == len(perm) or ((perm[i] < 0) != (perm[i - 1] < 0))
                    or (perm[i] >= 0 and perm[i] != perm[i - 1] + 1))
        if run_ends:
            if perm[start] < 0:
                shape = list(w.shape)
                shape[axis] = i - start
                parts.append(jnp.zeros(shape, w.dtype))
            else:
                parts.append(lax.slice_in_dim(w, int(perm[start]), int(perm[start]) + (i - start), axis=axis))
            start = i
    return jnp.concatenate(parts, axis=axis)


def _take_cols(w, perm):
    return _take_static(w, perm, w.ndim - 1)


def _split_bf16(x):
    hi = x.astype(BF16)
    lo = (x - hi.astype(F32)).astype(BF16)
    return hi, lo


def _pack_bf16_pairs(x):
    n = x.shape[1] // 2
    bits = pltpu.bitcast(x.astype(BF16).astype(F32), jnp.int32)
    return lax.shift_right_logical(bits[:, :n], 16) | bits[:, n:]


def _unpack_bf16_pairs(u):
    lo = pltpu.bitcast(lax.shift_left(u, 16), F32)
    hi = pltpu.bitcast(u & jnp.int32(-65536), F32)
    return jnp.concatenate([lo, hi], axis=1)


def _dot(a, b):
    return jnp.dot(a, b, preferred_element_type=F32)


def _dot_nt(a, b):
    return lax.dot_general(a, b, (((1,), (1,)), ((), ())), preferred_element_type=F32)


def _sigmoid(x):
    return 1.0 / (1.0 + jnp.exp(-x))


def _layer_norm(r, g, b):
    mu = jnp.mean(r, axis=-1, keepdims=True)
    d = r - mu
    var = jnp.mean(d * d, axis=-1, keepdims=True)
    return d * lax.rsqrt(var + LN_EPS) * g + b


def _cparams(sem):
    return pltpu.CompilerParams(dimension_semantics=sem, vmem_limit_bytes=VMEM_LIMIT)


def _mod_kernel(c_ref, w_ref, b_ref, o_ref):
    c = c_ref[...]
    ca = c * _sigmoid(c)
    o_ref[0] = _dot(ca.astype(BF16), w_ref[0].astype(BF16)) + b_ref[0]


def _modulation(c, w_ada, b_ada):
    depth, d, n = w_ada.shape
    bsz = c.shape[0]
    tn = MOD_TILE
    return pl.pallas_call(
        _mod_kernel,
        grid=(depth, n // tn),
        in_specs=[
            pl.BlockSpec((bsz, d), lambda l, j: (0, 0)),
            pl.BlockSpec((1, d, tn), lambda l, j: (l, 0, j)),
            pl.BlockSpec((1, 1, tn), lambda l, j: (l, 0, j)),
        ],
        out_specs=pl.BlockSpec((1, bsz, tn), lambda l, j: (l, 0, j)),
        out_shape=jax.ShapeDtypeStruct((depth, bsz, n), F32),
        compiler_params=_cparams(("arbitrary", "arbitrary")),
        name="adaln_mod",
    )(c, w_ada, b_ada.reshape(depth, 1, n))


def _perm_runs(perm):
    runs = []
    start = 0
    for i in range(1, len(perm) + 1):
        run_ends = (i == len(perm) or ((perm[i] < 0) != (perm[i - 1] < 0))
                    or (perm[i] >= 0 and perm[i] != perm[i - 1] + 1))
        if run_ends:
            runs.append((int(perm[start]) if perm[start] >= 0 else -1, i - start, start))
            start = i
    return runs


_IN_RUNS = _perm_runs(_IN_PERM)


def _pool_tile(ext, u, pos0, w_bd, scale):
    row = lax.broadcasted_iota(jnp.int32, ext.shape, 0)
    lane = lax.broadcasted_iota(jnp.int32, u.shape, 1)

    def shifted(a, s):
        return jnp.where(row >= s, pltpu.roll(a, s, axis=0), 0.0)

    sums = []
    acc = ext
    for wdt in POOL_WINDOWS:
        acc = acc + shifted(acc, wdt // 2)
        sums.append(acc[POOL_HALO:])
    grp = lane // POOL_GROUP
    wsum = sums[-1]
    win = jnp.full(u.shape, POOL_WINDOWS[-1], jnp.int32)
    for gi in range(len(POOL_WINDOWS) - 2, -1, -1):
        wsum = jnp.where(grp == gi, sums[gi], wsum)
        win = jnp.where(grp == gi, POOL_WINDOWS[gi], win)
    pos = pos0 + lax.broadcasted_iota(jnp.int32, u.shape, 0)
    cnt = jnp.minimum(pos + 1, win).astype(F32)
    p = wsum / cnt - u
    return (_dot(p.astype(BF16), w_bd) * scale).astype(BF16)


def _inproj_kernel(layer, seq, x_ref, mod_ref, wt_hbm, pw_ref, ps_ref, pool_ref, q_ref, kv_ref, gdn_ref, ba_ref,
                   wt_f32, tail_ref, wt_ref, halo_ref, sem):
    @pl.when(pl.program_id(0) == 0)
    def _():
        n_real = wt_hbm.shape[0]
        pad0 = (n_real // SUBLANES) * SUBLANES
        bulk = pltpu.make_async_copy(wt_hbm.at[pl.ds(0, pad0), layer, :], wt_f32.at[pl.ds(0, pad0)], sem.at[0])
        tail = pltpu.make_async_copy(wt_hbm.at[pl.ds(n_real - SUBLANES, SUBLANES), layer, :], tail_ref, sem.at[1])
        bulk.start()
        tail.start()
        bulk.wait()
        tail.wait()
        row8 = lax.broadcasted_iota(jnp.int32, tail_ref.shape, 0)
        left = n_real - pad0
        wt_f32[pad0:pad0 + SUBLANES, :] = jnp.where(row8 < left, pltpu.roll(tail_ref[...], left, axis=0), 0.0)
        wt_f32[pad0 + SUBLANES:, :] = jnp.zeros((wt_f32.shape[0] - pad0 - SUBLANES, wt_f32.shape[1]), F32)
        for src, n, dst in _IN_RUNS:
            if src >= 0:
                rows = n if src + n < n_real else wt_ref.shape[0] - dst
                wt_ref[dst:dst + rows, :] = wt_f32[src:src + rows, :].astype(BF16)
        q_rows = wt_ref[P_Q[0]:P_Q[1], :].astype(F32) * (HEAD_DIM ** -0.5)
        wt_ref[P_Q[0]:P_Q[1], :] = q_rows.astype(BF16)

    sh = mod_ref[0, 0:1, :]
    sc = mod_ref[0, 1:2, :]
    h = (x_ref[...] * (1.0 + sc) + sh).astype(BF16)

    def mm(rng):
        return _dot_nt(h, wt_ref[rng[0]:rng[1], :])

    tm = x_ref.shape[0]
    pos0 = (pl.program_id(0) * tm) % seq
    u = mm(P_POOL)
    ext = jnp.concatenate([jnp.where(pos0 == 0, 0.0, halo_ref[...]), u], axis=0)
    halo_ref[...] = u[tm - POOL_HALO:, :]
    gdn_ref[...] = mm(P_GDN)
    pool_ref[...] = _pool_tile(ext, u, pos0, pw_ref[...], ps_ref[...])
    q_ref[...] = mm(P_Q).astype(BF16)
    kv_ref[...] = mm(P_KV).astype(BF16)
    ba_ref[...] = mm(P_BA)


def _in_projection(x2d, mod, w_in, pool_w_bd, pool_scale, layer, seq):
    t, d = x2d.shape
    tm = min(IN_TILE, seq)
    assert seq % tm == 0 and tm > POOL_HALO
    widths = [r[1] - r[0] for r in (P_POOL, P_Q, P_KV, P_GDN, P_BA)]
    dtypes = [BF16, BF16, BF16, F32, F32]
    fixed = lambda i: (0, 0)
    return pl.pallas_call(
        functools.partial(_inproj_kernel, layer, seq),
        grid=(t // tm,),
        in_specs=[
            pl.BlockSpec((tm, d), lambda i: (i, 0)),
            pl.BlockSpec((1, 6, d), lambda i: ((i * tm) // seq, 0, 0)),
            pl.BlockSpec(memory_space=pl.ANY),
            pl.BlockSpec((POOL_DIM, POOL_DIM), fixed),
            pl.BlockSpec((1, POOL_DIM), fixed),
        ],
        out_specs=[pl.BlockSpec((tm, w), lambda i: (i, 0)) for w in widths],
        out_shape=[jax.ShapeDtypeStruct((t, w), dt) for w, dt in zip(widths, dtypes)],
        scratch_shapes=[pltpu.VMEM((P_TOTAL, d), F32), pltpu.VMEM((SUBLANES, d), F32),
                        pltpu.VMEM((P_TOTAL, d), BF16), pltpu.VMEM((POOL_HALO, POOL_DIM), F32),
                        pltpu.SemaphoreType.DMA((2,))],
        compiler_params=_cparams(("arbitrary",)),
        name="in_proj_pool",
    )(x2d, mod, jnp.transpose(w_in, (2, 0, 1)), pool_w_bd, pool_scale.reshape(1, POOL_DIM))


def _attn_kernel(sink_ref, q_ref, kvc_ref, kvp_ref, bias_ref, o_ref):
    step = pl.program_id(1)
    qi = lax.broadcasted_iota(jnp.int32, (WINDOW, 2 * WINDOW), 0)
    kj = lax.broadcasted_iota(jnp.int32, (WINDOW, 2 * WINDOW), 1)
    dist = qi + WINDOW - kj
    in_band = (dist >= 0) & (dist < WINDOW)
    lo = lax.broadcasted_iota(jnp.int32, (WINDOW, LANES), 1) < HEAD_DIM
    heads = [(p, half) for p in range(N_ATT_HEADS // 2) for half in range(2)]
    sinks = [sink_ref[p + 3 * half] for p, half in heads]
    for sub in range(ATT_BLOCKS):
        r0 = sub * WINDOW
        prev = kvp_ref[...] if sub == 0 else kvc_ref[r0 - WINDOW:r0, :]
        kv = jnp.concatenate([prev, kvc_ref[r0:r0 + WINDOW, :]], axis=0)
        k = kv[:, :KV_DIM]
        v = kv[:, KV_DIM:]
        valid = in_band & ((kj >= WINDOW) | (step > 0)) if sub == 0 else in_band
        for g0 in range(0, len(heads), ATT_GROUP):
            group = heads[g0:g0 + ATT_GROUP]
            sk_g = sinks[g0:g0 + ATT_GROUP]
            scores = []
            for p, half in group:
                qp = q_ref[r0:r0 + WINDOW, p * LANES:(p + 1) * LANES]
                qm = jnp.where(lo if half == 0 else jnp.logical_not(lo), qp, jnp.zeros_like(qp))
                scores.append(jnp.where(valid, _dot_nt(qm, k) + bias_ref[p + 3 * half], NEG_INF))
            tops = [jnp.maximum(jnp.max(s, axis=-1, keepdims=True), sk) for s, sk in zip(scores, sk_g)]
            probs = [jnp.exp(s - m) for s, m in zip(scores, tops)]
            dens = [jnp.sum(pr, axis=-1, keepdims=True) + jnp.exp(sk - m)
                    for pr, sk, m in zip(probs, sk_g, tops)]
            outs = [_dot(pr.astype(BF16), v) / den for pr, den in zip(probs, dens)]
            for idx in range(0, len(group), 2):
                p = group[idx][0]
                o_ref[r0:r0 + WINDOW, p * LANES:(p + 1) * LANES] = (
                    jnp.where(lo, outs[idx], outs[idx + 1]).astype(BF16))


def _swa_attention(q, kv, bias, sinks, seq):
    t = q.shape[0]
    rows = ATT_BLOCKS * WINDOW
    nblk = seq // rows
    return pl.pallas_call(
        _attn_kernel,
        grid=(t // seq, nblk),
        in_specs=[
            pl.BlockSpec(memory_space=pltpu.SMEM),
            pl.BlockSpec((rows, ATT_DIM), lambda b, n: (b * nblk + n, 0)),
            pl.BlockSpec((rows, 2 * KV_DIM), lambda b, n: (b * nblk + n, 0)),
            pl.BlockSpec((WINDOW, 2 * KV_DIM),
                         lambda b, n: (jnp.maximum((b * nblk + n) * ATT_BLOCKS - 1, 0), 0)),
            pl.BlockSpec((N_ATT_HEADS, WINDOW, 2 * WINDOW), lambda b, n: (0, 0, 0)),
        ],
        out_specs=pl.BlockSpec((rows, ATT_DIM), lambda b, n: (b * nblk + n, 0)),
        out_shape=jax.ShapeDtypeStruct((t, ATT_DIM), BF16),
        compiler_params=_cparams(("arbitrary", "arbitrary")),
        name="swa_attention",
    )(sinks, q, kv, kv, bias)


_GDN_BASE = SUBLANES
_GDN_LEVELS = int(np.log2(GDN_CHUNK // _GDN_BASE))


def _gdn_masks():
    r = np.arange(GDN_SUPER)
    ri, ci = r[:, None], r[None, :]
    same_chunk = (ri // GDN_CHUNK) == (ci // GDN_CHUNK)
    incl = same_chunk & (ri >= ci)
    planes = [incl, ri == ci]
    base = ((ri // _GDN_BASE) == (ci // _GDN_BASE)) & (ri > ci)
    planes.append(base)
    for lvl in range(_GDN_LEVELS):
        small = _GDN_BASE << lvl
        planes.append(((ri // (2 * small)) == (ci // (2 * small))) & ((ri // small) != (ci // small)) & (ri > ci))
    bmask = np.stack(planes).astype(np.float32)
    bmask[2] = -bmask[2]
    negmask = np.where(incl, 0.0, -np.inf).astype(np.float32)
    return negmask, bmask


_GDN_NEGMASK, _GDN_BMASK = _gdn_masks()


def _gdn_kernel(x_ref, halo_ref, ba_ref, cw_ref, alog_ref, dtb_ref, nw_ref, negmask_ref, bmask_ref,
                y_ref, state_ref, xs_ref):
    sc_id = pl.program_id(1)
    rows = GDN_SUPER
    nchunk = rows // GDN_CHUNK
    c_sz = GDN_CHUNK

    @pl.when(sc_id == 0)
    def _():
        state_ref[...] = jnp.zeros_like(state_ref)

    xs_ref[:SUBLANES, :] = jnp.where(sc_id == 0, 0.0, halo_ref[...])
    xs_ref[SUBLANES:, :] = x_ref[...]
    act = []
    for g in range(x_ref.shape[1] // LANES):
        cols = slice(g * LANES, (g + 1) * LANES)
        acc = x_ref[:, cols] * cw_ref[CONV_WIDTH - 1:CONV_WIDTH, cols]
        if g % 4 != 3:
            for s in range(1, CONV_WIDTH):
                acc = acc + (xs_ref[SUBLANES - s:SUBLANES - s + rows, cols]
                             * cw_ref[CONV_WIDTH - 1 - s:CONV_WIDTH - s, cols])
        act.append(acc * _sigmoid(acc))

    negmask = negmask_ref[...]
    tri_incl = bmask_ref[0]
    eye_b = bmask_ref[1]
    base_neg = bmask_ref[2]
    bands = [bmask_ref[3 + lvl] for lvl in range(_GDN_LEVELS)]
    li = lax.broadcasted_iota(jnp.int32, (LANES, LANES), 0)
    lj = lax.broadcasted_iota(jnp.int32, (LANES, LANES), 1)
    half_ones = jnp.where((li // HEAD_DIM) == (lj // HEAD_DIM), 1.0, 0.0).astype(BF16)
    lane_lo = lax.broadcasted_iota(jnp.int32, (rows, LANES), 1) < HEAD_DIM

    ba = ba_ref[...]
    beta_all = _sigmoid(ba)
    sp_in = ba + dtb_ref[...]
    softplus = jnp.maximum(sp_in, 0.0) + jnp.log(1.0 + jnp.exp(-jnp.abs(sp_in)))
    g_all = -jnp.exp(alog_ref[...]) * softplus
    gcum = _dot_hi_exact_rhs_lhs(tri_incl, g_all)
    gcum_t = gcum.T

    heads = range(N_GDN_HEADS)
    lane_hi = jnp.logical_not(lane_lo)
    mk = [lane_lo if h % 2 == 0 else lane_hi for h in heads]
    scale = HEAD_DIM ** -0.5

    xk, xq, gn, gc_col, beta, eg = [], [], [], [], [], []
    for h in heads:
        g = act[4 * (h // 2) + (h % 2)]
        g = g * lax.rsqrt(_dot((g * g).astype(BF16), half_ones) + NORM_EPS)
        gn.append(g)
        xk.append(jnp.where(mk[h], g, 0.0))
        xq.append(jnp.where(mk[h], pltpu.roll(g, HEAD_DIM, axis=1), 0.0) * scale)
        beta.append(beta_all[:, h:h + 1])
        gc_col.append(gcum[:, N_GDN_HEADS + h:N_GDN_HEADS + h + 1])
        eg.append(jnp.exp(gc_col[h]))

    l_b, attn, rhs = [], [], []
    for h in heads:
        gc_row = gcum_t[N_GDN_HEADS + h:N_GDN_HEADS + h + 1, :]
        decay = jnp.exp(gc_col[h] - gc_row + negmask)
        xk_b = xk[h].astype(BF16)
        kk = _dot_nt((xk[h] * beta[h]).astype(BF16), xk_b)
        l_b.append((kk * decay).astype(BF16))
        attn.append((_dot_nt(xq[h].astype(BF16), xk_b) * decay).astype(BF16))
        vv = act[4 * (h // 2) + 2]
        rhs.append(jnp.where(mk[h], gn[h] * eg[h], vv) * beta[h])

    a1 = [l_b[h] * base_neg for h in heads]
    a2 = [_dot(a1[h], a1[h]).astype(BF16) for h in heads]
    a4 = [_dot(a2[h], a2[h]).astype(BF16) for h in heads]
    inv0 = [eye_b + a1[h] for h in heads]
    acc1 = [inv0[h].astype(F32) + _dot(a2[h], inv0[h]) for h in heads]
    inv_b = [(acc1[h] + _dot(a4[h], acc1[h].astype(BF16))).astype(BF16) for h in heads]
    for lvl in range(_GDN_LEVELS - 1):
        mid = [_dot(l_b[h] * bands[lvl], inv_b[h]).astype(BF16) for h in heads]
        inv_b = [inv_b[h] - _dot(inv_b[h], mid[h]).astype(BF16) for h in heads]
    half = [_dot(inv_b[h], rhs[h].astype(BF16)) for h in heads]
    mid = [_dot(l_b[h] * bands[_GDN_LEVELS - 1], half[h].astype(BF16)) for h in heads]
    sol = [half[h] - _dot(inv_b[h], mid[h].astype(BF16)) for h in heads]

    lane_lo_s = lax.broadcasted_iota(jnp.int32, (LANES, LANES), 1) < HEAD_DIM
    mk_s = [lane_lo_s if h % 2 == 0 else jnp.logical_not(lane_lo_s) for h in heads]
    sol_b = [sol[h].astype(BF16) for h in heads]
    attn_sol = [_dot(attn[h], sol_b[h]) for h in heads]
    q_eff = [(xq[h] * eg[h] - jnp.where(mk[h], attn_sol[h], 0.0)).astype(BF16) for h in heads]
    o_free = [jnp.where(mk[h], 0.0, attn_sol[h]) for h in heads]
    kw = [[] for _ in heads]
    ku = [[] for _ in heads]
    cdec = [[] for _ in heads]
    for c in range(nchunk):
        r0 = c * c_sz
        for h in heads:
            glast = gcum[r0 + c_sz - 1:r0 + c_sz, N_GDN_HEADS + h:N_GDN_HEADS + h + 1]
            kd_t = (xk[h][r0:r0 + c_sz] * jnp.exp(glast - gc_col[h][r0:r0 + c_sz])).T
            both = _dot(kd_t.astype(BF16), sol_b[h][r0:r0 + c_sz])
            kw[h].append(jnp.where(mk_s[h], both, 0.0).astype(BF16))
            ku[h].append(jnp.where(mk_s[h], 0.0, both))
            cdec[h].append(jnp.exp(glast))
    st = [state_ref[h] for h in heads]
    o_parts = [[] for _ in heads]
    for c in range(nchunk):
        r0 = c * c_sz
        for h in heads:
            lhs = jnp.concatenate([kw[h][c], q_eff[h][r0:r0 + c_sz]], axis=0)
            prod = _dot(lhs, st[h].astype(BF16))
            o_parts[h].append(prod[LANES:] + o_free[h][r0:r0 + c_sz])
            st[h] = st[h] * cdec[h][c] + ku[h][c] - prod[:LANES]
    for h in heads:
        state_ref[h] = st[h]

    for p in range(N_GDN_HEADS // 2):
        o_pair = [jnp.concatenate(o_parts[h], axis=0) for h in (2 * p, 2 * p + 1)]
        o = jnp.where(lane_lo, o_pair[1], o_pair[0])
        ms = _dot((o * o).astype(BF16), half_ones) * (1.0 / HEAD_DIM)
        zz = act[4 * p + 3]
        y = o * lax.rsqrt(ms + NORM_EPS) * nw_ref[...] * zz
        y_ref[:, p * LANES:(p + 1) * LANES] = y.astype(BF16)


def _dot_hi_exact_rhs_lhs(m_bf16, x):
    hi, lo = _split_bf16(x)
    return _dot(m_bf16, hi) + _dot(m_bf16, lo)


def _gdn_mixer(gdn, ba, conv_p, alog_v, dtb_v, nw_v, seq):
    t, c = gdn.shape
    rows = GDN_SUPER
    nsc = seq // rows
    hb = rows // SUBLANES
    return pl.pallas_call(
        _gdn_kernel,
        grid=(t // seq, nsc),
        in_specs=[
            pl.BlockSpec((rows, c), lambda b, s: (b * nsc + s, 0)),
            pl.BlockSpec((SUBLANES, c), lambda b, s: (jnp.maximum((b * nsc + s) * hb - 1, 0), 0)),
            pl.BlockSpec((rows, LANES), lambda b, s: (b * nsc + s, 0)),
            pl.BlockSpec((CONV_WIDTH, c), lambda b, s: (0, 0)),
            pl.BlockSpec((1, LANES), lambda b, s: (0, 0)),
            pl.BlockSpec((1, LANES), lambda b, s: (0, 0)),
            pl.BlockSpec((1, LANES), lambda b, s: (0, 0)),
            pl.BlockSpec((rows, rows), lambda b, s: (0, 0)),
            pl.BlockSpec((3 + _GDN_LEVELS, rows, rows), lambda b, s: (0, 0, 0)),
        ],
        out_specs=pl.BlockSpec((rows, GDN_DIM), lambda b, s: (b * nsc + s, 0)),
        out_shape=jax.ShapeDtypeStruct((t, GDN_DIM), BF16),
        scratch_shapes=[pltpu.VMEM((N_GDN_HEADS, LANES, LANES), F32), pltpu.VMEM((rows + SUBLANES, c), F32)],
        compiler_params=_cparams(("arbitrary", "arbitrary")),
        name="gdn_mixer",
    )(gdn, gdn, ba, conv_p, alog_v, dtb_v, nw_v, jnp.asarray(_GDN_NEGMASK), jnp.asarray(_GDN_BMASK, BF16))


def _route_tile(logits, before, carry_ref, live):
    shape = logits.shape
    lane = lax.broadcasted_iota(jnp.int32, shape, 1).astype(F32)
    work = logits
    vals, idxs = [], []
    for _k in range(TOP_K):
        m = jnp.max(work, axis=-1, keepdims=True)
        idx = jnp.min(jnp.where(work == m, lane, float(LANES)), axis=-1, keepdims=True)
        vals.append(m)
        idxs.append(idx)
        work = jnp.where(lane == idx, -jnp.inf, work)
    exps = [jnp.exp(v - vals[0]) for v in vals]
    den = exps[0] + exps[1] + exps[2] + exps[3]
    onehots = [lane == idx for idx in idxs]
    member = jnp.zeros(shape, F32)
    for oh in onehots:
        member = member + jnp.where(oh, 1.0, 0.0)
    rank = _dot(before, member.astype(BF16)) + carry_ref[...]
    carry_ref[...] = carry_ref[...] + live * jnp.sum(member, axis=0, keepdims=True)
    info = jnp.zeros(shape, F32)
    for k in range(TOP_K):
        rank_k = jnp.sum(jnp.where(onehots[k], rank, 0.0), axis=-1, keepdims=True)
        info = jnp.where(lane == float(k), idxs[k], info)
        info = jnp.where(lane == float(TOP_K + k), rank_k, info)
        info = jnp.where(lane == float(2 * TOP_K + k), exps[k] / den, info)
    return info


def _outproj_kernel(x_ref, mod_ref, yp_ref, ya_ref, yg_ref, wp_ref, wa_ref, wg_ref, lng_ref, lnb_ref,
                    rw_ref, rb_ref, before_ref, x1_ref, h2_ref, info_ref, er_ref, cnt_ref,
                    carry_ref, logit_s):
    step = pl.program_id(0)

    @pl.when(step == 0)
    def _():
        carry_ref[...] = jnp.zeros_like(carry_ref)
        logit_s[...] = jnp.zeros_like(logit_s)

    part = x_ref.shape[0] // OUT_SPLIT
    halves = [slice(j * part, (j + 1) * part) for j in range(OUT_SPLIT)]
    y = [_dot(yp_ref[r, :], wp_ref[...]) + _dot(ya_ref[r, :], wa_ref[...]) + _dot(yg_ref[r, :], wg_ref[...])
         for r in halves]
    live = jnp.where(step > 0, 1.0, 0.0)
    info = _route_tile(logit_s[...], before_ref[...], carry_ref, live)
    info_ref[...] = info
    er_ref[...] = info.T[:SUBLANES]
    cnt_ref[...] = carry_ref[...]
    g1 = mod_ref[0, 2:3, :]
    sh2 = mod_ref[0, 3:4, :]
    sc2 = mod_ref[0, 4:5, :]
    for r, y_r in zip(halves, y):
        x1 = _layer_norm(DEEPNORM_ALPHA * x_ref[r, :] + g1 * y_r, lng_ref[...], lnb_ref[...])
        x1_ref[r, :] = x1
        h2 = x1 * (1.0 + sc2) + sh2
        h2_ref[r, :] = _pack_bf16_pairs(h2)
        logit_s[r, :] = _dot(h2.astype(BF16), rw_ref[...]) + rb_ref[...]


def _out_projection(x2d, mod, yp, ya, yg, w_out_p, ln_g, ln_b, rw, rb, seq):
    t, d = x2d.shape
    tm = ROW_TILE
    wp = w_out_p[:POOL_DIM]
    wa = w_out_p[POOL_DIM:POOL_DIM + ATT_DIM]
    wg = w_out_p[POOL_DIM + ATT_DIM:]
    last = t // tm - 1
    row = lambda i: (jnp.minimum(i, last), 0)
    routed = lambda i: (jnp.maximum(i - 1, 0), 0)
    fixed = lambda i: (0, 0)
    return pl.pallas_call(
        _outproj_kernel,
        grid=(t // tm + 1,),
        in_specs=[
            pl.BlockSpec((tm, d), row),
            pl.BlockSpec((1, 6, d), lambda i: ((jnp.minimum(i, last) * tm) // seq, 0, 0)),
            pl.BlockSpec((tm, POOL_DIM), row),
            pl.BlockSpec((tm, ATT_DIM), row),
            pl.BlockSpec((tm, GDN_DIM), row),
            pl.BlockSpec((POOL_DIM, d), fixed),
            pl.BlockSpec((ATT_DIM, d), fixed),
            pl.BlockSpec((GDN_DIM, d), fixed),
            pl.BlockSpec((1, d), fixed),
            pl.BlockSpec((1, d), fixed),
            pl.BlockSpec((d, LANES), fixed),
            pl.BlockSpec((1, LANES), fixed),
            pl.BlockSpec((tm, tm), fixed),
        ],
        out_specs=[pl.BlockSpec((tm, d), row), pl.BlockSpec((tm, d // 2), row), pl.BlockSpec((tm, LANES), routed),
                   pl.BlockSpec((SUBLANES, tm), lambda i: (0, jnp.maximum(i - 1, 0))),
                   pl.BlockSpec((1, LANES), fixed)],
        out_shape=[jax.ShapeDtypeStruct((t, d), F32), jax.ShapeDtypeStruct((t, d // 2), jnp.int32),
                   jax.ShapeDtypeStruct((t, LANES), F32), jax.ShapeDtypeStruct((SUBLANES, t), F32),
                   jax.ShapeDtypeStruct((1, LANES), F32)],
        scratch_shapes=[pltpu.VMEM((1, LANES), F32), pltpu.VMEM((tm, LANES), F32)],
        compiler_params=_cparams(("arbitrary",)),
        name="out_proj_ln_route",
    )(x2d, mod, yp, ya, yg, wp, wa, wg, ln_g.reshape(1, d), ln_b.reshape(1, d), rw, rb,
      jnp.tril(jnp.ones((tm, tm), BF16), -1))


def _slot_kernel(er_ref, cnt_ref, dest_ref, pcum_ref):
    cnt = jnp.broadcast_to(cnt_ref[...], (SUBLANES, LANES))
    padded = jnp.floor((cnt + float(EXPERT_BLOCK - 1)) * (1.0 / EXPERT_BLOCK)) * float(EXPERT_BLOCK)
    lane8 = lax.broadcasted_iota(jnp.int32, (SUBLANES, LANES), 1)
    acc = padded
    step = 1
    while step < LANES:
        acc = acc + jnp.where(lane8 >= step, pltpu.roll(acc, step, axis=1), 0.0)
        step *= 2
    pcum_ref[...] = acc[:1].astype(jnp.int32)
    pstart = acc - padded

    er = er_ref[...]
    start = jnp.zeros(er.shape, F32)
    for e in range(N_EXPERTS):
        offset = jnp.sum(jnp.where(lane8 == e, pstart, 0.0), axis=-1, keepdims=True)
        start = jnp.where(er == float(e), offset, start)
    row = lax.broadcasted_iota(jnp.int32, er.shape, 0)
    slots = jnp.where(row < TOP_K, start + pltpu.roll(er, TOP_K, axis=0), 0.0)
    dest_ref[...] = slots.astype(jnp.int32)


def _slots(er, cnt):
    t = er.shape[1]
    return pl.pallas_call(
        _slot_kernel,
        grid=(1,),
        in_specs=[pl.BlockSpec((SUBLANES, t), lambda i: (0, 0)), pl.BlockSpec((1, LANES), lambda i: (0, 0))],
        out_specs=[pl.BlockSpec((SUBLANES, t), lambda i: (0, 0)), pl.BlockSpec((1, LANES), lambda i: (0, 0))],
        out_shape=[jax.ShapeDtypeStruct((SUBLANES, t), jnp.int32), jax.ShapeDtypeStruct((1, LANES), jnp.int32)],
        compiler_params=_cparams(("arbitrary",)),
        name="moe_slots",
    )(er, cnt)


def _expert_kernel(e0, be_ref, nxt_ref, val_ref, nu_ref, x_ref, wup_hbm, bup_ref, wdn_hbm, bdn_ref, y_ref,
                   wup_st, wdn_st, wup_bf, wdn_bf, sems):
    i = pl.program_id(0)
    e = be_ref[i]
    prev = be_ref[jnp.maximum(i - 1, 0)]
    used = i < nu_ref[0]

    def weight_copies(expert):
        return (pltpu.make_async_copy(wup_hbm.at[e0 + expert], wup_st, sems.at[0]),
                pltpu.make_async_copy(wdn_hbm.at[e0 + expert], wdn_st, sems.at[1]))

    @pl.when(i == 0)
    def _():
        for cp in weight_copies(e):
            cp.start()

    @pl.when(used & ((i == 0) | (e != prev)))
    def _():
        for cp in weight_copies(e):
            cp.wait()
        wup_bf[...] = wup_st[...].astype(BF16)
        wdn_bf[...] = wdn_st[...].astype(BF16)

        @pl.when(nxt_ref[i] >= 0)
        def _():
            for cp in weight_copies(nxt_ref[i]):
                cp.start()

    def ffn(rows):
        xb = _unpack_bf16_pairs(x_ref[:rows, :]).astype(BF16)
        hb = _dot(xb, wup_bf[...]) + bup_ref[0]
        x_glu = jnp.minimum(hb[:, :EXPERT_DIM], SWIGLU_LIMIT)
        x_lin = jnp.clip(hb[:, EXPERT_DIM:], -SWIGLU_LIMIT, SWIGLU_LIMIT)
        act = x_glu * _sigmoid(SWIGLU_ALPHA * x_glu) * (x_lin + 1.0)
        y = _dot(act.astype(BF16), wdn_bf[...]) + bdn_ref[0]
        y_ref[:rows, :] = _pack_bf16_pairs(y)

    bm = x_ref.shape[0]
    pieces = (val_ref[i] + (EXPERT_GRAIN - 1)) // EXPERT_GRAIN
    for n_piece in range(1, bm // EXPERT_GRAIN + 1):
        rows_used = n_piece * EXPERT_GRAIN

        @pl.when(used & (pieces == n_piece))
        def _(rows_used=rows_used):
            ffn(rows_used)
            if rows_used < bm:
                y_ref[rows_used:, :] = jnp.zeros((bm - rows_used, y_ref.shape[1]), y_ref.dtype)

    @pl.when(i >= nu_ref[0])
    def _():
        y_ref[...] = jnp.zeros_like(y_ref)


def _expert_ffn(xbuf, block_e, next_e, valid, n_used, w_up, b_up, w_down, b_down, layer):
    p, dh = xbuf.shape
    d = 2 * dh
    bm = EXPERT_BLOCK
    ne, _, n_up = w_up.shape
    e0 = layer * N_EXPERTS
    grid_spec = pltpu.PrefetchScalarGridSpec(
        num_scalar_prefetch=4,
        grid=(p // bm,),
        in_specs=[
            pl.BlockSpec((bm, dh), lambda i, be, nx, vl, nu: (jnp.minimum(i, nu[0] - 1), 0)),
            pl.BlockSpec(memory_space=pl.ANY),
            pl.BlockSpec((1, 1, n_up), lambda i, be, nx, vl, nu: (e0 + be[i], 0, 0)),
            pl.BlockSpec(memory_space=pl.ANY),
            pl.BlockSpec((1, 1, d), lambda i, be, nx, vl, nu: (e0 + be[i], 0, 0)),
        ],
        out_specs=pl.BlockSpec((bm, dh), lambda i, be, nx, vl, nu: (i, 0)),
        scratch_shapes=[pltpu.VMEM((d, n_up), F32), pltpu.VMEM((EXPERT_DIM, d), F32),
                        pltpu.VMEM((d, n_up), BF16), pltpu.VMEM((EXPERT_DIM, d), BF16),
                        pltpu.SemaphoreType.DMA((2,))],
    )
    return pl.pallas_call(
        functools.partial(_expert_kernel, e0),
        grid_spec=grid_spec,
        out_shape=jax.ShapeDtypeStruct((p, dh), jnp.int32),
        compiler_params=_cparams(("arbitrary",)),
        name="expert_ffn",
    )(block_e, next_e, valid, n_used, xbuf, w_up, b_up, w_down, b_down)


def _combine_kernel(x1_ref, mod_ref, yg_hbm, info_ref, lng_ref, lnb_ref, o_ref, ring, sems):
    s = pl.program_id(0)
    n = pl.num_programs(0)
    tm = x1_ref.shape[0]

    def fetch(step):
        slot = step % COMBINE_SLOTS
        return pltpu.make_async_copy(yg_hbm.at[:, pl.ds(pl.multiple_of(step * tm, tm), tm), :], ring.at[slot],
                                     sems.at[slot])

    @pl.when(s == 0)
    def _():
        fetch(0).start()

        @pl.when(n > 1)
        def _():
            fetch(1).start()

    @pl.when(s + 2 < n)
    def _():
        fetch(s + 2).start()

    fetch(s).wait()
    yg_ref = ring.at[s % COMBINE_SLOTS]
    info = info_ref[...]
    y = jnp.zeros(x1_ref.shape, F32)
    for k in range(TOP_K):
        gate = info[:, 2 * TOP_K + k:2 * TOP_K + k + 1]
        y = y + gate * _unpack_bf16_pairs(yg_ref[k])
    g2 = mod_ref[0, 5:6, :]
    o_ref[...] = _layer_norm(DEEPNORM_ALPHA * x1_ref[...] + g2 * y, lng_ref[...], lnb_ref[...])


def _combine(x1, mod, yg, info, ln_g, ln_b, seq):
    t, d = x1.shape
    tm = min(IN_TILE, seq)
    row = lambda i: (i, 0)
    fixed = lambda i: (0, 0)
    return pl.pallas_call(
        _combine_kernel,
        grid=(t // tm,),
        in_specs=[
            pl.BlockSpec((tm, d), row),
            pl.BlockSpec((1, 6, d), lambda i: ((i * tm) // seq, 0, 0)),
            pl.BlockSpec(memory_space=pl.ANY),
            pl.BlockSpec((tm, LANES), row),
            pl.BlockSpec((1, d), fixed),
            pl.BlockSpec((1, d), fixed),
        ],
        out_specs=pl.BlockSpec((tm, d), row),
        out_shape=jax.ShapeDtypeStruct((t, d), F32),
        scratch_shapes=[pltpu.VMEM((COMBINE_SLOTS, TOP_K, tm, d // 2), jnp.int32),
                        pltpu.SemaphoreType.DMA((COMBINE_SLOTS,))],
        compiler_params=_cparams(("arbitrary",)),
        name="moe_combine_ln",
    )(x1, mod, yg, info, ln_g.reshape(1, d), ln_b.reshape(1, d))


def _sc_workers():
    info = plsc.get_sparse_core_info()
    return info.num_cores, info.num_cores * info.num_subcores


def _sc_scatter_rows(rows, idx, n_out):
    t, w = rows.shape
    kk = idx.shape[0]
    n_cores, n_workers = _sc_workers()
    ch = SC_CHUNK
    assert t % (2 * n_workers * ch) == 0
    n_chunk = t // (n_workers * ch)
    idx_c = jnp.transpose(idx.reshape(kk, t // ch, ch), (1, 0, 2))

    @functools.partial(
        pl.kernel,
        mesh=plsc.VectorSubcoreMesh(core_axis_name="c", subcore_axis_name="s"),
        out_type=jax.ShapeDtypeStruct((n_out, w), rows.dtype),
        scratch_types=[pltpu.VMEM((2, kk, ch), jnp.int32), pltpu.VMEM((2, ch, w), rows.dtype),
                       pltpu.SemaphoreType.DMA((2,)), pltpu.SemaphoreType.DMA((2,))],
        name="sc_dispatch_scatter",
    )
    def scatter_kernel(rows_hbm, idx_hbm, out_hbm, idx_v, rows_v, load_sem, scat_sem):
        base = (lax.axis_index("s") * n_cores + lax.axis_index("c")) * n_chunk

        def load(j, b):
            return pltpu.make_async_copy(rows_hbm.at[pl.ds((base + j) * ch, ch)], rows_v.at[b], load_sem.at[b])

        def scatters(b):
            return [pltpu.make_async_copy(rows_v.at[b], out_hbm.at[idx_v.at[b, q]], scat_sem.at[b])
                    for q in range(kk)]

        pltpu.sync_copy(idx_hbm.at[base], idx_v.at[0])
        load(0, 0).start()

        @pl.loop(0, n_chunk, step=2)
        def _(j0):
            for b in range(2):
                j = j0 + b
                other = 1 - b

                @pl.when(j >= 1)
                def _():
                    for cp in scatters(other):
                        cp.wait()

                @pl.when(j + 1 < n_chunk)
                def _():
                    pltpu.sync_copy(idx_hbm.at[base + j + 1], idx_v.at[other])
                    load(j + 1, other).start()

                load(j, b).wait()
                for cp in scatters(b):
                    cp.start()

        for cp in scatters((n_chunk - 1) % 2):
            cp.wait()

    return scatter_kernel(rows, idx_c)


def _sc_gather_rows(table, idx):
    m = idx.shape[0]
    w = table.shape[1]
    n_cores, n_workers = _sc_workers()
    ch = SC_CHUNK
    assert m % (2 * n_workers * ch) == 0
    n_chunk = m // (n_workers * ch)
    idx_c = idx.reshape(m // ch, 1, ch)

    @functools.partial(
        pl.kernel,
        mesh=plsc.VectorSubcoreMesh(core_axis_name="c", subcore_axis_name="s"),
        out_type=jax.ShapeDtypeStruct((m, w), table.dtype),
        scratch_types=[pltpu.VMEM((2, 1, ch), jnp.int32), pltpu.VMEM((2, ch, w), table.dtype),
                       pltpu.SemaphoreType.DMA((2,)), pltpu.SemaphoreType.DMA((2,))],
        name="sc_combine_gather",
    )
    def gather_kernel(table_hbm, idx_hbm, out_hbm, idx_v, rows_v, gather_sem, write_sem):
        base = (lax.axis_index("s") * n_cores + lax.axis_index("c")) * n_chunk

        def gather(b):
            return pltpu.make_async_copy(table_hbm.at[idx_v.at[b, 0]], rows_v.at[b], gather_sem.at[b])

        def write(j, b):
            return pltpu.make_async_copy(rows_v.at[b], out_hbm.at[pl.ds((base + j) * ch, ch)], write_sem.at[b])

        pltpu.sync_copy(idx_hbm.at[base], idx_v.at[0])
        gather(0).start()

        @pl.loop(0, n_chunk, step=2)
        def _(j0):
            for b in range(2):
                j = j0 + b
                other = 1 - b

                @pl.when(j >= 1)
                def _():
                    write(j - 1, other).wait()

                @pl.when(j + 1 < n_chunk)
                def _():
                    pltpu.sync_copy(idx_hbm.at[base + j + 1], idx_v.at[other])
                    gather(other).start()

                gather(b).wait()
                write(j, b).start()

        write(n_chunk - 1, (n_chunk - 1) % 2).wait()

    return gather_kernel(table, idx_c)


def _lane_vector(vals, offset):
    return jnp.zeros((1, LANES), F32).at[0, offset:offset + vals.shape[0]].set(vals.astype(F32))


def _moe(h2, info, er, cnt, x1, mod, ln_g, ln_b, w_up, b_up, w_down, b_down, layer, seq):
    t, dh = h2.shape
    a = t * TOP_K
    bm = EXPERT_BLOCK
    slots, pcum_v = _slots(er, cnt)
    pcum = pcum_v[0, :N_EXPERTS]
    dest = slots[:TOP_K]
    n_blocks = -(-a // bm) + N_EXPERTS
    starts = jnp.arange(n_blocks, dtype=jnp.int32) * bm
    block_e = jnp.minimum(jnp.sum(pcum[None, :] <= starts[:, None], axis=1), N_EXPERTS - 1).astype(jnp.int32)
    n_used = (pcum[-1] // bm).astype(jnp.int32).reshape(1)
    later = block_e[None, :] > block_e[:, None]
    group_end = n_blocks - jnp.sum(later, axis=1)
    next_e = jnp.min(jnp.where(later, block_e[None, :], N_EXPERTS), axis=1)
    next_e = jnp.where(group_end < n_used[0], next_e, -1).astype(jnp.int32)
    counts = cnt[0, :N_EXPERTS].astype(jnp.int32)
    pstart = pcum - ((counts + bm - 1) // bm) * bm
    mine = block_e[:, None] == jnp.arange(N_EXPERTS, dtype=jnp.int32)[None, :]
    count_b = jnp.sum(jnp.where(mine, counts[None, :], 0), axis=1)
    pstart_b = jnp.sum(jnp.where(mine, pstart[None, :], 0), axis=1)
    valid = jnp.clip(count_b - (starts - pstart_b), 0, bm).astype(jnp.int32)
    xbuf = _sc_scatter_rows(h2, dest, n_blocks * bm)
    ybuf = _expert_ffn(xbuf, block_e, next_e, valid, n_used, w_up, b_up, w_down, b_down, layer)
    yg = _sc_gather_rows(ybuf, dest.reshape(a)).reshape(TOP_K, t, dh)
    return _combine(x1, mod, yg, info, ln_g, ln_b, seq)


def kernel(x, c, rel_bias, w_in, w_out, w_ada, b_ada, ln1_g, ln1_b, ln2_g, ln2_b, pool_w, pool_scale,
           attn_sinks, conv_w, gdn_a_log, gdn_dt_bias, gdn_norm_w, router_w, router_b,
           exp_w_up, exp_b_up, exp_w_down, exp_b_down):
    bsz, seq, d = x.shape
    depth = w_in.shape[0]
    t = bsz * seq
    assert d == D_MODEL and w_in.shape[2] == IN_DIM and depth == DEPTH
    assert seq % GDN_SUPER == 0 and seq % (ATT_BLOCKS * WINDOW) == 0
    assert t % ROW_TILE == 0 and seq % ROW_TILE == 0

    mod_all = _modulation(c, w_ada, b_ada).reshape(depth, bsz, 6, d)
    bias = _band_bias(rel_bias)

    w_up_all = exp_w_up.reshape((depth * N_EXPERTS,) + exp_w_up.shape[2:])
    b_up_all = exp_b_up.reshape(depth * N_EXPERTS, 1, exp_b_up.shape[2])
    w_down_all = exp_w_down.reshape((depth * N_EXPERTS,) + exp_w_down.shape[2:])
    b_down_all = exp_b_down.reshape(depth * N_EXPERTS, 1, exp_b_down.shape[2])

    x2d = x.reshape(t, d)
    for l in range(depth):
        mod = mod_all[l]
        w_out_p = _take_static(w_out[l], _OUT_PERM, 0).astype(BF16)
        ident = jnp.zeros((CONV_WIDTH, 1), F32).at[CONV_WIDTH - 1, 0].set(1.0)
        conv_p = jnp.where(jnp.asarray(_GDN_CONV_SRC >= 0), _take_cols(conv_w[l].astype(F32), _GDN_CONV_SRC),
                           ident)
        pool_bd = jnp.zeros((POOL_DIM, POOL_DIM), F32)
        for gi in range(len(POOL_WINDOWS)):
            sl = slice(gi * POOL_GROUP, (gi + 1) * POOL_GROUP)
            pool_bd = pool_bd.at[sl, sl].set(pool_w[l, gi].astype(F32))
        alog_v = _lane_vector(gdn_a_log[l], N_GDN_HEADS)
        dtb_v = _lane_vector(gdn_dt_bias[l], N_GDN_HEADS)
        nw_v = jnp.tile(gdn_norm_w[l].astype(F32), 2).reshape(1, LANES)
        rw = jnp.zeros((d, LANES), BF16).at[:, :N_EXPERTS].set(router_w[l].astype(BF16))
        rb = jnp.full((1, LANES), NEG_INF, F32).at[0, :N_EXPERTS].set(router_b[l].astype(F32))

        y_pool, aq, akv, gdn, ba = _in_projection(x2d, mod, w_in.astype(F32), pool_bd.astype(BF16),
                                                  pool_scale[l].astype(F32), l, seq)
        y_att = _swa_attention(aq, akv, bias, attn_sinks[l].astype(F32), seq)
        y_gdn = _gdn_mixer(gdn, ba, conv_p, alog_v, dtb_v, nw_v, seq)
        x1, h2, info, er, cnt = _out_projection(x2d, mod, y_pool, y_att, y_gdn, w_out_p, ln1_g[l], ln1_b[l],
                                            rw, rb, seq)
        x2d = _moe(h2, info, er, cnt, x1, mod, ln2_g[l], ln2_b[l], w_up_all, b_up_all, w_down_all, b_down_all,
                   l, seq)
    return x2d.reshape(bsz, seq, d)
```

```python
import functools

import numpy as np
import jax
import jax.numpy as jnp
from jax import lax
from jax.experimental import pallas as pl
from jax.experimental.pallas import tpu as pltpu
from jax.experimental.pallas import tpu_sc as plsc

F32 = jnp.float32
BF16 = jnp.bfloat16

D_MODEL = 1024
HEAD_DIM = 64
POOL_DIM = 256
POOL_WINDOWS = (2, 4, 8, 16)
POOL_GROUP = 64
POOL_HALO = 16
N_ATT_HEADS = 6
N_KV_HEADS = 2
ATT_DIM = 384
KV_DIM = 128
WINDOW = 128
N_BUCKETS = 32
MAX_DISTANCE = 128
N_GDN_HEADS = 6
GDN_DIM = 384
CONV_WIDTH = 4
GDN_CHUNK = 64
N_EXPERTS = 32
TOP_K = 4
EXPERT_DIM = 1024
SWIGLU_ALPHA = 1.702
SWIGLU_LIMIT = 7.0
DEPTH = 2
DEEPNORM_ALPHA = (2 * DEPTH) ** 0.25
LN_EPS = 1e-5
NORM_EPS = 1e-6
NEG_INF = -1e30

LANES = 128
SUBLANES = 8
VMEM_LIMIT = 56 * 1024 * 1024

MOD_TILE = 2048
ROW_TILE = 512
OUT_SPLIT = 2
IN_TILE = 1024
ATT_BLOCKS = 8
ATT_GROUP = 6
GDN_SUPER = 256
EXPERT_BLOCK = 1024
EXPERT_GRAIN = 256
SC_CHUNK = 64

_OFF_AQ = POOL_DIM
_OFF_AK = _OFF_AQ + ATT_DIM
_OFF_AV = _OFF_AK + KV_DIM
_OFF_GQ = _OFF_AV + KV_DIM
_OFF_GK = _OFF_GQ + GDN_DIM
_OFF_GV = _OFF_GK + GDN_DIM
_OFF_GZ = _OFF_GV + GDN_DIM
_OFF_GB = _OFF_GZ + GDN_DIM
_OFF_GA = _OFF_GB + N_GDN_HEADS
IN_DIM = _OFF_GA + N_GDN_HEADS

P_POOL = (0, POOL_DIM)
P_Q = (P_POOL[1], P_POOL[1] + ATT_DIM)
P_KV = (P_Q[1], P_Q[1] + 2 * KV_DIM)
P_GDN = (P_KV[1], P_KV[1] + 4 * GDN_DIM)
P_BA = (P_GDN[1], P_GDN[1] + LANES)
P_TOTAL = P_BA[1]


def _head_cols(off, h):
    return list(range(off + HEAD_DIM * h, off + HEAD_DIM * (h + 1)))


def _build_in_perm():
    cols = list(range(POOL_DIM))
    for p in range(N_ATT_HEADS // 2):
        cols += _head_cols(_OFF_AQ, p) + _head_cols(_OFF_AQ, p + 3)
    cols += list(range(_OFF_AK, _OFF_AK + 2 * KV_DIM))
    gdn_src = []
    for p in range(N_GDN_HEADS // 2):
        e, o = 2 * p, 2 * p + 1
        grp = (_head_cols(_OFF_GK, e) + _head_cols(_OFF_GQ, e)
               + _head_cols(_OFF_GQ, o) + _head_cols(_OFF_GK, o)
               + _head_cols(_OFF_GV, o) + _head_cols(_OFF_GV, e)
               + _head_cols(_OFF_GZ, o) + _head_cols(_OFF_GZ, e))
        cols += grp
        gdn_src += [c - _OFF_GQ if c < _OFF_GZ else -1 for c in grp]
    cols += list(range(_OFF_GB, _OFF_GB + 2 * N_GDN_HEADS))
    cols += [-1] * (LANES - 2 * N_GDN_HEADS)
    assert len(cols) == P_TOTAL
    return np.asarray(cols, np.int32), np.asarray(gdn_src, np.int32)


_IN_PERM, _GDN_CONV_SRC = _build_in_perm()


def _build_out_perm():
    rows = list(range(POOL_DIM))
    for p in range(N_ATT_HEADS // 2):
        rows += _head_cols(POOL_DIM, p) + _head_cols(POOL_DIM, p + 3)
    for p in range(N_GDN_HEADS // 2):
        rows += _head_cols(POOL_DIM + ATT_DIM, 2 * p + 1) + _head_cols(POOL_DIM + ATT_DIM, 2 * p)
    return np.asarray(rows, np.int32)


_OUT_PERM = _build_out_perm()


def _t5_bucket_line():
    n = np.maximum(2 * WINDOW - 1 - np.arange(3 * WINDOW - 1), 0)
    max_exact = N_BUCKETS // 2
    nf = np.maximum(n, 1).astype(np.float32)
    large = max_exact + (np.log(nf / max_exact) / np.float32(np.log(MAX_DISTANCE / max_exact))
                         * (N_BUCKETS - max_exact)).astype(np.int32)
    large = np.minimum(large, N_BUCKETS - 1)
    return np.where(n < max_exact, n, large).astype(np.int32)


_BUCKET_LINE = _t5_bucket_line()


def _band_bias(rel_bias):
    n_line = 3 * WINDOW - 1
    line = jnp.take(rel_bias.astype(F32), jnp.asarray(_BUCKET_LINE), axis=0).T
    heads = line.shape[0]
    padded = jnp.concatenate([line, jnp.zeros((heads, 1), F32)], axis=1)
    skew = jnp.tile(padded, (1, WINDOW))[:, :WINDOW * n_line].reshape(heads, WINDOW, n_line)
    return skew[:, :, WINDOW - 1:3 * WINDOW - 1]


def _take_static(w, perm, axis):
    parts = []
    start = 0
    for i in range(1, len(perm) + 1):
        run_ends = (i == len(perm) or ((perm[i] < 0) != (perm[i - 1] < 0))
                    or (perm[i] >= 0 and perm[i] != perm[i - 1] + 1))
        if run_ends:
            if perm[start] < 0:
                shape = list(w.shape)
                shape[axis] = i - start
                parts.append(jnp.zeros(shape, w.dtype))
            else:
                parts.append(lax.slice_in_dim(w, int(perm[start]), int(perm[start]) + (i - start), axis=axis))
            start = i
    return jnp.concatenate(parts, axis=axis)


def _take_cols(w, perm):
    return _take_static(w, perm, w.ndim - 1)


def _split_bf16(x):
    hi = x.astype(BF16)
    lo = (x - hi.astype(F32)).astype(BF16)
    return hi, lo


def _pack_bf16_pairs(x):
    n = x.shape[1] // 2
    bits = pltpu.bitcast(x.astype(BF16).astype(F32), jnp.int32)
    return lax.shift_right_logical(bits[:, :n], 16) | bits[:, n:]


def _unpack_bf16_pairs(u):
    lo = pltpu.bitcast(lax.shift_left(u, 16), F32)
    hi = pltpu.bitcast(u & jnp.int32(-65536), F32)
    return jnp.concatenate([lo, hi], axis=1)


def _dot(a, b):
    return jnp.dot(a, b, preferred_element_type=F32)


def _dot_nt(a, b):
    return lax.dot_general(a, b, (((1,), (1,)), ((), ())), preferred_element_type=F32)


def _sigmoid(x):
    return 1.0 / (1.0 + jnp.exp(-x))


def _layer_norm(r, g, b):
    mu = jnp.mean(r, axis=-1, keepdims=True)
    d = r - mu
    var = jnp.mean(d * d, axis=-1, keepdims=True)
    return d * lax.rsqrt(var + LN_EPS) * g + b


def _cparams(sem):
    return pltpu.CompilerParams(dimension_semantics=sem, vmem_limit_bytes=VMEM_LIMIT)


def _mod_kernel(c_ref, w_ref, b_ref, o_ref):
    c = c_ref[...]
    ca = c * _sigmoid(c)
    o_ref[0] = _dot(ca.astype(BF16), w_ref[0].astype(BF16)) + b_ref[0]


def _modulation(c, w_ada, b_ada):
    depth, d, n = w_ada.shape
    bsz = c.shape[0]
    tn = MOD_TILE
    return pl.pallas_call(
        _mod_kernel,
        grid=(depth, n // tn),
        in_specs=[
            pl.BlockSpec((bsz, d), lambda l, j: (0, 0)),
            pl.BlockSpec((1, d, tn), lambda l, j: (l, 0, j)),
            pl.BlockSpec((1, 1, tn), lambda l, j: (l, 0, j)),
        ],
        out_specs=pl.BlockSpec((1, bsz, tn), lambda l, j: (l, 0, j)),
        out_shape=jax.ShapeDtypeStruct((depth, bsz, n), F32),
        compiler_params=_cparams(("arbitrary", "arbitrary")),
        name="adaln_mod",
    )(c, w_ada, b_ada.reshape(depth, 1, n))


def _perm_runs(perm):
    runs = []
    start = 0
    for i in range(1, len(perm) + 1):
        run_ends = (i == len(perm) or ((perm[i] < 0) != (perm[i - 1] < 0))
                    or (perm[i] >= 0 and perm[i] != perm[i - 1] + 1))
        if run_ends:
            runs.append((int(perm[start]) if perm[start] >= 0 else -1, i - start, start))
            start = i
    return runs


_IN_RUNS = _perm_runs(_IN_PERM)


def _pool_tile(ext, u, pos0, w_bd, scale):
    row = lax.broadcasted_iota(jnp.int32, ext.shape, 0)
    lane = lax.broadcasted_iota(jnp.int32, u.shape, 1)

    def shifted(a, s):
        return jnp.where(row >= s, pltpu.roll(a, s, axis=0), 0.0)

    sums = []
    acc = ext
    for wdt in POOL_WINDOWS:
        acc = acc + shifted(acc, wdt // 2)
        sums.append(acc[POOL_HALO:])
    grp = lane // POOL_GROUP
    wsum = sums[-1]
    win = jnp.full(u.shape, POOL_WINDOWS[-1], jnp.int32)
    for gi in range(len(POOL_WINDOWS) - 2, -1, -1):
        wsum = jnp.where(grp == gi, sums[gi], wsum)
        win = jnp.where(grp == gi, POOL_WINDOWS[gi], win)
    pos = pos0 + lax.broadcasted_iota(jnp.int32, u.shape, 0)
    cnt = jnp.minimum(pos + 1, win).astype(F32)
    p = wsum / cnt - u
    return (_dot(p.astype(BF16), w_bd) * scale).astype(BF16)


def _inproj_kernel(layer, seq, x_ref, mod_ref, wt_hbm, pw_ref, ps_ref, pool_ref, q_ref, kv_ref, gdn_ref, ba_ref,
                   wt_f32, tail_ref, wt_ref, halo_ref, sem):
    @pl.when(pl.program_id(0) == 0)
    def _():
        n_real = wt_hbm.shape[0]
        pad0 = (n_real // SUBLANES) * SUBLANES
        bulk = pltpu.make_async_copy(wt_hbm.at[pl.ds(0, pad0), layer, :], wt_f32.at[pl.ds(0, pad0)], sem.at[0])
        tail = pltpu.make_async_copy(wt_hbm.at[pl.ds(n_real - SUBLANES, SUBLANES), layer, :], tail_ref, sem.at[1])
        bulk.start()
        tail.start()
        bulk.wait()
        tail.wait()
        row8 = lax.broadcasted_iota(jnp.int32, tail_ref.shape, 0)
        left = n_real - pad0
        wt_f32[pad0:pad0 + SUBLANES, :] = jnp.where(row8 < left, pltpu.roll(tail_ref[...], left, axis=0), 0.0)
        wt_f32[pad0 + SUBLANES:, :] = jnp.zeros((wt_f32.shape[0] - pad0 - SUBLANES, wt_f32.shape[1]), F32)
        for src, n, dst in _IN_RUNS:
            if src >= 0:
                rows = n if src + n < n_real else wt_ref.shape[0] - dst
                wt_ref[dst:dst + rows, :] = wt_f32[src:src + rows, :].astype(BF16)
        q_rows = wt_ref[P_Q[0]:P_Q[1], :].astype(F32) * (HEAD_DIM ** -0.5)
        wt_ref[P_Q[0]:P_Q[1], :] = q_rows.astype(BF16)

    sh = mod_ref[0, 0:1, :]
    sc = mod_ref[0, 1:2, :]
    h = (x_ref[...] * (1.0 + sc) + sh).astype(BF16)

    def mm(rng):
        return _dot_nt(h, wt_ref[rng[0]:rng[1], :])

    tm = x_ref.shape[0]
    pos0 = (pl.program_id(0) * tm) % seq
    u = mm(P_POOL)
    ext = jnp.concatenate([jnp.where(pos0 == 0, 0.0, halo_ref[...]), u], axis=0)
    halo_ref[...] = u[tm - POOL_HALO:, :]
    gdn_ref[...] = mm(P_GDN)
    pool_ref[...] = _pool_tile(ext, u, pos0, pw_ref[...], ps_ref[...])
    q_ref[...] = mm(P_Q).astype(BF16)
    kv_ref[...] = mm(P_KV).astype(BF16)
    ba_ref[...] = mm(P_BA)


def _in_projection(x2d, mod, w_in, pool_w_bd, pool_scale, layer, seq):
    t, d = x2d.shape
    tm = min(IN_TILE, seq)
    assert seq % tm == 0 and tm > POOL_HALO
    widths = [r[1] - r[0] for r in (P_POOL, P_Q, P_KV, P_GDN, P_BA)]
    dtypes = [BF16, BF16, BF16, F32, F32]
    fixed = lambda i: (0, 0)
    return pl.pallas_call(
        functools.partial(_inproj_kernel, layer, seq),
        grid=(t // tm,),
        in_specs=[
            pl.BlockSpec((tm, d), lambda i: (i, 0)),
            pl.BlockSpec((1, 6, d), lambda i: ((i * tm) // seq, 0, 0)),
            pl.BlockSpec(memory_space=pl.ANY),
            pl.BlockSpec((POOL_DIM, POOL_DIM), fixed),
            pl.BlockSpec((1, POOL_DIM), fixed),
        ],
        out_specs=[pl.BlockSpec((tm, w), lambda i: (i, 0)) for w in widths],
        out_shape=[jax.ShapeDtypeStruct((t, w), dt) for w, dt in zip(widths, dtypes)],
        scratch_shapes=[pltpu.VMEM((P_TOTAL, d), F32), pltpu.VMEM((SUBLANES, d), F32),
                        pltpu.VMEM((P_TOTAL, d), BF16), pltpu.VMEM((POOL_HALO, POOL_DIM), F32),
                        pltpu.SemaphoreType.DMA((2,))],
        compiler_params=_cparams(("arbitrary",)),
        name="in_proj_pool",
    )(x2d, mod, jnp.transpose(w_in, (2, 0, 1)), pool_w_bd, pool_scale.reshape(1, POOL_DIM))


def _attn_kernel(sink_ref, q_ref, kvc_ref, kvp_ref, bias_ref, o_ref):
    step = pl.program_id(1)
    qi = lax.broadcasted_iota(jnp.int32, (WINDOW, 2 * WINDOW), 0)
    kj = lax.broadcasted_iota(jnp.int32, (WINDOW, 2 * WINDOW), 1)
    dist = qi + WINDOW - kj
    in_band = (dist >= 0) & (dist < WINDOW)
    lo = lax.broadcasted_iota(jnp.int32, (WINDOW, LANES), 1) < HEAD_DIM
    heads = [(p, half) for p in range(N_ATT_HEADS // 2) for half in range(2)]
    sinks = [sink_ref[p + 3 * half] for p, half in heads]
    for sub in range(ATT_BLOCKS):
        r0 = sub * WINDOW
        prev = kvp_ref[...] if sub == 0 else kvc_ref[r0 - WINDOW:r0, :]
        kv = jnp.concatenate([prev, kvc_ref[r0:r0 + WINDOW, :]], axis=0)
        k = kv[:, :KV_DIM]
        v = kv[:, KV_DIM:]
        valid = in_band & ((kj >= WINDOW) | (step > 0)) if sub == 0 else in_band
        for g0 in range(0, len(heads), ATT_GROUP):
            group = heads[g0:g0 + ATT_GROUP]
            sk_g = sinks[g0:g0 + ATT_GROUP]
            scores = []
            for p, half in group:
                qp = q_ref[r0:r0 + WINDOW, p * LANES:(p + 1) * LANES]
                qm = jnp.where(lo if half == 0 else jnp.logical_not(lo), qp, jnp.zeros_like(qp))
                scores.append(jnp.where(valid, _dot_nt(qm, k) + bias_ref[p + 3 * half], NEG_INF))
            tops = [jnp.maximum(jnp.max(s, axis=-1, keepdims=True), sk) for s, sk in zip(scores, sk_g)]
            probs = [jnp.exp(s - m) for s, m in zip(scores, tops)]
            dens = [jnp.sum(pr, axis=-1, keepdims=True) + jnp.exp(sk - m)
                    for pr, sk, m in zip(probs, sk_g, tops)]
            outs = [_dot(pr.astype(BF16), v) / den for pr, den in zip(probs, dens)]
            for idx in range(0, len(group), 2):
                p = group[idx][0]
                o_ref[r0:r0 + WINDOW, p * LANES:(p + 1) * LANES] = (
                    jnp.where(lo, outs[idx], outs[idx + 1]).astype(BF16))


def _swa_attention(q, kv, bias, sinks, seq):
    t = q.shape[0]
    rows = ATT_BLOCKS * WINDOW
    nblk = seq // rows
    return pl.pallas_call(
        _attn_kernel,
        grid=(t // seq, nblk),
        in_specs=[
            pl.BlockSpec(memory_space=pltpu.SMEM),
            pl.BlockSpec((rows, ATT_DIM), lambda b, n: (b * nblk + n, 0)),
            pl.BlockSpec((rows, 2 * KV_DIM), lambda b, n: (b * nblk + n, 0)),
            pl.BlockSpec((WINDOW, 2 * KV_DIM),
                         lambda b, n: (jnp.maximum((b * nblk + n) * ATT_BLOCKS - 1, 0), 0)),
            pl.BlockSpec((N_ATT_HEADS, WINDOW, 2 * WINDOW), lambda b, n: (0, 0, 0)),
        ],
        out_specs=pl.BlockSpec((rows, ATT_DIM), lambda b, n: (b * nblk + n, 0)),
        out_shape=jax.ShapeDtypeStruct((t, ATT_DIM), BF16),
        compiler_params=_cparams(("arbitrary", "arbitrary")),
        name="swa_attention",
    )(sinks, q, kv, kv, bias)


_GDN_BASE = SUBLANES
_GDN_LEVELS = int(np.log2(GDN_CHUNK // _GDN_BASE))


def _gdn_masks():
    r = np.arange(GDN_SUPER)
    ri, ci = r[:, None], r[None, :]
    same_chunk = (ri // GDN_CHUNK) == (ci // GDN_CHUNK)
    incl = same_chunk & (ri >= ci)
    planes = [incl, ri == ci]
    base = ((ri // _GDN_BASE) == (ci // _GDN_BASE)) & (ri > ci)
    planes.append(base)
    for lvl in range(_GDN_LEVELS):
        small = _GDN_BASE << lvl
        planes.append(((ri // (2 * small)) == (ci // (2 * small))) & ((ri // small) != (ci // small)) & (ri > ci))
    bmask = np.stack(planes).astype(np.float32)
    bmask[2] = -bmask[2]
    negmask = np.where(incl, 0.0, -np.inf).astype(np.float32)
    return negmask, bmask


_GDN_NEGMASK, _GDN_BMASK = _gdn_masks()


def _gdn_kernel(x_ref, halo_ref, ba_ref, cw_ref, alog_ref, dtb_ref, nw_ref, negmask_ref, bmask_ref,
                y_ref, state_ref, xs_ref):
    sc_id = pl.program_id(1)
    rows = GDN_SUPER
    nchunk = rows // GDN_CHUNK
    c_sz = GDN_CHUNK

    @pl.when(sc_id == 0)
    def _():
        state_ref[...] = jnp.zeros_like(state_ref)

    xs_ref[:SUBLANES, :] = jnp.where(sc_id == 0, 0.0, halo_ref[...])
    xs_ref[SUBLANES:, :] = x_ref[...]
    act = []
    for g in range(x_ref.shape[1] // LANES):
        cols = slice(g * LANES, (g + 1) * LANES)
        acc = x_ref[:, cols] * cw_ref[CONV_WIDTH - 1:CONV_WIDTH, cols]
        if g % 4 != 3:
            for s in range(1, CONV_WIDTH):
                acc = acc + (xs_ref[SUBLANES - s:SUBLANES - s + rows, cols]
                             * cw_ref[CONV_WIDTH - 1 - s:CONV_WIDTH - s, cols])
        act.append(acc * _sigmoid(acc))

    negmask = negmask_ref[...]
    tri_incl = bmask_ref[0]
    eye_b = bmask_ref[1]
    base_neg = bmask_ref[2]
    bands = [bmask_ref[3 + lvl] for lvl in range(_GDN_LEVELS)]
    li = lax.broadcasted_iota(jnp.int32, (LANES, LANES), 0)
    lj = lax.broadcasted_iota(jnp.int32, (LANES, LANES), 1)
    half_ones = jnp.where((li // HEAD_DIM) == (lj // HEAD_DIM), 1.0, 0.0).astype(BF16)
    lane_lo = lax.broadcasted_iota(jnp.int32, (rows, LANES), 1) < HEAD_DIM

    ba = ba_ref[...]
    beta_all = _sigmoid(ba)
    sp_in = ba + dtb_ref[...]
    softplus = jnp.maximum(sp_in, 0.0) + jnp.log(1.0 + jnp.exp(-jnp.abs(sp_in)))
    g_all = -jnp.exp(alog_ref[...]) * softplus
    gcum = _dot_hi_exact_rhs_lhs(tri_incl, g_all)
    gcum_t = gcum.T

    heads = range(N_GDN_HEADS)
    lane_hi = jnp.logical_not(lane_lo)
    mk = [lane_lo if h % 2 == 0 else lane_hi for h in heads]
    scale = HEAD_DIM ** -0.5

    xk, xq, gn, gc_col, beta, eg = [], [], [], [], [], []
    for h in heads:
        g = act[4 * (h // 2) + (h % 2)]
        g = g * lax.rsqrt(_dot((g * g).astype(BF16), half_ones) + NORM_EPS)
        gn.append(g)
        xk.append(jnp.where(mk[h], g, 0.0))
        xq.append(jnp.where(mk[h], pltpu.roll(g, HEAD_DIM, axis=1), 0.0) * scale)
        beta.append(beta_all[:, h:h + 1])
        gc_col.append(gcum[:, N_GDN_HEADS + h:N_GDN_HEADS + h + 1])
        eg.append(jnp.exp(gc_col[h]))

    l_b, attn, rhs = [], [], []
    for h in heads:
        gc_row = gcum_t[N_GDN_HEADS + h:N_GDN_HEADS + h + 1, :]
        decay = jnp.exp(gc_col[h] - gc_row + negmask)
        xk_b = xk[h].astype(BF16)
        kk = _dot_nt((xk[h] * beta[h]).astype(BF16), xk_b)
        l_b.append((kk * decay).astype(BF16))
        attn.append((_dot_nt(xq[h].astype(BF16), xk_b) * decay).astype(BF16))
        vv = act[4 * (h // 2) + 2]
        rhs.append(jnp.where(mk[h], gn[h] * eg[h], vv) * beta[h])

    a1 = [l_b[h] * base_neg for h in heads]
    a2 = [_dot(a1[h], a1[h]).astype(BF16) for h in heads]
    a4 = [_dot(a2[h], a2[h]).astype(BF16) for h in heads]
    inv0 = [eye_b + a1[h] for h in heads]
    acc1 = [inv0[h].astype(F32) + _dot(a2[h], inv0[h]) for h in heads]
    inv_b = [(acc1[h] + _dot(a4[h], acc1[h].astype(BF16))).astype(BF16) for h in heads]
    for lvl in range(_GDN_LEVELS - 1):
        mid = [_dot(l_b[h] * bands[lvl], inv_b[h]).astype(BF16) for h in heads]
        inv_b = [inv_b[h] - _dot(inv_b[h], mid[h]).astype(BF16) for h in heads]
    half = [_dot(inv_b[h], rhs[h].astype(BF16)) for h in heads]
    mid = [_dot(l_b[h] * bands[_GDN_LEVELS - 1], half[h].astype(BF16)) for h in heads]
    sol = [half[h] - _dot(inv_b[h], mid[h].astype(BF16)) for h in heads]

    lane_lo_s = lax.broadcasted_iota(jnp.int32, (LANES, LANES), 1) < HEAD_DIM
    mk_s = [lane_lo_s if h % 2 == 0 else jnp.logical_not(lane_lo_s) for h in heads]
    sol_b = [sol[h].astype(BF16) for h in heads]
    attn_sol = [_dot(attn[h], sol_b[h]) for h in heads]
    q_eff = [(xq[h] * eg[h] - jnp.where(mk[h], attn_sol[h], 0.0)).astype(BF16) for h in heads]
    o_free = [jnp.where(mk[h], 0.0, attn_sol[h]) for h in heads]
    kw = [[] for _ in heads]
    ku = [[] for _ in heads]
    cdec = [[] for _ in heads]
    for c in range(nchunk):
        r0 = c * c_sz
        for h in heads:
            glast = gcum[r0 + c_sz - 1:r0 + c_sz, N_GDN_HEADS + h:N_GDN_HEADS + h + 1]
            kd_t = (xk[h][r0:r0 + c_sz] * jnp.exp(glast - gc_col[h][r0:r0 + c_sz])).T
            both = _dot(kd_t.astype(BF16), sol_b[h][r0:r0 + c_sz])
            kw[h].append(jnp.where(mk_s[h], both, 0.0).astype(BF16))
            ku[h].append(jnp.where(mk_s[h], 0.0, both))
            cdec[h].append(jnp.exp(glast))
    st = [state_ref[h] for h in heads]
    o_parts = [[] for _ in heads]
    for c in range(nchunk):
        r0 = c * c_sz
        for h in heads:
            lhs = jnp.concatenate([kw[h][c], q_eff[h][r0:r0 + c_sz]], axis=0)
            prod = _dot(lhs, st[h].astype(BF16))
            o_parts[h].append(prod[LANES:] + o_free[h][r0:r0 + c_sz])
            st[h] = st[h] * cdec[h][c] + ku[h][c] - prod[:LANES]
    for h in heads:
        state_ref[h] = st[h]

    for p in range(N_GDN_HEADS // 2):
        o_pair = [jnp.concatenate(o_parts[h], axis=0) for h in (2 * p, 2 * p + 1)]
        o = jnp.where(lane_lo, o_pair[1], o_pair[0])
        ms = _dot((o * o).astype(BF16), half_ones) * (1.0 / HEAD_DIM)
        zz = act[4 * p + 3]
        y = o * lax.rsqrt(ms + NORM_EPS) * nw_ref[...] * zz
        y_ref[:, p * LANES:(p + 1) * LANES] = y.astype(BF16)


def _dot_hi_exact_rhs_lhs(m_bf16, x):
    hi, lo = _split_bf16(x)
    return _dot(m_bf16, hi) + _dot(m_bf16, lo)


def _gdn_mixer(gdn, ba, conv_p, alog_v, dtb_v, nw_v, seq):
    t, c = gdn.shape
    rows = GDN_SUPER
    nsc = seq // rows
    hb = rows // SUBLANES
    return pl.pallas_call(
        _gdn_kernel,
        grid=(t // seq, nsc),
        in_specs=[
            pl.BlockSpec((rows, c), lambda b, s: (b * nsc + s, 0)),
            pl.BlockSpec((SUBLANES, c), lambda b, s: (jnp.maximum((b * nsc + s) * hb - 1, 0), 0)),
            pl.BlockSpec((rows, LANES), lambda b, s: (b * nsc + s, 0)),
            pl.BlockSpec((CONV_WIDTH, c), lambda b, s: (0, 0)),
            pl.BlockSpec((1, LANES), lambda b, s: (0, 0)),
            pl.BlockSpec((1, LANES), lambda b, s: (0, 0)),
            pl.BlockSpec((1, LANES), lambda b, s: (0, 0)),
            pl.BlockSpec((rows, rows), lambda b, s: (0, 0)),
            pl.BlockSpec((3 + _GDN_LEVELS, rows, rows), lambda b, s: (0, 0, 0)),
        ],
        out_specs=pl.BlockSpec((rows, GDN_DIM), lambda b, s: (b * nsc + s, 0)),
        out_shape=jax.ShapeDtypeStruct((t, GDN_DIM), BF16),
        scratch_shapes=[pltpu.VMEM((N_GDN_HEADS, LANES, LANES), F32), pltpu.VMEM((rows + SUBLANES, c), F32)],
        compiler_params=_cparams(("arbitrary", "arbitrary")),
        name="gdn_mixer",
    )(gdn, gdn, ba, conv_p, alog_v, dtb_v, nw_v, jnp.asarray(_GDN_NEGMASK), jnp.asarray(_GDN_BMASK, BF16))


def _route_tile(logits, before, carry_ref, live):
    shape = logits.shape
    lane = lax.broadcasted_iota(jnp.int32, shape, 1).astype(F32)
    work = logits
    vals, idxs = [], []
    for _k in range(TOP_K):
        m = jnp.max(work, axis=-1, keepdims=True)
        idx = jnp.min(jnp.where(work == m, lane, float(LANES)), axis=-1, keepdims=True)
        vals.append(m)
        idxs.append(idx)
        work = jnp.where(lane == idx, -jnp.inf, work)
    exps = [jnp.exp(v - vals[0]) for v in vals]
    den = exps[0] + exps[1] + exps[2] + exps[3]
    onehots = [lane == idx for idx in idxs]
    member = jnp.zeros(shape, F32)
    for oh in onehots:
        member = member + jnp.where(oh, 1.0, 0.0)
    rank = _dot(before, member.astype(BF16)) + carry_ref[...]
    carry_ref[...] = carry_ref[...] + live * jnp.sum(member, axis=0, keepdims=True)
    info = jnp.zeros(shape, F32)
    for k in range(TOP_K):
        rank_k = jnp.sum(jnp.where(onehots[k], rank, 0.0), axis=-1, keepdims=True)
        info = jnp.where(lane == float(k), idxs[k], info)
        info = jnp.where(lane == float(TOP_K + k), rank_k, info)
        info = jnp.where(lane == float(2 * TOP_K + k), exps[k] / den, info)
    return info


def _outproj_kernel(x_ref, mod_ref, yp_ref, ya_ref, yg_ref, wp_ref, wa_ref, wg_ref, lng_ref, lnb_ref,
                    rw_ref, rb_ref, before_ref, x1_ref, h2_ref, info_ref, er_ref, cnt_ref,
                    carry_ref, logit_s):
    step = pl.program_id(0)

    @pl.when(step == 0)
    def _():
        carry_ref[...] = jnp.zeros_like(carry_ref)
        logit_s[...] = jnp.zeros_like(logit_s)

    part = x_ref.shape[0] // OUT_SPLIT
    halves = [slice(j * part, (j + 1) * part) for j in range(OUT_SPLIT)]
    y = [_dot(yp_ref[r, :], wp_ref[...]) + _dot(ya_ref[r, :], wa_ref[...]) + _dot(yg_ref[r, :], wg_ref[...])
         for r in halves]
    live = jnp.where(step > 0, 1.0, 0.0)
    info = _route_tile(logit_s[...], before_ref[...], carry_ref, live)
    info_ref[...] = info
    er_ref[...] = info.T[:SUBLANES]
    cnt_ref[...] = carry_ref[...]
    g1 = mod_ref[0, 2:3, :]
    sh2 = mod_ref[0, 3:4, :]
    sc2 = mod_ref[0, 4:5, :]
    for r, y_r in zip(halves, y):
        x1 = _layer_norm(DEEPNORM_ALPHA * x_ref[r, :] + g1 * y_r, lng_ref[...], lnb_ref[...])
        x1_ref[r, :] = x1
        h2 = x1 * (1.0 + sc2) + sh2
        h2_ref[r, :] = _pack_bf16_pairs(h2)
        logit_s[r, :] = _dot(h2.astype(BF16), rw_ref[...]) + rb_ref[...]


def _out_projection(x2d, mod, yp, ya, yg, w_out_p, ln_g, ln_b, rw, rb, seq):
    t, d = x2d.shape
    tm = ROW_TILE
    wp = w_out_p[:POOL_DIM]
    wa = w_out_p[POOL_DIM:POOL_DIM + ATT_DIM]
    wg = w_out_p[POOL_DIM + ATT_DIM:]
    last = t // tm - 1
    row = lambda i: (jnp.minimum(i, last), 0)
    routed = lambda i: (jnp.maximum(i - 1, 0), 0)
    fixed = lambda i: (0, 0)
    return pl.pallas_call(
        _outproj_kernel,
        grid=(t // tm + 1,),
        in_specs=[
            pl.BlockSpec((tm, d), row),
            pl.BlockSpec((1, 6, d), lambda i: ((jnp.minimum(i, last) * tm) // seq, 0, 0)),
            pl.BlockSpec((tm, POOL_DIM), row),
            pl.BlockSpec((tm, ATT_DIM), row),
            pl.BlockSpec((tm, GDN_DIM), row),
            pl.BlockSpec((POOL_DIM, d), fixed),
            pl.BlockSpec((ATT_DIM, d), fixed),
            pl.BlockSpec((GDN_DIM, d), fixed),
            pl.BlockSpec((1, d), fixed),
            pl.BlockSpec((1, d), fixed),
            pl.BlockSpec((d, LANES), fixed),
            pl.BlockSpec((1, LANES), fixed),
            pl.BlockSpec((tm, tm), fixed),
        ],
        out_specs=[pl.BlockSpec((tm, d), row), pl.BlockSpec((tm, d // 2), row), pl.BlockSpec((tm, LANES), routed),
                   pl.BlockSpec((SUBLANES, tm), lambda i: (0, jnp.maximum(i - 1, 0))),
                   pl.BlockSpec((1, LANES), fixed)],
        out_shape=[jax.ShapeDtypeStruct((t, d), F32), jax.ShapeDtypeStruct((t, d // 2), jnp.int32),
                   jax.ShapeDtypeStruct((t, LANES), F32), jax.ShapeDtypeStruct((SUBLANES, t), F32),
                   jax.ShapeDtypeStruct((1, LANES), F32)],
        scratch_shapes=[pltpu.VMEM((1, LANES), F32), pltpu.VMEM((tm, LANES), F32)],
        compiler_params=_cparams(("arbitrary",)),
        name="out_proj_ln_route",
    )(x2d, mod, yp, ya, yg, wp, wa, wg, ln_g.reshape(1, d), ln_b.reshape(1, d), rw, rb,
      jnp.tril(jnp.ones((tm, tm), BF16), -1))


def _slot_kernel(er_ref, cnt_ref, dest_ref, pcum_ref):
    cnt = jnp.broadcast_to(cnt_ref[...], (SUBLANES, LANES))
    padded = jnp.floor((cnt + float(EXPERT_BLOCK - 1)) * (1.0 / EXPERT_BLOCK)) * float(EXPERT_BLOCK)
    lane8 = lax.broadcasted_iota(jnp.int32, (SUBLANES, LANES), 1)
    acc = padded
    step = 1
    while step < LANES:
        acc = acc + jnp.where(lane8 >= step, pltpu.roll(acc, step, axis=1), 0.0)
        step *= 2
    pcum_ref[...] = acc[:1].astype(jnp.int32)
    pstart = acc - padded

    er = er_ref[...]
    start = jnp.zeros(er.shape, F32)
    for e in range(N_EXPERTS):
        offset = jnp.sum(jnp.where(lane8 == e, pstart, 0.0), axis=-1, keepdims=True)
        start = jnp.where(er == float(e), offset, start)
    row = lax.broadcasted_iota(jnp.int32, er.shape, 0)
    slots = jnp.where(row < TOP_K, start + pltpu.roll(er, TOP_K, axis=0), 0.0)
    dest_ref[...] = slots.astype(jnp.int32)


def _slots(er, cnt):
    t = er.shape[1]
    return pl.pallas_call(
        _slot_kernel,
        grid=(1,),
        in_specs=[pl.BlockSpec((SUBLANES, t), lambda i: (0, 0)), pl.BlockSpec((1, LANES), lambda i: (0, 0))],
        out_specs=[pl.BlockSpec((SUBLANES, t), lambda i: (0, 0)), pl.BlockSpec((1, LANES), lambda i: (0, 0))],
        out_shape=[jax.ShapeDtypeStruct((SUBLANES, t), jnp.int32), jax.ShapeDtypeStruct((1, LANES), jnp.int32)],
        compiler_params=_cparams(("arbitrary",)),
        name="moe_slots",
    )(er, cnt)


def _expert_kernel(e0, be_ref, nxt_ref, val_ref, nu_ref, x_ref, wup_hbm, bup_ref, wdn_hbm, bdn_ref, y_ref,
                   wup_st, wdn_st, wup_bf, wdn_bf, sems):
    i = pl.program_id(0)
    e = be_ref[i]
    prev = be_ref[jnp.maximum(i - 1, 0)]
    used = i < nu_ref[0]

    def weight_copies(expert):
        return (pltpu.make_async_copy(wup_hbm.at[e0 + expert], wup_st, sems.at[0]),
                pltpu.make_async_copy(wdn_hbm.at[e0 + expert], wdn_st, sems.at[1]))

    @pl.when(i == 0)
    def _():
        for prio, cp in enumerate(weight_copies(e)):
            cp.start(priority=prio)

    @pl.when(used & ((i == 0) | (e != prev)))
    def _():
        for cp in weight_copies(e):
            cp.wait()
        wup_bf[...] = wup_st[...].astype(BF16)
        wdn_bf[...] = wdn_st[...].astype(BF16)

        @pl.when(nxt_ref[i] >= 0)
        def _():
            for prio, cp in enumerate(weight_copies(nxt_ref[i])):
                cp.start(priority=prio)

    def ffn(rows):
        xb = _unpack_bf16_pairs(x_ref[:rows, :]).astype(BF16)
        hb = _dot(xb, wup_bf[...]) + bup_ref[0]
        x_glu = jnp.minimum(hb[:, :EXPERT_DIM], SWIGLU_LIMIT)
        x_lin = jnp.clip(hb[:, EXPERT_DIM:], -SWIGLU_LIMIT, SWIGLU_LIMIT)
        act = x_glu * _sigmoid(SWIGLU_ALPHA * x_glu) * (x_lin + 1.0)
        y = _dot(act.astype(BF16), wdn_bf[...]) + bdn_ref[0]
        y_ref[:rows, :] = _pack_bf16_pairs(y)

    bm = x_ref.shape[0]
    pieces = (val_ref[i] + (EXPERT_GRAIN - 1)) // EXPERT_GRAIN
    for n_piece in range(1, bm // EXPERT_GRAIN + 1):
        rows_used = n_piece * EXPERT_GRAIN

        @pl.when(used & (pieces == n_piece))
        def _(rows_used=rows_used):
            ffn(rows_used)
            if rows_used < bm:
                y_ref[rows_used:, :] = jnp.zeros((bm - rows_used, y_ref.shape[1]), y_ref.dtype)

    @pl.when(i >= nu_ref[0])
    def _():
        y_ref[...] = jnp.zeros_like(y_ref)


def _expert_ffn(xbuf, block_e, next_e, valid, n_used, w_up, b_up, w_down, b_down, layer):
    p, dh = xbuf.shape
    d = 2 * dh
    bm = EXPERT_BLOCK
    ne, _, n_up = w_up.shape
    e0 = layer * N_EXPERTS
    grid_spec = pltpu.PrefetchScalarGridSpec(
        num_scalar_prefetch=4,
        grid=(p // bm,),
        in_specs=[
            pl.BlockSpec((bm, dh), lambda i, be, nx, vl, nu: (jnp.minimum(i, nu[0] - 1), 0)),
            pl.BlockSpec(memory_space=pl.ANY),
            pl.BlockSpec((1, 1, n_up), lambda i, be, nx, vl, nu: (e0 + be[i], 0, 0)),
            pl.BlockSpec(memory_space=pl.ANY),
            pl.BlockSpec((1, 1, d), lambda i, be, nx, vl, nu: (e0 + be[i], 0, 0)),
        ],
        out_specs=pl.BlockSpec((bm, dh), lambda i, be, nx, vl, nu: (i, 0)),
        scratch_shapes=[pltpu.VMEM((d, n_up), F32), pltpu.VMEM((EXPERT_DIM, d), F32),
                        pltpu.VMEM((d, n_up), BF16), pltpu.VMEM((EXPERT_DIM, d), BF16),
                        pltpu.SemaphoreType.DMA((2,))],
    )
    return pl.pallas_call(
        functools.partial(_expert_kernel, e0),
        grid_spec=grid_spec,
        out_shape=jax.ShapeDtypeStruct((p, dh), jnp.int32),
        compiler_params=_cparams(("arbitrary",)),
        name="expert_ffn",
    )(block_e, next_e, valid, n_used, xbuf, w_up, b_up, w_down, b_down)


def _combine_kernel(x1_ref, mod_ref, yg_ref, info_ref, lng_ref, lnb_ref, o_ref):
    info = info_ref[...]
    y = jnp.zeros(x1_ref.shape, F32)
    for k in range(TOP_K):
        gate = info[:, 2 * TOP_K + k:2 * TOP_K + k + 1]
        y = y + gate * _unpack_bf16_pairs(yg_ref[k])
    g2 = mod_ref[0, 5:6, :]
    o_ref[...] = _layer_norm(DEEPNORM_ALPHA * x1_ref[...] + g2 * y, lng_ref[...], lnb_ref[...])


def _combine(x1, mod, yg, info, ln_g, ln_b, seq):
    t, d = x1.shape
    tm = min(IN_TILE, seq)
    row = lambda i: (i, 0)
    fixed = lambda i: (0, 0)
    return pl.pallas_call(
        _combine_kernel,
        grid=(t // tm,),
        in_specs=[
            pl.BlockSpec((tm, d), row),
            pl.BlockSpec((1, 6, d), lambda i: ((i * tm) // seq, 0, 0)),
            pl.BlockSpec((TOP_K, tm, d // 2), lambda i: (0, i, 0)),
            pl.BlockSpec((tm, LANES), row),
            pl.BlockSpec((1, d), fixed),
            pl.BlockSpec((1, d), fixed),
        ],
        out_specs=pl.BlockSpec((tm, d), row),
        out_shape=jax.ShapeDtypeStruct((t, d), F32),
        compiler_params=_cparams(("arbitrary",)),
        name="moe_combine_ln",
    )(x1, mod, yg, info, ln_g.reshape(1, d), ln_b.reshape(1, d))


def _sc_workers():
    info = plsc.get_sparse_core_info()
    return info.num_cores, info.num_cores * info.num_subcores


def _sc_scatter_rows(rows, idx, n_out):
    t, w = rows.shape
    kk = idx.shape[0]
    n_cores, n_workers = _sc_workers()
    ch = SC_CHUNK
    assert t % (2 * n_workers * ch) == 0
    n_chunk = t // (n_workers * ch)
    idx_c = jnp.transpose(idx.reshape(kk, t // ch, ch), (1, 0, 2))

    @functools.partial(
        pl.kernel,
        mesh=plsc.VectorSubcoreMesh(core_axis_name="c", subcore_axis_name="s"),
        out_type=jax.ShapeDtypeStruct((n_out, w), rows.dtype),
        scratch_types=[pltpu.VMEM((2, kk, ch), jnp.int32), pltpu.VMEM((2, ch, w), rows.dtype),
                       pltpu.SemaphoreType.DMA((2,)), pltpu.SemaphoreType.DMA((2,))],
        name="sc_dispatch_scatter",
    )
    def scatter_kernel(rows_hbm, idx_hbm, out_hbm, idx_v, rows_v, load_sem, scat_sem):
        base = (lax.axis_index("s") * n_cores + lax.axis_index("c")) * n_chunk

        def load(j, b):
            return pltpu.make_async_copy(rows_hbm.at[pl.ds((base + j) * ch, ch)], rows_v.at[b], load_sem.at[b])

        def scatters(b):
            return [pltpu.make_async_copy(rows_v.at[b], out_hbm.at[idx_v.at[b, q]], scat_sem.at[b])
                    for q in range(kk)]

        pltpu.sync_copy(idx_hbm.at[base], idx_v.at[0])
        load(0, 0).start()

        @pl.loop(0, n_chunk, step=2)
        def _(j0):
            for b in range(2):
                j = j0 + b
                other = 1 - b

                @pl.when(j >= 1)
                def _():
                    for cp in scatters(other):
                        cp.wait()

                @pl.when(j + 1 < n_chunk)
                def _():
                    pltpu.sync_copy(idx_hbm.at[base + j + 1], idx_v.at[other])
                    load(j + 1, other).start()

                load(j, b).wait()
                for cp in scatters(b):
                    cp.start()

        for cp in scatters((n_chunk - 1) % 2):
            cp.wait()

    return scatter_kernel(rows, idx_c)


def _sc_gather_rows(table, idx):
    m = idx.shape[0]
    w = table.shape[1]
    n_cores, n_workers = _sc_workers()
    ch = SC_CHUNK
    assert m % (2 * n_workers * ch) == 0
    n_chunk = m // (n_workers * ch)
    idx_c = idx.reshape(m // ch, 1, ch)

    @functools.partial(
        pl.kernel,
        mesh=plsc.VectorSubcoreMesh(core_axis_name="c", subcore_axis_name="s"),
        out_type=jax.ShapeDtypeStruct((m, w), table.dtype),
        scratch_types=[pltpu.VMEM((2, 1, ch), jnp.int32), pltpu.VMEM((2, ch, w), table.dtype),
                       pltpu.SemaphoreType.DMA((2,)), pltpu.SemaphoreType.DMA((2,))],
        name="sc_combine_gather",
    )
    def gather_kernel(table_hbm, idx_hbm, out_hbm, idx_v, rows_v, gather_sem, write_sem):
        base = (lax.axis_index("s") * n_cores + lax.axis_index("c")) * n_chunk

        def gather(b):
            return pltpu.make_async_copy(table_hbm.at[idx_v.at[b, 0]], rows_v.at[b], gather_sem.at[b])

        def write(j, b):
            return pltpu.make_async_copy(rows_v.at[b], out_hbm.at[pl.ds((base + j) * ch, ch)], write_sem.at[b])

        pltpu.sync_copy(idx_hbm.at[base], idx_v.at[0])
        gather(0).start()

        @pl.loop(0, n_chunk, step=2)
        def _(j0):
            for b in range(2):
                j = j0 + b
                other = 1 - b

                @pl.when(j >= 1)
                def _():
                    write(j - 1, other).wait()

                @pl.when(j + 1 < n_chunk)
                def _():
                    pltpu.sync_copy(idx_hbm.at[base + j + 1], idx_v.at[other])
                    gather(other).start()

                gather(b).wait()
                write(j, b).start()

        write(n_chunk - 1, (n_chunk - 1) % 2).wait()

    return gather_kernel(table, idx_c)


def _lane_vector(vals, offset):
    return jnp.zeros((1, LANES), F32).at[0, offset:offset + vals.shape[0]].set(vals.astype(F32))


def _moe(h2, info, er, cnt, x1, mod, ln_g, ln_b, w_up, b_up, w_down, b_down, layer, seq):
    t, dh = h2.shape
    a = t * TOP_K
    bm = EXPERT_BLOCK
    slots, pcum_v = _slots(er, cnt)
    pcum = pcum_v[0, :N_EXPERTS]
    dest = slots[:TOP_K]
    n_blocks = -(-a // bm) + N_EXPERTS
    starts = jnp.arange(n_blocks, dtype=jnp.int32) * bm
    block_e = jnp.minimum(jnp.sum(pcum[None, :] <= starts[:, None], axis=1), N_EXPERTS - 1).astype(jnp.int32)
    n_used = (pcum[-1] // bm).astype(jnp.int32).reshape(1)
    later = block_e[None, :] > block_e[:, None]
    group_end = n_blocks - jnp.sum(later, axis=1)
    next_e = jnp.min(jnp.where(later, block_e[None, :], N_EXPERTS), axis=1)
    next_e = jnp.where(group_end < n_used[0], next_e, -1).astype(jnp.int32)
    counts = cnt[0, :N_EXPERTS].astype(jnp.int32)
    pstart = pcum - ((counts + bm - 1) // bm) * bm
    mine = block_e[:, None] == jnp.arange(N_EXPERTS, dtype=jnp.int32)[None, :]
    count_b = jnp.sum(jnp.where(mine, counts[None, :], 0), axis=1)
    pstart_b = jnp.sum(jnp.where(mine, pstart[None, :], 0), axis=1)
    valid = jnp.clip(count_b - (starts - pstart_b), 0, bm).astype(jnp.int32)
    xbuf = _sc_scatter_rows(h2, dest, n_blocks * bm)
    ybuf = _expert_ffn(xbuf, block_e, next_e, valid, n_used, w_up, b_up, w_down, b_down, layer)
    yg = _sc_gather_rows(ybuf, dest.reshape(a)).reshape(TOP_K, t, dh)
    return _combine(x1, mod, yg, info, ln_g, ln_b, seq)


def kernel(x, c, rel_bias, w_in, w_out, w_ada, b_ada, ln1_g, ln1_b, ln2_g, ln2_b, pool_w, pool_scale,
           attn_sinks, conv_w, gdn_a_log, gdn_dt_bias, gdn_norm_w, router_w, router_b,
           exp_w_up, exp_b_up, exp_w_down, exp_b_down):
    bsz, seq, d = x.shape
    depth = w_in.shape[0]
    t = bsz * seq
    assert d == D_MODEL and w_in.shape[2] == IN_DIM and depth == DEPTH
    assert seq % GDN_SUPER == 0 and seq % (ATT_BLOCKS * WINDOW) == 0
    assert t % ROW_TILE == 0 and seq % ROW_TILE == 0

    mod_all = _modulation(c, w_ada, b_ada).reshape(depth, bsz, 6, d)
    bias = _band_bias(rel_bias)

    w_up_all = exp_w_up.reshape((depth * N_EXPERTS,) + exp_w_up.shape[2:])
    b_up_all = exp_b_up.reshape(depth * N_EXPERTS, 1, exp_b_up.shape[2])
    w_down_all = exp_w_down.reshape((depth * N_EXPERTS,) + exp_w_down.shape[2:])
    b_down_all = exp_b_down.reshape(depth * N_EXPERTS, 1, exp_b_down.shape[2])

    x2d = x.reshape(t, d)
    for l in range(depth):
        mod = mod_all[l]
        w_out_p = _take_static(w_out[l], _OUT_PERM, 0).astype(BF16)
        ident = jnp.zeros((CONV_WIDTH, 1), F32).at[CONV_WIDTH - 1, 0].set(1.0)
        conv_p = jnp.where(jnp.asarray(_GDN_CONV_SRC >= 0), _take_cols(conv_w[l].astype(F32), _GDN_CONV_SRC),
                           ident)
        pool_bd = jnp.zeros((POOL_DIM, POOL_DIM), F32)
        for gi in range(len(POOL_WINDOWS)):
            sl = slice(gi * POOL_GROUP, (gi + 1) * POOL_GROUP)
            pool_bd = pool_bd.at[sl, sl].set(pool_w[l, gi].astype(F32))
        alog_v = _lane_vector(gdn_a_log[l], N_GDN_HEADS)
        dtb_v = _lane_vector(gdn_dt_bias[l], N_GDN_HEADS)
        nw_v = jnp.tile(gdn_norm_w[l].astype(F32), 2).reshape(1, LANES)
        rw = jnp.zeros((d, LANES), BF16).at[:, :N_EXPERTS].set(router_w[l].astype(BF16))
        rb = jnp.full((1, LANES), NEG_INF, F32).at[0, :N_EXPERTS].set(router_b[l].astype(F32))

        y_pool, aq, akv, gdn, ba = _in_projection(x2d, mod, w_in.astype(F32), pool_bd.astype(BF16),
                                                  pool_scale[l].astype(F32), l, seq)
        y_att = _swa_attention(aq, akv, bias, attn_sinks[l].astype(F32), seq)
        y_gdn = _gdn_mixer(gdn, ba, conv_p, alog_v, dtb_v, nw_v, seq)
        x1, h2, info, er, cnt = _out_projection(x2d, mod, y_pool, y_att, y_gdn, w_out_p, ln1_g[l], ln1_b[l],
                                            rw, rb, seq)
        x2d = _moe(h2, info, er, cnt, x1, mod, ln2_g[l], ln2_b[l], w_up_all, b_up_all, w_down_all, b_down_all,
                   l, seq)
    return x2d.reshape(bsz, seq, d)
```
